```python
import jax, jax.numpy as jnp
from jax import lax
import numpy as np

D_MODEL = 1024
BATCH = 32
SEQ = 256
DEPTH = 1
DEC_BATCH = 4
DEC_SEQ = 2048
PAST_LEN = 512

GRID_W = 64
EPS = 1e-6
GLA_HEADS = 4
GLA_DK = 64
GLA_DV = 128
GLA_GATE_RANK = 16
GLA_GATE_NORM = 16.0
GLA_CHUNK = 32
GLA_KEY_W = GLA_HEADS * GLA_DK
GLA_VAL_W = GLA_HEADS * GLA_DV
MLA_HEADS = 4
Q_LORA = 256
KV_LORA = 128
NOPE_DIM = 64
ROPE_DIM = 32
V_DIM = 128
QK_DIM = NOPE_DIM + ROPE_DIM
MLA_OUT_W = MLA_HEADS * V_DIM
ROPE_THETA = 10000.0
ATTN_BLOCK = 128
MIX_W = GLA_VAL_W + MLA_OUT_W
IN_WIDTHS = (GLA_KEY_W, GLA_KEY_W, GLA_VAL_W, GLA_VAL_W, GLA_GATE_RANK, GLA_GATE_RANK, Q_LORA, KV_LORA, ROPE_DIM)
D_IN = sum(IN_WIDTHS)
N_EXPERTS = 64
TOP_K = 8
N_GROUPS = 8
TOPK_GROUPS = 4
D_EXPERT = 256
ROUTED_SCALE = 2.5
MOE_BLOCK = 128

kernel_name = 'hybrid_gla_mla_moe_diffusion_step'


def rmsnorm(x, g):
    xf = x.astype(jnp.float32)
    xf = xf * lax.rsqrt(jnp.mean(xf * xf, axis=-1, keepdims=True) + EPS)
    return xf.astype(x.dtype) * g


def to_heads(x, n_heads):
    b, l, w = x.shape
    return x.reshape(b, l, n_heads, w // n_heads).transpose(0, 2, 1, 3)


def grid_positions(length):
    rows = length // GRID_W
    r = jnp.repeat(jnp.arange(rows, dtype=jnp.float32), GRID_W)
    col = jnp.tile(jnp.arange(GRID_W, dtype=jnp.float32), rows)
    return r, col


def rope1d(x, pos):
    half = x.shape[-1] // 2
    inv_freq = ROPE_THETA ** (-jnp.arange(half, dtype=jnp.float32) / half)
    ang = (pos[:, None] * inv_freq[None, :])[:, None, :]
    cos, sin = jnp.cos(ang).astype(x.dtype), jnp.sin(ang).astype(x.dtype)
    x1, x2 = x[..., :half], x[..., half:]
    return jnp.concatenate([x1 * cos - x2 * sin, x1 * sin + x2 * cos], axis=-1)


def axial_rope(x, rows, cols):
    nope, rot = x[..., :NOPE_DIM], x[..., NOPE_DIM:]
    a = ROPE_DIM // 2
    rot = jnp.concatenate([rope1d(rot[..., :a], rows), rope1d(rot[..., a:], cols)], axis=-1)
    return jnp.concatenate([nope, rot], axis=-1)


def gla_chunked(q, k, v, logg, s0):
    b_, h_, l_, dk = q.shape
    dv = v.shape[-1]
    n = l_ // GLA_CHUNK
    qc = q.reshape(b_, h_, n, GLA_CHUNK, dk)
    kc = k.reshape(b_, h_, n, GLA_CHUNK, dk)
    vc = v.reshape(b_, h_, n, GLA_CHUNK, dv)
    cum = jnp.cumsum(logg.astype(jnp.float32).reshape(b_, h_, n, GLA_CHUNK, dk), axis=3)
    causal = jnp.tril(jnp.ones((GLA_CHUNK, GLA_CHUNK), dtype=bool))[:, :, None]
    rel = cum[:, :, :, :, None, :] - cum[:, :, :, None, :, :]
    decay = jnp.exp(jnp.where(causal, rel, -jnp.inf))
    attn = jnp.einsum('bhnijd,bhnjd->bhnij', qc[:, :, :, :, None, :] * decay, kc)
    o_intra = jnp.einsum('bhnij,bhnjv->bhniv', attn, vc)
    cum_last = cum[:, :, :, -1:, :]
    u = jnp.einsum('bhncd,bhncv->bhndv', kc * jnp.exp(cum_last - cum), vc).astype(jnp.float32)
    chunk_decay = jnp.exp(cum_last[:, :, :, 0, :])

    def step(s, inp):
        dec, upd = inp
        return dec[..., None] * s + upd, s

    s_final, s_enter = lax.scan(step, s0.astype(jnp.float32),
                                (jnp.moveaxis(chunk_decay, 2, 0), jnp.moveaxis(u, 2, 0)))
    s_enter = jnp.moveaxis(s_enter, 0, 2)
    o_inter = jnp.einsum('bhncd,bhndv->bhncv', qc * jnp.exp(cum), s_enter)
    o = (o_intra + o_inter).reshape(b_, h_, l_, dv)
    return o.astype(v.dtype), s_final.astype(v.dtype)


def mla_keys_values(ckv_n, krope, lp):
    b_, l_, _ = ckv_n.shape
    kv = (ckv_n @ lp['w_ukv']).reshape(b_, l_, MLA_HEADS, NOPE_DIM + V_DIM)
    k_rope = jnp.broadcast_to(krope[:, :, None, :], (b_, l_, MLA_HEADS, ROPE_DIM))
    k = rmsnorm(jnp.concatenate([kv[..., :NOPE_DIM], k_rope], axis=-1), lp['g_qk_k'])
    return k, kv[..., NOPE_DIM:]


def block_attention(q, k, v):
    b_, l_, h_, dk = q.shape
    nb = l_ // ATTN_BLOCK
    qb = (q * dk ** -0.5).reshape(b_, nb, ATTN_BLOCK, h_, dk).transpose(1, 0, 3, 2, 4)

    def one_block(qblk):
        s = jnp.einsum('bhqd,bkhd->bhqk', qblk, k).astype(jnp.float32)
        p = jax.nn.softmax(s, axis=-1).astype(v.dtype)
        return jnp.einsum('bhqk,bkhd->bqhd', p, v)

    o = lax.map(one_block, qb)
    return o.transpose(1, 0, 2, 3, 4).reshape(b_, l_, h_, v.shape[-1])


def token_mixers(h, lp, s0_f, s0_b, ctx_ckv, ctx_krope):
    b_, l_, _ = h.shape
    split_idx = np.cumsum(IN_WIDTHS)[:-1].tolist()
    q_g, k_g, v_g, g_g, lr_f, lr_b, cq, ckv, krope = jnp.split(h @ lp['w_in'], split_idx, axis=-1)
    q_g = to_heads(q_g, GLA_HEADS) * GLA_DK ** -0.5
    k_g = to_heads(k_g, GLA_HEADS)
    v_g = to_heads(v_g, GLA_HEADS)
    logg_f = to_heads(jax.nn.log_sigmoid((lr_f @ lp['w_gk_fwd'] + lp['b_gk_fwd']).astype(jnp.float32)) / GLA_GATE_NORM, GLA_HEADS)
    logg_b = to_heads(jax.nn.log_sigmoid((lr_b @ lp['w_gk_bwd'] + lp['b_gk_bwd']).astype(jnp.float32)) / GLA_GATE_NORM, GLA_HEADS)
    o_f, s_f = gla_chunked(q_g, k_g, v_g, logg_f, s0_f)
    flip = lambda t: jnp.flip(t, axis=2)
    o_b, s_b = gla_chunked(flip(q_g), flip(k_g), flip(v_g), flip(logg_b), s0_b)
    o_g = (o_f + flip(o_b)).transpose(0, 2, 1, 3)
    o_g = rmsnorm(o_g, lp['g_gla_out']) * jax.nn.silu(g_g.reshape(b_, l_, GLA_HEADS, GLA_DV))
    ckv_n = rmsnorm(ckv, lp['g_kv_lora'])
    q_m = (rmsnorm(cq, lp['g_q_lora']) @ lp['w_uq']).reshape(b_, l_, MLA_HEADS, QK_DIM)
    q_m = rmsnorm(q_m, lp['g_qk_q'])
    k_m, v_m = mla_keys_values(ckv_n, krope, lp)
    if ctx_ckv is not None:
        rows, cols = grid_positions(l_)
        q_m = axial_rope(q_m, rows, cols)
        k_m = axial_rope(k_m, rows, cols)
        k_c, v_c = mla_keys_values(ctx_ckv, ctx_krope, lp)
        k_m = jnp.concatenate([k_m, k_c], axis=1)
        v_m = jnp.concatenate([v_m, v_c], axis=1)
    o_m = block_attention(q_m, k_m, v_m)
    mixed = jnp.concatenate([o_g.reshape(b_, l_, GLA_VAL_W), o_m.reshape(b_, l_, MLA_OUT_W)], axis=-1)
    return mixed @ lp['w_out'], s_f, s_b, ckv_n, krope


def moe(h, lp):
    b_, l_, d = h.shape
    t = h.reshape(-1, d)
    n_tok = t.shape[0]
    scores = jax.nn.sigmoid((t @ lp['w_router']).astype(jnp.float32))
    sel = scores + lp['b_router'].astype(jnp.float32)
    grp_score = lax.top_k(sel.reshape(n_tok, N_GROUPS, N_EXPERTS // N_GROUPS), 2)[0].sum(-1)
    _, top_g = lax.top_k(grp_score, TOPK_GROUPS)
    g_mask = jax.nn.one_hot(top_g, N_GROUPS, dtype=jnp.float32).sum(-2) > 0
    e_mask = jnp.repeat(g_mask, N_EXPERTS // N_GROUPS, axis=-1)
    _, top_e = lax.top_k(jnp.where(e_mask, sel, -jnp.inf), TOP_K)
    w = jnp.take_along_axis(scores, top_e, axis=-1)
    w = w / jnp.sum(w, axis=-1, keepdims=True) * ROUTED_SCALE
    gates = jnp.sum(jax.nn.one_hot(top_e, N_EXPERTS, dtype=jnp.float32) * w[..., None], axis=-2)
    nb = n_tok // MOE_BLOCK

    def block(args):
        tb, gb = args
        a = jax.nn.silu(jnp.einsum('td,edf->tef', tb, lp['w_exp_gate'])) * jnp.einsum('td,edf->tef', tb, lp['w_exp_up'])
        a = a * gb[:, :, None].astype(a.dtype)
        return jnp.einsum('tef,efd->td', a, lp['w_exp_down'])

    routed = lax.map(block, (t.reshape(nb, MOE_BLOCK, d), gates.reshape(nb, MOE_BLOCK, N_EXPERTS))).reshape(n_tok, d)
    shared = (jax.nn.silu(t @ lp['w_sh_gate']) * (t @ lp['w_sh_up'])) @ lp['w_sh_down']
    return (routed + shared).reshape(b_, l_, d)


def adaln_layer(x, mod, lp, s0_f, s0_b, ctx_ckv, ctx_krope):
    sh1, sc1, gt1, sh2, sc2, gt2 = jnp.split(mod, 6, axis=-1)
    h = rmsnorm(x, lp['g_norm1']) * (1 + sc1) + sh1
    mix, s_f, s_b, ckv_n, krope = token_mixers(h, lp, s0_f, s0_b, ctx_ckv, ctx_krope)
    x = x + gt1 * mix
    h = rmsnorm(x, lp['g_norm2']) * (1 + sc2) + sh2
    x = x + gt2 * moe(h, lp)
    return x, s_f, s_b, ckv_n, krope


def setup_inputs(seed: int = 0) -> dict:
    key = jax.random.key(seed)
    ks = iter(jax.random.split(key, 40))
    nrm = lambda shape, scale: jax.random.normal(next(ks), shape, jnp.float32) * scale
    gain = lambda shape: 1.0 + nrm(shape, 0.02)
    D, L = D_MODEL, DEPTH
    return {
        'x_prompt': nrm((BATCH, SEQ, D), 1.0),
        'x_sample': nrm((DEC_BATCH, DEC_SEQ, D), 1.0),
        'c': nrm((DEC_BATCH, D), 1.0),
        'state_gla_fwd': nrm((DEC_BATCH, L, GLA_HEADS, GLA_DK, GLA_DV), 0.5),
        'state_gla_bwd': nrm((DEC_BATCH, L, GLA_HEADS, GLA_DK, GLA_DV), 0.5),
        'cache_mla_ckv': nrm((DEC_BATCH, L, PAST_LEN, KV_LORA), 1.0),
        'cache_mla_krope': nrm((DEC_BATCH, L, PAST_LEN, ROPE_DIM), 1.0),
        'c_ctx': nrm((D,), 1.0),
        'w_ada': nrm((L, D, 6 * D), 0.5 * D ** -0.5),
        'b_ada': nrm((L, 6 * D), 0.02),
        'g_norm1': gain((L, D)),
        'g_norm2': gain((L, D)),
        'w_in': nrm((L, D, D_IN), D ** -0.5),
        'w_gk_fwd': nrm((L, GLA_GATE_RANK, GLA_KEY_W), GLA_GATE_RANK ** -0.5),
        'b_gk_fwd': nrm((L, GLA_KEY_W), 0.1),
        'w_gk_bwd': nrm((L, GLA_GATE_RANK, GLA_KEY_W), GLA_GATE_RANK ** -0.5),
        'b_gk_bwd': nrm((L, GLA_KEY_W), 0.1),
        'g_gla_out': gain((L, GLA_DV)),
        'g_q_lora': gain((L, Q_LORA)),
        'w_uq': nrm((L, Q_LORA, MLA_HEADS * QK_DIM), Q_LORA ** -0.5),
        'g_kv_lora': gain((L, KV_LORA)),
        'w_ukv': nrm((L, KV_LORA, MLA_HEADS * (NOPE_DIM + V_DIM)), KV_LORA ** -0.5),
        'g_qk_q': gain((L, QK_DIM)),
        'g_qk_k': gain((L, QK_DIM)),
        'w_out': nrm((L, MIX_W, D), MIX_W ** -0.5),
        'w_router': nrm((L, D, N_EXPERTS), D ** -0.5),
        'b_router': nrm((L, N_EXPERTS), 0.01),
        'w_exp_gate': nrm((L, N_EXPERTS, D, D_EXPERT), D ** -0.5),
        'w_exp_up': nrm((L, N_EXPERTS, D, D_EXPERT), D ** -0.5),
        'w_exp_down': nrm((L, N_EXPERTS, D_EXPERT, D), D_EXPERT ** -0.5),
        'w_sh_gate': nrm((L, D, D_EXPERT), D ** -0.5),
        'w_sh_up': nrm((L, D, D_EXPERT), D ** -0.5),
        'w_sh_down': nrm((L, D_EXPERT, D), D_EXPERT ** -0.5),
    }


def reference(x_prompt, x_sample, c, state_gla_fwd, state_gla_bwd, cache_mla_ckv, cache_mla_krope,
              c_ctx, w_ada, b_ada, g_norm1, g_norm2, w_in, w_gk_fwd, b_gk_fwd, w_gk_bwd, b_gk_bwd,
              g_gla_out, g_q_lora, w_uq, g_kv_lora, w_ukv, g_qk_q, g_qk_k, w_out,
              w_router, b_router, w_exp_gate, w_exp_up, w_exp_down, w_sh_gate, w_sh_up, w_sh_down):
    xp, xs = x_prompt, x_sample
    new_f, new_b, new_ckv, new_kr = [], [], [], []
    for l in range(DEPTH):
        lp = {'g_norm1': g_norm1[l], 'g_norm2': g_norm2[l], 'w_in': w_in[l],
              'w_gk_fwd': w_gk_fwd[l], 'b_gk_fwd': b_gk_fwd[l], 'w_gk_bwd': w_gk_bwd[l], 'b_gk_bwd': b_gk_bwd[l],
              'g_gla_out': g_gla_out[l], 'g_q_lora': g_q_lora[l], 'w_uq': w_uq[l],
              'g_kv_lora': g_kv_lora[l], 'w_ukv': w_ukv[l], 'g_qk_q': g_qk_q[l], 'g_qk_k': g_qk_k[l],
              'w_out': w_out[l], 'w_router': w_router[l], 'b_router': b_router[l],
              'w_exp_gate': w_exp_gate[l], 'w_exp_up': w_exp_up[l], 'w_exp_down': w_exp_down[l],
              'w_sh_gate': w_sh_gate[l], 'w_sh_up': w_sh_up[l], 'w_sh_down': w_sh_down[l]}
        mod_ctx = (jax.nn.silu(c_ctx) @ w_ada[l] + b_ada[l])[None, None, :]
        mod_lat = (jax.nn.silu(c) @ w_ada[l] + b_ada[l])[:, None, :]
        zeros = jnp.zeros((xp.shape[0], GLA_HEADS, GLA_DK, GLA_DV), jnp.float32)
        xp, s_f, s_b, ckv_n, kr = adaln_layer(xp, mod_ctx, lp, zeros, zeros, None, None)
        new_f.append(s_f)
        new_b.append(s_b)
        new_ckv.append(ckv_n)
        new_kr.append(kr)
        xs, _, _, _, _ = adaln_layer(xs, mod_lat, lp, state_gla_fwd[:, l], state_gla_bwd[:, l],
                                     cache_mla_ckv[:, l], cache_mla_krope[:, l])
    return (xp, xs, jnp.stack(new_f, axis=1), jnp.stack(new_b, axis=1),
            jnp.stack(new_ckv, axis=1), jnp.stack(new_kr, axis=1))
```

```python
import functools

import jax
import jax.numpy as jnp
from jax import lax
from jax.experimental import pallas as pl
from jax.experimental.pallas import tpu as pltpu

F32 = jnp.float32
BF16 = jnp.bfloat16

D_MODEL = 1024
EPS = 1e-6
GRID_W = 64
GLA_HEADS = 4
GLA_DK = 64
GLA_DV = 128
GLA_GATE_RANK = 16
GLA_GATE_NORM = 16.0
GLA_KEY_W = GLA_HEADS * GLA_DK
GLA_VAL_W = GLA_HEADS * GLA_DV
MLA_HEADS = 4
Q_LORA = 256
KV_LORA = 128
NOPE_DIM = 64
ROPE_DIM = 32
V_DIM = 128
QK_DIM = NOPE_DIM + ROPE_DIM
ROPE_THETA = 10000.0
N_EXPERTS = 64
TOP_K = 8
N_GROUPS = 8
TOPK_GROUPS = 4
D_EXPERT = 256
ROUTED_SCALE = 2.5

LANES = 128
HEAD_W = LANES
MLA_W = MLA_HEADS * HEAD_W
ROPE_LANE0 = NOPE_DIM
IN_W = 2048
TAIL0 = IN_W - LANES
GLA_BLOCK = 256
GLA_SUB = 64
VMEM_LIMIT = 56 * 1024 * 1024

HIGHEST = lax.Precision.HIGHEST


def _dot(a, b, precision=None):
    return jnp.dot(a, b, preferred_element_type=F32, precision=precision)


def _dot_nt(a, b, precision=None):
    return lax.dot_general(a, b, (((1,), (1,)), ((), ())), preferred_element_type=F32,
                           precision=precision)


def _rms(x, width):
    ss = jnp.sum(x * x, axis=-1, keepdims=True) * (1.0 / width)
    return x * lax.rsqrt(ss + EPS)


def _silu(x):
    return x * jax.nn.sigmoid(x)


def _log_sigmoid(x):
    return jnp.minimum(x, 0.0) - jnp.log1p(jnp.exp(-jnp.abs(x)))


def _ada_kernel(c_ref, w_ref, b_ref, o_ref):
    o_ref[...] = _dot(_silu(c_ref[...]), w_ref[...], precision=HIGHEST) + b_ref[...]


def _ada(cvecs, w_ada, b_ada):
    n = w_ada.shape[1]
    tn = 768
    return pl.pallas_call(
        _ada_kernel,
        grid=(n // tn,),
        in_specs=[pl.BlockSpec((8, D_MODEL), lambda j: (0, 0)),
                  pl.BlockSpec((D_MODEL, tn), lambda j: (0, j)),
                  pl.BlockSpec((1, tn), lambda j: (0, j))],
        out_specs=pl.BlockSpec((8, tn), lambda j: (0, j)),
        out_shape=jax.ShapeDtypeStruct((8, n), F32),
        name="ada",
    )(cvecs, w_ada, b_ada.reshape(1, n))


def _rope(x, c, s1, s2):
    return x * c + pltpu.roll(x, LANES - 8, 1) * s1 + pltpu.roll(x, 8, 1) * s2


def _mla_kv(ckv_n, kr, w_ukv_ref, gk, rope_tabs, k_ref, v_ref):
    kv = _dot(ckv_n.astype(BF16), w_ukv_ref[...])
    for h in range(MLA_HEADS):
        k_h = kv[:, h * HEAD_W:(h + 1) * HEAD_W] + kr
        k_h = _rms(k_h, QK_DIM) * gk
        if rope_tabs is not None:
            k_h = _rope(k_h, *rope_tabs)
        k_ref[0, :, h * HEAD_W:(h + 1) * HEAD_W] = k_h.astype(BF16)
    v_ref[0] = kv[:, MLA_W:].astype(BF16)


def _lane_mask(lo, hi, rows):
    lane = lax.broadcasted_iota(jnp.int32, (rows, LANES), 1)
    return (lane >= lo) & (lane < hi)


def _inproj_kernel(x_ref, mod_ref, g1_ref, win_ref, wgk_ref, bgk_ref, gql_ref, wuq_ref,
                   gkv_ref, wukv_ref, gq_ref, gk_ref, rc_ref, rs1_ref, rs2_ref,
                   q_ref, k_ref, v_ref, g_ref, lgf_ref, lgb_ref,
                   qm_ref, km_ref, vm_ref, ckvn_ref, kr_ref):
    x = x_ref[0]
    mod = mod_ref[0]
    h = _rms(x, D_MODEL) * g1_ref[...] * (1.0 + mod[1:2]) + mod[0:1]
    y = _dot(h.astype(BF16), win_ref[...])
    q_ref[0] = (y[:, 0:GLA_KEY_W] * GLA_DK ** -0.5).astype(BF16)
    k_ref[0] = y[:, GLA_KEY_W:2 * GLA_KEY_W].astype(BF16)
    v_ref[0] = y[:, 512:1024].astype(BF16)
    g_ref[0] = y[:, 1024:1536].astype(BF16)
    tail = y[:, TAIL0:IN_W]
    pre = _dot(tail.astype(BF16), wgk_ref[...]) + bgk_ref[...]
    logg = _log_sigmoid(pre) * (1.0 / GLA_GATE_NORM)
    lgf_ref[0] = logg[:, 0:GLA_KEY_W]
    lgb_ref[0] = logg[:, GLA_KEY_W:]
    rows = x.shape[0]
    tabs = (rc_ref[...], rs1_ref[...], rs2_ref[...])
    cq = _rms(y[:, 1536:1792], Q_LORA) * gql_ref[...]
    qm = _dot(cq.astype(BF16), wuq_ref[...])
    gq = gq_ref[...]
    for hh in range(MLA_HEADS):
        q_h = _rms(qm[:, hh * HEAD_W:(hh + 1) * HEAD_W], QK_DIM) * gq
        q_h = _rope(q_h, *tabs) * QK_DIM ** -0.5
        qm_ref[0, :, hh * HEAD_W:(hh + 1) * HEAD_W] = q_h.astype(BF16)
    ckv_n = _rms(y[:, 1792:1920], KV_LORA) * gkv_ref[...]
    ckvn_ref[0] = ckv_n
    kr_ref[0] = tail
    kr = jnp.where(_lane_mask(ROPE_LANE0, ROPE_LANE0 + ROPE_DIM, rows), tail, 0.0)
    _mla_kv(ckv_n, kr, wukv_ref, gk_ref[...], tabs, km_ref, vm_ref)


def _inproj(x, mod, per_batch_mod, p, rope_tabs, tm):
    b, l, d = x.shape
    nt = l // tm
    tab_blocks = rope_tabs[0].shape[0] // tm
    mod_map = (lambda bi, i: (bi, 0, 0)) if per_batch_mod else (lambda bi, i: (0, 0, 0))
    tab_map = (lambda bi, i: (i, 0)) if tab_blocks > 1 else (lambda bi, i: (0, 0))
    const = lambda bi, i: (0, 0)
    tok = lambda w: pl.BlockSpec((1, tm, w), lambda bi, i: (bi, i, 0))
    full = lambda a: pl.BlockSpec(a.shape, const)
    weights = [p['g_norm1'], p['w_in_p'], p['w_gk_big'], p['b_gk'], p['g_q_lora'], p['w_uq_p'],
               p['g_kv_lora'], p['w_ukv_p'], p['gq'], p['gk']]
    outs = [(GLA_KEY_W, BF16), (GLA_KEY_W, BF16), (GLA_VAL_W, BF16), (GLA_VAL_W, BF16),
            (GLA_KEY_W, F32), (GLA_KEY_W, F32), (MLA_W, BF16), (MLA_W, BF16), (MLA_W, BF16),
            (KV_LORA, F32), (LANES, F32)]
    return pl.pallas_call(
        _inproj_kernel,
        grid=(b, nt),
        in_specs=[tok(d), pl.BlockSpec((1, 6, d), mod_map)] + [full(w) for w in weights]
                 + [pl.BlockSpec((tm, LANES), tab_map)] * 3,
        out_specs=[tok(w) for w, _ in outs],
        out_shape=[jax.ShapeDtypeStruct((b, l, w), dt) for w, dt in outs],
        compiler_params=pltpu.CompilerParams(
            dimension_semantics=("parallel", "parallel"), vmem_limit_bytes=VMEM_LIMIT),
        name="inproj",
    )(x, mod, *weights, *rope_tabs)


def _cache_kv_kernel(ckv_ref, kr_ref, wukv_ref, gk_ref, k_ref, v_ref):
    _mla_kv(ckv_ref[0], kr_ref[0], wukv_ref, gk_ref[...], None, k_ref, v_ref)


def _cache_kv(ckv, kr128, p):
    b, l, _ = ckv.shape
    tok = lambda w: pl.BlockSpec((1, l, w), lambda bi: (bi, 0, 0))
    full = lambda a: pl.BlockSpec(a.shape, lambda bi: (0, 0))
    return pl.pallas_call(
        _cache_kv_kernel,
        grid=(b,),
        in_specs=[tok(KV_LORA), tok(LANES), full(p['w_ukv_p']), full(p['gk'])],
        out_specs=[tok(MLA_W), tok(MLA_W)],
        out_shape=[jax.ShapeDtypeStruct((b, l, MLA_W), BF16)] * 2,
        name="cache_kv",
    )(ckv, kr128, p['w_ukv_p'], p['gk'])


def _split3(x):
    hi = x.astype(BF16)
    r1 = x - hi.astype(F32)
    mid = r1.astype(BF16)
    lo = (r1 - mid.astype(F32)).astype(BF16)
    return hi, mid, lo


def _gla_block(q, k, v, lg, s, fwd):
    n = GLA_BLOCK
    nc = n // GLA_SUB
    row = lax.broadcasted_iota(jnp.int32, (n, n), 0)
    col = lax.broadcasted_iota(jnp.int32, (n, n), 1)
    same = (row // GLA_SUB) == (col // GLA_SUB)
    causal = same & ((col <= row) if fwd else (col >= row))
    tri = causal.astype(BF16)
    hi, mid, lo = _split3(lg)
    cum = _dot(tri, hi) + _dot(tri, mid) + _dot(tri, lo)
    tot_rows, mid_rows = [], []
    for c in range(nc):
        r_tot = c * GLA_SUB + (GLA_SUB - 1 if fwd else 0)
        r_mid = c * GLA_SUB + GLA_SUB // 2
        tot_rows.append(jnp.broadcast_to(cum[r_tot:r_tot + 1], (GLA_SUB, GLA_KEY_W)))
        mid_rows.append(jnp.broadcast_to(cum[r_mid:r_mid + 1], (GLA_SUB, GLA_KEY_W)))
    tot_rows = jnp.concatenate(tot_rows, axis=0)
    mid_rows = jnp.concatenate(mid_rows, axis=0)
    rel = cum - mid_rows
    qi = q * jnp.exp(rel)
    ki = (k * jnp.exp(-rel)).astype(BF16)
    q_in = (q * jnp.exp(cum)).astype(BF16)
    k_up = k * jnp.exp(tot_rows - cum)
    dec = jnp.exp(tot_rows)
    k_up_t = k_up.T.astype(BF16)
    dec_t = dec.T
    lane_head = lax.broadcasted_iota(jnp.int32, (n, GLA_KEY_W), 1) // GLA_DK
    srow_head = lax.broadcasted_iota(jnp.int32, (GLA_KEY_W, GLA_DV), 0) // GLA_DK
    vrow_chunk = lax.broadcasted_iota(jnp.int32, (n, GLA_VAL_W), 0) // GLA_SUB
    o_heads = []
    for h in range(GLA_HEADS):
        a = _dot_nt(jnp.where(lane_head == h, qi, 0.0).astype(BF16), ki)
        a = jnp.where(causal, a, 0.0).astype(BF16)
        o_heads.append(_dot(a, v[:, h * GLA_DV:(h + 1) * GLA_DV]))
    o = jnp.concatenate(o_heads, axis=1)
    o_inter = [None] * nc
    for c in (range(nc) if fwd else range(nc - 1, -1, -1)):
        s_bd = jnp.concatenate(
            [jnp.where(srow_head == h, s, 0.0).astype(BF16) for h in range(GLA_HEADS)], axis=1)
        o_inter[c] = _dot(q_in[c * GLA_SUB:(c + 1) * GLA_SUB], s_bd)
        v_c = jnp.where(vrow_chunk == c, v, jnp.zeros_like(v))
        u = jnp.concatenate(
            [_dot(k_up_t[h * GLA_DK:(h + 1) * GLA_DK], v_c[:, h * GLA_DV:(h + 1) * GLA_DV])
             for h in range(GLA_HEADS)], axis=0)
        s = dec_t[:, c * GLA_SUB:c * GLA_SUB + 1] * s + u
    return o + jnp.concatenate(o_inter, axis=0), s


def _gla_kernel(q_ref, k_ref, v_ref, g_ref, lgf_ref, lgb_ref, s0f_ref, s0b_ref, gout_ref,
                o_ref, sf_ref, sb_ref, acc_ref, st_ref, *, nblk):
    st_ref[0] = s0f_ref[0]
    st_ref[1] = s0b_ref[0]

    def load(blk):
        r = pl.ds(pl.multiple_of(blk * GLA_BLOCK, GLA_BLOCK), GLA_BLOCK)
        return r, q_ref[0, r, :].astype(F32), k_ref[0, r, :].astype(F32), v_ref[0, r, :]

    def fwd_step(blk, carry):
        r, q, k, v = load(blk)
        o, s = _gla_block(q, k, v, lgf_ref[0, r, :], st_ref[0], True)
        acc_ref[r, :] = o
        st_ref[0] = s
        return carry

    def bwd_step(i, carry):
        r, q, k, v = load(nblk - 1 - i)
        o, s = _gla_block(q, k, v, lgb_ref[0, r, :], st_ref[1], False)
        acc_ref[r, :] += o
        st_ref[1] = s
        return carry

    lax.fori_loop(0, nblk, fwd_step, 0)
    lax.fori_loop(0, nblk, bwd_step, 0)
    sf_ref[0] = st_ref[0]
    sb_ref[0] = st_ref[1]

    def fin_step(blk, carry):
        r = pl.ds(pl.multiple_of(blk * GLA_BLOCK, GLA_BLOCK), GLA_BLOCK)
        o = acc_ref[r, :]
        gate = _silu(g_ref[0, r, :].astype(F32))
        for h in range(GLA_HEADS):
            sl = slice(h * GLA_DV, (h + 1) * GLA_DV)
            o_ref[0, r, sl] = (_rms(o[:, sl], GLA_DV) * gout_ref[...] * gate[:, sl]).astype(BF16)
        return carry

    lax.fori_loop(0, nblk, fin_step, 0)


def _gla(q, k, v, g, lgf, lgb, s0f, s0b, g_out):
    b, l, _ = q.shape
    seq = lambda w: pl.BlockSpec((1, l, w), lambda bi: (bi, 0, 0))
    st = pl.BlockSpec((1, GLA_KEY_W, GLA_DV), lambda bi: (bi, 0, 0))
    return pl.pallas_call(
        functools.partial(_gla_kernel, nblk=l // GLA_BLOCK),
        grid=(b,),
        in_specs=[seq(GLA_KEY_W), seq(GLA_KEY_W), seq(GLA_VAL_W), seq(GLA_VAL_W),
                  seq(GLA_KEY_W), seq(GLA_KEY_W), st, st,
                  pl.BlockSpec((1, GLA_DV), lambda bi: (0, 0))],
        out_specs=[seq(GLA_VAL_W), st, st],
        out_shape=[jax.ShapeDtypeStruct((b, l, GLA_VAL_W), BF16),
                   jax.ShapeDtypeStruct((b, GLA_KEY_W, GLA_DV), F32),
                   jax.ShapeDtypeStruct((b, GLA_KEY_W, GLA_DV), F32)],
        scratch_shapes=[pltpu.VMEM((l, GLA_VAL_W), F32),
                        pltpu.VMEM((2, GLA_KEY_W, GLA_DV), F32)],
        compiler_params=pltpu.CompilerParams(
            dimension_semantics=("parallel",), vmem_limit_bytes=VMEM_LIMIT),
        name="gla",
    )(q, k, v, g, lgf, lgb, s0f, s0b, g_out)


def _attn_kernel(q_ref, k_ref, v_ref, o_ref):
    for h in range(MLA_HEADS):
        sl = slice(h * HEAD_W, (h + 1) * HEAD_W)
        s = _dot_nt(q_ref[0, :, sl], k_ref[0, :, sl])
        m = jnp.max(s, axis=-1, keepdims=True)
        p = jnp.exp(s - m)
        den = jnp.sum(p, axis=-1, keepdims=True)
        o = _dot(p.astype(BF16), v_ref[0, :, sl])
        o_ref[0, :, sl] = (o / den).astype(BF16)


def _attn(q, k, v, tq):
    b, l, _ = q.shape
    lk = k.shape[1]
    return pl.pallas_call(
        _attn_kernel,
        grid=(b, l // tq),
        in_specs=[pl.BlockSpec((1, tq, MLA_W), lambda bi, i: (bi, i, 0)),
                  pl.BlockSpec((1, lk, MLA_W), lambda bi, i: (bi, 0, 0)),
                  pl.BlockSpec((1, lk, MLA_W), lambda bi, i: (bi, 0, 0))],
        out_specs=pl.BlockSpec((1, tq, MLA_W), lambda bi, i: (bi, i, 0)),
        out_shape=jax.ShapeDtypeStruct((b, l, MLA_W), BF16),
        compiler_params=pltpu.CompilerParams(
            dimension_semantics=("parallel", "parallel"), vmem_limit_bytes=VMEM_LIMIT),
        name="attn",
    )(q, k, v)


def _route(logits_t, bias_col):
    t = logits_t.shape[1]
    gsz = N_EXPERTS // N_GROUPS
    scores = jax.nn.sigmoid(logits_t)
    sel = (scores + bias_col).reshape(N_GROUPS, gsz, t)
    scores = scores.reshape(N_GROUPS, gsz, t)
    neg = jnp.float32(-jnp.inf)
    ie = lax.broadcasted_iota(jnp.int32, (N_GROUPS, gsz, t), 1)
    ig = lax.broadcasted_iota(jnp.int32, (N_GROUPS, gsz, t), 0)
    m1 = jnp.max(sel, axis=1, keepdims=True)
    first = jnp.min(jnp.where(sel == m1, ie, gsz), axis=1, keepdims=True)
    m2 = jnp.max(jnp.where(ie == first, neg, sel), axis=1, keepdims=True)
    grp = m1 + m2
    igk = lax.broadcasted_iota(jnp.int32, (N_GROUPS, 1, t), 0)
    g_sel = jnp.zeros((N_GROUPS, 1, t), jnp.bool_)
    cur = grp
    for _ in range(TOPK_GROUPS):
        m = jnp.max(cur, axis=0, keepdims=True)
        pick = igk == jnp.min(jnp.where(cur == m, igk, N_GROUPS), axis=0, keepdims=True)
        g_sel = g_sel | pick
        cur = jnp.where(pick, neg, cur)
    cur = jnp.where(g_sel, sel, neg)
    idx = ig * gsz + ie
    e_sel = jnp.zeros((N_GROUPS, gsz, t), jnp.bool_)
    for _ in range(TOP_K):
        m = jnp.max(jnp.max(cur, axis=1, keepdims=True), axis=0, keepdims=True)
        cand = jnp.where(cur == m, idx, N_EXPERTS)
        pick = idx == jnp.min(jnp.min(cand, axis=1, keepdims=True), axis=0, keepdims=True)
        e_sel = e_sel | pick
        cur = jnp.where(pick, neg, cur)
    w = jnp.where(e_sel, scores, 0.0)
    wsum = jnp.sum(jnp.sum(w, axis=1, keepdims=True), axis=0, keepdims=True)
    return (w / wsum * ROUTED_SCALE).reshape(N_EXPERTS, t)


def _outproj_kernel(og_ref, om_ref, x_ref, mod_ref, wtop_ref, wbot_ref, g2_ref, wr_ref, br_ref,
                    x1_ref, h2_ref, gates_ref):
    mod = mod_ref[0]
    mix = _dot(og_ref[0], wtop_ref[...]) + _dot(om_ref[0], wbot_ref[...])
    x1 = x_ref[0] + mod[2:3] * mix
    x1_ref[0] = x1
    h2 = _rms(x1, D_MODEL) * g2_ref[...] * (1.0 + mod[4:5]) + mod[3:4]
    h2_ref[0] = h2.astype(BF16)
    logits_t = _dot_nt(wr_ref[...], h2, precision=HIGHEST)
    gates_t = _route(logits_t, br_ref[...])
    t = gates_t.shape[1]
    extra = (lax.broadcasted_iota(jnp.int32, (LANES - N_EXPERTS, t), 0) == 0).astype(F32)
    gates_ref[0] = jnp.concatenate([gates_t, extra], axis=0).T


def _outproj(og, om, x, mod, per_batch_mod, p, tm):
    b, l, d = x.shape
    mod_map = (lambda bi, i: (bi, 0, 0)) if per_batch_mod else (lambda bi, i: (0, 0, 0))
    tok = lambda w: pl.BlockSpec((1, tm, w), lambda bi, i: (bi, i, 0))
    full = lambda a: pl.BlockSpec(a.shape, lambda bi, i: (0, 0))
    weights = [p['w_out_top'], p['w_out_bot'], p['g_norm2'], p['w_router_t'], p['b_router_col']]
    return pl.pallas_call(
        _outproj_kernel,
        grid=(b, l // tm),
        in_specs=[tok(GLA_VAL_W), tok(MLA_W), tok(d), pl.BlockSpec((1, 6, d), mod_map)]
                 + [full(w) for w in weights],
        out_specs=[tok(d), tok(d), tok(LANES)],
        out_shape=[jax.ShapeDtypeStruct((b, l, d), F32), jax.ShapeDtypeStruct((b, l, d), BF16),
                   jax.ShapeDtypeStruct((b, l, LANES), F32)],
        compiler_params=pltpu.CompilerParams(
            dimension_semantics=("parallel", "parallel"), vmem_limit_bytes=VMEM_LIMIT),
        name="outproj",
    )(og, om, x, mod, *weights)


def _moe_kernel(x1_ref, h2_ref, gates_ref, mod_ref, wg_ref, wu_ref, wd_ref, o_ref, acc_ref):
    e = pl.program_id(2)

    @pl.when(e == 0)
    def _():
        acc_ref[...] = jnp.zeros_like(acc_ref)

    h2 = h2_ref[0]
    a = _silu(_dot(h2, wg_ref[0])) * _dot(h2, wu_ref[0])
    gates = gates_ref[0]
    lane = lax.broadcasted_iota(jnp.int32, gates.shape, 1)
    gate = jnp.sum(jnp.where(lane == e, gates, 0.0), axis=1, keepdims=True)
    acc_ref[...] += _dot((a * gate).astype(BF16), wd_ref[0])

    @pl.when(e == pl.num_programs(2) - 1)
    def _():
        o_ref[0] = x1_ref[0] + mod_ref[0][5:6] * acc_ref[...]


def _moe(x1, h2, gates, mod, per_batch_mod, p, tm):
    b, l, d = x1.shape
    ne = p['w_gate_all'].shape[0]
    mod_map = (lambda bi, i, e: (bi, 0, 0)) if per_batch_mod else (lambda bi, i, e: (0, 0, 0))
    tok = lambda w: pl.BlockSpec((1, tm, w), lambda bi, i, e: (bi, i, 0))
    return pl.pallas_call(
        _moe_kernel,
        grid=(b, l // tm, ne),
        in_specs=[tok(d), tok(d), tok(LANES), pl.BlockSpec((1, 6, d), mod_map),
                  pl.BlockSpec((1, d, D_EXPERT), lambda bi, i, e: (e, 0, 0)),
                  pl.BlockSpec((1, d, D_EXPERT), lambda bi, i, e: (e, 0, 0)),
                  pl.BlockSpec((1, D_EXPERT, d), lambda bi, i, e: (e, 0, 0))],
        out_specs=tok(d),
        out_shape=jax.ShapeDtypeStruct((b, l, d), F32),
        scratch_shapes=[pltpu.VMEM((tm, d), F32)],
        compiler_params=pltpu.CompilerParams(
            dimension_semantics=("parallel", "parallel", "arbitrary"),
            vmem_limit_bytes=VMEM_LIMIT),
        name="moe",
    )(x1, h2, gates, mod, p['w_gate_all'], p['w_up_all'], p['w_down_all'])


def _pad_heads(w, parts):
    k = w.shape[0]
    per = w.shape[1] // MLA_HEADS
    w = w.reshape(k, MLA_HEADS, per)[:, :, parts[0]:parts[1]]
    w = jnp.pad(w, ((0, 0), (0, 0), (0, HEAD_W - (parts[1] - parts[0]))))
    return w.reshape(k, MLA_HEADS * HEAD_W)


def _prep_params(l, g_norm1, g_norm2, w_in, w_gk_fwd, b_gk_fwd, w_gk_bwd, b_gk_bwd, g_gla_out,
                 g_q_lora, w_uq, g_kv_lora, w_ukv, g_qk_q, g_qk_k, w_out, w_router, b_router,
                 w_exp_gate, w_exp_up, w_exp_down, w_sh_gate, w_sh_up, w_sh_down):
    w = w_in[l]
    d = w.shape[0]
    w_in_p = jnp.concatenate([
        w[:, 0:1536], w[:, 1568:1824], w[:, 1824:1952], jnp.zeros((d, 64), w.dtype),
        w[:, 1952:1984], w[:, 1536:1552], w[:, 1552:1568]], axis=1).astype(BF16)
    z = jnp.zeros((GLA_GATE_RANK, GLA_KEY_W), F32)
    w_gk_big = jnp.concatenate([
        jnp.zeros((LANES - 2 * GLA_GATE_RANK, 2 * GLA_KEY_W), F32),
        jnp.concatenate([w_gk_fwd[l], z], axis=1),
        jnp.concatenate([z, w_gk_bwd[l]], axis=1)], axis=0).astype(BF16)
    pad_gain = lambda g: jnp.pad(g, (0, HEAD_W - QK_DIM)).reshape(1, HEAD_W)
    w_ukv_h = w_ukv[l]
    return {
        'g_norm1': g_norm1[l].reshape(1, d), 'g_norm2': g_norm2[l].reshape(1, d),
        'w_in_p': w_in_p, 'w_gk_big': w_gk_big,
        'b_gk': jnp.concatenate([b_gk_fwd[l], b_gk_bwd[l]]).reshape(1, 2 * GLA_KEY_W),
        'g_gla_out': g_gla_out[l].reshape(1, GLA_DV),
        'g_q_lora': g_q_lora[l].reshape(1, Q_LORA),
        'w_uq_p': _pad_heads(w_uq[l], (0, QK_DIM)).astype(BF16),
        'g_kv_lora': g_kv_lora[l].reshape(1, KV_LORA),
        'w_ukv_p': jnp.concatenate([_pad_heads(w_ukv_h, (0, NOPE_DIM)),
                                    _pad_heads(w_ukv_h, (NOPE_DIM, NOPE_DIM + V_DIM))],
                                   axis=1).astype(BF16),
        'gq': pad_gain(g_qk_q[l]), 'gk': pad_gain(g_qk_k[l]),
        'w_out_top': w_out[l][:GLA_VAL_W].astype(BF16),
        'w_out_bot': w_out[l][GLA_VAL_W:].astype(BF16),
        'w_router_t': w_router[l].T, 'b_router_col': b_router[l].reshape(N_EXPERTS, 1),
        'w_gate_all': jnp.concatenate([w_exp_gate[l], w_sh_gate[l][None]]).astype(BF16),
        'w_up_all': jnp.concatenate([w_exp_up[l], w_sh_up[l][None]]).astype(BF16),
        'w_down_all': jnp.concatenate([w_exp_down[l], w_sh_down[l][None]]).astype(BF16),
    }


def _rope_tables(length):
    pos = jnp.arange(length)
    r = (pos // GRID_W).astype(F32)
    c = (pos % GRID_W).astype(F32)
    half = ROPE_DIM // 4
    inv_freq = ROPE_THETA ** (-jnp.arange(half, dtype=F32) / half)
    ang_r = r[:, None] * inv_freq[None, :]
    ang_c = c[:, None] * inv_freq[None, :]
    zeros = lambda w: jnp.zeros((length, w), F32)
    ones = lambda w: jnp.ones((length, w), F32)
    cos = jnp.concatenate([ones(ROPE_LANE0), jnp.cos(ang_r), jnp.cos(ang_r), jnp.cos(ang_c),
                           jnp.cos(ang_c), ones(HEAD_W - ROPE_LANE0 - ROPE_DIM)], axis=1)
    s1 = jnp.concatenate([zeros(ROPE_LANE0), -jnp.sin(ang_r), zeros(half), -jnp.sin(ang_c),
                          zeros(half), zeros(HEAD_W - ROPE_LANE0 - ROPE_DIM)], axis=1)
    s2 = jnp.concatenate([zeros(ROPE_LANE0), zeros(half), jnp.sin(ang_r), zeros(half),
                          jnp.sin(ang_c), zeros(HEAD_W - ROPE_LANE0 - ROPE_DIM)], axis=1)
    return cos, s1, s2


def _identity_tables(rows):
    return (jnp.ones((rows, LANES), F32), jnp.zeros((rows, LANES), F32),
            jnp.zeros((rows, LANES), F32))


def _layer(x_tok, seq_shape, mod, per_batch_mod, p, s0f, s0b, rope_tabs, ctx_kv, tm, tq, tmoe):
    bt, lt, d = x_tok.shape
    b, l = seq_shape
    (q, k, v, g, lgf, lgb, qm, km, vm, ckvn, kr) = _inproj(x_tok, mod, per_batch_mod, p,
                                                            rope_tabs, tm)
    seq = lambda a: a.reshape(b, l, a.shape[-1])
    og, sf, sb = _gla(seq(q), seq(k), seq(v), seq(g), seq(lgf), seq(lgb), s0f, s0b,
                      p['g_gla_out'])
    km, vm = seq(km), seq(vm)
    if ctx_kv is not None:
        km = jnp.concatenate([km, ctx_kv[0]], axis=1)
        vm = jnp.concatenate([vm, ctx_kv[1]], axis=1)
    om = _attn(seq(qm), km, vm, tq)
    tokv = lambda a: a.reshape(bt, lt, a.shape[-1])
    x1, h2, gates = _outproj(tokv(og), tokv(om), x_tok, mod, per_batch_mod, p, tm)
    y = _moe(x1, h2, gates, mod, per_batch_mod, p, tmoe)
    return y, sf, sb, ckvn, kr


def kernel(x_prompt, x_sample, c, state_gla_fwd, state_gla_bwd, cache_mla_ckv, cache_mla_krope,
           c_ctx, w_ada, b_ada, g_norm1, g_norm2, w_in, w_gk_fwd, b_gk_fwd, w_gk_bwd, b_gk_bwd,
           g_gla_out, g_q_lora, w_uq, g_kv_lora, w_ukv, g_qk_q, g_qk_k, w_out,
           w_router, b_router, w_exp_gate, w_exp_up, w_exp_down, w_sh_gate, w_sh_up, w_sh_down):
    bp, lp, d = x_prompt.shape
    bs, ls, _ = x_sample.shape
    depth = w_ada.shape[0]
    xp = x_prompt.reshape(1, bp * lp, d)
    xs = x_sample
    new_f, new_b, new_ckv, new_kr = [], [], [], []
    lat_tabs = _rope_tables(ls)
    tm = 256
    ctx_tabs = _identity_tables(tm)
    cvecs = jnp.concatenate([c_ctx[None], c, jnp.zeros((8 - 1 - bs, d), F32)], axis=0)
    for l in range(depth):
        p = _prep_params(l, g_norm1, g_norm2, w_in, w_gk_fwd, b_gk_fwd, w_gk_bwd, b_gk_bwd,
                         g_gla_out, g_q_lora, w_uq, g_kv_lora, w_ukv, g_qk_q, g_qk_k, w_out,
                         w_router, b_router, w_exp_gate, w_exp_up, w_exp_down,
                         w_sh_gate, w_sh_up, w_sh_down)
        mod = _ada(cvecs, w_ada[l], b_ada[l]).reshape(8, 6, d)
        zeros = jnp.zeros((bp, GLA_KEY_W, GLA_DV), F32)
        xp, sf, sb, ckvn, kr = _layer(xp, (bp, lp), mod[0:1], False, p, zeros, zeros,
                                      ctx_tabs, None, tm, lp, 1024)
        new_f.append(sf.reshape(bp, GLA_HEADS, GLA_DK, GLA_DV))
        new_b.append(sb.reshape(bp, GLA_HEADS, GLA_DK, GLA_DV))
        new_ckv.append(ckvn.reshape(bp, lp, KV_LORA))
        new_kr.append(kr.reshape(bp, lp, LANES)[:, :, ROPE_LANE0:ROPE_LANE0 + ROPE_DIM])
        past = cache_mla_krope.shape[2]
        kr_cache = jnp.pad(cache_mla_krope[:, l],
                           ((0, 0), (0, 0), (ROPE_LANE0, LANES - ROPE_LANE0 - ROPE_DIM)))
        ctx_kv = _cache_kv(cache_mla_ckv[:, l], kr_cache, p)
        s0f = state_gla_fwd[:, l].reshape(bs, GLA_KEY_W, GLA_DV)
        s0b = state_gla_bwd[:, l].reshape(bs, GLA_KEY_W, GLA_DV)
        xs, _, _, _, _ = _layer(xs, (bs, ls), mod[1:1 + bs], True, p, s0f, s0b,
                                lat_tabs, ctx_kv, tm, 256, 1024)
        del past
    return (xp.reshape(bp, lp, d), xs, jnp.stack(new_f, axis=1), jnp.stack(new_b, axis=1),
            jnp.stack(new_ckv, axis=1), jnp.stack(new_kr, axis=1))
```

```python
import functools

import jax
import jax.numpy as jnp
from jax import lax
from jax.experimental import pallas as pl
from jax.experimental.pallas import tpu as pltpu
from jax.experimental.pallas import tpu_sc as plsc

F32 = jnp.float32
BF16 = jnp.bfloat16
I32 = jnp.int32
U32 = jnp.uint32

D_MODEL = 1024
EPS = 1e-6
GRID_W = 64
GLA_HEADS = 4
GLA_DK = 64
GLA_DV = 128
GLA_GATE_RANK = 16
GLA_GATE_NORM = 16.0
GLA_KEY_W = GLA_HEADS * GLA_DK
GLA_VAL_W = GLA_HEADS * GLA_DV
MLA_HEADS = 4
Q_LORA = 256
KV_LORA = 128
NOPE_DIM = 64
ROPE_DIM = 32
V_DIM = 128
QK_DIM = NOPE_DIM + ROPE_DIM
ROPE_THETA = 10000.0
N_EXPERTS = 64
TOP_K = 8
N_GROUPS = 8
TOPK_GROUPS = 4
D_EXPERT = 256
ROUTED_SCALE = 2.5

LANES = 128
HEAD_W = LANES
MLA_W = MLA_HEADS * HEAD_W
ROPE_LANE0 = NOPE_DIM
COL_V = 2 * GLA_KEY_W
COL_G = COL_V + GLA_VAL_W
COL_CQ = COL_G + GLA_VAL_W
COL_CKV = COL_CQ + Q_LORA
TAIL0 = COL_CKV + KV_LORA
IN_W = TAIL0 + LANES
KR_LANE0 = LANES - ROPE_DIM - 2 * GLA_GATE_RANK
GLA_BLOCK = 256
GLA_SUB = 64
FFN_TILE = 256
PACK_PARTS = 2
PACK_W = D_MODEL // (2 * PACK_PARTS)
SC_WINDOW = 128
VMEM_LIMIT = 56 * 1024 * 1024

HIGHEST = lax.Precision.HIGHEST


def _dot(a, b, precision=None):
    return jnp.dot(a, b, preferred_element_type=F32, precision=precision)


def _dot_nt(a, b, precision=None):
    return lax.dot_general(a, b, (((1,), (1,)), ((), ())), preferred_element_type=F32,
                           precision=precision)


def _rms(x, width):
    ss = jnp.sum(x * x, axis=-1, keepdims=True) * (1.0 / width)
    return x * lax.rsqrt(ss + EPS)


def _silu(x):
    return x * jax.nn.sigmoid(x)


def _log_sigmoid(x):
    return jnp.minimum(x, 0.0) - jnp.log1p(jnp.exp(-jnp.abs(x)))


def _pack_rows(x):
    parts = []
    for i in range(PACK_PARTS):
        c0 = i * 2 * PACK_W
        lo = lax.bitcast_convert_type(x[:, c0:c0 + PACK_W].astype(BF16).astype(F32), U32)
        hi = lax.bitcast_convert_type(
            x[:, c0 + PACK_W:c0 + 2 * PACK_W].astype(BF16).astype(F32), U32)
        parts.append(hi | (lo >> 16))
    return parts


def _unpack_rows(parts):
    cols = []
    for w in parts:
        cols.append(lax.bitcast_convert_type(w << 16, F32))
        cols.append(lax.bitcast_convert_type(w & jnp.uint32(0xFFFF0000), F32))
    return jnp.concatenate(cols, axis=1)


def _ada_kernel(c_ref, w_ref, b_ref, o_ref):
    o_ref[...] = _dot(_silu(c_ref[...]), w_ref[...], precision=HIGHEST) + b_ref[...]


def _ada(cvecs, w_ada, b_ada):
    n = w_ada.shape[1]
    tn = 768
    return pl.pallas_call(
        _ada_kernel,
        grid=(n // tn,),
        in_specs=[pl.BlockSpec((8, D_MODEL), lambda j: (0, 0)),
                  pl.BlockSpec((D_MODEL, tn), lambda j: (0, j)),
                  pl.BlockSpec((1, tn), lambda j: (0, j))],
        out_specs=pl.BlockSpec((8, tn), lambda j: (0, j)),
        out_shape=jax.ShapeDtypeStruct((8, n), F32),
        name="ada",
    )(cvecs, w_ada, b_ada.reshape(1, n))


def _rope(x, c, s1, s2):
    return x * c + pltpu.roll(x, LANES - 8, 1) * s1 + pltpu.roll(x, 8, 1) * s2


def _mla_kv(ckv_n, kr, w_ukv_ref, gk, rope_tabs, k_ref, v_ref):
    kv = _dot(ckv_n.astype(BF16), w_ukv_ref[...])
    for h in range(MLA_HEADS):
        k_h = kv[:, h * HEAD_W:(h + 1) * HEAD_W] + kr
        k_h = _rms(k_h, QK_DIM) * gk
        if rope_tabs is not None:
            k_h = _rope(k_h, *rope_tabs)
        k_ref[0, :, h * HEAD_W:(h + 1) * HEAD_W] = k_h.astype(BF16)
    v_ref[0] = kv[:, MLA_W:].astype(BF16)


def _lane_mask(lo, hi, rows):
    lane = lax.broadcasted_iota(I32, (rows, LANES), 1)
    return (lane >= lo) & (lane < hi)


def _inproj_kernel(x_ref, mod_ref, g1_ref, win_ref, wgk_ref, bgk_ref, gql_ref, wuq_ref,
                   gkv_ref, wukv_ref, gq_ref, gk_ref, rc_ref, rs1_ref, rs2_ref,
                   q_ref, k_ref, v_ref, g_ref, lgf_ref, lgb_ref,
                   qm_ref, km_ref, vm_ref, ckvn_ref, kr_ref):
    x = x_ref[0]
    mod = mod_ref[0]
    h = _rms(x, D_MODEL) * g1_ref[...] * (1.0 + mod[1:2]) + mod[0:1]
    y = _dot(h.astype(BF16), win_ref[...])
    q_ref[0] = (y[:, 0:GLA_KEY_W] * GLA_DK ** -0.5).astype(BF16)
    k_ref[0] = y[:, GLA_KEY_W:COL_V].astype(BF16)
    v_ref[0] = y[:, COL_V:COL_G].astype(BF16)
    g_ref[0] = y[:, COL_G:COL_CQ].astype(BF16)
    tail = y[:, TAIL0:IN_W]
    pre = _dot(tail.astype(BF16), wgk_ref[...]) + bgk_ref[...]
    logg = _log_sigmoid(pre) * (1.0 / GLA_GATE_NORM)
    lgf_ref[0] = logg[:, 0:GLA_KEY_W]
    lgb_ref[0] = logg[:, GLA_KEY_W:]
    rows = x.shape[0]
    tabs = (rc_ref[...], rs1_ref[...], rs2_ref[...])
    cq = _rms(y[:, COL_CQ:COL_CKV], Q_LORA) * gql_ref[...]
    qm = _dot(cq.astype(BF16), wuq_ref[...])
    gq = gq_ref[...]
    for hh in range(MLA_HEADS):
        q_h = _rms(qm[:, hh * HEAD_W:(hh + 1) * HEAD_W], QK_DIM) * gq
        q_h = _rope(q_h, *tabs) * QK_DIM ** -0.5
        qm_ref[0, :, hh * HEAD_W:(hh + 1) * HEAD_W] = q_h.astype(BF16)
    ckv_n = _rms(y[:, COL_CKV:TAIL0], KV_LORA) * gkv_ref[...]
    ckvn_ref[0] = ckv_n
    kr_ref[0] = tail
    kr = jnp.where(_lane_mask(KR_LANE0, KR_LANE0 + ROPE_DIM, rows), tail, 0.0)
    _mla_kv(ckv_n, kr, wukv_ref, gk_ref[...], tabs, km_ref, vm_ref)


def _inproj(x, mod, per_batch_mod, p, rope_tabs, tm):
    b, l, d = x.shape
    nt = l // tm
    tab_blocks = rope_tabs[0].shape[0] // tm
    mod_map = (lambda bi, i: (bi, 0, 0)) if per_batch_mod else (lambda bi, i: (0, 0, 0))
    tab_map = (lambda bi, i: (i, 0)) if tab_blocks > 1 else (lambda bi, i: (0, 0))
    const = lambda bi, i: (0, 0)
    tok = lambda w: pl.BlockSpec((1, tm, w), lambda bi, i: (bi, i, 0))
    full = lambda a: pl.BlockSpec(a.shape, const)
    weights = [p['g_norm1'], p['w_in_p'], p['w_gk_big'], p['b_gk'], p['g_q_lora'], p['w_uq_p'],
               p['g_kv_lora'], p['w_ukv_p'], p['gq'], p['gk']]
    outs = [(GLA_KEY_W, BF16), (GLA_KEY_W, BF16), (GLA_VAL_W, BF16), (GLA_VAL_W, BF16),
            (GLA_KEY_W, F32), (GLA_KEY_W, F32), (MLA_W, BF16), (MLA_W, BF16), (MLA_W, BF16),
            (KV_LORA, F32), (LANES, F32)]
    return pl.pallas_call(
        _inproj_kernel,
        grid=(b, nt),
        in_specs=[tok(d), pl.BlockSpec((1, 6, d), mod_map)] + [full(w) for w in weights]
                 + [pl.BlockSpec((tm, LANES), tab_map)] * 3,
        out_specs=[tok(w) for w, _ in outs],
        out_shape=[jax.ShapeDtypeStruct((b, l, w), dt) for w, dt in outs],
        compiler_params=pltpu.CompilerParams(
            dimension_semantics=("parallel", "parallel"), vmem_limit_bytes=VMEM_LIMIT),
        name="inproj",
    )(x, mod, *weights, *rope_tabs)


def _cache_kv_kernel(ckv_ref, kr_ref, wukv_ref, gk_ref, k_ref, v_ref):
    _mla_kv(ckv_ref[0], kr_ref[0], wukv_ref, gk_ref[...], None, k_ref, v_ref)


def _cache_kv(ckv, kr128, p):
    b, l, _ = ckv.shape
    tok = lambda w: pl.BlockSpec((1, l, w), lambda bi: (bi, 0, 0))
    full = lambda a: pl.BlockSpec(a.shape, lambda bi: (0, 0))
    return pl.pallas_call(
        _cache_kv_kernel,
        grid=(b,),
        in_specs=[tok(KV_LORA), tok(LANES), full(p['w_ukv_p']), full(p['gk'])],
        out_specs=[tok(MLA_W), tok(MLA_W)],
        out_shape=[jax.ShapeDtypeStruct((b, l, MLA_W), BF16)] * 2,
        name="cache_kv",
    )(ckv, kr128, p['w_ukv_p'], p['gk'])


def _split3(x):
    hi = x.astype(BF16)
    r1 = x - hi.astype(F32)
    mid = r1.astype(BF16)
    lo = (r1 - mid.astype(F32)).astype(BF16)
    return hi, mid, lo


def _gla_block(q, k, v, lg, s, fwd):
    n = GLA_BLOCK
    nc = n // GLA_SUB
    row = lax.broadcasted_iota(I32, (n, n), 0)
    col = lax.broadcasted_iota(I32, (n, n), 1)
    same = (row // GLA_SUB) == (col // GLA_SUB)
    causal = same & ((col <= row) if fwd else (col >= row))
    tri = causal.astype(BF16)
    hi, mid, lo = _split3(lg)
    cum = _dot(tri, hi) + _dot(tri, mid) + _dot(tri, lo)
    tot_rows, mid_rows = [], []
    for c in range(nc):
        r_tot = c * GLA_SUB + (GLA_SUB - 1 if fwd else 0)
        r_mid = c * GLA_SUB + GLA_SUB // 2
        tot_rows.append(jnp.broadcast_to(cum[r_tot:r_tot + 1], (GLA_SUB, GLA_KEY_W)))
        mid_rows.append(jnp.broadcast_to(cum[r_mid:r_mid + 1], (GLA_SUB, GLA_KEY_W)))
    tot_rows = jnp.concatenate(tot_rows, axis=0)
    mid_rows = jnp.concatenate(mid_rows, axis=0)
    rel = cum - mid_rows
    qi = q * jnp.exp(rel)
    ki = (k * jnp.exp(-rel)).astype(BF16)
    q_in = (q * jnp.exp(cum)).astype(BF16)
    k_up = k * jnp.exp(tot_rows - cum)
    dec = jnp.exp(tot_rows)
    k_up_t = k_up.T.astype(BF16)
    dec_t = dec.T
    lane_head = lax.broadcasted_iota(I32, (n, GLA_KEY_W), 1) // GLA_DK
    srow_head = lax.broadcasted_iota(I32, (GLA_KEY_W, GLA_DV), 0) // GLA_DK
    vrow_chunk = lax.broadcasted_iota(I32, (n, GLA_VAL_W), 0) // GLA_SUB
    o_heads = []
    for h in range(GLA_HEADS):
        a = _dot_nt(jnp.where(lane_head == h, qi, 0.0).astype(BF16), ki)
        a = jnp.where(causal, a, 0.0).astype(BF16)
        o_heads.append(_dot(a, v[:, h * GLA_DV:(h + 1) * GLA_DV]))
    o = jnp.concatenate(o_heads, axis=1)
    o_inter = [None] * nc
    for c in (range(nc) if fwd else range(nc - 1, -1, -1)):
        s_bd = jnp.concatenate(
            [jnp.where(srow_head == h, s, 0.0).astype(BF16) for h in range(GLA_HEADS)], axis=1)
        o_inter[c] = _dot(q_in[c * GLA_SUB:(c + 1) * GLA_SUB], s_bd)
        v_c = jnp.where(vrow_chunk == c, v, jnp.zeros_like(v))
        u = jnp.concatenate(
            [_dot(k_up_t[h * GLA_DK:(h + 1) * GLA_DK], v_c[:, h * GLA_DV:(h + 1) * GLA_DV])
             for h in range(GLA_HEADS)], axis=0)
        s = dec_t[:, c * GLA_SUB:c * GLA_SUB + 1] * s + u
    return o + jnp.concatenate(o_inter, axis=0), s


def _gla_kernel(q_ref, k_ref, v_ref, g_ref, lgf_ref, lgb_ref, s0f_ref, s0b_ref, gout_ref,
                o_ref, sf_ref, sb_ref, acc_ref, st_ref, *, nblk):
    st_ref[0] = s0f_ref[0]
    st_ref[1] = s0b_ref[0]

    def load(blk):
        r = pl.ds(pl.multiple_of(blk * GLA_BLOCK, GLA_BLOCK), GLA_BLOCK)
        return r, q_ref[0, r, :].astype(F32), k_ref[0, r, :].astype(F32), v_ref[0, r, :]

    def fwd_step(blk, carry):
        r, q, k, v = load(blk)
        o, s = _gla_block(q, k, v, lgf_ref[0, r, :], st_ref[0], True)
        acc_ref[r, :] = o
        st_ref[0] = s
        return carry

    def bwd_step(i, carry):
        r, q, k, v = load(nblk - 1 - i)
        o, s = _gla_block(q, k, v, lgb_ref[0, r, :], st_ref[1], False)
        acc_ref[r, :] += o
        st_ref[1] = s
        return carry

    lax.fori_loop(0, nblk, fwd_step, 0)
    lax.fori_loop(0, nblk, bwd_step, 0)
    sf_ref[0] = st_ref[0]
    sb_ref[0] = st_ref[1]

    def fin_step(blk, carry):
        r = pl.ds(pl.multiple_of(blk * GLA_BLOCK, GLA_BLOCK), GLA_BLOCK)
        o = acc_ref[r, :]
        gate = _silu(g_ref[0, r, :].astype(F32))
        for h in range(GLA_HEADS):
            sl = slice(h * GLA_DV, (h + 1) * GLA_DV)
            o_ref[0, r, sl] = (_rms(o[:, sl], GLA_DV) * gout_ref[...] * gate[:, sl]).astype(BF16)
        return carry

    lax.fori_loop(0, nblk, fin_step, 0)


def _gla(q, k, v, g, lgf, lgb, s0f, s0b, g_out):
    b, l, _ = q.shape
    seq = lambda w: pl.BlockSpec((1, l, w), lambda bi: (bi, 0, 0))
    st = pl.BlockSpec((1, GLA_KEY_W, GLA_DV), lambda bi: (bi, 0, 0))
    return pl.pallas_call(
        functools.partial(_gla_kernel, nblk=l // GLA_BLOCK),
        grid=(b,),
        in_specs=[seq(GLA_KEY_W), seq(GLA_KEY_W), seq(GLA_VAL_W), seq(GLA_VAL_W),
                  seq(GLA_KEY_W), seq(GLA_KEY_W), st, st,
                  pl.BlockSpec((1, GLA_DV), lambda bi: (0, 0))],
        out_specs=[seq(GLA_VAL_W), st, st],
        out_shape=[jax.ShapeDtypeStruct((b, l, GLA_VAL_W), BF16),
                   jax.ShapeDtypeStruct((b, GLA_KEY_W, GLA_DV), F32),
                   jax.ShapeDtypeStruct((b, GLA_KEY_W, GLA_DV), F32)],
        scratch_shapes=[pltpu.VMEM((l, GLA_VAL_W), F32),
                        pltpu.VMEM((2, GLA_KEY_W, GLA_DV), F32)],
        compiler_params=pltpu.CompilerParams(
            dimension_semantics=("parallel",), vmem_limit_bytes=VMEM_LIMIT),
        name="gla",
    )(q, k, v, g, lgf, lgb, s0f, s0b, g_out)


def _attn_kernel(q_ref, k_ref, v_ref, o_ref):
    for h in range(MLA_HEADS):
        sl = slice(h * HEAD_W, (h + 1) * HEAD_W)
        s = _dot_nt(q_ref[0, :, sl], k_ref[0, :, sl])
        m = jnp.max(s, axis=-1, keepdims=True)
        p = jnp.exp(s - m)
        den = jnp.sum(p, axis=-1, keepdims=True)
        o = _dot(p.astype(BF16), v_ref[0, :, sl])
        o_ref[0, :, sl] = (o / den).astype(BF16)


def _attn(q, k, v, tq):
    b, l, _ = q.shape
    lk = k.shape[1]
    return pl.pallas_call(
        _attn_kernel,
        grid=(b, l // tq),
        in_specs=[pl.BlockSpec((1, tq, MLA_W), lambda bi, i: (bi, i, 0)),
                  pl.BlockSpec((1, lk, MLA_W), lambda bi, i: (bi, 0, 0)),
                  pl.BlockSpec((1, lk, MLA_W), lambda bi, i: (bi, 0, 0))],
        out_specs=pl.BlockSpec((1, tq, MLA_W), lambda bi, i: (bi, i, 0)),
        out_shape=jax.ShapeDtypeStruct((b, l, MLA_W), BF16),
        compiler_params=pltpu.CompilerParams(
            dimension_semantics=("parallel", "parallel"), vmem_limit_bytes=VMEM_LIMIT),
        name="attn",
    )(q, k, v)


def _sum01(x):
    return jnp.sum(jnp.sum(x, axis=1, keepdims=True), axis=0, keepdims=True)


def _route(logits_t, bias_col, cnt_col, cap):
    t = logits_t.shape[1]
    gsz = N_EXPERTS // N_GROUPS
    scores = jax.nn.sigmoid(logits_t)
    sel = (scores + bias_col).reshape(N_GROUPS, gsz, t)
    scores = scores.reshape(N_GROUPS, gsz, t)
    neg = jnp.float32(-jnp.inf)
    ie = lax.broadcasted_iota(I32, (N_GROUPS, gsz, t), 1)
    ig = lax.broadcasted_iota(I32, (N_GROUPS, gsz, t), 0)
    m1 = jnp.max(sel, axis=1, keepdims=True)
    first = jnp.min(jnp.where(sel == m1, ie, gsz), axis=1, keepdims=True)
    m2 = jnp.max(jnp.where(ie == first, neg, sel), axis=1, keepdims=True)
    grp = m1 + m2
    igk = lax.broadcasted_iota(I32, (N_GROUPS, 1, t), 0)
    g_sel = jnp.zeros((N_GROUPS, 1, t), jnp.bool_)
    cur = grp
    for _ in range(TOPK_GROUPS):
        m = jnp.max(cur, axis=0, keepdims=True)
        pick = igk == jnp.min(jnp.where(cur == m, igk, N_GROUPS), axis=0, keepdims=True)
        g_sel = g_sel | pick
        cur = jnp.where(pick, neg, cur)
    cur = jnp.where(g_sel, sel, neg)
    idx = ig * gsz + ie
    e_sel = jnp.zeros((N_GROUPS, gsz, t), jnp.bool_)
    picks = []
    for _ in range(TOP_K):
        m = jnp.max(jnp.max(cur, axis=1, keepdims=True), axis=0, keepdims=True)
        cand = jnp.where(cur == m, idx, N_EXPERTS)
        pick = idx == jnp.min(jnp.min(cand, axis=1, keepdims=True), axis=0, keepdims=True)
        picks.append(pick)
        e_sel = e_sel | pick
        cur = jnp.where(pick, neg, cur)
    w = jnp.where(e_sel, scores, 0.0)
    gate = w / _sum01(w) * ROUTED_SCALE
    sel_f = e_sel.astype(F32).reshape(N_EXPERTS, t)
    earlier = (lax.broadcasted_iota(I32, (t, t), 0) < lax.broadcasted_iota(I32, (t, t), 1))
    rank = _dot(sel_f.astype(BF16), earlier.astype(BF16))
    base = lax.broadcasted_iota(I32, (N_EXPERTS, 1), 0).astype(F32) * float(cap) + cnt_col
    slot = (base + rank).reshape(N_GROUPS, gsz, t)
    slot8 = jnp.concatenate([_sum01(jnp.where(pk, slot, 0.0)).reshape(1, t) for pk in picks],
                            axis=0).astype(I32)
    w8 = jnp.concatenate([_sum01(jnp.where(pk, gate, 0.0)).reshape(1, t) for pk in picks], axis=0)
    return slot8, w8, cnt_col + jnp.sum(sel_f, axis=1, keepdims=True)


def _outproj_kernel(og_ref, om_ref, x_ref, mod_ref, wtop_ref, wbot_ref, g2_ref, wr_ref, br_ref,
                    cnt0_ref, x1_ref, h2pa_ref, h2pb_ref, slot_ref, w8_ref, cnt_ref, cnt_scr,
                    *, cap):
    @pl.when((pl.program_id(0) == 0) & (pl.program_id(1) == 0))
    def _():
        cnt_scr[...] = cnt0_ref[...]

    mod = mod_ref[0]
    mix = _dot(og_ref[0], wtop_ref[...]) + _dot(om_ref[0], wbot_ref[...])
    x1 = x_ref[0] + mod[2:3] * mix
    x1_ref[0] = x1
    h2 = _rms(x1, D_MODEL) * g2_ref[...] * (1.0 + mod[4:5]) + mod[3:4]
    h2pa_ref[0], h2pb_ref[0] = _pack_rows(h2)
    logits_t = _dot_nt(wr_ref[...], h2, precision=HIGHEST)
    slot8, w8, cnt = _route(logits_t, br_ref[...], cnt_scr[:, 0:1], cap)
    slot_ref[...] = slot8
    t = w8.shape[1]
    w8_ref[0] = jnp.concatenate([w8, jnp.zeros((LANES - TOP_K, t), F32)], axis=0).T
    cnt_scr[...] = jnp.broadcast_to(cnt, cnt_scr.shape)
    cnt_ref[...] = cnt_scr[...]


def _outproj(og, om, x, mod, per_batch_mod, p, cnt0, cap, tm):
    b, l, d = x.shape
    nt = l // tm
    mod_map = (lambda bi, i: (bi, 0, 0)) if per_batch_mod else (lambda bi, i: (0, 0, 0))
    tok = lambda w: pl.BlockSpec((1, tm, w), lambda bi, i: (bi, i, 0))
    full = lambda a: pl.BlockSpec(a.shape, lambda bi, i: (0, 0))
    weights = [p['w_out_top'], p['w_out_bot'], p['g_norm2'], p['w_router_t'], p['b_router_col']]
    return pl.pallas_call(
        functools.partial(_outproj_kernel, cap=cap),
        grid=(b, nt),
        in_specs=[tok(GLA_VAL_W), tok(MLA_W), tok(d), pl.BlockSpec((1, 6, d), mod_map)]
                 + [full(w) for w in weights] + [full(cnt0)],
        out_specs=[tok(d), tok(PACK_W), tok(PACK_W),
                   pl.BlockSpec((TOP_K, tm), lambda bi, i: (0, bi * nt + i)),
                   tok(LANES), full(cnt0)],
        out_shape=[jax.ShapeDtypeStruct((b, l, d), F32), jax.ShapeDtypeStruct((b, l, PACK_W), U32),
                   jax.ShapeDtypeStruct((b, l, PACK_W), U32),
                   jax.ShapeDtypeStruct((TOP_K, b * l), I32),
                   jax.ShapeDtypeStruct((b, l, LANES), F32),
                   jax.ShapeDtypeStruct(cnt0.shape, F32)],
        scratch_shapes=[pltpu.VMEM(cnt0.shape, F32)],
        compiler_params=pltpu.CompilerParams(
            dimension_semantics=("arbitrary", "arbitrary"), vmem_limit_bytes=VMEM_LIMIT),
        name="outproj",
    )(og, om, x, mod, *weights, cnt0)


def _sc_mesh():
    return plsc.VectorSubcoreMesh(core_axis_name="c", subcore_axis_name="s")


def _sc_scatter_rows(src, slot8, n_rows_out):
    t, d = src.shape
    nk = slot8.shape[0]

    @functools.partial(pl.kernel, out_type=jax.ShapeDtypeStruct((n_rows_out, d), src.dtype),
                       mesh=_sc_mesh(), scratch_types=[], name="dispatch")
    def run(src_hbm, slot_hbm, out_hbm):
        def body(x_vmem, i_vmem):
            for k in range(nk):
                pltpu.sync_copy(x_vmem, out_hbm.at[i_vmem.at[k]])

        pltpu.emit_pipeline(
            body, grid=(t // SC_WINDOW,),
            in_specs=[pl.BlockSpec((SC_WINDOW, d), lambda i: (i, 0)),
                      pl.BlockSpec((nk, SC_WINDOW), lambda i: (0, i))],
            out_specs=[], core_axis_name=("c", "s"),
            dimension_semantics=(pltpu.PARALLEL,))(src_hbm, slot_hbm)

    return run(src, slot8)


def _sc_gather_rows(table, slots):
    n = slots.shape[1]
    d = table.shape[1]

    @functools.partial(pl.kernel, out_type=jax.ShapeDtypeStruct((n, d), table.dtype),
                       mesh=_sc_mesh(), scratch_types=[], name="collect")
    def run(tab_hbm, slot_hbm, out_hbm):
        def body(i_vmem, o_vmem):
            pltpu.sync_copy(tab_hbm.at[i_vmem.at[0]], o_vmem)

        pltpu.emit_pipeline(
            body, grid=(n // SC_WINDOW,),
            in_specs=[pl.BlockSpec((1, SC_WINDOW), lambda i: (0, i))],
            out_specs=[pl.BlockSpec((SC_WINDOW, d), lambda i: (i, 0))],
            core_axis_name=("c", "s"),
            dimension_semantics=(pltpu.PARALLEL,))(slot_hbm, out_hbm)

    return run(table, slots)


def _ffn_kernel(blk_ref, exp_ref, nt_ref, xsa_ref, xsb_ref, wg_ref, wu_ref, wd_ref,
                ysa_ref, ysb_ref, wg_b, wu_b, wd_b):
    j = pl.program_id(0)

    @pl.when(j < nt_ref[0])
    def _():
        @pl.when((j == 0) | (exp_ref[j] != exp_ref[jnp.maximum(j - 1, 0)]))
        def _():
            wg_b[...] = wg_ref[0].astype(BF16)
            wu_b[...] = wu_ref[0].astype(BF16)
            wd_b[...] = wd_ref[0].astype(BF16)

        x = _unpack_rows([xsa_ref[...], xsb_ref[...]]).astype(BF16)
        a = _silu(_dot(x, wg_b[...])) * _dot(x, wu_b[...])
        ysa_ref[...], ysb_ref[...] = _pack_rows(_dot(a.astype(BF16), wd_b[...]))


def _ffn(xs_parts, blk, exp, ntiles, w_gate, w_up, w_down):
    n_steps = blk.shape[0]
    d = w_gate.shape[1]
    row_map = lambda j, blk, exp, nt: (blk[j], 0)
    wmap = lambda j, blk, exp, nt: (exp[j], 0, 0)
    rows = pl.BlockSpec((FFN_TILE, PACK_W), row_map)
    return pl.pallas_call(
        _ffn_kernel,
        grid_spec=pltpu.PrefetchScalarGridSpec(
            num_scalar_prefetch=3, grid=(n_steps,),
            in_specs=[rows] * PACK_PARTS
                     + [pl.BlockSpec((1, d, D_EXPERT), wmap), pl.BlockSpec((1, d, D_EXPERT), wmap),
                        pl.BlockSpec((1, D_EXPERT, d), wmap)],
            out_specs=[rows] * PACK_PARTS,
            scratch_shapes=[pltpu.VMEM((d, D_EXPERT), BF16), pltpu.VMEM((d, D_EXPERT), BF16),
                            pltpu.VMEM((D_EXPERT, d), BF16)]),
        out_shape=[jax.ShapeDtypeStruct(xs_parts[0].shape, U32)] * PACK_PARTS,
        compiler_params=pltpu.CompilerParams(
            dimension_semantics=("arbitrary",), vmem_limit_bytes=VMEM_LIMIT),
        name="ffn",
    )(blk, exp, ntiles, *xs_parts, w_gate, w_up, w_down)


def _tile_table(counts, cap, n_steps):
    nt_e = (counts + FFN_TILE - 1) // FFN_TILE
    cum = jnp.cumsum(nt_e)
    total = cum[-1]
    j = jnp.minimum(jnp.arange(n_steps, dtype=I32), total - 1)
    e = jnp.searchsorted(cum, j, side='right').astype(I32)
    blk = e * (cap // FFN_TILE) + (j - (cum[e] - nt_e[e]))
    return blk.astype(I32), e, total.reshape(1).astype(I32)


def _combine_kernel(x1_ref, h2pa_ref, h2pb_ref, yga_ref, ygb_ref, w8_ref, mod_ref,
                    wsg_ref, wsu_ref, wsd_ref, o_ref):
    h2 = _unpack_rows([h2pa_ref[0], h2pb_ref[0]]).astype(BF16)
    a = _silu(_dot(h2, wsg_ref[...])) * _dot(h2, wsu_ref[...])
    acc = _dot(a.astype(BF16), wsd_ref[...])
    w8 = w8_ref[0]
    for k in range(TOP_K):
        acc += w8[:, k:k + 1] * _unpack_rows([yga_ref[k], ygb_ref[k]])
    o_ref[0] = x1_ref[0] + mod_ref[0][5:6] * acc


def _combine(x1, h2p_parts, yg_parts, tile0, w8, mod, per_batch_mod, p, tm):
    b, l, d = x1.shape
    nt = l // tm
    mod_map = (lambda bi, i: (bi, 0, 0)) if per_batch_mod else (lambda bi, i: (0, 0, 0))
    tok = lambda w: pl.BlockSpec((1, tm, w), lambda bi, i: (bi, i, 0))
    full = lambda a: pl.BlockSpec(a.shape, lambda bi, i: (0, 0))
    weights = [p['w_sh_gate'], p['w_sh_up'], p['w_sh_down']]
    gathered = pl.BlockSpec((TOP_K, tm, PACK_W), lambda bi, i: (0, tile0 + bi * nt + i, 0))
    return pl.pallas_call(
        _combine_kernel,
        grid=(b, nt),
        in_specs=[tok(d)] + [tok(PACK_W)] * PACK_PARTS + [gathered] * PACK_PARTS
                 + [tok(LANES), pl.BlockSpec((1, 6, d), mod_map)] + [full(w) for w in weights],
        out_specs=tok(d),
        out_shape=jax.ShapeDtypeStruct((b, l, d), F32),
        compiler_params=pltpu.CompilerParams(
            dimension_semantics=("parallel", "parallel"), vmem_limit_bytes=VMEM_LIMIT),
        name="combine",
    )(x1, *h2p_parts, *yg_parts, w8, mod, *weights)


def _pad_heads(w, parts):
    k = w.shape[0]
    per = w.shape[1] // MLA_HEADS
    w = w.reshape(k, MLA_HEADS, per)[:, :, parts[0]:parts[1]]
    w = jnp.pad(w, ((0, 0), (0, 0), (0, HEAD_W - (parts[1] - parts[0]))))
    return w.reshape(k, MLA_HEADS * HEAD_W)


def _prep_params(l, g_norm1, g_norm2, w_in, w_gk_fwd, b_gk_fwd, w_gk_bwd, b_gk_bwd, g_gla_out,
                 g_q_lora, w_uq, g_kv_lora, w_ukv, g_qk_q, g_qk_k, w_out, w_router, b_router,
                 w_sh_gate, w_sh_up, w_sh_down):
    w = w_in[l]
    d = w.shape[0]
    o_lrf = COL_CQ
    o_lrb = o_lrf + GLA_GATE_RANK
    o_cq = o_lrb + GLA_GATE_RANK
    o_ckv = o_cq + Q_LORA
    o_kr = o_ckv + KV_LORA
    w_in_p = jnp.concatenate([
        w[:, 0:COL_CQ], w[:, o_cq:o_ckv], w[:, o_ckv:o_kr], jnp.zeros((d, KR_LANE0), w.dtype),
        w[:, o_kr:o_kr + ROPE_DIM], w[:, o_lrf:o_lrb], w[:, o_lrb:o_cq]], axis=1).astype(BF16)
    z = jnp.zeros((GLA_GATE_RANK, GLA_KEY_W), F32)
    w_gk_big = jnp.concatenate([
        jnp.zeros((LANES - 2 * GLA_GATE_RANK, 2 * GLA_KEY_W), F32),
        jnp.concatenate([w_gk_fwd[l], z], axis=1),
        jnp.concatenate([z, w_gk_bwd[l]], axis=1)], axis=0).astype(BF16)
    pad_gain = lambda g: jnp.pad(g, (0, HEAD_W - QK_DIM)).reshape(1, HEAD_W)
    w_ukv_h = w_ukv[l]
    return {
        'g_norm1': g_norm1[l].reshape(1, d), 'g_norm2': g_norm2[l].reshape(1, d),
        'w_in_p': w_in_p, 'w_gk_big': w_gk_big,
        'b_gk': jnp.concatenate([b_gk_fwd[l], b_gk_bwd[l]]).reshape(1, 2 * GLA_KEY_W),
        'g_gla_out': g_gla_out[l].reshape(1, GLA_DV),
        'g_q_lora': g_q_lora[l].reshape(1, Q_LORA),
        'w_uq_p': _pad_heads(w_uq[l], (0, QK_DIM)).astype(BF16),
        'g_kv_lora': g_kv_lora[l].reshape(1, KV_LORA),
        'w_ukv_p': jnp.concatenate([_pad_heads(w_ukv_h, (0, NOPE_DIM)),
                                    _pad_heads(w_ukv_h, (NOPE_DIM, NOPE_DIM + V_DIM))],
                                   axis=1).astype(BF16),
        'gq': pad_gain(g_qk_q[l]), 'gk': pad_gain(g_qk_k[l]),
        'w_out_top': w_out[l][:GLA_VAL_W].astype(BF16),
        'w_out_bot': w_out[l][GLA_VAL_W:].astype(BF16),
        'w_router_t': w_router[l].T, 'b_router_col': b_router[l].reshape(N_EXPERTS, 1),
        'w_sh_gate': w_sh_gate[l].astype(BF16), 'w_sh_up': w_sh_up[l].astype(BF16),
        'w_sh_down': w_sh_down[l].astype(BF16),
    }


def _rope_tables(length):
    pos = jnp.arange(length)
    r = (pos // GRID_W).astype(F32)
    c = (pos % GRID_W).astype(F32)
    half = ROPE_DIM // 4
    inv_freq = ROPE_THETA ** (-jnp.arange(half, dtype=F32) / half)
    ang_r = r[:, None] * inv_freq[None, :]
    ang_c = c[:, None] * inv_freq[None, :]
    zeros = lambda w: jnp.zeros((length, w), F32)
    ones = lambda w: jnp.ones((length, w), F32)
    tail_w = HEAD_W - ROPE_LANE0 - ROPE_DIM
    cos = jnp.concatenate([ones(ROPE_LANE0), jnp.cos(ang_r), jnp.cos(ang_r), jnp.cos(ang_c),
                           jnp.cos(ang_c), ones(tail_w)], axis=1)
    s1 = jnp.concatenate([zeros(ROPE_LANE0), -jnp.sin(ang_r), zeros(half), -jnp.sin(ang_c),
                          zeros(half), zeros(tail_w)], axis=1)
    s2 = jnp.concatenate([zeros(ROPE_LANE0), zeros(half), jnp.sin(ang_r), zeros(half),
                          jnp.sin(ang_c), zeros(tail_w)], axis=1)
    return cos, s1, s2


def _identity_tables(rows):
    return (jnp.ones((rows, LANES), F32), jnp.zeros((rows, LANES), F32),
            jnp.zeros((rows, LANES), F32))


def _mix_and_route(x_tok, seq_shape, mod, per_batch_mod, p, s0f, s0b, rope_tabs, ctx_kv,
                   cnt0, cap, tm, tq):
    bt, lt, d = x_tok.shape
    b, l = seq_shape
    (q, k, v, g, lgf, lgb, qm, km, vm, ckvn, kr) = _inproj(x_tok, mod, per_batch_mod, p,
                                                            rope_tabs, tm)
    seq = lambda a: a.reshape(b, l, a.shape[-1])
    og, sf, sb = _gla(seq(q), seq(k), seq(v), seq(g), seq(lgf), seq(lgb), s0f, s0b,
                      p['g_gla_out'])
    km, vm = seq(km), seq(vm)
    if ctx_kv is not None:
        km = jnp.concatenate([km, ctx_kv[0]], axis=1)
        vm = jnp.concatenate([vm, ctx_kv[1]], axis=1)
    om = _attn(seq(qm), km, vm, tq)
    tokv = lambda a: a.reshape(bt, lt, a.shape[-1])
    routed = _outproj(tokv(og), tokv(om), x_tok, mod, per_batch_mod, p, cnt0, cap, tm)
    return routed, sf, sb, ckvn, kr


def kernel(x_prompt, x_sample, c, state_gla_fwd, state_gla_bwd, cache_mla_ckv, cache_mla_krope,
           c_ctx, w_ada, b_ada, g_norm1, g_norm2, w_in, w_gk_fwd, b_gk_fwd, w_gk_bwd, b_gk_bwd,
           g_gla_out, g_q_lora, w_uq, g_kv_lora, w_ukv, g_qk_q, g_qk_k, w_out,
           w_router, b_router, w_exp_gate, w_exp_up, w_exp_down, w_sh_gate, w_sh_up, w_sh_down):
    bp, lp, d = x_prompt.shape
    bs, ls, _ = x_sample.shape
    depth = w_ada.shape[0]
    n_ctx, n_lat = bp * lp, bs * ls
    cap = n_ctx + n_lat
    n_steps = (cap * TOP_K) // FFN_TILE + N_EXPERTS
    xp = x_prompt.reshape(1, n_ctx, d)
    xs = x_sample
    new_f, new_b, new_ckv, new_kr = [], [], [], []
    lat_tabs = _rope_tables(ls)
    tm = 256
    ctx_tabs = _identity_tables(tm)
    cvecs = jnp.concatenate([c_ctx[None], c, jnp.zeros((8 - 1 - bs, d), F32)], axis=0)
    for l in range(depth):
        p = _prep_params(l, g_norm1, g_norm2, w_in, w_gk_fwd, b_gk_fwd, w_gk_bwd, b_gk_bwd,
                         g_gla_out, g_q_lora, w_uq, g_kv_lora, w_ukv, g_qk_q, g_qk_k, w_out,
                         w_router, b_router, w_sh_gate, w_sh_up, w_sh_down)
        mod = _ada(cvecs, w_ada[l], b_ada[l]).reshape(8, 6, d)
        mod_p, mod_s = mod[0:1], mod[1:1 + bs]
        zeros = jnp.zeros((bp, GLA_KEY_W, GLA_DV), F32)
        cnt0 = jnp.zeros((N_EXPERTS, LANES), F32)
        (x1p, h2a_p, h2b_p, slot_p, w8_p, cnt1), sf, sb, ckvn, kr = _mix_and_route(
            xp, (bp, lp), mod_p, False, p, zeros, zeros, ctx_tabs, None, cnt0, cap, tm, lp)
        new_f.append(sf.reshape(bp, GLA_HEADS, GLA_DK, GLA_DV))
        new_b.append(sb.reshape(bp, GLA_HEADS, GLA_DK, GLA_DV))
        new_ckv.append(ckvn.reshape(bp, lp, KV_LORA))
        new_kr.append(kr.reshape(bp, lp, LANES)[:, :, KR_LANE0:KR_LANE0 + ROPE_DIM])
        kr_cache = jnp.pad(cache_mla_krope[:, l],
                           ((0, 0), (0, 0), (ROPE_LANE0, LANES - ROPE_LANE0 - ROPE_DIM)))
        ctx_kv = _cache_kv(cache_mla_ckv[:, l], kr_cache, p)
        s0f = state_gla_fwd[:, l].reshape(bs, GLA_KEY_W, GLA_DV)
        s0b = state_gla_bwd[:, l].reshape(bs, GLA_KEY_W, GLA_DV)
        (x1s, h2a_s, h2b_s, slot_s, w8_s, cnt2), _, _, _, _ = _mix_and_route(
            xs, (bs, ls), mod_s, True, p, s0f, s0b, lat_tabs, ctx_kv, cnt1, cap, tm, 256)
        slot8 = jnp.concatenate([slot_p, slot_s], axis=1)
        slots_flat = slot8.reshape(1, TOP_K * cap)
        blk, exp, ntiles = _tile_table(cnt2[:, 0].astype(I32), cap, n_steps)
        x_sorted = [
            _sc_scatter_rows(jnp.concatenate([hp.reshape(n_ctx, PACK_W),
                                              hs.reshape(n_lat, PACK_W)], axis=0),
                             slot8, N_EXPERTS * cap)
            for hp, hs in ((h2a_p, h2a_s), (h2b_p, h2b_s))]
        y_sorted = _ffn(x_sorted, blk, exp, ntiles, w_exp_gate[l], w_exp_up[l], w_exp_down[l])
        yg = [_sc_gather_rows(y, slots_flat).reshape(TOP_K, cap, PACK_W) for y in y_sorted]
        xp = _combine(x1p, (h2a_p, h2b_p), yg, 0, w8_p, mod_p, False, p, tm)
        xs = _combine(x1s, (h2a_s, h2b_s), yg, n_ctx // tm, w8_s, mod_s, True, p, tm)
    return (xp.reshape(bp, lp, d), xs, jnp.stack(new_f, axis=1), jnp.stack(new_b, axis=1),
            jnp.stack(new_ckv, axis=1), jnp.stack(new_kr, axis=1))
```

```python
import functools

import jax
import jax.numpy as jnp
from jax import lax
from jax.experimental import pallas as pl
from jax.experimental.pallas import tpu as pltpu
from jax.experimental.pallas import tpu_sc as plsc

F32 = jnp.float32
BF16 = jnp.bfloat16
I32 = jnp.int32
U32 = jnp.uint32

D_MODEL = 1024
EPS = 1e-6
GRID_W = 64
GLA_HEADS = 4
GLA_DK = 64
GLA_DV = 128
GLA_GATE_RANK = 16
GLA_GATE_NORM = 16.0
GLA_KEY_W = GLA_HEADS * GLA_DK
GLA_VAL_W = GLA_HEADS * GLA_DV
MLA_HEADS = 4
Q_LORA = 256
KV_LORA = 128
NOPE_DIM = 64
ROPE_DIM = 32
V_DIM = 128
QK_DIM = NOPE_DIM + ROPE_DIM
ROPE_THETA = 10000.0
N_EXPERTS = 64
TOP_K = 8
N_GROUPS = 8
TOPK_GROUPS = 4
D_EXPERT = 256
ROUTED_SCALE = 2.5

LANES = 128
HEAD_W = LANES
MLA_W = MLA_HEADS * HEAD_W
ROPE_LANE0 = NOPE_DIM
COL_V = 2 * GLA_KEY_W
COL_G = COL_V + GLA_VAL_W
COL_CQ = COL_G + GLA_VAL_W
COL_CKV = COL_CQ + Q_LORA
TAIL0 = COL_CKV + KV_LORA
IN_W = TAIL0 + LANES
KR_LANE0 = LANES - ROPE_DIM - 2 * GLA_GATE_RANK
GLA_BLOCK = 256
GLA_SUB = 64
FFN_TILE = 512
PACK_PARTS = 2
PACK_W = D_MODEL // (2 * PACK_PARTS)
SC_WINDOW = 128
VMEM_LIMIT = 56 * 1024 * 1024

HIGHEST = lax.Precision.HIGHEST


def _dot(a, b, precision=None):
    return jnp.dot(a, b, preferred_element_type=F32, precision=precision)


def _dot_nt(a, b, precision=None):
    return lax.dot_general(a, b, (((1,), (1,)), ((), ())), preferred_element_type=F32,
                           precision=precision)


def _rms(x, width):
    ss = jnp.sum(x * x, axis=-1, keepdims=True) * (1.0 / width)
    return x * lax.rsqrt(ss + EPS)


def _silu(x):
    return x * jax.nn.sigmoid(x)


def _log_sigmoid(x):
    return jnp.minimum(x, 0.0) - jnp.log1p(jnp.exp(-jnp.abs(x)))


def _pack_rows(x):
    parts = []
    for i in range(PACK_PARTS):
        c0 = i * 2 * PACK_W
        lo = lax.bitcast_convert_type(x[:, c0:c0 + PACK_W].astype(BF16).astype(F32), U32)
        hi = lax.bitcast_convert_type(
            x[:, c0 + PACK_W:c0 + 2 * PACK_W].astype(BF16).astype(F32), U32)
        parts.append(hi | (lo >> 16))
    return parts


def _unpack_rows(parts):
    cols = []
    for w in parts:
        cols.append(lax.bitcast_convert_type(w << 16, F32))
        cols.append(lax.bitcast_convert_type(w & jnp.uint32(0xFFFF0000), F32))
    return jnp.concatenate(cols, axis=1)


def _ada_kernel(c_ref, w_ref, b_ref, o_ref):
    o_ref[...] = _dot(_silu(c_ref[...]), w_ref[...], precision=HIGHEST) + b_ref[...]


def _ada(cvecs, w_ada, b_ada):
    n = w_ada.shape[1]
    tn = 768
    return pl.pallas_call(
        _ada_kernel,
        grid=(n // tn,),
        in_specs=[pl.BlockSpec((8, D_MODEL), lambda j: (0, 0)),
                  pl.BlockSpec((D_MODEL, tn), lambda j: (0, j)),
                  pl.BlockSpec((1, tn), lambda j: (0, j))],
        out_specs=pl.BlockSpec((8, tn), lambda j: (0, j)),
        out_shape=jax.ShapeDtypeStruct((8, n), F32),
        name="ada",
    )(cvecs, w_ada, b_ada.reshape(1, n))


def _rope(x, c, s1, s2):
    return x * c + pltpu.roll(x, LANES - 8, 1) * s1 + pltpu.roll(x, 8, 1) * s2


def _mla_kv(ckv_n, kr, w_ukv_ref, gk, rope_tabs, k_ref, v_ref):
    kv = _dot(ckv_n.astype(BF16), w_ukv_ref[...])
    for h in range(MLA_HEADS):
        k_h = kv[:, h * HEAD_W:(h + 1) * HEAD_W] + kr
        k_h = _rms(k_h, QK_DIM) * gk
        if rope_tabs is not None:
            k_h = _rope(k_h, *rope_tabs)
        k_ref[0, :, h * HEAD_W:(h + 1) * HEAD_W] = k_h.astype(BF16)
    v_ref[0] = kv[:, MLA_W:].astype(BF16)


def _lane_mask(lo, hi, rows):
    lane = lax.broadcasted_iota(I32, (rows, LANES), 1)
    return (lane >= lo) & (lane < hi)


def _inproj_kernel(x_ref, mod_ref, g1_ref, win_ref, wgk_ref, bgk_ref, gql_ref, wuq_ref,
                   gkv_ref, wukv_ref, gq_ref, gk_ref, rc_ref, rs1_ref, rs2_ref,
                   q_ref, k_ref, v_ref, g_ref, lgf_ref, lgb_ref,
                   qm_ref, km_ref, vm_ref, ckvn_ref, kr_ref):
    x = x_ref[0]
    mod = mod_ref[0]
    h = _rms(x, D_MODEL) * g1_ref[...] * (1.0 + mod[1:2]) + mod[0:1]
    y = _dot(h.astype(BF16), win_ref[...])
    q_ref[0] = (y[:, 0:GLA_KEY_W] * GLA_DK ** -0.5).astype(BF16)
    k_ref[0] = y[:, GLA_KEY_W:COL_V].astype(BF16)
    v_ref[0] = y[:, COL_V:COL_G].astype(BF16)
    g_ref[0] = y[:, COL_G:COL_CQ].astype(BF16)
    tail = y[:, TAIL0:IN_W]
    pre = _dot(tail.astype(BF16), wgk_ref[...]) + bgk_ref[...]
    logg = _log_sigmoid(pre) * (1.0 / GLA_GATE_NORM)
    lgf_ref[0] = logg[:, 0:GLA_KEY_W]
    lgb_ref[0] = logg[:, GLA_KEY_W:]
    rows = x.shape[0]
    tabs = (rc_ref[...], rs1_ref[...], rs2_ref[...])
    cq = _rms(y[:, COL_CQ:COL_CKV], Q_LORA) * gql_ref[...]
    qm = _dot(cq.astype(BF16), wuq_ref[...])
    gq = gq_ref[...]
    for hh in range(MLA_HEADS):
        q_h = _rms(qm[:, hh * HEAD_W:(hh + 1) * HEAD_W], QK_DIM) * gq
        q_h = _rope(q_h, *tabs) * QK_DIM ** -0.5
        qm_ref[0, :, hh * HEAD_W:(hh + 1) * HEAD_W] = q_h.astype(BF16)
    ckv_n = _rms(y[:, COL_CKV:TAIL0], KV_LORA) * gkv_ref[...]
    ckvn_ref[0] = ckv_n
    kr_ref[0] = tail
    kr = jnp.where(_lane_mask(KR_LANE0, KR_LANE0 + ROPE_DIM, rows), tail, 0.0)
    _mla_kv(ckv_n, kr, wukv_ref, gk_ref[...], tabs, km_ref, vm_ref)


def _inproj(x, mod, per_batch_mod, p, rope_tabs, tm):
    b, l, d = x.shape
    nt = l // tm
    tab_blocks = rope_tabs[0].shape[0] // tm
    mod_map = (lambda bi, i: (bi, 0, 0)) if per_batch_mod else (lambda bi, i: (0, 0, 0))
    tab_map = (lambda bi, i: (i, 0)) if tab_blocks > 1 else (lambda bi, i: (0, 0))
    const = lambda bi, i: (0, 0)
    tok = lambda w: pl.BlockSpec((1, tm, w), lambda bi, i: (bi, i, 0))
    full = lambda a: pl.BlockSpec(a.shape, const)
    weights = [p['g_norm1'], p['w_in_p'], p['w_gk_big'], p['b_gk'], p['g_q_lora'], p['w_uq_p'],
               p['g_kv_lora'], p['w_ukv_p'], p['gq'], p['gk']]
    outs = [(GLA_KEY_W, BF16), (GLA_KEY_W, BF16), (GLA_VAL_W, BF16), (GLA_VAL_W, BF16),
            (GLA_KEY_W, F32), (GLA_KEY_W, F32), (MLA_W, BF16), (MLA_W, BF16), (MLA_W, BF16),
            (KV_LORA, F32), (LANES, F32)]
    return pl.pallas_call(
        _inproj_kernel,
        grid=(b, nt),
        in_specs=[tok(d), pl.BlockSpec((1, 6, d), mod_map)] + [full(w) for w in weights]
                 + [pl.BlockSpec((tm, LANES), tab_map)] * 3,
        out_specs=[tok(w) for w, _ in outs],
        out_shape=[jax.ShapeDtypeStruct((b, l, w), dt) for w, dt in outs],
        compiler_params=pltpu.CompilerParams(
            dimension_semantics=("parallel", "parallel"), vmem_limit_bytes=VMEM_LIMIT),
        name="inproj",
    )(x, mod, *weights, *rope_tabs)


def _cache_kv_kernel(ckv_ref, kr_ref, wukv_ref, gk_ref, k_ref, v_ref):
    _mla_kv(ckv_ref[0], kr_ref[0], wukv_ref, gk_ref[...], None, k_ref, v_ref)


def _cache_kv(ckv, kr128, p):
    b, l, _ = ckv.shape
    tok = lambda w: pl.BlockSpec((1, l, w), lambda bi: (bi, 0, 0))
    full = lambda a: pl.BlockSpec(a.shape, lambda bi: (0, 0))
    return pl.pallas_call(
        _cache_kv_kernel,
        grid=(b,),
        in_specs=[tok(KV_LORA), tok(LANES), full(p['w_ukv_p']), full(p['gk'])],
        out_specs=[tok(MLA_W), tok(MLA_W)],
        out_shape=[jax.ShapeDtypeStruct((b, l, MLA_W), BF16)] * 2,
        name="cache_kv",
    )(ckv, kr128, p['w_ukv_p'], p['gk'])


def _split3(x):
    hi = x.astype(BF16)
    r1 = x - hi.astype(F32)
    mid = r1.astype(BF16)
    lo = (r1 - mid.astype(F32)).astype(BF16)
    return hi, mid, lo


def _gla_block(q, k, v, lg, s, fwd):
    n = GLA_BLOCK
    nc = n // GLA_SUB
    row = lax.broadcasted_iota(I32, (n, n), 0)
    col = lax.broadcasted_iota(I32, (n, n), 1)
    same = (row // GLA_SUB) == (col // GLA_SUB)
    causal = same & ((col <= row) if fwd else (col >= row))
    tri = causal.astype(BF16)
    hi, mid, lo = _split3(lg)
    cum = _dot(tri, hi) + _dot(tri, mid) + _dot(tri, lo)
    tot_rows, mid_rows = [], []
    for c in range(nc):
        r_tot = c * GLA_SUB + (GLA_SUB - 1 if fwd else 0)
        r_mid = c * GLA_SUB + GLA_SUB // 2
        tot_rows.append(jnp.broadcast_to(cum[r_tot:r_tot + 1], (GLA_SUB, GLA_KEY_W)))
        mid_rows.append(jnp.broadcast_to(cum[r_mid:r_mid + 1], (GLA_SUB, GLA_KEY_W)))
    tot_rows = jnp.concatenate(tot_rows, axis=0)
    mid_rows = jnp.concatenate(mid_rows, axis=0)
    rel = cum - mid_rows
    qi = q * jnp.exp(rel)
    ki = (k * jnp.exp(-rel)).astype(BF16)
    q_in = (q * jnp.exp(cum)).astype(BF16)
    k_up = k * jnp.exp(tot_rows - cum)
    dec = jnp.exp(tot_rows)
    k_up_t = k_up.T.astype(BF16)
    dec_t = dec.T
    lane_head = lax.broadcasted_iota(I32, (n, GLA_KEY_W), 1) // GLA_DK
    srow_head = lax.broadcasted_iota(I32, (GLA_KEY_W, GLA_DV), 0) // GLA_DK
    vrow_chunk = lax.broadcasted_iota(I32, (n, GLA_VAL_W), 0) // GLA_SUB
    o_heads = []
    for h in range(GLA_HEADS):
        a = _dot_nt(jnp.where(lane_head == h, qi, 0.0).astype(BF16), ki)
        a = jnp.where(causal, a, 0.0).astype(BF16)
        o_heads.append(_dot(a, v[:, h * GLA_DV:(h + 1) * GLA_DV]))
    o = jnp.concatenate(o_heads, axis=1)
    o_inter = [None] * nc
    for c in (range(nc) if fwd else range(nc - 1, -1, -1)):
        s_bd = jnp.concatenate(
            [jnp.where(srow_head == h, s, 0.0).astype(BF16) for h in range(GLA_HEADS)], axis=1)
        o_inter[c] = _dot(q_in[c * GLA_SUB:(c + 1) * GLA_SUB], s_bd)
        v_c = jnp.where(vrow_chunk == c, v, jnp.zeros_like(v))
        u = jnp.concatenate(
            [_dot(k_up_t[h * GLA_DK:(h + 1) * GLA_DK], v_c[:, h * GLA_DV:(h + 1) * GLA_DV])
             for h in range(GLA_HEADS)], axis=0)
        s = dec_t[:, c * GLA_SUB:c * GLA_SUB + 1] * s + u
    return o + jnp.concatenate(o_inter, axis=0), s


def _gla_kernel(q_ref, k_ref, v_ref, g_ref, lgf_ref, lgb_ref, s0f_ref, s0b_ref, gout_ref,
                o_ref, sf_ref, sb_ref, acc_ref, st_ref, *, nblk):
    st_ref[0] = s0f_ref[0]
    st_ref[1] = s0b_ref[0]

    def load(blk):
        r = pl.ds(pl.multiple_of(blk * GLA_BLOCK, GLA_BLOCK), GLA_BLOCK)
        return r, q_ref[0, r, :].astype(F32), k_ref[0, r, :].astype(F32), v_ref[0, r, :]

    def fwd_step(blk, carry):
        r, q, k, v = load(blk)
        o, s = _gla_block(q, k, v, lgf_ref[0, r, :], st_ref[0], True)
        acc_ref[r, :] = o
        st_ref[0] = s
        return carry

    def bwd_step(i, carry):
        r, q, k, v = load(nblk - 1 - i)
        o, s = _gla_block(q, k, v, lgb_ref[0, r, :], st_ref[1], False)
        acc_ref[r, :] += o
        st_ref[1] = s
        return carry

    lax.fori_loop(0, nblk, fwd_step, 0)
    lax.fori_loop(0, nblk, bwd_step, 0)
    sf_ref[0] = st_ref[0]
    sb_ref[0] = st_ref[1]

    def fin_step(blk, carry):
        r = pl.ds(pl.multiple_of(blk * GLA_BLOCK, GLA_BLOCK), GLA_BLOCK)
        o = acc_ref[r, :]
        gate = _silu(g_ref[0, r, :].astype(F32))
        for h in range(GLA_HEADS):
            sl = slice(h * GLA_DV, (h + 1) * GLA_DV)
            o_ref[0, r, sl] = (_rms(o[:, sl], GLA_DV) * gout_ref[...] * gate[:, sl]).astype(BF16)
        return carry

    lax.fori_loop(0, nblk, fin_step, 0)


def _gla(q, k, v, g, lgf, lgb, s0f, s0b, g_out):
    b, l, _ = q.shape
    seq = lambda w: pl.BlockSpec((1, l, w), lambda bi: (bi, 0, 0))
    st = pl.BlockSpec((1, GLA_KEY_W, GLA_DV), lambda bi: (bi, 0, 0))
    return pl.pallas_call(
        functools.partial(_gla_kernel, nblk=l // GLA_BLOCK),
        grid=(b,),
        in_specs=[seq(GLA_KEY_W), seq(GLA_KEY_W), seq(GLA_VAL_W), seq(GLA_VAL_W),
                  seq(GLA_KEY_W), seq(GLA_KEY_W), st, st,
                  pl.BlockSpec((1, GLA_DV), lambda bi: (0, 0))],
        out_specs=[seq(GLA_VAL_W), st, st],
        out_shape=[jax.ShapeDtypeStruct((b, l, GLA_VAL_W), BF16),
                   jax.ShapeDtypeStruct((b, GLA_KEY_W, GLA_DV), F32),
                   jax.ShapeDtypeStruct((b, GLA_KEY_W, GLA_DV), F32)],
        scratch_shapes=[pltpu.VMEM((l, GLA_VAL_W), F32),
                        pltpu.VMEM((2, GLA_KEY_W, GLA_DV), F32)],
        compiler_params=pltpu.CompilerParams(
            dimension_semantics=("parallel",), vmem_limit_bytes=VMEM_LIMIT),
        name="gla",
    )(q, k, v, g, lgf, lgb, s0f, s0b, g_out)


def _attn_kernel(q_ref, k_ref, v_ref, o_ref):
    for h in range(MLA_HEADS):
        sl = slice(h * HEAD_W, (h + 1) * HEAD_W)
        s = _dot_nt(q_ref[0, :, sl], k_ref[0, :, sl])
        m = jnp.max(s, axis=-1, keepdims=True)
        p = jnp.exp(s - m)
        den = jnp.sum(p, axis=-1, keepdims=True)
        o = _dot(p.astype(BF16), v_ref[0, :, sl])
        o_ref[0, :, sl] = (o / den).astype(BF16)


def _attn(q, k, v, tq):
    b, l, _ = q.shape
    lk = k.shape[1]
    return pl.pallas_call(
        _attn_kernel,
        grid=(b, l // tq),
        in_specs=[pl.BlockSpec((1, tq, MLA_W), lambda bi, i: (bi, i, 0)),
                  pl.BlockSpec((1, lk, MLA_W), lambda bi, i: (bi, 0, 0)),
                  pl.BlockSpec((1, lk, MLA_W), lambda bi, i: (bi, 0, 0))],
        out_specs=pl.BlockSpec((1, tq, MLA_W), lambda bi, i: (bi, i, 0)),
        out_shape=jax.ShapeDtypeStruct((b, l, MLA_W), BF16),
        compiler_params=pltpu.CompilerParams(
            dimension_semantics=("parallel", "parallel"), vmem_limit_bytes=VMEM_LIMIT),
        name="attn",
    )(q, k, v)


def _sum01(x):
    return jnp.sum(jnp.sum(x, axis=1, keepdims=True), axis=0, keepdims=True)


def _route(logits_t, bias_col, cnt_col, cap):
    t = logits_t.shape[1]
    gsz = N_EXPERTS // N_GROUPS
    scores = jax.nn.sigmoid(logits_t)
    sel = (scores + bias_col).reshape(N_GROUPS, gsz, t)
    scores = scores.reshape(N_GROUPS, gsz, t)
    neg = jnp.float32(-jnp.inf)
    ie = lax.broadcasted_iota(I32, (N_GROUPS, gsz, t), 1)
    ig = lax.broadcasted_iota(I32, (N_GROUPS, gsz, t), 0)
    m1 = jnp.max(sel, axis=1, keepdims=True)
    first = jnp.min(jnp.where(sel == m1, ie, gsz), axis=1, keepdims=True)
    m2 = jnp.max(jnp.where(ie == first, neg, sel), axis=1, keepdims=True)
    grp = m1 + m2
    igk = lax.broadcasted_iota(I32, (N_GROUPS, 1, t), 0)
    g_sel = jnp.zeros((N_GROUPS, 1, t), jnp.bool_)
    cur = grp
    for _ in range(TOPK_GROUPS):
        m = jnp.max(cur, axis=0, keepdims=True)
        pick = igk == jnp.min(jnp.where(cur == m, igk, N_GROUPS), axis=0, keepdims=True)
        g_sel = g_sel | pick
        cur = jnp.where(pick, neg, cur)
    cur = jnp.where(g_sel, sel, neg)
    idx = ig * gsz + ie
    e_sel = jnp.zeros((N_GROUPS, gsz, t), jnp.bool_)
    picks = []
    for _ in range(TOP_K):
        m = jnp.max(jnp.max(cur, axis=1, keepdims=True), axis=0, keepdims=True)
        cand = jnp.where(cur == m, idx, N_EXPERTS)
        pick = idx == jnp.min(jnp.min(cand, axis=1, keepdims=True), axis=0, keepdims=True)
        picks.append(pick)
        e_sel = e_sel | pick
        cur = jnp.where(pick, neg, cur)
    w = jnp.where(e_sel, scores, 0.0)
    gate = w / _sum01(w) * ROUTED_SCALE
    sel_f = e_sel.astype(F32).reshape(N_EXPERTS, t)
    earlier = (lax.broadcasted_iota(I32, (t, t), 0) < lax.broadcasted_iota(I32, (t, t), 1))
    rank = _dot(sel_f.astype(BF16), earlier.astype(BF16))
    base = lax.broadcasted_iota(I32, (N_EXPERTS, 1), 0).astype(F32) * float(cap) + cnt_col
    slot = (base + rank).reshape(N_GROUPS, gsz, t)
    slot8 = jnp.concatenate([_sum01(jnp.where(pk, slot, 0.0)).reshape(1, t) for pk in picks],
                            axis=0).astype(I32)
    w8 = jnp.concatenate([_sum01(jnp.where(pk, gate, 0.0)).reshape(1, t) for pk in picks], axis=0)
    return slot8, w8, cnt_col + jnp.sum(sel_f, axis=1, keepdims=True)


def _outproj_kernel(og_ref, om_ref, x_ref, mod_ref, wtop_ref, wbot_ref, g2_ref, wr_ref, br_ref,
                    cnt0_ref, x1_ref, h2pa_ref, h2pb_ref, slot_ref, w8_ref, cnt_ref, cnt_scr,
                    *, cap):
    @pl.when((pl.program_id(0) == 0) & (pl.program_id(1) == 0))
    def _():
        cnt_scr[...] = cnt0_ref[...]

    mod = mod_ref[0]
    mix = _dot(og_ref[0], wtop_ref[...]) + _dot(om_ref[0], wbot_ref[...])
    x1 = x_ref[0] + mod[2:3] * mix
    x1_ref[0] = x1
    h2 = _rms(x1, D_MODEL) * g2_ref[...] * (1.0 + mod[4:5]) + mod[3:4]
    h2pa_ref[0], h2pb_ref[0] = _pack_rows(h2)
    logits_t = _dot_nt(wr_ref[...], h2, precision=HIGHEST)
    slot8, w8, cnt = _route(logits_t, br_ref[...], cnt_scr[:, 0:1], cap)
    slot_ref[...] = slot8
    t = w8.shape[1]
    w8_ref[0] = jnp.concatenate([w8, jnp.zeros((LANES - TOP_K, t), F32)], axis=0).T
    cnt_scr[...] = jnp.broadcast_to(cnt, cnt_scr.shape)
    cnt_ref[...] = cnt_scr[...]


def _outproj(og, om, x, mod, per_batch_mod, p, cnt0, cap, tm):
    b, l, d = x.shape
    nt = l // tm
    mod_map = (lambda bi, i: (bi, 0, 0)) if per_batch_mod else (lambda bi, i: (0, 0, 0))
    tok = lambda w: pl.BlockSpec((1, tm, w), lambda bi, i: (bi, i, 0))
    full = lambda a: pl.BlockSpec(a.shape, lambda bi, i: (0, 0))
    weights = [p['w_out_top'], p['w_out_bot'], p['g_norm2'], p['w_router_t'], p['b_router_col']]
    return pl.pallas_call(
        functools.partial(_outproj_kernel, cap=cap),
        grid=(b, nt),
        in_specs=[tok(GLA_VAL_W), tok(MLA_W), tok(d), pl.BlockSpec((1, 6, d), mod_map)]
                 + [full(w) for w in weights] + [full(cnt0)],
        out_specs=[tok(d), tok(PACK_W), tok(PACK_W),
                   pl.BlockSpec((TOP_K, tm), lambda bi, i: (0, bi * nt + i)),
                   tok(LANES), full(cnt0)],
        out_shape=[jax.ShapeDtypeStruct((b, l, d), F32), jax.ShapeDtypeStruct((b, l, PACK_W), U32),
                   jax.ShapeDtypeStruct((b, l, PACK_W), U32),
                   jax.ShapeDtypeStruct((TOP_K, b * l), I32),
                   jax.ShapeDtypeStruct((b, l, LANES), F32),
                   jax.ShapeDtypeStruct(cnt0.shape, F32)],
        scratch_shapes=[pltpu.VMEM(cnt0.shape, F32)],
        compiler_params=pltpu.CompilerParams(
            dimension_semantics=("arbitrary", "arbitrary"), vmem_limit_bytes=VMEM_LIMIT),
        name="outproj",
    )(og, om, x, mod, *weights, cnt0)


def _sc_mesh():
    return plsc.VectorSubcoreMesh(core_axis_name="c", subcore_axis_name="s")


def _sc_scatter_rows(src, slot8, n_rows_out):
    t, d = src.shape
    nk = slot8.shape[0]

    @functools.partial(pl.kernel, out_type=jax.ShapeDtypeStruct((n_rows_out, d), src.dtype),
                       mesh=_sc_mesh(), scratch_types=[], name="dispatch")
    def run(src_hbm, slot_hbm, out_hbm):
        def body(x_vmem, i_vmem):
            for k in range(nk):
                pltpu.sync_copy(x_vmem, out_hbm.at[i_vmem.at[k]])

        pltpu.emit_pipeline(
            body, grid=(t // SC_WINDOW,),
            in_specs=[pl.BlockSpec((SC_WINDOW, d), lambda i: (i, 0)),
                      pl.BlockSpec((nk, SC_WINDOW), lambda i: (0, i))],
            out_specs=[], core_axis_name=("c", "s"),
            dimension_semantics=(pltpu.PARALLEL,))(src_hbm, slot_hbm)

    return run(src, slot8)


def _sc_gather_rows(table, slots):
    n = slots.shape[1]
    d = table.shape[1]

    @functools.partial(pl.kernel, out_type=jax.ShapeDtypeStruct((n, d), table.dtype),
                       mesh=_sc_mesh(), scratch_types=[], name="collect")
    def run(tab_hbm, slot_hbm, out_hbm):
        def body(i_vmem, o_vmem):
            pltpu.sync_copy(tab_hbm.at[i_vmem.at[0]], o_vmem)

        pltpu.emit_pipeline(
            body, grid=(n // SC_WINDOW,),
            in_specs=[pl.BlockSpec((1, SC_WINDOW), lambda i: (0, i))],
            out_specs=[pl.BlockSpec((SC_WINDOW, d), lambda i: (i, 0))],
            core_axis_name=("c", "s"),
            dimension_semantics=(pltpu.PARALLEL,))(slot_hbm, out_hbm)

    return run(table, slots)


def _ffn_kernel(blk_ref, exp_ref, nt_ref, xsa_ref, xsb_ref, wg_ref, wu_ref, wd_ref,
                ysa_ref, ysb_ref, wg_b, wu_b, wd_b):
    j = pl.program_id(0)

    @pl.when(j < nt_ref[0])
    def _():
        @pl.when((j == 0) | (exp_ref[j] != exp_ref[jnp.maximum(j - 1, 0)]))
        def _():
            wg_b[...] = wg_ref[0].astype(BF16)
            wu_b[...] = wu_ref[0].astype(BF16)
            wd_b[...] = wd_ref[0].astype(BF16)

        x = _unpack_rows([xsa_ref[...], xsb_ref[...]]).astype(BF16)
        a = _silu(_dot(x, wg_b[...])) * _dot(x, wu_b[...])
        ysa_ref[...], ysb_ref[...] = _pack_rows(_dot(a.astype(BF16), wd_b[...]))


def _ffn(xs_parts, blk, exp, ntiles, w_gate, w_up, w_down):
    n_steps = blk.shape[0]
    d = w_gate.shape[1]
    row_map = lambda j, blk, exp, nt: (blk[j], 0)
    wmap = lambda j, blk, exp, nt: (exp[j], 0, 0)
    rows = pl.BlockSpec((FFN_TILE, PACK_W), row_map)
    return pl.pallas_call(
        _ffn_kernel,
        grid_spec=pltpu.PrefetchScalarGridSpec(
            num_scalar_prefetch=3, grid=(n_steps,),
            in_specs=[rows] * PACK_PARTS
                     + [pl.BlockSpec((1, d, D_EXPERT), wmap), pl.BlockSpec((1, d, D_EXPERT), wmap),
                        pl.BlockSpec((1, D_EXPERT, d), wmap)],
            out_specs=[rows] * PACK_PARTS,
            scratch_shapes=[pltpu.VMEM((d, D_EXPERT), BF16), pltpu.VMEM((d, D_EXPERT), BF16),
                            pltpu.VMEM((D_EXPERT, d), BF16)]),
        out_shape=[jax.ShapeDtypeStruct(xs_parts[0].shape, U32)] * PACK_PARTS,
        compiler_params=pltpu.CompilerParams(
            dimension_semantics=("arbitrary",), vmem_limit_bytes=VMEM_LIMIT),
        name="ffn",
    )(blk, exp, ntiles, *xs_parts, w_gate, w_up, w_down)


def _tile_table(counts, cap, n_steps):
    nt_e = (counts + FFN_TILE - 1) // FFN_TILE
    cum = jnp.cumsum(nt_e)
    total = cum[-1]
    j = jnp.minimum(jnp.arange(n_steps, dtype=I32), total - 1)
    done = j[:, None] >= cum[None, :]
    e = jnp.sum(done, axis=1).astype(I32)
    first = jnp.sum(jnp.where(done, nt_e[None, :], 0), axis=1)
    blk = e * (cap // FFN_TILE) + (j - first)
    return blk.astype(I32), e, total.reshape(1).astype(I32)


def _combine_kernel(x1_ref, h2pa_ref, h2pb_ref, yga_ref, ygb_ref, w8_ref, mod_ref,
                    wsg_ref, wsu_ref, wsd_ref, o_ref):
    h2 = _unpack_rows([h2pa_ref[0], h2pb_ref[0]]).astype(BF16)
    a = _silu(_dot(h2, wsg_ref[...])) * _dot(h2, wsu_ref[...])
    acc = _dot(a.astype(BF16), wsd_ref[...])
    w8 = w8_ref[0]
    for k in range(TOP_K):
        acc += w8[:, k:k + 1] * _unpack_rows([yga_ref[k], ygb_ref[k]])
    o_ref[0] = x1_ref[0] + mod_ref[0][5:6] * acc


def _combine(x1, h2p_parts, yg_parts, tile0, w8, mod, per_batch_mod, p, tm):
    b, l, d = x1.shape
    nt = l // tm
    mod_map = (lambda bi, i: (bi, 0, 0)) if per_batch_mod else (lambda bi, i: (0, 0, 0))
    tok = lambda w: pl.BlockSpec((1, tm, w), lambda bi, i: (bi, i, 0))
    full = lambda a: pl.BlockSpec(a.shape, lambda bi, i: (0, 0))
    weights = [p['w_sh_gate'], p['w_sh_up'], p['w_sh_down']]
    gathered = pl.BlockSpec((TOP_K, tm, PACK_W), lambda bi, i: (0, tile0 + bi * nt + i, 0))
    return pl.pallas_call(
        _combine_kernel,
        grid=(b, nt),
        in_specs=[tok(d)] + [tok(PACK_W)] * PACK_PARTS + [gathered] * PACK_PARTS
                 + [tok(LANES), pl.BlockSpec((1, 6, d), mod_map)] + [full(w) for w in weights],
        out_specs=tok(d),
        out_shape=jax.ShapeDtypeStruct((b, l, d), F32),
        compiler_params=pltpu.CompilerParams(
            dimension_semantics=("parallel", "parallel"), vmem_limit_bytes=VMEM_LIMIT),
        name="combine",
    )(x1, *h2p_parts, *yg_parts, w8, mod, *weights)


def _pad_heads(w, parts):
    k = w.shape[0]
    per = w.shape[1] // MLA_HEADS
    w = w.reshape(k, MLA_HEADS, per)[:, :, parts[0]:parts[1]]
    w = jnp.pad(w, ((0, 0), (0, 0), (0, HEAD_W - (parts[1] - parts[0]))))
    return w.reshape(k, MLA_HEADS * HEAD_W)


def _prep_params(l, g_norm1, g_norm2, w_in, w_gk_fwd, b_gk_fwd, w_gk_bwd, b_gk_bwd, g_gla_out,
                 g_q_lora, w_uq, g_kv_lora, w_ukv, g_qk_q, g_qk_k, w_out, w_router, b_router,
                 w_sh_gate, w_sh_up, w_sh_down):
    w = w_in[l]
    d = w.shape[0]
    o_lrf = COL_CQ
    o_lrb = o_lrf + GLA_GATE_RANK
    o_cq = o_lrb + GLA_GATE_RANK
    o_ckv = o_cq + Q_LORA
    o_kr = o_ckv + KV_LORA
    w_in_p = jnp.concatenate([
        w[:, 0:COL_CQ], w[:, o_cq:o_ckv], w[:, o_ckv:o_kr], jnp.zeros((d, KR_LANE0), w.dtype),
        w[:, o_kr:o_kr + ROPE_DIM], w[:, o_lrf:o_lrb], w[:, o_lrb:o_cq]], axis=1).astype(BF16)
    z = jnp.zeros((GLA_GATE_RANK, GLA_KEY_W), F32)
    w_gk_big = jnp.concatenate([
        jnp.zeros((LANES - 2 * GLA_GATE_RANK, 2 * GLA_KEY_W), F32),
        jnp.concatenate([w_gk_fwd[l], z], axis=1),
        jnp.concatenate([z, w_gk_bwd[l]], axis=1)], axis=0).astype(BF16)
    pad_gain = lambda g: jnp.pad(g, (0, HEAD_W - QK_DIM)).reshape(1, HEAD_W)
    w_ukv_h = w_ukv[l]
    return {
        'g_norm1': g_norm1[l].reshape(1, d), 'g_norm2': g_norm2[l].reshape(1, d),
        'w_in_p': w_in_p, 'w_gk_big': w_gk_big,
        'b_gk': jnp.concatenate([b_gk_fwd[l], b_gk_bwd[l]]).reshape(1, 2 * GLA_KEY_W),
        'g_gla_out': g_gla_out[l].reshape(1, GLA_DV),
        'g_q_lora': g_q_lora[l].reshape(1, Q_LORA),
        'w_uq_p': _pad_heads(w_uq[l], (0, QK_DIM)).astype(BF16),
        'g_kv_lora': g_kv_lora[l].reshape(1, KV_LORA),
        'w_ukv_p': jnp.concatenate([_pad_heads(w_ukv_h, (0, NOPE_DIM)),
                                    _pad_heads(w_ukv_h, (NOPE_DIM, NOPE_DIM + V_DIM))],
                                   axis=1).astype(BF16),
        'gq': pad_gain(g_qk_q[l]), 'gk': pad_gain(g_qk_k[l]),
        'w_out_top': w_out[l][:GLA_VAL_W].astype(BF16),
        'w_out_bot': w_out[l][GLA_VAL_W:].astype(BF16),
        'w_router_t': w_router[l].T, 'b_router_col': b_router[l].reshape(N_EXPERTS, 1),
        'w_sh_gate': w_sh_gate[l].astype(BF16), 'w_sh_up': w_sh_up[l].astype(BF16),
        'w_sh_down': w_sh_down[l].astype(BF16),
    }


def _rope_tables(length):
    pos = jnp.arange(length)
    r = (pos // GRID_W).astype(F32)
    c = (pos % GRID_W).astype(F32)
    half = ROPE_DIM // 4
    inv_freq = ROPE_THETA ** (-jnp.arange(half, dtype=F32) / half)
    ang_r = r[:, None] * inv_freq[None, :]
    ang_c = c[:, None] * inv_freq[None, :]
    zeros = lambda w: jnp.zeros((length, w), F32)
    ones = lambda w: jnp.ones((length, w), F32)
    tail_w = HEAD_W - ROPE_LANE0 - ROPE_DIM
    cos = jnp.concatenate([ones(ROPE_LANE0), jnp.cos(ang_r), jnp.cos(ang_r), jnp.cos(ang_c),
                           jnp.cos(ang_c), ones(tail_w)], axis=1)
    s1 = jnp.concatenate([zeros(ROPE_LANE0), -jnp.sin(ang_r), zeros(half), -jnp.sin(ang_c),
                          zeros(half), zeros(tail_w)], axis=1)
    s2 = jnp.concatenate([zeros(ROPE_LANE0), zeros(half), jnp.sin(ang_r), zeros(half),
                          jnp.sin(ang_c), zeros(tail_w)], axis=1)
    return cos, s1, s2


def _identity_tables(rows):
    return (jnp.ones((rows, LANES), F32), jnp.zeros((rows, LANES), F32),
            jnp.zeros((rows, LANES), F32))


def _mix_and_route(x_tok, seq_shape, mod, per_batch_mod, p, s0f, s0b, rope_tabs, ctx_kv,
                   cnt0, cap, tm, tq):
    bt, lt, d = x_tok.shape
    b, l = seq_shape
    (q, k, v, g, lgf, lgb, qm, km, vm, ckvn, kr) = _inproj(x_tok, mod, per_batch_mod, p,
                                                            rope_tabs, tm)
    seq = lambda a: a.reshape(b, l, a.shape[-1])
    og, sf, sb = _gla(seq(q), seq(k), seq(v), seq(g), seq(lgf), seq(lgb), s0f, s0b,
                      p['g_gla_out'])
    km, vm = seq(km), seq(vm)
    if ctx_kv is not None:
        km = jnp.concatenate([km, ctx_kv[0]], axis=1)
        vm = jnp.concatenate([vm, ctx_kv[1]], axis=1)
    om = _attn(seq(qm), km, vm, tq)
    tokv = lambda a: a.reshape(bt, lt, a.shape[-1])
    routed = _outproj(tokv(og), tokv(om), x_tok, mod, per_batch_mod, p, cnt0, cap, tm)
    return routed, sf, sb, ckvn, kr


def kernel(x_prompt, x_sample, c, state_gla_fwd, state_gla_bwd, cache_mla_ckv, cache_mla_krope,
           c_ctx, w_ada, b_ada, g_norm1, g_norm2, w_in, w_gk_fwd, b_gk_fwd, w_gk_bwd, b_gk_bwd,
           g_gla_out, g_q_lora, w_uq, g_kv_lora, w_ukv, g_qk_q, g_qk_k, w_out,
           w_router, b_router, w_exp_gate, w_exp_up, w_exp_down, w_sh_gate, w_sh_up, w_sh_down):
    bp, lp, d = x_prompt.shape
    bs, ls, _ = x_sample.shape
    depth = w_ada.shape[0]
    n_ctx, n_lat = bp * lp, bs * ls
    cap = n_ctx + n_lat
    n_steps = (cap * TOP_K) // FFN_TILE + N_EXPERTS
    xp = x_prompt.reshape(1, n_ctx, d)
    xs = x_sample
    new_f, new_b, new_ckv, new_kr = [], [], [], []
    lat_tabs = _rope_tables(ls)
    tm = 256
    ctx_tabs = _identity_tables(tm)
    cvecs = jnp.concatenate([c_ctx[None], c, jnp.zeros((8 - 1 - bs, d), F32)], axis=0)
    for l in range(depth):
        p = _prep_params(l, g_norm1, g_norm2, w_in, w_gk_fwd, b_gk_fwd, w_gk_bwd, b_gk_bwd,
                         g_gla_out, g_q_lora, w_uq, g_kv_lora, w_ukv, g_qk_q, g_qk_k, w_out,
                         w_router, b_router, w_sh_gate, w_sh_up, w_sh_down)
        mod = _ada(cvecs, w_ada[l], b_ada[l]).reshape(8, 6, d)
        mod_p, mod_s = mod[0:1], mod[1:1 + bs]
        zeros = jnp.zeros((bp, GLA_KEY_W, GLA_DV), F32)
        cnt0 = jnp.zeros((N_EXPERTS, LANES), F32)
        (x1p, h2a_p, h2b_p, slot_p, w8_p, cnt1), sf, sb, ckvn, kr = _mix_and_route(
            xp, (bp, lp), mod_p, False, p, zeros, zeros, ctx_tabs, None, cnt0, cap, tm, lp)
        new_f.append(sf.reshape(bp, GLA_HEADS, GLA_DK, GLA_DV))
        new_b.append(sb.reshape(bp, GLA_HEADS, GLA_DK, GLA_DV))
        new_ckv.append(ckvn.reshape(bp, lp, KV_LORA))
        new_kr.append(kr.reshape(bp, lp, LANES)[:, :, KR_LANE0:KR_LANE0 + ROPE_DIM])
        kr_cache = jnp.pad(cache_mla_krope[:, l],
                           ((0, 0), (0, 0), (ROPE_LANE0, LANES - ROPE_LANE0 - ROPE_DIM)))
        ctx_kv = _cache_kv(cache_mla_ckv[:, l], kr_cache, p)
        s0f = state_gla_fwd[:, l].reshape(bs, GLA_KEY_W, GLA_DV)
        s0b = state_gla_bwd[:, l].reshape(bs, GLA_KEY_W, GLA_DV)
        (x1s, h2a_s, h2b_s, slot_s, w8_s, cnt2), _, _, _, _ = _mix_and_route(
            xs, (bs, ls), mod_s, True, p, s0f, s0b, lat_tabs, ctx_kv, cnt1, cap, tm, 256)
        slot8 = jnp.concatenate([slot_p, slot_s], axis=1)
        slots_flat = slot8.reshape(1, TOP_K * cap)
        blk, exp, ntiles = _tile_table(cnt2[:, 0].astype(I32), cap, n_steps)
        x_sorted = [
            _sc_scatter_rows(jnp.concatenate([hp.reshape(n_ctx, PACK_W),
                                              hs.reshape(n_lat, PACK_W)], axis=0),
                             slot8, N_EXPERTS * cap)
            for hp, hs in ((h2a_p, h2a_s), (h2b_p, h2b_s))]
        y_sorted = _ffn(x_sorted, blk, exp, ntiles, w_exp_gate[l], w_exp_up[l], w_exp_down[l])
        yg = [_sc_gather_rows(y, slots_flat).reshape(TOP_K, cap, PACK_W) for y in y_sorted]
        xp = _combine(x1p, (h2a_p, h2b_p), yg, 0, w8_p, mod_p, False, p, tm)
        xs = _combine(x1s, (h2a_s, h2b_s), yg, n_ctx // tm, w8_s, mod_s, True, p, tm)
    return (xp.reshape(bp, lp, d), xs, jnp.stack(new_f, axis=1), jnp.stack(new_b, axis=1),
            jnp.stack(new_ckv, axis=1), jnp.stack(new_kr, axis=1))
```

```python
import functools

import jax
import jax.numpy as jnp
import numpy as np
from jax import lax
from jax.experimental import pallas as pl
from jax.experimental.pallas import tpu as pltpu
from jax.experimental.pallas import tpu_sc as plsc

F32 = jnp.float32
BF16 = jnp.bfloat16
I32 = jnp.int32
U32 = jnp.uint32

D_MODEL = 1024
EPS = 1e-6
GRID_W = 64
GLA_HEADS = 4
GLA_DK = 64
GLA_DV = 128
GLA_GATE_RANK = 16
GLA_GATE_NORM = 16.0
GLA_KEY_W = GLA_HEADS * GLA_DK
GLA_VAL_W = GLA_HEADS * GLA_DV
MLA_HEADS = 4
Q_LORA = 256
KV_LORA = 128
NOPE_DIM = 64
ROPE_DIM = 32
V_DIM = 128
QK_DIM = NOPE_DIM + ROPE_DIM
ROPE_THETA = 10000.0
N_EXPERTS = 64
TOP_K = 8
N_GROUPS = 8
TOPK_GROUPS = 4
D_EXPERT = 256
ROUTED_SCALE = 2.5

LANES = 128
HEAD_W = LANES
MLA_W = MLA_HEADS * HEAD_W
ROPE_LANE0 = NOPE_DIM
COL_V = 2 * GLA_KEY_W
COL_G = COL_V + GLA_VAL_W
COL_CQ = COL_G + GLA_VAL_W
COL_CKV = COL_CQ + Q_LORA
TAIL0 = COL_CKV + KV_LORA
IN_W = TAIL0 + LANES
KR_LANE0 = LANES - ROPE_DIM - 2 * GLA_GATE_RANK
GLA_BLOCK = 256
GLA_SUB = 64
FFN_TILE = 512
PACK_PARTS = 2
PACK_W = D_MODEL // (2 * PACK_PARTS)
SC_WINDOW = 128
VMEM_LIMIT = 56 * 1024 * 1024

HIGHEST = lax.Precision.HIGHEST


def _dot(a, b, precision=None):
    return jnp.dot(a, b, preferred_element_type=F32, precision=precision)


def _dot_nt(a, b, precision=None):
    return lax.dot_general(a, b, (((1,), (1,)), ((), ())), preferred_element_type=F32,
                           precision=precision)


def _rms(x, width):
    ss = jnp.sum(x * x, axis=-1, keepdims=True) * (1.0 / width)
    return x * lax.rsqrt(ss + EPS)


def _silu(x):
    return x * jax.nn.sigmoid(x)


def _log_sigmoid(x):
    return jnp.minimum(x, 0.0) - jnp.log1p(jnp.exp(-jnp.abs(x)))


def _pack_rows(x):
    parts = []
    for i in range(PACK_PARTS):
        c0 = i * 2 * PACK_W
        lo = lax.bitcast_convert_type(x[:, c0:c0 + PACK_W].astype(BF16).astype(F32), U32)
        hi = lax.bitcast_convert_type(
            x[:, c0 + PACK_W:c0 + 2 * PACK_W].astype(BF16).astype(F32), U32)
        parts.append(hi | (lo >> 16))
    return parts


def _unpack_rows(parts):
    cols = []
    for w in parts:
        cols.append(lax.bitcast_convert_type(w << 16, F32))
        cols.append(lax.bitcast_convert_type(w & jnp.uint32(0xFFFF0000), F32))
    return jnp.concatenate(cols, axis=1)


def _ada_kernel(c_ref, w_ref, b_ref, o_ref):
    o_ref[...] = _dot(_silu(c_ref[...]), w_ref[...], precision=HIGHEST) + b_ref[...]


def _ada(cvecs, w_ada, b_ada):
    n = w_ada.shape[1]
    tn = 768
    return pl.pallas_call(
        _ada_kernel,
        grid=(n // tn,),
        in_specs=[pl.BlockSpec((8, D_MODEL), lambda j: (0, 0)),
                  pl.BlockSpec((D_MODEL, tn), lambda j: (0, j)),
                  pl.BlockSpec((1, tn), lambda j: (0, j))],
        out_specs=pl.BlockSpec((8, tn), lambda j: (0, j)),
        out_shape=jax.ShapeDtypeStruct((8, n), F32),
        name="ada",
    )(cvecs, w_ada, b_ada.reshape(1, n))


def _rope(x, c, s1, s2):
    return x * c + pltpu.roll(x, LANES - 8, 1) * s1 + pltpu.roll(x, 8, 1) * s2


def _mla_kv(ckv_n, kr, w_ukv_ref, gk, rope_tabs, k_ref, v_ref):
    kv = _dot(ckv_n.astype(BF16), w_ukv_ref[...])
    for h in range(MLA_HEADS):
        k_h = kv[:, h * HEAD_W:(h + 1) * HEAD_W] + kr
        k_h = _rms(k_h, QK_DIM) * gk
        if rope_tabs is not None:
            k_h = _rope(k_h, *rope_tabs)
        k_ref[0, :, h * HEAD_W:(h + 1) * HEAD_W] = k_h.astype(BF16)
    v_ref[0] = kv[:, MLA_W:].astype(BF16)


def _lane_mask(lo, hi, rows):
    lane = lax.broadcasted_iota(I32, (rows, LANES), 1)
    return (lane >= lo) & (lane < hi)


def _inproj_kernel(x_ref, mod_ref, g1_ref, win_ref, wgk_ref, bgk_ref, gql_ref, wuq_ref,
                   gkv_ref, wukv_ref, gq_ref, gk_ref, rc_ref, rs1_ref, rs2_ref,
                   q_ref, k_ref, v_ref, g_ref, lgf_ref, lgb_ref,
                   qm_ref, km_ref, vm_ref, ckvn_ref, kr_ref):
    x = x_ref[0]
    mod = mod_ref[0]
    h = _rms(x, D_MODEL) * g1_ref[...] * (1.0 + mod[1:2]) + mod[0:1]
    y = _dot(h.astype(BF16), win_ref[...])
    q_ref[0] = (y[:, 0:GLA_KEY_W] * GLA_DK ** -0.5).astype(BF16)
    k_ref[0] = y[:, GLA_KEY_W:COL_V].astype(BF16)
    v_ref[0] = y[:, COL_V:COL_G].astype(BF16)
    g_ref[0] = y[:, COL_G:COL_CQ].astype(BF16)
    tail = y[:, TAIL0:IN_W]
    pre = _dot(tail.astype(BF16), wgk_ref[...]) + bgk_ref[...]
    logg = _log_sigmoid(pre) * (1.0 / GLA_GATE_NORM)
    lgf_ref[0] = logg[:, 0:GLA_KEY_W]
    lgb_ref[0] = logg[:, GLA_KEY_W:]
    rows = x.shape[0]
    tabs = (rc_ref[...], rs1_ref[...], rs2_ref[...])
    cq = _rms(y[:, COL_CQ:COL_CKV], Q_LORA) * gql_ref[...]
    qm = _dot(cq.astype(BF16), wuq_ref[...])
    gq = gq_ref[...]
    for hh in range(MLA_HEADS):
        q_h = _rms(qm[:, hh * HEAD_W:(hh + 1) * HEAD_W], QK_DIM) * gq
        q_h = _rope(q_h, *tabs) * QK_DIM ** -0.5
        qm_ref[0, :, hh * HEAD_W:(hh + 1) * HEAD_W] = q_h.astype(BF16)
    ckv_n = _rms(y[:, COL_CKV:TAIL0], KV_LORA) * gkv_ref[...]
    ckvn_ref[0] = ckv_n
    kr_ref[0] = tail
    kr = jnp.where(_lane_mask(KR_LANE0, KR_LANE0 + ROPE_DIM, rows), tail, 0.0)
    _mla_kv(ckv_n, kr, wukv_ref, gk_ref[...], tabs, km_ref, vm_ref)


def _inproj(x, mod, per_batch_mod, p, rope_tabs, tm):
    b, l, d = x.shape
    nt = l // tm
    tab_blocks = rope_tabs[0].shape[0] // tm
    mod_map = (lambda bi, i: (bi, 0, 0)) if per_batch_mod else (lambda bi, i: (0, 0, 0))
    tab_map = (lambda bi, i: (i, 0)) if tab_blocks > 1 else (lambda bi, i: (0, 0))
    const = lambda bi, i: (0, 0)
    tok = lambda w: pl.BlockSpec((1, tm, w), lambda bi, i: (bi, i, 0))
    full = lambda a: pl.BlockSpec(a.shape, const)
    weights = [p['g_norm1'], p['w_in_p'], p['w_gk_big'], p['b_gk'], p['g_q_lora'], p['w_uq_p'],
               p['g_kv_lora'], p['w_ukv_p'], p['gq'], p['gk']]
    outs = [(GLA_KEY_W, BF16), (GLA_KEY_W, BF16), (GLA_VAL_W, BF16), (GLA_VAL_W, BF16),
            (GLA_KEY_W, F32), (GLA_KEY_W, F32), (MLA_W, BF16), (MLA_W, BF16), (MLA_W, BF16),
            (KV_LORA, F32), (LANES, F32)]
    return pl.pallas_call(
        _inproj_kernel,
        grid=(b, nt),
        in_specs=[tok(d), pl.BlockSpec((1, 6, d), mod_map)] + [full(w) for w in weights]
                 + [pl.BlockSpec((tm, LANES), tab_map)] * 3,
        out_specs=[tok(w) for w, _ in outs],
        out_shape=[jax.ShapeDtypeStruct((b, l, w), dt) for w, dt in outs],
        compiler_params=pltpu.CompilerParams(
            dimension_semantics=("parallel", "parallel"), vmem_limit_bytes=VMEM_LIMIT),
        name="inproj",
    )(x, mod, *weights, *rope_tabs)


def _cache_kv_kernel(ckv_ref, kr_ref, wukv_ref, gk_ref, k_ref, v_ref):
    _mla_kv(ckv_ref[0], kr_ref[0], wukv_ref, gk_ref[...], None, k_ref, v_ref)


def _cache_kv(ckv, kr128, p):
    b, l, _ = ckv.shape
    tok = lambda w: pl.BlockSpec((1, l, w), lambda bi: (bi, 0, 0))
    full = lambda a: pl.BlockSpec(a.shape, lambda bi: (0, 0))
    return pl.pallas_call(
        _cache_kv_kernel,
        grid=(b,),
        in_specs=[tok(KV_LORA), tok(LANES), full(p['w_ukv_p']), full(p['gk'])],
        out_specs=[tok(MLA_W), tok(MLA_W)],
        out_shape=[jax.ShapeDtypeStruct((b, l, MLA_W), BF16)] * 2,
        name="cache_kv",
    )(ckv, kr128, p['w_ukv_p'], p['gk'])


def _split3(x):
    hi = x.astype(BF16)
    r1 = x - hi.astype(F32)
    mid = r1.astype(BF16)
    lo = (r1 - mid.astype(F32)).astype(BF16)
    return hi, mid, lo


def _gla_block(q, k, v, lg, s, fwd):
    n = GLA_BLOCK
    nc = n // GLA_SUB
    row = lax.broadcasted_iota(I32, (n, n), 0)
    col = lax.broadcasted_iota(I32, (n, n), 1)
    same = (row // GLA_SUB) == (col // GLA_SUB)
    causal = same & ((col <= row) if fwd else (col >= row))
    tri = causal.astype(BF16)
    hi, mid, lo = _split3(lg)
    cum = _dot(tri, hi) + _dot(tri, mid) + _dot(tri, lo)
    tot_rows, mid_rows = [], []
    for c in range(nc):
        r_tot = c * GLA_SUB + (GLA_SUB - 1 if fwd else 0)
        r_mid = c * GLA_SUB + GLA_SUB // 2
        tot_rows.append(jnp.broadcast_to(cum[r_tot:r_tot + 1], (GLA_SUB, GLA_KEY_W)))
        mid_rows.append(jnp.broadcast_to(cum[r_mid:r_mid + 1], (GLA_SUB, GLA_KEY_W)))
    tot_rows = jnp.concatenate(tot_rows, axis=0)
    mid_rows = jnp.concatenate(mid_rows, axis=0)
    rel = cum - mid_rows
    qi = q * jnp.exp(rel)
    ki = (k * jnp.exp(-rel)).astype(BF16)
    q_in = (q * jnp.exp(cum)).astype(BF16)
    k_up = k * jnp.exp(tot_rows - cum)
    dec = jnp.exp(tot_rows)
    k_up_t = k_up.T.astype(BF16)
    dec_t = dec.T
    lane_head = lax.broadcasted_iota(I32, (n, GLA_KEY_W), 1) // GLA_DK
    srow_head = lax.broadcasted_iota(I32, (GLA_KEY_W, GLA_DV), 0) // GLA_DK
    vrow_chunk = lax.broadcasted_iota(I32, (n, GLA_VAL_W), 0) // GLA_SUB
    o_heads = []
    for h in range(GLA_HEADS):
        a = _dot_nt(jnp.where(lane_head == h, qi, 0.0).astype(BF16), ki)
        a = jnp.where(causal, a, 0.0).astype(BF16)
        o_heads.append(_dot(a, v[:, h * GLA_DV:(h + 1) * GLA_DV]))
    o = jnp.concatenate(o_heads, axis=1)
    o_inter = [None] * nc
    for c in (range(nc) if fwd else range(nc - 1, -1, -1)):
        s_bd = jnp.concatenate(
            [jnp.where(srow_head == h, s, 0.0).astype(BF16) for h in range(GLA_HEADS)], axis=1)
        o_inter[c] = _dot(q_in[c * GLA_SUB:(c + 1) * GLA_SUB], s_bd)
        v_c = jnp.where(vrow_chunk == c, v, jnp.zeros_like(v))
        u = jnp.concatenate(
            [_dot(k_up_t[h * GLA_DK:(h + 1) * GLA_DK], v_c[:, h * GLA_DV:(h + 1) * GLA_DV])
             for h in range(GLA_HEADS)], axis=0)
        s = dec_t[:, c * GLA_SUB:c * GLA_SUB + 1] * s + u
    return o + jnp.concatenate(o_inter, axis=0), s


def _gla_kernel(q_ref, k_ref, v_ref, g_ref, lgf_ref, lgb_ref, s0f_ref, s0b_ref, gout_ref,
                o_ref, sf_ref, sb_ref, acc_ref, st_ref, *, nblk):
    st_ref[0] = s0f_ref[0]
    st_ref[1] = s0b_ref[0]

    def load(blk):
        r = pl.ds(pl.multiple_of(blk * GLA_BLOCK, GLA_BLOCK), GLA_BLOCK)
        return r, q_ref[0, r, :].astype(F32), k_ref[0, r, :].astype(F32), v_ref[0, r, :]

    def fwd_step(blk, carry):
        r, q, k, v = load(blk)
        o, s = _gla_block(q, k, v, lgf_ref[0, r, :], st_ref[0], True)
        acc_ref[r, :] = o
        st_ref[0] = s
        return carry

    def bwd_step(i, carry):
        r, q, k, v = load(nblk - 1 - i)
        o, s = _gla_block(q, k, v, lgb_ref[0, r, :], st_ref[1], False)
        acc_ref[r, :] += o
        st_ref[1] = s
        return carry

    lax.fori_loop(0, nblk, fwd_step, 0)
    lax.fori_loop(0, nblk, bwd_step, 0)
    sf_ref[0] = st_ref[0]
    sb_ref[0] = st_ref[1]

    def fin_step(blk, carry):
        r = pl.ds(pl.multiple_of(blk * GLA_BLOCK, GLA_BLOCK), GLA_BLOCK)
        o = acc_ref[r, :]
        gate = _silu(g_ref[0, r, :].astype(F32))
        for h in range(GLA_HEADS):
            sl = slice(h * GLA_DV, (h + 1) * GLA_DV)
            o_ref[0, r, sl] = (_rms(o[:, sl], GLA_DV) * gout_ref[...] * gate[:, sl]).astype(BF16)
        return carry

    lax.fori_loop(0, nblk, fin_step, 0)


def _gla(q, k, v, g, lgf, lgb, s0f, s0b, g_out):
    b, l, _ = q.shape
    seq = lambda w: pl.BlockSpec((1, l, w), lambda bi: (bi, 0, 0))
    st = pl.BlockSpec((1, GLA_KEY_W, GLA_DV), lambda bi: (bi, 0, 0))
    return pl.pallas_call(
        functools.partial(_gla_kernel, nblk=l // GLA_BLOCK),
        grid=(b,),
        in_specs=[seq(GLA_KEY_W), seq(GLA_KEY_W), seq(GLA_VAL_W), seq(GLA_VAL_W),
                  seq(GLA_KEY_W), seq(GLA_KEY_W), st, st,
                  pl.BlockSpec((1, GLA_DV), lambda bi: (0, 0))],
        out_specs=[seq(GLA_VAL_W), st, st],
        out_shape=[jax.ShapeDtypeStruct((b, l, GLA_VAL_W), BF16),
                   jax.ShapeDtypeStruct((b, GLA_KEY_W, GLA_DV), F32),
                   jax.ShapeDtypeStruct((b, GLA_KEY_W, GLA_DV), F32)],
        scratch_shapes=[pltpu.VMEM((l, GLA_VAL_W), F32),
                        pltpu.VMEM((2, GLA_KEY_W, GLA_DV), F32)],
        compiler_params=pltpu.CompilerParams(
            dimension_semantics=("parallel",), vmem_limit_bytes=VMEM_LIMIT),
        name="gla",
    )(q, k, v, g, lgf, lgb, s0f, s0b, g_out)


def _attn_kernel(*refs, n_kv):
    q_ref, o_ref = refs[0], refs[-1]
    kv = [(refs[1 + 2 * i], refs[2 + 2 * i]) for i in range(n_kv)]
    for h in range(MLA_HEADS):
        sl = slice(h * HEAD_W, (h + 1) * HEAD_W)
        q = q_ref[0, :, sl]
        scores = [_dot_nt(q, k_ref[0, :, sl]) for k_ref, _ in kv]
        m = functools.reduce(jnp.maximum, [jnp.max(s, axis=-1, keepdims=True) for s in scores])
        o, den = 0.0, 0.0
        for s, (_, v_ref) in zip(scores, kv):
            p = jnp.exp(s - m)
            den = den + jnp.sum(p, axis=-1, keepdims=True)
            o = o + _dot(p.astype(BF16), v_ref[0, :, sl])
        o_ref[0, :, sl] = (o / den).astype(BF16)


def _attn(q, kv_pairs, tq):
    b, l, _ = q.shape
    kv_specs, kv_args = [], []
    for k, v in kv_pairs:
        spec = pl.BlockSpec((1, k.shape[1], MLA_W), lambda bi, i: (bi, 0, 0))
        kv_specs += [spec, spec]
        kv_args += [k, v]
    return pl.pallas_call(
        functools.partial(_attn_kernel, n_kv=len(kv_pairs)),
        grid=(b, l // tq),
        in_specs=[pl.BlockSpec((1, tq, MLA_W), lambda bi, i: (bi, i, 0))] + kv_specs,
        out_specs=pl.BlockSpec((1, tq, MLA_W), lambda bi, i: (bi, i, 0)),
        out_shape=jax.ShapeDtypeStruct((b, l, MLA_W), BF16),
        compiler_params=pltpu.CompilerParams(
            dimension_semantics=("parallel", "parallel"), vmem_limit_bytes=VMEM_LIMIT),
        name="attn",
    )(q, *kv_args)


def _sum01(x):
    return jnp.sum(jnp.sum(x, axis=1, keepdims=True), axis=0, keepdims=True)


def _route(logits_t, bias_col, cnt_col, cap):
    t = logits_t.shape[1]
    gsz = N_EXPERTS // N_GROUPS
    scores = jax.nn.sigmoid(logits_t)
    sel = (scores + bias_col).reshape(N_GROUPS, gsz, t)
    scores = scores.reshape(N_GROUPS, gsz, t)
    neg = jnp.float32(-jnp.inf)
    ie = lax.broadcasted_iota(I32, (N_GROUPS, gsz, t), 1)
    ig = lax.broadcasted_iota(I32, (N_GROUPS, gsz, t), 0)
    m1 = jnp.max(sel, axis=1, keepdims=True)
    first = jnp.min(jnp.where(sel == m1, ie, gsz), axis=1, keepdims=True)
    m2 = jnp.max(jnp.where(ie == first, neg, sel), axis=1, keepdims=True)
    grp = m1 + m2
    igk = lax.broadcasted_iota(I32, (N_GROUPS, 1, t), 0)
    g_sel = jnp.zeros((N_GROUPS, 1, t), jnp.bool_)
    cur = grp
    for _ in range(TOPK_GROUPS):
        m = jnp.max(cur, axis=0, keepdims=True)
        pick = igk == jnp.min(jnp.where(cur == m, igk, N_GROUPS), axis=0, keepdims=True)
        g_sel = g_sel | pick
        cur = jnp.where(pick, neg, cur)
    cur = jnp.where(g_sel, sel, neg)
    idx = ig * gsz + ie
    e_sel = jnp.zeros((N_GROUPS, gsz, t), jnp.bool_)
    picks = []
    for _ in range(TOP_K):
        m = jnp.max(jnp.max(cur, axis=1, keepdims=True), axis=0, keepdims=True)
        cand = jnp.where(cur == m, idx, N_EXPERTS)
        pick = idx == jnp.min(jnp.min(cand, axis=1, keepdims=True), axis=0, keepdims=True)
        picks.append(pick)
        e_sel = e_sel | pick
        cur = jnp.where(pick, neg, cur)
    w = jnp.where(e_sel, scores, 0.0)
    gate = w / _sum01(w) * ROUTED_SCALE
    sel_f = e_sel.astype(F32).reshape(N_EXPERTS, t)
    earlier = (lax.broadcasted_iota(I32, (t, t), 0) < lax.broadcasted_iota(I32, (t, t), 1))
    rank = _dot(sel_f.astype(BF16), earlier.astype(BF16))
    base = lax.broadcasted_iota(I32, (N_EXPERTS, 1), 0).astype(F32) * float(cap) + cnt_col
    slot = (base + rank).reshape(N_GROUPS, gsz, t)
    slot8 = jnp.concatenate([_sum01(jnp.where(pk, slot, 0.0)).reshape(1, t) for pk in picks],
                            axis=0).astype(I32)
    w8 = jnp.concatenate([_sum01(jnp.where(pk, gate, 0.0)).reshape(1, t) for pk in picks], axis=0)
    return slot8, w8, cnt_col + jnp.sum(sel_f, axis=1, keepdims=True)


def _outproj_kernel(og_ref, om_ref, x_ref, mod_ref, wtop_ref, wbot_ref, g2_ref, wr_ref, br_ref,
                    x1_ref, h2pa_ref, h2pb_ref, slot_ref, w8_ref, cnt_ref, cnt_scr, *, cap):
    @pl.when((pl.program_id(0) == 0) & (pl.program_id(1) == 0))
    def _():
        cnt_scr[...] = jnp.zeros_like(cnt_scr)

    mod = mod_ref[0]
    mix = _dot(og_ref[0], wtop_ref[...]) + _dot(om_ref[0], wbot_ref[...])
    x1 = x_ref[0] + mod[2:3] * mix
    x1_ref[0] = x1
    h2 = _rms(x1, D_MODEL) * g2_ref[...] * (1.0 + mod[4:5]) + mod[3:4]
    h2pa_ref[0], h2pb_ref[0] = _pack_rows(h2)
    logits_t = _dot_nt(wr_ref[...], h2, precision=HIGHEST)
    slot8, w8, cnt = _route(logits_t, br_ref[...], cnt_scr[:, 0:1], cap)
    slot_ref[...] = slot8
    t = w8.shape[1]
    w8_ref[0] = jnp.concatenate([w8, jnp.zeros((LANES - TOP_K, t), F32)], axis=0).T
    cnt_scr[...] = jnp.broadcast_to(cnt, cnt_scr.shape)
    cnt_ref[...] = cnt_scr[...]


def _outproj(og, om, x, mod, per_batch_mod, p, tm):
    b, l, d = x.shape
    nt = l // tm
    mod_map = (lambda bi, i: (bi, 0, 0)) if per_batch_mod else (lambda bi, i: (0, 0, 0))
    tok = lambda w: pl.BlockSpec((1, tm, w), lambda bi, i: (bi, i, 0))
    full = lambda a: pl.BlockSpec(a.shape, lambda bi, i: (0, 0))
    weights = [p['w_out_top'], p['w_out_bot'], p['g_norm2'], p['w_router_t'], p['b_router_col']]
    cnt_shape = (N_EXPERTS, LANES)
    return pl.pallas_call(
        functools.partial(_outproj_kernel, cap=b * l),
        grid=(b, nt),
        in_specs=[tok(GLA_VAL_W), tok(MLA_W), tok(d), pl.BlockSpec((1, 6, d), mod_map)]
                 + [full(w) for w in weights],
        out_specs=[tok(d), tok(PACK_W), tok(PACK_W),
                   pl.BlockSpec((TOP_K, tm), lambda bi, i: (0, bi * nt + i)),
                   tok(LANES), pl.BlockSpec(cnt_shape, lambda bi, i: (0, 0))],
        out_shape=[jax.ShapeDtypeStruct((b, l, d), F32), jax.ShapeDtypeStruct((b, l, PACK_W), U32),
                   jax.ShapeDtypeStruct((b, l, PACK_W), U32),
                   jax.ShapeDtypeStruct((TOP_K, b * l), I32),
                   jax.ShapeDtypeStruct((b, l, LANES), F32),
                   jax.ShapeDtypeStruct(cnt_shape, F32)],
        scratch_shapes=[pltpu.VMEM(cnt_shape, F32)],
        compiler_params=pltpu.CompilerParams(
            dimension_semantics=("arbitrary", "arbitrary"), vmem_limit_bytes=VMEM_LIMIT),
        name="outproj",
    )(og, om, x, mod, *weights)


def _sc_mesh():
    return plsc.VectorSubcoreMesh(core_axis_name="c", subcore_axis_name="s")


def _sc_scatter_rows(src, slot8, n_rows_out):
    t, d = src.shape
    nk = slot8.shape[0]

    @functools.partial(pl.kernel, out_type=jax.ShapeDtypeStruct((n_rows_out, d), src.dtype),
                       mesh=_sc_mesh(), scratch_types=[], name="dispatch")
    def run(src_hbm, slot_hbm, out_hbm):
        def body(x_vmem, i_vmem):
            for k in range(nk):
                pltpu.sync_copy(x_vmem, out_hbm.at[i_vmem.at[k]])

        pltpu.emit_pipeline(
            body, grid=(t // SC_WINDOW,),
            in_specs=[pl.BlockSpec((SC_WINDOW, d), lambda i: (i, 0)),
                      pl.BlockSpec((nk, SC_WINDOW), lambda i: (0, i))],
            out_specs=[], core_axis_name=("c", "s"),
            dimension_semantics=(pltpu.PARALLEL,))(src_hbm, slot_hbm)

    return run(src, slot8)


def _sc_gather_rows(table, slots):
    n = slots.shape[1]
    d = table.shape[1]

    @functools.partial(pl.kernel, out_type=jax.ShapeDtypeStruct((n, d), table.dtype),
                       mesh=_sc_mesh(), scratch_types=[], name="collect")
    def run(tab_hbm, slot_hbm, out_hbm):
        def body(i_vmem, o_vmem):
            pltpu.sync_copy(tab_hbm.at[i_vmem.at[0]], o_vmem)

        pltpu.emit_pipeline(
            body, grid=(n // SC_WINDOW,),
            in_specs=[pl.BlockSpec((1, SC_WINDOW), lambda i: (0, i))],
            out_specs=[pl.BlockSpec((SC_WINDOW, d), lambda i: (i, 0))],
            core_axis_name=("c", "s"),
            dimension_semantics=(pltpu.PARALLEL,))(slot_hbm, out_hbm)

    return run(table, slots)


def _ffn_kernel(blk_ref, exp_ref, nt_ref, xsa_ref, xsb_ref, wg_ref, wu_ref, wd_ref,
                ysa_ref, ysb_ref, wg_b, wu_b, wd_b):
    j = pl.program_id(0)

    @pl.when(j < nt_ref[0])
    def _():
        @pl.when((j == 0) | (exp_ref[j] != exp_ref[jnp.maximum(j - 1, 0)]))
        def _():
            wg_b[...] = wg_ref[0].astype(BF16)
            wu_b[...] = wu_ref[0].astype(BF16)
            wd_b[...] = wd_ref[0].astype(BF16)

        x = _unpack_rows([xsa_ref[...], xsb_ref[...]]).astype(BF16)
        a = _silu(_dot(x, wg_b[...])) * _dot(x, wu_b[...])
        ysa_ref[...], ysb_ref[...] = _pack_rows(_dot(a.astype(BF16), wd_b[...]))


def _ffn(xs_parts, blk, exp, ntiles, w_gate, w_up, w_down):
    n_steps = blk.shape[0]
    d = w_gate.shape[1]
    row_map = lambda j, blk, exp, nt: (blk[j], 0)
    wmap = lambda j, blk, exp, nt: (exp[j], 0, 0)
    rows = pl.BlockSpec((FFN_TILE, PACK_W), row_map)
    return pl.pallas_call(
        _ffn_kernel,
        grid_spec=pltpu.PrefetchScalarGridSpec(
            num_scalar_prefetch=3, grid=(n_steps,),
            in_specs=[rows] * PACK_PARTS
                     + [pl.BlockSpec((1, d, D_EXPERT), wmap), pl.BlockSpec((1, d, D_EXPERT), wmap),
                        pl.BlockSpec((1, D_EXPERT, d), wmap)],
            out_specs=[rows] * PACK_PARTS,
            scratch_shapes=[pltpu.VMEM((d, D_EXPERT), BF16), pltpu.VMEM((d, D_EXPERT), BF16),
                            pltpu.VMEM((D_EXPERT, d), BF16)]),
        out_shape=[jax.ShapeDtypeStruct(xs_parts[0].shape, U32)] * PACK_PARTS,
        compiler_params=pltpu.CompilerParams(
            dimension_semantics=("arbitrary",), vmem_limit_bytes=VMEM_LIMIT),
        name="ffn",
    )(blk, exp, ntiles, *xs_parts, w_gate, w_up, w_down)


def _tile_table(counts, cap, n_steps):
    nt_e = (counts + FFN_TILE - 1) // FFN_TILE
    cum = jnp.cumsum(nt_e)
    total = cum[-1]
    j = jnp.minimum(jnp.arange(n_steps, dtype=I32), total - 1)
    done = j[:, None] >= cum[None, :]
    e = jnp.sum(done, axis=1).astype(I32)
    first = jnp.sum(jnp.where(done, nt_e[None, :], 0), axis=1)
    blk = e * (cap // FFN_TILE) + (j - first)
    return blk.astype(I32), e, total.reshape(1).astype(I32)


def _combine_kernel(x1_ref, h2pa_ref, h2pb_ref, yga_ref, ygb_ref, w8_ref, mod_ref,
                    wsg_ref, wsu_ref, wsd_ref, o_ref):
    h2 = _unpack_rows([h2pa_ref[0], h2pb_ref[0]]).astype(BF16)
    a = _silu(_dot(h2, wsg_ref[...])) * _dot(h2, wsu_ref[...])
    acc = _dot(a.astype(BF16), wsd_ref[...])
    w8 = w8_ref[0]
    for k in range(TOP_K):
        acc += w8[:, k:k + 1] * _unpack_rows([yga_ref[k], ygb_ref[k]])
    o_ref[0] = x1_ref[0] + mod_ref[0][5:6] * acc


def _combine(x1, h2p_parts, yg_parts, w8, mod, per_batch_mod, p, tm):
    b, l, d = x1.shape
    nt = l // tm
    mod_map = (lambda bi, i: (bi, 0, 0)) if per_batch_mod else (lambda bi, i: (0, 0, 0))
    tok = lambda w: pl.BlockSpec((1, tm, w), lambda bi, i: (bi, i, 0))
    full = lambda a: pl.BlockSpec(a.shape, lambda bi, i: (0, 0))
    weights = [p['w_sh_gate'], p['w_sh_up'], p['w_sh_down']]
    gathered = pl.BlockSpec((TOP_K, tm, PACK_W), lambda bi, i: (0, bi * nt + i, 0))
    return pl.pallas_call(
        _combine_kernel,
        grid=(b, nt),
        in_specs=[tok(d)] + [tok(PACK_W)] * PACK_PARTS + [gathered] * PACK_PARTS
                 + [tok(LANES), pl.BlockSpec((1, 6, d), mod_map)] + [full(w) for w in weights],
        out_specs=tok(d),
        out_shape=jax.ShapeDtypeStruct((b, l, d), F32),
        compiler_params=pltpu.CompilerParams(
            dimension_semantics=("parallel", "parallel"), vmem_limit_bytes=VMEM_LIMIT),
        name="combine",
    )(x1, *h2p_parts, *yg_parts, w8, mod, *weights)


def _pad_heads(w, parts):
    k = w.shape[0]
    per = w.shape[1] // MLA_HEADS
    w = w.reshape(k, MLA_HEADS, per)[:, :, parts[0]:parts[1]]
    w = jnp.pad(w, ((0, 0), (0, 0), (0, HEAD_W - (parts[1] - parts[0]))))
    return w.reshape(k, MLA_HEADS * HEAD_W)


def _prep_params(l, g_norm1, g_norm2, w_in, w_gk_fwd, b_gk_fwd, w_gk_bwd, b_gk_bwd, g_gla_out,
                 g_q_lora, w_uq, g_kv_lora, w_ukv, g_qk_q, g_qk_k, w_out, w_router, b_router,
                 w_sh_gate, w_sh_up, w_sh_down):
    w = w_in[l]
    d = w.shape[0]
    o_lrf = COL_CQ
    o_lrb = o_lrf + GLA_GATE_RANK
    o_cq = o_lrb + GLA_GATE_RANK
    o_ckv = o_cq + Q_LORA
    o_kr = o_ckv + KV_LORA
    w_in_p = jnp.concatenate([
        w[:, 0:COL_CQ], w[:, o_cq:o_ckv], w[:, o_ckv:o_kr], jnp.zeros((d, KR_LANE0), w.dtype),
        w[:, o_kr:o_kr + ROPE_DIM], w[:, o_lrf:o_lrb], w[:, o_lrb:o_cq]], axis=1).astype(BF16)
    z = jnp.zeros((GLA_GATE_RANK, GLA_KEY_W), F32)
    w_gk_big = jnp.concatenate([
        jnp.zeros((LANES - 2 * GLA_GATE_RANK, 2 * GLA_KEY_W), F32),
        jnp.concatenate([w_gk_fwd[l], z], axis=1),
        jnp.concatenate([z, w_gk_bwd[l]], axis=1)], axis=0).astype(BF16)
    pad_gain = lambda g: jnp.pad(g, (0, HEAD_W - QK_DIM)).reshape(1, HEAD_W)
    w_ukv_h = w_ukv[l]
    return {
        'g_norm1': g_norm1[l].reshape(1, d), 'g_norm2': g_norm2[l].reshape(1, d),
        'w_in_p': w_in_p, 'w_gk_big': w_gk_big,
        'b_gk': jnp.concatenate([b_gk_fwd[l], b_gk_bwd[l]]).reshape(1, 2 * GLA_KEY_W),
        'g_gla_out': g_gla_out[l].reshape(1, GLA_DV),
        'g_q_lora': g_q_lora[l].reshape(1, Q_LORA),
        'w_uq_p': _pad_heads(w_uq[l], (0, QK_DIM)).astype(BF16),
        'g_kv_lora': g_kv_lora[l].reshape(1, KV_LORA),
        'w_ukv_p': jnp.concatenate([_pad_heads(w_ukv_h, (0, NOPE_DIM)),
                                    _pad_heads(w_ukv_h, (NOPE_DIM, NOPE_DIM + V_DIM))],
                                   axis=1).astype(BF16),
        'gq': pad_gain(g_qk_q[l]), 'gk': pad_gain(g_qk_k[l]),
        'w_out_top': w_out[l][:GLA_VAL_W].astype(BF16),
        'w_out_bot': w_out[l][GLA_VAL_W:].astype(BF16),
        'w_router_t': w_router[l].T, 'b_router_col': b_router[l].reshape(N_EXPERTS, 1),
        'w_sh_gate': w_sh_gate[l].astype(BF16), 'w_sh_up': w_sh_up[l].astype(BF16),
        'w_sh_down': w_sh_down[l].astype(BF16),
    }


def _rope_tables(length):
    pos = np.arange(length)
    r = (pos // GRID_W).astype(np.float32)
    c = (pos % GRID_W).astype(np.float32)
    half = ROPE_DIM // 4
    inv_freq = np.float32(ROPE_THETA) ** (-np.arange(half, dtype=np.float32) / np.float32(half))
    ang_r = r[:, None] * inv_freq[None, :]
    ang_c = c[:, None] * inv_freq[None, :]
    zeros = lambda w: np.zeros((length, w), np.float32)
    ones = lambda w: np.ones((length, w), np.float32)
    tail_w = HEAD_W - ROPE_LANE0 - ROPE_DIM
    cos = np.concatenate([ones(ROPE_LANE0), np.cos(ang_r), np.cos(ang_r), np.cos(ang_c),
                          np.cos(ang_c), ones(tail_w)], axis=1)
    s1 = np.concatenate([zeros(ROPE_LANE0), -np.sin(ang_r), zeros(half), -np.sin(ang_c),
                         zeros(half), zeros(tail_w)], axis=1)
    s2 = np.concatenate([zeros(ROPE_LANE0), zeros(half), np.sin(ang_r), zeros(half),
                         np.sin(ang_c), zeros(tail_w)], axis=1)
    return tuple(jnp.asarray(t, F32) for t in (cos, s1, s2))


def _identity_tables(rows):
    return (jnp.ones((rows, LANES), F32), jnp.zeros((rows, LANES), F32),
            jnp.zeros((rows, LANES), F32))


def _mix_route_dispatch(x_tok, seq_shape, mod, per_batch_mod, p, s0f, s0b, rope_tabs, ctx_kv,
                        tm, tq):
    bt, lt, d = x_tok.shape
    b, l = seq_shape
    n_tok = bt * lt
    (q, k, v, g, lgf, lgb, qm, km, vm, ckvn, kr) = _inproj(x_tok, mod, per_batch_mod, p,
                                                            rope_tabs, tm)
    seq = lambda a: a.reshape(b, l, a.shape[-1])
    og, sf, sb = _gla(seq(q), seq(k), seq(v), seq(g), seq(lgf), seq(lgb), s0f, s0b,
                      p['g_gla_out'])
    kv_pairs = [(seq(km), seq(vm))] + ([ctx_kv] if ctx_kv is not None else [])
    om = _attn(seq(qm), kv_pairs, tq)
    tokv = lambda a: a.reshape(bt, lt, a.shape[-1])
    x1, h2a, h2b, slot8, w8, cnt = _outproj(tokv(og), tokv(om), x_tok, mod, per_batch_mod, p, tm)
    x_sorted = [_sc_scatter_rows(h.reshape(n_tok, PACK_W), slot8, N_EXPERTS * n_tok)
                for h in (h2a, h2b)]
    routed = dict(x1=x1, h2=(h2a, h2b), slot8=slot8, w8=w8, cnt=cnt, x_sorted=x_sorted)
    return routed, sf, sb, ckvn, kr


def _experts_collect(r, w_gate, w_up, w_down):
    n_tok = r['slot8'].shape[1]
    n_steps = (n_tok * TOP_K) // FFN_TILE + N_EXPERTS
    blk, exp, ntiles = _tile_table(r['cnt'][:, 0].astype(I32), n_tok, n_steps)
    y_sorted = _ffn(r['x_sorted'], blk, exp, ntiles, w_gate, w_up, w_down)
    slots_flat = r['slot8'].reshape(1, TOP_K * n_tok)
    return [_sc_gather_rows(y, slots_flat).reshape(TOP_K, n_tok, PACK_W) for y in y_sorted]


def kernel(x_prompt, x_sample, c, state_gla_fwd, state_gla_bwd, cache_mla_ckv, cache_mla_krope,
           c_ctx, w_ada, b_ada, g_norm1, g_norm2, w_in, w_gk_fwd, b_gk_fwd, w_gk_bwd, b_gk_bwd,
           g_gla_out, g_q_lora, w_uq, g_kv_lora, w_ukv, g_qk_q, g_qk_k, w_out,
           w_router, b_router, w_exp_gate, w_exp_up, w_exp_down, w_sh_gate, w_sh_up, w_sh_down):
    bp, lp, d = x_prompt.shape
    bs, ls, _ = x_sample.shape
    depth = w_ada.shape[0]
    xp = x_prompt.reshape(1, bp * lp, d)
    xs = x_sample
    new_f, new_b, new_ckv, new_kr = [], [], [], []
    lat_tabs = _rope_tables(ls)
    tm = 256
    ctx_tabs = _identity_tables(tm)
    cvecs = jnp.concatenate([c_ctx[None], c, jnp.zeros((8 - 1 - bs, d), F32)], axis=0)
    for l in range(depth):
        p = _prep_params(l, g_norm1, g_norm2, w_in, w_gk_fwd, b_gk_fwd, w_gk_bwd, b_gk_bwd,
                         g_gla_out, g_q_lora, w_uq, g_kv_lora, w_ukv, g_qk_q, g_qk_k, w_out,
                         w_router, b_router, w_sh_gate, w_sh_up, w_sh_down)
        mod = _ada(cvecs, w_ada[l], b_ada[l]).reshape(8, 6, d)
        mod_p, mod_s = mod[0:1], mod[1:1 + bs]
        zeros = jnp.zeros((bp, GLA_KEY_W, GLA_DV), F32)
        r_ctx, sf, sb, ckvn, kr = _mix_route_dispatch(
            xp, (bp, lp), mod_p, False, p, zeros, zeros, ctx_tabs, None, tm, lp)
        new_f.append(sf.reshape(bp, GLA_HEADS, GLA_DK, GLA_DV))
        new_b.append(sb.reshape(bp, GLA_HEADS, GLA_DK, GLA_DV))
        new_ckv.append(ckvn.reshape(bp, lp, KV_LORA))
        new_kr.append(kr.reshape(bp, lp, LANES)[:, :, KR_LANE0:KR_LANE0 + ROPE_DIM])
        kr_cache = jnp.pad(cache_mla_krope[:, l],
                           ((0, 0), (0, 0), (ROPE_LANE0, LANES - ROPE_LANE0 - ROPE_DIM)))
        ctx_kv = _cache_kv(cache_mla_ckv[:, l], kr_cache, p)
        s0f = state_gla_fwd[:, l].reshape(bs, GLA_KEY_W, GLA_DV)
        s0b = state_gla_bwd[:, l].reshape(bs, GLA_KEY_W, GLA_DV)
        r_lat, _, _, _, _ = _mix_route_dispatch(
            xs, (bs, ls), mod_s, True, p, s0f, s0b, lat_tabs, ctx_kv, tm, 256)
        experts = (w_exp_gate[l], w_exp_up[l], w_exp_down[l])
        yg_ctx = _experts_collect(r_ctx, *experts)
        yg_lat = _experts_collect(r_lat, *experts)
        xp = _combine(r_ctx['x1'], r_ctx['h2'], yg_ctx, r_ctx['w8'], mod_p, False, p, tm)
        xs = _combine(r_lat['x1'], r_lat['h2'], yg_lat, r_lat['w8'], mod_s, True, p, tm)
    return (xp.reshape(bp, lp, d), xs, jnp.stack(new_f, axis=1), jnp.stack(new_b, axis=1),
            jnp.stack(new_ckv, axis=1), jnp.stack(new_kr, axis=1))
```

```python
import functools

import jax
import jax.numpy as jnp
import numpy as np
from jax import lax
from jax.experimental import pallas as pl
from jax.experimental.pallas import tpu as pltpu
from jax.experimental.pallas import tpu_sc as plsc

F32 = jnp.float32
BF16 = jnp.bfloat16
I32 = jnp.int32
U32 = jnp.uint32

D_MODEL = 1024
EPS = 1e-6
GRID_W = 64
GLA_HEADS = 4
GLA_DK = 64
GLA_DV = 128
GLA_GATE_RANK = 16
GLA_GATE_NORM = 16.0
GLA_KEY_W = GLA_HEADS * GLA_DK
GLA_VAL_W = GLA_HEADS * GLA_DV
MLA_HEADS = 4
Q_LORA = 256
KV_LORA = 128
NOPE_DIM = 64
ROPE_DIM = 32
V_DIM = 128
QK_DIM = NOPE_DIM + ROPE_DIM
ROPE_THETA = 10000.0
N_EXPERTS = 64
TOP_K = 8
N_GROUPS = 8
TOPK_GROUPS = 4
D_EXPERT = 256
ROUTED_SCALE = 2.5

LANES = 128
HEAD_W = LANES
MLA_W = MLA_HEADS * HEAD_W
ROPE_LANE0 = NOPE_DIM
COL_V = 2 * GLA_KEY_W
COL_G = COL_V + GLA_VAL_W
COL_CQ = COL_G + GLA_VAL_W
COL_CKV = COL_CQ + Q_LORA
TAIL0 = COL_CKV + KV_LORA
IN_W = TAIL0 + LANES
KR_LANE0 = LANES - ROPE_DIM - 2 * GLA_GATE_RANK
GLA_BLOCK = 256
GLA_SUB = 64
FFN_TILE = 512
PACK_PARTS = 2
PACK_W = D_MODEL // (2 * PACK_PARTS)
SC_WINDOW = 128
VMEM_LIMIT = 56 * 1024 * 1024

HIGHEST = lax.Precision.HIGHEST


def _dot(a, b, precision=None):
    return jnp.dot(a, b, preferred_element_type=F32, precision=precision)


def _dot_nt(a, b, precision=None):
    return lax.dot_general(a, b, (((1,), (1,)), ((), ())), preferred_element_type=F32,
                           precision=precision)


def _rms(x, width):
    ss = jnp.sum(x * x, axis=-1, keepdims=True) * (1.0 / width)
    return x * lax.rsqrt(ss + EPS)


def _silu(x):
    return x * jax.nn.sigmoid(x)


def _log_sigmoid(x):
    return jnp.minimum(x, 0.0) - jnp.log1p(jnp.exp(-jnp.abs(x)))


def _pack_rows(x):
    parts = []
    for i in range(PACK_PARTS):
        c0 = i * 2 * PACK_W
        lo = lax.bitcast_convert_type(x[:, c0:c0 + PACK_W].astype(BF16).astype(F32), U32)
        hi = lax.bitcast_convert_type(
            x[:, c0 + PACK_W:c0 + 2 * PACK_W].astype(BF16).astype(F32), U32)
        parts.append(hi | (lo >> 16))
    return parts


def _unpack_rows(parts):
    cols = []
    for w in parts:
        cols.append(lax.bitcast_convert_type(w << 16, F32))
        cols.append(lax.bitcast_convert_type(w & jnp.uint32(0xFFFF0000), F32))
    return jnp.concatenate(cols, axis=1)


def _ada_kernel(c_ref, w_ref, b_ref, o_ref):
    o_ref[...] = _dot(_silu(c_ref[...]), w_ref[...], precision=HIGHEST) + b_ref[...]


def _ada(cvecs, w_ada, b_ada):
    n = w_ada.shape[1]
    tn = 768
    return pl.pallas_call(
        _ada_kernel,
        grid=(n // tn,),
        in_specs=[pl.BlockSpec((8, D_MODEL), lambda j: (0, 0)),
                  pl.BlockSpec((D_MODEL, tn), lambda j: (0, j)),
                  pl.BlockSpec((1, tn), lambda j: (0, j))],
        out_specs=pl.BlockSpec((8, tn), lambda j: (0, j)),
        out_shape=jax.ShapeDtypeStruct((8, n), F32),
        name="ada",
    )(cvecs, w_ada, b_ada.reshape(1, n))


def _rope(x, c, s1, s2):
    return x * c + pltpu.roll(x, LANES - 8, 1) * s1 + pltpu.roll(x, 8, 1) * s2


def _mla_kv(ckv_n, kr, w_ukv_ref, gk, rope_tabs, k_ref, v_ref):
    kv = _dot(ckv_n.astype(BF16), w_ukv_ref[...])
    for h in range(MLA_HEADS):
        k_h = kv[:, h * HEAD_W:(h + 1) * HEAD_W] + kr
        k_h = _rms(k_h, QK_DIM) * gk
        if rope_tabs is not None:
            k_h = _rope(k_h, *rope_tabs)
        k_ref[0, :, h * HEAD_W:(h + 1) * HEAD_W] = k_h.astype(BF16)
    v_ref[0] = kv[:, MLA_W:].astype(BF16)


def _lane_mask(lo, hi, rows):
    lane = lax.broadcasted_iota(I32, (rows, LANES), 1)
    return (lane >= lo) & (lane < hi)


def _inproj_kernel(x_ref, mod_ref, g1_ref, win_ref, wgk_ref, bgk_ref, gql_ref, wuq_ref,
                   gkv_ref, wukv_ref, gq_ref, gk_ref, rc_ref, rs1_ref, rs2_ref,
                   q_ref, k_ref, v_ref, g_ref, lgf_ref, lgb_ref,
                   qm_ref, km_ref, vm_ref, ckvn_ref, kr_ref):
    x = x_ref[0]
    mod = mod_ref[0]
    h = _rms(x, D_MODEL) * g1_ref[...] * (1.0 + mod[1:2]) + mod[0:1]
    y = _dot(h.astype(BF16), win_ref[...])
    q_ref[0] = (y[:, 0:GLA_KEY_W] * GLA_DK ** -0.5).astype(BF16)
    k_ref[0] = y[:, GLA_KEY_W:COL_V].astype(BF16)
    v_ref[0] = y[:, COL_V:COL_G].astype(BF16)
    g_ref[0] = y[:, COL_G:COL_CQ].astype(BF16)
    tail = y[:, TAIL0:IN_W]
    pre = _dot(tail.astype(BF16), wgk_ref[...]) + bgk_ref[...]
    logg = _log_sigmoid(pre) * (1.0 / GLA_GATE_NORM)
    lgf_ref[0] = logg[:, 0:GLA_KEY_W]
    lgb_ref[0] = logg[:, GLA_KEY_W:]
    rows = x.shape[0]
    tabs = (rc_ref[...], rs1_ref[...], rs2_ref[...])
    cq = _rms(y[:, COL_CQ:COL_CKV], Q_LORA) * gql_ref[...]
    qm = _dot(cq.astype(BF16), wuq_ref[...])
    gq = gq_ref[...]
    for hh in range(MLA_HEADS):
        q_h = _rms(qm[:, hh * HEAD_W:(hh + 1) * HEAD_W], QK_DIM) * gq
        q_h = _rope(q_h, *tabs) * QK_DIM ** -0.5
        qm_ref[0, :, hh * HEAD_W:(hh + 1) * HEAD_W] = q_h.astype(BF16)
    ckv_n = _rms(y[:, COL_CKV:TAIL0], KV_LORA) * gkv_ref[...]
    ckvn_ref[0] = ckv_n
    kr_ref[0] = tail
    kr = jnp.where(_lane_mask(KR_LANE0, KR_LANE0 + ROPE_DIM, rows), tail, 0.0)
    _mla_kv(ckv_n, kr, wukv_ref, gk_ref[...], tabs, km_ref, vm_ref)


def _inproj(x, mod, per_batch_mod, p, rope_tabs, tm):
    b, l, d = x.shape
    nt = l // tm
    tab_blocks = rope_tabs[0].shape[0] // tm
    mod_map = (lambda bi, i: (bi, 0, 0)) if per_batch_mod else (lambda bi, i: (0, 0, 0))
    tab_map = (lambda bi, i: (i, 0)) if tab_blocks > 1 else (lambda bi, i: (0, 0))
    const = lambda bi, i: (0, 0)
    tok = lambda w: pl.BlockSpec((1, tm, w), lambda bi, i: (bi, i, 0))
    full = lambda a: pl.BlockSpec(a.shape, const)
    weights = [p['g_norm1'], p['w_in_p'], p['w_gk_big'], p['b_gk'], p['g_q_lora'], p['w_uq_p'],
               p['g_kv_lora'], p['w_ukv_p'], p['gq'], p['gk']]
    outs = [(GLA_KEY_W, BF16), (GLA_KEY_W, BF16), (GLA_VAL_W, BF16), (GLA_VAL_W, BF16),
            (GLA_KEY_W, F32), (GLA_KEY_W, F32), (MLA_W, BF16), (MLA_W, BF16), (MLA_W, BF16),
            (KV_LORA, F32), (LANES, F32)]
    return pl.pallas_call(
        _inproj_kernel,
        grid=(b, nt),
        in_specs=[tok(d), pl.BlockSpec((1, 6, d), mod_map)] + [full(w) for w in weights]
                 + [pl.BlockSpec((tm, LANES), tab_map)] * 3,
        out_specs=[tok(w) for w, _ in outs],
        out_shape=[jax.ShapeDtypeStruct((b, l, w), dt) for w, dt in outs],
        compiler_params=pltpu.CompilerParams(
            dimension_semantics=("parallel", "parallel"), vmem_limit_bytes=VMEM_LIMIT),
        name="inproj",
    )(x, mod, *weights, *rope_tabs)


def _cache_kv_kernel(ckv_ref, kr_ref, wukv_ref, gk_ref, k_ref, v_ref):
    _mla_kv(ckv_ref[0], kr_ref[0], wukv_ref, gk_ref[...], None, k_ref, v_ref)


def _cache_kv(ckv, kr128, p):
    b, l, _ = ckv.shape
    tok = lambda w: pl.BlockSpec((1, l, w), lambda bi: (bi, 0, 0))
    full = lambda a: pl.BlockSpec(a.shape, lambda bi: (0, 0))
    return pl.pallas_call(
        _cache_kv_kernel,
        grid=(b,),
        in_specs=[tok(KV_LORA), tok(LANES), full(p['w_ukv_p']), full(p['gk'])],
        out_specs=[tok(MLA_W), tok(MLA_W)],
        out_shape=[jax.ShapeDtypeStruct((b, l, MLA_W), BF16)] * 2,
        name="cache_kv",
    )(ckv, kr128, p['w_ukv_p'], p['gk'])


def _split3(x):
    hi = x.astype(BF16)
    r1 = x - hi.astype(F32)
    mid = r1.astype(BF16)
    lo = (r1 - mid.astype(F32)).astype(BF16)
    return hi, mid, lo


def _gla_block(q, k, v, lg, s, fwd):
    n = GLA_BLOCK
    nc = n // GLA_SUB
    row = lax.broadcasted_iota(I32, (n, n), 0)
    col = lax.broadcasted_iota(I32, (n, n), 1)
    same = (row // GLA_SUB) == (col // GLA_SUB)
    causal = same & ((col <= row) if fwd else (col >= row))
    tri = causal.astype(BF16)
    hi, mid, lo = _split3(lg)
    cum = _dot(tri, hi) + _dot(tri, mid) + _dot(tri, lo)
    tot_rows, mid_rows = [], []
    for c in range(nc):
        r_tot = c * GLA_SUB + (GLA_SUB - 1 if fwd else 0)
        r_mid = c * GLA_SUB + GLA_SUB // 2
        tot_rows.append(jnp.broadcast_to(cum[r_tot:r_tot + 1], (GLA_SUB, GLA_KEY_W)))
        mid_rows.append(jnp.broadcast_to(cum[r_mid:r_mid + 1], (GLA_SUB, GLA_KEY_W)))
    tot_rows = jnp.concatenate(tot_rows, axis=0)
    mid_rows = jnp.concatenate(mid_rows, axis=0)
    rel = cum - mid_rows
    qi = q * jnp.exp(rel)
    ki = (k * jnp.exp(-rel)).astype(BF16)
    q_in = (q * jnp.exp(cum)).astype(BF16)
    k_up = k * jnp.exp(tot_rows - cum)
    dec = jnp.exp(tot_rows)
    k_up_t = k_up.T.astype(BF16)
    dec_t = dec.T
    lane_head = lax.broadcasted_iota(I32, (n, GLA_KEY_W), 1) // GLA_DK
    srow_head = lax.broadcasted_iota(I32, (GLA_KEY_W, GLA_DV), 0) // GLA_DK
    vrow_chunk = lax.broadcasted_iota(I32, (n, GLA_VAL_W), 0) // GLA_SUB
    o_heads = []
    for h in range(GLA_HEADS):
        a = _dot_nt(jnp.where(lane_head == h, qi, 0.0).astype(BF16), ki)
        a = jnp.where(causal, a, 0.0).astype(BF16)
        o_heads.append(_dot(a, v[:, h * GLA_DV:(h + 1) * GLA_DV]))
    o = jnp.concatenate(o_heads, axis=1)
    o_inter = [None] * nc
    for c in (range(nc) if fwd else range(nc - 1, -1, -1)):
        s_bd = jnp.concatenate(
            [jnp.where(srow_head == h, s, 0.0).astype(BF16) for h in range(GLA_HEADS)], axis=1)
        o_inter[c] = _dot(q_in[c * GLA_SUB:(c + 1) * GLA_SUB], s_bd)
        v_c = jnp.where(vrow_chunk == c, v, jnp.zeros_like(v))
        u = jnp.concatenate(
            [_dot(k_up_t[h * GLA_DK:(h + 1) * GLA_DK], v_c[:, h * GLA_DV:(h + 1) * GLA_DV])
             for h in range(GLA_HEADS)], axis=0)
        s = dec_t[:, c * GLA_SUB:c * GLA_SUB + 1] * s + u
    return o + jnp.concatenate(o_inter, axis=0), s


def _gla_kernel(q_ref, k_ref, v_ref, g_ref, lgf_ref, lgb_ref, s0f_ref, s0b_ref, gout_ref,
                o_ref, sf_ref, sb_ref, acc_ref, st_ref, *, nblk):
    st_ref[0] = s0f_ref[0]
    st_ref[1] = s0b_ref[0]

    def load(blk):
        r = pl.ds(pl.multiple_of(blk * GLA_BLOCK, GLA_BLOCK), GLA_BLOCK)
        return r, q_ref[0, r, :].astype(F32), k_ref[0, r, :].astype(F32), v_ref[0, r, :]

    def fwd_step(blk, carry):
        r, q, k, v = load(blk)
        o, s = _gla_block(q, k, v, lgf_ref[0, r, :], st_ref[0], True)
        acc_ref[r, :] = o
        st_ref[0] = s
        return carry

    def bwd_step(i, carry):
        r, q, k, v = load(nblk - 1 - i)
        o, s = _gla_block(q, k, v, lgb_ref[0, r, :], st_ref[1], False)
        acc_ref[r, :] += o
        st_ref[1] = s
        return carry

    lax.fori_loop(0, nblk, fwd_step, 0)
    lax.fori_loop(0, nblk, bwd_step, 0)
    sf_ref[0] = st_ref[0]
    sb_ref[0] = st_ref[1]

    def fin_step(blk, carry):
        r = pl.ds(pl.multiple_of(blk * GLA_BLOCK, GLA_BLOCK), GLA_BLOCK)
        o = acc_ref[r, :]
        gate = _silu(g_ref[0, r, :].astype(F32))
        for h in range(GLA_HEADS):
            sl = slice(h * GLA_DV, (h + 1) * GLA_DV)
            o_ref[0, r, sl] = (_rms(o[:, sl], GLA_DV) * gout_ref[...] * gate[:, sl]).astype(BF16)
        return carry

    lax.fori_loop(0, nblk, fin_step, 0)


def _gla(q, k, v, g, lgf, lgb, s0f, s0b, g_out):
    b, l, _ = q.shape
    seq = lambda w: pl.BlockSpec((1, l, w), lambda bi: (bi, 0, 0))
    st = pl.BlockSpec((1, GLA_KEY_W, GLA_DV), lambda bi: (bi, 0, 0))
    return pl.pallas_call(
        functools.partial(_gla_kernel, nblk=l // GLA_BLOCK),
        grid=(b,),
        in_specs=[seq(GLA_KEY_W), seq(GLA_KEY_W), seq(GLA_VAL_W), seq(GLA_VAL_W),
                  seq(GLA_KEY_W), seq(GLA_KEY_W), st, st,
                  pl.BlockSpec((1, GLA_DV), lambda bi: (0, 0))],
        out_specs=[seq(GLA_VAL_W), st, st],
        out_shape=[jax.ShapeDtypeStruct((b, l, GLA_VAL_W), BF16),
                   jax.ShapeDtypeStruct((b, GLA_KEY_W, GLA_DV), F32),
                   jax.ShapeDtypeStruct((b, GLA_KEY_W, GLA_DV), F32)],
        scratch_shapes=[pltpu.VMEM((l, GLA_VAL_W), F32),
                        pltpu.VMEM((2, GLA_KEY_W, GLA_DV), F32)],
        compiler_params=pltpu.CompilerParams(
            dimension_semantics=("parallel",), vmem_limit_bytes=VMEM_LIMIT),
        name="gla",
    )(q, k, v, g, lgf, lgb, s0f, s0b, g_out)


def _attn_kernel(*refs, n_kv):
    q_ref, o_ref = refs[0], refs[-1]
    kv = [(refs[1 + 2 * i], refs[2 + 2 * i]) for i in range(n_kv)]
    for h in range(MLA_HEADS):
        sl = slice(h * HEAD_W, (h + 1) * HEAD_W)
        q = q_ref[0, :, sl]
        scores = [_dot_nt(q, k_ref[0, :, sl]) for k_ref, _ in kv]
        m = functools.reduce(jnp.maximum, [jnp.max(s, axis=-1, keepdims=True) for s in scores])
        o, den = 0.0, 0.0
        for s, (_, v_ref) in zip(scores, kv):
            p = jnp.exp(s - m)
            den = den + jnp.sum(p, axis=-1, keepdims=True)
            o = o + _dot(p.astype(BF16), v_ref[0, :, sl])
        o_ref[0, :, sl] = (o / den).astype(BF16)


def _attn(q, kv_pairs, tq):
    b, l, _ = q.shape
    kv_specs, kv_args = [], []
    for k, v in kv_pairs:
        spec = pl.BlockSpec((1, k.shape[1], MLA_W), lambda bi, i: (bi, 0, 0))
        kv_specs += [spec, spec]
        kv_args += [k, v]
    return pl.pallas_call(
        functools.partial(_attn_kernel, n_kv=len(kv_pairs)),
        grid=(b, l // tq),
        in_specs=[pl.BlockSpec((1, tq, MLA_W), lambda bi, i: (bi, i, 0))] + kv_specs,
        out_specs=pl.BlockSpec((1, tq, MLA_W), lambda bi, i: (bi, i, 0)),
        out_shape=jax.ShapeDtypeStruct((b, l, MLA_W), BF16),
        compiler_params=pltpu.CompilerParams(
            dimension_semantics=("parallel", "parallel"), vmem_limit_bytes=VMEM_LIMIT),
        name="attn",
    )(q, *kv_args)


def _sum01(x):
    return jnp.sum(jnp.sum(x, axis=1, keepdims=True), axis=0, keepdims=True)


def _route(logits_t, bias_col, cnt_col, cap):
    t = logits_t.shape[1]
    gsz = N_EXPERTS // N_GROUPS
    scores = jax.nn.sigmoid(logits_t)
    sel = (scores + bias_col).reshape(N_GROUPS, gsz, t)
    scores = scores.reshape(N_GROUPS, gsz, t)
    neg = jnp.float32(-jnp.inf)
    ie = lax.broadcasted_iota(I32, (N_GROUPS, gsz, t), 1)
    ig = lax.broadcasted_iota(I32, (N_GROUPS, gsz, t), 0)
    m1 = jnp.max(sel, axis=1, keepdims=True)
    first = jnp.min(jnp.where(sel == m1, ie, gsz), axis=1, keepdims=True)
    m2 = jnp.max(jnp.where(ie == first, neg, sel), axis=1, keepdims=True)
    grp = m1 + m2
    igk = lax.broadcasted_iota(I32, (N_GROUPS, 1, t), 0)
    g_sel = jnp.zeros((N_GROUPS, 1, t), jnp.bool_)
    cur = grp
    for _ in range(TOPK_GROUPS):
        m = jnp.max(cur, axis=0, keepdims=True)
        pick = igk == jnp.min(jnp.where(cur == m, igk, N_GROUPS), axis=0, keepdims=True)
        g_sel = g_sel | pick
        cur = jnp.where(pick, neg, cur)
    cur = jnp.where(g_sel, sel, neg)
    idx = ig * gsz + ie
    e_sel = jnp.zeros((N_GROUPS, gsz, t), jnp.bool_)
    picks = []
    for _ in range(TOP_K):
        m = jnp.max(jnp.max(cur, axis=1, keepdims=True), axis=0, keepdims=True)
        cand = jnp.where(cur == m, idx, N_EXPERTS)
        pick = idx == jnp.min(jnp.min(cand, axis=1, keepdims=True), axis=0, keepdims=True)
        picks.append(pick)
        e_sel = e_sel | pick
        cur = jnp.where(pick, neg, cur)
    w = jnp.where(e_sel, scores, 0.0)
    gate = w / _sum01(w) * ROUTED_SCALE
    sel_f = e_sel.astype(F32).reshape(N_EXPERTS, t)
    earlier = (lax.broadcasted_iota(I32, (t, t), 0) < lax.broadcasted_iota(I32, (t, t), 1))
    rank = _dot(sel_f.astype(BF16), earlier.astype(BF16))
    base = lax.broadcasted_iota(I32, (N_EXPERTS, 1), 0).astype(F32) * float(cap) + cnt_col
    slot = (base + rank).reshape(N_GROUPS, gsz, t)
    slot8 = jnp.concatenate([_sum01(jnp.where(pk, slot, 0.0)).reshape(1, t) for pk in picks],
                            axis=0).astype(I32)
    w8 = jnp.concatenate([_sum01(jnp.where(pk, gate, 0.0)).reshape(1, t) for pk in picks], axis=0)
    return slot8, w8, cnt_col + jnp.sum(sel_f, axis=1, keepdims=True)


def _outproj_kernel(og_ref, om_ref, x_ref, mod_ref, wtop_ref, wbot_ref, g2_ref, wr_ref, br_ref,
                    x1_ref, h2pa_ref, h2pb_ref, slot_ref, w8_ref, cnt_ref, cnt_scr, *, cap):
    @pl.when((pl.program_id(0) == 0) & (pl.program_id(1) == 0))
    def _():
        cnt_scr[...] = jnp.zeros_like(cnt_scr)

    mod = mod_ref[0]
    mix = _dot(og_ref[0], wtop_ref[...]) + _dot(om_ref[0], wbot_ref[...])
    x1 = x_ref[0] + mod[2:3] * mix
    x1_ref[0] = x1
    h2 = _rms(x1, D_MODEL) * g2_ref[...] * (1.0 + mod[4:5]) + mod[3:4]
    h2pa_ref[0], h2pb_ref[0] = _pack_rows(h2)
    logits_t = _dot_nt(wr_ref[...], h2, precision=HIGHEST)
    slot8, w8, cnt = _route(logits_t, br_ref[...], cnt_scr[:, 0:1], cap)
    slot_ref[...] = slot8
    t = w8.shape[1]
    w8_ref[0] = jnp.concatenate([w8, jnp.zeros((LANES - TOP_K, t), F32)], axis=0).T
    cnt_scr[...] = jnp.broadcast_to(cnt, cnt_scr.shape)
    cnt_ref[...] = cnt_scr[...]


def _outproj(og, om, x, mod, per_batch_mod, p, tm):
    b, l, d = x.shape
    nt = l // tm
    mod_map = (lambda bi, i: (bi, 0, 0)) if per_batch_mod else (lambda bi, i: (0, 0, 0))
    tok = lambda w: pl.BlockSpec((1, tm, w), lambda bi, i: (bi, i, 0))
    full = lambda a: pl.BlockSpec(a.shape, lambda bi, i: (0, 0))
    weights = [p['w_out_top'], p['w_out_bot'], p['g_norm2'], p['w_router_t'], p['b_router_col']]
    cnt_shape = (N_EXPERTS, LANES)
    return pl.pallas_call(
        functools.partial(_outproj_kernel, cap=b * l),
        grid=(b, nt),
        in_specs=[tok(GLA_VAL_W), tok(MLA_W), tok(d), pl.BlockSpec((1, 6, d), mod_map)]
                 + [full(w) for w in weights],
        out_specs=[tok(d), tok(PACK_W), tok(PACK_W),
                   pl.BlockSpec((TOP_K, tm), lambda bi, i: (0, bi * nt + i)),
                   tok(LANES), pl.BlockSpec(cnt_shape, lambda bi, i: (0, 0))],
        out_shape=[jax.ShapeDtypeStruct((b, l, d), F32), jax.ShapeDtypeStruct((b, l, PACK_W), U32),
                   jax.ShapeDtypeStruct((b, l, PACK_W), U32),
                   jax.ShapeDtypeStruct((TOP_K, b * l), I32),
                   jax.ShapeDtypeStruct((b, l, LANES), F32),
                   jax.ShapeDtypeStruct(cnt_shape, F32)],
        scratch_shapes=[pltpu.VMEM(cnt_shape, F32)],
        compiler_params=pltpu.CompilerParams(
            dimension_semantics=("arbitrary", "arbitrary"), vmem_limit_bytes=VMEM_LIMIT),
        name="outproj",
    )(og, om, x, mod, *weights)


def _sc_mesh():
    return plsc.VectorSubcoreMesh(core_axis_name="c", subcore_axis_name="s")


def _sc_scatter_rows(src, slot8, n_rows_out):
    t, d = src.shape
    nk = slot8.shape[0]

    @functools.partial(pl.kernel, out_type=jax.ShapeDtypeStruct((n_rows_out, d), src.dtype),
                       mesh=_sc_mesh(), scratch_types=[], name="dispatch")
    def run(src_hbm, slot_hbm, out_hbm):
        def body(x_vmem, i_vmem):
            for k in range(nk):
                pltpu.sync_copy(x_vmem, out_hbm.at[i_vmem.at[k]])

        pltpu.emit_pipeline(
            body, grid=(t // SC_WINDOW,),
            in_specs=[pl.BlockSpec((SC_WINDOW, d), lambda i: (i, 0)),
                      pl.BlockSpec((nk, SC_WINDOW), lambda i: (0, i))],
            out_specs=[], core_axis_name=("c", "s"),
            dimension_semantics=(pltpu.PARALLEL,))(src_hbm, slot_hbm)

    return run(src, slot8)


def _sc_gather_rows(table, slots):
    n = slots.shape[1]
    d = table.shape[1]

    @functools.partial(pl.kernel, out_type=jax.ShapeDtypeStruct((n, d), table.dtype),
                       mesh=_sc_mesh(), scratch_types=[], name="collect")
    def run(tab_hbm, slot_hbm, out_hbm):
        def body(i_vmem, o_vmem):
            pltpu.sync_copy(tab_hbm.at[i_vmem.at[0]], o_vmem)

        pltpu.emit_pipeline(
            body, grid=(n // SC_WINDOW,),
            in_specs=[pl.BlockSpec((1, SC_WINDOW), lambda i: (0, i))],
            out_specs=[pl.BlockSpec((SC_WINDOW, d), lambda i: (i, 0))],
            core_axis_name=("c", "s"),
            dimension_semantics=(pltpu.PARALLEL,))(slot_hbm, out_hbm)

    return run(table, slots)


def _ffn_kernel(nt_ref, xsa_hbm, xsb_hbm, wg_ref, wu_ref, wd_ref, ysa_hbm, ysb_hbm,
                xbuf, ybuf, wg_b, wu_b, wd_b, sem_in, sem_out, *, cap):
    e = pl.program_id(0)
    n = nt_ref[e]
    xs_hbm = (xsa_hbm, xsb_hbm)
    ys_hbm = (ysa_hbm, ysb_hbm)

    def rows(t):
        return pl.ds(pl.multiple_of(e * cap + t * FFN_TILE, FFN_TILE), FFN_TILE)

    def in_copy(t, slot, part):
        return pltpu.make_async_copy(xs_hbm[part].at[rows(t)], xbuf.at[slot, part],
                                     sem_in.at[slot, part])

    def out_copy(t, slot, part):
        return pltpu.make_async_copy(ybuf.at[slot, part], ys_hbm[part].at[rows(t)],
                                     sem_out.at[slot, part])

    @pl.when(n > 0)
    def _():
        for part in range(PACK_PARTS):
            in_copy(0, 0, part).start()
        wg_b[...] = wg_ref[0].astype(BF16)
        wu_b[...] = wu_ref[0].astype(BF16)
        wd_b[...] = wd_ref[0].astype(BF16)

    def tile(t, carry):
        slot = t % 2
        for part in range(PACK_PARTS):
            in_copy(t, slot, part).wait()

        @pl.when(t + 1 < n)
        def _():
            for part in range(PACK_PARTS):
                in_copy(t + 1, 1 - slot, part).start()

        @pl.when(t >= 2)
        def _():
            for part in range(PACK_PARTS):
                out_copy(t - 2, slot, part).wait()

        x = _unpack_rows([xbuf[slot, part] for part in range(PACK_PARTS)]).astype(BF16)
        a = _silu(_dot(x, wg_b[...])) * _dot(x, wu_b[...])
        y = _pack_rows(_dot(a.astype(BF16), wd_b[...]))
        for part in range(PACK_PARTS):
            ybuf[slot, part] = y[part]
            out_copy(t, slot, part).start()
        return carry

    lax.fori_loop(0, n, tile, 0)

    for back in (2, 1):
        @pl.when(n >= back)
        def _():
            for part in range(PACK_PARTS):
                out_copy(n - back, (n - back) % 2, part).wait()


def _ffn(xs_parts, counts, w_gate, w_up, w_down):
    n_exp, d = w_gate.shape[0], w_gate.shape[1]
    cap = xs_parts[0].shape[0] // n_exp
    ntiles = (counts + FFN_TILE - 1) // FFN_TILE
    wmap = lambda e, nt: (e, 0, 0)
    hbm = pl.BlockSpec(memory_space=pl.ANY)
    return pl.pallas_call(
        functools.partial(_ffn_kernel, cap=cap),
        grid_spec=pltpu.PrefetchScalarGridSpec(
            num_scalar_prefetch=1, grid=(n_exp,),
            in_specs=[hbm] * PACK_PARTS
                     + [pl.BlockSpec((1, d, D_EXPERT), wmap), pl.BlockSpec((1, d, D_EXPERT), wmap),
                        pl.BlockSpec((1, D_EXPERT, d), wmap)],
            out_specs=[hbm] * PACK_PARTS,
            scratch_shapes=[pltpu.VMEM((2, PACK_PARTS, FFN_TILE, PACK_W), U32),
                            pltpu.VMEM((2, PACK_PARTS, FFN_TILE, PACK_W), U32),
                            pltpu.VMEM((d, D_EXPERT), BF16), pltpu.VMEM((d, D_EXPERT), BF16),
                            pltpu.VMEM((D_EXPERT, d), BF16),
                            pltpu.SemaphoreType.DMA((2, PACK_PARTS)),
                            pltpu.SemaphoreType.DMA((2, PACK_PARTS))]),
        out_shape=[jax.ShapeDtypeStruct(xs_parts[0].shape, U32)] * PACK_PARTS,
        compiler_params=pltpu.CompilerParams(
            dimension_semantics=("arbitrary",), vmem_limit_bytes=VMEM_LIMIT),
        name="ffn",
    )(ntiles, *xs_parts, w_gate, w_up, w_down)


def _combine_kernel(x1_ref, h2pa_ref, h2pb_ref, yga_ref, ygb_ref, w8_ref, mod_ref,
                    wsg_ref, wsu_ref, wsd_ref, o_ref):
    h2 = _unpack_rows([h2pa_ref[0], h2pb_ref[0]]).astype(BF16)
    a = _silu(_dot(h2, wsg_ref[...])) * _dot(h2, wsu_ref[...])
    acc = _dot(a.astype(BF16), wsd_ref[...])
    w8 = w8_ref[0]
    for k in range(TOP_K):
        acc += w8[:, k:k + 1] * _unpack_rows([yga_ref[k], ygb_ref[k]])
    o_ref[0] = x1_ref[0] + mod_ref[0][5:6] * acc


def _combine(x1, h2p_parts, yg_parts, w8, mod, per_batch_mod, p, tm):
    b, l, d = x1.shape
    nt = l // tm
    mod_map = (lambda bi, i: (bi, 0, 0)) if per_batch_mod else (lambda bi, i: (0, 0, 0))
    tok = lambda w: pl.BlockSpec((1, tm, w), lambda bi, i: (bi, i, 0))
    full = lambda a: pl.BlockSpec(a.shape, lambda bi, i: (0, 0))
    weights = [p['w_sh_gate'], p['w_sh_up'], p['w_sh_down']]
    gathered = pl.BlockSpec((TOP_K, tm, PACK_W), lambda bi, i: (0, bi * nt + i, 0))
    return pl.pallas_call(
        _combine_kernel,
        grid=(b, nt),
        in_specs=[tok(d)] + [tok(PACK_W)] * PACK_PARTS + [gathered] * PACK_PARTS
                 + [tok(LANES), pl.BlockSpec((1, 6, d), mod_map)] + [full(w) for w in weights],
        out_specs=tok(d),
        out_shape=jax.ShapeDtypeStruct((b, l, d), F32),
        compiler_params=pltpu.CompilerParams(
            dimension_semantics=("parallel", "parallel"), vmem_limit_bytes=VMEM_LIMIT),
        name="combine",
    )(x1, *h2p_parts, *yg_parts, w8, mod, *weights)


def _pad_heads(w, parts):
    k = w.shape[0]
    per = w.shape[1] // MLA_HEADS
    w = w.reshape(k, MLA_HEADS, per)[:, :, parts[0]:parts[1]]
    w = jnp.pad(w, ((0, 0), (0, 0), (0, HEAD_W - (parts[1] - parts[0]))))
    return w.reshape(k, MLA_HEADS * HEAD_W)


def _prep_params(l, g_norm1, g_norm2, w_in, w_gk_fwd, b_gk_fwd, w_gk_bwd, b_gk_bwd, g_gla_out,
                 g_q_lora, w_uq, g_kv_lora, w_ukv, g_qk_q, g_qk_k, w_out, w_router, b_router,
                 w_sh_gate, w_sh_up, w_sh_down):
    w = w_in[l]
    d = w.shape[0]
    o_lrf = COL_CQ
    o_lrb = o_lrf + GLA_GATE_RANK
    o_cq = o_lrb + GLA_GATE_RANK
    o_ckv = o_cq + Q_LORA
    o_kr = o_ckv + KV_LORA
    w_in_p = jnp.concatenate([
        w[:, 0:COL_CQ], w[:, o_cq:o_ckv], w[:, o_ckv:o_kr], jnp.zeros((d, KR_LANE0), w.dtype),
        w[:, o_kr:o_kr + ROPE_DIM], w[:, o_lrf:o_lrb], w[:, o_lrb:o_cq]], axis=1).astype(BF16)
    z = jnp.zeros((GLA_GATE_RANK, GLA_KEY_W), F32)
    w_gk_big = jnp.concatenate([
        jnp.zeros((LANES - 2 * GLA_GATE_RANK, 2 * GLA_KEY_W), F32),
        jnp.concatenate([w_gk_fwd[l], z], axis=1),
        jnp.concatenate([z, w_gk_bwd[l]], axis=1)], axis=0).astype(BF16)
    pad_gain = lambda g: jnp.pad(g, (0, HEAD_W - QK_DIM)).reshape(1, HEAD_W)
    w_ukv_h = w_ukv[l]
    return {
        'g_norm1': g_norm1[l].reshape(1, d), 'g_norm2': g_norm2[l].reshape(1, d),
        'w_in_p': w_in_p, 'w_gk_big': w_gk_big,
        'b_gk': jnp.concatenate([b_gk_fwd[l], b_gk_bwd[l]]).reshape(1, 2 * GLA_KEY_W),
        'g_gla_out': g_gla_out[l].reshape(1, GLA_DV),
        'g_q_lora': g_q_lora[l].reshape(1, Q_LORA),
        'w_uq_p': _pad_heads(w_uq[l], (0, QK_DIM)).astype(BF16),
        'g_kv_lora': g_kv_lora[l].reshape(1, KV_LORA),
        'w_ukv_p': jnp.concatenate([_pad_heads(w_ukv_h, (0, NOPE_DIM)),
                                    _pad_heads(w_ukv_h, (NOPE_DIM, NOPE_DIM + V_DIM))],
                                   axis=1).astype(BF16),
        'gq': pad_gain(g_qk_q[l]), 'gk': pad_gain(g_qk_k[l]),
        'w_out_top': w_out[l][:GLA_VAL_W].astype(BF16),
        'w_out_bot': w_out[l][GLA_VAL_W:].astype(BF16),
        'w_router_t': w_router[l].T, 'b_router_col': b_router[l].reshape(N_EXPERTS, 1),
        'w_sh_gate': w_sh_gate[l].astype(BF16), 'w_sh_up': w_sh_up[l].astype(BF16),
        'w_sh_down': w_sh_down[l].astype(BF16),
    }


def _rope_tables(length):
    pos = np.arange(length)
    r = (pos // GRID_W).astype(np.float32)
    c = (pos % GRID_W).astype(np.float32)
    half = ROPE_DIM // 4
    inv_freq = np.float32(ROPE_THETA) ** (-np.arange(half, dtype=np.float32) / np.float32(half))
    ang_r = r[:, None] * inv_freq[None, :]
    ang_c = c[:, None] * inv_freq[None, :]
    zeros = lambda w: np.zeros((length, w), np.float32)
    ones = lambda w: np.ones((length, w), np.float32)
    tail_w = HEAD_W - ROPE_LANE0 - ROPE_DIM
    cos = np.concatenate([ones(ROPE_LANE0), np.cos(ang_r), np.cos(ang_r), np.cos(ang_c),
                          np.cos(ang_c), ones(tail_w)], axis=1)
    s1 = np.concatenate([zeros(ROPE_LANE0), -np.sin(ang_r), zeros(half), -np.sin(ang_c),
                         zeros(half), zeros(tail_w)], axis=1)
    s2 = np.concatenate([zeros(ROPE_LANE0), zeros(half), np.sin(ang_r), zeros(half),
                         np.sin(ang_c), zeros(tail_w)], axis=1)
    return tuple(jnp.asarray(t, F32) for t in (cos, s1, s2))


def _identity_tables(rows):
    return (jnp.ones((rows, LANES), F32), jnp.zeros((rows, LANES), F32),
            jnp.zeros((rows, LANES), F32))


def _mix_route_dispatch(x_tok, seq_shape, mod, per_batch_mod, p, s0f, s0b, rope_tabs, ctx_kv,
                        tm, tq):
    bt, lt, d = x_tok.shape
    b, l = seq_shape
    n_tok = bt * lt
    (q, k, v, g, lgf, lgb, qm, km, vm, ckvn, kr) = _inproj(x_tok, mod, per_batch_mod, p,
                                                            rope_tabs, tm)
    seq = lambda a: a.reshape(b, l, a.shape[-1])
    og, sf, sb = _gla(seq(q), seq(k), seq(v), seq(g), seq(lgf), seq(lgb), s0f, s0b,
                      p['g_gla_out'])
    kv_pairs = [(seq(km), seq(vm))] + ([ctx_kv] if ctx_kv is not None else [])
    om = _attn(seq(qm), kv_pairs, tq)
    tokv = lambda a: a.reshape(bt, lt, a.shape[-1])
    x1, h2a, h2b, slot8, w8, cnt = _outproj(tokv(og), tokv(om), x_tok, mod, per_batch_mod, p, tm)
    x_sorted = [_sc_scatter_rows(h.reshape(n_tok, PACK_W), slot8, N_EXPERTS * n_tok)
                for h in (h2a, h2b)]
    routed = dict(x1=x1, h2=(h2a, h2b), slot8=slot8, w8=w8, cnt=cnt, x_sorted=x_sorted)
    return routed, sf, sb, ckvn, kr


def _experts_collect(r, w_gate, w_up, w_down):
    n_tok = r['slot8'].shape[1]
    y_sorted = _ffn(r['x_sorted'], r['cnt'][:, 0].astype(I32), w_gate, w_up, w_down)
    slots_flat = r['slot8'].reshape(1, TOP_K * n_tok)
    return [_sc_gather_rows(y, slots_flat).reshape(TOP_K, n_tok, PACK_W) for y in y_sorted]


def kernel(x_prompt, x_sample, c, state_gla_fwd, state_gla_bwd, cache_mla_ckv, cache_mla_krope,
           c_ctx, w_ada, b_ada, g_norm1, g_norm2, w_in, w_gk_fwd, b_gk_fwd, w_gk_bwd, b_gk_bwd,
           g_gla_out, g_q_lora, w_uq, g_kv_lora, w_ukv, g_qk_q, g_qk_k, w_out,
           w_router, b_router, w_exp_gate, w_exp_up, w_exp_down, w_sh_gate, w_sh_up, w_sh_down):
    bp, lp, d = x_prompt.shape
    bs, ls, _ = x_sample.shape
    depth = w_ada.shape[0]
    xp = x_prompt.reshape(1, bp * lp, d)
    xs = x_sample
    new_f, new_b, new_ckv, new_kr = [], [], [], []
    lat_tabs = _rope_tables(ls)
    tm = 256
    ctx_tabs = _identity_tables(tm)
    cvecs = jnp.concatenate([c_ctx[None], c, jnp.zeros((8 - 1 - bs, d), F32)], axis=0)
    for l in range(depth):
        p = _prep_params(l, g_norm1, g_norm2, w_in, w_gk_fwd, b_gk_fwd, w_gk_bwd, b_gk_bwd,
                         g_gla_out, g_q_lora, w_uq, g_kv_lora, w_ukv, g_qk_q, g_qk_k, w_out,
                         w_router, b_router, w_sh_gate, w_sh_up, w_sh_down)
        mod = _ada(cvecs, w_ada[l], b_ada[l]).reshape(8, 6, d)
        mod_p, mod_s = mod[0:1], mod[1:1 + bs]
        zeros = jnp.zeros((bp, GLA_KEY_W, GLA_DV), F32)
        r_ctx, sf, sb, ckvn, kr = _mix_route_dispatch(
            xp, (bp, lp), mod_p, False, p, zeros, zeros, ctx_tabs, None, tm, lp)
        new_f.append(sf.reshape(bp, GLA_HEADS, GLA_DK, GLA_DV))
        new_b.append(sb.reshape(bp, GLA_HEADS, GLA_DK, GLA_DV))
        new_ckv.append(ckvn.reshape(bp, lp, KV_LORA))
        new_kr.append(kr.reshape(bp, lp, LANES)[:, :, KR_LANE0:KR_LANE0 + ROPE_DIM])
        kr_cache = jnp.pad(cache_mla_krope[:, l],
                           ((0, 0), (0, 0), (ROPE_LANE0, LANES - ROPE_LANE0 - ROPE_DIM)))
        ctx_kv = _cache_kv(cache_mla_ckv[:, l], kr_cache, p)
        s0f = state_gla_fwd[:, l].reshape(bs, GLA_KEY_W, GLA_DV)
        s0b = state_gla_bwd[:, l].reshape(bs, GLA_KEY_W, GLA_DV)
        r_lat, _, _, _, _ = _mix_route_dispatch(
            xs, (bs, ls), mod_s, True, p, s0f, s0b, lat_tabs, ctx_kv, tm, 256)
        experts = (w_exp_gate[l], w_exp_up[l], w_exp_down[l])
        yg_ctx = _experts_collect(r_ctx, *experts)
        yg_lat = _experts_collect(r_lat, *experts)
        xp = _combine(r_ctx['x1'], r_ctx['h2'], yg_ctx, r_ctx['w8'], mod_p, False, p, tm)
        xs = _combine(r_lat['x1'], r_lat['h2'], yg_lat, r_lat['w8'], mod_s, True, p, tm)
    return (xp.reshape(bp, lp, d), xs, jnp.stack(new_f, axis=1), jnp.stack(new_b, axis=1),
            jnp.stack(new_ckv, axis=1), jnp.stack(new_kr, axis=1))
```

```python
import functools

import jax
import jax.numpy as jnp
import numpy as np
from jax import lax
from jax.experimental import pallas as pl
from jax.experimental.pallas import tpu as pltpu
from jax.experimental.pallas import tpu_sc as plsc

F32 = jnp.float32
BF16 = jnp.bfloat16
I32 = jnp.int32
U32 = jnp.uint32

D_MODEL = 1024
EPS = 1e-6
GRID_W = 64
GLA_HEADS = 4
GLA_DK = 64
GLA_DV = 128
GLA_GATE_RANK = 16
GLA_GATE_NORM = 16.0
GLA_KEY_W = GLA_HEADS * GLA_DK
GLA_VAL_W = GLA_HEADS * GLA_DV
MLA_HEADS = 4
Q_LORA = 256
KV_LORA = 128
NOPE_DIM = 64
ROPE_DIM = 32
V_DIM = 128
QK_DIM = NOPE_DIM + ROPE_DIM
ROPE_THETA = 10000.0
N_EXPERTS = 64
TOP_K = 8
N_GROUPS = 8
TOPK_GROUPS = 4
D_EXPERT = 256
ROUTED_SCALE = 2.5

LANES = 128
HEAD_W = LANES
MLA_W = MLA_HEADS * HEAD_W
ROPE_LANE0 = NOPE_DIM
COL_V = 2 * GLA_KEY_W
COL_G = COL_V + GLA_VAL_W
COL_CQ = COL_G + GLA_VAL_W
COL_CKV = COL_CQ + Q_LORA
TAIL0 = COL_CKV + KV_LORA
IN_W = TAIL0 + LANES
KR_LANE0 = LANES - ROPE_DIM - 2 * GLA_GATE_RANK
GLA_BLOCK = 256
GLA_SUB = 64
FFN_TILE = 512
PACK_PARTS = 2
PACK_W = D_MODEL // (2 * PACK_PARTS)
SC_WINDOW = 128
VMEM_LIMIT = 56 * 1024 * 1024

HIGHEST = lax.Precision.HIGHEST


def _dot(a, b, precision=None):
    return jnp.dot(a, b, preferred_element_type=F32, precision=precision)


def _dot_nt(a, b, precision=None):
    return lax.dot_general(a, b, (((1,), (1,)), ((), ())), preferred_element_type=F32,
                           precision=precision)


def _rms(x, width):
    ss = jnp.sum(x * x, axis=-1, keepdims=True) * (1.0 / width)
    return x * lax.rsqrt(ss + EPS)


def _silu(x):
    return x * jax.nn.sigmoid(x)


def _log_sigmoid(x):
    return jnp.minimum(x, 0.0) - jnp.log1p(jnp.exp(-jnp.abs(x)))


def _pack_rows(x):
    parts = []
    for i in range(PACK_PARTS):
        c0 = i * 2 * PACK_W
        lo = lax.bitcast_convert_type(x[:, c0:c0 + PACK_W].astype(BF16).astype(F32), U32)
        hi = lax.bitcast_convert_type(
            x[:, c0 + PACK_W:c0 + 2 * PACK_W].astype(BF16).astype(F32), U32)
        parts.append(hi | (lo >> 16))
    return parts


def _unpack_rows(parts):
    cols = []
    for w in parts:
        cols.append(lax.bitcast_convert_type(w << 16, F32))
        cols.append(lax.bitcast_convert_type(w & jnp.uint32(0xFFFF0000), F32))
    return jnp.concatenate(cols, axis=1)


def _ada_kernel(c_ref, w_ref, b_ref, o_ref):
    o_ref[...] = _dot(_silu(c_ref[...]), w_ref[...], precision=HIGHEST) + b_ref[...]


def _ada(cvecs, w_ada, b_ada):
    n = w_ada.shape[1]
    tn = 768
    return pl.pallas_call(
        _ada_kernel,
        grid=(n // tn,),
        in_specs=[pl.BlockSpec((8, D_MODEL), lambda j: (0, 0)),
                  pl.BlockSpec((D_MODEL, tn), lambda j: (0, j)),
                  pl.BlockSpec((1, tn), lambda j: (0, j))],
        out_specs=pl.BlockSpec((8, tn), lambda j: (0, j)),
        out_shape=jax.ShapeDtypeStruct((8, n), F32),
        name="ada",
    )(cvecs, w_ada, b_ada.reshape(1, n))


def _rope(x, c, s1, s2):
    return x * c + pltpu.roll(x, LANES - 8, 1) * s1 + pltpu.roll(x, 8, 1) * s2


def _mla_kv(ckv_n, kr, w_ukv_ref, gk, rope_tabs, k_ref, v_ref):
    kv = _dot(ckv_n.astype(BF16), w_ukv_ref[...])
    for h in range(MLA_HEADS):
        k_h = kv[:, h * HEAD_W:(h + 1) * HEAD_W] + kr
        k_h = _rms(k_h, QK_DIM) * gk
        if rope_tabs is not None:
            k_h = _rope(k_h, *rope_tabs)
        k_ref[0, :, h * HEAD_W:(h + 1) * HEAD_W] = k_h.astype(BF16)
    v_ref[0] = kv[:, MLA_W:].astype(BF16)


def _lane_mask(lo, hi, rows):
    lane = lax.broadcasted_iota(I32, (rows, LANES), 1)
    return (lane >= lo) & (lane < hi)


def _inproj_kernel(x_ref, mod_ref, g1_ref, win_ref, wgk_ref, bgk_ref, gql_ref, wuq_ref,
                   gkv_ref, wukv_ref, gq_ref, gk_ref, rc_ref, rs1_ref, rs2_ref,
                   q_ref, k_ref, v_ref, g_ref, lgf_ref, lgb_ref,
                   qm_ref, km_ref, vm_ref, ckvn_ref, kr_ref):
    x = x_ref[0]
    mod = mod_ref[0]
    h = _rms(x, D_MODEL) * g1_ref[...] * (1.0 + mod[1:2]) + mod[0:1]
    y = _dot(h.astype(BF16), win_ref[...])
    q_ref[0] = (y[:, 0:GLA_KEY_W] * GLA_DK ** -0.5).astype(BF16)
    k_ref[0] = y[:, GLA_KEY_W:COL_V].astype(BF16)
    v_ref[0] = y[:, COL_V:COL_G].astype(BF16)
    g_ref[0] = y[:, COL_G:COL_CQ].astype(BF16)
    tail = y[:, TAIL0:IN_W]
    pre = _dot(tail.astype(BF16), wgk_ref[...]) + bgk_ref[...]
    logg = _log_sigmoid(pre) * (1.0 / GLA_GATE_NORM)
    lgf_ref[0] = logg[:, 0:GLA_KEY_W]
    lgb_ref[0] = logg[:, GLA_KEY_W:]
    rows = x.shape[0]
    tabs = (rc_ref[...], rs1_ref[...], rs2_ref[...])
    cq = _rms(y[:, COL_CQ:COL_CKV], Q_LORA) * gql_ref[...]
    qm = _dot(cq.astype(BF16), wuq_ref[...])
    gq = gq_ref[...]
    for hh in range(MLA_HEADS):
        q_h = _rms(qm[:, hh * HEAD_W:(hh + 1) * HEAD_W], QK_DIM) * gq
        q_h = _rope(q_h, *tabs) * QK_DIM ** -0.5
        qm_ref[0, :, hh * HEAD_W:(hh + 1) * HEAD_W] = q_h.astype(BF16)
    ckv_n = _rms(y[:, COL_CKV:TAIL0], KV_LORA) * gkv_ref[...]
    ckvn_ref[0] = ckv_n
    kr_ref[0] = tail
    kr = jnp.where(_lane_mask(KR_LANE0, KR_LANE0 + ROPE_DIM, rows), tail, 0.0)
    _mla_kv(ckv_n, kr, wukv_ref, gk_ref[...], tabs, km_ref, vm_ref)


def _inproj(x, mod, per_batch_mod, p, rope_tabs, tm):
    b, l, d = x.shape
    nt = l // tm
    tab_blocks = rope_tabs[0].shape[0] // tm
    mod_map = (lambda bi, i: (bi, 0, 0)) if per_batch_mod else (lambda bi, i: (0, 0, 0))
    tab_map = (lambda bi, i: (i, 0)) if tab_blocks > 1 else (lambda bi, i: (0, 0))
    const = lambda bi, i: (0, 0)
    tok = lambda w: pl.BlockSpec((1, tm, w), lambda bi, i: (bi, i, 0))
    full = lambda a: pl.BlockSpec(a.shape, const)
    weights = [p['g_norm1'], p['w_in_p'], p['w_gk_big'], p['b_gk'], p['g_q_lora'], p['w_uq_p'],
               p['g_kv_lora'], p['w_ukv_p'], p['gq'], p['gk']]
    outs = [(GLA_KEY_W, BF16), (GLA_KEY_W, BF16), (GLA_VAL_W, BF16), (GLA_VAL_W, BF16),
            (GLA_KEY_W, F32), (GLA_KEY_W, F32), (MLA_W, BF16), (MLA_W, BF16), (MLA_W, BF16),
            (KV_LORA, F32), (LANES, F32)]
    return pl.pallas_call(
        _inproj_kernel,
        grid=(b, nt),
        in_specs=[tok(d), pl.BlockSpec((1, 6, d), mod_map)] + [full(w) for w in weights]
                 + [pl.BlockSpec((tm, LANES), tab_map)] * 3,
        out_specs=[tok(w) for w, _ in outs],
        out_shape=[jax.ShapeDtypeStruct((b, l, w), dt) for w, dt in outs],
        compiler_params=pltpu.CompilerParams(
            dimension_semantics=("parallel", "parallel"), vmem_limit_bytes=VMEM_LIMIT),
        name="inproj",
    )(x, mod, *weights, *rope_tabs)


def _cache_kv_kernel(ckv_ref, kr_ref, wukv_ref, gk_ref, k_ref, v_ref):
    _mla_kv(ckv_ref[0], kr_ref[0], wukv_ref, gk_ref[...], None, k_ref, v_ref)


def _cache_kv(ckv, kr128, p):
    b, l, _ = ckv.shape
    tok = lambda w: pl.BlockSpec((1, l, w), lambda bi: (bi, 0, 0))
    full = lambda a: pl.BlockSpec(a.shape, lambda bi: (0, 0))
    return pl.pallas_call(
        _cache_kv_kernel,
        grid=(b,),
        in_specs=[tok(KV_LORA), tok(LANES), full(p['w_ukv_p']), full(p['gk'])],
        out_specs=[tok(MLA_W), tok(MLA_W)],
        out_shape=[jax.ShapeDtypeStruct((b, l, MLA_W), BF16)] * 2,
        name="cache_kv",
    )(ckv, kr128, p['w_ukv_p'], p['gk'])


def _split3(x):
    hi = x.astype(BF16)
    r1 = x - hi.astype(F32)
    mid = r1.astype(BF16)
    lo = (r1 - mid.astype(F32)).astype(BF16)
    return hi, mid, lo


def _gla_block(q, k, v, lg, s, fwd):
    n = GLA_BLOCK
    nc = n // GLA_SUB
    row = lax.broadcasted_iota(I32, (n, n), 0)
    col = lax.broadcasted_iota(I32, (n, n), 1)
    same = (row // GLA_SUB) == (col // GLA_SUB)
    causal = same & ((col <= row) if fwd else (col >= row))
    tri = causal.astype(BF16)
    hi, mid, lo = _split3(lg)
    cum = _dot(tri, hi) + _dot(tri, mid) + _dot(tri, lo)
    tot_rows, mid_rows = [], []
    for c in range(nc):
        r_tot = c * GLA_SUB + (GLA_SUB - 1 if fwd else 0)
        r_mid = c * GLA_SUB + GLA_SUB // 2
        tot_rows.append(jnp.broadcast_to(cum[r_tot:r_tot + 1], (GLA_SUB, GLA_KEY_W)))
        mid_rows.append(jnp.broadcast_to(cum[r_mid:r_mid + 1], (GLA_SUB, GLA_KEY_W)))
    tot_rows = jnp.concatenate(tot_rows, axis=0)
    mid_rows = jnp.concatenate(mid_rows, axis=0)
    rel = cum - mid_rows
    qi = q * jnp.exp(rel)
    ki = (k * jnp.exp(-rel)).astype(BF16)
    q_in = (q * jnp.exp(cum)).astype(BF16)
    k_up = k * jnp.exp(tot_rows - cum)
    dec = jnp.exp(tot_rows)
    k_up_t = k_up.T.astype(BF16)
    dec_t = dec.T
    lane_head = lax.broadcasted_iota(I32, (n, GLA_KEY_W), 1) // GLA_DK
    srow_head = lax.broadcasted_iota(I32, (GLA_KEY_W, GLA_DV), 0) // GLA_DK
    vrow_chunk = lax.broadcasted_iota(I32, (n, GLA_VAL_W), 0) // GLA_SUB
    o_heads = []
    for h in range(GLA_HEADS):
        a = _dot_nt(jnp.where(lane_head == h, qi, 0.0).astype(BF16), ki)
        a = jnp.where(causal, a, 0.0).astype(BF16)
        o_heads.append(_dot(a, v[:, h * GLA_DV:(h + 1) * GLA_DV]))
    o = jnp.concatenate(o_heads, axis=1)
    o_inter = [None] * nc
    for c in (range(nc) if fwd else range(nc - 1, -1, -1)):
        s_bd = jnp.concatenate(
            [jnp.where(srow_head == h, s, 0.0).astype(BF16) for h in range(GLA_HEADS)], axis=1)
        o_inter[c] = _dot(q_in[c * GLA_SUB:(c + 1) * GLA_SUB], s_bd)
        v_c = jnp.where(vrow_chunk == c, v, jnp.zeros_like(v))
        u = jnp.concatenate(
            [_dot(k_up_t[h * GLA_DK:(h + 1) * GLA_DK], v_c[:, h * GLA_DV:(h + 1) * GLA_DV])
             for h in range(GLA_HEADS)], axis=0)
        s = dec_t[:, c * GLA_SUB:c * GLA_SUB + 1] * s + u
    return o + jnp.concatenate(o_inter, axis=0), s


def _gla_kernel(q_ref, k_ref, v_ref, g_ref, lgf_ref, lgb_ref, s0f_ref, s0b_ref, gout_ref,
                o_ref, sf_ref, sb_ref, acc_ref, st_ref, *, nblk):
    st_ref[0] = s0f_ref[0]
    st_ref[1] = s0b_ref[0]

    def load(blk):
        r = pl.ds(pl.multiple_of(blk * GLA_BLOCK, GLA_BLOCK), GLA_BLOCK)
        return r, q_ref[0, r, :].astype(F32), k_ref[0, r, :].astype(F32), v_ref[0, r, :]

    def fwd_step(blk, carry):
        r, q, k, v = load(blk)
        o, s = _gla_block(q, k, v, lgf_ref[0, r, :], st_ref[0], True)
        acc_ref[r, :] = o
        st_ref[0] = s
        return carry

    def bwd_step(i, carry):
        r, q, k, v = load(nblk - 1 - i)
        o, s = _gla_block(q, k, v, lgb_ref[0, r, :], st_ref[1], False)
        acc_ref[r, :] += o
        st_ref[1] = s
        return carry

    lax.fori_loop(0, nblk, fwd_step, 0)
    lax.fori_loop(0, nblk, bwd_step, 0)
    sf_ref[0] = st_ref[0]
    sb_ref[0] = st_ref[1]

    def fin_step(blk, carry):
        r = pl.ds(pl.multiple_of(blk * GLA_BLOCK, GLA_BLOCK), GLA_BLOCK)
        o = acc_ref[r, :]
        gate = _silu(g_ref[0, r, :].astype(F32))
        for h in range(GLA_HEADS):
            sl = slice(h * GLA_DV, (h + 1) * GLA_DV)
            o_ref[0, r, sl] = (_rms(o[:, sl], GLA_DV) * gout_ref[...] * gate[:, sl]).astype(BF16)
        return carry

    lax.fori_loop(0, nblk, fin_step, 0)


def _gla(q, k, v, g, lgf, lgb, s0f, s0b, g_out):
    b, l, _ = q.shape
    seq = lambda w: pl.BlockSpec((1, l, w), lambda bi: (bi, 0, 0))
    st = pl.BlockSpec((1, GLA_KEY_W, GLA_DV), lambda bi: (bi, 0, 0))
    return pl.pallas_call(
        functools.partial(_gla_kernel, nblk=l // GLA_BLOCK),
        grid=(b,),
        in_specs=[seq(GLA_KEY_W), seq(GLA_KEY_W), seq(GLA_VAL_W), seq(GLA_VAL_W),
                  seq(GLA_KEY_W), seq(GLA_KEY_W), st, st,
                  pl.BlockSpec((1, GLA_DV), lambda bi: (0, 0))],
        out_specs=[seq(GLA_VAL_W), st, st],
        out_shape=[jax.ShapeDtypeStruct((b, l, GLA_VAL_W), BF16),
                   jax.ShapeDtypeStruct((b, GLA_KEY_W, GLA_DV), F32),
                   jax.ShapeDtypeStruct((b, GLA_KEY_W, GLA_DV), F32)],
        scratch_shapes=[pltpu.VMEM((l, GLA_VAL_W), F32),
                        pltpu.VMEM((2, GLA_KEY_W, GLA_DV), F32)],
        compiler_params=pltpu.CompilerParams(
            dimension_semantics=("parallel",), vmem_limit_bytes=VMEM_LIMIT),
        name="gla",
    )(q, k, v, g, lgf, lgb, s0f, s0b, g_out)


def _attn_kernel(*refs, n_kv):
    q_ref, o_ref = refs[0], refs[-1]
    kv = [(refs[1 + 2 * i], refs[2 + 2 * i]) for i in range(n_kv)]
    for h in range(MLA_HEADS):
        sl = slice(h * HEAD_W, (h + 1) * HEAD_W)
        q = q_ref[0, :, sl]
        scores = [_dot_nt(q, k_ref[0, :, sl]) for k_ref, _ in kv]
        m = functools.reduce(jnp.maximum, [jnp.max(s, axis=-1, keepdims=True) for s in scores])
        o, den = 0.0, 0.0
        for s, (_, v_ref) in zip(scores, kv):
            p = jnp.exp(s - m)
            den = den + jnp.sum(p, axis=-1, keepdims=True)
            o = o + _dot(p.astype(BF16), v_ref[0, :, sl])
        o_ref[0, :, sl] = (o / den).astype(BF16)


def _attn(q, kv_pairs, tq):
    b, l, _ = q.shape
    kv_specs, kv_args = [], []
    for k, v in kv_pairs:
        spec = pl.BlockSpec((1, k.shape[1], MLA_W), lambda bi, i: (bi, 0, 0))
        kv_specs += [spec, spec]
        kv_args += [k, v]
    return pl.pallas_call(
        functools.partial(_attn_kernel, n_kv=len(kv_pairs)),
        grid=(b, l // tq),
        in_specs=[pl.BlockSpec((1, tq, MLA_W), lambda bi, i: (bi, i, 0))] + kv_specs,
        out_specs=pl.BlockSpec((1, tq, MLA_W), lambda bi, i: (bi, i, 0)),
        out_shape=jax.ShapeDtypeStruct((b, l, MLA_W), BF16),
        compiler_params=pltpu.CompilerParams(
            dimension_semantics=("parallel", "parallel"), vmem_limit_bytes=VMEM_LIMIT),
        name="attn",
    )(q, *kv_args)


def _sum01(x):
    return jnp.sum(jnp.sum(x, axis=1, keepdims=True), axis=0, keepdims=True)


def _route(logits_t, bias_col, cnt_col, cap):
    t = logits_t.shape[1]
    gsz = N_EXPERTS // N_GROUPS
    scores = jax.nn.sigmoid(logits_t)
    sel = (scores + bias_col).reshape(N_GROUPS, gsz, t)
    scores = scores.reshape(N_GROUPS, gsz, t)
    neg = jnp.float32(-jnp.inf)
    ie = lax.broadcasted_iota(I32, (N_GROUPS, gsz, t), 1)
    ig = lax.broadcasted_iota(I32, (N_GROUPS, gsz, t), 0)
    m1 = jnp.max(sel, axis=1, keepdims=True)
    first = jnp.min(jnp.where(sel == m1, ie, gsz), axis=1, keepdims=True)
    m2 = jnp.max(jnp.where(ie == first, neg, sel), axis=1, keepdims=True)
    grp = m1 + m2
    igk = lax.broadcasted_iota(I32, (N_GROUPS, 1, t), 0)
    g_sel = jnp.zeros((N_GROUPS, 1, t), jnp.bool_)
    cur = grp
    for _ in range(TOPK_GROUPS):
        m = jnp.max(cur, axis=0, keepdims=True)
        pick = igk == jnp.min(jnp.where(cur == m, igk, N_GROUPS), axis=0, keepdims=True)
        g_sel = g_sel | pick
        cur = jnp.where(pick, neg, cur)
    cur = jnp.where(g_sel, sel, neg)
    idx = ig * gsz + ie
    e_sel = jnp.zeros((N_GROUPS, gsz, t), jnp.bool_)
    picks = []
    for _ in range(TOP_K):
        m = jnp.max(jnp.max(cur, axis=1, keepdims=True), axis=0, keepdims=True)
        cand = jnp.where(cur == m, idx, N_EXPERTS)
        pick = idx == jnp.min(jnp.min(cand, axis=1, keepdims=True), axis=0, keepdims=True)
        picks.append(pick)
        e_sel = e_sel | pick
        cur = jnp.where(pick, neg, cur)
    w = jnp.where(e_sel, scores, 0.0)
    gate = w / _sum01(w) * ROUTED_SCALE
    sel_f = e_sel.astype(F32).reshape(N_EXPERTS, t)
    earlier = (lax.broadcasted_iota(I32, (t, t), 0) < lax.broadcasted_iota(I32, (t, t), 1))
    rank = _dot(sel_f.astype(BF16), earlier.astype(BF16))
    base = lax.broadcasted_iota(I32, (N_EXPERTS, 1), 0).astype(F32) * float(cap) + cnt_col
    slot = (base + rank).reshape(N_GROUPS, gsz, t)
    slot8 = jnp.concatenate([_sum01(jnp.where(pk, slot, 0.0)).reshape(1, t) for pk in picks],
                            axis=0).astype(I32)
    w8 = jnp.concatenate([_sum01(jnp.where(pk, gate, 0.0)).reshape(1, t) for pk in picks], axis=0)
    return slot8, w8, cnt_col + jnp.sum(sel_f, axis=1, keepdims=True)


def _outproj_kernel(og_ref, om_ref, x_ref, mod_ref, wtop_ref, wbot_ref, g2_ref, wr_ref, br_ref,
                    x1_ref, h2pa_ref, h2pb_ref, slot_ref, w8_ref, cnt_ref, cnt_scr, *, cap):
    @pl.when((pl.program_id(0) == 0) & (pl.program_id(1) == 0))
    def _():
        cnt_scr[...] = jnp.zeros_like(cnt_scr)

    mod = mod_ref[0]
    mix = _dot(og_ref[0], wtop_ref[...]) + _dot(om_ref[0], wbot_ref[...])
    x1 = x_ref[0] + mod[2:3] * mix
    x1_ref[0] = x1
    h2 = _rms(x1, D_MODEL) * g2_ref[...] * (1.0 + mod[4:5]) + mod[3:4]
    h2pa_ref[0], h2pb_ref[0] = _pack_rows(h2)
    logits_t = _dot_nt(wr_ref[...], h2, precision=HIGHEST)
    slot8, w8, cnt = _route(logits_t, br_ref[...], cnt_scr[:, 0:1], cap)
    slot_ref[...] = slot8
    t = w8.shape[1]
    w8_ref[0] = jnp.concatenate([w8, jnp.zeros((LANES - TOP_K, t), F32)], axis=0).T
    cnt_scr[...] = jnp.broadcast_to(cnt, cnt_scr.shape)
    cnt_ref[...] = cnt_scr[...]


def _outproj(og, om, x, mod, per_batch_mod, p, tm):
    b, l, d = x.shape
    nt = l // tm
    mod_map = (lambda bi, i: (bi, 0, 0)) if per_batch_mod else (lambda bi, i: (0, 0, 0))
    tok = lambda w: pl.BlockSpec((1, tm, w), lambda bi, i: (bi, i, 0))
    full = lambda a: pl.BlockSpec(a.shape, lambda bi, i: (0, 0))
    weights = [p['w_out_top'], p['w_out_bot'], p['g_norm2'], p['w_router_t'], p['b_router_col']]
    cnt_shape = (N_EXPERTS, LANES)
    return pl.pallas_call(
        functools.partial(_outproj_kernel, cap=b * l),
        grid=(b, nt),
        in_specs=[tok(GLA_VAL_W), tok(MLA_W), tok(d), pl.BlockSpec((1, 6, d), mod_map)]
                 + [full(w) for w in weights],
        out_specs=[tok(d), tok(PACK_W), tok(PACK_W),
                   pl.BlockSpec((TOP_K, tm), lambda bi, i: (0, bi * nt + i)),
                   tok(LANES), pl.BlockSpec(cnt_shape, lambda bi, i: (0, 0))],
        out_shape=[jax.ShapeDtypeStruct((b, l, d), F32), jax.ShapeDtypeStruct((b, l, PACK_W), U32),
                   jax.ShapeDtypeStruct((b, l, PACK_W), U32),
                   jax.ShapeDtypeStruct((TOP_K, b * l), I32),
                   jax.ShapeDtypeStruct((b, l, LANES), F32),
                   jax.ShapeDtypeStruct(cnt_shape, F32)],
        scratch_shapes=[pltpu.VMEM(cnt_shape, F32)],
        compiler_params=pltpu.CompilerParams(
            dimension_semantics=("arbitrary", "arbitrary"), vmem_limit_bytes=VMEM_LIMIT),
        name="outproj",
    )(og, om, x, mod, *weights)


def _sc_mesh():
    return plsc.VectorSubcoreMesh(core_axis_name="c", subcore_axis_name="s")


def _sc_scatter_rows(src, slot8, n_rows_out):
    t, d = src.shape
    nk = slot8.shape[0]

    @functools.partial(pl.kernel, out_type=jax.ShapeDtypeStruct((n_rows_out, d), src.dtype),
                       mesh=_sc_mesh(), scratch_types=[], name="dispatch")
    def run(src_hbm, slot_hbm, out_hbm):
        def body(x_vmem, i_vmem):
            for k in range(nk):
                pltpu.sync_copy(x_vmem, out_hbm.at[i_vmem.at[k]])

        pltpu.emit_pipeline(
            body, grid=(t // SC_WINDOW,),
            in_specs=[pl.BlockSpec((SC_WINDOW, d), lambda i: (i, 0)),
                      pl.BlockSpec((nk, SC_WINDOW), lambda i: (0, i))],
            out_specs=[], core_axis_name=("c", "s"),
            dimension_semantics=(pltpu.PARALLEL,))(src_hbm, slot_hbm)

    return run(src, slot8)


def _sc_gather_rows(table, slots):
    n = slots.shape[1]
    d = table.shape[1]

    @functools.partial(pl.kernel, out_type=jax.ShapeDtypeStruct((n, d), table.dtype),
                       mesh=_sc_mesh(), scratch_types=[], name="collect")
    def run(tab_hbm, slot_hbm, out_hbm):
        def body(i_vmem, o_vmem):
            pltpu.sync_copy(tab_hbm.at[i_vmem.at[0]], o_vmem)

        pltpu.emit_pipeline(
            body, grid=(n // SC_WINDOW,),
            in_specs=[pl.BlockSpec((1, SC_WINDOW), lambda i: (0, i))],
            out_specs=[pl.BlockSpec((SC_WINDOW, d), lambda i: (i, 0))],
            core_axis_name=("c", "s"),
            dimension_semantics=(pltpu.PARALLEL,))(slot_hbm, out_hbm)

    return run(table, slots)


def _ffn_kernel(nt_ref, first_ref, nxt_ref, xsa_hbm, xsb_hbm, wg_ref, wu_ref, wd_ref,
                ysa_hbm, ysb_hbm, xbuf, ybuf, wg_b, wu_b, wd_b, sem_in, sem_out, *, cap):
    e = pl.program_id(0)
    n_exp = pl.num_programs(0)
    n = nt_ref[e]
    g0 = first_ref[e]
    total = first_ref[n_exp - 1] + nt_ref[n_exp - 1]
    xs_hbm = (xsa_hbm, xsb_hbm)
    ys_hbm = (ysa_hbm, ysb_hbm)

    def rows(ex, t):
        return pl.ds(pl.multiple_of(ex * cap + t * FFN_TILE, FFN_TILE), FFN_TILE)

    def in_copy(ex, t, slot, part):
        return pltpu.make_async_copy(xs_hbm[part].at[rows(ex, t)], xbuf.at[slot, part],
                                     sem_in.at[slot, part])

    def out_copy(ex, t, slot, part):
        return pltpu.make_async_copy(ybuf.at[slot, part], ys_hbm[part].at[rows(ex, t)],
                                     sem_out.at[slot, part])

    def wait_out(slot):
        for part in range(PACK_PARTS):
            out_copy(e, 0, slot, part).wait()

    @pl.when(n > 0)
    def _():
        @pl.when(g0 == 0)
        def _():
            for part in range(PACK_PARTS):
                in_copy(e, 0, 0, part).start()

        wg_b[...] = wg_ref[0].astype(BF16)
        wu_b[...] = wu_ref[0].astype(BF16)
        wd_b[...] = wd_ref[0].astype(BF16)
        nxt = nxt_ref[e]

        def tile(t, carry):
            g = g0 + t
            slot = g % 2
            for part in range(PACK_PARTS):
                in_copy(e, t, slot, part).wait()

            @pl.when(t + 1 < n)
            def _():
                for part in range(PACK_PARTS):
                    in_copy(e, t + 1, 1 - slot, part).start()

            @pl.when((t + 1 == n) & (nxt < n_exp))
            def _():
                for part in range(PACK_PARTS):
                    in_copy(nxt, 0, 1 - slot, part).start()

            @pl.when(g >= 2)
            def _():
                wait_out(slot)

            x = _unpack_rows([xbuf[slot, part] for part in range(PACK_PARTS)]).astype(BF16)
            a = _silu(_dot(x, wg_b[...])) * _dot(x, wu_b[...])
            y = _pack_rows(_dot(a.astype(BF16), wd_b[...]))
            for part in range(PACK_PARTS):
                ybuf[slot, part] = y[part]
                out_copy(e, t, slot, part).start()
            return carry

        lax.fori_loop(0, n, tile, 0)

        @pl.when(g0 + n == total)
        def _():
            @pl.when(total >= 2)
            def _():
                wait_out(total % 2)

            wait_out((total - 1) % 2)


def _ffn(xs_parts, counts, w_gate, w_up, w_down):
    n_exp, d = w_gate.shape[0], w_gate.shape[1]
    cap = xs_parts[0].shape[0] // n_exp
    ntiles = (counts + FFN_TILE - 1) // FFN_TILE
    first = jnp.cumsum(ntiles) - ntiles
    ids = jnp.arange(n_exp, dtype=I32)
    later_busy = (ids[None, :] > ids[:, None]) & (ntiles[None, :] > 0)
    nxt = jnp.min(jnp.where(later_busy, ids[None, :], n_exp), axis=1).astype(I32)
    wmap = lambda e, nt, first, nxt: (e, 0, 0)
    hbm = pl.BlockSpec(memory_space=pl.ANY)
    return pl.pallas_call(
        functools.partial(_ffn_kernel, cap=cap),
        grid_spec=pltpu.PrefetchScalarGridSpec(
            num_scalar_prefetch=3, grid=(n_exp,),
            in_specs=[hbm] * PACK_PARTS
                     + [pl.BlockSpec((1, d, D_EXPERT), wmap), pl.BlockSpec((1, d, D_EXPERT), wmap),
                        pl.BlockSpec((1, D_EXPERT, d), wmap)],
            out_specs=[hbm] * PACK_PARTS,
            scratch_shapes=[pltpu.VMEM((2, PACK_PARTS, FFN_TILE, PACK_W), U32),
                            pltpu.VMEM((2, PACK_PARTS, FFN_TILE, PACK_W), U32),
                            pltpu.VMEM((d, D_EXPERT), BF16), pltpu.VMEM((d, D_EXPERT), BF16),
                            pltpu.VMEM((D_EXPERT, d), BF16),
                            pltpu.SemaphoreType.DMA((2, PACK_PARTS)),
                            pltpu.SemaphoreType.DMA((2, PACK_PARTS))]),
        out_shape=[jax.ShapeDtypeStruct(xs_parts[0].shape, U32)] * PACK_PARTS,
        compiler_params=pltpu.CompilerParams(
            dimension_semantics=("arbitrary",), vmem_limit_bytes=VMEM_LIMIT),
        name="ffn",
    )(ntiles.astype(I32), first.astype(I32), nxt, *xs_parts, w_gate, w_up, w_down)


def _combine_kernel(x1_ref, h2pa_ref, h2pb_ref, yga_ref, ygb_ref, w8_ref, mod_ref,
                    wsg_ref, wsu_ref, wsd_ref, o_ref):
    h2 = _unpack_rows([h2pa_ref[0], h2pb_ref[0]]).astype(BF16)
    a = _silu(_dot(h2, wsg_ref[...])) * _dot(h2, wsu_ref[...])
    acc = _dot(a.astype(BF16), wsd_ref[...])
    w8 = w8_ref[0]
    for k in range(TOP_K):
        acc += w8[:, k:k + 1] * _unpack_rows([yga_ref[k], ygb_ref[k]])
    o_ref[0] = x1_ref[0] + mod_ref[0][5:6] * acc


def _combine(x1, h2p_parts, yg_parts, w8, mod, per_batch_mod, p, tm):
    b, l, d = x1.shape
    nt = l // tm
    mod_map = (lambda bi, i: (bi, 0, 0)) if per_batch_mod else (lambda bi, i: (0, 0, 0))
    tok = lambda w: pl.BlockSpec((1, tm, w), lambda bi, i: (bi, i, 0))
    full = lambda a: pl.BlockSpec(a.shape, lambda bi, i: (0, 0))
    weights = [p['w_sh_gate'], p['w_sh_up'], p['w_sh_down']]
    gathered = pl.BlockSpec((TOP_K, tm, PACK_W), lambda bi, i: (0, bi * nt + i, 0))
    return pl.pallas_call(
        _combine_kernel,
        grid=(b, nt),
        in_specs=[tok(d)] + [tok(PACK_W)] * PACK_PARTS + [gathered] * PACK_PARTS
                 + [tok(LANES), pl.BlockSpec((1, 6, d), mod_map)] + [full(w) for w in weights],
        out_specs=tok(d),
        out_shape=jax.ShapeDtypeStruct((b, l, d), F32),
        compiler_params=pltpu.CompilerParams(
            dimension_semantics=("parallel", "parallel"), vmem_limit_bytes=VMEM_LIMIT),
        name="combine",
    )(x1, *h2p_parts, *yg_parts, w8, mod, *weights)


def _pad_heads(w, parts):
    k = w.shape[0]
    per = w.shape[1] // MLA_HEADS
    w = w.reshape(k, MLA_HEADS, per)[:, :, parts[0]:parts[1]]
    w = jnp.pad(w, ((0, 0), (0, 0), (0, HEAD_W - (parts[1] - parts[0]))))
    return w.reshape(k, MLA_HEADS * HEAD_W)


def _prep_params(l, g_norm1, g_norm2, w_in, w_gk_fwd, b_gk_fwd, w_gk_bwd, b_gk_bwd, g_gla_out,
                 g_q_lora, w_uq, g_kv_lora, w_ukv, g_qk_q, g_qk_k, w_out, w_router, b_router,
                 w_sh_gate, w_sh_up, w_sh_down):
    w = w_in[l]
    d = w.shape[0]
    o_lrf = COL_CQ
    o_lrb = o_lrf + GLA_GATE_RANK
    o_cq = o_lrb + GLA_GATE_RANK
    o_ckv = o_cq + Q_LORA
    o_kr = o_ckv + KV_LORA
    w_in_p = jnp.concatenate([
        w[:, 0:COL_CQ], w[:, o_cq:o_ckv], w[:, o_ckv:o_kr], jnp.zeros((d, KR_LANE0), w.dtype),
        w[:, o_kr:o_kr + ROPE_DIM], w[:, o_lrf:o_lrb], w[:, o_lrb:o_cq]], axis=1).astype(BF16)
    z = jnp.zeros((GLA_GATE_RANK, GLA_KEY_W), F32)
    w_gk_big = jnp.concatenate([
        jnp.zeros((LANES - 2 * GLA_GATE_RANK, 2 * GLA_KEY_W), F32),
        jnp.concatenate([w_gk_fwd[l], z], axis=1),
        jnp.concatenate([z, w_gk_bwd[l]], axis=1)], axis=0).astype(BF16)
    pad_gain = lambda g: jnp.pad(g, (0, HEAD_W - QK_DIM)).reshape(1, HEAD_W)
    w_ukv_h = w_ukv[l]
    return {
        'g_norm1': g_norm1[l].reshape(1, d), 'g_norm2': g_norm2[l].reshape(1, d),
        'w_in_p': w_in_p, 'w_gk_big': w_gk_big,
        'b_gk': jnp.concatenate([b_gk_fwd[l], b_gk_bwd[l]]).reshape(1, 2 * GLA_KEY_W),
        'g_gla_out': g_gla_out[l].reshape(1, GLA_DV),
        'g_q_lora': g_q_lora[l].reshape(1, Q_LORA),
        'w_uq_p': _pad_heads(w_uq[l], (0, QK_DIM)).astype(BF16),
        'g_kv_lora': g_kv_lora[l].reshape(1, KV_LORA),
        'w_ukv_p': jnp.concatenate([_pad_heads(w_ukv_h, (0, NOPE_DIM)),
                                    _pad_heads(w_ukv_h, (NOPE_DIM, NOPE_DIM + V_DIM))],
                                   axis=1).astype(BF16),
        'gq': pad_gain(g_qk_q[l]), 'gk': pad_gain(g_qk_k[l]),
        'w_out_top': w_out[l][:GLA_VAL_W].astype(BF16),
        'w_out_bot': w_out[l][GLA_VAL_W:].astype(BF16),
        'w_router_t': w_router[l].T, 'b_router_col': b_router[l].reshape(N_EXPERTS, 1),
        'w_sh_gate': w_sh_gate[l].astype(BF16), 'w_sh_up': w_sh_up[l].astype(BF16),
        'w_sh_down': w_sh_down[l].astype(BF16),
    }


def _rope_tables(length):
    pos = np.arange(length)
    r = (pos // GRID_W).astype(np.float32)
    c = (pos % GRID_W).astype(np.float32)
    half = ROPE_DIM // 4
    inv_freq = np.float32(ROPE_THETA) ** (-np.arange(half, dtype=np.float32) / np.float32(half))
    ang_r = r[:, None] * inv_freq[None, :]
    ang_c = c[:, None] * inv_freq[None, :]
    zeros = lambda w: np.zeros((length, w), np.float32)
    ones = lambda w: np.ones((length, w), np.float32)
    tail_w = HEAD_W - ROPE_LANE0 - ROPE_DIM
    cos = np.concatenate([ones(ROPE_LANE0), np.cos(ang_r), np.cos(ang_r), np.cos(ang_c),
                          np.cos(ang_c), ones(tail_w)], axis=1)
    s1 = np.concatenate([zeros(ROPE_LANE0), -np.sin(ang_r), zeros(half), -np.sin(ang_c),
                         zeros(half), zeros(tail_w)], axis=1)
    s2 = np.concatenate([zeros(ROPE_LANE0), zeros(half), np.sin(ang_r), zeros(half),
                         np.sin(ang_c), zeros(tail_w)], axis=1)
    return tuple(jnp.asarray(t, F32) for t in (cos, s1, s2))


def _identity_tables(rows):
    return (jnp.ones((rows, LANES), F32), jnp.zeros((rows, LANES), F32),
            jnp.zeros((rows, LANES), F32))


def _mix_route_dispatch(x_tok, seq_shape, mod, per_batch_mod, p, s0f, s0b, rope_tabs, ctx_kv,
                        tm, tq):
    bt, lt, d = x_tok.shape
    b, l = seq_shape
    n_tok = bt * lt
    (q, k, v, g, lgf, lgb, qm, km, vm, ckvn, kr) = _inproj(x_tok, mod, per_batch_mod, p,
                                                            rope_tabs, tm)
    seq = lambda a: a.reshape(b, l, a.shape[-1])
    og, sf, sb = _gla(seq(q), seq(k), seq(v), seq(g), seq(lgf), seq(lgb), s0f, s0b,
                      p['g_gla_out'])
    kv_pairs = [(seq(km), seq(vm))] + ([ctx_kv] if ctx_kv is not None else [])
    om = _attn(seq(qm), kv_pairs, tq)
    tokv = lambda a: a.reshape(bt, lt, a.shape[-1])
    x1, h2a, h2b, slot8, w8, cnt = _outproj(tokv(og), tokv(om), x_tok, mod, per_batch_mod, p, tm)
    x_sorted = [_sc_scatter_rows(h.reshape(n_tok, PACK_W), slot8, N_EXPERTS * n_tok)
                for h in (h2a, h2b)]
    routed = dict(x1=x1, h2=(h2a, h2b), slot8=slot8, w8=w8, cnt=cnt, x_sorted=x_sorted)
    return routed, sf, sb, ckvn, kr


def _experts_collect(r, w_gate, w_up, w_down):
    n_tok = r['slot8'].shape[1]
    y_sorted = _ffn(r['x_sorted'], r['cnt'][:, 0].astype(I32), w_gate, w_up, w_down)
    slots_flat = r['slot8'].reshape(1, TOP_K * n_tok)
    return [_sc_gather_rows(y, slots_flat).reshape(TOP_K, n_tok, PACK_W) for y in y_sorted]


def kernel(x_prompt, x_sample, c, state_gla_fwd, state_gla_bwd, cache_mla_ckv, cache_mla_krope,
           c_ctx, w_ada, b_ada, g_norm1, g_norm2, w_in, w_gk_fwd, b_gk_fwd, w_gk_bwd, b_gk_bwd,
           g_gla_out, g_q_lora, w_uq, g_kv_lora, w_ukv, g_qk_q, g_qk_k, w_out,
           w_router, b_router, w_exp_gate, w_exp_up, w_exp_down, w_sh_gate, w_sh_up, w_sh_down):
    bp, lp, d = x_prompt.shape
    bs, ls, _ = x_sample.shape
    depth = w_ada.shape[0]
    xp = x_prompt.reshape(1, bp * lp, d)
    xs = x_sample
    new_f, new_b, new_ckv, new_kr = [], [], [], []
    lat_tabs = _rope_tables(ls)
    tm = 256
    ctx_tabs = _identity_tables(tm)
    cvecs = jnp.concatenate([c_ctx[None], c, jnp.zeros((8 - 1 - bs, d), F32)], axis=0)
    for l in range(depth):
        p = _prep_params(l, g_norm1, g_norm2, w_in, w_gk_fwd, b_gk_fwd, w_gk_bwd, b_gk_bwd,
                         g_gla_out, g_q_lora, w_uq, g_kv_lora, w_ukv, g_qk_q, g_qk_k, w_out,
                         w_router, b_router, w_sh_gate, w_sh_up, w_sh_down)
        mod = _ada(cvecs, w_ada[l], b_ada[l]).reshape(8, 6, d)
        mod_p, mod_s = mod[0:1], mod[1:1 + bs]
        zeros = jnp.zeros((bp, GLA_KEY_W, GLA_DV), F32)
        r_ctx, sf, sb, ckvn, kr = _mix_route_dispatch(
            xp, (bp, lp), mod_p, False, p, zeros, zeros, ctx_tabs, None, tm, lp)
        new_f.append(sf.reshape(bp, GLA_HEADS, GLA_DK, GLA_DV))
        new_b.append(sb.reshape(bp, GLA_HEADS, GLA_DK, GLA_DV))
        new_ckv.append(ckvn.reshape(bp, lp, KV_LORA))
        new_kr.append(kr.reshape(bp, lp, LANES)[:, :, KR_LANE0:KR_LANE0 + ROPE_DIM])
        kr_cache = jnp.pad(cache_mla_krope[:, l],
                           ((0, 0), (0, 0), (ROPE_LANE0, LANES - ROPE_LANE0 - ROPE_DIM)))
        ctx_kv = _cache_kv(cache_mla_ckv[:, l], kr_cache, p)
        s0f = state_gla_fwd[:, l].reshape(bs, GLA_KEY_W, GLA_DV)
        s0b = state_gla_bwd[:, l].reshape(bs, GLA_KEY_W, GLA_DV)
        r_lat, _, _, _, _ = _mix_route_dispatch(
            xs, (bs, ls), mod_s, True, p, s0f, s0b, lat_tabs, ctx_kv, tm, 256)
        experts = (w_exp_gate[l], w_exp_up[l], w_exp_down[l])
        yg_ctx = _experts_collect(r_ctx, *experts)
        yg_lat = _experts_collect(r_lat, *experts)
        xp = _combine(r_ctx['x1'], r_ctx['h2'], yg_ctx, r_ctx['w8'], mod_p, False, p, tm)
        xs = _combine(r_lat['x1'], r_lat['h2'], yg_lat, r_lat['w8'], mod_s, True, p, tm)
    return (xp.reshape(bp, lp, d), xs, jnp.stack(new_f, axis=1), jnp.stack(new_b, axis=1),
            jnp.stack(new_ckv, axis=1), jnp.stack(new_kr, axis=1))
```

```python
import functools

import jax
import jax.numpy as jnp
import numpy as np
from jax import lax
from jax.experimental import pallas as pl
from jax.experimental.pallas import tpu as pltpu
from jax.experimental.pallas import tpu_sc as plsc

F32 = jnp.float32
BF16 = jnp.bfloat16
I32 = jnp.int32
U32 = jnp.uint32

D_MODEL = 1024
EPS = 1e-6
GRID_W = 64
GLA_HEADS = 4
GLA_DK = 64
GLA_DV = 128
GLA_GATE_RANK = 16
GLA_GATE_NORM = 16.0
GLA_KEY_W = GLA_HEADS * GLA_DK
GLA_VAL_W = GLA_HEADS * GLA_DV
MLA_HEADS = 4
Q_LORA = 256
KV_LORA = 128
NOPE_DIM = 64
ROPE_DIM = 32
V_DIM = 128
QK_DIM = NOPE_DIM + ROPE_DIM
ROPE_THETA = 10000.0
N_EXPERTS = 64
TOP_K = 8
N_GROUPS = 8
TOPK_GROUPS = 4
D_EXPERT = 256
ROUTED_SCALE = 2.5

LANES = 128
HEAD_W = LANES
MLA_W = MLA_HEADS * HEAD_W
ROPE_LANE0 = NOPE_DIM
COL_V = 2 * GLA_KEY_W
COL_G = COL_V + GLA_VAL_W
COL_CQ = COL_G + GLA_VAL_W
COL_CKV = COL_CQ + Q_LORA
TAIL0 = COL_CKV + KV_LORA
IN_W = TAIL0 + LANES
KR_LANE0 = LANES - ROPE_DIM - 2 * GLA_GATE_RANK
GLA_BLOCK = 256
GLA_SUB = 64
IN_TILE = 256
OUT_TILE = 512
FFN_TILE = 512
PACK_PARTS = 2
PACK_W = D_MODEL // (2 * PACK_PARTS)
SC_WINDOW = 128
VMEM_LIMIT = 56 * 1024 * 1024

HIGHEST = lax.Precision.HIGHEST


def _dot(a, b, precision=None):
    return jnp.dot(a, b, preferred_element_type=F32, precision=precision)


def _dot_nt(a, b, precision=None):
    return lax.dot_general(a, b, (((1,), (1,)), ((), ())), preferred_element_type=F32,
                           precision=precision)


def _rms(x, width):
    ss = jnp.sum(x * x, axis=-1, keepdims=True) * (1.0 / width)
    return x * lax.rsqrt(ss + EPS)


def _silu(x):
    return x * jax.nn.sigmoid(x)


def _log_sigmoid(x):
    return jnp.minimum(x, 0.0) - jnp.log1p(jnp.exp(-jnp.abs(x)))


def _pack_rows(x):
    parts = []
    for i in range(PACK_PARTS):
        c0 = i * 2 * PACK_W
        lo = lax.bitcast_convert_type(x[:, c0:c0 + PACK_W].astype(BF16).astype(F32), U32)
        hi = lax.bitcast_convert_type(
            x[:, c0 + PACK_W:c0 + 2 * PACK_W].astype(BF16).astype(F32), U32)
        parts.append(hi | (lo >> 16))
    return parts


def _unpack_rows(parts):
    cols = []
    for w in parts:
        cols.append(lax.bitcast_convert_type(w << 16, F32))
        cols.append(lax.bitcast_convert_type(w & jnp.uint32(0xFFFF0000), F32))
    return jnp.concatenate(cols, axis=1)


def _ada_kernel(c_ref, w_ref, b_ref, o_ref):
    o_ref[...] = _dot(_silu(c_ref[...]), w_ref[...], precision=HIGHEST) + b_ref[...]


def _ada(cvecs, w_ada, b_ada):
    n = w_ada.shape[1]
    tn = 768
    return pl.pallas_call(
        _ada_kernel,
        grid=(n // tn,),
        in_specs=[pl.BlockSpec((8, D_MODEL), lambda j: (0, 0)),
                  pl.BlockSpec((D_MODEL, tn), lambda j: (0, j)),
                  pl.BlockSpec((1, tn), lambda j: (0, j))],
        out_specs=pl.BlockSpec((8, tn), lambda j: (0, j)),
        out_shape=jax.ShapeDtypeStruct((8, n), F32),
        name="ada",
    )(cvecs, w_ada, b_ada.reshape(1, n))


def _rope(x, c, s1, s2):
    return x * c + pltpu.roll(x, LANES - 8, 1) * s1 + pltpu.roll(x, 8, 1) * s2


def _mla_kv(ckv_n, kr, w_ukv_ref, gk, rope_tabs, k_ref, v_ref):
    kv = _dot(ckv_n.astype(BF16), w_ukv_ref[...])
    for h in range(MLA_HEADS):
        k_h = kv[:, h * HEAD_W:(h + 1) * HEAD_W] + kr
        k_h = _rms(k_h, QK_DIM) * gk
        if rope_tabs is not None:
            k_h = _rope(k_h, *rope_tabs)
        k_ref[0, :, h * HEAD_W:(h + 1) * HEAD_W] = k_h.astype(BF16)
    v_ref[0] = kv[:, MLA_W:].astype(BF16)


def _lane_mask(lo, hi, rows):
    lane = lax.broadcasted_iota(I32, (rows, LANES), 1)
    return (lane >= lo) & (lane < hi)


def _inproj_kernel(x_ref, mod_ref, g1_ref, win_ref, wgk_ref, bgk_ref, gql_ref, wuq_ref,
                   gkv_ref, wukv_ref, gq_ref, gk_ref, *refs, positions):
    tab_refs, outs = (refs[:3], refs[3:]) if positions else ((), refs)
    (q_ref, k_ref, v_ref, g_ref, lgf_ref, lgb_ref, qm_ref, km_ref, vm_ref, ckvn_ref,
     kr_ref) = outs
    x = x_ref[0]
    mod = mod_ref[0]
    h = _rms(x, D_MODEL) * g1_ref[...] * (1.0 + mod[1:2]) + mod[0:1]
    y = _dot(h.astype(BF16), win_ref[...])
    q_ref[0] = (y[:, 0:GLA_KEY_W] * GLA_DK ** -0.5).astype(BF16)
    k_ref[0] = y[:, GLA_KEY_W:COL_V].astype(BF16)
    v_ref[0] = y[:, COL_V:COL_G].astype(BF16)
    g_ref[0] = y[:, COL_G:COL_CQ].astype(BF16)
    tail = y[:, TAIL0:IN_W]
    pre = _dot(tail.astype(BF16), wgk_ref[...]) + bgk_ref[...]
    logg = _log_sigmoid(pre) * (1.0 / GLA_GATE_NORM)
    lgf_ref[0] = logg[:, 0:GLA_KEY_W]
    lgb_ref[0] = logg[:, GLA_KEY_W:]
    rows = x.shape[0]
    tabs = tuple(r[...] for r in tab_refs) if positions else None
    cq = _rms(y[:, COL_CQ:COL_CKV], Q_LORA) * gql_ref[...]
    qm = _dot(cq.astype(BF16), wuq_ref[...])
    gq = gq_ref[...]
    for hh in range(MLA_HEADS):
        q_h = _rms(qm[:, hh * HEAD_W:(hh + 1) * HEAD_W], QK_DIM) * gq
        if positions:
            q_h = _rope(q_h, *tabs)
        qm_ref[0, :, hh * HEAD_W:(hh + 1) * HEAD_W] = (q_h * QK_DIM ** -0.5).astype(BF16)
    ckv_n = _rms(y[:, COL_CKV:TAIL0], KV_LORA) * gkv_ref[...]
    ckvn_ref[0] = ckv_n
    kr_ref[0] = tail
    kr = jnp.where(_lane_mask(KR_LANE0, KR_LANE0 + ROPE_DIM, rows), tail, 0.0)
    _mla_kv(ckv_n, kr, wukv_ref, gk_ref[...], tabs, km_ref, vm_ref)


def _inproj(x, mod, per_batch_mod, p, rope_tabs, tm):
    b, l, d = x.shape
    nt = l // tm
    mod_map = (lambda bi, i: (bi, 0, 0)) if per_batch_mod else (lambda bi, i: (0, 0, 0))
    tab_map = lambda bi, i: (i, 0)
    const = lambda bi, i: (0, 0)
    tok = lambda w: pl.BlockSpec((1, tm, w), lambda bi, i: (bi, i, 0))
    full = lambda a: pl.BlockSpec(a.shape, const)
    weights = [p['g_norm1'], p['w_in_p'], p['w_gk_big'], p['b_gk'], p['g_q_lora'], p['w_uq_p'],
               p['g_kv_lora'], p['w_ukv_p'], p['gq'], p['gk']]
    outs = [(GLA_KEY_W, BF16), (GLA_KEY_W, BF16), (GLA_VAL_W, BF16), (GLA_VAL_W, BF16),
            (GLA_KEY_W, F32), (GLA_KEY_W, F32), (MLA_W, BF16), (MLA_W, BF16), (MLA_W, BF16),
            (KV_LORA, F32), (LANES, F32)]
    return pl.pallas_call(
        functools.partial(_inproj_kernel, positions=bool(rope_tabs)),
        grid=(b, nt),
        in_specs=[tok(d), pl.BlockSpec((1, 6, d), mod_map)] + [full(w) for w in weights]
                 + [pl.BlockSpec((tm, LANES), tab_map)] * len(rope_tabs),
        out_specs=[tok(w) for w, _ in outs],
        out_shape=[jax.ShapeDtypeStruct((b, l, w), dt) for w, dt in outs],
        compiler_params=pltpu.CompilerParams(
            dimension_semantics=("parallel", "parallel"), vmem_limit_bytes=VMEM_LIMIT),
        name="inproj",
    )(x, mod, *weights, *rope_tabs)


def _cache_kv_kernel(ckv_ref, kr_ref, wukv_ref, gk_ref, k_ref, v_ref):
    _mla_kv(ckv_ref[0], kr_ref[0], wukv_ref, gk_ref[...], None, k_ref, v_ref)


def _cache_kv(ckv, kr128, p):
    b, l, _ = ckv.shape
    tok = lambda w: pl.BlockSpec((1, l, w), lambda bi: (bi, 0, 0))
    full = lambda a: pl.BlockSpec(a.shape, lambda bi: (0, 0))
    return pl.pallas_call(
        _cache_kv_kernel,
        grid=(b,),
        in_specs=[tok(KV_LORA), tok(LANES), full(p['w_ukv_p']), full(p['gk'])],
        out_specs=[tok(MLA_W), tok(MLA_W)],
        out_shape=[jax.ShapeDtypeStruct((b, l, MLA_W), BF16)] * 2,
        name="cache_kv",
    )(ckv, kr128, p['w_ukv_p'], p['gk'])


def _split3(x):
    hi = x.astype(BF16)
    r1 = x - hi.astype(F32)
    mid = r1.astype(BF16)
    lo = (r1 - mid.astype(F32)).astype(BF16)
    return hi, mid, lo


def _gla_block(q, k, v, lg, s, fwd):
    n = GLA_BLOCK
    nc = n // GLA_SUB
    row = lax.broadcasted_iota(I32, (n, n), 0)
    col = lax.broadcasted_iota(I32, (n, n), 1)
    same = (row // GLA_SUB) == (col // GLA_SUB)
    causal = same & ((col <= row) if fwd else (col >= row))
    tri = causal.astype(BF16)
    hi, mid, lo = _split3(lg)
    cum = _dot(tri, hi) + _dot(tri, mid) + _dot(tri, lo)
    tot_rows, mid_rows = [], []
    for c in range(nc):
        r_tot = c * GLA_SUB + (GLA_SUB - 1 if fwd else 0)
        r_mid = c * GLA_SUB + GLA_SUB // 2
        tot_rows.append(jnp.broadcast_to(cum[r_tot:r_tot + 1], (GLA_SUB, GLA_KEY_W)))
        mid_rows.append(jnp.broadcast_to(cum[r_mid:r_mid + 1], (GLA_SUB, GLA_KEY_W)))
    tot_rows = jnp.concatenate(tot_rows, axis=0)
    mid_rows = jnp.concatenate(mid_rows, axis=0)
    rel = cum - mid_rows
    qi = q * jnp.exp(rel)
    ki = (k * jnp.exp(-rel)).astype(BF16)
    q_in = (q * jnp.exp(cum)).astype(BF16)
    k_up = k * jnp.exp(tot_rows - cum)
    dec = jnp.exp(tot_rows)
    k_up_t = k_up.T.astype(BF16)
    dec_t = dec.T
    lane_head = lax.broadcasted_iota(I32, (n, GLA_KEY_W), 1) // GLA_DK
    srow_head = lax.broadcasted_iota(I32, (GLA_KEY_W, GLA_DV), 0) // GLA_DK
    vrow_chunk = lax.broadcasted_iota(I32, (n, GLA_VAL_W), 0) // GLA_SUB
    o_heads = []
    for h in range(GLA_HEADS):
        a = _dot_nt(jnp.where(lane_head == h, qi, 0.0).astype(BF16), ki)
        a = jnp.where(causal, a, 0.0).astype(BF16)
        o_heads.append(_dot(a, v[:, h * GLA_DV:(h + 1) * GLA_DV]))
    o = jnp.concatenate(o_heads, axis=1)
    o_inter = [None] * nc
    for c in (range(nc) if fwd else range(nc - 1, -1, -1)):
        s_bd = jnp.concatenate(
            [jnp.where(srow_head == h, s, 0.0).astype(BF16) for h in range(GLA_HEADS)], axis=1)
        o_inter[c] = _dot(q_in[c * GLA_SUB:(c + 1) * GLA_SUB], s_bd)
        v_c = jnp.where(vrow_chunk == c, v, jnp.zeros_like(v))
        u = jnp.concatenate(
            [_dot(k_up_t[h * GLA_DK:(h + 1) * GLA_DK], v_c[:, h * GLA_DV:(h + 1) * GLA_DV])
             for h in range(GLA_HEADS)], axis=0)
        s = dec_t[:, c * GLA_SUB:c * GLA_SUB + 1] * s + u
    return o + jnp.concatenate(o_inter, axis=0), s


def _gla_kernel(q_ref, k_ref, v_ref, g_ref, lgf_ref, lgb_ref, s0f_ref, s0b_ref, gout_ref,
                o_ref, sf_ref, sb_ref, acc_ref, st_ref, *, nblk):
    st_ref[0] = s0f_ref[0]
    st_ref[1] = s0b_ref[0]

    def load(blk):
        start = blk * GLA_BLOCK
        r = pl.ds(start if isinstance(blk, int) else pl.multiple_of(start, GLA_BLOCK), GLA_BLOCK)
        return r, q_ref[0, r, :].astype(F32), k_ref[0, r, :].astype(F32), v_ref[0, r, :]

    def pair_step(i, first_touch):
        rf, qf, kf, vf = load(i)
        rb, qb, kb, vb = load(nblk - 1 - i)
        of, s_f = _gla_block(qf, kf, vf, lgf_ref[0, rf, :], st_ref[0], True)
        ob, s_b = _gla_block(qb, kb, vb, lgb_ref[0, rb, :], st_ref[1], False)
        if first_touch:
            acc_ref[rf, :] = of
            acc_ref[rb, :] = ob
        else:
            acc_ref[rf, :] += of
            acc_ref[rb, :] += ob
        st_ref[0] = s_f
        st_ref[1] = s_b

    if nblk == 1:
        r, q, k, v = load(0)
        of, s_f = _gla_block(q, k, v, lgf_ref[0, r, :], st_ref[0], True)
        ob, s_b = _gla_block(q, k, v, lgb_ref[0, r, :], st_ref[1], False)
        acc_ref[r, :] = of + ob
        st_ref[0] = s_f
        st_ref[1] = s_b
    else:
        half = nblk // 2
        lax.fori_loop(0, half, lambda i, c: (pair_step(i, True), c)[1], 0)
        lax.fori_loop(half, nblk, lambda i, c: (pair_step(i, False), c)[1], 0)
    sf_ref[0] = st_ref[0]
    sb_ref[0] = st_ref[1]

    def fin_step(blk, carry):
        r = pl.ds(pl.multiple_of(blk * GLA_BLOCK, GLA_BLOCK), GLA_BLOCK)
        o = acc_ref[r, :]
        gate = _silu(g_ref[0, r, :].astype(F32))
        for h in range(GLA_HEADS):
            sl = slice(h * GLA_DV, (h + 1) * GLA_DV)
            o_ref[0, r, sl] = (_rms(o[:, sl], GLA_DV) * gout_ref[...] * gate[:, sl]).astype(BF16)
        return carry

    lax.fori_loop(0, nblk, fin_step, 0)


def _gla(q, k, v, g, lgf, lgb, s0f, s0b, g_out):
    b, l, _ = q.shape
    assert l == GLA_BLOCK or l % (2 * GLA_BLOCK) == 0
    seq = lambda w: pl.BlockSpec((1, l, w), lambda bi: (bi, 0, 0))
    st = pl.BlockSpec((1, GLA_KEY_W, GLA_DV), lambda bi: (bi, 0, 0))
    return pl.pallas_call(
        functools.partial(_gla_kernel, nblk=l // GLA_BLOCK),
        grid=(b,),
        in_specs=[seq(GLA_KEY_W), seq(GLA_KEY_W), seq(GLA_VAL_W), seq(GLA_VAL_W),
                  seq(GLA_KEY_W), seq(GLA_KEY_W), st, st,
                  pl.BlockSpec((1, GLA_DV), lambda bi: (0, 0))],
        out_specs=[seq(GLA_VAL_W), st, st],
        out_shape=[jax.ShapeDtypeStruct((b, l, GLA_VAL_W), BF16),
                   jax.ShapeDtypeStruct((b, GLA_KEY_W, GLA_DV), F32),
                   jax.ShapeDtypeStruct((b, GLA_KEY_W, GLA_DV), F32)],
        scratch_shapes=[pltpu.VMEM((l, GLA_VAL_W), F32),
                        pltpu.VMEM((2, GLA_KEY_W, GLA_DV), F32)],
        compiler_params=pltpu.CompilerParams(
            dimension_semantics=("parallel",), vmem_limit_bytes=VMEM_LIMIT),
        name="gla",
    )(q, k, v, g, lgf, lgb, s0f, s0b, g_out)


def _attn_kernel(*refs, n_kv):
    q_ref, o_ref = refs[0], refs[-1]
    kv = [(refs[1 + 2 * i], refs[2 + 2 * i]) for i in range(n_kv)]
    for h in range(MLA_HEADS):
        sl = slice(h * HEAD_W, (h + 1) * HEAD_W)
        q = q_ref[0, :, sl]
        scores = [_dot_nt(q, k_ref[0, :, sl]) for k_ref, _ in kv]
        m = functools.reduce(jnp.maximum, [jnp.max(s, axis=-1, keepdims=True) for s in scores])
        o, den = 0.0, 0.0
        for s, (_, v_ref) in zip(scores, kv):
            p = jnp.exp(s - m)
            den = den + jnp.sum(p, axis=-1, keepdims=True)
            o = o + _dot(p.astype(BF16), v_ref[0, :, sl])
        o_ref[0, :, sl] = (o / den).astype(BF16)


def _attn(q, kv_pairs, tq):
    b, l, _ = q.shape
    kv_specs, kv_args = [], []
    for k, v in kv_pairs:
        spec = pl.BlockSpec((1, k.shape[1], MLA_W), lambda bi, i: (bi, 0, 0))
        kv_specs += [spec, spec]
        kv_args += [k, v]
    return pl.pallas_call(
        functools.partial(_attn_kernel, n_kv=len(kv_pairs)),
        grid=(b, l // tq),
        in_specs=[pl.BlockSpec((1, tq, MLA_W), lambda bi, i: (bi, i, 0))] + kv_specs,
        out_specs=pl.BlockSpec((1, tq, MLA_W), lambda bi, i: (bi, i, 0)),
        out_shape=jax.ShapeDtypeStruct((b, l, MLA_W), BF16),
        compiler_params=pltpu.CompilerParams(
            dimension_semantics=("parallel", "parallel"), vmem_limit_bytes=VMEM_LIMIT),
        name="attn",
    )(q, *kv_args)


def _sum01(x):
    return jnp.sum(jnp.sum(x, axis=1, keepdims=True), axis=0, keepdims=True)


def _route(logits_t, bias_col, cnt_col, cap):
    t = logits_t.shape[1]
    gsz = N_EXPERTS // N_GROUPS
    scores = jax.nn.sigmoid(logits_t)
    sel = (scores + bias_col).reshape(N_GROUPS, gsz, t)
    scores = scores.reshape(N_GROUPS, gsz, t)
    neg = jnp.float32(-jnp.inf)
    ie = lax.broadcasted_iota(I32, (N_GROUPS, gsz, t), 1)
    ig = lax.broadcasted_iota(I32, (N_GROUPS, gsz, t), 0)
    m1 = jnp.max(sel, axis=1, keepdims=True)
    first = jnp.min(jnp.where(sel == m1, ie, gsz), axis=1, keepdims=True)
    m2 = jnp.max(jnp.where(ie == first, neg, sel), axis=1, keepdims=True)
    grp = m1 + m2
    igk = lax.broadcasted_iota(I32, (N_GROUPS, 1, t), 0)
    g_sel = jnp.zeros((N_GROUPS, 1, t), jnp.bool_)
    cur = grp
    for _ in range(TOPK_GROUPS):
        m = jnp.max(cur, axis=0, keepdims=True)
        pick = igk == jnp.min(jnp.where(cur == m, igk, N_GROUPS), axis=0, keepdims=True)
        g_sel = g_sel | pick
        cur = jnp.where(pick, neg, cur)
    cur = jnp.where(g_sel, sel, neg)
    idx = ig * gsz + ie
    e_sel = jnp.zeros((N_GROUPS, gsz, t), jnp.bool_)
    picks = []
    for _ in range(TOP_K):
        m = jnp.max(jnp.max(cur, axis=1, keepdims=True), axis=0, keepdims=True)
        cand = jnp.where(cur == m, idx, N_EXPERTS)
        pick = idx == jnp.min(jnp.min(cand, axis=1, keepdims=True), axis=0, keepdims=True)
        picks.append(pick)
        e_sel = e_sel | pick
        cur = jnp.where(pick, neg, cur)
    w = jnp.where(e_sel, scores, 0.0)
    gate = w / _sum01(w) * ROUTED_SCALE
    sel_f = e_sel.astype(F32).reshape(N_EXPERTS, t)
    earlier = (lax.broadcasted_iota(I32, (t, t), 0) < lax.broadcasted_iota(I32, (t, t), 1))
    rank = _dot(sel_f.astype(BF16), earlier.astype(BF16))
    base = lax.broadcasted_iota(I32, (N_EXPERTS, 1), 0).astype(F32) * float(cap) + cnt_col
    slot = (base + rank).reshape(N_GROUPS, gsz, t)
    slot8 = jnp.concatenate([_sum01(jnp.where(pk, slot, 0.0)).reshape(1, t) for pk in picks],
                            axis=0).astype(I32)
    w8 = jnp.concatenate([_sum01(jnp.where(pk, gate, 0.0)).reshape(1, t) for pk in picks], axis=0)
    return slot8, w8, cnt_col + jnp.sum(sel_f, axis=1, keepdims=True)


def _outproj_kernel(og_ref, om_ref, x_ref, mod_ref, wtop_ref, wbot_ref, g2_ref, wr_ref, br_ref,
                    x1_ref, h2pa_ref, h2pb_ref, slot_ref, w8_ref, cnt_ref, cnt_scr, *, cap):
    @pl.when((pl.program_id(0) == 0) & (pl.program_id(1) == 0))
    def _():
        cnt_scr[...] = jnp.zeros_like(cnt_scr)

    mod = mod_ref[0]
    mix = _dot(og_ref[0], wtop_ref[...]) + _dot(om_ref[0], wbot_ref[...])
    x1 = x_ref[0] + mod[2:3] * mix
    x1_ref[0] = x1
    h2 = _rms(x1, D_MODEL) * g2_ref[...] * (1.0 + mod[4:5]) + mod[3:4]
    h2pa_ref[0], h2pb_ref[0] = _pack_rows(h2)
    logits_t = _dot_nt(wr_ref[...], h2, precision=HIGHEST)
    slot8, w8, cnt = _route(logits_t, br_ref[...], cnt_scr[:, 0:1], cap)
    slot_ref[...] = slot8
    t = w8.shape[1]
    w8_ref[0] = jnp.concatenate([w8, jnp.zeros((LANES - TOP_K, t), F32)], axis=0).T
    cnt_scr[...] = jnp.broadcast_to(cnt, cnt_scr.shape)
    cnt_ref[...] = cnt_scr[...]


def _outproj(og, om, x, mod, per_batch_mod, p, tm):
    b, l, d = x.shape
    nt = l // tm
    mod_map = (lambda bi, i: (bi, 0, 0)) if per_batch_mod else (lambda bi, i: (0, 0, 0))
    tok = lambda w: pl.BlockSpec((1, tm, w), lambda bi, i: (bi, i, 0))
    full = lambda a: pl.BlockSpec(a.shape, lambda bi, i: (0, 0))
    weights = [p['w_out_top'], p['w_out_bot'], p['g_norm2'], p['w_router_t'], p['b_router_col']]
    cnt_shape = (N_EXPERTS, LANES)
    return pl.pallas_call(
        functools.partial(_outproj_kernel, cap=b * l),
        grid=(b, nt),
        in_specs=[tok(GLA_VAL_W), tok(MLA_W), tok(d), pl.BlockSpec((1, 6, d), mod_map)]
                 + [full(w) for w in weights],
        out_specs=[tok(d), tok(PACK_W), tok(PACK_W),
                   pl.BlockSpec((TOP_K, tm), lambda bi, i: (0, bi * nt + i)),
                   tok(LANES), pl.BlockSpec(cnt_shape, lambda bi, i: (0, 0))],
        out_shape=[jax.ShapeDtypeStruct((b, l, d), F32), jax.ShapeDtypeStruct((b, l, PACK_W), U32),
                   jax.ShapeDtypeStruct((b, l, PACK_W), U32),
                   jax.ShapeDtypeStruct((TOP_K, b * l), I32),
                   jax.ShapeDtypeStruct((b, l, LANES), F32),
                   jax.ShapeDtypeStruct(cnt_shape, F32)],
        scratch_shapes=[pltpu.VMEM(cnt_shape, F32)],
        compiler_params=pltpu.CompilerParams(
            dimension_semantics=("arbitrary", "arbitrary"), vmem_limit_bytes=VMEM_LIMIT),
        name="outproj",
    )(og, om, x, mod, *weights)


def _sc_mesh():
    return plsc.VectorSubcoreMesh(core_axis_name="c", subcore_axis_name="s")


def _sc_scatter_rows(src, slot8, n_rows_out):
    t, d = src.shape
    nk = slot8.shape[0]

    @functools.partial(pl.kernel, out_type=jax.ShapeDtypeStruct((n_rows_out, d), src.dtype),
                       mesh=_sc_mesh(), scratch_types=[], name="dispatch")
    def run(src_hbm, slot_hbm, out_hbm):
        def body(x_vmem, i_vmem):
            for k in range(nk):
                pltpu.sync_copy(x_vmem, out_hbm.at[i_vmem.at[k]])

        pltpu.emit_pipeline(
            body, grid=(t // SC_WINDOW,),
            in_specs=[pl.BlockSpec((SC_WINDOW, d), lambda i: (i, 0)),
                      pl.BlockSpec((nk, SC_WINDOW), lambda i: (0, i))],
            out_specs=[], core_axis_name=("c", "s"),
            dimension_semantics=(pltpu.PARALLEL,))(src_hbm, slot_hbm)

    return run(src, slot8)


def _sc_gather_rows(table, slots):
    n = slots.shape[1]
    d = table.shape[1]

    @functools.partial(pl.kernel, out_type=jax.ShapeDtypeStruct((n, d), table.dtype),
                       mesh=_sc_mesh(), scratch_types=[], name="collect")
    def run(tab_hbm, slot_hbm, out_hbm):
        def body(i_vmem, o_vmem):
            pltpu.sync_copy(tab_hbm.at[i_vmem.at[0]], o_vmem)

        pltpu.emit_pipeline(
            body, grid=(n // SC_WINDOW,),
            in_specs=[pl.BlockSpec((1, SC_WINDOW), lambda i: (0, i))],
            out_specs=[pl.BlockSpec((SC_WINDOW, d), lambda i: (i, 0))],
            core_axis_name=("c", "s"),
            dimension_semantics=(pltpu.PARALLEL,))(slot_hbm, out_hbm)

    return run(table, slots)


def _ffn_kernel(nt_ref, first_ref, nxt_ref, xsa_hbm, xsb_hbm, wg_ref, wu_ref, wd_ref,
                ysa_hbm, ysb_hbm, xbuf, ybuf, wg_b, wu_b, wd_b, sem_in, sem_out, *, cap):
    e = pl.program_id(0)
    n_exp = pl.num_programs(0)
    n = nt_ref[e]
    g0 = first_ref[e]
    total = first_ref[n_exp - 1] + nt_ref[n_exp - 1]
    xs_hbm = (xsa_hbm, xsb_hbm)
    ys_hbm = (ysa_hbm, ysb_hbm)

    def rows(ex, t):
        return pl.ds(pl.multiple_of(ex * cap + t * FFN_TILE, FFN_TILE), FFN_TILE)

    def in_copy(ex, t, slot, part):
        return pltpu.make_async_copy(xs_hbm[part].at[rows(ex, t)], xbuf.at[slot, part],
                                     sem_in.at[slot, part])

    def out_copy(ex, t, slot, part):
        return pltpu.make_async_copy(ybuf.at[slot, part], ys_hbm[part].at[rows(ex, t)],
                                     sem_out.at[slot, part])

    def wait_out(slot):
        for part in range(PACK_PARTS):
            out_copy(e, 0, slot, part).wait()

    @pl.when(n > 0)
    def _():
        @pl.when(g0 == 0)
        def _():
            for part in range(PACK_PARTS):
                in_copy(e, 0, 0, part).start()

        wg_b[...] = wg_ref[0].astype(BF16)
        wu_b[...] = wu_ref[0].astype(BF16)
        wd_b[...] = wd_ref[0].astype(BF16)
        nxt = nxt_ref[e]

        def tile(t, carry):
            g = g0 + t
            slot = g % 2
            for part in range(PACK_PARTS):
                in_copy(e, t, slot, part).wait()

            @pl.when(t + 1 < n)
            def _():
                for part in range(PACK_PARTS):
                    in_copy(e, t + 1, 1 - slot, part).start()

            @pl.when((t + 1 == n) & (nxt < n_exp))
            def _():
                for part in range(PACK_PARTS):
                    in_copy(nxt, 0, 1 - slot, part).start()

            @pl.when(g >= 2)
            def _():
                wait_out(slot)

            x = _unpack_rows([xbuf[slot, part] for part in range(PACK_PARTS)]).astype(BF16)
            a = _silu(_dot(x, wg_b[...])) * _dot(x, wu_b[...])
            y = _pack_rows(_dot(a.astype(BF16), wd_b[...]))
            for part in range(PACK_PARTS):
                ybuf[slot, part] = y[part]
                out_copy(e, t, slot, part).start()
            return carry

        lax.fori_loop(0, n, tile, 0)

        @pl.when(g0 + n == total)
        def _():
            @pl.when(total >= 2)
            def _():
                wait_out(total % 2)

            wait_out((total - 1) % 2)


def _ffn(xs_parts, counts, w_gate, w_up, w_down):
    n_exp, d = w_gate.shape[0], w_gate.shape[1]
    cap = xs_parts[0].shape[0] // n_exp
    ntiles = (counts + FFN_TILE - 1) // FFN_TILE
    first = jnp.cumsum(ntiles) - ntiles
    ids = jnp.arange(n_exp, dtype=I32)
    later_busy = (ids[None, :] > ids[:, None]) & (ntiles[None, :] > 0)
    nxt = jnp.min(jnp.where(later_busy, ids[None, :], n_exp), axis=1).astype(I32)
    wmap = lambda e, nt, first, nxt: (e, 0, 0)
    hbm = pl.BlockSpec(memory_space=pl.ANY)
    return pl.pallas_call(
        functools.partial(_ffn_kernel, cap=cap),
        grid_spec=pltpu.PrefetchScalarGridSpec(
            num_scalar_prefetch=3, grid=(n_exp,),
            in_specs=[hbm] * PACK_PARTS
                     + [pl.BlockSpec((1, d, D_EXPERT), wmap), pl.BlockSpec((1, d, D_EXPERT), wmap),
                        pl.BlockSpec((1, D_EXPERT, d), wmap)],
            out_specs=[hbm] * PACK_PARTS,
            scratch_shapes=[pltpu.VMEM((2, PACK_PARTS, FFN_TILE, PACK_W), U32),
                            pltpu.VMEM((2, PACK_PARTS, FFN_TILE, PACK_W), U32),
                            pltpu.VMEM((d, D_EXPERT), BF16), pltpu.VMEM((d, D_EXPERT), BF16),
                            pltpu.VMEM((D_EXPERT, d), BF16),
                            pltpu.SemaphoreType.DMA((2, PACK_PARTS)),
                            pltpu.SemaphoreType.DMA((2, PACK_PARTS))]),
        out_shape=[jax.ShapeDtypeStruct(xs_parts[0].shape, U32)] * PACK_PARTS,
        compiler_params=pltpu.CompilerParams(
            dimension_semantics=("arbitrary",), vmem_limit_bytes=VMEM_LIMIT),
        name="ffn",
    )(ntiles.astype(I32), first.astype(I32), nxt, *xs_parts, w_gate, w_up, w_down)


def _combine_kernel(x1_ref, h2pa_ref, h2pb_ref, yga_ref, ygb_ref, w8_ref, mod_ref,
                    wsg_ref, wsu_ref, wsd_ref, o_ref):
    h2 = _unpack_rows([h2pa_ref[0], h2pb_ref[0]]).astype(BF16)
    a = _silu(_dot(h2, wsg_ref[...])) * _dot(h2, wsu_ref[...])
    acc = _dot(a.astype(BF16), wsd_ref[...])
    w8 = w8_ref[0]
    for k in range(TOP_K):
        acc += w8[:, k:k + 1] * _unpack_rows([yga_ref[k], ygb_ref[k]])
    o_ref[0] = x1_ref[0] + mod_ref[0][5:6] * acc


def _combine(x1, h2p_parts, yg_parts, w8, mod, per_batch_mod, p, tm):
    b, l, d = x1.shape
    nt = l // tm
    mod_map = (lambda bi, i: (bi, 0, 0)) if per_batch_mod else (lambda bi, i: (0, 0, 0))
    tok = lambda w: pl.BlockSpec((1, tm, w), lambda bi, i: (bi, i, 0))
    full = lambda a: pl.BlockSpec(a.shape, lambda bi, i: (0, 0))
    weights = [p['w_sh_gate'], p['w_sh_up'], p['w_sh_down']]
    gathered = pl.BlockSpec((TOP_K, tm, PACK_W), lambda bi, i: (0, bi * nt + i, 0))
    return pl.pallas_call(
        _combine_kernel,
        grid=(b, nt),
        in_specs=[tok(d)] + [tok(PACK_W)] * PACK_PARTS + [gathered] * PACK_PARTS
                 + [tok(LANES), pl.BlockSpec((1, 6, d), mod_map)] + [full(w) for w in weights],
        out_specs=tok(d),
        out_shape=jax.ShapeDtypeStruct((b, l, d), F32),
        compiler_params=pltpu.CompilerParams(
            dimension_semantics=("parallel", "parallel"), vmem_limit_bytes=VMEM_LIMIT),
        name="combine",
    )(x1, *h2p_parts, *yg_parts, w8, mod, *weights)


def _pad_heads(w, parts):
    k = w.shape[0]
    per = w.shape[1] // MLA_HEADS
    w = w.reshape(k, MLA_HEADS, per)[:, :, parts[0]:parts[1]]
    w = jnp.pad(w, ((0, 0), (0, 0), (0, HEAD_W - (parts[1] - parts[0]))))
    return w.reshape(k, MLA_HEADS * HEAD_W)


def _prep_params(l, g_norm1, g_norm2, w_in, w_gk_fwd, b_gk_fwd, w_gk_bwd, b_gk_bwd, g_gla_out,
                 g_q_lora, w_uq, g_kv_lora, w_ukv, g_qk_q, g_qk_k, w_out, w_router, b_router,
                 w_sh_gate, w_sh_up, w_sh_down):
    w = w_in[l]
    d = w.shape[0]
    o_lrf = COL_CQ
    o_lrb = o_lrf + GLA_GATE_RANK
    o_cq = o_lrb + GLA_GATE_RANK
    o_ckv = o_cq + Q_LORA
    o_kr = o_ckv + KV_LORA
    w_in_p = jnp.concatenate([
        w[:, 0:COL_CQ], w[:, o_cq:o_ckv], w[:, o_ckv:o_kr], jnp.zeros((d, KR_LANE0), w.dtype),
        w[:, o_kr:o_kr + ROPE_DIM], w[:, o_lrf:o_lrb], w[:, o_lrb:o_cq]], axis=1).astype(BF16)
    z = jnp.zeros((GLA_GATE_RANK, GLA_KEY_W), F32)
    w_gk_big = jnp.concatenate([
        jnp.zeros((LANES - 2 * GLA_GATE_RANK, 2 * GLA_KEY_W), F32),
        jnp.concatenate([w_gk_fwd[l], z], axis=1),
        jnp.concatenate([z, w_gk_bwd[l]], axis=1)], axis=0).astype(BF16)
    pad_gain = lambda g: jnp.pad(g, (0, HEAD_W - QK_DIM)).reshape(1, HEAD_W)
    w_ukv_h = w_ukv[l]
    return {
        'g_norm1': g_norm1[l].reshape(1, d), 'g_norm2': g_norm2[l].reshape(1, d),
        'w_in_p': w_in_p, 'w_gk_big': w_gk_big,
        'b_gk': jnp.concatenate([b_gk_fwd[l], b_gk_bwd[l]]).reshape(1, 2 * GLA_KEY_W),
        'g_gla_out': g_gla_out[l].reshape(1, GLA_DV),
        'g_q_lora': g_q_lora[l].reshape(1, Q_LORA),
        'w_uq_p': _pad_heads(w_uq[l], (0, QK_DIM)).astype(BF16),
        'g_kv_lora': g_kv_lora[l].reshape(1, KV_LORA),
        'w_ukv_p': jnp.concatenate([_pad_heads(w_ukv_h, (0, NOPE_DIM)),
                                    _pad_heads(w_ukv_h, (NOPE_DIM, NOPE_DIM + V_DIM))],
                                   axis=1).astype(BF16),
        'gq': pad_gain(g_qk_q[l]), 'gk': pad_gain(g_qk_k[l]),
        'w_out_top': w_out[l][:GLA_VAL_W].astype(BF16),
        'w_out_bot': w_out[l][GLA_VAL_W:].astype(BF16),
        'w_router_t': w_router[l].T, 'b_router_col': b_router[l].reshape(N_EXPERTS, 1),
        'w_sh_gate': w_sh_gate[l].astype(BF16), 'w_sh_up': w_sh_up[l].astype(BF16),
        'w_sh_down': w_sh_down[l].astype(BF16),
    }


def _rope_tables(length):
    pos = np.arange(length)
    r = (pos // GRID_W).astype(np.float32)
    c = (pos % GRID_W).astype(np.float32)
    half = ROPE_DIM // 4
    inv_freq = np.float32(ROPE_THETA) ** (-np.arange(half, dtype=np.float32) / np.float32(half))
    ang_r = r[:, None] * inv_freq[None, :]
    ang_c = c[:, None] * inv_freq[None, :]
    zeros = lambda w: np.zeros((length, w), np.float32)
    ones = lambda w: np.ones((length, w), np.float32)
    tail_w = HEAD_W - ROPE_LANE0 - ROPE_DIM
    cos = np.concatenate([ones(ROPE_LANE0), np.cos(ang_r), np.cos(ang_r), np.cos(ang_c),
                          np.cos(ang_c), ones(tail_w)], axis=1)
    s1 = np.concatenate([zeros(ROPE_LANE0), -np.sin(ang_r), zeros(half), -np.sin(ang_c),
                         zeros(half), zeros(tail_w)], axis=1)
    s2 = np.concatenate([zeros(ROPE_LANE0), zeros(half), np.sin(ang_r), zeros(half),
                         np.sin(ang_c), zeros(tail_w)], axis=1)
    return tuple(jnp.asarray(t, F32) for t in (cos, s1, s2))


def _mix_route_dispatch(x_tok, seq_shape, mod, per_batch_mod, p, s0f, s0b, rope_tabs, ctx_kv,
                        tq):
    bt, lt, d = x_tok.shape
    b, l = seq_shape
    n_tok = bt * lt
    (q, k, v, g, lgf, lgb, qm, km, vm, ckvn, kr) = _inproj(x_tok, mod, per_batch_mod, p,
                                                            rope_tabs, IN_TILE)
    seq = lambda a: a.reshape(b, l, a.shape[-1])
    og, sf, sb = _gla(seq(q), seq(k), seq(v), seq(g), seq(lgf), seq(lgb), s0f, s0b,
                      p['g_gla_out'])
    kv_pairs = [(seq(km), seq(vm))] + ([ctx_kv] if ctx_kv is not None else [])
    om = _attn(seq(qm), kv_pairs, tq)
    tokv = lambda a: a.reshape(bt, lt, a.shape[-1])
    x1, h2a, h2b, slot8, w8, cnt = _outproj(tokv(og), tokv(om), x_tok, mod, per_batch_mod, p,
                                            OUT_TILE)
    x_sorted = [_sc_scatter_rows(h.reshape(n_tok, PACK_W), slot8, N_EXPERTS * n_tok)
                for h in (h2a, h2b)]
    routed = dict(x1=x1, h2=(h2a, h2b), slot8=slot8, w8=w8, cnt=cnt, x_sorted=x_sorted)
    return routed, sf, sb, ckvn, kr


def _experts_collect(r, w_gate, w_up, w_down):
    n_tok = r['slot8'].shape[1]
    y_sorted = _ffn(r['x_sorted'], r['cnt'][:, 0].astype(I32), w_gate, w_up, w_down)
    slots_flat = r['slot8'].reshape(1, TOP_K * n_tok)
    return [_sc_gather_rows(y, slots_flat).reshape(TOP_K, n_tok, PACK_W) for y in y_sorted]


def kernel(x_prompt, x_sample, c, state_gla_fwd, state_gla_bwd, cache_mla_ckv, cache_mla_krope,
           c_ctx, w_ada, b_ada, g_norm1, g_norm2, w_in, w_gk_fwd, b_gk_fwd, w_gk_bwd, b_gk_bwd,
           g_gla_out, g_q_lora, w_uq, g_kv_lora, w_ukv, g_qk_q, g_qk_k, w_out,
           w_router, b_router, w_exp_gate, w_exp_up, w_exp_down, w_sh_gate, w_sh_up, w_sh_down):
    bp, lp, d = x_prompt.shape
    bs, ls, _ = x_sample.shape
    depth = w_ada.shape[0]
    xp = x_prompt.reshape(1, bp * lp, d)
    xs = x_sample
    new_f, new_b, new_ckv, new_kr = [], [], [], []
    lat_tabs = _rope_tables(ls)
    cvecs = jnp.concatenate([c_ctx[None], c, jnp.zeros((8 - 1 - bs, d), F32)], axis=0)
    for l in range(depth):
        p = _prep_params(l, g_norm1, g_norm2, w_in, w_gk_fwd, b_gk_fwd, w_gk_bwd, b_gk_bwd,
                         g_gla_out, g_q_lora, w_uq, g_kv_lora, w_ukv, g_qk_q, g_qk_k, w_out,
                         w_router, b_router, w_sh_gate, w_sh_up, w_sh_down)
        mod = _ada(cvecs, w_ada[l], b_ada[l]).reshape(8, 6, d)
        mod_p, mod_s = mod[0:1], mod[1:1 + bs]
        zeros = jnp.zeros((bp, GLA_KEY_W, GLA_DV), F32)
        r_ctx, sf, sb, ckvn, kr = _mix_route_dispatch(
            xp, (bp, lp), mod_p, False, p, zeros, zeros, (), None, lp)
        new_f.append(sf.reshape(bp, GLA_HEADS, GLA_DK, GLA_DV))
        new_b.append(sb.reshape(bp, GLA_HEADS, GLA_DK, GLA_DV))
        new_ckv.append(ckvn.reshape(bp, lp, KV_LORA))
        new_kr.append(kr.reshape(bp, lp, LANES)[:, :, KR_LANE0:KR_LANE0 + ROPE_DIM])
        kr_cache = jnp.pad(cache_mla_krope[:, l],
                           ((0, 0), (0, 0), (ROPE_LANE0, LANES - ROPE_LANE0 - ROPE_DIM)))
        ctx_kv = _cache_kv(cache_mla_ckv[:, l], kr_cache, p)
        s0f = state_gla_fwd[:, l].reshape(bs, GLA_KEY_W, GLA_DV)
        s0b = state_gla_bwd[:, l].reshape(bs, GLA_KEY_W, GLA_DV)
        r_lat, _, _, _, _ = _mix_route_dispatch(
            xs, (bs, ls), mod_s, True, p, s0f, s0b, lat_tabs, ctx_kv, 256)
        experts = (w_exp_gate[l], w_exp_up[l], w_exp_down[l])
        yg_ctx = _experts_collect(r_ctx, *experts)
        yg_lat = _experts_collect(r_lat, *experts)
        xp = _combine(r_ctx['x1'], r_ctx['h2'], yg_ctx, r_ctx['w8'], mod_p, False, p, OUT_TILE)
        xs = _combine(r_lat['x1'], r_lat['h2'], yg_lat, r_lat['w8'], mod_s, True, p, OUT_TILE)
    return (xp.reshape(bp, lp, d), xs, jnp.stack(new_f, axis=1), jnp.stack(new_b, axis=1),
            jnp.stack(new_ckv, axis=1), jnp.stack(new_kr, axis=1))
```

```python
import functools

import jax
import jax.numpy as jnp
import numpy as np
from jax import lax
from jax.experimental import pallas as pl
from jax.experimental.pallas import tpu as pltpu
from jax.experimental.pallas import tpu_sc as plsc

F32 = jnp.float32
BF16 = jnp.bfloat16
I32 = jnp.int32
U32 = jnp.uint32

D_MODEL = 1024
EPS = 1e-6
GRID_W = 64
GLA_HEADS = 4
GLA_DK = 64
GLA_DV = 128
GLA_GATE_RANK = 16
GLA_GATE_NORM = 16.0
GLA_KEY_W = GLA_HEADS * GLA_DK
GLA_VAL_W = GLA_HEADS * GLA_DV
MLA_HEADS = 4
Q_LORA = 256
KV_LORA = 128
NOPE_DIM = 64
ROPE_DIM = 32
V_DIM = 128
QK_DIM = NOPE_DIM + ROPE_DIM
ROPE_THETA = 10000.0
N_EXPERTS = 64
TOP_K = 8
N_GROUPS = 8
TOPK_GROUPS = 4
D_EXPERT = 256
ROUTED_SCALE = 2.5

LANES = 128
HEAD_W = LANES
MLA_W = MLA_HEADS * HEAD_W
ROPE_LANE0 = NOPE_DIM
COL_V = 2 * GLA_KEY_W
COL_G = COL_V + GLA_VAL_W
COL_CQ = COL_G + GLA_VAL_W
COL_CKV = COL_CQ + Q_LORA
TAIL0 = COL_CKV + KV_LORA
IN_W = TAIL0 + LANES
KR_LANE0 = LANES - ROPE_DIM - 2 * GLA_GATE_RANK
GLA_BLOCK = 256
GLA_SUB = 64
IN_TILE = 256
OUT_TILE = 512
FFN_TILE = 512
PACK_PARTS = 2
PACK_W = D_MODEL // (2 * PACK_PARTS)
SC_WINDOW = 128
SC_LANES = 16
VMEM_LIMIT = 56 * 1024 * 1024

HIGHEST = lax.Precision.HIGHEST


def _dot(a, b, precision=None):
    return jnp.dot(a, b, preferred_element_type=F32, precision=precision)


def _dot_nt(a, b, precision=None):
    return lax.dot_general(a, b, (((1,), (1,)), ((), ())), preferred_element_type=F32,
                           precision=precision)


def _rms(x, width):
    ss = jnp.sum(x * x, axis=-1, keepdims=True) * (1.0 / width)
    return x * lax.rsqrt(ss + EPS)


def _silu(x):
    return x * jax.nn.sigmoid(x)


def _log_sigmoid(x):
    return jnp.minimum(x, 0.0) - jnp.log1p(jnp.exp(-jnp.abs(x)))


def _pack_rows(x):
    parts = []
    for i in range(PACK_PARTS):
        c0 = i * 2 * PACK_W
        lo = lax.bitcast_convert_type(x[:, c0:c0 + PACK_W].astype(BF16).astype(F32), U32)
        hi = lax.bitcast_convert_type(
            x[:, c0 + PACK_W:c0 + 2 * PACK_W].astype(BF16).astype(F32), U32)
        parts.append(hi | (lo >> 16))
    return parts


def _unpack_rows(parts):
    cols = []
    for w in parts:
        cols.append(lax.bitcast_convert_type(w << 16, F32))
        cols.append(lax.bitcast_convert_type(w & jnp.uint32(0xFFFF0000), F32))
    return jnp.concatenate(cols, axis=1)


def _ada_kernel(c_ref, w_ref, b_ref, o_ref):
    o_ref[...] = _dot(_silu(c_ref[...]), w_ref[...], precision=HIGHEST) + b_ref[...]


def _ada(cvecs, w_ada, b_ada):
    n = w_ada.shape[1]
    tn = 768
    return pl.pallas_call(
        _ada_kernel,
        grid=(n // tn,),
        in_specs=[pl.BlockSpec((8, D_MODEL), lambda j: (0, 0)),
                  pl.BlockSpec((D_MODEL, tn), lambda j: (0, j)),
                  pl.BlockSpec((1, tn), lambda j: (0, j))],
        out_specs=pl.BlockSpec((8, tn), lambda j: (0, j)),
        out_shape=jax.ShapeDtypeStruct((8, n), F32),
        name="ada",
    )(cvecs, w_ada, b_ada.reshape(1, n))


def _rope(x, c, s1, s2):
    return x * c + pltpu.roll(x, LANES - 8, 1) * s1 + pltpu.roll(x, 8, 1) * s2


def _mla_kv(ckv_n, kr, w_ukv_ref, gk, rope_tabs, k_ref, v_ref):
    kv = _dot(ckv_n.astype(BF16), w_ukv_ref[...])
    for h in range(MLA_HEADS):
        k_h = kv[:, h * HEAD_W:(h + 1) * HEAD_W] + kr
        k_h = _rms(k_h, QK_DIM) * gk
        if rope_tabs is not None:
            k_h = _rope(k_h, *rope_tabs)
        k_ref[0, :, h * HEAD_W:(h + 1) * HEAD_W] = k_h.astype(BF16)
    v_ref[0] = kv[:, MLA_W:].astype(BF16)


def _lane_mask(lo, hi, rows):
    lane = lax.broadcasted_iota(I32, (rows, LANES), 1)
    return (lane >= lo) & (lane < hi)


def _inproj_kernel(x_ref, mod_ref, g1_ref, win_ref, wgk_ref, bgk_ref, gql_ref, wuq_ref,
                   gkv_ref, wukv_ref, gq_ref, gk_ref, *refs, positions):
    tab_refs, outs = (refs[:3], refs[3:]) if positions else ((), refs)
    (q_ref, k_ref, v_ref, g_ref, lgf_ref, lgb_ref, qm_ref, km_ref, vm_ref, ckvn_ref,
     kr_ref) = outs
    x = x_ref[0]
    mod = mod_ref[0]
    h = _rms(x, D_MODEL) * g1_ref[...] * (1.0 + mod[1:2]) + mod[0:1]
    y = _dot(h.astype(BF16), win_ref[...])
    q_ref[0] = (y[:, 0:GLA_KEY_W] * GLA_DK ** -0.5).astype(BF16)
    k_ref[0] = y[:, GLA_KEY_W:COL_V].astype(BF16)
    v_ref[0] = y[:, COL_V:COL_G].astype(BF16)
    g_ref[0] = y[:, COL_G:COL_CQ].astype(BF16)
    tail = y[:, TAIL0:IN_W]
    pre = _dot(tail.astype(BF16), wgk_ref[...]) + bgk_ref[...]
    logg = _log_sigmoid(pre) * (1.0 / GLA_GATE_NORM)
    lgf_ref[0] = logg[:, 0:GLA_KEY_W]
    lgb_ref[0] = logg[:, GLA_KEY_W:]
    rows = x.shape[0]
    tabs = tuple(r[...] for r in tab_refs) if positions else None
    cq = _rms(y[:, COL_CQ:COL_CKV], Q_LORA) * gql_ref[...]
    qm = _dot(cq.astype(BF16), wuq_ref[...])
    gq = gq_ref[...]
    for hh in range(MLA_HEADS):
        q_h = _rms(qm[:, hh * HEAD_W:(hh + 1) * HEAD_W], QK_DIM) * gq
        if positions:
            q_h = _rope(q_h, *tabs)
        qm_ref[0, :, hh * HEAD_W:(hh + 1) * HEAD_W] = (q_h * QK_DIM ** -0.5).astype(BF16)
    ckv_n = _rms(y[:, COL_CKV:TAIL0], KV_LORA) * gkv_ref[...]
    ckvn_ref[0] = ckv_n
    kr_ref[0] = tail
    kr = jnp.where(_lane_mask(KR_LANE0, KR_LANE0 + ROPE_DIM, rows), tail, 0.0)
    _mla_kv(ckv_n, kr, wukv_ref, gk_ref[...], tabs, km_ref, vm_ref)


def _inproj(x, mod, per_batch_mod, p, rope_tabs, tm):
    b, l, d = x.shape
    nt = l // tm
    mod_map = (lambda bi, i: (bi, 0, 0)) if per_batch_mod else (lambda bi, i: (0, 0, 0))
    tab_map = lambda bi, i: (i, 0)
    const = lambda bi, i: (0, 0)
    tok = lambda w: pl.BlockSpec((1, tm, w), lambda bi, i: (bi, i, 0))
    full = lambda a: pl.BlockSpec(a.shape, const)
    weights = [p['g_norm1'], p['w_in_p'], p['w_gk_big'], p['b_gk'], p['g_q_lora'], p['w_uq_p'],
               p['g_kv_lora'], p['w_ukv_p'], p['gq'], p['gk']]
    outs = [(GLA_KEY_W, BF16), (GLA_KEY_W, BF16), (GLA_VAL_W, BF16), (GLA_VAL_W, BF16),
            (GLA_KEY_W, F32), (GLA_KEY_W, F32), (MLA_W, BF16), (MLA_W, BF16), (MLA_W, BF16),
            (KV_LORA, F32), (LANES, F32)]
    return pl.pallas_call(
        functools.partial(_inproj_kernel, positions=bool(rope_tabs)),
        grid=(b, nt),
        in_specs=[tok(d), pl.BlockSpec((1, 6, d), mod_map)] + [full(w) for w in weights]
                 + [pl.BlockSpec((tm, LANES), tab_map)] * len(rope_tabs),
        out_specs=[tok(w) for w, _ in outs],
        out_shape=[jax.ShapeDtypeStruct((b, l, w), dt) for w, dt in outs],
        compiler_params=pltpu.CompilerParams(
            dimension_semantics=("parallel", "parallel"), vmem_limit_bytes=VMEM_LIMIT),
        name="inproj",
    )(x, mod, *weights, *rope_tabs)


def _cache_kv_kernel(ckv_ref, kr_ref, wukv_ref, gk_ref, k_ref, v_ref):
    _mla_kv(ckv_ref[0], kr_ref[0], wukv_ref, gk_ref[...], None, k_ref, v_ref)


def _cache_kv(ckv, kr128, p):
    b, l, _ = ckv.shape
    tok = lambda w: pl.BlockSpec((1, l, w), lambda bi: (bi, 0, 0))
    full = lambda a: pl.BlockSpec(a.shape, lambda bi: (0, 0))
    return pl.pallas_call(
        _cache_kv_kernel,
        grid=(b,),
        in_specs=[tok(KV_LORA), tok(LANES), full(p['w_ukv_p']), full(p['gk'])],
        out_specs=[tok(MLA_W), tok(MLA_W)],
        out_shape=[jax.ShapeDtypeStruct((b, l, MLA_W), BF16)] * 2,
        name="cache_kv",
    )(ckv, kr128, p['w_ukv_p'], p['gk'])


def _split3(x):
    hi = x.astype(BF16)
    r1 = x - hi.astype(F32)
    mid = r1.astype(BF16)
    lo = (r1 - mid.astype(F32)).astype(BF16)
    return hi, mid, lo


def _gla_block(q, k, v, lg, s, fwd):
    n = GLA_BLOCK
    nc = n // GLA_SUB
    row = lax.broadcasted_iota(I32, (n, n), 0)
    col = lax.broadcasted_iota(I32, (n, n), 1)
    same = (row // GLA_SUB) == (col // GLA_SUB)
    causal = same & ((col <= row) if fwd else (col >= row))
    tri = causal.astype(BF16)
    hi, mid, lo = _split3(lg)
    cum = _dot(tri, hi) + _dot(tri, mid) + _dot(tri, lo)
    tot_rows, mid_rows = [], []
    for c in range(nc):
        r_tot = c * GLA_SUB + (GLA_SUB - 1 if fwd else 0)
        r_mid = c * GLA_SUB + GLA_SUB // 2
        tot_rows.append(jnp.broadcast_to(cum[r_tot:r_tot + 1], (GLA_SUB, GLA_KEY_W)))
        mid_rows.append(jnp.broadcast_to(cum[r_mid:r_mid + 1], (GLA_SUB, GLA_KEY_W)))
    tot_rows = jnp.concatenate(tot_rows, axis=0)
    mid_rows = jnp.concatenate(mid_rows, axis=0)
    rel = cum - mid_rows
    qi = q * jnp.exp(rel)
    ki = (k * jnp.exp(-rel)).astype(BF16)
    q_in = (q * jnp.exp(cum)).astype(BF16)
    k_up = k * jnp.exp(tot_rows - cum)
    dec = jnp.exp(tot_rows)
    k_up_t = k_up.T.astype(BF16)
    dec_t = dec.T
    lane_head = lax.broadcasted_iota(I32, (n, GLA_KEY_W), 1) // GLA_DK
    srow_head = lax.broadcasted_iota(I32, (GLA_KEY_W, GLA_DV), 0) // GLA_DK
    vrow_chunk = lax.broadcasted_iota(I32, (n, GLA_VAL_W), 0) // GLA_SUB
    o_heads = []
    for h in range(GLA_HEADS):
        a = _dot_nt(jnp.where(lane_head == h, qi, 0.0).astype(BF16), ki)
        a = jnp.where(causal, a, 0.0).astype(BF16)
        o_heads.append(_dot(a, v[:, h * GLA_DV:(h + 1) * GLA_DV]))
    o = jnp.concatenate(o_heads, axis=1)
    o_inter = [None] * nc
    for c in (range(nc) if fwd else range(nc - 1, -1, -1)):
        s_bd = jnp.concatenate(
            [jnp.where(srow_head == h, s, 0.0).astype(BF16) for h in range(GLA_HEADS)], axis=1)
        o_inter[c] = _dot(q_in[c * GLA_SUB:(c + 1) * GLA_SUB], s_bd)
        v_c = jnp.where(vrow_chunk == c, v, jnp.zeros_like(v))
        u = jnp.concatenate(
            [_dot(k_up_t[h * GLA_DK:(h + 1) * GLA_DK], v_c[:, h * GLA_DV:(h + 1) * GLA_DV])
             for h in range(GLA_HEADS)], axis=0)
        s = dec_t[:, c * GLA_SUB:c * GLA_SUB + 1] * s + u
    return o + jnp.concatenate(o_inter, axis=0), s


def _gla_kernel(q_ref, k_ref, v_ref, g_ref, lgf_ref, lgb_ref, s0f_ref, s0b_ref, gout_ref,
                o_ref, sf_ref, sb_ref, acc_ref, st_ref, *, nblk):
    st_ref[0] = s0f_ref[0]
    st_ref[1] = s0b_ref[0]

    def load(blk):
        start = blk * GLA_BLOCK
        r = pl.ds(start if isinstance(blk, int) else pl.multiple_of(start, GLA_BLOCK), GLA_BLOCK)
        return r, q_ref[0, r, :].astype(F32), k_ref[0, r, :].astype(F32), v_ref[0, r, :]

    def pair_step(i, first_touch):
        rf, qf, kf, vf = load(i)
        rb, qb, kb, vb = load(nblk - 1 - i)
        of, s_f = _gla_block(qf, kf, vf, lgf_ref[0, rf, :], st_ref[0], True)
        ob, s_b = _gla_block(qb, kb, vb, lgb_ref[0, rb, :], st_ref[1], False)
        if first_touch:
            acc_ref[rf, :] = of
            acc_ref[rb, :] = ob
        else:
            acc_ref[rf, :] += of
            acc_ref[rb, :] += ob
        st_ref[0] = s_f
        st_ref[1] = s_b

    if nblk == 1:
        r, q, k, v = load(0)
        of, s_f = _gla_block(q, k, v, lgf_ref[0, r, :], st_ref[0], True)
        ob, s_b = _gla_block(q, k, v, lgb_ref[0, r, :], st_ref[1], False)
        acc_ref[r, :] = of + ob
        st_ref[0] = s_f
        st_ref[1] = s_b
    else:
        half = nblk // 2
        lax.fori_loop(0, half, lambda i, c: (pair_step(i, True), c)[1], 0)
        lax.fori_loop(half, nblk, lambda i, c: (pair_step(i, False), c)[1], 0)
    sf_ref[0] = st_ref[0]
    sb_ref[0] = st_ref[1]

    def fin_step(blk, carry):
        r = pl.ds(pl.multiple_of(blk * GLA_BLOCK, GLA_BLOCK), GLA_BLOCK)
        o = acc_ref[r, :]
        gate = _silu(g_ref[0, r, :].astype(F32))
        for h in range(GLA_HEADS):
            sl = slice(h * GLA_DV, (h + 1) * GLA_DV)
            o_ref[0, r, sl] = (_rms(o[:, sl], GLA_DV) * gout_ref[...] * gate[:, sl]).astype(BF16)
        return carry

    lax.fori_loop(0, nblk, fin_step, 0)


def _gla(q, k, v, g, lgf, lgb, s0f, s0b, g_out):
    b, l, _ = q.shape
    assert l == GLA_BLOCK or l % (2 * GLA_BLOCK) == 0
    seq = lambda w: pl.BlockSpec((1, l, w), lambda bi: (bi, 0, 0))
    st = pl.BlockSpec((1, GLA_KEY_W, GLA_DV), lambda bi: (bi, 0, 0))
    return pl.pallas_call(
        functools.partial(_gla_kernel, nblk=l // GLA_BLOCK),
        grid=(b,),
        in_specs=[seq(GLA_KEY_W), seq(GLA_KEY_W), seq(GLA_VAL_W), seq(GLA_VAL_W),
                  seq(GLA_KEY_W), seq(GLA_KEY_W), st, st,
                  pl.BlockSpec((1, GLA_DV), lambda bi: (0, 0))],
        out_specs=[seq(GLA_VAL_W), st, st],
        out_shape=[jax.ShapeDtypeStruct((b, l, GLA_VAL_W), BF16),
                   jax.ShapeDtypeStruct((b, GLA_KEY_W, GLA_DV), F32),
                   jax.ShapeDtypeStruct((b, GLA_KEY_W, GLA_DV), F32)],
        scratch_shapes=[pltpu.VMEM((l, GLA_VAL_W), F32),
                        pltpu.VMEM((2, GLA_KEY_W, GLA_DV), F32)],
        compiler_params=pltpu.CompilerParams(
            dimension_semantics=("parallel",), vmem_limit_bytes=VMEM_LIMIT),
        name="gla",
    )(q, k, v, g, lgf, lgb, s0f, s0b, g_out)


def _attn_kernel(*refs, n_kv):
    q_ref, o_ref = refs[0], refs[-1]
    kv = [(refs[1 + 2 * i], refs[2 + 2 * i]) for i in range(n_kv)]
    for h in range(MLA_HEADS):
        sl = slice(h * HEAD_W, (h + 1) * HEAD_W)
        q = q_ref[0, :, sl]
        scores = [_dot_nt(q, k_ref[0, :, sl]) for k_ref, _ in kv]
        m = functools.reduce(jnp.maximum, [jnp.max(s, axis=-1, keepdims=True) for s in scores])
        o, den = 0.0, 0.0
        for s, (_, v_ref) in zip(scores, kv):
            p = jnp.exp(s - m)
            den = den + jnp.sum(p, axis=-1, keepdims=True)
            o = o + _dot(p.astype(BF16), v_ref[0, :, sl])
        o_ref[0, :, sl] = (o / den).astype(BF16)


def _attn(q, kv_pairs, tq):
    b, l, _ = q.shape
    kv_specs, kv_args = [], []
    for k, v in kv_pairs:
        spec = pl.BlockSpec((1, k.shape[1], MLA_W), lambda bi, i: (bi, 0, 0))
        kv_specs += [spec, spec]
        kv_args += [k, v]
    return pl.pallas_call(
        functools.partial(_attn_kernel, n_kv=len(kv_pairs)),
        grid=(b, l // tq),
        in_specs=[pl.BlockSpec((1, tq, MLA_W), lambda bi, i: (bi, i, 0))] + kv_specs,
        out_specs=pl.BlockSpec((1, tq, MLA_W), lambda bi, i: (bi, i, 0)),
        out_shape=jax.ShapeDtypeStruct((b, l, MLA_W), BF16),
        compiler_params=pltpu.CompilerParams(
            dimension_semantics=("parallel", "parallel"), vmem_limit_bytes=VMEM_LIMIT),
        name="attn",
    )(q, *kv_args)


def _sum01(x):
    return jnp.sum(jnp.sum(x, axis=1, keepdims=True), axis=0, keepdims=True)


def _route(logits_t, bias_col, cnt_col, cap):
    t = logits_t.shape[1]
    gsz = N_EXPERTS // N_GROUPS
    scores = jax.nn.sigmoid(logits_t)
    sel = (scores + bias_col).reshape(N_GROUPS, gsz, t)
    scores = scores.reshape(N_GROUPS, gsz, t)
    neg = jnp.float32(-jnp.inf)
    ie = lax.broadcasted_iota(I32, (N_GROUPS, gsz, t), 1)
    ig = lax.broadcasted_iota(I32, (N_GROUPS, gsz, t), 0)
    m1 = jnp.max(sel, axis=1, keepdims=True)
    first = jnp.min(jnp.where(sel == m1, ie, gsz), axis=1, keepdims=True)
    m2 = jnp.max(jnp.where(ie == first, neg, sel), axis=1, keepdims=True)
    grp = m1 + m2
    igk = lax.broadcasted_iota(I32, (N_GROUPS, 1, t), 0)
    g_sel = jnp.zeros((N_GROUPS, 1, t), jnp.bool_)
    cur = grp
    for _ in range(TOPK_GROUPS):
        m = jnp.max(cur, axis=0, keepdims=True)
        pick = igk == jnp.min(jnp.where(cur == m, igk, N_GROUPS), axis=0, keepdims=True)
        g_sel = g_sel | pick
        cur = jnp.where(pick, neg, cur)
    cur = jnp.where(g_sel, sel, neg)
    idx = ig * gsz + ie
    e_sel = jnp.zeros((N_GROUPS, gsz, t), jnp.bool_)
    picks = []
    for _ in range(TOP_K):
        m = jnp.max(jnp.max(cur, axis=1, keepdims=True), axis=0, keepdims=True)
        cand = jnp.where(cur == m, idx, N_EXPERTS)
        pick = idx == jnp.min(jnp.min(cand, axis=1, keepdims=True), axis=0, keepdims=True)
        picks.append(pick)
        e_sel = e_sel | pick
        cur = jnp.where(pick, neg, cur)
    w = jnp.where(e_sel, scores, 0.0)
    gate = w / _sum01(w) * ROUTED_SCALE
    sel_f = e_sel.astype(F32).reshape(N_EXPERTS, t)
    earlier = (lax.broadcasted_iota(I32, (t, t), 0) < lax.broadcasted_iota(I32, (t, t), 1))
    rank = _dot(sel_f.astype(BF16), earlier.astype(BF16))
    base = lax.broadcasted_iota(I32, (N_EXPERTS, 1), 0).astype(F32) * float(cap) + cnt_col
    slot = (base + rank).reshape(N_GROUPS, gsz, t)
    slot8 = jnp.concatenate([_sum01(jnp.where(pk, slot, 0.0)).reshape(1, t) for pk in picks],
                            axis=0).astype(I32)
    w8 = jnp.concatenate([_sum01(jnp.where(pk, gate, 0.0)).reshape(1, t) for pk in picks], axis=0)
    return slot8, w8, cnt_col + jnp.sum(sel_f, axis=1, keepdims=True)


def _outproj_kernel(og_ref, om_ref, x_ref, mod_ref, wtop_ref, wbot_ref, g2_ref, wr_ref, br_ref,
                    x1_ref, h2pa_ref, h2pb_ref, slot_ref, w8_ref, cnt_ref, cnt_scr, *, cap):
    @pl.when((pl.program_id(0) == 0) & (pl.program_id(1) == 0))
    def _():
        cnt_scr[...] = jnp.zeros_like(cnt_scr)

    mod = mod_ref[0]
    mix = _dot(og_ref[0], wtop_ref[...]) + _dot(om_ref[0], wbot_ref[...])
    x1 = x_ref[0] + mod[2:3] * mix
    x1_ref[0] = x1
    h2 = _rms(x1, D_MODEL) * g2_ref[...] * (1.0 + mod[4:5]) + mod[3:4]
    h2pa_ref[0], h2pb_ref[0] = _pack_rows(h2)
    logits_t = _dot_nt(wr_ref[...], h2, precision=HIGHEST)
    slot8, w8, cnt = _route(logits_t, br_ref[...], cnt_scr[:, 0:1], cap)
    slot_ref[...] = slot8
    t = w8.shape[1]
    w8_ref[0] = jnp.concatenate(
        [jnp.broadcast_to(w8[k:k + 1], (SC_LANES, t)) for k in range(TOP_K)], axis=0).T
    cnt_scr[...] = jnp.broadcast_to(cnt, cnt_scr.shape)
    cnt_ref[...] = cnt_scr[...]


def _outproj(og, om, x, mod, per_batch_mod, p, tm):
    b, l, d = x.shape
    nt = l // tm
    mod_map = (lambda bi, i: (bi, 0, 0)) if per_batch_mod else (lambda bi, i: (0, 0, 0))
    tok = lambda w: pl.BlockSpec((1, tm, w), lambda bi, i: (bi, i, 0))
    full = lambda a: pl.BlockSpec(a.shape, lambda bi, i: (0, 0))
    weights = [p['w_out_top'], p['w_out_bot'], p['g_norm2'], p['w_router_t'], p['b_router_col']]
    cnt_shape = (N_EXPERTS, LANES)
    return pl.pallas_call(
        functools.partial(_outproj_kernel, cap=b * l),
        grid=(b, nt),
        in_specs=[tok(GLA_VAL_W), tok(MLA_W), tok(d), pl.BlockSpec((1, 6, d), mod_map)]
                 + [full(w) for w in weights],
        out_specs=[tok(d), tok(PACK_W), tok(PACK_W),
                   pl.BlockSpec((TOP_K, tm), lambda bi, i: (0, bi * nt + i)),
                   tok(LANES), pl.BlockSpec(cnt_shape, lambda bi, i: (0, 0))],
        out_shape=[jax.ShapeDtypeStruct((b, l, d), F32), jax.ShapeDtypeStruct((b, l, PACK_W), U32),
                   jax.ShapeDtypeStruct((b, l, PACK_W), U32),
                   jax.ShapeDtypeStruct((TOP_K, b * l), I32),
                   jax.ShapeDtypeStruct((b, l, LANES), F32),
                   jax.ShapeDtypeStruct(cnt_shape, F32)],
        scratch_shapes=[pltpu.VMEM(cnt_shape, F32)],
        compiler_params=pltpu.CompilerParams(
            dimension_semantics=("arbitrary", "arbitrary"), vmem_limit_bytes=VMEM_LIMIT),
        name="outproj",
    )(og, om, x, mod, *weights)


def _sc_mesh():
    return plsc.VectorSubcoreMesh(core_axis_name="c", subcore_axis_name="s")


def _sc_scatter_rows(src, slot8, n_rows_out):
    t, d = src.shape
    nk = slot8.shape[0]

    @functools.partial(pl.kernel, out_type=jax.ShapeDtypeStruct((n_rows_out, d), src.dtype),
                       mesh=_sc_mesh(), scratch_types=[], name="dispatch")
    def run(src_hbm, slot_hbm, out_hbm):
        def body(x_vmem, i_vmem):
            for k in range(nk):
                pltpu.sync_copy(x_vmem, out_hbm.at[i_vmem.at[k]])

        pltpu.emit_pipeline(
            body, grid=(t // SC_WINDOW,),
            in_specs=[pl.BlockSpec((SC_WINDOW, d), lambda i: (i, 0)),
                      pl.BlockSpec((nk, SC_WINDOW), lambda i: (0, i))],
            out_specs=[], core_axis_name=("c", "s"),
            dimension_semantics=(pltpu.PARALLEL,))(src_hbm, slot_hbm)

    return run(src, slot8)


def _sc_collect_sum(table, slots_tk, w_rep):
    n_tok = w_rep.shape[0]
    d = table.shape[1]
    group = SC_WINDOW // TOP_K
    chunks = d // SC_LANES

    @functools.partial(pl.kernel, out_type=jax.ShapeDtypeStruct((n_tok, 2 * d), F32),
                       mesh=_sc_mesh(), scratch_types=[pltpu.VMEM((SC_WINDOW, d), table.dtype)],
                       compiler_params=pltpu.CompilerParams(needs_layout_passes=False),
                       name="collect")
    def run(tab_hbm, slot_hbm, w_hbm, out_hbm, rows_v):
        def body(i_vmem, w_vmem, o_vmem):
            pltpu.sync_copy(tab_hbm.at[i_vmem.at[0]], rows_v)

            @pl.loop(0, group)
            def _(j):
                wv = [w_vmem[j, pl.ds(k * SC_LANES, SC_LANES)] for k in range(TOP_K)]

                @pl.loop(0, chunks)
                def _(c):
                    off = c * SC_LANES
                    lo = jnp.zeros((SC_LANES,), F32)
                    hi = jnp.zeros((SC_LANES,), F32)
                    for k in range(TOP_K):
                        r = rows_v[j * TOP_K + k, pl.ds(off, SC_LANES)]
                        lo = lo + wv[k] * lax.bitcast_convert_type(r << 16, F32)
                        hi = hi + wv[k] * lax.bitcast_convert_type(
                            r & jnp.uint32(0xFFFF0000), F32)
                    o_vmem[j, pl.ds(off, SC_LANES)] = lo
                    o_vmem[j, pl.ds(d + off, SC_LANES)] = hi

        pltpu.emit_pipeline(
            body, grid=(n_tok // group,),
            in_specs=[pl.BlockSpec((1, SC_WINDOW), lambda i: (0, i)),
                      pl.BlockSpec((group, TOP_K * SC_LANES), lambda i: (i, 0))],
            out_specs=[pl.BlockSpec((group, 2 * d), lambda i: (i, 0))],
            core_axis_name=("c", "s"),
            dimension_semantics=(pltpu.PARALLEL,))(slot_hbm, w_hbm, out_hbm)

    return run(table, slots_tk, w_rep)


def _ffn_kernel(nt_ref, first_ref, nxt_ref, xsa_hbm, xsb_hbm, wg_ref, wu_ref, wd_ref,
                ysa_hbm, ysb_hbm, xbuf, ybuf, wg_b, wu_b, wd_b, sem_in, sem_out, *, cap):
    e = pl.program_id(0)
    n_exp = pl.num_programs(0)
    n = nt_ref[e]
    g0 = first_ref[e]
    total = first_ref[n_exp - 1] + nt_ref[n_exp - 1]
    xs_hbm = (xsa_hbm, xsb_hbm)
    ys_hbm = (ysa_hbm, ysb_hbm)

    def rows(ex, t):
        return pl.ds(pl.multiple_of(ex * cap + t * FFN_TILE, FFN_TILE), FFN_TILE)

    def in_copy(ex, t, slot, part):
        return pltpu.make_async_copy(xs_hbm[part].at[rows(ex, t)], xbuf.at[slot, part],
                                     sem_in.at[slot, part])

    def out_copy(ex, t, slot, part):
        return pltpu.make_async_copy(ybuf.at[slot, part], ys_hbm[part].at[rows(ex, t)],
                                     sem_out.at[slot, part])

    def wait_out(slot):
        for part in range(PACK_PARTS):
            out_copy(e, 0, slot, part).wait()

    @pl.when(n > 0)
    def _():
        @pl.when(g0 == 0)
        def _():
            for part in range(PACK_PARTS):
                in_copy(e, 0, 0, part).start()

        wg_b[...] = wg_ref[0].astype(BF16)
        wu_b[...] = wu_ref[0].astype(BF16)
        wd_b[...] = wd_ref[0].astype(BF16)
        nxt = nxt_ref[e]

        def tile(t, carry):
            g = g0 + t
            slot = g % 2
            for part in range(PACK_PARTS):
                in_copy(e, t, slot, part).wait()

            @pl.when(t + 1 < n)
            def _():
                for part in range(PACK_PARTS):
                    in_copy(e, t + 1, 1 - slot, part).start()

            @pl.when((t + 1 == n) & (nxt < n_exp))
            def _():
                for part in range(PACK_PARTS):
                    in_copy(nxt, 0, 1 - slot, part).start()

            @pl.when(g >= 2)
            def _():
                wait_out(slot)

            x = _unpack_rows([xbuf[slot, part] for part in range(PACK_PARTS)]).astype(BF16)
            a = _silu(_dot(x, wg_b[...])) * _dot(x, wu_b[...])
            y = _pack_rows(_dot(a.astype(BF16), wd_b[...]))
            for part in range(PACK_PARTS):
                ybuf[slot, part] = y[part]
                out_copy(e, t, slot, part).start()
            return carry

        lax.fori_loop(0, n, tile, 0)

        @pl.when(g0 + n == total)
        def _():
            @pl.when(total >= 2)
            def _():
                wait_out(total % 2)

            wait_out((total - 1) % 2)


def _ffn(xs_parts, counts, w_gate, w_up, w_down):
    n_exp, d = w_gate.shape[0], w_gate.shape[1]
    cap = xs_parts[0].shape[0] // n_exp
    ntiles = (counts + FFN_TILE - 1) // FFN_TILE
    first = jnp.cumsum(ntiles) - ntiles
    ids = jnp.arange(n_exp, dtype=I32)
    later_busy = (ids[None, :] > ids[:, None]) & (ntiles[None, :] > 0)
    nxt = jnp.min(jnp.where(later_busy, ids[None, :], n_exp), axis=1).astype(I32)
    wmap = lambda e, nt, first, nxt: (e, 0, 0)
    hbm = pl.BlockSpec(memory_space=pl.ANY)
    return pl.pallas_call(
        functools.partial(_ffn_kernel, cap=cap),
        grid_spec=pltpu.PrefetchScalarGridSpec(
            num_scalar_prefetch=3, grid=(n_exp,),
            in_specs=[hbm] * PACK_PARTS
                     + [pl.BlockSpec((1, d, D_EXPERT), wmap), pl.BlockSpec((1, d, D_EXPERT), wmap),
                        pl.BlockSpec((1, D_EXPERT, d), wmap)],
            out_specs=[hbm] * PACK_PARTS,
            scratch_shapes=[pltpu.VMEM((2, PACK_PARTS, FFN_TILE, PACK_W), U32),
                            pltpu.VMEM((2, PACK_PARTS, FFN_TILE, PACK_W), U32),
                            pltpu.VMEM((d, D_EXPERT), BF16), pltpu.VMEM((d, D_EXPERT), BF16),
                            pltpu.VMEM((D_EXPERT, d), BF16),
                            pltpu.SemaphoreType.DMA((2, PACK_PARTS)),
                            pltpu.SemaphoreType.DMA((2, PACK_PARTS))]),
        out_shape=[jax.ShapeDtypeStruct(xs_parts[0].shape, U32)] * PACK_PARTS,
        compiler_params=pltpu.CompilerParams(
            dimension_semantics=("arbitrary",), vmem_limit_bytes=VMEM_LIMIT),
        name="ffn",
    )(ntiles.astype(I32), first.astype(I32), nxt, *xs_parts, w_gate, w_up, w_down)


def _combine_kernel(x1_ref, h2pa_ref, h2pb_ref, ra_ref, rb_ref, mod_ref,
                    wsg_ref, wsu_ref, wsd_ref, o_ref):
    h2 = _unpack_rows([h2pa_ref[0], h2pb_ref[0]]).astype(BF16)
    a = _silu(_dot(h2, wsg_ref[...])) * _dot(h2, wsu_ref[...])
    routed = jnp.concatenate([ra_ref[0], rb_ref[0]], axis=1)
    o_ref[0] = x1_ref[0] + mod_ref[0][5:6] * (_dot(a.astype(BF16), wsd_ref[...]) + routed)


def _combine(x1, h2p_parts, routed_parts, mod, per_batch_mod, p, tm):
    b, l, d = x1.shape
    mod_map = (lambda bi, i: (bi, 0, 0)) if per_batch_mod else (lambda bi, i: (0, 0, 0))
    tok = lambda w: pl.BlockSpec((1, tm, w), lambda bi, i: (bi, i, 0))
    full = lambda a: pl.BlockSpec(a.shape, lambda bi, i: (0, 0))
    weights = [p['w_sh_gate'], p['w_sh_up'], p['w_sh_down']]
    routed_parts = [r.reshape(b, l, d // PACK_PARTS) for r in routed_parts]
    return pl.pallas_call(
        _combine_kernel,
        grid=(b, l // tm),
        in_specs=[tok(d)] + [tok(PACK_W)] * PACK_PARTS + [tok(d // PACK_PARTS)] * PACK_PARTS
                 + [pl.BlockSpec((1, 6, d), mod_map)] + [full(w) for w in weights],
        out_specs=tok(d),
        out_shape=jax.ShapeDtypeStruct((b, l, d), F32),
        compiler_params=pltpu.CompilerParams(
            dimension_semantics=("parallel", "parallel"), vmem_limit_bytes=VMEM_LIMIT),
        name="combine",
    )(x1, *h2p_parts, *routed_parts, mod, *weights)


def _pad_heads(w, parts):
    k = w.shape[0]
    per = w.shape[1] // MLA_HEADS
    w = w.reshape(k, MLA_HEADS, per)[:, :, parts[0]:parts[1]]
    w = jnp.pad(w, ((0, 0), (0, 0), (0, HEAD_W - (parts[1] - parts[0]))))
    return w.reshape(k, MLA_HEADS * HEAD_W)


def _prep_params(l, g_norm1, g_norm2, w_in, w_gk_fwd, b_gk_fwd, w_gk_bwd, b_gk_bwd, g_gla_out,
                 g_q_lora, w_uq, g_kv_lora, w_ukv, g_qk_q, g_qk_k, w_out, w_router, b_router,
                 w_sh_gate, w_sh_up, w_sh_down):
    w = w_in[l]
    d = w.shape[0]
    o_lrf = COL_CQ
    o_lrb = o_lrf + GLA_GATE_RANK
    o_cq = o_lrb + GLA_GATE_RANK
    o_ckv = o_cq + Q_LORA
    o_kr = o_ckv + KV_LORA
    w_in_p = jnp.concatenate([
        w[:, 0:COL_CQ], w[:, o_cq:o_ckv], w[:, o_ckv:o_kr], jnp.zeros((d, KR_LANE0), w.dtype),
        w[:, o_kr:o_kr + ROPE_DIM], w[:, o_lrf:o_lrb], w[:, o_lrb:o_cq]], axis=1).astype(BF16)
    z = jnp.zeros((GLA_GATE_RANK, GLA_KEY_W), F32)
    w_gk_big = jnp.concatenate([
        jnp.zeros((LANES - 2 * GLA_GATE_RANK, 2 * GLA_KEY_W), F32),
        jnp.concatenate([w_gk_fwd[l], z], axis=1),
        jnp.concatenate([z, w_gk_bwd[l]], axis=1)], axis=0).astype(BF16)
    pad_gain = lambda g: jnp.pad(g, (0, HEAD_W - QK_DIM)).reshape(1, HEAD_W)
    w_ukv_h = w_ukv[l]
    return {
        'g_norm1': g_norm1[l].reshape(1, d), 'g_norm2': g_norm2[l].reshape(1, d),
        'w_in_p': w_in_p, 'w_gk_big': w_gk_big,
        'b_gk': jnp.concatenate([b_gk_fwd[l], b_gk_bwd[l]]).reshape(1, 2 * GLA_KEY_W),
        'g_gla_out': g_gla_out[l].reshape(1, GLA_DV),
        'g_q_lora': g_q_lora[l].reshape(1, Q_LORA),
        'w_uq_p': _pad_heads(w_uq[l], (0, QK_DIM)).astype(BF16),
        'g_kv_lora': g_kv_lora[l].reshape(1, KV_LORA),
        'w_ukv_p': jnp.concatenate([_pad_heads(w_ukv_h, (0, NOPE_DIM)),
                                    _pad_heads(w_ukv_h, (NOPE_DIM, NOPE_DIM + V_DIM))],
                                   axis=1).astype(BF16),
        'gq': pad_gain(g_qk_q[l]), 'gk': pad_gain(g_qk_k[l]),
        'w_out_top': w_out[l][:GLA_VAL_W].astype(BF16),
        'w_out_bot': w_out[l][GLA_VAL_W:].astype(BF16),
        'w_router_t': w_router[l].T, 'b_router_col': b_router[l].reshape(N_EXPERTS, 1),
        'w_sh_gate': w_sh_gate[l].astype(BF16), 'w_sh_up': w_sh_up[l].astype(BF16),
        'w_sh_down': w_sh_down[l].astype(BF16),
    }


def _rope_tables(length):
    pos = np.arange(length)
    r = (pos // GRID_W).astype(np.float32)
    c = (pos % GRID_W).astype(np.float32)
    half = ROPE_DIM // 4
    inv_freq = np.float32(ROPE_THETA) ** (-np.arange(half, dtype=np.float32) / np.float32(half))
    ang_r = r[:, None] * inv_freq[None, :]
    ang_c = c[:, None] * inv_freq[None, :]
    zeros = lambda w: np.zeros((length, w), np.float32)
    ones = lambda w: np.ones((length, w), np.float32)
    tail_w = HEAD_W - ROPE_LANE0 - ROPE_DIM
    cos = np.concatenate([ones(ROPE_LANE0), np.cos(ang_r), np.cos(ang_r), np.cos(ang_c),
                          np.cos(ang_c), ones(tail_w)], axis=1)
    s1 = np.concatenate([zeros(ROPE_LANE0), -np.sin(ang_r), zeros(half), -np.sin(ang_c),
                         zeros(half), zeros(tail_w)], axis=1)
    s2 = np.concatenate([zeros(ROPE_LANE0), zeros(half), np.sin(ang_r), zeros(half),
                         np.sin(ang_c), zeros(tail_w)], axis=1)
    return tuple(jnp.asarray(t, F32) for t in (cos, s1, s2))


def _mix_route_dispatch(x_tok, seq_shape, mod, per_batch_mod, p, s0f, s0b, rope_tabs, ctx_kv,
                        tq):
    bt, lt, d = x_tok.shape
    b, l = seq_shape
    n_tok = bt * lt
    (q, k, v, g, lgf, lgb, qm, km, vm, ckvn, kr) = _inproj(x_tok, mod, per_batch_mod, p,
                                                            rope_tabs, IN_TILE)
    seq = lambda a: a.reshape(b, l, a.shape[-1])
    og, sf, sb = _gla(seq(q), seq(k), seq(v), seq(g), seq(lgf), seq(lgb), s0f, s0b,
                      p['g_gla_out'])
    kv_pairs = [(seq(km), seq(vm))] + ([ctx_kv] if ctx_kv is not None else [])
    om = _attn(seq(qm), kv_pairs, tq)
    tokv = lambda a: a.reshape(bt, lt, a.shape[-1])
    x1, h2a, h2b, slot8, w8, cnt = _outproj(tokv(og), tokv(om), x_tok, mod, per_batch_mod, p,
                                            OUT_TILE)
    x_sorted = [_sc_scatter_rows(h.reshape(n_tok, PACK_W), slot8, N_EXPERTS * n_tok)
                for h in (h2a, h2b)]
    routed = dict(x1=x1, h2=(h2a, h2b), slot8=slot8, w8=w8, cnt=cnt, x_sorted=x_sorted)
    return routed, sf, sb, ckvn, kr


def _experts_collect(r, w_gate, w_up, w_down):
    n_tok = r['slot8'].shape[1]
    y_sorted = _ffn(r['x_sorted'], r['cnt'][:, 0].astype(I32), w_gate, w_up, w_down)
    slots_tk = r['slot8'].T.reshape(1, TOP_K * n_tok)
    w_rep = r['w8'].reshape(n_tok, LANES)
    return [_sc_collect_sum(y, slots_tk, w_rep) for y in y_sorted]


def kernel(x_prompt, x_sample, c, state_gla_fwd, state_gla_bwd, cache_mla_ckv, cache_mla_krope,
           c_ctx, w_ada, b_ada, g_norm1, g_norm2, w_in, w_gk_fwd, b_gk_fwd, w_gk_bwd, b_gk_bwd,
           g_gla_out, g_q_lora, w_uq, g_kv_lora, w_ukv, g_qk_q, g_qk_k, w_out,
           w_router, b_router, w_exp_gate, w_exp_up, w_exp_down, w_sh_gate, w_sh_up, w_sh_down):
    bp, lp, d = x_prompt.shape
    bs, ls, _ = x_sample.shape
    depth = w_ada.shape[0]
    xp = x_prompt.reshape(1, bp * lp, d)
    xs = x_sample
    new_f, new_b, new_ckv, new_kr = [], [], [], []
    lat_tabs = _rope_tables(ls)
    cvecs = jnp.concatenate([c_ctx[None], c, jnp.zeros((8 - 1 - bs, d), F32)], axis=0)
    for l in range(depth):
        p = _prep_params(l, g_norm1, g_norm2, w_in, w_gk_fwd, b_gk_fwd, w_gk_bwd, b_gk_bwd,
                         g_gla_out, g_q_lora, w_uq, g_kv_lora, w_ukv, g_qk_q, g_qk_k, w_out,
                         w_router, b_router, w_sh_gate, w_sh_up, w_sh_down)
        mod = _ada(cvecs, w_ada[l], b_ada[l]).reshape(8, 6, d)
        mod_p, mod_s = mod[0:1], mod[1:1 + bs]
        zeros = jnp.zeros((bp, GLA_KEY_W, GLA_DV), F32)
        r_ctx, sf, sb, ckvn, kr = _mix_route_dispatch(
            xp, (bp, lp), mod_p, False, p, zeros, zeros, (), None, lp)
        new_f.append(sf.reshape(bp, GLA_HEADS, GLA_DK, GLA_DV))
        new_b.append(sb.reshape(bp, GLA_HEADS, GLA_DK, GLA_DV))
        new_ckv.append(ckvn.reshape(bp, lp, KV_LORA))
        new_kr.append(kr.reshape(bp, lp, LANES)[:, :, KR_LANE0:KR_LANE0 + ROPE_DIM])
        kr_cache = jnp.pad(cache_mla_krope[:, l],
                           ((0, 0), (0, 0), (ROPE_LANE0, LANES - ROPE_LANE0 - ROPE_DIM)))
        ctx_kv = _cache_kv(cache_mla_ckv[:, l], kr_cache, p)
        s0f = state_gla_fwd[:, l].reshape(bs, GLA_KEY_W, GLA_DV)
        s0b = state_gla_bwd[:, l].reshape(bs, GLA_KEY_W, GLA_DV)
        r_lat, _, _, _, _ = _mix_route_dispatch(
            xs, (bs, ls), mod_s, True, p, s0f, s0b, lat_tabs, ctx_kv, 256)
        experts = (w_exp_gate[l], w_exp_up[l], w_exp_down[l])
        routed_ctx = _experts_collect(r_ctx, *experts)
        routed_lat = _experts_collect(r_lat, *experts)
        xp = _combine(r_ctx['x1'], r_ctx['h2'], routed_ctx, mod_p, False, p, OUT_TILE)
        xs = _combine(r_lat['x1'], r_lat['h2'], routed_lat, mod_s, True, p, OUT_TILE)
    return (xp.reshape(bp, lp, d), xs, jnp.stack(new_f, axis=1), jnp.stack(new_b, axis=1),
            jnp.stack(new_ckv, axis=1), jnp.stack(new_kr, axis=1))
```

```python
import functools

import jax
import jax.numpy as jnp
import numpy as np
from jax import lax
from jax.experimental import pallas as pl
from jax.experimental.pallas import tpu as pltpu
from jax.experimental.pallas import tpu_sc as plsc

F32 = jnp.float32
BF16 = jnp.bfloat16
I32 = jnp.int32
U32 = jnp.uint32

D_MODEL = 1024
EPS = 1e-6
GRID_W = 64
GLA_HEADS = 4
GLA_DK = 64
GLA_DV = 128
GLA_GATE_RANK = 16
GLA_GATE_NORM = 16.0
GLA_KEY_W = GLA_HEADS * GLA_DK
GLA_VAL_W = GLA_HEADS * GLA_DV
MLA_HEADS = 4
Q_LORA = 256
KV_LORA = 128
NOPE_DIM = 64
ROPE_DIM = 32
V_DIM = 128
QK_DIM = NOPE_DIM + ROPE_DIM
ROPE_THETA = 10000.0
N_EXPERTS = 64
TOP_K = 8
N_GROUPS = 8
TOPK_GROUPS = 4
D_EXPERT = 256
ROUTED_SCALE = 2.5

LANES = 128
HEAD_W = LANES
MLA_W = MLA_HEADS * HEAD_W
ROPE_LANE0 = NOPE_DIM
COL_V = 2 * GLA_KEY_W
COL_G = COL_V + GLA_VAL_W
COL_CQ = COL_G + GLA_VAL_W
COL_CKV = COL_CQ + Q_LORA
TAIL0 = COL_CKV + KV_LORA
IN_W = TAIL0 + LANES
KR_LANE0 = LANES - ROPE_DIM - 2 * GLA_GATE_RANK
GLA_BLOCK = 256
GLA_SUB = 64
IN_TILE = 256
OUT_TILE = 512
FFN_TILE = 512
FFN_AHEAD = 3
PACK_PARTS = 2
PACK_W = D_MODEL // (2 * PACK_PARTS)
SC_WINDOW = 128
SC_LANES = 16
VMEM_LIMIT = 56 * 1024 * 1024

HIGHEST = lax.Precision.HIGHEST


def _dot(a, b, precision=None):
    return jnp.dot(a, b, preferred_element_type=F32, precision=precision)


def _dot_nt(a, b, precision=None):
    return lax.dot_general(a, b, (((1,), (1,)), ((), ())), preferred_element_type=F32,
                           precision=precision)


def _rms(x, width):
    ss = jnp.sum(x * x, axis=-1, keepdims=True) * (1.0 / width)
    return x * lax.rsqrt(ss + EPS)


def _silu(x):
    return x * jax.nn.sigmoid(x)


def _log_sigmoid(x):
    return jnp.minimum(x, 0.0) - jnp.log1p(jnp.exp(-jnp.abs(x)))


def _pack_rows(x):
    parts = []
    for i in range(PACK_PARTS):
        c0 = i * 2 * PACK_W
        lo = lax.bitcast_convert_type(x[:, c0:c0 + PACK_W].astype(BF16).astype(F32), U32)
        hi = lax.bitcast_convert_type(
            x[:, c0 + PACK_W:c0 + 2 * PACK_W].astype(BF16).astype(F32), U32)
        parts.append(hi | (lo >> 16))
    return parts


def _unpack_rows(parts):
    cols = []
    for w in parts:
        cols.append(lax.bitcast_convert_type(w << 16, F32))
        cols.append(lax.bitcast_convert_type(w & jnp.uint32(0xFFFF0000), F32))
    return jnp.concatenate(cols, axis=1)


def _ada_kernel(c_ref, w_ref, b_ref, o_ref):
    o_ref[...] = _dot(_silu(c_ref[...]), w_ref[...], precision=HIGHEST) + b_ref[...]


def _ada(cvecs, w_ada, b_ada):
    n = w_ada.shape[1]
    tn = 768
    return pl.pallas_call(
        _ada_kernel,
        grid=(n // tn,),
        in_specs=[pl.BlockSpec((8, D_MODEL), lambda j: (0, 0)),
                  pl.BlockSpec((D_MODEL, tn), lambda j: (0, j)),
                  pl.BlockSpec((1, tn), lambda j: (0, j))],
        out_specs=pl.BlockSpec((8, tn), lambda j: (0, j)),
        out_shape=jax.ShapeDtypeStruct((8, n), F32),
        name="ada",
    )(cvecs, w_ada, b_ada.reshape(1, n))


def _rope(x, c, s1, s2):
    return x * c + pltpu.roll(x, LANES - 8, 1) * s1 + pltpu.roll(x, 8, 1) * s2


def _mla_kv(ckv_n, kr, w_ukv_ref, gk, rope_tabs, k_ref, v_ref):
    kv = _dot(ckv_n.astype(BF16), w_ukv_ref[...])
    for h in range(MLA_HEADS):
        k_h = kv[:, h * HEAD_W:(h + 1) * HEAD_W] + kr
        k_h = _rms(k_h, QK_DIM) * gk
        if rope_tabs is not None:
            k_h = _rope(k_h, *rope_tabs)
        k_ref[0, :, h * HEAD_W:(h + 1) * HEAD_W] = k_h.astype(BF16)
    v_ref[0] = kv[:, MLA_W:].astype(BF16)


def _lane_mask(lo, hi, rows):
    lane = lax.broadcasted_iota(I32, (rows, LANES), 1)
    return (lane >= lo) & (lane < hi)


def _inproj_kernel(x_ref, mod_ref, g1_ref, win_ref, wgk_ref, bgk_ref, gql_ref, wuq_ref,
                   gkv_ref, wukv_ref, gq_ref, gk_ref, *refs, positions):
    tab_refs, outs = (refs[:3], refs[3:]) if positions else ((), refs)
    (q_ref, k_ref, v_ref, g_ref, lgf_ref, lgb_ref, qm_ref, km_ref, vm_ref, ckvn_ref,
     kr_ref) = outs
    x = x_ref[0]
    mod = mod_ref[0]
    h = _rms(x, D_MODEL) * g1_ref[...] * (1.0 + mod[1:2]) + mod[0:1]
    y = _dot(h.astype(BF16), win_ref[...])
    q_ref[0] = (y[:, 0:GLA_KEY_W] * GLA_DK ** -0.5).astype(BF16)
    k_ref[0] = y[:, GLA_KEY_W:COL_V].astype(BF16)
    v_ref[0] = y[:, COL_V:COL_G].astype(BF16)
    g_ref[0] = y[:, COL_G:COL_CQ].astype(BF16)
    tail = y[:, TAIL0:IN_W]
    pre = _dot(tail.astype(BF16), wgk_ref[...]) + bgk_ref[...]
    logg = _log_sigmoid(pre) * (1.0 / GLA_GATE_NORM)
    lgf_ref[0] = logg[:, 0:GLA_KEY_W]
    lgb_ref[0] = logg[:, GLA_KEY_W:]
    rows = x.shape[0]
    tabs = tuple(r[...] for r in tab_refs) if positions else None
    cq = _rms(y[:, COL_CQ:COL_CKV], Q_LORA) * gql_ref[...]
    qm = _dot(cq.astype(BF16), wuq_ref[...])
    gq = gq_ref[...]
    for hh in range(MLA_HEADS):
        q_h = _rms(qm[:, hh * HEAD_W:(hh + 1) * HEAD_W], QK_DIM) * gq
        if positions:
            q_h = _rope(q_h, *tabs)
        qm_ref[0, :, hh * HEAD_W:(hh + 1) * HEAD_W] = (q_h * QK_DIM ** -0.5).astype(BF16)
    ckv_n = _rms(y[:, COL_CKV:TAIL0], KV_LORA) * gkv_ref[...]
    ckvn_ref[0] = ckv_n
    kr_ref[0] = tail
    kr = jnp.where(_lane_mask(KR_LANE0, KR_LANE0 + ROPE_DIM, rows), tail, 0.0)
    _mla_kv(ckv_n, kr, wukv_ref, gk_ref[...], tabs, km_ref, vm_ref)


def _inproj(x, mod, per_batch_mod, p, rope_tabs, tm):
    b, l, d = x.shape
    nt = l // tm
    mod_map = (lambda bi, i: (bi, 0, 0)) if per_batch_mod else (lambda bi, i: (0, 0, 0))
    tab_map = lambda bi, i: (i, 0)
    const = lambda bi, i: (0, 0)
    tok = lambda w: pl.BlockSpec((1, tm, w), lambda bi, i: (bi, i, 0))
    full = lambda a: pl.BlockSpec(a.shape, const)
    weights = [p['g_norm1'], p['w_in_p'], p['w_gk_big'], p['b_gk'], p['g_q_lora'], p['w_uq_p'],
               p['g_kv_lora'], p['w_ukv_p'], p['gq'], p['gk']]
    outs = [(GLA_KEY_W, BF16), (GLA_KEY_W, BF16), (GLA_VAL_W, BF16), (GLA_VAL_W, BF16),
            (GLA_KEY_W, F32), (GLA_KEY_W, F32), (MLA_W, BF16), (MLA_W, BF16), (MLA_W, BF16),
            (KV_LORA, F32), (LANES, F32)]
    return pl.pallas_call(
        functools.partial(_inproj_kernel, positions=bool(rope_tabs)),
        grid=(b, nt),
        in_specs=[tok(d), pl.BlockSpec((1, 6, d), mod_map)] + [full(w) for w in weights]
                 + [pl.BlockSpec((tm, LANES), tab_map)] * len(rope_tabs),
        out_specs=[tok(w) for w, _ in outs],
        out_shape=[jax.ShapeDtypeStruct((b, l, w), dt) for w, dt in outs],
        compiler_params=pltpu.CompilerParams(
            dimension_semantics=("parallel", "parallel"), vmem_limit_bytes=VMEM_LIMIT),
        name="inproj",
    )(x, mod, *weights, *rope_tabs)


def _cache_kv_kernel(ckv_ref, kr_ref, wukv_ref, gk_ref, k_ref, v_ref):
    _mla_kv(ckv_ref[0], kr_ref[0], wukv_ref, gk_ref[...], None, k_ref, v_ref)


def _cache_kv(ckv, kr128, p):
    b, l, _ = ckv.shape
    tok = lambda w: pl.BlockSpec((1, l, w), lambda bi: (bi, 0, 0))
    full = lambda a: pl.BlockSpec(a.shape, lambda bi: (0, 0))
    return pl.pallas_call(
        _cache_kv_kernel,
        grid=(b,),
        in_specs=[tok(KV_LORA), tok(LANES), full(p['w_ukv_p']), full(p['gk'])],
        out_specs=[tok(MLA_W), tok(MLA_W)],
        out_shape=[jax.ShapeDtypeStruct((b, l, MLA_W), BF16)] * 2,
        name="cache_kv",
    )(ckv, kr128, p['w_ukv_p'], p['gk'])


def _split3(x):
    hi = x.astype(BF16)
    r1 = x - hi.astype(F32)
    mid = r1.astype(BF16)
    lo = (r1 - mid.astype(F32)).astype(BF16)
    return hi, mid, lo


def _gla_block(q, k, v, lg, s, fwd):
    n = GLA_BLOCK
    nc = n // GLA_SUB
    row = lax.broadcasted_iota(I32, (n, n), 0)
    col = lax.broadcasted_iota(I32, (n, n), 1)
    same = (row // GLA_SUB) == (col // GLA_SUB)
    causal = same & ((col <= row) if fwd else (col >= row))
    tri = causal.astype(BF16)
    hi, mid, lo = _split3(lg)
    cum = _dot(tri, hi) + _dot(tri, mid) + _dot(tri, lo)
    tot_rows, mid_rows = [], []
    for c in range(nc):
        r_tot = c * GLA_SUB + (GLA_SUB - 1 if fwd else 0)
        r_mid = c * GLA_SUB + GLA_SUB // 2
        tot_rows.append(jnp.broadcast_to(cum[r_tot:r_tot + 1], (GLA_SUB, GLA_KEY_W)))
        mid_rows.append(jnp.broadcast_to(cum[r_mid:r_mid + 1], (GLA_SUB, GLA_KEY_W)))
    tot_rows = jnp.concatenate(tot_rows, axis=0)
    mid_rows = jnp.concatenate(mid_rows, axis=0)
    rel = cum - mid_rows
    qi = q * jnp.exp(rel)
    ki = (k * jnp.exp(-rel)).astype(BF16)
    q_in = (q * jnp.exp(cum)).astype(BF16)
    k_up = k * jnp.exp(tot_rows - cum)
    dec = jnp.exp(tot_rows)
    k_up_t = k_up.T.astype(BF16)
    dec_t = dec.T
    lane_head = lax.broadcasted_iota(I32, (n, GLA_KEY_W), 1) // GLA_DK
    srow_head = lax.broadcasted_iota(I32, (GLA_KEY_W, GLA_DV), 0) // GLA_DK
    vrow_chunk = lax.broadcasted_iota(I32, (n, GLA_VAL_W), 0) // GLA_SUB
    o_heads = []
    for h in range(GLA_HEADS):
        a = _dot_nt(jnp.where(lane_head == h, qi, 0.0).astype(BF16), ki)
        a = jnp.where(causal, a, 0.0).astype(BF16)
        o_heads.append(_dot(a, v[:, h * GLA_DV:(h + 1) * GLA_DV]))
    o = jnp.concatenate(o_heads, axis=1)
    o_inter = [None] * nc
    for c in (range(nc) if fwd else range(nc - 1, -1, -1)):
        s_bd = jnp.concatenate(
            [jnp.where(srow_head == h, s, 0.0).astype(BF16) for h in range(GLA_HEADS)], axis=1)
        o_inter[c] = _dot(q_in[c * GLA_SUB:(c + 1) * GLA_SUB], s_bd)
        v_c = jnp.where(vrow_chunk == c, v, jnp.zeros_like(v))
        u = jnp.concatenate(
            [_dot(k_up_t[h * GLA_DK:(h + 1) * GLA_DK], v_c[:, h * GLA_DV:(h + 1) * GLA_DV])
             for h in range(GLA_HEADS)], axis=0)
        s = dec_t[:, c * GLA_SUB:c * GLA_SUB + 1] * s + u
    return o + jnp.concatenate(o_inter, axis=0), s


def _gla_kernel(q_ref, k_ref, v_ref, g_ref, lgf_ref, lgb_ref, s0f_ref, s0b_ref, gout_ref,
                o_ref, sf_ref, sb_ref, acc_ref, st_ref, *, nblk):
    st_ref[0] = s0f_ref[0]
    st_ref[1] = s0b_ref[0]

    def load(blk):
        start = blk * GLA_BLOCK
        r = pl.ds(start if isinstance(blk, int) else pl.multiple_of(start, GLA_BLOCK), GLA_BLOCK)
        return r, q_ref[0, r, :].astype(F32), k_ref[0, r, :].astype(F32), v_ref[0, r, :]

    def pair_step(i, first_touch):
        rf, qf, kf, vf = load(i)
        rb, qb, kb, vb = load(nblk - 1 - i)
        of, s_f = _gla_block(qf, kf, vf, lgf_ref[0, rf, :], st_ref[0], True)
        ob, s_b = _gla_block(qb, kb, vb, lgb_ref[0, rb, :], st_ref[1], False)
        if first_touch:
            acc_ref[rf, :] = of
            acc_ref[rb, :] = ob
        else:
            acc_ref[rf, :] += of
            acc_ref[rb, :] += ob
        st_ref[0] = s_f
        st_ref[1] = s_b

    if nblk == 1:
        r, q, k, v = load(0)
        of, s_f = _gla_block(q, k, v, lgf_ref[0, r, :], st_ref[0], True)
        ob, s_b = _gla_block(q, k, v, lgb_ref[0, r, :], st_ref[1], False)
        acc_ref[r, :] = of + ob
        st_ref[0] = s_f
        st_ref[1] = s_b
    else:
        half = nblk // 2
        lax.fori_loop(0, half, lambda i, c: (pair_step(i, True), c)[1], 0)
        lax.fori_loop(half, nblk, lambda i, c: (pair_step(i, False), c)[1], 0)
    sf_ref[0] = st_ref[0]
    sb_ref[0] = st_ref[1]

    def fin_step(blk, carry):
        r = pl.ds(pl.multiple_of(blk * GLA_BLOCK, GLA_BLOCK), GLA_BLOCK)
        o = acc_ref[r, :]
        gate = _silu(g_ref[0, r, :].astype(F32))
        for h in range(GLA_HEADS):
            sl = slice(h * GLA_DV, (h + 1) * GLA_DV)
            o_ref[0, r, sl] = (_rms(o[:, sl], GLA_DV) * gout_ref[...] * gate[:, sl]).astype(BF16)
        return carry

    lax.fori_loop(0, nblk, fin_step, 0)


def _gla(q, k, v, g, lgf, lgb, s0f, s0b, g_out):
    b, l, _ = q.shape
    assert l == GLA_BLOCK or l % (2 * GLA_BLOCK) == 0
    seq = lambda w: pl.BlockSpec((1, l, w), lambda bi: (bi, 0, 0))
    st = pl.BlockSpec((1, GLA_KEY_W, GLA_DV), lambda bi: (bi, 0, 0))
    return pl.pallas_call(
        functools.partial(_gla_kernel, nblk=l // GLA_BLOCK),
        grid=(b,),
        in_specs=[seq(GLA_KEY_W), seq(GLA_KEY_W), seq(GLA_VAL_W), seq(GLA_VAL_W),
                  seq(GLA_KEY_W), seq(GLA_KEY_W), st, st,
                  pl.BlockSpec((1, GLA_DV), lambda bi: (0, 0))],
        out_specs=[seq(GLA_VAL_W), st, st],
        out_shape=[jax.ShapeDtypeStruct((b, l, GLA_VAL_W), BF16),
                   jax.ShapeDtypeStruct((b, GLA_KEY_W, GLA_DV), F32),
                   jax.ShapeDtypeStruct((b, GLA_KEY_W, GLA_DV), F32)],
        scratch_shapes=[pltpu.VMEM((l, GLA_VAL_W), F32),
                        pltpu.VMEM((2, GLA_KEY_W, GLA_DV), F32)],
        compiler_params=pltpu.CompilerParams(
            dimension_semantics=("parallel",), vmem_limit_bytes=VMEM_LIMIT),
        name="gla",
    )(q, k, v, g, lgf, lgb, s0f, s0b, g_out)


def _attn_kernel(*refs, n_kv):
    q_ref, o_ref = refs[0], refs[-1]
    kv = [(refs[1 + 2 * i], refs[2 + 2 * i]) for i in range(n_kv)]
    for h in range(MLA_HEADS):
        sl = slice(h * HEAD_W, (h + 1) * HEAD_W)
        q = q_ref[0, :, sl]
        scores = [_dot_nt(q, k_ref[0, :, sl]) for k_ref, _ in kv]
        m = functools.reduce(jnp.maximum, [jnp.max(s, axis=-1, keepdims=True) for s in scores])
        o, den = 0.0, 0.0
        for s, (_, v_ref) in zip(scores, kv):
            p = jnp.exp(s - m)
            den = den + jnp.sum(p, axis=-1, keepdims=True)
            o = o + _dot(p.astype(BF16), v_ref[0, :, sl])
        o_ref[0, :, sl] = (o / den).astype(BF16)


def _attn(q, kv_pairs, tq):
    b, l, _ = q.shape
    kv_specs, kv_args = [], []
    for k, v in kv_pairs:
        spec = pl.BlockSpec((1, k.shape[1], MLA_W), lambda bi, i: (bi, 0, 0))
        kv_specs += [spec, spec]
        kv_args += [k, v]
    return pl.pallas_call(
        functools.partial(_attn_kernel, n_kv=len(kv_pairs)),
        grid=(b, l // tq),
        in_specs=[pl.BlockSpec((1, tq, MLA_W), lambda bi, i: (bi, i, 0))] + kv_specs,
        out_specs=pl.BlockSpec((1, tq, MLA_W), lambda bi, i: (bi, i, 0)),
        out_shape=jax.ShapeDtypeStruct((b, l, MLA_W), BF16),
        compiler_params=pltpu.CompilerParams(
            dimension_semantics=("parallel", "parallel"), vmem_limit_bytes=VMEM_LIMIT),
        name="attn",
    )(q, *kv_args)


def _sum01(x):
    return jnp.sum(jnp.sum(x, axis=1, keepdims=True), axis=0, keepdims=True)


def _route(logits_t, bias_col, cnt_col, cap):
    t = logits_t.shape[1]
    gsz = N_EXPERTS // N_GROUPS
    scores = jax.nn.sigmoid(logits_t)
    sel = (scores + bias_col).reshape(N_GROUPS, gsz, t)
    scores = scores.reshape(N_GROUPS, gsz, t)
    neg = jnp.float32(-jnp.inf)
    ie = lax.broadcasted_iota(I32, (N_GROUPS, gsz, t), 1)
    ig = lax.broadcasted_iota(I32, (N_GROUPS, gsz, t), 0)
    m1 = jnp.max(sel, axis=1, keepdims=True)
    first = jnp.min(jnp.where(sel == m1, ie, gsz), axis=1, keepdims=True)
    m2 = jnp.max(jnp.where(ie == first, neg, sel), axis=1, keepdims=True)
    grp = m1 + m2
    igk = lax.broadcasted_iota(I32, (N_GROUPS, 1, t), 0)
    g_sel = jnp.zeros((N_GROUPS, 1, t), jnp.bool_)
    cur = grp
    for _ in range(TOPK_GROUPS):
        m = jnp.max(cur, axis=0, keepdims=True)
        pick = igk == jnp.min(jnp.where(cur == m, igk, N_GROUPS), axis=0, keepdims=True)
        g_sel = g_sel | pick
        cur = jnp.where(pick, neg, cur)
    cur = jnp.where(g_sel, sel, neg)
    idx = ig * gsz + ie
    e_sel = jnp.zeros((N_GROUPS, gsz, t), jnp.bool_)
    picks = []
    for _ in range(TOP_K):
        m = jnp.max(jnp.max(cur, axis=1, keepdims=True), axis=0, keepdims=True)
        cand = jnp.where(cur == m, idx, N_EXPERTS)
        pick = idx == jnp.min(jnp.min(cand, axis=1, keepdims=True), axis=0, keepdims=True)
        picks.append(pick)
        e_sel = e_sel | pick
        cur = jnp.where(pick, neg, cur)
    w = jnp.where(e_sel, scores, 0.0)
    gate = w / _sum01(w) * ROUTED_SCALE
    sel_f = e_sel.astype(F32).reshape(N_EXPERTS, t)
    earlier = (lax.broadcasted_iota(I32, (t, t), 0) < lax.broadcasted_iota(I32, (t, t), 1))
    rank = _dot(sel_f.astype(BF16), earlier.astype(BF16))
    base = lax.broadcasted_iota(I32, (N_EXPERTS, 1), 0).astype(F32) * float(cap) + cnt_col
    slot = (base + rank).reshape(N_GROUPS, gsz, t)
    slot8 = jnp.concatenate([_sum01(jnp.where(pk, slot, 0.0)).reshape(1, t) for pk in picks],
                            axis=0).astype(I32)
    w8 = jnp.concatenate([_sum01(jnp.where(pk, gate, 0.0)).reshape(1, t) for pk in picks], axis=0)
    return slot8, w8, cnt_col + jnp.sum(sel_f, axis=1, keepdims=True)


def _outproj_kernel(og_ref, om_ref, x_ref, mod_ref, wtop_ref, wbot_ref, g2_ref, wr_ref, br_ref,
                    x1_ref, h2pa_ref, h2pb_ref, slot_ref, w8_ref, cnt_ref, cnt_scr, *, cap):
    @pl.when((pl.program_id(0) == 0) & (pl.program_id(1) == 0))
    def _():
        cnt_scr[...] = jnp.zeros_like(cnt_scr)

    mod = mod_ref[0]
    mix = _dot(og_ref[0], wtop_ref[...]) + _dot(om_ref[0], wbot_ref[...])
    x1 = x_ref[0] + mod[2:3] * mix
    x1_ref[0] = x1
    h2 = _rms(x1, D_MODEL) * g2_ref[...] * (1.0 + mod[4:5]) + mod[3:4]
    h2pa_ref[0], h2pb_ref[0] = _pack_rows(h2)
    logits_t = _dot_nt(wr_ref[...], h2, precision=HIGHEST)
    slot8, w8, cnt = _route(logits_t, br_ref[...], cnt_scr[:, 0:1], cap)
    slot_ref[...] = slot8
    t = w8.shape[1]
    w8_ref[0] = jnp.concatenate(
        [jnp.broadcast_to(w8[k:k + 1], (SC_LANES, t)) for k in range(TOP_K)], axis=0).T
    cnt_scr[...] = jnp.broadcast_to(cnt, cnt_scr.shape)
    cnt_ref[...] = cnt_scr[...]


def _outproj(og, om, x, mod, per_batch_mod, p, tm):
    b, l, d = x.shape
    nt = l // tm
    mod_map = (lambda bi, i: (bi, 0, 0)) if per_batch_mod else (lambda bi, i: (0, 0, 0))
    tok = lambda w: pl.BlockSpec((1, tm, w), lambda bi, i: (bi, i, 0))
    full = lambda a: pl.BlockSpec(a.shape, lambda bi, i: (0, 0))
    weights = [p['w_out_top'], p['w_out_bot'], p['g_norm2'], p['w_router_t'], p['b_router_col']]
    cnt_shape = (N_EXPERTS, LANES)
    return pl.pallas_call(
        functools.partial(_outproj_kernel, cap=b * l),
        grid=(b, nt),
        in_specs=[tok(GLA_VAL_W), tok(MLA_W), tok(d), pl.BlockSpec((1, 6, d), mod_map)]
                 + [full(w) for w in weights],
        out_specs=[tok(d), tok(PACK_W), tok(PACK_W),
                   pl.BlockSpec((TOP_K, tm), lambda bi, i: (0, bi * nt + i)),
                   tok(LANES), pl.BlockSpec(cnt_shape, lambda bi, i: (0, 0))],
        out_shape=[jax.ShapeDtypeStruct((b, l, d), F32), jax.ShapeDtypeStruct((b, l, PACK_W), U32),
                   jax.ShapeDtypeStruct((b, l, PACK_W), U32),
                   jax.ShapeDtypeStruct((TOP_K, b * l), I32),
                   jax.ShapeDtypeStruct((b, l, LANES), F32),
                   jax.ShapeDtypeStruct(cnt_shape, F32)],
        scratch_shapes=[pltpu.VMEM(cnt_shape, F32)],
        compiler_params=pltpu.CompilerParams(
            dimension_semantics=("arbitrary", "arbitrary"), vmem_limit_bytes=VMEM_LIMIT),
        name="outproj",
    )(og, om, x, mod, *weights)


def _sc_mesh():
    return plsc.VectorSubcoreMesh(core_axis_name="c", subcore_axis_name="s")


def _sc_scatter_rows(src, slot8, n_rows_out):
    t, d = src.shape
    nk = slot8.shape[0]

    @functools.partial(pl.kernel, out_type=jax.ShapeDtypeStruct((n_rows_out, d), src.dtype),
                       mesh=_sc_mesh(), scratch_types=[], name="dispatch")
    def run(src_hbm, slot_hbm, out_hbm):
        def body(x_vmem, i_vmem):
            for k in range(nk):
                pltpu.sync_copy(x_vmem, out_hbm.at[i_vmem.at[k]])

        pltpu.emit_pipeline(
            body, grid=(t // SC_WINDOW,),
            in_specs=[pl.BlockSpec((SC_WINDOW, d), lambda i: (i, 0)),
                      pl.BlockSpec((nk, SC_WINDOW), lambda i: (0, i))],
            out_specs=[], core_axis_name=("c", "s"),
            dimension_semantics=(pltpu.PARALLEL,))(src_hbm, slot_hbm)

    return run(src, slot8)


def _sc_collect_sum(table, slots_tk, w_rep):
    n_tok = w_rep.shape[0]
    d = table.shape[1]
    group = SC_WINDOW // TOP_K
    chunks = d // SC_LANES

    @functools.partial(pl.kernel, out_type=jax.ShapeDtypeStruct((n_tok, 2 * d), F32),
                       mesh=_sc_mesh(), scratch_types=[pltpu.VMEM((SC_WINDOW, d), table.dtype)],
                       compiler_params=pltpu.CompilerParams(needs_layout_passes=False),
                       name="collect")
    def run(tab_hbm, slot_hbm, w_hbm, out_hbm, rows_v):
        def body(i_vmem, w_vmem, o_vmem):
            pltpu.sync_copy(tab_hbm.at[i_vmem.at[0]], rows_v)

            @pl.loop(0, group)
            def _(j):
                wv = [w_vmem[j, pl.ds(k * SC_LANES, SC_LANES)] for k in range(TOP_K)]

                @pl.loop(0, chunks)
                def _(c):
                    off = c * SC_LANES
                    lo = jnp.zeros((SC_LANES,), F32)
                    hi = jnp.zeros((SC_LANES,), F32)
                    for k in range(TOP_K):
                        r = rows_v[j * TOP_K + k, pl.ds(off, SC_LANES)]
                        lo = lo + wv[k] * lax.bitcast_convert_type(r << 16, F32)
                        hi = hi + wv[k] * lax.bitcast_convert_type(
                            r & jnp.uint32(0xFFFF0000), F32)
                    o_vmem[j, pl.ds(off, SC_LANES)] = lo
                    o_vmem[j, pl.ds(d + off, SC_LANES)] = hi

        pltpu.emit_pipeline(
            body, grid=(n_tok // group,),
            in_specs=[pl.BlockSpec((1, SC_WINDOW), lambda i: (0, i)),
                      pl.BlockSpec((group, TOP_K * SC_LANES), lambda i: (i, 0))],
            out_specs=[pl.BlockSpec((group, 2 * d), lambda i: (i, 0))],
            core_axis_name=("c", "s"),
            dimension_semantics=(pltpu.PARALLEL,))(slot_hbm, w_hbm, out_hbm)

    return run(table, slots_tk, w_rep)


def _ffn_kernel(nt_ref, first_ref, te_ref, tt_ref, xsa_hbm, xsb_hbm, wg_ref, wu_ref, wd_ref,
                ysa_hbm, ysb_hbm, xbuf, ybuf, wg_b, wu_b, wd_b, sem_in, sem_out, *, cap):
    e = pl.program_id(0)
    n_exp = pl.num_programs(0)
    n = nt_ref[e]
    g0 = first_ref[e]
    total = first_ref[n_exp - 1] + nt_ref[n_exp - 1]
    xs_hbm = (xsa_hbm, xsb_hbm)
    ys_hbm = (ysa_hbm, ysb_hbm)
    n_in = FFN_AHEAD + 1

    def rows(ex, t):
        return pl.ds(pl.multiple_of(ex * cap + t * FFN_TILE, FFN_TILE), FFN_TILE)

    def in_copy(ex, t, slot, part):
        return pltpu.make_async_copy(xs_hbm[part].at[rows(ex, t)], xbuf.at[slot, part],
                                     sem_in.at[slot, part])

    def out_copy(ex, t, slot, part):
        return pltpu.make_async_copy(ybuf.at[slot, part], ys_hbm[part].at[rows(ex, t)],
                                     sem_out.at[slot, part])

    def start_in(g):
        for part in range(PACK_PARTS):
            in_copy(te_ref[g], tt_ref[g], g % n_in, part).start()

    def wait_out(slot):
        for part in range(PACK_PARTS):
            out_copy(e, 0, slot, part).wait()

    @pl.when(n > 0)
    def _():
        @pl.when(g0 == 0)
        def _():
            for ahead in range(FFN_AHEAD):
                @pl.when(ahead < total)
                def _():
                    start_in(ahead)

        wg_b[...] = wg_ref[0].astype(BF16)
        wu_b[...] = wu_ref[0].astype(BF16)
        wd_b[...] = wd_ref[0].astype(BF16)

        def tile(t, carry):
            g = g0 + t
            slot = g % 2
            slot_in = g % n_in
            for part in range(PACK_PARTS):
                in_copy(e, t, slot_in, part).wait()

            @pl.when(g + FFN_AHEAD < total)
            def _():
                start_in(g + FFN_AHEAD)

            @pl.when(g >= 2)
            def _():
                wait_out(slot)

            x = _unpack_rows([xbuf[slot_in, part] for part in range(PACK_PARTS)]).astype(BF16)
            a = _silu(_dot(x, wg_b[...])) * _dot(x, wu_b[...])
            y = _pack_rows(_dot(a.astype(BF16), wd_b[...]))
            for part in range(PACK_PARTS):
                ybuf[slot, part] = y[part]
                out_copy(e, t, slot, part).start()
            return carry

        lax.fori_loop(0, n, tile, 0)

        @pl.when(g0 + n == total)
        def _():
            @pl.when(total >= 2)
            def _():
                wait_out(total % 2)

            wait_out((total - 1) % 2)


def _ffn(xs_parts, counts, w_gate, w_up, w_down):
    n_exp, d = w_gate.shape[0], w_gate.shape[1]
    cap = xs_parts[0].shape[0] // n_exp
    ntiles = (counts + FFN_TILE - 1) // FFN_TILE
    cum = jnp.cumsum(ntiles)
    first = cum - ntiles
    max_tiles = (cap * TOP_K) // FFN_TILE + n_exp
    g = jnp.arange(max_tiles, dtype=I32)
    done = g[:, None] >= cum[None, :]
    tile_e = jnp.minimum(jnp.sum(done, axis=1), n_exp - 1).astype(I32)
    tile_t = (g - jnp.sum(jnp.where(done, ntiles[None, :], 0), axis=1)).astype(I32)
    wmap = lambda e, nt, first, te, tt: (e, 0, 0)
    hbm = pl.BlockSpec(memory_space=pl.ANY)
    return pl.pallas_call(
        functools.partial(_ffn_kernel, cap=cap),
        grid_spec=pltpu.PrefetchScalarGridSpec(
            num_scalar_prefetch=4, grid=(n_exp,),
            in_specs=[hbm] * PACK_PARTS
                     + [pl.BlockSpec((1, d, D_EXPERT), wmap), pl.BlockSpec((1, d, D_EXPERT), wmap),
                        pl.BlockSpec((1, D_EXPERT, d), wmap)],
            out_specs=[hbm] * PACK_PARTS,
            scratch_shapes=[pltpu.VMEM((FFN_AHEAD + 1, PACK_PARTS, FFN_TILE, PACK_W), U32),
                            pltpu.VMEM((2, PACK_PARTS, FFN_TILE, PACK_W), U32),
                            pltpu.VMEM((d, D_EXPERT), BF16), pltpu.VMEM((d, D_EXPERT), BF16),
                            pltpu.VMEM((D_EXPERT, d), BF16),
                            pltpu.SemaphoreType.DMA((FFN_AHEAD + 1, PACK_PARTS)),
                            pltpu.SemaphoreType.DMA((2, PACK_PARTS))]),
        out_shape=[jax.ShapeDtypeStruct(xs_parts[0].shape, U32)] * PACK_PARTS,
        compiler_params=pltpu.CompilerParams(
            dimension_semantics=("arbitrary",), vmem_limit_bytes=VMEM_LIMIT),
        name="ffn",
    )(ntiles.astype(I32), first.astype(I32), tile_e, tile_t, *xs_parts, w_gate, w_up, w_down)


def _combine_kernel(x1_ref, h2pa_ref, h2pb_ref, ra_ref, rb_ref, mod_ref,
                    wsg_ref, wsu_ref, wsd_ref, o_ref):
    h2 = _unpack_rows([h2pa_ref[0], h2pb_ref[0]]).astype(BF16)
    a = _silu(_dot(h2, wsg_ref[...])) * _dot(h2, wsu_ref[...])
    routed = jnp.concatenate([ra_ref[0], rb_ref[0]], axis=1)
    o_ref[0] = x1_ref[0] + mod_ref[0][5:6] * (_dot(a.astype(BF16), wsd_ref[...]) + routed)


def _combine(x1, h2p_parts, routed_parts, mod, per_batch_mod, p, tm):
    b, l, d = x1.shape
    mod_map = (lambda bi, i: (bi, 0, 0)) if per_batch_mod else (lambda bi, i: (0, 0, 0))
    tok = lambda w: pl.BlockSpec((1, tm, w), lambda bi, i: (bi, i, 0))
    full = lambda a: pl.BlockSpec(a.shape, lambda bi, i: (0, 0))
    weights = [p['w_sh_gate'], p['w_sh_up'], p['w_sh_down']]
    routed_parts = [r.reshape(b, l, d // PACK_PARTS) for r in routed_parts]
    return pl.pallas_call(
        _combine_kernel,
        grid=(b, l // tm),
        in_specs=[tok(d)] + [tok(PACK_W)] * PACK_PARTS + [tok(d // PACK_PARTS)] * PACK_PARTS
                 + [pl.BlockSpec((1, 6, d), mod_map)] + [full(w) for w in weights],
        out_specs=tok(d),
        out_shape=jax.ShapeDtypeStruct((b, l, d), F32),
        compiler_params=pltpu.CompilerParams(
            dimension_semantics=("parallel", "parallel"), vmem_limit_bytes=VMEM_LIMIT),
        name="combine",
    )(x1, *h2p_parts, *routed_parts, mod, *weights)


def _pad_heads(w, parts):
    k = w.shape[0]
    per = w.shape[1] // MLA_HEADS
    w = w.reshape(k, MLA_HEADS, per)[:, :, parts[0]:parts[1]]
    w = jnp.pad(w, ((0, 0), (0, 0), (0, HEAD_W - (parts[1] - parts[0]))))
    return w.reshape(k, MLA_HEADS * HEAD_W)


def _prep_params(l, g_norm1, g_norm2, w_in, w_gk_fwd, b_gk_fwd, w_gk_bwd, b_gk_bwd, g_gla_out,
                 g_q_lora, w_uq, g_kv_lora, w_ukv, g_qk_q, g_qk_k, w_out, w_router, b_router,
                 w_sh_gate, w_sh_up, w_sh_down):
    w = w_in[l]
    d = w.shape[0]
    o_lrf = COL_CQ
    o_lrb = o_lrf + GLA_GATE_RANK
    o_cq = o_lrb + GLA_GATE_RANK
    o_ckv = o_cq + Q_LORA
    o_kr = o_ckv + KV_LORA
    w_in_p = jnp.concatenate([
        w[:, 0:COL_CQ], w[:, o_cq:o_ckv], w[:, o_ckv:o_kr], jnp.zeros((d, KR_LANE0), w.dtype),
        w[:, o_kr:o_kr + ROPE_DIM], w[:, o_lrf:o_lrb], w[:, o_lrb:o_cq]], axis=1).astype(BF16)
    z = jnp.zeros((GLA_GATE_RANK, GLA_KEY_W), F32)
    w_gk_big = jnp.concatenate([
        jnp.zeros((LANES - 2 * GLA_GATE_RANK, 2 * GLA_KEY_W), F32),
        jnp.concatenate([w_gk_fwd[l], z], axis=1),
        jnp.concatenate([z, w_gk_bwd[l]], axis=1)], axis=0).astype(BF16)
    pad_gain = lambda g: jnp.pad(g, (0, HEAD_W - QK_DIM)).reshape(1, HEAD_W)
    w_ukv_h = w_ukv[l]
    return {
        'g_norm1': g_norm1[l].reshape(1, d), 'g_norm2': g_norm2[l].reshape(1, d),
        'w_in_p': w_in_p, 'w_gk_big': w_gk_big,
        'b_gk': jnp.concatenate([b_gk_fwd[l], b_gk_bwd[l]]).reshape(1, 2 * GLA_KEY_W),
        'g_gla_out': g_gla_out[l].reshape(1, GLA_DV),
        'g_q_lora': g_q_lora[l].reshape(1, Q_LORA),
        'w_uq_p': _pad_heads(w_uq[l], (0, QK_DIM)).astype(BF16),
        'g_kv_lora': g_kv_lora[l].reshape(1, KV_LORA),
        'w_ukv_p': jnp.concatenate([_pad_heads(w_ukv_h, (0, NOPE_DIM)),
                                    _pad_heads(w_ukv_h, (NOPE_DIM, NOPE_DIM + V_DIM))],
                                   axis=1).astype(BF16),
        'gq': pad_gain(g_qk_q[l]), 'gk': pad_gain(g_qk_k[l]),
        'w_out_top': w_out[l][:GLA_VAL_W].astype(BF16),
        'w_out_bot': w_out[l][GLA_VAL_W:].astype(BF16),
        'w_router_t': w_router[l].T, 'b_router_col': b_router[l].reshape(N_EXPERTS, 1),
        'w_sh_gate': w_sh_gate[l].astype(BF16), 'w_sh_up': w_sh_up[l].astype(BF16),
        'w_sh_down': w_sh_down[l].astype(BF16),
    }


def _rope_tables(length):
    pos = np.arange(length)
    r = (pos // GRID_W).astype(np.float32)
    c = (pos % GRID_W).astype(np.float32)
    half = ROPE_DIM // 4
    inv_freq = np.float32(ROPE_THETA) ** (-np.arange(half, dtype=np.float32) / np.float32(half))
    ang_r = r[:, None] * inv_freq[None, :]
    ang_c = c[:, None] * inv_freq[None, :]
    zeros = lambda w: np.zeros((length, w), np.float32)
    ones = lambda w: np.ones((length, w), np.float32)
    tail_w = HEAD_W - ROPE_LANE0 - ROPE_DIM
    cos = np.concatenate([ones(ROPE_LANE0), np.cos(ang_r), np.cos(ang_r), np.cos(ang_c),
                          np.cos(ang_c), ones(tail_w)], axis=1)
    s1 = np.concatenate([zeros(ROPE_LANE0), -np.sin(ang_r), zeros(half), -np.sin(ang_c),
                         zeros(half), zeros(tail_w)], axis=1)
    s2 = np.concatenate([zeros(ROPE_LANE0), zeros(half), np.sin(ang_r), zeros(half),
                         np.sin(ang_c), zeros(tail_w)], axis=1)
    return tuple(jnp.asarray(t, F32) for t in (cos, s1, s2))


def _mix_route_dispatch(x_tok, seq_shape, mod, per_batch_mod, p, s0f, s0b, rope_tabs, ctx_kv,
                        tq):
    bt, lt, d = x_tok.shape
    b, l = seq_shape
    n_tok = bt * lt
    (q, k, v, g, lgf, lgb, qm, km, vm, ckvn, kr) = _inproj(x_tok, mod, per_batch_mod, p,
                                                            rope_tabs, IN_TILE)
    seq = lambda a: a.reshape(b, l, a.shape[-1])
    og, sf, sb = _gla(seq(q), seq(k), seq(v), seq(g), seq(lgf), seq(lgb), s0f, s0b,
                      p['g_gla_out'])
    kv_pairs = [(seq(km), seq(vm))] + ([ctx_kv] if ctx_kv is not None else [])
    om = _attn(seq(qm), kv_pairs, tq)
    tokv = lambda a: a.reshape(bt, lt, a.shape[-1])
    x1, h2a, h2b, slot8, w8, cnt = _outproj(tokv(og), tokv(om), x_tok, mod, per_batch_mod, p,
                                            OUT_TILE)
    x_sorted = [_sc_scatter_rows(h.reshape(n_tok, PACK_W), slot8, N_EXPERTS * n_tok)
                for h in (h2a, h2b)]
    routed = dict(x1=x1, h2=(h2a, h2b), slot8=slot8, w8=w8, cnt=cnt, x_sorted=x_sorted)
    return routed, sf, sb, ckvn, kr


def _experts_collect(r, w_gate, w_up, w_down):
    n_tok = r['slot8'].shape[1]
    y_sorted = _ffn(r['x_sorted'], r['cnt'][:, 0].astype(I32), w_gate, w_up, w_down)
    slots_tk = r['slot8'].T.reshape(1, TOP_K * n_tok)
    w_rep = r['w8'].reshape(n_tok, LANES)
    return [_sc_collect_sum(y, slots_tk, w_rep) for y in y_sorted]


def kernel(x_prompt, x_sample, c, state_gla_fwd, state_gla_bwd, cache_mla_ckv, cache_mla_krope,
           c_ctx, w_ada, b_ada, g_norm1, g_norm2, w_in, w_gk_fwd, b_gk_fwd, w_gk_bwd, b_gk_bwd,
           g_gla_out, g_q_lora, w_uq, g_kv_lora, w_ukv, g_qk_q, g_qk_k, w_out,
           w_router, b_router, w_exp_gate, w_exp_up, w_exp_down, w_sh_gate, w_sh_up, w_sh_down):
    bp, lp, d = x_prompt.shape
    bs, ls, _ = x_sample.shape
    depth = w_ada.shape[0]
    xp = x_prompt.reshape(1, bp * lp, d)
    xs = x_sample
    new_f, new_b, new_ckv, new_kr = [], [], [], []
    lat_tabs = _rope_tables(ls)
    cvecs = jnp.concatenate([c_ctx[None], c, jnp.zeros((8 - 1 - bs, d), F32)], axis=0)
    for l in range(depth):
        p = _prep_params(l, g_norm1, g_norm2, w_in, w_gk_fwd, b_gk_fwd, w_gk_bwd, b_gk_bwd,
                         g_gla_out, g_q_lora, w_uq, g_kv_lora, w_ukv, g_qk_q, g_qk_k, w_out,
                         w_router, b_router, w_sh_gate, w_sh_up, w_sh_down)
        mod = _ada(cvecs, w_ada[l], b_ada[l]).reshape(8, 6, d)
        mod_p, mod_s = mod[0:1], mod[1:1 + bs]
        zeros = jnp.zeros((bp, GLA_KEY_W, GLA_DV), F32)
        r_ctx, sf, sb, ckvn, kr = _mix_route_dispatch(
            xp, (bp, lp), mod_p, False, p, zeros, zeros, (), None, lp)
        new_f.append(sf.reshape(bp, GLA_HEADS, GLA_DK, GLA_DV))
        new_b.append(sb.reshape(bp, GLA_HEADS, GLA_DK, GLA_DV))
        new_ckv.append(ckvn.reshape(bp, lp, KV_LORA))
        new_kr.append(kr.reshape(bp, lp, LANES)[:, :, KR_LANE0:KR_LANE0 + ROPE_DIM])
        kr_cache = jnp.pad(cache_mla_krope[:, l],
                           ((0, 0), (0, 0), (ROPE_LANE0, LANES - ROPE_LANE0 - ROPE_DIM)))
        ctx_kv = _cache_kv(cache_mla_ckv[:, l], kr_cache, p)
        s0f = state_gla_fwd[:, l].reshape(bs, GLA_KEY_W, GLA_DV)
        s0b = state_gla_bwd[:, l].reshape(bs, GLA_KEY_W, GLA_DV)
        r_lat, _, _, _, _ = _mix_route_dispatch(
            xs, (bs, ls), mod_s, True, p, s0f, s0b, lat_tabs, ctx_kv, 256)
        experts = (w_exp_gate[l], w_exp_up[l], w_exp_down[l])
        routed_ctx = _experts_collect(r_ctx, *experts)
        routed_lat = _experts_collect(r_lat, *experts)
        xp = _combine(r_ctx['x1'], r_ctx['h2'], routed_ctx, mod_p, False, p, OUT_TILE)
        xs = _combine(r_lat['x1'], r_lat['h2'], routed_lat, mod_s, True, p, OUT_TILE)
    return (xp.reshape(bp, lp, d), xs, jnp.stack(new_f, axis=1), jnp.stack(new_b, axis=1),
            jnp.stack(new_ckv, axis=1), jnp.stack(new_kr, axis=1))
```

```python
import functools

import jax
import jax.numpy as jnp
import numpy as np
from jax import lax
from jax.experimental import pallas as pl
from jax.experimental.pallas import tpu as pltpu
from jax.experimental.pallas import tpu_sc as plsc

F32 = jnp.float32
BF16 = jnp.bfloat16
I32 = jnp.int32
U32 = jnp.uint32

D_MODEL = 1024
EPS = 1e-6
GRID_W = 64
GLA_HEADS = 4
GLA_DK = 64
GLA_DV = 128
GLA_GATE_RANK = 16
GLA_GATE_NORM = 16.0
GLA_KEY_W = GLA_HEADS * GLA_DK
GLA_VAL_W = GLA_HEADS * GLA_DV
MLA_HEADS = 4
Q_LORA = 256
KV_LORA = 128
NOPE_DIM = 64
ROPE_DIM = 32
V_DIM = 128
QK_DIM = NOPE_DIM + ROPE_DIM
ROPE_THETA = 10000.0
N_EXPERTS = 64
TOP_K = 8
N_GROUPS = 8
TOPK_GROUPS = 4
D_EXPERT = 256
ROUTED_SCALE = 2.5

LANES = 128
HEAD_W = LANES
MLA_W = MLA_HEADS * HEAD_W
ROPE_LANE0 = NOPE_DIM
COL_V = 2 * GLA_KEY_W
COL_G = COL_V + GLA_VAL_W
COL_CQ = COL_G + GLA_VAL_W
COL_CKV = COL_CQ + Q_LORA
TAIL0 = COL_CKV + KV_LORA
IN_W = TAIL0 + LANES
KR_LANE0 = LANES - ROPE_DIM - 2 * GLA_GATE_RANK
GLA_BLOCK = 256
GLA_SUB = 64
IN_TILE = 256
OUT_TILE = 512
FFN_TILE = 512
FFN_AHEAD = 5
PACK_PARTS = 2
PACK_W = D_MODEL // (2 * PACK_PARTS)
SC_WINDOW = 128
SC_LANES = 16
VMEM_LIMIT = 56 * 1024 * 1024

HIGHEST = lax.Precision.HIGHEST


def _dot(a, b, precision=None):
    return jnp.dot(a, b, preferred_element_type=F32, precision=precision)


def _dot_nt(a, b, precision=None):
    return lax.dot_general(a, b, (((1,), (1,)), ((), ())), preferred_element_type=F32,
                           precision=precision)


def _rms(x, width):
    ss = jnp.sum(x * x, axis=-1, keepdims=True) * (1.0 / width)
    return x * lax.rsqrt(ss + EPS)


def _silu(x):
    return x * jax.nn.sigmoid(x)


def _log_sigmoid(x):
    return jnp.minimum(x, 0.0) - jnp.log1p(jnp.exp(-jnp.abs(x)))


def _pack_rows(x):
    parts = []
    for i in range(PACK_PARTS):
        c0 = i * 2 * PACK_W
        lo = lax.bitcast_convert_type(x[:, c0:c0 + PACK_W].astype(BF16).astype(F32), U32)
        hi = lax.bitcast_convert_type(
            x[:, c0 + PACK_W:c0 + 2 * PACK_W].astype(BF16).astype(F32), U32)
        parts.append(hi | (lo >> 16))
    return parts


def _unpack_rows(parts):
    cols = []
    for w in parts:
        cols.append(lax.bitcast_convert_type(w << 16, F32))
        cols.append(lax.bitcast_convert_type(w & jnp.uint32(0xFFFF0000), F32))
    return jnp.concatenate(cols, axis=1)


def _ada_kernel(c_ref, w_ref, b_ref, o_ref):
    o_ref[...] = _dot(_silu(c_ref[...]), w_ref[...], precision=HIGHEST) + b_ref[...]


def _ada(cvecs, w_ada, b_ada):
    n = w_ada.shape[1]
    tn = 768
    return pl.pallas_call(
        _ada_kernel,
        grid=(n // tn,),
        in_specs=[pl.BlockSpec((8, D_MODEL), lambda j: (0, 0)),
                  pl.BlockSpec((D_MODEL, tn), lambda j: (0, j)),
                  pl.BlockSpec((1, tn), lambda j: (0, j))],
        out_specs=pl.BlockSpec((8, tn), lambda j: (0, j)),
        out_shape=jax.ShapeDtypeStruct((8, n), F32),
        name="ada",
    )(cvecs, w_ada, b_ada.reshape(1, n))


def _rope(x, c, s1, s2):
    return x * c + pltpu.roll(x, LANES - 8, 1) * s1 + pltpu.roll(x, 8, 1) * s2


def _mla_kv(ckv_n, kr, w_ukv_ref, gk, rope_tabs, k_ref, v_ref):
    kv = _dot(ckv_n.astype(BF16), w_ukv_ref[...])
    for h in range(MLA_HEADS):
        k_h = kv[:, h * HEAD_W:(h + 1) * HEAD_W] + kr
        k_h = _rms(k_h, QK_DIM) * gk
        if rope_tabs is not None:
            k_h = _rope(k_h, *rope_tabs)
        k_ref[0, :, h * HEAD_W:(h + 1) * HEAD_W] = k_h.astype(BF16)
    v_ref[0] = kv[:, MLA_W:].astype(BF16)


def _lane_mask(lo, hi, rows):
    lane = lax.broadcasted_iota(I32, (rows, LANES), 1)
    return (lane >= lo) & (lane < hi)


def _inproj_kernel(x_ref, mod_ref, g1_ref, win_ref, wgk_ref, bgk_ref, gql_ref, wuq_ref,
                   gkv_ref, wukv_ref, gq_ref, gk_ref, *refs, positions):
    tab_refs, outs = (refs[:3], refs[3:]) if positions else ((), refs)
    (q_ref, k_ref, v_ref, g_ref, lgf_ref, lgb_ref, qm_ref, km_ref, vm_ref, ckvn_ref,
     kr_ref) = outs
    x = x_ref[0]
    mod = mod_ref[0]
    h = _rms(x, D_MODEL) * g1_ref[...] * (1.0 + mod[1:2]) + mod[0:1]
    y = _dot(h.astype(BF16), win_ref[...])
    q_ref[0] = (y[:, 0:GLA_KEY_W] * GLA_DK ** -0.5).astype(BF16)
    k_ref[0] = y[:, GLA_KEY_W:COL_V].astype(BF16)
    v_ref[0] = y[:, COL_V:COL_G].astype(BF16)
    g_ref[0] = y[:, COL_G:COL_CQ].astype(BF16)
    tail = y[:, TAIL0:IN_W]
    pre = _dot(tail.astype(BF16), wgk_ref[...]) + bgk_ref[...]
    logg = _log_sigmoid(pre) * (1.0 / GLA_GATE_NORM)
    lgf_ref[0] = logg[:, 0:GLA_KEY_W]
    lgb_ref[0] = logg[:, GLA_KEY_W:]
    rows = x.shape[0]
    tabs = tuple(r[...] for r in tab_refs) if positions else None
    cq = _rms(y[:, COL_CQ:COL_CKV], Q_LORA) * gql_ref[...]
    qm = _dot(cq.astype(BF16), wuq_ref[...])
    gq = gq_ref[...]
    for hh in range(MLA_HEADS):
        q_h = _rms(qm[:, hh * HEAD_W:(hh + 1) * HEAD_W], QK_DIM) * gq
        if positions:
            q_h = _rope(q_h, *tabs)
        qm_ref[0, :, hh * HEAD_W:(hh + 1) * HEAD_W] = (q_h * QK_DIM ** -0.5).astype(BF16)
    ckv_n = _rms(y[:, COL_CKV:TAIL0], KV_LORA) * gkv_ref[...]
    ckvn_ref[0] = ckv_n
    kr_ref[0] = tail
    kr = jnp.where(_lane_mask(KR_LANE0, KR_LANE0 + ROPE_DIM, rows), tail, 0.0)
    _mla_kv(ckv_n, kr, wukv_ref, gk_ref[...], tabs, km_ref, vm_ref)


def _inproj(x, mod, per_batch_mod, p, rope_tabs, tm):
    b, l, d = x.shape
    nt = l // tm
    mod_map = (lambda bi, i: (bi, 0, 0)) if per_batch_mod else (lambda bi, i: (0, 0, 0))
    tab_map = lambda bi, i: (i, 0)
    const = lambda bi, i: (0, 0)
    tok = lambda w: pl.BlockSpec((1, tm, w), lambda bi, i: (bi, i, 0))
    full = lambda a: pl.BlockSpec(a.shape, const)
    weights = [p['g_norm1'], p['w_in_p'], p['w_gk_big'], p['b_gk'], p['g_q_lora'], p['w_uq_p'],
               p['g_kv_lora'], p['w_ukv_p'], p['gq'], p['gk']]
    outs = [(GLA_KEY_W, BF16), (GLA_KEY_W, BF16), (GLA_VAL_W, BF16), (GLA_VAL_W, BF16),
            (GLA_KEY_W, F32), (GLA_KEY_W, F32), (MLA_W, BF16), (MLA_W, BF16), (MLA_W, BF16),
            (KV_LORA, F32), (LANES, F32)]
    return pl.pallas_call(
        functools.partial(_inproj_kernel, positions=bool(rope_tabs)),
        grid=(b, nt),
        in_specs=[tok(d), pl.BlockSpec((1, 6, d), mod_map)] + [full(w) for w in weights]
                 + [pl.BlockSpec((tm, LANES), tab_map)] * len(rope_tabs),
        out_specs=[tok(w) for w, _ in outs],
        out_shape=[jax.ShapeDtypeStruct((b, l, w), dt) for w, dt in outs],
        compiler_params=pltpu.CompilerParams(
            dimension_semantics=("parallel", "parallel"), vmem_limit_bytes=VMEM_LIMIT),
        name="inproj",
    )(x, mod, *weights, *rope_tabs)


def _cache_kv_kernel(ckv_ref, kr_ref, wukv_ref, gk_ref, k_ref, v_ref):
    _mla_kv(ckv_ref[0], kr_ref[0], wukv_ref, gk_ref[...], None, k_ref, v_ref)


def _cache_kv(ckv, kr128, p):
    b, l, _ = ckv.shape
    tok = lambda w: pl.BlockSpec((1, l, w), lambda bi: (bi, 0, 0))
    full = lambda a: pl.BlockSpec(a.shape, lambda bi: (0, 0))
    return pl.pallas_call(
        _cache_kv_kernel,
        grid=(b,),
        in_specs=[tok(KV_LORA), tok(LANES), full(p['w_ukv_p']), full(p['gk'])],
        out_specs=[tok(MLA_W), tok(MLA_W)],
        out_shape=[jax.ShapeDtypeStruct((b, l, MLA_W), BF16)] * 2,
        name="cache_kv",
    )(ckv, kr128, p['w_ukv_p'], p['gk'])


def _split3(x):
    hi = x.astype(BF16)
    r1 = x - hi.astype(F32)
    mid = r1.astype(BF16)
    lo = (r1 - mid.astype(F32)).astype(BF16)
    return hi, mid, lo


def _gla_block(q, k, v, lg, s, fwd):
    n = GLA_BLOCK
    nc = n // GLA_SUB
    row = lax.broadcasted_iota(I32, (n, n), 0)
    col = lax.broadcasted_iota(I32, (n, n), 1)
    same = (row // GLA_SUB) == (col // GLA_SUB)
    causal = same & ((col <= row) if fwd else (col >= row))
    tri = causal.astype(BF16)
    hi, mid, lo = _split3(lg)
    cum = _dot(tri, hi) + _dot(tri, mid) + _dot(tri, lo)
    tot_rows, mid_rows = [], []
    for c in range(nc):
        r_tot = c * GLA_SUB + (GLA_SUB - 1 if fwd else 0)
        r_mid = c * GLA_SUB + GLA_SUB // 2
        tot_rows.append(jnp.broadcast_to(cum[r_tot:r_tot + 1], (GLA_SUB, GLA_KEY_W)))
        mid_rows.append(jnp.broadcast_to(cum[r_mid:r_mid + 1], (GLA_SUB, GLA_KEY_W)))
    tot_rows = jnp.concatenate(tot_rows, axis=0)
    mid_rows = jnp.concatenate(mid_rows, axis=0)
    rel = cum - mid_rows
    qi = q * jnp.exp(rel)
    ki = (k * jnp.exp(-rel)).astype(BF16)
    q_in = (q * jnp.exp(cum)).astype(BF16)
    k_up = k * jnp.exp(tot_rows - cum)
    dec = jnp.exp(tot_rows)
    k_up_t = k_up.T.astype(BF16)
    dec_t = dec.T
    lane_head = lax.broadcasted_iota(I32, (n, GLA_KEY_W), 1) // GLA_DK
    srow_head = lax.broadcasted_iota(I32, (GLA_KEY_W, GLA_DV), 0) // GLA_DK
    vrow_chunk = lax.broadcasted_iota(I32, (n, GLA_VAL_W), 0) // GLA_SUB
    o_heads = []
    for h in range(GLA_HEADS):
        a = _dot_nt(jnp.where(lane_head == h, qi, 0.0).astype(BF16), ki)
        a = jnp.where(causal, a, 0.0).astype(BF16)
        o_heads.append(_dot(a, v[:, h * GLA_DV:(h + 1) * GLA_DV]))
    o = jnp.concatenate(o_heads, axis=1)
    o_inter = [None] * nc
    for c in (range(nc) if fwd else range(nc - 1, -1, -1)):
        s_bd = jnp.concatenate(
            [jnp.where(srow_head == h, s, 0.0).astype(BF16) for h in range(GLA_HEADS)], axis=1)
        o_inter[c] = _dot(q_in[c * GLA_SUB:(c + 1) * GLA_SUB], s_bd)
        v_c = jnp.where(vrow_chunk == c, v, jnp.zeros_like(v))
        u = jnp.concatenate(
            [_dot(k_up_t[h * GLA_DK:(h + 1) * GLA_DK], v_c[:, h * GLA_DV:(h + 1) * GLA_DV])
             for h in range(GLA_HEADS)], axis=0)
        s = dec_t[:, c * GLA_SUB:c * GLA_SUB + 1] * s + u
    return o + jnp.concatenate(o_inter, axis=0), s


def _gla_kernel(q_ref, k_ref, v_ref, g_ref, lgf_ref, lgb_ref, s0f_ref, s0b_ref, gout_ref,
                o_ref, sf_ref, sb_ref, acc_ref, st_ref, *, nblk):
    st_ref[0] = s0f_ref[0]
    st_ref[1] = s0b_ref[0]

    def load(blk):
        start = blk * GLA_BLOCK
        r = pl.ds(start if isinstance(blk, int) else pl.multiple_of(start, GLA_BLOCK), GLA_BLOCK)
        return r, q_ref[0, r, :].astype(F32), k_ref[0, r, :].astype(F32), v_ref[0, r, :]

    def pair_step(i, first_touch):
        rf, qf, kf, vf = load(i)
        rb, qb, kb, vb = load(nblk - 1 - i)
        of, s_f = _gla_block(qf, kf, vf, lgf_ref[0, rf, :], st_ref[0], True)
        ob, s_b = _gla_block(qb, kb, vb, lgb_ref[0, rb, :], st_ref[1], False)
        if first_touch:
            acc_ref[rf, :] = of
            acc_ref[rb, :] = ob
        else:
            acc_ref[rf, :] += of
            acc_ref[rb, :] += ob
        st_ref[0] = s_f
        st_ref[1] = s_b

    if nblk == 1:
        r, q, k, v = load(0)
        of, s_f = _gla_block(q, k, v, lgf_ref[0, r, :], st_ref[0], True)
        ob, s_b = _gla_block(q, k, v, lgb_ref[0, r, :], st_ref[1], False)
        acc_ref[r, :] = of + ob
        st_ref[0] = s_f
        st_ref[1] = s_b
    else:
        half = nblk // 2
        lax.fori_loop(0, half, lambda i, c: (pair_step(i, True), c)[1], 0)
        lax.fori_loop(half, nblk, lambda i, c: (pair_step(i, False), c)[1], 0)
    sf_ref[0] = st_ref[0]
    sb_ref[0] = st_ref[1]

    def fin_step(blk, carry):
        r = pl.ds(pl.multiple_of(blk * GLA_BLOCK, GLA_BLOCK), GLA_BLOCK)
        o = acc_ref[r, :]
        gate = _silu(g_ref[0, r, :].astype(F32))
        for h in range(GLA_HEADS):
            sl = slice(h * GLA_DV, (h + 1) * GLA_DV)
            o_ref[0, r, sl] = (_rms(o[:, sl], GLA_DV) * gout_ref[...] * gate[:, sl]).astype(BF16)
        return carry

    lax.fori_loop(0, nblk, fin_step, 0)


def _gla(q, k, v, g, lgf, lgb, s0f, s0b, g_out):
    b, l, _ = q.shape
    assert l == GLA_BLOCK or l % (2 * GLA_BLOCK) == 0
    seq = lambda w: pl.BlockSpec((1, l, w), lambda bi: (bi, 0, 0))
    st = pl.BlockSpec((1, GLA_KEY_W, GLA_DV), lambda bi: (bi, 0, 0))
    return pl.pallas_call(
        functools.partial(_gla_kernel, nblk=l // GLA_BLOCK),
        grid=(b,),
        in_specs=[seq(GLA_KEY_W), seq(GLA_KEY_W), seq(GLA_VAL_W), seq(GLA_VAL_W),
                  seq(GLA_KEY_W), seq(GLA_KEY_W), st, st,
                  pl.BlockSpec((1, GLA_DV), lambda bi: (0, 0))],
        out_specs=[seq(GLA_VAL_W), st, st],
        out_shape=[jax.ShapeDtypeStruct((b, l, GLA_VAL_W), BF16),
                   jax.ShapeDtypeStruct((b, GLA_KEY_W, GLA_DV), F32),
                   jax.ShapeDtypeStruct((b, GLA_KEY_W, GLA_DV), F32)],
        scratch_shapes=[pltpu.VMEM((l, GLA_VAL_W), F32),
                        pltpu.VMEM((2, GLA_KEY_W, GLA_DV), F32)],
        compiler_params=pltpu.CompilerParams(
            dimension_semantics=("parallel",), vmem_limit_bytes=VMEM_LIMIT),
        name="gla",
    )(q, k, v, g, lgf, lgb, s0f, s0b, g_out)


def _attn_kernel(*refs, n_kv):
    q_ref, o_ref = refs[0], refs[-1]
    kv = [(refs[1 + 2 * i], refs[2 + 2 * i]) for i in range(n_kv)]
    for h in range(MLA_HEADS):
        sl = slice(h * HEAD_W, (h + 1) * HEAD_W)
        q = q_ref[0, :, sl]
        scores = [_dot_nt(q, k_ref[0, :, sl]) for k_ref, _ in kv]
        m = functools.reduce(jnp.maximum, [jnp.max(s, axis=-1, keepdims=True) for s in scores])
        o, den = 0.0, 0.0
        for s, (_, v_ref) in zip(scores, kv):
            p = jnp.exp(s - m)
            den = den + jnp.sum(p, axis=-1, keepdims=True)
            o = o + _dot(p.astype(BF16), v_ref[0, :, sl])
        o_ref[0, :, sl] = (o / den).astype(BF16)


def _attn(q, kv_pairs, tq):
    b, l, _ = q.shape
    kv_specs, kv_args = [], []
    for k, v in kv_pairs:
        spec = pl.BlockSpec((1, k.shape[1], MLA_W), lambda bi, i: (bi, 0, 0))
        kv_specs += [spec, spec]
        kv_args += [k, v]
    return pl.pallas_call(
        functools.partial(_attn_kernel, n_kv=len(kv_pairs)),
        grid=(b, l // tq),
        in_specs=[pl.BlockSpec((1, tq, MLA_W), lambda bi, i: (bi, i, 0))] + kv_specs,
        out_specs=pl.BlockSpec((1, tq, MLA_W), lambda bi, i: (bi, i, 0)),
        out_shape=jax.ShapeDtypeStruct((b, l, MLA_W), BF16),
        compiler_params=pltpu.CompilerParams(
            dimension_semantics=("parallel", "parallel"), vmem_limit_bytes=VMEM_LIMIT),
        name="attn",
    )(q, *kv_args)


def _sum01(x):
    return jnp.sum(jnp.sum(x, axis=1, keepdims=True), axis=0, keepdims=True)


def _route(logits_t, bias_col, cnt_col, cap):
    t = logits_t.shape[1]
    gsz = N_EXPERTS // N_GROUPS
    scores = jax.nn.sigmoid(logits_t)
    sel = (scores + bias_col).reshape(N_GROUPS, gsz, t)
    scores = scores.reshape(N_GROUPS, gsz, t)
    neg = jnp.float32(-jnp.inf)
    ie = lax.broadcasted_iota(I32, (N_GROUPS, gsz, t), 1)
    ig = lax.broadcasted_iota(I32, (N_GROUPS, gsz, t), 0)
    m1 = jnp.max(sel, axis=1, keepdims=True)
    first = jnp.min(jnp.where(sel == m1, ie, gsz), axis=1, keepdims=True)
    m2 = jnp.max(jnp.where(ie == first, neg, sel), axis=1, keepdims=True)
    grp = m1 + m2
    igk = lax.broadcasted_iota(I32, (N_GROUPS, 1, t), 0)
    g_sel = jnp.zeros((N_GROUPS, 1, t), jnp.bool_)
    cur = grp
    for _ in range(TOPK_GROUPS):
        m = jnp.max(cur, axis=0, keepdims=True)
        pick = igk == jnp.min(jnp.where(cur == m, igk, N_GROUPS), axis=0, keepdims=True)
        g_sel = g_sel | pick
        cur = jnp.where(pick, neg, cur)
    cur = jnp.where(g_sel, sel, neg)
    idx = ig * gsz + ie
    e_sel = jnp.zeros((N_GROUPS, gsz, t), jnp.bool_)
    picks = []
    for _ in range(TOP_K):
        m = jnp.max(jnp.max(cur, axis=1, keepdims=True), axis=0, keepdims=True)
        cand = jnp.where(cur == m, idx, N_EXPERTS)
        pick = idx == jnp.min(jnp.min(cand, axis=1, keepdims=True), axis=0, keepdims=True)
        picks.append(pick)
        e_sel = e_sel | pick
        cur = jnp.where(pick, neg, cur)
    w = jnp.where(e_sel, scores, 0.0)
    gate = w / _sum01(w) * ROUTED_SCALE
    sel_f = e_sel.astype(F32).reshape(N_EXPERTS, t)
    earlier = (lax.broadcasted_iota(I32, (t, t), 0) < lax.broadcasted_iota(I32, (t, t), 1))
    rank = _dot(sel_f.astype(BF16), earlier.astype(BF16))
    base = lax.broadcasted_iota(I32, (N_EXPERTS, 1), 0).astype(F32) * float(cap) + cnt_col
    slot = (base + rank).reshape(N_GROUPS, gsz, t)
    slot8 = jnp.concatenate([_sum01(jnp.where(pk, slot, 0.0)).reshape(1, t) for pk in picks],
                            axis=0).astype(I32)
    w8 = jnp.concatenate([_sum01(jnp.where(pk, gate, 0.0)).reshape(1, t) for pk in picks], axis=0)
    return slot8, w8, cnt_col + jnp.sum(sel_f, axis=1, keepdims=True)


def _outproj_kernel(og_ref, om_ref, x_ref, mod_ref, wtop_ref, wbot_ref, g2_ref, wr_ref, br_ref,
                    x1_ref, h2pa_ref, h2pb_ref, slot_ref, w8_ref, cnt_ref, cnt_scr, *, cap):
    @pl.when((pl.program_id(0) == 0) & (pl.program_id(1) == 0))
    def _():
        cnt_scr[...] = jnp.zeros_like(cnt_scr)

    mod = mod_ref[0]
    mix = _dot(og_ref[0], wtop_ref[...]) + _dot(om_ref[0], wbot_ref[...])
    x1 = x_ref[0] + mod[2:3] * mix
    x1_ref[0] = x1
    h2 = _rms(x1, D_MODEL) * g2_ref[...] * (1.0 + mod[4:5]) + mod[3:4]
    h2pa_ref[0], h2pb_ref[0] = _pack_rows(h2)
    logits_t = _dot_nt(wr_ref[...], h2, precision=HIGHEST)
    slot8, w8, cnt = _route(logits_t, br_ref[...], cnt_scr[:, 0:1], cap)
    slot_ref[...] = slot8
    t = w8.shape[1]
    wb = lax.bitcast_convert_type(w8.astype(BF16).astype(F32), U32)
    wb = wb | (wb >> 16)
    w8_ref[0] = jnp.concatenate(
        [jnp.broadcast_to(wb[k:k + 1], (SC_LANES, t)) for k in range(TOP_K)], axis=0).T
    cnt_scr[...] = jnp.broadcast_to(cnt, cnt_scr.shape)
    cnt_ref[...] = cnt_scr[...]


def _outproj(og, om, x, mod, per_batch_mod, p, tm):
    b, l, d = x.shape
    nt = l // tm
    mod_map = (lambda bi, i: (bi, 0, 0)) if per_batch_mod else (lambda bi, i: (0, 0, 0))
    tok = lambda w: pl.BlockSpec((1, tm, w), lambda bi, i: (bi, i, 0))
    full = lambda a: pl.BlockSpec(a.shape, lambda bi, i: (0, 0))
    weights = [p['w_out_top'], p['w_out_bot'], p['g_norm2'], p['w_router_t'], p['b_router_col']]
    cnt_shape = (N_EXPERTS, LANES)
    return pl.pallas_call(
        functools.partial(_outproj_kernel, cap=b * l),
        grid=(b, nt),
        in_specs=[tok(GLA_VAL_W), tok(MLA_W), tok(d), pl.BlockSpec((1, 6, d), mod_map)]
                 + [full(w) for w in weights],
        out_specs=[tok(d), tok(PACK_W), tok(PACK_W),
                   pl.BlockSpec((TOP_K, tm), lambda bi, i: (0, bi * nt + i)),
                   tok(LANES), pl.BlockSpec(cnt_shape, lambda bi, i: (0, 0))],
        out_shape=[jax.ShapeDtypeStruct((b, l, d), F32), jax.ShapeDtypeStruct((b, l, PACK_W), U32),
                   jax.ShapeDtypeStruct((b, l, PACK_W), U32),
                   jax.ShapeDtypeStruct((TOP_K, b * l), I32),
                   jax.ShapeDtypeStruct((b, l, LANES), U32),
                   jax.ShapeDtypeStruct(cnt_shape, F32)],
        scratch_shapes=[pltpu.VMEM(cnt_shape, F32)],
        compiler_params=pltpu.CompilerParams(
            dimension_semantics=("arbitrary", "arbitrary"), vmem_limit_bytes=VMEM_LIMIT),
        name="outproj",
    )(og, om, x, mod, *weights)


def _sc_mesh():
    return plsc.VectorSubcoreMesh(core_axis_name="c", subcore_axis_name="s")


def _sc_scatter_rows(src, slot8, n_rows_out):
    t, d = src.shape
    nk = slot8.shape[0]

    @functools.partial(pl.kernel, out_type=jax.ShapeDtypeStruct((n_rows_out, d), src.dtype),
                       mesh=_sc_mesh(), scratch_types=[], name="dispatch")
    def run(src_hbm, slot_hbm, out_hbm):
        def body(x_vmem, i_vmem):
            for k in range(nk):
                pltpu.sync_copy(x_vmem, out_hbm.at[i_vmem.at[k]])

        pltpu.emit_pipeline(
            body, grid=(t // SC_WINDOW,),
            in_specs=[pl.BlockSpec((SC_WINDOW, d), lambda i: (i, 0)),
                      pl.BlockSpec((nk, SC_WINDOW), lambda i: (0, i))],
            out_specs=[], core_axis_name=("c", "s"),
            dimension_semantics=(pltpu.PARALLEL,))(src_hbm, slot_hbm)

    return run(src, slot8)


def _sc_collect_sum(table, slots_tk, w_rep):
    n_tok = w_rep.shape[0]
    d = table.shape[1]
    group = SC_WINDOW // TOP_K
    chunks = d // SC_LANES

    @functools.partial(pl.kernel, out_type=jax.ShapeDtypeStruct((n_tok, 2 * d), F32),
                       mesh=_sc_mesh(), scratch_types=[pltpu.VMEM((SC_WINDOW, d), table.dtype)],
                       compiler_params=pltpu.CompilerParams(needs_layout_passes=False),
                       name="collect")
    def run(tab_hbm, slot_hbm, w_hbm, out_hbm, rows_v):
        def body(i_vmem, w_vmem, o_vmem):
            pltpu.sync_copy(tab_hbm.at[i_vmem.at[0]], rows_v)

            @pl.loop(0, group)
            def _(j):
                wv = [plsc.bitcast(w_vmem[j, pl.ds(k * SC_LANES, SC_LANES)], BF16)
                      for k in range(TOP_K)]

                def chunk(off):
                    prods = [plsc.bitcast(rows_v[j * TOP_K + k, pl.ds(off, SC_LANES)], BF16)
                             * wv[k] for k in range(TOP_K)]
                    lo = jnp.zeros((SC_LANES,), F32)
                    hi = jnp.zeros((SC_LANES,), F32)
                    for k in range(0, TOP_K, 2):
                        u = plsc.bitcast(prods[k] + prods[k + 1], U32)
                        lo = lo + lax.bitcast_convert_type(u << 16, F32)
                        hi = hi + lax.bitcast_convert_type(u & jnp.uint32(0xFFFF0000), F32)
                    o_vmem[j, pl.ds(off, SC_LANES)] = lo
                    o_vmem[j, pl.ds(d + off, SC_LANES)] = hi

                @pl.loop(0, chunks)
                def _(c):
                    chunk(c * SC_LANES)

        pltpu.emit_pipeline(
            body, grid=(n_tok // group,),
            in_specs=[pl.BlockSpec((1, SC_WINDOW), lambda i: (0, i)),
                      pl.BlockSpec((group, TOP_K * SC_LANES), lambda i: (i, 0))],
            out_specs=[pl.BlockSpec((group, 2 * d), lambda i: (i, 0))],
            core_axis_name=("c", "s"),
            dimension_semantics=(pltpu.PARALLEL,))(slot_hbm, w_hbm, out_hbm)

    return run(table, slots_tk, w_rep)


def _ffn_kernel(nt_ref, first_ref, te_ref, tt_ref, xsa_hbm, xsb_hbm, wg_ref, wu_ref, wd_ref,
                ysa_hbm, ysb_hbm, xbuf, ybuf, wg_b, wu_b, wd_b, sem_in, sem_out, *, cap):
    e = pl.program_id(0)
    n_exp = pl.num_programs(0)
    n = nt_ref[e]
    g0 = first_ref[e]
    total = first_ref[n_exp - 1] + nt_ref[n_exp - 1]
    xs_hbm = (xsa_hbm, xsb_hbm)
    ys_hbm = (ysa_hbm, ysb_hbm)
    n_in = FFN_AHEAD + 1

    def rows(ex, t):
        return pl.ds(pl.multiple_of(ex * cap + t * FFN_TILE, FFN_TILE), FFN_TILE)

    def in_copy(ex, t, slot, part):
        return pltpu.make_async_copy(xs_hbm[part].at[rows(ex, t)], xbuf.at[slot, part],
                                     sem_in.at[slot, part])

    def out_copy(ex, t, slot, part):
        return pltpu.make_async_copy(ybuf.at[slot, part], ys_hbm[part].at[rows(ex, t)],
                                     sem_out.at[slot, part])

    def start_in(g):
        for part in range(PACK_PARTS):
            in_copy(te_ref[g], tt_ref[g], g % n_in, part).start()

    def wait_out(slot):
        for part in range(PACK_PARTS):
            out_copy(e, 0, slot, part).wait()

    @pl.when(n > 0)
    def _():
        @pl.when(g0 == 0)
        def _():
            for ahead in range(FFN_AHEAD):
                @pl.when(ahead < total)
                def _():
                    start_in(ahead)

        wg_b[...] = wg_ref[0].astype(BF16)
        wu_b[...] = wu_ref[0].astype(BF16)
        wd_b[...] = wd_ref[0].astype(BF16)

        def tile(t, carry):
            g = g0 + t
            slot = g % 2
            slot_in = g % n_in
            for part in range(PACK_PARTS):
                in_copy(e, t, slot_in, part).wait()

            @pl.when(g + FFN_AHEAD < total)
            def _():
                start_in(g + FFN_AHEAD)

            @pl.when(g >= 2)
            def _():
                wait_out(slot)

            x = _unpack_rows([xbuf[slot_in, part] for part in range(PACK_PARTS)]).astype(BF16)
            a = _silu(_dot(x, wg_b[...])) * _dot(x, wu_b[...])
            y = _pack_rows(_dot(a.astype(BF16), wd_b[...]))
            for part in range(PACK_PARTS):
                ybuf[slot, part] = y[part]
                out_copy(e, t, slot, part).start()
            return carry

        lax.fori_loop(0, n, tile, 0)

        @pl.when(g0 + n == total)
        def _():
            @pl.when(total >= 2)
            def _():
                wait_out(total % 2)

            wait_out((total - 1) % 2)


def _ffn(xs_parts, counts, w_gate, w_up, w_down):
    n_exp, d = w_gate.shape[0], w_gate.shape[1]
    cap = xs_parts[0].shape[0] // n_exp
    ntiles = (counts + FFN_TILE - 1) // FFN_TILE
    cum = jnp.cumsum(ntiles)
    first = cum - ntiles
    max_tiles = (cap * TOP_K) // FFN_TILE + n_exp
    g = jnp.arange(max_tiles, dtype=I32)
    done = g[:, None] >= cum[None, :]
    tile_e = jnp.minimum(jnp.sum(done, axis=1), n_exp - 1).astype(I32)
    tile_t = (g - jnp.sum(jnp.where(done, ntiles[None, :], 0), axis=1)).astype(I32)
    wmap = lambda e, nt, first, te, tt: (e, 0, 0)
    hbm = pl.BlockSpec(memory_space=pl.ANY)
    return pl.pallas_call(
        functools.partial(_ffn_kernel, cap=cap),
        grid_spec=pltpu.PrefetchScalarGridSpec(
            num_scalar_prefetch=4, grid=(n_exp,),
            in_specs=[hbm] * PACK_PARTS
                     + [pl.BlockSpec((1, d, D_EXPERT), wmap), pl.BlockSpec((1, d, D_EXPERT), wmap),
                        pl.BlockSpec((1, D_EXPERT, d), wmap)],
            out_specs=[hbm] * PACK_PARTS,
            scratch_shapes=[pltpu.VMEM((FFN_AHEAD + 1, PACK_PARTS, FFN_TILE, PACK_W), U32),
                            pltpu.VMEM((2, PACK_PARTS, FFN_TILE, PACK_W), U32),
                            pltpu.VMEM((d, D_EXPERT), BF16), pltpu.VMEM((d, D_EXPERT), BF16),
                            pltpu.VMEM((D_EXPERT, d), BF16),
                            pltpu.SemaphoreType.DMA((FFN_AHEAD + 1, PACK_PARTS)),
                            pltpu.SemaphoreType.DMA((2, PACK_PARTS))]),
        out_shape=[jax.ShapeDtypeStruct(xs_parts[0].shape, U32)] * PACK_PARTS,
        compiler_params=pltpu.CompilerParams(
            dimension_semantics=("arbitrary",), vmem_limit_bytes=VMEM_LIMIT),
        name="ffn",
    )(ntiles.astype(I32), first.astype(I32), tile_e, tile_t, *xs_parts, w_gate, w_up, w_down)


def _combine_kernel(x1_ref, h2pa_ref, h2pb_ref, ra_ref, rb_ref, mod_ref,
                    wsg_ref, wsu_ref, wsd_ref, o_ref):
    h2 = _unpack_rows([h2pa_ref[0], h2pb_ref[0]]).astype(BF16)
    a = _silu(_dot(h2, wsg_ref[...])) * _dot(h2, wsu_ref[...])
    routed = jnp.concatenate([ra_ref[0], rb_ref[0]], axis=1)
    o_ref[0] = x1_ref[0] + mod_ref[0][5:6] * (_dot(a.astype(BF16), wsd_ref[...]) + routed)


def _combine(x1, h2p_parts, routed_parts, mod, per_batch_mod, p, tm):
    b, l, d = x1.shape
    mod_map = (lambda bi, i: (bi, 0, 0)) if per_batch_mod else (lambda bi, i: (0, 0, 0))
    tok = lambda w: pl.BlockSpec((1, tm, w), lambda bi, i: (bi, i, 0))
    full = lambda a: pl.BlockSpec(a.shape, lambda bi, i: (0, 0))
    weights = [p['w_sh_gate'], p['w_sh_up'], p['w_sh_down']]
    routed_parts = [r.reshape(b, l, d // PACK_PARTS) for r in routed_parts]
    return pl.pallas_call(
        _combine_kernel,
        grid=(b, l // tm),
        in_specs=[tok(d)] + [tok(PACK_W)] * PACK_PARTS + [tok(d // PACK_PARTS)] * PACK_PARTS
                 + [pl.BlockSpec((1, 6, d), mod_map)] + [full(w) for w in weights],
        out_specs=tok(d),
        out_shape=jax.ShapeDtypeStruct((b, l, d), F32),
        compiler_params=pltpu.CompilerParams(
            dimension_semantics=("parallel", "parallel"), vmem_limit_bytes=VMEM_LIMIT),
        name="combine",
    )(x1, *h2p_parts, *routed_parts, mod, *weights)


def _pad_heads(w, parts):
    k = w.shape[0]
    per = w.shape[1] // MLA_HEADS
    w = w.reshape(k, MLA_HEADS, per)[:, :, parts[0]:parts[1]]
    w = jnp.pad(w, ((0, 0), (0, 0), (0, HEAD_W - (parts[1] - parts[0]))))
    return w.reshape(k, MLA_HEADS * HEAD_W)


def _prep_params(l, g_norm1, g_norm2, w_in, w_gk_fwd, b_gk_fwd, w_gk_bwd, b_gk_bwd, g_gla_out,
                 g_q_lora, w_uq, g_kv_lora, w_ukv, g_qk_q, g_qk_k, w_out, w_router, b_router,
                 w_sh_gate, w_sh_up, w_sh_down):
    w = w_in[l]
    d = w.shape[0]
    o_lrf = COL_CQ
    o_lrb = o_lrf + GLA_GATE_RANK
    o_cq = o_lrb + GLA_GATE_RANK
    o_ckv = o_cq + Q_LORA
    o_kr = o_ckv + KV_LORA
    w_in_p = jnp.concatenate([
        w[:, 0:COL_CQ], w[:, o_cq:o_ckv], w[:, o_ckv:o_kr], jnp.zeros((d, KR_LANE0), w.dtype),
        w[:, o_kr:o_kr + ROPE_DIM], w[:, o_lrf:o_lrb], w[:, o_lrb:o_cq]], axis=1).astype(BF16)
    z = jnp.zeros((GLA_GATE_RANK, GLA_KEY_W), F32)
    w_gk_big = jnp.concatenate([
        jnp.zeros((LANES - 2 * GLA_GATE_RANK, 2 * GLA_KEY_W), F32),
        jnp.concatenate([w_gk_fwd[l], z], axis=1),
        jnp.concatenate([z, w_gk_bwd[l]], axis=1)], axis=0).astype(BF16)
    pad_gain = lambda g: jnp.pad(g, (0, HEAD_W - QK_DIM)).reshape(1, HEAD_W)
    w_ukv_h = w_ukv[l]
    return {
        'g_norm1': g_norm1[l].reshape(1, d), 'g_norm2': g_norm2[l].reshape(1, d),
        'w_in_p': w_in_p, 'w_gk_big': w_gk_big,
        'b_gk': jnp.concatenate([b_gk_fwd[l], b_gk_bwd[l]]).reshape(1, 2 * GLA_KEY_W),
        'g_gla_out': g_gla_out[l].reshape(1, GLA_DV),
        'g_q_lora': g_q_lora[l].reshape(1, Q_LORA),
        'w_uq_p': _pad_heads(w_uq[l], (0, QK_DIM)).astype(BF16),
        'g_kv_lora': g_kv_lora[l].reshape(1, KV_LORA),
        'w_ukv_p': jnp.concatenate([_pad_heads(w_ukv_h, (0, NOPE_DIM)),
                                    _pad_heads(w_ukv_h, (NOPE_DIM, NOPE_DIM + V_DIM))],
                                   axis=1).astype(BF16),
        'gq': pad_gain(g_qk_q[l]), 'gk': pad_gain(g_qk_k[l]),
        'w_out_top': w_out[l][:GLA_VAL_W].astype(BF16),
        'w_out_bot': w_out[l][GLA_VAL_W:].astype(BF16),
        'w_router_t': w_router[l].T, 'b_router_col': b_router[l].reshape(N_EXPERTS, 1),
        'w_sh_gate': w_sh_gate[l].astype(BF16), 'w_sh_up': w_sh_up[l].astype(BF16),
        'w_sh_down': w_sh_down[l].astype(BF16),
    }


def _rope_tables(length):
    pos = np.arange(length)
    r = (pos // GRID_W).astype(np.float32)
    c = (pos % GRID_W).astype(np.float32)
    half = ROPE_DIM // 4
    inv_freq = np.float32(ROPE_THETA) ** (-np.arange(half, dtype=np.float32) / np.float32(half))
    ang_r = r[:, None] * inv_freq[None, :]
    ang_c = c[:, None] * inv_freq[None, :]
    zeros = lambda w: np.zeros((length, w), np.float32)
    ones = lambda w: np.ones((length, w), np.float32)
    tail_w = HEAD_W - ROPE_LANE0 - ROPE_DIM
    cos = np.concatenate([ones(ROPE_LANE0), np.cos(ang_r), np.cos(ang_r), np.cos(ang_c),
                          np.cos(ang_c), ones(tail_w)], axis=1)
    s1 = np.concatenate([zeros(ROPE_LANE0), -np.sin(ang_r), zeros(half), -np.sin(ang_c),
                         zeros(half), zeros(tail_w)], axis=1)
    s2 = np.concatenate([zeros(ROPE_LANE0), zeros(half), np.sin(ang_r), zeros(half),
                         np.sin(ang_c), zeros(tail_w)], axis=1)
    return tuple(jnp.asarray(t, F32) for t in (cos, s1, s2))


def _mix_route_dispatch(x_tok, seq_shape, mod, per_batch_mod, p, s0f, s0b, rope_tabs, ctx_kv,
                        tq):
    bt, lt, d = x_tok.shape
    b, l = seq_shape
    n_tok = bt * lt
    (q, k, v, g, lgf, lgb, qm, km, vm, ckvn, kr) = _inproj(x_tok, mod, per_batch_mod, p,
                                                            rope_tabs, IN_TILE)
    seq = lambda a: a.reshape(b, l, a.shape[-1])
    og, sf, sb = _gla(seq(q), seq(k), seq(v), seq(g), seq(lgf), seq(lgb), s0f, s0b,
                      p['g_gla_out'])
    kv_pairs = [(seq(km), seq(vm))] + ([ctx_kv] if ctx_kv is not None else [])
    om = _attn(seq(qm), kv_pairs, tq)
    tokv = lambda a: a.reshape(bt, lt, a.shape[-1])
    x1, h2a, h2b, slot8, w8, cnt = _outproj(tokv(og), tokv(om), x_tok, mod, per_batch_mod, p,
                                            OUT_TILE)
    x_sorted = [_sc_scatter_rows(h.reshape(n_tok, PACK_W), slot8, N_EXPERTS * n_tok)
                for h in (h2a, h2b)]
    routed = dict(x1=x1, h2=(h2a, h2b), slot8=slot8, w8=w8, cnt=cnt, x_sorted=x_sorted)
    return routed, sf, sb, ckvn, kr


def _experts_collect(r, w_gate, w_up, w_down):
    n_tok = r['slot8'].shape[1]
    y_sorted = _ffn(r['x_sorted'], r['cnt'][:, 0].astype(I32), w_gate, w_up, w_down)
    slots_tk = r['slot8'].T.reshape(1, TOP_K * n_tok)
    w_rep = r['w8'].reshape(n_tok, LANES)
    return [_sc_collect_sum(y, slots_tk, w_rep) for y in y_sorted]


def kernel(x_prompt, x_sample, c, state_gla_fwd, state_gla_bwd, cache_mla_ckv, cache_mla_krope,
           c_ctx, w_ada, b_ada, g_norm1, g_norm2, w_in, w_gk_fwd, b_gk_fwd, w_gk_bwd, b_gk_bwd,
           g_gla_out, g_q_lora, w_uq, g_kv_lora, w_ukv, g_qk_q, g_qk_k, w_out,
           w_router, b_router, w_exp_gate, w_exp_up, w_exp_down, w_sh_gate, w_sh_up, w_sh_down):
    bp, lp, d = x_prompt.shape
    bs, ls, _ = x_sample.shape
    depth = w_ada.shape[0]
    xp = x_prompt.reshape(1, bp * lp, d)
    xs = x_sample
    new_f, new_b, new_ckv, new_kr = [], [], [], []
    lat_tabs = _rope_tables(ls)
    cvecs = jnp.concatenate([c_ctx[None], c, jnp.zeros((8 - 1 - bs, d), F32)], axis=0)
    for l in range(depth):
        p = _prep_params(l, g_norm1, g_norm2, w_in, w_gk_fwd, b_gk_fwd, w_gk_bwd, b_gk_bwd,
                         g_gla_out, g_q_lora, w_uq, g_kv_lora, w_ukv, g_qk_q, g_qk_k, w_out,
                         w_router, b_router, w_sh_gate, w_sh_up, w_sh_down)
        mod = _ada(cvecs, w_ada[l], b_ada[l]).reshape(8, 6, d)
        mod_p, mod_s = mod[0:1], mod[1:1 + bs]
        zeros = jnp.zeros((bp, GLA_KEY_W, GLA_DV), F32)
        r_ctx, sf, sb, ckvn, kr = _mix_route_dispatch(
            xp, (bp, lp), mod_p, False, p, zeros, zeros, (), None, lp)
        new_f.append(sf.reshape(bp, GLA_HEADS, GLA_DK, GLA_DV))
        new_b.append(sb.reshape(bp, GLA_HEADS, GLA_DK, GLA_DV))
        new_ckv.append(ckvn.reshape(bp, lp, KV_LORA))
        new_kr.append(kr.reshape(bp, lp, LANES)[:, :, KR_LANE0:KR_LANE0 + ROPE_DIM])
        kr_cache = jnp.pad(cache_mla_krope[:, l],
                           ((0, 0), (0, 0), (ROPE_LANE0, LANES - ROPE_LANE0 - ROPE_DIM)))
        ctx_kv = _cache_kv(cache_mla_ckv[:, l], kr_cache, p)
        s0f = state_gla_fwd[:, l].reshape(bs, GLA_KEY_W, GLA_DV)
        s0b = state_gla_bwd[:, l].reshape(bs, GLA_KEY_W, GLA_DV)
        r_lat, _, _, _, _ = _mix_route_dispatch(
            xs, (bs, ls), mod_s, True, p, s0f, s0b, lat_tabs, ctx_kv, 256)
        experts = (w_exp_gate[l], w_exp_up[l], w_exp_down[l])
        routed_ctx = _experts_collect(r_ctx, *experts)
        routed_lat = _experts_collect(r_lat, *experts)
        xp = _combine(r_ctx['x1'], r_ctx['h2'], routed_ctx, mod_p, False, p, OUT_TILE)
        xs = _combine(r_lat['x1'], r_lat['h2'], routed_lat, mod_s, True, p, OUT_TILE)
    return (xp.reshape(bp, lp, d), xs, jnp.stack(new_f, axis=1), jnp.stack(new_b, axis=1),
            jnp.stack(new_ckv, axis=1), jnp.stack(new_kr, axis=1))
```

```python
import functools

import jax
import jax.numpy as jnp
import numpy as np
from jax import lax
from jax.experimental import pallas as pl
from jax.experimental.pallas import tpu as pltpu
from jax.experimental.pallas import tpu_sc as plsc

F32 = jnp.float32
BF16 = jnp.bfloat16
I32 = jnp.int32
U32 = jnp.uint32

D_MODEL = 1024
EPS = 1e-6
GRID_W = 64
GLA_HEADS = 4
GLA_DK = 64
GLA_DV = 128
GLA_GATE_RANK = 16
GLA_GATE_NORM = 16.0
GLA_KEY_W = GLA_HEADS * GLA_DK
GLA_VAL_W = GLA_HEADS * GLA_DV
MLA_HEADS = 4
Q_LORA = 256
KV_LORA = 128
NOPE_DIM = 64
ROPE_DIM = 32
V_DIM = 128
QK_DIM = NOPE_DIM + ROPE_DIM
ROPE_THETA = 10000.0
N_EXPERTS = 64
TOP_K = 8
N_GROUPS = 8
TOPK_GROUPS = 4
D_EXPERT = 256
ROUTED_SCALE = 2.5

LANES = 128
HEAD_W = LANES
MLA_W = MLA_HEADS * HEAD_W
ROPE_LANE0 = NOPE_DIM
COL_V = 2 * GLA_KEY_W
COL_G = COL_V + GLA_VAL_W
COL_CQ = COL_G + GLA_VAL_W
COL_CKV = COL_CQ + Q_LORA
TAIL0 = COL_CKV + KV_LORA
IN_W = TAIL0 + LANES
KR_LANE0 = LANES - ROPE_DIM - 2 * GLA_GATE_RANK
GLA_BLOCK = 256
GLA_SUB = 64
IN_TILE = 256
OUT_TILE = 512
FFN_TILE = 512
FFN_AHEAD = 5
PACK_PARTS = 2
PACK_W = D_MODEL // (2 * PACK_PARTS)
SC_WINDOW = 128
SC_LANES = 16
VMEM_LIMIT = 56 * 1024 * 1024

HIGHEST = lax.Precision.HIGHEST


def _dot(a, b, precision=None):
    return jnp.dot(a, b, preferred_element_type=F32, precision=precision)


def _dot_nt(a, b, precision=None):
    return lax.dot_general(a, b, (((1,), (1,)), ((), ())), preferred_element_type=F32,
                           precision=precision)


def _rms(x, width):
    ss = jnp.sum(x * x, axis=-1, keepdims=True) * (1.0 / width)
    return x * lax.rsqrt(ss + EPS)


def _silu(x):
    return x * jax.nn.sigmoid(x)


def _log_sigmoid(x):
    return jnp.minimum(x, 0.0) - jnp.log1p(jnp.exp(-jnp.abs(x)))


def _pack_rows(x):
    parts = []
    for i in range(PACK_PARTS):
        c0 = i * 2 * PACK_W
        lo = lax.bitcast_convert_type(x[:, c0:c0 + PACK_W].astype(BF16).astype(F32), U32)
        hi = lax.bitcast_convert_type(
            x[:, c0 + PACK_W:c0 + 2 * PACK_W].astype(BF16).astype(F32), U32)
        parts.append(hi | (lo >> 16))
    return parts


def _unpack_rows(parts):
    cols = []
    for w in parts:
        cols.append(lax.bitcast_convert_type(w << 16, F32))
        cols.append(lax.bitcast_convert_type(w & jnp.uint32(0xFFFF0000), F32))
    return jnp.concatenate(cols, axis=1)


def _ada_kernel(c_ref, w_ref, b_ref, o_ref):
    o_ref[...] = _dot(_silu(c_ref[...]), w_ref[...], precision=HIGHEST) + b_ref[...]


def _ada(cvecs, w_ada, b_ada):
    n = w_ada.shape[1]
    tn = 768
    return pl.pallas_call(
        _ada_kernel,
        grid=(n // tn,),
        in_specs=[pl.BlockSpec((8, D_MODEL), lambda j: (0, 0)),
                  pl.BlockSpec((D_MODEL, tn), lambda j: (0, j)),
                  pl.BlockSpec((1, tn), lambda j: (0, j))],
        out_specs=pl.BlockSpec((8, tn), lambda j: (0, j)),
        out_shape=jax.ShapeDtypeStruct((8, n), F32),
        name="ada",
    )(cvecs, w_ada, b_ada.reshape(1, n))


def _rope(x, c, s1, s2):
    return x * c + pltpu.roll(x, LANES - 8, 1) * s1 + pltpu.roll(x, 8, 1) * s2


def _mla_kv(ckv_n, kr, w_ukv_ref, gk, rope_tabs, k_ref, v_ref):
    kv = _dot(ckv_n.astype(BF16), w_ukv_ref[...])
    for h in range(MLA_HEADS):
        k_h = kv[:, h * HEAD_W:(h + 1) * HEAD_W] + kr
        k_h = _rms(k_h, QK_DIM) * gk
        if rope_tabs is not None:
            k_h = _rope(k_h, *rope_tabs)
        k_ref[0, :, h * HEAD_W:(h + 1) * HEAD_W] = k_h.astype(BF16)
    v_ref[0] = kv[:, MLA_W:].astype(BF16)


def _lane_mask(lo, hi, rows):
    lane = lax.broadcasted_iota(I32, (rows, LANES), 1)
    return (lane >= lo) & (lane < hi)


def _inproj_kernel(x_ref, mod_ref, g1_ref, win_ref, wgk_ref, bgk_ref, gql_ref, wuq_ref,
                   gkv_ref, wukv_ref, gq_ref, gk_ref, *refs, positions):
    tab_refs, outs = (refs[:3], refs[3:]) if positions else ((), refs)
    (q_ref, k_ref, v_ref, g_ref, lgf_ref, lgb_ref, qm_ref, km_ref, vm_ref, ckvn_ref,
     kr_ref) = outs
    x = x_ref[0]
    mod = mod_ref[0]
    h = _rms(x, D_MODEL) * g1_ref[...] * (1.0 + mod[1:2]) + mod[0:1]
    y = _dot(h.astype(BF16), win_ref[...])
    q_ref[0] = (y[:, 0:GLA_KEY_W] * GLA_DK ** -0.5).astype(BF16)
    k_ref[0] = y[:, GLA_KEY_W:COL_V].astype(BF16)
    v_ref[0] = y[:, COL_V:COL_G].astype(BF16)
    g_ref[0] = y[:, COL_G:COL_CQ].astype(BF16)
    tail = y[:, TAIL0:IN_W]
    pre = _dot(tail.astype(BF16), wgk_ref[...]) + bgk_ref[...]
    logg = _log_sigmoid(pre) * (1.0 / GLA_GATE_NORM)
    lgf_ref[0] = logg[:, 0:GLA_KEY_W]
    lgb_ref[0] = logg[:, GLA_KEY_W:]
    rows = x.shape[0]
    tabs = tuple(r[...] for r in tab_refs) if positions else None
    cq = _rms(y[:, COL_CQ:COL_CKV], Q_LORA) * gql_ref[...]
    qm = _dot(cq.astype(BF16), wuq_ref[...])
    gq = gq_ref[...]
    for hh in range(MLA_HEADS):
        q_h = _rms(qm[:, hh * HEAD_W:(hh + 1) * HEAD_W], QK_DIM) * gq
        if positions:
            q_h = _rope(q_h, *tabs)
        qm_ref[0, :, hh * HEAD_W:(hh + 1) * HEAD_W] = (q_h * QK_DIM ** -0.5).astype(BF16)
    ckv_n = _rms(y[:, COL_CKV:TAIL0], KV_LORA) * gkv_ref[...]
    ckvn_ref[0] = ckv_n
    kr_ref[0] = tail
    kr = jnp.where(_lane_mask(KR_LANE0, KR_LANE0 + ROPE_DIM, rows), tail, 0.0)
    _mla_kv(ckv_n, kr, wukv_ref, gk_ref[...], tabs, km_ref, vm_ref)


def _inproj(x, mod, per_batch_mod, p, rope_tabs, tm):
    b, l, d = x.shape
    nt = l // tm
    mod_map = (lambda bi, i: (bi, 0, 0)) if per_batch_mod else (lambda bi, i: (0, 0, 0))
    tab_map = lambda bi, i: (i, 0)
    const = lambda bi, i: (0, 0)
    tok = lambda w: pl.BlockSpec((1, tm, w), lambda bi, i: (bi, i, 0))
    full = lambda a: pl.BlockSpec(a.shape, const)
    weights = [p['g_norm1'], p['w_in_p'], p['w_gk_big'], p['b_gk'], p['g_q_lora'], p['w_uq_p'],
               p['g_kv_lora'], p['w_ukv_p'], p['gq'], p['gk']]
    outs = [(GLA_KEY_W, BF16), (GLA_KEY_W, BF16), (GLA_VAL_W, BF16), (GLA_VAL_W, BF16),
            (GLA_KEY_W, F32), (GLA_KEY_W, F32), (MLA_W, BF16), (MLA_W, BF16), (MLA_W, BF16),
            (KV_LORA, F32), (LANES, F32)]
    return pl.pallas_call(
        functools.partial(_inproj_kernel, positions=bool(rope_tabs)),
        grid=(b, nt),
        in_specs=[tok(d), pl.BlockSpec((1, 6, d), mod_map)] + [full(w) for w in weights]
                 + [pl.BlockSpec((tm, LANES), tab_map)] * len(rope_tabs),
        out_specs=[tok(w) for w, _ in outs],
        out_shape=[jax.ShapeDtypeStruct((b, l, w), dt) for w, dt in outs],
        compiler_params=pltpu.CompilerParams(
            dimension_semantics=("parallel", "parallel"), vmem_limit_bytes=VMEM_LIMIT),
        name="inproj",
    )(x, mod, *weights, *rope_tabs)


def _cache_kv_kernel(ckv_ref, kr_ref, wukv_ref, gk_ref, k_ref, v_ref):
    _mla_kv(ckv_ref[0], kr_ref[0], wukv_ref, gk_ref[...], None, k_ref, v_ref)


def _cache_kv(ckv, kr128, p):
    b, l, _ = ckv.shape
    tok = lambda w: pl.BlockSpec((1, l, w), lambda bi: (bi, 0, 0))
    full = lambda a: pl.BlockSpec(a.shape, lambda bi: (0, 0))
    return pl.pallas_call(
        _cache_kv_kernel,
        grid=(b,),
        in_specs=[tok(KV_LORA), tok(LANES), full(p['w_ukv_p']), full(p['gk'])],
        out_specs=[tok(MLA_W), tok(MLA_W)],
        out_shape=[jax.ShapeDtypeStruct((b, l, MLA_W), BF16)] * 2,
        name="cache_kv",
    )(ckv, kr128, p['w_ukv_p'], p['gk'])


def _split3(x):
    hi = x.astype(BF16)
    r1 = x - hi.astype(F32)
    mid = r1.astype(BF16)
    lo = (r1 - mid.astype(F32)).astype(BF16)
    return hi, mid, lo


def _gla_block(q, k, v, lg, s, fwd):
    n = GLA_BLOCK
    nc = n // GLA_SUB
    row = lax.broadcasted_iota(I32, (n, n), 0)
    col = lax.broadcasted_iota(I32, (n, n), 1)
    same = (row // GLA_SUB) == (col // GLA_SUB)
    causal = same & ((col <= row) if fwd else (col >= row))
    tri = causal.astype(BF16)
    hi, mid, lo = _split3(lg)
    cum = _dot(tri, hi) + _dot(tri, mid) + _dot(tri, lo)
    tot_rows, mid_rows = [], []
    for c in range(nc):
        r_tot = c * GLA_SUB + (GLA_SUB - 1 if fwd else 0)
        r_mid = c * GLA_SUB + GLA_SUB // 2
        tot_rows.append(jnp.broadcast_to(cum[r_tot:r_tot + 1], (GLA_SUB, GLA_KEY_W)))
        mid_rows.append(jnp.broadcast_to(cum[r_mid:r_mid + 1], (GLA_SUB, GLA_KEY_W)))
    tot_rows = jnp.concatenate(tot_rows, axis=0)
    mid_rows = jnp.concatenate(mid_rows, axis=0)
    rel = cum - mid_rows
    qi = q * jnp.exp(rel)
    ki = (k * jnp.exp(-rel)).astype(BF16)
    q_in = (q * jnp.exp(cum)).astype(BF16)
    k_up = k * jnp.exp(tot_rows - cum)
    dec = jnp.exp(tot_rows)
    k_up_t = k_up.T.astype(BF16)
    dec_t = dec.T
    lane_head = lax.broadcasted_iota(I32, (n, GLA_KEY_W), 1) // GLA_DK
    srow_head = lax.broadcasted_iota(I32, (GLA_KEY_W, GLA_DV), 0) // GLA_DK
    vrow_chunk = lax.broadcasted_iota(I32, (n, GLA_VAL_W), 0) // GLA_SUB
    o_heads = []
    for h in range(GLA_HEADS):
        a = _dot_nt(jnp.where(lane_head == h, qi, 0.0).astype(BF16), ki)
        a = jnp.where(causal, a, 0.0).astype(BF16)
        o_heads.append(_dot(a, v[:, h * GLA_DV:(h + 1) * GLA_DV]))
    o = jnp.concatenate(o_heads, axis=1)
    o_inter = [None] * nc
    for c in (range(nc) if fwd else range(nc - 1, -1, -1)):
        s_bd = jnp.concatenate(
            [jnp.where(srow_head == h, s, 0.0).astype(BF16) for h in range(GLA_HEADS)], axis=1)
        o_inter[c] = _dot(q_in[c * GLA_SUB:(c + 1) * GLA_SUB], s_bd)
        v_c = jnp.where(vrow_chunk == c, v, jnp.zeros_like(v))
        u = jnp.concatenate(
            [_dot(k_up_t[h * GLA_DK:(h + 1) * GLA_DK], v_c[:, h * GLA_DV:(h + 1) * GLA_DV])
             for h in range(GLA_HEADS)], axis=0)
        s = dec_t[:, c * GLA_SUB:c * GLA_SUB + 1] * s + u
    return o + jnp.concatenate(o_inter, axis=0), s


def _gla_kernel(q_ref, k_ref, v_ref, g_ref, lgf_ref, lgb_ref, s0f_ref, s0b_ref, gout_ref,
                o_ref, sf_ref, sb_ref, acc_ref, st_ref, *, nblk):
    st_ref[0] = s0f_ref[0]
    st_ref[1] = s0b_ref[0]

    def load(blk):
        start = blk * GLA_BLOCK
        r = pl.ds(start if isinstance(blk, int) else pl.multiple_of(start, GLA_BLOCK), GLA_BLOCK)
        return r, q_ref[0, r, :].astype(F32), k_ref[0, r, :].astype(F32), v_ref[0, r, :]

    def pair_step(i, first_touch):
        rf, qf, kf, vf = load(i)
        rb, qb, kb, vb = load(nblk - 1 - i)
        of, s_f = _gla_block(qf, kf, vf, lgf_ref[0, rf, :], st_ref[0], True)
        ob, s_b = _gla_block(qb, kb, vb, lgb_ref[0, rb, :], st_ref[1], False)
        if first_touch:
            acc_ref[rf, :] = of
            acc_ref[rb, :] = ob
        else:
            acc_ref[rf, :] += of
            acc_ref[rb, :] += ob
        st_ref[0] = s_f
        st_ref[1] = s_b

    if nblk == 1:
        r, q, k, v = load(0)
        of, s_f = _gla_block(q, k, v, lgf_ref[0, r, :], st_ref[0], True)
        ob, s_b = _gla_block(q, k, v, lgb_ref[0, r, :], st_ref[1], False)
        acc_ref[r, :] = of + ob
        st_ref[0] = s_f
        st_ref[1] = s_b
    else:
        half = nblk // 2
        lax.fori_loop(0, half, lambda i, c: (pair_step(i, True), c)[1], 0)
        lax.fori_loop(half, nblk, lambda i, c: (pair_step(i, False), c)[1], 0)
    sf_ref[0] = st_ref[0]
    sb_ref[0] = st_ref[1]

    def fin_step(blk, carry):
        r = pl.ds(pl.multiple_of(blk * GLA_BLOCK, GLA_BLOCK), GLA_BLOCK)
        o = acc_ref[r, :]
        gate = _silu(g_ref[0, r, :].astype(F32))
        for h in range(GLA_HEADS):
            sl = slice(h * GLA_DV, (h + 1) * GLA_DV)
            o_ref[0, r, sl] = (_rms(o[:, sl], GLA_DV) * gout_ref[...] * gate[:, sl]).astype(BF16)
        return carry

    lax.fori_loop(0, nblk, fin_step, 0)


def _gla(q, k, v, g, lgf, lgb, s0f, s0b, g_out):
    b, l, _ = q.shape
    assert l == GLA_BLOCK or l % (2 * GLA_BLOCK) == 0
    seq = lambda w: pl.BlockSpec((1, l, w), lambda bi: (bi, 0, 0))
    st = pl.BlockSpec((1, GLA_KEY_W, GLA_DV), lambda bi: (bi, 0, 0))
    return pl.pallas_call(
        functools.partial(_gla_kernel, nblk=l // GLA_BLOCK),
        grid=(b,),
        in_specs=[seq(GLA_KEY_W), seq(GLA_KEY_W), seq(GLA_VAL_W), seq(GLA_VAL_W),
                  seq(GLA_KEY_W), seq(GLA_KEY_W), st, st,
                  pl.BlockSpec((1, GLA_DV), lambda bi: (0, 0))],
        out_specs=[seq(GLA_VAL_W), st, st],
        out_shape=[jax.ShapeDtypeStruct((b, l, GLA_VAL_W), BF16),
                   jax.ShapeDtypeStruct((b, GLA_KEY_W, GLA_DV), F32),
                   jax.ShapeDtypeStruct((b, GLA_KEY_W, GLA_DV), F32)],
        scratch_shapes=[pltpu.VMEM((l, GLA_VAL_W), F32),
                        pltpu.VMEM((2, GLA_KEY_W, GLA_DV), F32)],
        compiler_params=pltpu.CompilerParams(
            dimension_semantics=("parallel",), vmem_limit_bytes=VMEM_LIMIT),
        name="gla",
    )(q, k, v, g, lgf, lgb, s0f, s0b, g_out)


def _attn_kernel(*refs, n_kv):
    q_ref, o_ref = refs[0], refs[-1]
    kv = [(refs[1 + 2 * i], refs[2 + 2 * i]) for i in range(n_kv)]
    for h in range(MLA_HEADS):
        sl = slice(h * HEAD_W, (h + 1) * HEAD_W)
        q = q_ref[0, :, sl]
        scores = [_dot_nt(q, k_ref[0, :, sl]) for k_ref, _ in kv]
        m = functools.reduce(jnp.maximum, [jnp.max(s, axis=-1, keepdims=True) for s in scores])
        o, den = 0.0, 0.0
        for s, (_, v_ref) in zip(scores, kv):
            p = jnp.exp(s - m)
            den = den + jnp.sum(p, axis=-1, keepdims=True)
            o = o + _dot(p.astype(BF16), v_ref[0, :, sl])
        o_ref[0, :, sl] = (o / den).astype(BF16)


def _attn(q, kv_pairs, tq):
    b, l, _ = q.shape
    kv_specs, kv_args = [], []
    for k, v in kv_pairs:
        spec = pl.BlockSpec((1, k.shape[1], MLA_W), lambda bi, i: (bi, 0, 0))
        kv_specs += [spec, spec]
        kv_args += [k, v]
    return pl.pallas_call(
        functools.partial(_attn_kernel, n_kv=len(kv_pairs)),
        grid=(b, l // tq),
        in_specs=[pl.BlockSpec((1, tq, MLA_W), lambda bi, i: (bi, i, 0))] + kv_specs,
        out_specs=pl.BlockSpec((1, tq, MLA_W), lambda bi, i: (bi, i, 0)),
        out_shape=jax.ShapeDtypeStruct((b, l, MLA_W), BF16),
        compiler_params=pltpu.CompilerParams(
            dimension_semantics=("parallel", "parallel"), vmem_limit_bytes=VMEM_LIMIT),
        name="attn",
    )(q, *kv_args)


def _sum01(x):
    return jnp.sum(jnp.sum(x, axis=1, keepdims=True), axis=0, keepdims=True)


def _route(logits_t, bias_col, cnt_col, cap):
    t = logits_t.shape[1]
    gsz = N_EXPERTS // N_GROUPS
    scores = jax.nn.sigmoid(logits_t)
    sel = (scores + bias_col).reshape(N_GROUPS, gsz, t)
    scores = scores.reshape(N_GROUPS, gsz, t)
    neg = jnp.float32(-jnp.inf)
    ie = lax.broadcasted_iota(I32, (N_GROUPS, gsz, t), 1)
    ig = lax.broadcasted_iota(I32, (N_GROUPS, gsz, t), 0)
    m1 = jnp.max(sel, axis=1, keepdims=True)
    first = jnp.min(jnp.where(sel == m1, ie, gsz), axis=1, keepdims=True)
    m2 = jnp.max(jnp.where(ie == first, neg, sel), axis=1, keepdims=True)
    grp = m1 + m2
    igk = lax.broadcasted_iota(I32, (N_GROUPS, 1, t), 0)
    g_sel = jnp.zeros((N_GROUPS, 1, t), jnp.bool_)
    cur = grp
    for _ in range(TOPK_GROUPS):
        m = jnp.max(cur, axis=0, keepdims=True)
        pick = igk == jnp.min(jnp.where(cur == m, igk, N_GROUPS), axis=0, keepdims=True)
        g_sel = g_sel | pick
        cur = jnp.where(pick, neg, cur)
    cur = jnp.where(g_sel, sel, neg)
    idx = ig * gsz + ie
    e_sel = jnp.zeros((N_GROUPS, gsz, t), jnp.bool_)
    picks = []
    for _ in range(TOP_K):
        m = jnp.max(jnp.max(cur, axis=1, keepdims=True), axis=0, keepdims=True)
        cand = jnp.where(cur == m, idx, N_EXPERTS)
        pick = idx == jnp.min(jnp.min(cand, axis=1, keepdims=True), axis=0, keepdims=True)
        picks.append(pick)
        e_sel = e_sel | pick
        cur = jnp.where(pick, neg, cur)
    w = jnp.where(e_sel, scores, 0.0)
    gate = w / _sum01(w) * ROUTED_SCALE
    sel_f = e_sel.astype(F32).reshape(N_EXPERTS, t)
    earlier = (lax.broadcasted_iota(I32, (t, t), 0) < lax.broadcasted_iota(I32, (t, t), 1))
    rank = _dot(sel_f.astype(BF16), earlier.astype(BF16))
    base = lax.broadcasted_iota(I32, (N_EXPERTS, 1), 0).astype(F32) * float(cap) + cnt_col
    slot = (base + rank).reshape(N_GROUPS, gsz, t)
    slot8 = jnp.concatenate([_sum01(jnp.where(pk, slot, 0.0)).reshape(1, t) for pk in picks],
                            axis=0).astype(I32)
    w8 = jnp.concatenate([_sum01(jnp.where(pk, gate, 0.0)).reshape(1, t) for pk in picks], axis=0)
    return slot8, w8, cnt_col + jnp.sum(sel_f, axis=1, keepdims=True)


def _outproj_kernel(og_ref, om_ref, x_ref, mod_ref, wtop_ref, wbot_ref, g2_ref, wr_ref, br_ref,
                    x1_ref, h2pa_ref, h2pb_ref, slot_ref, w8_ref, cnt_ref, cnt_scr, *, cap):
    @pl.when((pl.program_id(0) == 0) & (pl.program_id(1) == 0))
    def _():
        cnt_scr[...] = jnp.zeros_like(cnt_scr)

    mod = mod_ref[0]
    mix = _dot(og_ref[0], wtop_ref[...]) + _dot(om_ref[0], wbot_ref[...])
    x1 = x_ref[0] + mod[2:3] * mix
    x1_ref[0] = x1
    h2 = _rms(x1, D_MODEL) * g2_ref[...] * (1.0 + mod[4:5]) + mod[3:4]
    h2pa_ref[0], h2pb_ref[0] = _pack_rows(h2)
    logits_t = _dot_nt(wr_ref[...], h2, precision=HIGHEST)
    slot8, w8, cnt = _route(logits_t, br_ref[...], cnt_scr[:, 0:1], cap)
    slot_ref[...] = slot8
    t = w8.shape[1]
    wb = lax.bitcast_convert_type(w8.astype(BF16).astype(F32), U32)
    wb = wb | (wb >> 16)
    w8_ref[0] = jnp.concatenate(
        [jnp.broadcast_to(wb[k:k + 1], (SC_LANES, t)) for k in range(TOP_K)], axis=0).T
    cnt_scr[...] = jnp.broadcast_to(cnt, cnt_scr.shape)
    cnt_ref[...] = cnt_scr[...]


def _outproj(og, om, x, mod, per_batch_mod, p, tm):
    b, l, d = x.shape
    nt = l // tm
    mod_map = (lambda bi, i: (bi, 0, 0)) if per_batch_mod else (lambda bi, i: (0, 0, 0))
    tok = lambda w: pl.BlockSpec((1, tm, w), lambda bi, i: (bi, i, 0))
    full = lambda a: pl.BlockSpec(a.shape, lambda bi, i: (0, 0))
    weights = [p['w_out_top'], p['w_out_bot'], p['g_norm2'], p['w_router_t'], p['b_router_col']]
    cnt_shape = (N_EXPERTS, LANES)
    return pl.pallas_call(
        functools.partial(_outproj_kernel, cap=b * l),
        grid=(b, nt),
        in_specs=[tok(GLA_VAL_W), tok(MLA_W), tok(d), pl.BlockSpec((1, 6, d), mod_map)]
                 + [full(w) for w in weights],
        out_specs=[tok(d), tok(PACK_W), tok(PACK_W),
                   pl.BlockSpec((TOP_K, tm), lambda bi, i: (0, bi * nt + i)),
                   tok(LANES), pl.BlockSpec(cnt_shape, lambda bi, i: (0, 0))],
        out_shape=[jax.ShapeDtypeStruct((b, l, d), F32), jax.ShapeDtypeStruct((b, l, PACK_W), U32),
                   jax.ShapeDtypeStruct((b, l, PACK_W), U32),
                   jax.ShapeDtypeStruct((TOP_K, b * l), I32),
                   jax.ShapeDtypeStruct((b, l, LANES), U32),
                   jax.ShapeDtypeStruct(cnt_shape, F32)],
        scratch_shapes=[pltpu.VMEM(cnt_shape, F32)],
        compiler_params=pltpu.CompilerParams(
            dimension_semantics=("arbitrary", "arbitrary"), vmem_limit_bytes=VMEM_LIMIT),
        name="outproj",
    )(og, om, x, mod, *weights)


def _sc_mesh():
    return plsc.VectorSubcoreMesh(core_axis_name="c", subcore_axis_name="s")


def _sc_scatter_rows(src, slot8, n_rows_out):
    t, d = src.shape
    nk = slot8.shape[0]

    @functools.partial(pl.kernel, out_type=jax.ShapeDtypeStruct((n_rows_out, d), src.dtype),
                       mesh=_sc_mesh(), scratch_types=[], name="dispatch")
    def run(src_hbm, slot_hbm, out_hbm):
        def body(x_vmem, i_vmem):
            for k in range(nk):
                pltpu.sync_copy(x_vmem, out_hbm.at[i_vmem.at[k]])

        pltpu.emit_pipeline(
            body, grid=(t // SC_WINDOW,),
            in_specs=[pl.BlockSpec((SC_WINDOW, d), lambda i: (i, 0)),
                      pl.BlockSpec((nk, SC_WINDOW), lambda i: (0, i))],
            out_specs=[], core_axis_name=("c", "s"),
            dimension_semantics=(pltpu.PARALLEL,))(src_hbm, slot_hbm)

    return run(src, slot8)


def _sc_collect_sum(table, slots_tk, w_rep):
    n_tok = w_rep.shape[0]
    d = table.shape[1]
    group = SC_WINDOW // TOP_K
    chunks = d // SC_LANES

    @functools.partial(pl.kernel, out_type=jax.ShapeDtypeStruct((n_tok, 2 * d), F32),
                       mesh=_sc_mesh(), scratch_types=[pltpu.VMEM((SC_WINDOW, d), table.dtype)],
                       compiler_params=pltpu.CompilerParams(needs_layout_passes=False),
                       name="collect")
    def run(tab_hbm, slot_hbm, w_hbm, out_hbm, rows_v):
        def body(i_vmem, w_vmem, o_vmem):
            pltpu.sync_copy(tab_hbm.at[i_vmem.at[0]], rows_v)

            @pl.loop(0, group)
            def _(j):
                wv = [plsc.bitcast(w_vmem[j, pl.ds(k * SC_LANES, SC_LANES)], BF16)
                      for k in range(TOP_K)]

                def chunk(off):
                    prods = [plsc.bitcast(rows_v[j * TOP_K + k, pl.ds(off, SC_LANES)], BF16)
                             * wv[k] for k in range(TOP_K)]
                    lo = jnp.zeros((SC_LANES,), F32)
                    hi = jnp.zeros((SC_LANES,), F32)
                    for k in range(0, TOP_K, 2):
                        u = plsc.bitcast(prods[k] + prods[k + 1], U32)
                        lo = lo + lax.bitcast_convert_type(u << 16, F32)
                        hi = hi + lax.bitcast_convert_type(u & jnp.uint32(0xFFFF0000), F32)
                    o_vmem[j, pl.ds(off, SC_LANES)] = lo
                    o_vmem[j, pl.ds(d + off, SC_LANES)] = hi

                @plsc.parallel_loop(0, chunks, unroll=2)
                def _(c):
                    chunk(c * SC_LANES)

        pltpu.emit_pipeline(
            body, grid=(n_tok // group,),
            in_specs=[pl.BlockSpec((1, SC_WINDOW), lambda i: (0, i)),
                      pl.BlockSpec((group, TOP_K * SC_LANES), lambda i: (i, 0))],
            out_specs=[pl.BlockSpec((group, 2 * d), lambda i: (i, 0))],
            core_axis_name=("c", "s"),
            dimension_semantics=(pltpu.PARALLEL,))(slot_hbm, w_hbm, out_hbm)

    return run(table, slots_tk, w_rep)


def _ffn_kernel(nt_ref, first_ref, te_ref, tt_ref, xsa_hbm, xsb_hbm, wg_ref, wu_ref, wd_ref,
                ysa_hbm, ysb_hbm, xbuf, ybuf, wg_b, wu_b, wd_b, sem_in, sem_out, *, cap):
    e = pl.program_id(0)
    n_exp = pl.num_programs(0)
    n = nt_ref[e]
    g0 = first_ref[e]
    total = first_ref[n_exp - 1] + nt_ref[n_exp - 1]
    xs_hbm = (xsa_hbm, xsb_hbm)
    ys_hbm = (ysa_hbm, ysb_hbm)
    n_in = FFN_AHEAD + 1

    def rows(ex, t):
        return pl.ds(pl.multiple_of(ex * cap + t * FFN_TILE, FFN_TILE), FFN_TILE)

    def in_copy(ex, t, slot, part):
        return pltpu.make_async_copy(xs_hbm[part].at[rows(ex, t)], xbuf.at[slot, part],
                                     sem_in.at[slot, part])

    def out_copy(ex, t, slot, part):
        return pltpu.make_async_copy(ybuf.at[slot, part], ys_hbm[part].at[rows(ex, t)],
                                     sem_out.at[slot, part])

    def start_in(g):
        for part in range(PACK_PARTS):
            in_copy(te_ref[g], tt_ref[g], g % n_in, part).start()

    def wait_out(slot):
        for part in range(PACK_PARTS):
            out_copy(e, 0, slot, part).wait()

    @pl.when(n > 0)
    def _():
        @pl.when(g0 == 0)
        def _():
            for ahead in range(FFN_AHEAD):
                @pl.when(ahead < total)
                def _():
                    start_in(ahead)

        wg_b[...] = wg_ref[0].astype(BF16)
        wu_b[...] = wu_ref[0].astype(BF16)
        wd_b[...] = wd_ref[0].astype(BF16)

        def tile(t, carry):
            g = g0 + t
            slot = g % 2
            slot_in = g % n_in
            for part in range(PACK_PARTS):
                in_copy(e, t, slot_in, part).wait()

            @pl.when(g + FFN_AHEAD < total)
            def _():
                start_in(g + FFN_AHEAD)

            @pl.when(g >= 2)
            def _():
                wait_out(slot)

            x = _unpack_rows([xbuf[slot_in, part] for part in range(PACK_PARTS)]).astype(BF16)
            a = _silu(_dot(x, wg_b[...])) * _dot(x, wu_b[...])
            y = _pack_rows(_dot(a.astype(BF16), wd_b[...]))
            for part in range(PACK_PARTS):
                ybuf[slot, part] = y[part]
                out_copy(e, t, slot, part).start()
            return carry

        lax.fori_loop(0, n, tile, 0)

        @pl.when(g0 + n == total)
        def _():
            @pl.when(total >= 2)
            def _():
                wait_out(total % 2)

            wait_out((total - 1) % 2)


def _ffn(xs_parts, counts, w_gate, w_up, w_down):
    n_exp, d = w_gate.shape[0], w_gate.shape[1]
    cap = xs_parts[0].shape[0] // n_exp
    ntiles = (counts + FFN_TILE - 1) // FFN_TILE
    cum = jnp.cumsum(ntiles)
    first = cum - ntiles
    max_tiles = (cap * TOP_K) // FFN_TILE + n_exp
    g = jnp.arange(max_tiles, dtype=I32)
    done = g[:, None] >= cum[None, :]
    tile_e = jnp.minimum(jnp.sum(done, axis=1), n_exp - 1).astype(I32)
    tile_t = (g - jnp.sum(jnp.where(done, ntiles[None, :], 0), axis=1)).astype(I32)
    wmap = lambda e, nt, first, te, tt: (e, 0, 0)
    hbm = pl.BlockSpec(memory_space=pl.ANY)
    return pl.pallas_call(
        functools.partial(_ffn_kernel, cap=cap),
        grid_spec=pltpu.PrefetchScalarGridSpec(
            num_scalar_prefetch=4, grid=(n_exp,),
            in_specs=[hbm] * PACK_PARTS
                     + [pl.BlockSpec((1, d, D_EXPERT), wmap), pl.BlockSpec((1, d, D_EXPERT), wmap),
                        pl.BlockSpec((1, D_EXPERT, d), wmap)],
            out_specs=[hbm] * PACK_PARTS,
            scratch_shapes=[pltpu.VMEM((FFN_AHEAD + 1, PACK_PARTS, FFN_TILE, PACK_W), U32),
                            pltpu.VMEM((2, PACK_PARTS, FFN_TILE, PACK_W), U32),
                            pltpu.VMEM((d, D_EXPERT), BF16), pltpu.VMEM((d, D_EXPERT), BF16),
                            pltpu.VMEM((D_EXPERT, d), BF16),
                            pltpu.SemaphoreType.DMA((FFN_AHEAD + 1, PACK_PARTS)),
                            pltpu.SemaphoreType.DMA((2, PACK_PARTS))]),
        out_shape=[jax.ShapeDtypeStruct(xs_parts[0].shape, U32)] * PACK_PARTS,
        compiler_params=pltpu.CompilerParams(
            dimension_semantics=("arbitrary",), vmem_limit_bytes=VMEM_LIMIT),
        name="ffn",
    )(ntiles.astype(I32), first.astype(I32), tile_e, tile_t, *xs_parts, w_gate, w_up, w_down)


def _combine_kernel(x1_ref, h2pa_ref, h2pb_ref, ra_ref, rb_ref, mod_ref,
                    wsg_ref, wsu_ref, wsd_ref, o_ref):
    h2 = _unpack_rows([h2pa_ref[0], h2pb_ref[0]]).astype(BF16)
    a = _silu(_dot(h2, wsg_ref[...])) * _dot(h2, wsu_ref[...])
    routed = jnp.concatenate([ra_ref[0], rb_ref[0]], axis=1)
    o_ref[0] = x1_ref[0] + mod_ref[0][5:6] * (_dot(a.astype(BF16), wsd_ref[...]) + routed)


def _combine(x1, h2p_parts, routed_parts, mod, per_batch_mod, p, tm):
    b, l, d = x1.shape
    mod_map = (lambda bi, i: (bi, 0, 0)) if per_batch_mod else (lambda bi, i: (0, 0, 0))
    tok = lambda w: pl.BlockSpec((1, tm, w), lambda bi, i: (bi, i, 0))
    full = lambda a: pl.BlockSpec(a.shape, lambda bi, i: (0, 0))
    weights = [p['w_sh_gate'], p['w_sh_up'], p['w_sh_down']]
    routed_parts = [r.reshape(b, l, d // PACK_PARTS) for r in routed_parts]
    return pl.pallas_call(
        _combine_kernel,
        grid=(b, l // tm),
        in_specs=[tok(d)] + [tok(PACK_W)] * PACK_PARTS + [tok(d // PACK_PARTS)] * PACK_PARTS
                 + [pl.BlockSpec((1, 6, d), mod_map)] + [full(w) for w in weights],
        out_specs=tok(d),
        out_shape=jax.ShapeDtypeStruct((b, l, d), F32),
        compiler_params=pltpu.CompilerParams(
            dimension_semantics=("parallel", "parallel"), vmem_limit_bytes=VMEM_LIMIT),
        name="combine",
    )(x1, *h2p_parts, *routed_parts, mod, *weights)


def _pad_heads(w, parts):
    k = w.shape[0]
    per = w.shape[1] // MLA_HEADS
    w = w.reshape(k, MLA_HEADS, per)[:, :, parts[0]:parts[1]]
    w = jnp.pad(w, ((0, 0), (0, 0), (0, HEAD_W - (parts[1] - parts[0]))))
    return w.reshape(k, MLA_HEADS * HEAD_W)


def _prep_params(l, g_norm1, g_norm2, w_in, w_gk_fwd, b_gk_fwd, w_gk_bwd, b_gk_bwd, g_gla_out,
                 g_q_lora, w_uq, g_kv_lora, w_ukv, g_qk_q, g_qk_k, w_out, w_router, b_router,
                 w_sh_gate, w_sh_up, w_sh_down):
    w = w_in[l]
    d = w.shape[0]
    o_lrf = COL_CQ
    o_lrb = o_lrf + GLA_GATE_RANK
    o_cq = o_lrb + GLA_GATE_RANK
    o_ckv = o_cq + Q_LORA
    o_kr = o_ckv + KV_LORA
    w_in_p = jnp.concatenate([
        w[:, 0:COL_CQ], w[:, o_cq:o_ckv], w[:, o_ckv:o_kr], jnp.zeros((d, KR_LANE0), w.dtype),
        w[:, o_kr:o_kr + ROPE_DIM], w[:, o_lrf:o_lrb], w[:, o_lrb:o_cq]], axis=1).astype(BF16)
    z = jnp.zeros((GLA_GATE_RANK, GLA_KEY_W), F32)
    w_gk_big = jnp.concatenate([
        jnp.zeros((LANES - 2 * GLA_GATE_RANK, 2 * GLA_KEY_W), F32),
        jnp.concatenate([w_gk_fwd[l], z], axis=1),
        jnp.concatenate([z, w_gk_bwd[l]], axis=1)], axis=0).astype(BF16)
    pad_gain = lambda g: jnp.pad(g, (0, HEAD_W - QK_DIM)).reshape(1, HEAD_W)
    w_ukv_h = w_ukv[l]
    return {
        'g_norm1': g_norm1[l].reshape(1, d), 'g_norm2': g_norm2[l].reshape(1, d),
        'w_in_p': w_in_p, 'w_gk_big': w_gk_big,
        'b_gk': jnp.concatenate([b_gk_fwd[l], b_gk_bwd[l]]).reshape(1, 2 * GLA_KEY_W),
        'g_gla_out': g_gla_out[l].reshape(1, GLA_DV),
        'g_q_lora': g_q_lora[l].reshape(1, Q_LORA),
        'w_uq_p': _pad_heads(w_uq[l], (0, QK_DIM)).astype(BF16),
        'g_kv_lora': g_kv_lora[l].reshape(1, KV_LORA),
        'w_ukv_p': jnp.concatenate([_pad_heads(w_ukv_h, (0, NOPE_DIM)),
                                    _pad_heads(w_ukv_h, (NOPE_DIM, NOPE_DIM + V_DIM))],
                                   axis=1).astype(BF16),
        'gq': pad_gain(g_qk_q[l]), 'gk': pad_gain(g_qk_k[l]),
        'w_out_top': w_out[l][:GLA_VAL_W].astype(BF16),
        'w_out_bot': w_out[l][GLA_VAL_W:].astype(BF16),
        'w_router_t': w_router[l].T, 'b_router_col': b_router[l].reshape(N_EXPERTS, 1),
        'w_sh_gate': w_sh_gate[l].astype(BF16), 'w_sh_up': w_sh_up[l].astype(BF16),
        'w_sh_down': w_sh_down[l].astype(BF16),
    }


def _rope_tables(length):
    pos = np.arange(length)
    r = (pos // GRID_W).astype(np.float32)
    c = (pos % GRID_W).astype(np.float32)
    half = ROPE_DIM // 4
    inv_freq = np.float32(ROPE_THETA) ** (-np.arange(half, dtype=np.float32) / np.float32(half))
    ang_r = r[:, None] * inv_freq[None, :]
    ang_c = c[:, None] * inv_freq[None, :]
    zeros = lambda w: np.zeros((length, w), np.float32)
    ones = lambda w: np.ones((length, w), np.float32)
    tail_w = HEAD_W - ROPE_LANE0 - ROPE_DIM
    cos = np.concatenate([ones(ROPE_LANE0), np.cos(ang_r), np.cos(ang_r), np.cos(ang_c),
                          np.cos(ang_c), ones(tail_w)], axis=1)
    s1 = np.concatenate([zeros(ROPE_LANE0), -np.sin(ang_r), zeros(half), -np.sin(ang_c),
                         zeros(half), zeros(tail_w)], axis=1)
    s2 = np.concatenate([zeros(ROPE_LANE0), zeros(half), np.sin(ang_r), zeros(half),
                         np.sin(ang_c), zeros(tail_w)], axis=1)
    return tuple(jnp.asarray(t, F32) for t in (cos, s1, s2))


def _mix_route_dispatch(x_tok, seq_shape, mod, per_batch_mod, p, s0f, s0b, rope_tabs, ctx_kv,
                        tq):
    bt, lt, d = x_tok.shape
    b, l = seq_shape
    n_tok = bt * lt
    (q, k, v, g, lgf, lgb, qm, km, vm, ckvn, kr) = _inproj(x_tok, mod, per_batch_mod, p,
                                                            rope_tabs, IN_TILE)
    seq = lambda a: a.reshape(b, l, a.shape[-1])
    og, sf, sb = _gla(seq(q), seq(k), seq(v), seq(g), seq(lgf), seq(lgb), s0f, s0b,
                      p['g_gla_out'])
    kv_pairs = [(seq(km), seq(vm))] + ([ctx_kv] if ctx_kv is not None else [])
    om = _attn(seq(qm), kv_pairs, tq)
    tokv = lambda a: a.reshape(bt, lt, a.shape[-1])
    x1, h2a, h2b, slot8, w8, cnt = _outproj(tokv(og), tokv(om), x_tok, mod, per_batch_mod, p,
                                            OUT_TILE)
    x_sorted = [_sc_scatter_rows(h.reshape(n_tok, PACK_W), slot8, N_EXPERTS * n_tok)
                for h in (h2a, h2b)]
    routed = dict(x1=x1, h2=(h2a, h2b), slot8=slot8, w8=w8, cnt=cnt, x_sorted=x_sorted)
    return routed, sf, sb, ckvn, kr


def _experts_collect(r, w_gate, w_up, w_down):
    n_tok = r['slot8'].shape[1]
    y_sorted = _ffn(r['x_sorted'], r['cnt'][:, 0].astype(I32), w_gate, w_up, w_down)
    slots_tk = r['slot8'].T.reshape(1, TOP_K * n_tok)
    w_rep = r['w8'].reshape(n_tok, LANES)
    return [_sc_collect_sum(y, slots_tk, w_rep) for y in y_sorted]


def kernel(x_prompt, x_sample, c, state_gla_fwd, state_gla_bwd, cache_mla_ckv, cache_mla_krope,
           c_ctx, w_ada, b_ada, g_norm1, g_norm2, w_in, w_gk_fwd, b_gk_fwd, w_gk_bwd, b_gk_bwd,
           g_gla_out, g_q_lora, w_uq, g_kv_lora, w_ukv, g_qk_q, g_qk_k, w_out,
           w_router, b_router, w_exp_gate, w_exp_up, w_exp_down, w_sh_gate, w_sh_up, w_sh_down):
    bp, lp, d = x_prompt.shape
    bs, ls, _ = x_sample.shape
    depth = w_ada.shape[0]
    xp = x_prompt.reshape(1, bp * lp, d)
    xs = x_sample
    new_f, new_b, new_ckv, new_kr = [], [], [], []
    lat_tabs = _rope_tables(ls)
    cvecs = jnp.concatenate([c_ctx[None], c, jnp.zeros((8 - 1 - bs, d), F32)], axis=0)
    for l in range(depth):
        p = _prep_params(l, g_norm1, g_norm2, w_in, w_gk_fwd, b_gk_fwd, w_gk_bwd, b_gk_bwd,
                         g_gla_out, g_q_lora, w_uq, g_kv_lora, w_ukv, g_qk_q, g_qk_k, w_out,
                         w_router, b_router, w_sh_gate, w_sh_up, w_sh_down)
        mod = _ada(cvecs, w_ada[l], b_ada[l]).reshape(8, 6, d)
        mod_p, mod_s = mod[0:1], mod[1:1 + bs]
        zeros = jnp.zeros((bp, GLA_KEY_W, GLA_DV), F32)
        r_ctx, sf, sb, ckvn, kr = _mix_route_dispatch(
            xp, (bp, lp), mod_p, False, p, zeros, zeros, (), None, lp)
        new_f.append(sf.reshape(bp, GLA_HEADS, GLA_DK, GLA_DV))
        new_b.append(sb.reshape(bp, GLA_HEADS, GLA_DK, GLA_DV))
        new_ckv.append(ckvn.reshape(bp, lp, KV_LORA))
        new_kr.append(kr.reshape(bp, lp, LANES)[:, :, KR_LANE0:KR_LANE0 + ROPE_DIM])
        kr_cache = jnp.pad(cache_mla_krope[:, l],
                           ((0, 0), (0, 0), (ROPE_LANE0, LANES - ROPE_LANE0 - ROPE_DIM)))
        ctx_kv = _cache_kv(cache_mla_ckv[:, l], kr_cache, p)
        s0f = state_gla_fwd[:, l].reshape(bs, GLA_KEY_W, GLA_DV)
        s0b = state_gla_bwd[:, l].reshape(bs, GLA_KEY_W, GLA_DV)
        r_lat, _, _, _, _ = _mix_route_dispatch(
            xs, (bs, ls), mod_s, True, p, s0f, s0b, lat_tabs, ctx_kv, 256)
        experts = (w_exp_gate[l], w_exp_up[l], w_exp_down[l])
        routed_ctx = _experts_collect(r_ctx, *experts)
        routed_lat = _experts_collect(r_lat, *experts)
        xp = _combine(r_ctx['x1'], r_ctx['h2'], routed_ctx, mod_p, False, p, OUT_TILE)
        xs = _combine(r_lat['x1'], r_lat['h2'], routed_lat, mod_s, True, p, OUT_TILE)
    return (xp.reshape(bp, lp, d), xs, jnp.stack(new_f, axis=1), jnp.stack(new_b, axis=1),
            jnp.stack(new_ckv, axis=1), jnp.stack(new_kr, axis=1))
```

```python
import functools

import jax
import jax.numpy as jnp
import numpy as np
from jax import lax
from jax.experimental import pallas as pl
from jax.experimental.pallas import tpu as pltpu
from jax.experimental.pallas import tpu_sc as plsc

F32 = jnp.float32
BF16 = jnp.bfloat16
I32 = jnp.int32
U32 = jnp.uint32

D_MODEL = 1024
EPS = 1e-6
GRID_W = 64
GLA_HEADS = 4
GLA_DK = 64
GLA_DV = 128
GLA_GATE_RANK = 16
GLA_GATE_NORM = 16.0
GLA_KEY_W = GLA_HEADS * GLA_DK
GLA_VAL_W = GLA_HEADS * GLA_DV
MLA_HEADS = 4
Q_LORA = 256
KV_LORA = 128
NOPE_DIM = 64
ROPE_DIM = 32
V_DIM = 128
QK_DIM = NOPE_DIM + ROPE_DIM
ROPE_THETA = 10000.0
N_EXPERTS = 64
TOP_K = 8
N_GROUPS = 8
TOPK_GROUPS = 4
D_EXPERT = 256
ROUTED_SCALE = 2.5

LANES = 128
HEAD_W = LANES
MLA_W = MLA_HEADS * HEAD_W
ROPE_LANE0 = NOPE_DIM
COL_V = 2 * GLA_KEY_W
COL_G = COL_V + GLA_VAL_W
COL_CQ = COL_G + GLA_VAL_W
COL_CKV = COL_CQ + Q_LORA
TAIL0 = COL_CKV + KV_LORA
IN_W = TAIL0 + LANES
KR_LANE0 = LANES - ROPE_DIM - 2 * GLA_GATE_RANK
GLA_BLOCK = 256
GLA_SUB = 64
GLA_SEQS = 2
IN_TILE = 512
IN_SUB = 256
OUT_TILE = 512
OUT_SUB = 512
FFN_TILE = 512
FFN_AHEAD = 5
PACK_PARTS = 2
PACK_W = D_MODEL // (2 * PACK_PARTS)
SC_WINDOW = 128
SC_LANES = 16
VMEM_LIMIT = 56 * 1024 * 1024

HIGHEST = lax.Precision.HIGHEST


def _dot(a, b, precision=None):
    return jnp.dot(a, b, preferred_element_type=F32, precision=precision)


def _dot_nt(a, b, precision=None):
    return lax.dot_general(a, b, (((1,), (1,)), ((), ())), preferred_element_type=F32,
                           precision=precision)


def _rms(x, width):
    ss = jnp.sum(x * x, axis=-1, keepdims=True) * (1.0 / width)
    return x * lax.rsqrt(ss + EPS)


def _silu(x):
    return x * jax.nn.sigmoid(x)


def _log_sigmoid(x):
    return jnp.minimum(x, 0.0) - jnp.log1p(jnp.exp(-jnp.abs(x)))


def _pack_rows(x):
    parts = []
    for i in range(PACK_PARTS):
        c0 = i * 2 * PACK_W
        lo = lax.bitcast_convert_type(x[:, c0:c0 + PACK_W].astype(BF16).astype(F32), U32)
        hi = lax.bitcast_convert_type(
            x[:, c0 + PACK_W:c0 + 2 * PACK_W].astype(BF16).astype(F32), U32)
        parts.append(hi | (lo >> 16))
    return parts


def _unpack_rows(parts):
    cols = []
    for w in parts:
        cols.append(lax.bitcast_convert_type(w << 16, F32))
        cols.append(lax.bitcast_convert_type(w & jnp.uint32(0xFFFF0000), F32))
    return jnp.concatenate(cols, axis=1)


def _ada_kernel(c_ref, w_ref, b_ref, o_ref):
    o_ref[...] = _dot(_silu(c_ref[...]), w_ref[...], precision=HIGHEST) + b_ref[...]


def _ada(cvecs, w_ada, b_ada):
    n = w_ada.shape[1]
    tn = 768
    return pl.pallas_call(
        _ada_kernel,
        grid=(n // tn,),
        in_specs=[pl.BlockSpec((8, D_MODEL), lambda j: (0, 0)),
                  pl.BlockSpec((D_MODEL, tn), lambda j: (0, j)),
                  pl.BlockSpec((1, tn), lambda j: (0, j))],
        out_specs=pl.BlockSpec((8, tn), lambda j: (0, j)),
        out_shape=jax.ShapeDtypeStruct((8, n), F32),
        name="ada",
    )(cvecs, w_ada, b_ada.reshape(1, n))


def _rope(x, c, s1, s2):
    return x * c + pltpu.roll(x, LANES - 8, 1) * s1 + pltpu.roll(x, 8, 1) * s2


def _mla_kv(ckv_n, kr, w_ukv_ref, gk, rope_tabs, k_ref, v_ref, rs=slice(None)):
    kv = _dot(ckv_n.astype(BF16), w_ukv_ref[...])
    for h in range(MLA_HEADS):
        k_h = kv[:, h * HEAD_W:(h + 1) * HEAD_W] + kr
        k_h = _rms(k_h, QK_DIM) * gk
        if rope_tabs is not None:
            k_h = _rope(k_h, *rope_tabs)
        k_ref[0, rs, h * HEAD_W:(h + 1) * HEAD_W] = k_h.astype(BF16)
    v_ref[0, rs, :] = kv[:, MLA_W:].astype(BF16)


def _lane_mask(lo, hi, rows):
    lane = lax.broadcasted_iota(I32, (rows, LANES), 1)
    return (lane >= lo) & (lane < hi)


def _inproj_kernel(x_ref, mod_ref, g1_ref, win_ref, wgk_ref, bgk_ref, gql_ref, wuq_ref,
                   gkv_ref, wukv_ref, gq_ref, gk_ref, *refs, positions):
    tab_refs, outs = (refs[:3], refs[3:]) if positions else ((), refs)
    (q_ref, k_ref, v_ref, g_ref, lgf_ref, lgb_ref, qm_ref, km_ref, vm_ref, ckvn_ref,
     kr_ref) = outs
    mod = mod_ref[0]
    for r0 in range(0, x_ref.shape[1], IN_SUB):
        rs = slice(r0, r0 + IN_SUB)
        x = x_ref[0, rs, :]
        h = _rms(x, D_MODEL) * g1_ref[...] * (1.0 + mod[1:2]) + mod[0:1]
        y = _dot(h.astype(BF16), win_ref[...])
        q_ref[0, rs, :] = (y[:, 0:GLA_KEY_W] * GLA_DK ** -0.5).astype(BF16)
        k_ref[0, rs, :] = y[:, GLA_KEY_W:COL_V].astype(BF16)
        v_ref[0, rs, :] = y[:, COL_V:COL_G].astype(BF16)
        g_ref[0, rs, :] = y[:, COL_G:COL_CQ].astype(BF16)
        tail = y[:, TAIL0:IN_W]
        pre = _dot(tail.astype(BF16), wgk_ref[...]) + bgk_ref[...]
        logg = _log_sigmoid(pre) * (1.0 / GLA_GATE_NORM)
        lgf_ref[0, rs, :] = logg[:, 0:GLA_KEY_W]
        lgb_ref[0, rs, :] = logg[:, GLA_KEY_W:]
        tabs = tuple(r[rs, :] for r in tab_refs) if positions else None
        cq = _rms(y[:, COL_CQ:COL_CKV], Q_LORA) * gql_ref[...]
        qm = _dot(cq.astype(BF16), wuq_ref[...])
        gq = gq_ref[...]
        for hh in range(MLA_HEADS):
            q_h = _rms(qm[:, hh * HEAD_W:(hh + 1) * HEAD_W], QK_DIM) * gq
            if positions:
                q_h = _rope(q_h, *tabs)
            qm_ref[0, rs, hh * HEAD_W:(hh + 1) * HEAD_W] = (q_h * QK_DIM ** -0.5).astype(BF16)
        ckv_n = _rms(y[:, COL_CKV:TAIL0], KV_LORA) * gkv_ref[...]
        ckvn_ref[0, rs, :] = ckv_n
        kr_ref[0, rs, :] = tail
        kr = jnp.where(_lane_mask(KR_LANE0, KR_LANE0 + ROPE_DIM, IN_SUB), tail, 0.0)
        _mla_kv(ckv_n, kr, wukv_ref, gk_ref[...], tabs, km_ref, vm_ref, rs)


def _inproj(x, mod, per_batch_mod, p, rope_tabs, tm):
    b, l, d = x.shape
    nt = l // tm
    mod_map = (lambda bi, i: (bi, 0, 0)) if per_batch_mod else (lambda bi, i: (0, 0, 0))
    tab_map = lambda bi, i: (i, 0)
    const = lambda bi, i: (0, 0)
    tok = lambda w: pl.BlockSpec((1, tm, w), lambda bi, i: (bi, i, 0))
    full = lambda a: pl.BlockSpec(a.shape, const)
    weights = [p['g_norm1'], p['w_in_p'], p['w_gk_big'], p['b_gk'], p['g_q_lora'], p['w_uq_p'],
               p['g_kv_lora'], p['w_ukv_p'], p['gq'], p['gk']]
    outs = [(GLA_KEY_W, BF16), (GLA_KEY_W, BF16), (GLA_VAL_W, BF16), (GLA_VAL_W, BF16),
            (GLA_KEY_W, F32), (GLA_KEY_W, F32), (MLA_W, BF16), (MLA_W, BF16), (MLA_W, BF16),
            (KV_LORA, F32), (LANES, F32)]
    return pl.pallas_call(
        functools.partial(_inproj_kernel, positions=bool(rope_tabs)),
        grid=(b, nt),
        in_specs=[tok(d), pl.BlockSpec((1, 6, d), mod_map)] + [full(w) for w in weights]
                 + [pl.BlockSpec((tm, LANES), tab_map)] * len(rope_tabs),
        out_specs=[tok(w) for w, _ in outs],
        out_shape=[jax.ShapeDtypeStruct((b, l, w), dt) for w, dt in outs],
        compiler_params=pltpu.CompilerParams(
            dimension_semantics=("parallel", "parallel"), vmem_limit_bytes=VMEM_LIMIT),
        name="inproj",
    )(x, mod, *weights, *rope_tabs)


def _cache_kv_kernel(ckv_ref, kr_ref, wukv_ref, gk_ref, k_ref, v_ref):
    _mla_kv(ckv_ref[0], kr_ref[0], wukv_ref, gk_ref[...], None, k_ref, v_ref)


def _cache_kv(ckv, kr128, p):
    b, l, _ = ckv.shape
    tok = lambda w: pl.BlockSpec((1, l, w), lambda bi: (bi, 0, 0))
    full = lambda a: pl.BlockSpec(a.shape, lambda bi: (0, 0))
    return pl.pallas_call(
        _cache_kv_kernel,
        grid=(b,),
        in_specs=[tok(KV_LORA), tok(LANES), full(p['w_ukv_p']), full(p['gk'])],
        out_specs=[tok(MLA_W), tok(MLA_W)],
        out_shape=[jax.ShapeDtypeStruct((b, l, MLA_W), BF16)] * 2,
        name="cache_kv",
    )(ckv, kr128, p['w_ukv_p'], p['gk'])


def _split3(x):
    hi = x.astype(BF16)
    r1 = x - hi.astype(F32)
    mid = r1.astype(BF16)
    lo = (r1 - mid.astype(F32)).astype(BF16)
    return hi, mid, lo


def _gla_block(q, k, v, lg, s, fwd):
    n = GLA_BLOCK
    nc = n // GLA_SUB
    row = lax.broadcasted_iota(I32, (n, n), 0)
    col = lax.broadcasted_iota(I32, (n, n), 1)
    same = (row // GLA_SUB) == (col // GLA_SUB)
    causal = same & ((col <= row) if fwd else (col >= row))
    tri = causal.astype(BF16)
    hi, mid, lo = _split3(lg)
    cum = _dot(tri, hi) + _dot(tri, mid) + _dot(tri, lo)
    tot_rows, mid_rows = [], []
    for c in range(nc):
        r_tot = c * GLA_SUB + (GLA_SUB - 1 if fwd else 0)
        r_mid = c * GLA_SUB + GLA_SUB // 2
        tot_rows.append(jnp.broadcast_to(cum[r_tot:r_tot + 1], (GLA_SUB, GLA_KEY_W)))
        mid_rows.append(jnp.broadcast_to(cum[r_mid:r_mid + 1], (GLA_SUB, GLA_KEY_W)))
    tot_rows = jnp.concatenate(tot_rows, axis=0)
    mid_rows = jnp.concatenate(mid_rows, axis=0)
    rel = cum - mid_rows
    qi = q * jnp.exp(rel)
    ki = (k * jnp.exp(-rel)).astype(BF16)
    q_in = (q * jnp.exp(cum)).astype(BF16)
    k_up = k * jnp.exp(tot_rows - cum)
    dec = jnp.exp(tot_rows)
    k_up_t = k_up.T.astype(BF16)
    dec_t = dec.T
    lane_head = lax.broadcasted_iota(I32, (n, GLA_KEY_W), 1) // GLA_DK
    srow_head = lax.broadcasted_iota(I32, (GLA_KEY_W, GLA_DV), 0) // GLA_DK
    vrow_chunk = lax.broadcasted_iota(I32, (n, GLA_VAL_W), 0) // GLA_SUB
    o_heads = []
    for h in range(GLA_HEADS):
        a = _dot_nt(jnp.where(lane_head == h, qi, 0.0).astype(BF16), ki)
        a = jnp.where(causal, a, 0.0).astype(BF16)
        o_heads.append(_dot(a, v[:, h * GLA_DV:(h + 1) * GLA_DV]))
    o = jnp.concatenate(o_heads, axis=1)
    o_inter = [None] * nc
    for c in (range(nc) if fwd else range(nc - 1, -1, -1)):
        s_bd = jnp.concatenate(
            [jnp.where(srow_head == h, s, 0.0).astype(BF16) for h in range(GLA_HEADS)], axis=1)
        o_inter[c] = _dot(q_in[c * GLA_SUB:(c + 1) * GLA_SUB], s_bd)
        v_c = jnp.where(vrow_chunk == c, v, jnp.zeros_like(v))
        u = jnp.concatenate(
            [_dot(k_up_t[h * GLA_DK:(h + 1) * GLA_DK], v_c[:, h * GLA_DV:(h + 1) * GLA_DV])
             for h in range(GLA_HEADS)], axis=0)
        s = dec_t[:, c * GLA_SUB:c * GLA_SUB + 1] * s + u
    return o + jnp.concatenate(o_inter, axis=0), s


def _gla_kernel(q_ref, k_ref, v_ref, g_ref, lgf_ref, lgb_ref, s0f_ref, s0b_ref, gout_ref,
                o_ref, sf_ref, sb_ref, acc_ref, st_ref, *, nblk, n_seq):
    for si in range(n_seq):
        _gla_sequence(si, q_ref, k_ref, v_ref, g_ref, lgf_ref, lgb_ref, s0f_ref, s0b_ref,
                      gout_ref, o_ref, sf_ref, sb_ref, acc_ref, st_ref, nblk)


def _gla_sequence(si, q_ref, k_ref, v_ref, g_ref, lgf_ref, lgb_ref, s0f_ref, s0b_ref, gout_ref,
                  o_ref, sf_ref, sb_ref, acc_ref, st_ref, nblk):
    st_ref[si, 0] = s0f_ref[si]
    st_ref[si, 1] = s0b_ref[si]

    def load(blk):
        start = blk * GLA_BLOCK
        r = pl.ds(start if isinstance(blk, int) else pl.multiple_of(start, GLA_BLOCK), GLA_BLOCK)
        return r, q_ref[si, r, :].astype(F32), k_ref[si, r, :].astype(F32), v_ref[si, r, :]

    def pair_step(i, first_touch):
        rf, qf, kf, vf = load(i)
        rb, qb, kb, vb = load(nblk - 1 - i)
        of, s_f = _gla_block(qf, kf, vf, lgf_ref[si, rf, :], st_ref[si, 0], True)
        ob, s_b = _gla_block(qb, kb, vb, lgb_ref[si, rb, :], st_ref[si, 1], False)
        if first_touch:
            acc_ref[si, rf, :] = of
            acc_ref[si, rb, :] = ob
        else:
            acc_ref[si, rf, :] += of
            acc_ref[si, rb, :] += ob
        st_ref[si, 0] = s_f
        st_ref[si, 1] = s_b

    if nblk == 1:
        r, q, k, v = load(0)
        of, s_f = _gla_block(q, k, v, lgf_ref[si, r, :], st_ref[si, 0], True)
        ob, s_b = _gla_block(q, k, v, lgb_ref[si, r, :], st_ref[si, 1], False)
        acc_ref[si, r, :] = of + ob
        st_ref[si, 0] = s_f
        st_ref[si, 1] = s_b
    else:
        half = nblk // 2
        lax.fori_loop(0, half, lambda i, c: (pair_step(i, True), c)[1], 0)
        lax.fori_loop(half, nblk, lambda i, c: (pair_step(i, False), c)[1], 0)
    sf_ref[si] = st_ref[si, 0]
    sb_ref[si] = st_ref[si, 1]

    def fin_step(blk, carry):
        r = pl.ds(pl.multiple_of(blk * GLA_BLOCK, GLA_BLOCK), GLA_BLOCK)
        o = acc_ref[si, r, :]
        gate = _silu(g_ref[si, r, :].astype(F32))
        for h in range(GLA_HEADS):
            sl = slice(h * GLA_DV, (h + 1) * GLA_DV)
            o_ref[si, r, sl] = (_rms(o[:, sl], GLA_DV) * gout_ref[...] * gate[:, sl]).astype(BF16)
        return carry

    lax.fori_loop(0, nblk, fin_step, 0)


def _gla(q, k, v, g, lgf, lgb, s0f, s0b, g_out, n_seq):
    b, l, _ = q.shape
    assert l == GLA_BLOCK or l % (2 * GLA_BLOCK) == 0
    seq = lambda w: pl.BlockSpec((n_seq, l, w), lambda bi: (bi, 0, 0))
    st = pl.BlockSpec((n_seq, GLA_KEY_W, GLA_DV), lambda bi: (bi, 0, 0))
    return pl.pallas_call(
        functools.partial(_gla_kernel, nblk=l // GLA_BLOCK, n_seq=n_seq),
        grid=(b // n_seq,),
        in_specs=[seq(GLA_KEY_W), seq(GLA_KEY_W), seq(GLA_VAL_W), seq(GLA_VAL_W),
                  seq(GLA_KEY_W), seq(GLA_KEY_W), st, st,
                  pl.BlockSpec((1, GLA_DV), lambda bi: (0, 0))],
        out_specs=[seq(GLA_VAL_W), st, st],
        out_shape=[jax.ShapeDtypeStruct((b, l, GLA_VAL_W), BF16),
                   jax.ShapeDtypeStruct((b, GLA_KEY_W, GLA_DV), F32),
                   jax.ShapeDtypeStruct((b, GLA_KEY_W, GLA_DV), F32)],
        scratch_shapes=[pltpu.VMEM((n_seq, l, GLA_VAL_W), F32),
                        pltpu.VMEM((n_seq, 2, GLA_KEY_W, GLA_DV), F32)],
        compiler_params=pltpu.CompilerParams(
            dimension_semantics=("parallel",), vmem_limit_bytes=VMEM_LIMIT),
        name="gla",
    )(q, k, v, g, lgf, lgb, s0f, s0b, g_out)


def _attn_kernel(*refs, n_kv):
    q_ref, o_ref = refs[0], refs[-1]
    kv = [(refs[1 + 2 * i], refs[2 + 2 * i]) for i in range(n_kv)]
    for h in range(MLA_HEADS):
        sl = slice(h * HEAD_W, (h + 1) * HEAD_W)
        q = q_ref[0, :, sl]
        scores = [_dot_nt(q, k_ref[0, :, sl]) for k_ref, _ in kv]
        m = functools.reduce(jnp.maximum, [jnp.max(s, axis=-1, keepdims=True) for s in scores])
        o, den = 0.0, 0.0
        for s, (_, v_ref) in zip(scores, kv):
            p = jnp.exp(s - m)
            den = den + jnp.sum(p, axis=-1, keepdims=True)
            o = o + _dot(p.astype(BF16), v_ref[0, :, sl])
        o_ref[0, :, sl] = (o / den).astype(BF16)


def _attn(q, kv_pairs, tq):
    b, l, _ = q.shape
    kv_specs, kv_args = [], []
    for k, v in kv_pairs:
        spec = pl.BlockSpec((1, k.shape[1], MLA_W), lambda bi, i: (bi, 0, 0))
        kv_specs += [spec, spec]
        kv_args += [k, v]
    return pl.pallas_call(
        functools.partial(_attn_kernel, n_kv=len(kv_pairs)),
        grid=(b, l // tq),
        in_specs=[pl.BlockSpec((1, tq, MLA_W), lambda bi, i: (bi, i, 0))] + kv_specs,
        out_specs=pl.BlockSpec((1, tq, MLA_W), lambda bi, i: (bi, i, 0)),
        out_shape=jax.ShapeDtypeStruct((b, l, MLA_W), BF16),
        compiler_params=pltpu.CompilerParams(
            dimension_semantics=("parallel", "parallel"), vmem_limit_bytes=VMEM_LIMIT),
        name="attn",
    )(q, *kv_args)


def _expert_cap(n_tok):
    return -(-n_tok // FFN_TILE) * FFN_TILE


def _sum01(x):
    return jnp.sum(jnp.sum(x, axis=1, keepdims=True), axis=0, keepdims=True)


def _route(logits_t, bias_col, cnt_col, cap):
    t = logits_t.shape[1]
    gsz = N_EXPERTS // N_GROUPS
    scores = jax.nn.sigmoid(logits_t)
    sel = (scores + bias_col).reshape(N_GROUPS, gsz, t)
    scores = scores.reshape(N_GROUPS, gsz, t)
    neg = jnp.float32(-jnp.inf)
    ie = lax.broadcasted_iota(I32, (N_GROUPS, gsz, t), 1)
    ig = lax.broadcasted_iota(I32, (N_GROUPS, gsz, t), 0)
    m1 = jnp.max(sel, axis=1, keepdims=True)
    first = jnp.min(jnp.where(sel == m1, ie, gsz), axis=1, keepdims=True)
    m2 = jnp.max(jnp.where(ie == first, neg, sel), axis=1, keepdims=True)
    grp = m1 + m2
    igk = lax.broadcasted_iota(I32, (N_GROUPS, 1, t), 0)
    g_sel = jnp.zeros((N_GROUPS, 1, t), jnp.bool_)
    cur = grp
    for _ in range(TOPK_GROUPS):
        m = jnp.max(cur, axis=0, keepdims=True)
        pick = igk == jnp.min(jnp.where(cur == m, igk, N_GROUPS), axis=0, keepdims=True)
        g_sel = g_sel | pick
        cur = jnp.where(pick, neg, cur)
    cur = jnp.where(g_sel, sel, neg)
    idx = ig * gsz + ie
    e_sel = jnp.zeros((N_GROUPS, gsz, t), jnp.bool_)
    picks = []
    for _ in range(TOP_K):
        m = jnp.max(jnp.max(cur, axis=1, keepdims=True), axis=0, keepdims=True)
        cand = jnp.where(cur == m, idx, N_EXPERTS)
        pick = idx == jnp.min(jnp.min(cand, axis=1, keepdims=True), axis=0, keepdims=True)
        picks.append(pick)
        e_sel = e_sel | pick
        cur = jnp.where(pick, neg, cur)
    w = jnp.where(e_sel, scores, 0.0)
    gate = w / _sum01(w) * ROUTED_SCALE
    sel_f = e_sel.astype(F32).reshape(N_EXPERTS, t)
    earlier = (lax.broadcasted_iota(I32, (t, t), 0) < lax.broadcasted_iota(I32, (t, t), 1))
    rank = _dot(sel_f.astype(BF16), earlier.astype(BF16))
    base = lax.broadcasted_iota(I32, (N_EXPERTS, 1), 0).astype(F32) * float(cap) + cnt_col
    slot = (base + rank).reshape(N_GROUPS, gsz, t)
    slot8 = jnp.concatenate([_sum01(jnp.where(pk, slot, 0.0)).reshape(1, t) for pk in picks],
                            axis=0).astype(I32)
    w8 = jnp.concatenate([_sum01(jnp.where(pk, gate, 0.0)).reshape(1, t) for pk in picks], axis=0)
    return slot8, w8, cnt_col + jnp.sum(sel_f, axis=1, keepdims=True)


def _outproj_kernel(og_ref, om_ref, x_ref, mod_ref, wtop_ref, wbot_ref, g2_ref, wr_ref, br_ref,
                    x1_ref, h2pa_ref, h2pb_ref, slot_ref, w8_ref, cnt_ref, cnt_scr, *, cap):
    @pl.when((pl.program_id(0) == 0) & (pl.program_id(1) == 0))
    def _():
        cnt_scr[...] = jnp.zeros_like(cnt_scr)

    mod = mod_ref[0]
    cnt = cnt_scr[:, 0:1]
    for r0 in range(0, x_ref.shape[1], OUT_SUB):
        rs = slice(r0, r0 + OUT_SUB)
        mix = _dot(og_ref[0, rs, :], wtop_ref[...]) + _dot(om_ref[0, rs, :], wbot_ref[...])
        x1 = x_ref[0, rs, :] + mod[2:3] * mix
        x1_ref[0, rs, :] = x1
        h2 = _rms(x1, D_MODEL) * g2_ref[...] * (1.0 + mod[4:5]) + mod[3:4]
        h2pa_ref[0, rs, :], h2pb_ref[0, rs, :] = _pack_rows(h2)
        logits_t = _dot_nt(wr_ref[...], h2, precision=HIGHEST)
        slot8, w8, cnt = _route(logits_t, br_ref[...], cnt, cap)
        slot_ref[:, rs] = slot8
        wb = lax.bitcast_convert_type(w8.astype(BF16).astype(F32), U32)
        wb = wb | (wb >> 16)
        w8_ref[0, rs, :] = jnp.concatenate(
            [jnp.broadcast_to(wb[k:k + 1], (SC_LANES, OUT_SUB)) for k in range(TOP_K)], axis=0).T
    cnt_scr[...] = jnp.broadcast_to(cnt, cnt_scr.shape)
    cnt_ref[...] = cnt_scr[...]


def _outproj(og, om, x, mod, per_batch_mod, p, tm):
    b, l, d = x.shape
    nt = l // tm
    mod_map = (lambda bi, i: (bi, 0, 0)) if per_batch_mod else (lambda bi, i: (0, 0, 0))
    tok = lambda w: pl.BlockSpec((1, tm, w), lambda bi, i: (bi, i, 0))
    full = lambda a: pl.BlockSpec(a.shape, lambda bi, i: (0, 0))
    weights = [p['w_out_top'], p['w_out_bot'], p['g_norm2'], p['w_router_t'], p['b_router_col']]
    cnt_shape = (N_EXPERTS, LANES)
    return pl.pallas_call(
        functools.partial(_outproj_kernel, cap=_expert_cap(b * l)),
        grid=(b, nt),
        in_specs=[tok(GLA_VAL_W), tok(MLA_W), tok(d), pl.BlockSpec((1, 6, d), mod_map)]
                 + [full(w) for w in weights],
        out_specs=[tok(d), tok(PACK_W), tok(PACK_W),
                   pl.BlockSpec((TOP_K, tm), lambda bi, i: (0, bi * nt + i)),
                   tok(LANES), pl.BlockSpec(cnt_shape, lambda bi, i: (0, 0))],
        out_shape=[jax.ShapeDtypeStruct((b, l, d), F32), jax.ShapeDtypeStruct((b, l, PACK_W), U32),
                   jax.ShapeDtypeStruct((b, l, PACK_W), U32),
                   jax.ShapeDtypeStruct((TOP_K, b * l), I32),
                   jax.ShapeDtypeStruct((b, l, LANES), U32),
                   jax.ShapeDtypeStruct(cnt_shape, F32)],
        scratch_shapes=[pltpu.VMEM(cnt_shape, F32)],
        compiler_params=pltpu.CompilerParams(
            dimension_semantics=("arbitrary", "arbitrary"), vmem_limit_bytes=VMEM_LIMIT),
        name="outproj",
    )(og, om, x, mod, *weights)


def _sc_mesh():
    return plsc.VectorSubcoreMesh(core_axis_name="c", subcore_axis_name="s")


def _sc_scatter_rows(src, slot8, n_rows_out):
    t, d = src.shape
    nk = slot8.shape[0]

    @functools.partial(pl.kernel, out_type=jax.ShapeDtypeStruct((n_rows_out, d), src.dtype),
                       mesh=_sc_mesh(), scratch_types=[], name="dispatch")
    def run(src_hbm, slot_hbm, out_hbm):
        def body(x_vmem, i_vmem):
            for k in range(nk):
                pltpu.sync_copy(x_vmem, out_hbm.at[i_vmem.at[k]])

        pltpu.emit_pipeline(
            body, grid=(t // SC_WINDOW,),
            in_specs=[pl.BlockSpec((SC_WINDOW, d), lambda i: (i, 0)),
                      pl.BlockSpec((nk, SC_WINDOW), lambda i: (0, i))],
            out_specs=[], core_axis_name=("c", "s"),
            dimension_semantics=(pltpu.PARALLEL,))(src_hbm, slot_hbm)

    return run(src, slot8)


def _sc_collect_sum(table, slots_tk, w_rep):
    n_tok = w_rep.shape[0]
    d = table.shape[1]
    group = SC_WINDOW // TOP_K
    chunks = d // SC_LANES

    @functools.partial(pl.kernel, out_type=jax.ShapeDtypeStruct((n_tok, 2 * d), F32),
                       mesh=_sc_mesh(), scratch_types=[pltpu.VMEM((SC_WINDOW, d), table.dtype)],
                       compiler_params=pltpu.CompilerParams(needs_layout_passes=False),
                       name="collect")
    def run(tab_hbm, slot_hbm, w_hbm, out_hbm, rows_v):
        def body(i_vmem, w_vmem, o_vmem):
            pltpu.sync_copy(tab_hbm.at[i_vmem.at[0]], rows_v)

            @pl.loop(0, group)
            def _(j):
                wv = [plsc.bitcast(w_vmem[j, pl.ds(k * SC_LANES, SC_LANES)], BF16)
                      for k in range(TOP_K)]

                def chunk(off):
                    prods = [plsc.bitcast(rows_v[j * TOP_K + k, pl.ds(off, SC_LANES)], BF16)
                             * wv[k] for k in range(TOP_K)]
                    lo = jnp.zeros((SC_LANES,), F32)
                    hi = jnp.zeros((SC_LANES,), F32)
                    for k in range(0, TOP_K, 2):
                        u = plsc.bitcast(prods[k] + prods[k + 1], U32)
                        lo = lo + lax.bitcast_convert_type(u << 16, F32)
                        hi = hi + lax.bitcast_convert_type(u & jnp.uint32(0xFFFF0000), F32)
                    o_vmem[j, pl.ds(off, SC_LANES)] = lo
                    o_vmem[j, pl.ds(d + off, SC_LANES)] = hi

                @plsc.parallel_loop(0, chunks, unroll=2)
                def _(c):
                    chunk(c * SC_LANES)

        pltpu.emit_pipeline(
            body, grid=(n_tok // group,),
            in_specs=[pl.BlockSpec((1, SC_WINDOW), lambda i: (0, i)),
                      pl.BlockSpec((group, TOP_K * SC_LANES), lambda i: (i, 0))],
            out_specs=[pl.BlockSpec((group, 2 * d), lambda i: (i, 0))],
            core_axis_name=("c", "s"),
            dimension_semantics=(pltpu.PARALLEL,))(slot_hbm, w_hbm, out_hbm)

    return run(table, slots_tk, w_rep)


def _ffn_kernel(nt_ref, first_ref, te_ref, tt_ref, xsa_hbm, xsb_hbm, wg_ref, wu_ref, wd_ref,
                ysa_hbm, ysb_hbm, xbuf, ybuf, wg_b, wu_b, wd_b, sem_in, sem_out, *, cap):
    e = pl.program_id(0)
    n_exp = pl.num_programs(0)
    n = nt_ref[e]
    g0 = first_ref[e]
    total = first_ref[n_exp - 1] + nt_ref[n_exp - 1]
    xs_hbm = (xsa_hbm, xsb_hbm)
    ys_hbm = (ysa_hbm, ysb_hbm)
    n_in = FFN_AHEAD + 1

    def rows(ex, t):
        return pl.ds(pl.multiple_of(ex * cap + t * FFN_TILE, FFN_TILE), FFN_TILE)

    def in_copy(ex, t, slot, part):
        return pltpu.make_async_copy(xs_hbm[part].at[rows(ex, t)], xbuf.at[slot, part],
                                     sem_in.at[slot, part])

    def out_copy(ex, t, slot, part):
        return pltpu.make_async_copy(ybuf.at[slot, part], ys_hbm[part].at[rows(ex, t)],
                                     sem_out.at[slot, part])

    def start_in(g):
        for part in range(PACK_PARTS):
            in_copy(te_ref[g], tt_ref[g], g % n_in, part).start()

    def wait_out(slot):
        for part in range(PACK_PARTS):
            out_copy(e, 0, slot, part).wait()

    @pl.when(n > 0)
    def _():
        @pl.when(g0 == 0)
        def _():
            for ahead in range(FFN_AHEAD):
                @pl.when(ahead < total)
                def _():
                    start_in(ahead)

        wg_b[...] = wg_ref[0].astype(BF16)
        wu_b[...] = wu_ref[0].astype(BF16)
        wd_b[...] = wd_ref[0].astype(BF16)

        def tile(t, carry):
            g = g0 + t
            slot = g % 2
            slot_in = g % n_in
            for part in range(PACK_PARTS):
                in_copy(e, t, slot_in, part).wait()

            @pl.when(g + FFN_AHEAD < total)
            def _():
                start_in(g + FFN_AHEAD)

            @pl.when(g >= 2)
            def _():
                wait_out(slot)

            x = _unpack_rows([xbuf[slot_in, part] for part in range(PACK_PARTS)]).astype(BF16)
            a = _silu(_dot(x, wg_b[...])) * _dot(x, wu_b[...])
            y = _pack_rows(_dot(a.astype(BF16), wd_b[...]))
            for part in range(PACK_PARTS):
                ybuf[slot, part] = y[part]
                out_copy(e, t, slot, part).start()
            return carry

        lax.fori_loop(0, n, tile, 0)

        @pl.when(g0 + n == total)
        def _():
            @pl.when(total >= 2)
            def _():
                wait_out(total % 2)

            wait_out((total - 1) % 2)


def _ffn(xs_parts, counts, w_gate, w_up, w_down):
    n_exp, d = w_gate.shape[0], w_gate.shape[1]
    cap = xs_parts[0].shape[0] // n_exp
    ntiles = (counts + FFN_TILE - 1) // FFN_TILE
    cum = jnp.cumsum(ntiles)
    first = cum - ntiles
    max_tiles = (cap * TOP_K) // FFN_TILE + n_exp
    g = jnp.arange(max_tiles, dtype=I32)
    done = g[:, None] >= cum[None, :]
    tile_e = jnp.minimum(jnp.sum(done, axis=1), n_exp - 1).astype(I32)
    tile_t = (g - jnp.sum(jnp.where(done, ntiles[None, :], 0), axis=1)).astype(I32)
    wmap = lambda e, nt, first, te, tt: (e, 0, 0)
    hbm = pl.BlockSpec(memory_space=pl.ANY)
    return pl.pallas_call(
        functools.partial(_ffn_kernel, cap=cap),
        grid_spec=pltpu.PrefetchScalarGridSpec(
            num_scalar_prefetch=4, grid=(n_exp,),
            in_specs=[hbm] * PACK_PARTS
                     + [pl.BlockSpec((1, d, D_EXPERT), wmap), pl.BlockSpec((1, d, D_EXPERT), wmap),
                        pl.BlockSpec((1, D_EXPERT, d), wmap)],
            out_specs=[hbm] * PACK_PARTS,
            scratch_shapes=[pltpu.VMEM((FFN_AHEAD + 1, PACK_PARTS, FFN_TILE, PACK_W), U32),
                            pltpu.VMEM((2, PACK_PARTS, FFN_TILE, PACK_W), U32),
                            pltpu.VMEM((d, D_EXPERT), BF16), pltpu.VMEM((d, D_EXPERT), BF16),
                            pltpu.VMEM((D_EXPERT, d), BF16),
                            pltpu.SemaphoreType.DMA((FFN_AHEAD + 1, PACK_PARTS)),
                            pltpu.SemaphoreType.DMA((2, PACK_PARTS))]),
        out_shape=[jax.ShapeDtypeStruct(xs_parts[0].shape, U32)] * PACK_PARTS,
        compiler_params=pltpu.CompilerParams(
            dimension_semantics=("arbitrary",), vmem_limit_bytes=VMEM_LIMIT),
        name="ffn",
    )(ntiles.astype(I32), first.astype(I32), tile_e, tile_t, *xs_parts, w_gate, w_up, w_down)


def _combine_kernel(x1_ref, h2pa_ref, h2pb_ref, ra_ref, rb_ref, mod_ref,
                    wsg_ref, wsu_ref, wsd_ref, o_ref):
    h2 = _unpack_rows([h2pa_ref[0], h2pb_ref[0]]).astype(BF16)
    a = _silu(_dot(h2, wsg_ref[...])) * _dot(h2, wsu_ref[...])
    routed = jnp.concatenate([ra_ref[0], rb_ref[0]], axis=1)
    o_ref[0] = x1_ref[0] + mod_ref[0][5:6] * (_dot(a.astype(BF16), wsd_ref[...]) + routed)


def _combine(x1, h2p_parts, routed_parts, mod, per_batch_mod, p, tm):
    b, l, d = x1.shape
    mod_map = (lambda bi, i: (bi, 0, 0)) if per_batch_mod else (lambda bi, i: (0, 0, 0))
    tok = lambda w: pl.BlockSpec((1, tm, w), lambda bi, i: (bi, i, 0))
    full = lambda a: pl.BlockSpec(a.shape, lambda bi, i: (0, 0))
    weights = [p['w_sh_gate'], p['w_sh_up'], p['w_sh_down']]
    routed_parts = [r.reshape(b, l, d // PACK_PARTS) for r in routed_parts]
    return pl.pallas_call(
        _combine_kernel,
        grid=(b, l // tm),
        in_specs=[tok(d)] + [tok(PACK_W)] * PACK_PARTS + [tok(d // PACK_PARTS)] * PACK_PARTS
                 + [pl.BlockSpec((1, 6, d), mod_map)] + [full(w) for w in weights],
        out_specs=tok(d),
        out_shape=jax.ShapeDtypeStruct((b, l, d), F32),
        compiler_params=pltpu.CompilerParams(
            dimension_semantics=("parallel", "parallel"), vmem_limit_bytes=VMEM_LIMIT),
        name="combine",
    )(x1, *h2p_parts, *routed_parts, mod, *weights)


def _pad_heads(w, parts):
    k = w.shape[0]
    per = w.shape[1] // MLA_HEADS
    w = w.reshape(k, MLA_HEADS, per)[:, :, parts[0]:parts[1]]
    w = jnp.pad(w, ((0, 0), (0, 0), (0, HEAD_W - (parts[1] - parts[0]))))
    return w.reshape(k, MLA_HEADS * HEAD_W)


def _prep_params(l, g_norm1, g_norm2, w_in, w_gk_fwd, b_gk_fwd, w_gk_bwd, b_gk_bwd, g_gla_out,
                 g_q_lora, w_uq, g_kv_lora, w_ukv, g_qk_q, g_qk_k, w_out, w_router, b_router,
                 w_sh_gate, w_sh_up, w_sh_down):
    w = w_in[l]
    d = w.shape[0]
    o_lrf = COL_CQ
    o_lrb = o_lrf + GLA_GATE_RANK
    o_cq = o_lrb + GLA_GATE_RANK
    o_ckv = o_cq + Q_LORA
    o_kr = o_ckv + KV_LORA
    w_in_p = jnp.concatenate([
        w[:, 0:COL_CQ], w[:, o_cq:o_ckv], w[:, o_ckv:o_kr], jnp.zeros((d, KR_LANE0), w.dtype),
        w[:, o_kr:o_kr + ROPE_DIM], w[:, o_lrf:o_lrb], w[:, o_lrb:o_cq]], axis=1).astype(BF16)
    z = jnp.zeros((GLA_GATE_RANK, GLA_KEY_W), F32)
    w_gk_big = jnp.concatenate([
        jnp.zeros((LANES - 2 * GLA_GATE_RANK, 2 * GLA_KEY_W), F32),
        jnp.concatenate([w_gk_fwd[l], z], axis=1),
        jnp.concatenate([z, w_gk_bwd[l]], axis=1)], axis=0).astype(BF16)
    pad_gain = lambda g: jnp.pad(g, (0, HEAD_W - QK_DIM)).reshape(1, HEAD_W)
    w_ukv_h = w_ukv[l]
    return {
        'g_norm1': g_norm1[l].reshape(1, d), 'g_norm2': g_norm2[l].reshape(1, d),
        'w_in_p': w_in_p, 'w_gk_big': w_gk_big,
        'b_gk': jnp.concatenate([b_gk_fwd[l], b_gk_bwd[l]]).reshape(1, 2 * GLA_KEY_W),
        'g_gla_out': g_gla_out[l].reshape(1, GLA_DV),
        'g_q_lora': g_q_lora[l].reshape(1, Q_LORA),
        'w_uq_p': _pad_heads(w_uq[l], (0, QK_DIM)).astype(BF16),
        'g_kv_lora': g_kv_lora[l].reshape(1, KV_LORA),
        'w_ukv_p': jnp.concatenate([_pad_heads(w_ukv_h, (0, NOPE_DIM)),
                                    _pad_heads(w_ukv_h, (NOPE_DIM, NOPE_DIM + V_DIM))],
                                   axis=1).astype(BF16),
        'gq': pad_gain(g_qk_q[l]), 'gk': pad_gain(g_qk_k[l]),
        'w_out_top': w_out[l][:GLA_VAL_W].astype(BF16),
        'w_out_bot': w_out[l][GLA_VAL_W:].astype(BF16),
        'w_router_t': w_router[l].T, 'b_router_col': b_router[l].reshape(N_EXPERTS, 1),
        'w_sh_gate': w_sh_gate[l].astype(BF16), 'w_sh_up': w_sh_up[l].astype(BF16),
        'w_sh_down': w_sh_down[l].astype(BF16),
    }


def _rope_tables(length):
    pos = np.arange(length)
    r = (pos // GRID_W).astype(np.float32)
    c = (pos % GRID_W).astype(np.float32)
    half = ROPE_DIM // 4
    inv_freq = np.float32(ROPE_THETA) ** (-np.arange(half, dtype=np.float32) / np.float32(half))
    ang_r = r[:, None] * inv_freq[None, :]
    ang_c = c[:, None] * inv_freq[None, :]
    zeros = lambda w: np.zeros((length, w), np.float32)
    ones = lambda w: np.ones((length, w), np.float32)
    tail_w = HEAD_W - ROPE_LANE0 - ROPE_DIM
    cos = np.concatenate([ones(ROPE_LANE0), np.cos(ang_r), np.cos(ang_r), np.cos(ang_c),
                          np.cos(ang_c), ones(tail_w)], axis=1)
    s1 = np.concatenate([zeros(ROPE_LANE0), -np.sin(ang_r), zeros(half), -np.sin(ang_c),
                         zeros(half), zeros(tail_w)], axis=1)
    s2 = np.concatenate([zeros(ROPE_LANE0), zeros(half), np.sin(ang_r), zeros(half),
                         np.sin(ang_c), zeros(tail_w)], axis=1)
    return tuple(jnp.asarray(t, F32) for t in (cos, s1, s2))


def _mix_route_dispatch(x_tok, seq_shape, mod, per_batch_mod, p, s0f, s0b, rope_tabs, ctx_kv,
                        tq):
    bt, lt, d = x_tok.shape
    b, l = seq_shape
    n_tok = bt * lt
    (q, k, v, g, lgf, lgb, qm, km, vm, ckvn, kr) = _inproj(x_tok, mod, per_batch_mod, p,
                                                            rope_tabs, IN_TILE)
    seq = lambda a: a.reshape(b, l, a.shape[-1])
    og, sf, sb = _gla(seq(q), seq(k), seq(v), seq(g), seq(lgf), seq(lgb), s0f, s0b,
                      p['g_gla_out'], GLA_SEQS if l == GLA_BLOCK else 1)
    kv_pairs = [(seq(km), seq(vm))] + ([ctx_kv] if ctx_kv is not None else [])
    om = _attn(seq(qm), kv_pairs, tq)
    tokv = lambda a: a.reshape(bt, lt, a.shape[-1])
    x1, h2a, h2b, slot8, w8, cnt = _outproj(tokv(og), tokv(om), x_tok, mod, per_batch_mod, p,
                                            OUT_TILE)
    x_sorted = [_sc_scatter_rows(h.reshape(n_tok, PACK_W), slot8,
                                 N_EXPERTS * _expert_cap(n_tok)) for h in (h2a, h2b)]
    routed = dict(x1=x1, h2=(h2a, h2b), slot8=slot8, w8=w8, cnt=cnt, x_sorted=x_sorted)
    return routed, sf, sb, ckvn, kr


def _experts_collect(r, w_gate, w_up, w_down):
    n_tok = r['slot8'].shape[1]
    y_sorted = _ffn(r['x_sorted'], r['cnt'][:, 0].astype(I32), w_gate, w_up, w_down)
    slots_tk = r['slot8'].T.reshape(1, TOP_K * n_tok)
    w_rep = r['w8'].reshape(n_tok, LANES)
    return [_sc_collect_sum(y, slots_tk, w_rep) for y in y_sorted]


def kernel(x_prompt, x_sample, c, state_gla_fwd, state_gla_bwd, cache_mla_ckv, cache_mla_krope,
           c_ctx, w_ada, b_ada, g_norm1, g_norm2, w_in, w_gk_fwd, b_gk_fwd, w_gk_bwd, b_gk_bwd,
           g_gla_out, g_q_lora, w_uq, g_kv_lora, w_ukv, g_qk_q, g_qk_k, w_out,
           w_router, b_router, w_exp_gate, w_exp_up, w_exp_down, w_sh_gate, w_sh_up, w_sh_down):
    bp, lp, d = x_prompt.shape
    bs, ls, _ = x_sample.shape
    depth = w_ada.shape[0]
    xp = x_prompt.reshape(1, bp * lp, d)
    xs = x_sample
    new_f, new_b, new_ckv, new_kr = [], [], [], []
    lat_tabs = _rope_tables(ls)
    cvecs = jnp.concatenate([c_ctx[None], c, jnp.zeros((8 - 1 - bs, d), F32)], axis=0)
    for l in range(depth):
        p = _prep_params(l, g_norm1, g_norm2, w_in, w_gk_fwd, b_gk_fwd, w_gk_bwd, b_gk_bwd,
                         g_gla_out, g_q_lora, w_uq, g_kv_lora, w_ukv, g_qk_q, g_qk_k, w_out,
                         w_router, b_router, w_sh_gate, w_sh_up, w_sh_down)
        mod = _ada(cvecs, w_ada[l], b_ada[l]).reshape(8, 6, d)
        mod_p, mod_s = mod[0:1], mod[1:1 + bs]
        zeros = jnp.zeros((bp, GLA_KEY_W, GLA_DV), F32)
        r_ctx, sf, sb, ckvn, kr = _mix_route_dispatch(
            xp, (bp, lp), mod_p, False, p, zeros, zeros, (), None, lp)
        new_f.append(sf.reshape(bp, GLA_HEADS, GLA_DK, GLA_DV))
        new_b.append(sb.reshape(bp, GLA_HEADS, GLA_DK, GLA_DV))
        new_ckv.append(ckvn.reshape(bp, lp, KV_LORA))
        new_kr.append(kr.reshape(bp, lp, LANES)[:, :, KR_LANE0:KR_LANE0 + ROPE_DIM])
        kr_cache = jnp.pad(cache_mla_krope[:, l],
                           ((0, 0), (0, 0), (ROPE_LANE0, LANES - ROPE_LANE0 - ROPE_DIM)))
        ctx_kv = _cache_kv(cache_mla_ckv[:, l], kr_cache, p)
        s0f = state_gla_fwd[:, l].reshape(bs, GLA_KEY_W, GLA_DV)
        s0b = state_gla_bwd[:, l].reshape(bs, GLA_KEY_W, GLA_DV)
        r_lat, _, _, _, _ = _mix_route_dispatch(
            xs, (bs, ls), mod_s, True, p, s0f, s0b, lat_tabs, ctx_kv, 256)
        experts = (w_exp_gate[l], w_exp_up[l], w_exp_down[l])
        routed_ctx = _experts_collect(r_ctx, *experts)
        routed_lat = _experts_collect(r_lat, *experts)
        xp = _combine(r_ctx['x1'], r_ctx['h2'], routed_ctx, mod_p, False, p, OUT_TILE)
        xs = _combine(r_lat['x1'], r_lat['h2'], routed_lat, mod_s, True, p, OUT_TILE)
    return (xp.reshape(bp, lp, d), xs, jnp.stack(new_f, axis=1), jnp.stack(new_b, axis=1),
            jnp.stack(new_ckv, axis=1), jnp.stack(new_kr, axis=1))
```

```python
import functools

import jax
import jax.numpy as jnp
import numpy as np
from jax import lax
from jax.experimental import pallas as pl
from jax.experimental.pallas import tpu as pltpu
from jax.experimental.pallas import tpu_sc as plsc

F32 = jnp.float32
BF16 = jnp.bfloat16
I32 = jnp.int32
U32 = jnp.uint32

D_MODEL = 1024
EPS = 1e-6
GRID_W = 64
GLA_HEADS = 4
GLA_DK = 64
GLA_DV = 128
GLA_GATE_RANK = 16
GLA_GATE_NORM = 16.0
GLA_KEY_W = GLA_HEADS * GLA_DK
GLA_VAL_W = GLA_HEADS * GLA_DV
MLA_HEADS = 4
Q_LORA = 256
KV_LORA = 128
NOPE_DIM = 64
ROPE_DIM = 32
V_DIM = 128
QK_DIM = NOPE_DIM + ROPE_DIM
ROPE_THETA = 10000.0
N_EXPERTS = 64
TOP_K = 8
N_GROUPS = 8
TOPK_GROUPS = 4
D_EXPERT = 256
ROUTED_SCALE = 2.5

LANES = 128
HEAD_W = LANES
MLA_W = MLA_HEADS * HEAD_W
ROPE_LANE0 = NOPE_DIM
COL_V = 2 * GLA_KEY_W
COL_G = COL_V + GLA_VAL_W
COL_CQ = COL_G + GLA_VAL_W
COL_CKV = COL_CQ + Q_LORA
TAIL0 = COL_CKV + KV_LORA
IN_W = TAIL0 + LANES
KR_LANE0 = LANES - ROPE_DIM - 2 * GLA_GATE_RANK
GLA_BLOCK = 256
GLA_SUB = 64
GLA_SEQS = 2
IN_TILE = 512
IN_SUB = 256
OUT_TILE = 1024
OUT_SUB = 1024
FFN_TILE = 512
FFN_AHEAD = 5
PACK_PARTS = 2
PACK_W = D_MODEL // (2 * PACK_PARTS)
SC_WINDOW = 128
SC_LANES = 16
VMEM_LIMIT = 56 * 1024 * 1024

HIGHEST = lax.Precision.HIGHEST


def _dot(a, b, precision=None):
    return jnp.dot(a, b, preferred_element_type=F32, precision=precision)


def _dot_nt(a, b, precision=None):
    return lax.dot_general(a, b, (((1,), (1,)), ((), ())), preferred_element_type=F32,
                           precision=precision)


def _rms(x, width):
    ss = jnp.sum(x * x, axis=-1, keepdims=True) * (1.0 / width)
    return x * lax.rsqrt(ss + EPS)


def _silu(x):
    return x * jax.nn.sigmoid(x)


def _log_sigmoid(x):
    return jnp.minimum(x, 0.0) - jnp.log1p(jnp.exp(-jnp.abs(x)))


def _pack_rows(x):
    parts = []
    for i in range(PACK_PARTS):
        c0 = i * 2 * PACK_W
        lo = lax.bitcast_convert_type(x[:, c0:c0 + PACK_W].astype(BF16).astype(F32), U32)
        hi = lax.bitcast_convert_type(
            x[:, c0 + PACK_W:c0 + 2 * PACK_W].astype(BF16).astype(F32), U32)
        parts.append(hi | (lo >> 16))
    return parts


def _unpack_rows(parts):
    cols = []
    for w in parts:
        cols.append(lax.bitcast_convert_type(w << 16, F32))
        cols.append(lax.bitcast_convert_type(w & jnp.uint32(0xFFFF0000), F32))
    return jnp.concatenate(cols, axis=1)


def _ada_kernel(c_ref, w_ref, b_ref, o_ref):
    o_ref[...] = _dot(_silu(c_ref[...]), w_ref[...], precision=HIGHEST) + b_ref[...]


def _ada(cvecs, w_ada, b_ada):
    n = w_ada.shape[1]
    tn = 768
    return pl.pallas_call(
        _ada_kernel,
        grid=(n // tn,),
        in_specs=[pl.BlockSpec((8, D_MODEL), lambda j: (0, 0)),
                  pl.BlockSpec((D_MODEL, tn), lambda j: (0, j)),
                  pl.BlockSpec((1, tn), lambda j: (0, j))],
        out_specs=pl.BlockSpec((8, tn), lambda j: (0, j)),
        out_shape=jax.ShapeDtypeStruct((8, n), F32),
        name="ada",
    )(cvecs, w_ada, b_ada.reshape(1, n))


def _rope(x, c, s1, s2):
    return x * c + pltpu.roll(x, LANES - 8, 1) * s1 + pltpu.roll(x, 8, 1) * s2


def _mla_kv(ckv_n, kr, w_ukv_ref, gk, rope_tabs, k_ref, v_ref, rs=slice(None)):
    kv = _dot(ckv_n.astype(BF16), w_ukv_ref[...])
    for h in range(MLA_HEADS):
        k_h = kv[:, h * HEAD_W:(h + 1) * HEAD_W] + kr
        k_h = _rms(k_h, QK_DIM) * gk
        if rope_tabs is not None:
            k_h = _rope(k_h, *rope_tabs)
        k_ref[0, rs, h * HEAD_W:(h + 1) * HEAD_W] = k_h.astype(BF16)
    v_ref[0, rs, :] = kv[:, MLA_W:].astype(BF16)


def _lane_mask(lo, hi, rows):
    lane = lax.broadcasted_iota(I32, (rows, LANES), 1)
    return (lane >= lo) & (lane < hi)


def _inproj_kernel(x_ref, mod_ref, g1_ref, win_ref, wgk_ref, bgk_ref, gql_ref, wuq_ref,
                   gkv_ref, wukv_ref, gq_ref, gk_ref, *refs, positions):
    tab_refs, outs = (refs[:3], refs[3:]) if positions else ((), refs)
    (q_ref, k_ref, v_ref, g_ref, lgf_ref, lgb_ref, qm_ref, km_ref, vm_ref, ckvn_ref,
     kr_ref) = outs
    mod = mod_ref[0]
    for r0 in range(0, x_ref.shape[1], IN_SUB):
        rs = slice(r0, r0 + IN_SUB)
        x = x_ref[0, rs, :]
        h = _rms(x, D_MODEL) * g1_ref[...] * (1.0 + mod[1:2]) + mod[0:1]
        y = _dot(h.astype(BF16), win_ref[...])
        q_ref[0, rs, :] = (y[:, 0:GLA_KEY_W] * GLA_DK ** -0.5).astype(BF16)
        k_ref[0, rs, :] = y[:, GLA_KEY_W:COL_V].astype(BF16)
        v_ref[0, rs, :] = y[:, COL_V:COL_G].astype(BF16)
        g_ref[0, rs, :] = y[:, COL_G:COL_CQ].astype(BF16)
        tail = y[:, TAIL0:IN_W]
        pre = _dot(tail.astype(BF16), wgk_ref[...]) + bgk_ref[...]
        logg = _log_sigmoid(pre) * (1.0 / GLA_GATE_NORM)
        lgf_ref[0, rs, :] = logg[:, 0:GLA_KEY_W]
        lgb_ref[0, rs, :] = logg[:, GLA_KEY_W:]
        tabs = tuple(r[rs, :] for r in tab_refs) if positions else None
        cq = _rms(y[:, COL_CQ:COL_CKV], Q_LORA) * gql_ref[...]
        qm = _dot(cq.astype(BF16), wuq_ref[...])
        gq = gq_ref[...]
        for hh in range(MLA_HEADS):
            q_h = _rms(qm[:, hh * HEAD_W:(hh + 1) * HEAD_W], QK_DIM) * gq
            if positions:
                q_h = _rope(q_h, *tabs)
            qm_ref[0, rs, hh * HEAD_W:(hh + 1) * HEAD_W] = (q_h * QK_DIM ** -0.5).astype(BF16)
        ckv_n = _rms(y[:, COL_CKV:TAIL0], KV_LORA) * gkv_ref[...]
        ckvn_ref[0, rs, :] = ckv_n
        kr_ref[0, rs, :] = tail
        kr = jnp.where(_lane_mask(KR_LANE0, KR_LANE0 + ROPE_DIM, IN_SUB), tail, 0.0)
        _mla_kv(ckv_n, kr, wukv_ref, gk_ref[...], tabs, km_ref, vm_ref, rs)


def _inproj(x, mod, per_batch_mod, p, rope_tabs, tm):
    b, l, d = x.shape
    nt = l // tm
    mod_map = (lambda bi, i: (bi, 0, 0)) if per_batch_mod else (lambda bi, i: (0, 0, 0))
    tab_map = lambda bi, i: (i, 0)
    const = lambda bi, i: (0, 0)
    tok = lambda w: pl.BlockSpec((1, tm, w), lambda bi, i: (bi, i, 0))
    full = lambda a: pl.BlockSpec(a.shape, const)
    weights = [p['g_norm1'], p['w_in_p'], p['w_gk_big'], p['b_gk'], p['g_q_lora'], p['w_uq_p'],
               p['g_kv_lora'], p['w_ukv_p'], p['gq'], p['gk']]
    outs = [(GLA_KEY_W, BF16), (GLA_KEY_W, BF16), (GLA_VAL_W, BF16), (GLA_VAL_W, BF16),
            (GLA_KEY_W, F32), (GLA_KEY_W, F32), (MLA_W, BF16), (MLA_W, BF16), (MLA_W, BF16),
            (KV_LORA, F32), (LANES, F32)]
    return pl.pallas_call(
        functools.partial(_inproj_kernel, positions=bool(rope_tabs)),
        grid=(b, nt),
        in_specs=[tok(d), pl.BlockSpec((1, 6, d), mod_map)] + [full(w) for w in weights]
                 + [pl.BlockSpec((tm, LANES), tab_map)] * len(rope_tabs),
        out_specs=[tok(w) for w, _ in outs],
        out_shape=[jax.ShapeDtypeStruct((b, l, w), dt) for w, dt in outs],
        compiler_params=pltpu.CompilerParams(
            dimension_semantics=("parallel", "parallel"), vmem_limit_bytes=VMEM_LIMIT),
        name="inproj",
    )(x, mod, *weights, *rope_tabs)


def _cache_kv_kernel(ckv_ref, kr_ref, wukv_ref, gk_ref, k_ref, v_ref):
    _mla_kv(ckv_ref[0], kr_ref[0], wukv_ref, gk_ref[...], None, k_ref, v_ref)


def _cache_kv(ckv, kr128, p):
    b, l, _ = ckv.shape
    tok = lambda w: pl.BlockSpec((1, l, w), lambda bi: (bi, 0, 0))
    full = lambda a: pl.BlockSpec(a.shape, lambda bi: (0, 0))
    return pl.pallas_call(
        _cache_kv_kernel,
        grid=(b,),
        in_specs=[tok(KV_LORA), tok(LANES), full(p['w_ukv_p']), full(p['gk'])],
        out_specs=[tok(MLA_W), tok(MLA_W)],
        out_shape=[jax.ShapeDtypeStruct((b, l, MLA_W), BF16)] * 2,
        name="cache_kv",
    )(ckv, kr128, p['w_ukv_p'], p['gk'])


def _split3(x):
    hi = x.astype(BF16)
    r1 = x - hi.astype(F32)
    mid = r1.astype(BF16)
    lo = (r1 - mid.astype(F32)).astype(BF16)
    return hi, mid, lo


def _gla_block(q, k, v, lg, s, fwd):
    n = GLA_BLOCK
    nc = n // GLA_SUB
    row = lax.broadcasted_iota(I32, (n, n), 0)
    col = lax.broadcasted_iota(I32, (n, n), 1)
    same = (row // GLA_SUB) == (col // GLA_SUB)
    causal = same & ((col <= row) if fwd else (col >= row))
    tri = causal.astype(BF16)
    hi, mid, lo = _split3(lg)
    cum = _dot(tri, hi) + _dot(tri, mid) + _dot(tri, lo)
    tot_rows, mid_rows = [], []
    for c in range(nc):
        r_tot = c * GLA_SUB + (GLA_SUB - 1 if fwd else 0)
        r_mid = c * GLA_SUB + GLA_SUB // 2
        tot_rows.append(jnp.broadcast_to(cum[r_tot:r_tot + 1], (GLA_SUB, GLA_KEY_W)))
        mid_rows.append(jnp.broadcast_to(cum[r_mid:r_mid + 1], (GLA_SUB, GLA_KEY_W)))
    tot_rows = jnp.concatenate(tot_rows, axis=0)
    mid_rows = jnp.concatenate(mid_rows, axis=0)
    rel = cum - mid_rows
    qi = q * jnp.exp(rel)
    ki = (k * jnp.exp(-rel)).astype(BF16)
    q_in = (q * jnp.exp(cum)).astype(BF16)
    k_up = k * jnp.exp(tot_rows - cum)
    dec = jnp.exp(tot_rows)
    k_up_t = k_up.T.astype(BF16)
    dec_t = dec.T
    lane_head = lax.broadcasted_iota(I32, (n, GLA_KEY_W), 1) // GLA_DK
    srow_head = lax.broadcasted_iota(I32, (GLA_KEY_W, GLA_DV), 0) // GLA_DK
    vrow_chunk = lax.broadcasted_iota(I32, (n, GLA_VAL_W), 0) // GLA_SUB
    o_heads = []
    for h in range(GLA_HEADS):
        a = _dot_nt(jnp.where(lane_head == h, qi, 0.0).astype(BF16), ki)
        a = jnp.where(causal, a, 0.0).astype(BF16)
        o_heads.append(_dot(a, v[:, h * GLA_DV:(h + 1) * GLA_DV]))
    o = jnp.concatenate(o_heads, axis=1)
    o_inter = [None] * nc
    for c in (range(nc) if fwd else range(nc - 1, -1, -1)):
        s_bd = jnp.concatenate(
            [jnp.where(srow_head == h, s, 0.0).astype(BF16) for h in range(GLA_HEADS)], axis=1)
        o_inter[c] = _dot(q_in[c * GLA_SUB:(c + 1) * GLA_SUB], s_bd)
        v_c = jnp.where(vrow_chunk == c, v, jnp.zeros_like(v))
        u = jnp.concatenate(
            [_dot(k_up_t[h * GLA_DK:(h + 1) * GLA_DK], v_c[:, h * GLA_DV:(h + 1) * GLA_DV])
             for h in range(GLA_HEADS)], axis=0)
        s = dec_t[:, c * GLA_SUB:c * GLA_SUB + 1] * s + u
    return o + jnp.concatenate(o_inter, axis=0), s


def _gla_kernel(q_ref, k_ref, v_ref, g_ref, lgf_ref, lgb_ref, s0f_ref, s0b_ref, gout_ref,
                o_ref, sf_ref, sb_ref, acc_ref, st_ref, *, nblk, n_seq):
    for si in range(n_seq):
        _gla_sequence(si, q_ref, k_ref, v_ref, g_ref, lgf_ref, lgb_ref, s0f_ref, s0b_ref,
                      gout_ref, o_ref, sf_ref, sb_ref, acc_ref, st_ref, nblk)


def _gla_sequence(si, q_ref, k_ref, v_ref, g_ref, lgf_ref, lgb_ref, s0f_ref, s0b_ref, gout_ref,
                  o_ref, sf_ref, sb_ref, acc_ref, st_ref, nblk):
    st_ref[si, 0] = s0f_ref[si]
    st_ref[si, 1] = s0b_ref[si]

    def load(blk):
        start = blk * GLA_BLOCK
        r = pl.ds(start if isinstance(blk, int) else pl.multiple_of(start, GLA_BLOCK), GLA_BLOCK)
        return r, q_ref[si, r, :].astype(F32), k_ref[si, r, :].astype(F32), v_ref[si, r, :]

    def pair_step(i, first_touch):
        rf, qf, kf, vf = load(i)
        rb, qb, kb, vb = load(nblk - 1 - i)
        of, s_f = _gla_block(qf, kf, vf, lgf_ref[si, rf, :], st_ref[si, 0], True)
        ob, s_b = _gla_block(qb, kb, vb, lgb_ref[si, rb, :], st_ref[si, 1], False)
        if first_touch:
            acc_ref[si, rf, :] = of
            acc_ref[si, rb, :] = ob
        else:
            acc_ref[si, rf, :] += of
            acc_ref[si, rb, :] += ob
        st_ref[si, 0] = s_f
        st_ref[si, 1] = s_b

    if nblk == 1:
        r, q, k, v = load(0)
        of, s_f = _gla_block(q, k, v, lgf_ref[si, r, :], st_ref[si, 0], True)
        ob, s_b = _gla_block(q, k, v, lgb_ref[si, r, :], st_ref[si, 1], False)
        acc_ref[si, r, :] = of + ob
        st_ref[si, 0] = s_f
        st_ref[si, 1] = s_b
    else:
        half = nblk // 2
        lax.fori_loop(0, half, lambda i, c: (pair_step(i, True), c)[1], 0)
        lax.fori_loop(half, nblk, lambda i, c: (pair_step(i, False), c)[1], 0)
    sf_ref[si] = st_ref[si, 0]
    sb_ref[si] = st_ref[si, 1]

    def fin_step(blk, carry):
        r = pl.ds(pl.multiple_of(blk * GLA_BLOCK, GLA_BLOCK), GLA_BLOCK)
        o = acc_ref[si, r, :]
        gate = _silu(g_ref[si, r, :].astype(F32))
        for h in range(GLA_HEADS):
            sl = slice(h * GLA_DV, (h + 1) * GLA_DV)
            o_ref[si, r, sl] = (_rms(o[:, sl], GLA_DV) * gout_ref[...] * gate[:, sl]).astype(BF16)
        return carry

    lax.fori_loop(0, nblk, fin_step, 0)


def _gla(q, k, v, g, lgf, lgb, s0f, s0b, g_out, n_seq):
    b, l, _ = q.shape
    assert l == GLA_BLOCK or l % (2 * GLA_BLOCK) == 0
    seq = lambda w: pl.BlockSpec((n_seq, l, w), lambda bi: (bi, 0, 0))
    st = pl.BlockSpec((n_seq, GLA_KEY_W, GLA_DV), lambda bi: (bi, 0, 0))
    return pl.pallas_call(
        functools.partial(_gla_kernel, nblk=l // GLA_BLOCK, n_seq=n_seq),
        grid=(b // n_seq,),
        in_specs=[seq(GLA_KEY_W), seq(GLA_KEY_W), seq(GLA_VAL_W), seq(GLA_VAL_W),
                  seq(GLA_KEY_W), seq(GLA_KEY_W), st, st,
                  pl.BlockSpec((1, GLA_DV), lambda bi: (0, 0))],
        out_specs=[seq(GLA_VAL_W), st, st],
        out_shape=[jax.ShapeDtypeStruct((b, l, GLA_VAL_W), BF16),
                   jax.ShapeDtypeStruct((b, GLA_KEY_W, GLA_DV), F32),
                   jax.ShapeDtypeStruct((b, GLA_KEY_W, GLA_DV), F32)],
        scratch_shapes=[pltpu.VMEM((n_seq, l, GLA_VAL_W), F32),
                        pltpu.VMEM((n_seq, 2, GLA_KEY_W, GLA_DV), F32)],
        compiler_params=pltpu.CompilerParams(
            dimension_semantics=("parallel",), vmem_limit_bytes=VMEM_LIMIT),
        name="gla",
    )(q, k, v, g, lgf, lgb, s0f, s0b, g_out)


def _attn_kernel(*refs, n_kv):
    q_ref, o_ref = refs[0], refs[-1]
    kv = [(refs[1 + 2 * i], refs[2 + 2 * i]) for i in range(n_kv)]
    for h in range(MLA_HEADS):
        sl = slice(h * HEAD_W, (h + 1) * HEAD_W)
        q = q_ref[0, :, sl]
        scores = [_dot_nt(q, k_ref[0, :, sl]) for k_ref, _ in kv]
        m = functools.reduce(jnp.maximum, [jnp.max(s, axis=-1, keepdims=True) for s in scores])
        o, den = 0.0, 0.0
        for s, (_, v_ref) in zip(scores, kv):
            p = jnp.exp(s - m)
            den = den + jnp.sum(p, axis=-1, keepdims=True)
            o = o + _dot(p.astype(BF16), v_ref[0, :, sl])
        o_ref[0, :, sl] = (o / den).astype(BF16)


def _attn(q, kv_pairs, tq):
    b, l, _ = q.shape
    kv_specs, kv_args = [], []
    for k, v in kv_pairs:
        spec = pl.BlockSpec((1, k.shape[1], MLA_W), lambda bi, i: (bi, 0, 0))
        kv_specs += [spec, spec]
        kv_args += [k, v]
    return pl.pallas_call(
        functools.partial(_attn_kernel, n_kv=len(kv_pairs)),
        grid=(b, l // tq),
        in_specs=[pl.BlockSpec((1, tq, MLA_W), lambda bi, i: (bi, i, 0))] + kv_specs,
        out_specs=pl.BlockSpec((1, tq, MLA_W), lambda bi, i: (bi, i, 0)),
        out_shape=jax.ShapeDtypeStruct((b, l, MLA_W), BF16),
        compiler_params=pltpu.CompilerParams(
            dimension_semantics=("parallel", "parallel"), vmem_limit_bytes=VMEM_LIMIT),
        name="attn",
    )(q, *kv_args)


def _expert_cap(n_tok):
    return -(-n_tok // FFN_TILE) * FFN_TILE


def _sum01(x):
    return jnp.sum(jnp.sum(x, axis=1, keepdims=True), axis=0, keepdims=True)


def _route(logits_t, bias_col, cnt_col, cap):
    t = logits_t.shape[1]
    gsz = N_EXPERTS // N_GROUPS
    scores = jax.nn.sigmoid(logits_t)
    sel = (scores + bias_col).reshape(N_GROUPS, gsz, t)
    scores = scores.reshape(N_GROUPS, gsz, t)
    neg = jnp.float32(-jnp.inf)
    ie = lax.broadcasted_iota(I32, (N_GROUPS, gsz, t), 1)
    ig = lax.broadcasted_iota(I32, (N_GROUPS, gsz, t), 0)
    m1 = jnp.max(sel, axis=1, keepdims=True)
    first = jnp.min(jnp.where(sel == m1, ie, gsz), axis=1, keepdims=True)
    m2 = jnp.max(jnp.where(ie == first, neg, sel), axis=1, keepdims=True)
    grp = m1 + m2
    igk = lax.broadcasted_iota(I32, (N_GROUPS, 1, t), 0)
    g_sel = jnp.zeros((N_GROUPS, 1, t), jnp.bool_)
    cur = grp
    for _ in range(TOPK_GROUPS):
        m = jnp.max(cur, axis=0, keepdims=True)
        pick = igk == jnp.min(jnp.where(cur == m, igk, N_GROUPS), axis=0, keepdims=True)
        g_sel = g_sel | pick
        cur = jnp.where(pick, neg, cur)
    cur = jnp.where(g_sel, sel, neg)
    idx = ig * gsz + ie
    e_sel = jnp.zeros((N_GROUPS, gsz, t), jnp.bool_)
    picks = []
    for _ in range(TOP_K):
        m = jnp.max(jnp.max(cur, axis=1, keepdims=True), axis=0, keepdims=True)
        cand = jnp.where(cur == m, idx, N_EXPERTS)
        pick = idx == jnp.min(jnp.min(cand, axis=1, keepdims=True), axis=0, keepdims=True)
        picks.append(pick)
        e_sel = e_sel | pick
        cur = jnp.where(pick, neg, cur)
    w = jnp.where(e_sel, scores, 0.0)
    gate = w / _sum01(w) * ROUTED_SCALE
    sel_f = e_sel.astype(F32).reshape(N_EXPERTS, t)
    earlier = (lax.broadcasted_iota(I32, (t, t), 0) < lax.broadcasted_iota(I32, (t, t), 1))
    rank = _dot(sel_f.astype(BF16), earlier.astype(BF16))
    base = lax.broadcasted_iota(I32, (N_EXPERTS, 1), 0).astype(F32) * float(cap) + cnt_col
    slot = (base + rank).reshape(N_GROUPS, gsz, t)
    slot8 = jnp.concatenate([_sum01(jnp.where(pk, slot, 0.0)).reshape(1, t) for pk in picks],
                            axis=0).astype(I32)
    w8 = jnp.concatenate([_sum01(jnp.where(pk, gate, 0.0)).reshape(1, t) for pk in picks], axis=0)
    return slot8, w8, cnt_col + jnp.sum(sel_f, axis=1, keepdims=True)


def _outproj_kernel(og_ref, om_ref, x_ref, mod_ref, wtop_ref, wbot_ref, g2_ref, wr_ref, br_ref,
                    x1_ref, h2pa_ref, h2pb_ref, slot_ref, w8_ref, cnt_ref, cnt_scr, *, cap):
    @pl.when((pl.program_id(0) == 0) & (pl.program_id(1) == 0))
    def _():
        cnt_scr[...] = jnp.zeros_like(cnt_scr)

    mod = mod_ref[0]
    cnt = cnt_scr[:, 0:1]
    for r0 in range(0, x_ref.shape[1], OUT_SUB):
        rs = slice(r0, r0 + OUT_SUB)
        mix = _dot(og_ref[0, rs, :], wtop_ref[...]) + _dot(om_ref[0, rs, :], wbot_ref[...])
        x1 = x_ref[0, rs, :] + mod[2:3] * mix
        x1_ref[0, rs, :] = x1
        h2 = _rms(x1, D_MODEL) * g2_ref[...] * (1.0 + mod[4:5]) + mod[3:4]
        h2pa_ref[0, rs, :], h2pb_ref[0, rs, :] = _pack_rows(h2)
        logits_t = _dot_nt(wr_ref[...], h2, precision=HIGHEST)
        slot8, w8, cnt = _route(logits_t, br_ref[...], cnt, cap)
        slot_ref[:, rs] = slot8
        wb = lax.bitcast_convert_type(w8.astype(BF16).astype(F32), U32)
        wb = wb | (wb >> 16)
        w8_ref[0, rs, :] = jnp.concatenate(
            [jnp.broadcast_to(wb[k:k + 1], (SC_LANES, OUT_SUB)) for k in range(TOP_K)], axis=0).T
    cnt_scr[...] = jnp.broadcast_to(cnt, cnt_scr.shape)
    cnt_ref[...] = cnt_scr[...]


def _outproj(og, om, x, mod, per_batch_mod, p, tm):
    b, l, d = x.shape
    nt = l // tm
    mod_map = (lambda bi, i: (bi, 0, 0)) if per_batch_mod else (lambda bi, i: (0, 0, 0))
    tok = lambda w: pl.BlockSpec((1, tm, w), lambda bi, i: (bi, i, 0))
    full = lambda a: pl.BlockSpec(a.shape, lambda bi, i: (0, 0))
    weights = [p['w_out_top'], p['w_out_bot'], p['g_norm2'], p['w_router_t'], p['b_router_col']]
    cnt_shape = (N_EXPERTS, LANES)
    return pl.pallas_call(
        functools.partial(_outproj_kernel, cap=_expert_cap(b * l)),
        grid=(b, nt),
        in_specs=[tok(GLA_VAL_W), tok(MLA_W), tok(d), pl.BlockSpec((1, 6, d), mod_map)]
                 + [full(w) for w in weights],
        out_specs=[tok(d), tok(PACK_W), tok(PACK_W),
                   pl.BlockSpec((TOP_K, tm), lambda bi, i: (0, bi * nt + i)),
                   tok(LANES), pl.BlockSpec(cnt_shape, lambda bi, i: (0, 0))],
        out_shape=[jax.ShapeDtypeStruct((b, l, d), F32), jax.ShapeDtypeStruct((b, l, PACK_W), U32),
                   jax.ShapeDtypeStruct((b, l, PACK_W), U32),
                   jax.ShapeDtypeStruct((TOP_K, b * l), I32),
                   jax.ShapeDtypeStruct((b, l, LANES), U32),
                   jax.ShapeDtypeStruct(cnt_shape, F32)],
        scratch_shapes=[pltpu.VMEM(cnt_shape, F32)],
        compiler_params=pltpu.CompilerParams(
            dimension_semantics=("arbitrary", "arbitrary"), vmem_limit_bytes=VMEM_LIMIT),
        name="outproj",
    )(og, om, x, mod, *weights)


def _sc_mesh():
    return plsc.VectorSubcoreMesh(core_axis_name="c", subcore_axis_name="s")


def _sc_scatter_rows(src, slot8, n_rows_out):
    t, d = src.shape
    nk = slot8.shape[0]

    @functools.partial(pl.kernel, out_type=jax.ShapeDtypeStruct((n_rows_out, d), src.dtype),
                       mesh=_sc_mesh(), scratch_types=[], name="dispatch")
    def run(src_hbm, slot_hbm, out_hbm):
        def body(x_vmem, i_vmem):
            for k in range(nk):
                pltpu.sync_copy(x_vmem, out_hbm.at[i_vmem.at[k]])

        pltpu.emit_pipeline(
            body, grid=(t // SC_WINDOW,),
            in_specs=[pl.BlockSpec((SC_WINDOW, d), lambda i: (i, 0)),
                      pl.BlockSpec((nk, SC_WINDOW), lambda i: (0, i))],
            out_specs=[], core_axis_name=("c", "s"),
            dimension_semantics=(pltpu.PARALLEL,))(src_hbm, slot_hbm)

    return run(src, slot8)


def _sc_collect_sum(table, slots_tk, w_rep):
    n_tok = w_rep.shape[0]
    d = table.shape[1]
    group = SC_WINDOW // TOP_K
    chunks = d // SC_LANES

    @functools.partial(pl.kernel, out_type=jax.ShapeDtypeStruct((n_tok, 2 * d), F32),
                       mesh=_sc_mesh(), scratch_types=[pltpu.VMEM((SC_WINDOW, d), table.dtype)],
                       compiler_params=pltpu.CompilerParams(needs_layout_passes=False),
                       name="collect")
    def run(tab_hbm, slot_hbm, w_hbm, out_hbm, rows_v):
        def body(i_vmem, w_vmem, o_vmem):
            pltpu.sync_copy(tab_hbm.at[i_vmem.at[0]], rows_v)

            @pl.loop(0, group)
            def _(j):
                wv = [plsc.bitcast(w_vmem[j, pl.ds(k * SC_LANES, SC_LANES)], BF16)
                      for k in range(TOP_K)]

                def chunk(off):
                    prods = [plsc.bitcast(rows_v[j * TOP_K + k, pl.ds(off, SC_LANES)], BF16)
                             * wv[k] for k in range(TOP_K)]
                    lo = jnp.zeros((SC_LANES,), F32)
                    hi = jnp.zeros((SC_LANES,), F32)
                    for k in range(0, TOP_K, 2):
                        u = plsc.bitcast(prods[k] + prods[k + 1], U32)
                        lo = lo + lax.bitcast_convert_type(u << 16, F32)
                        hi = hi + lax.bitcast_convert_type(u & jnp.uint32(0xFFFF0000), F32)
                    o_vmem[j, pl.ds(off, SC_LANES)] = lo
                    o_vmem[j, pl.ds(d + off, SC_LANES)] = hi

                @plsc.parallel_loop(0, chunks, unroll=2)
                def _(c):
                    chunk(c * SC_LANES)

        pltpu.emit_pipeline(
            body, grid=(n_tok // group,),
            in_specs=[pl.BlockSpec((1, SC_WINDOW), lambda i: (0, i)),
                      pl.BlockSpec((group, TOP_K * SC_LANES), lambda i: (i, 0))],
            out_specs=[pl.BlockSpec((group, 2 * d), lambda i: (i, 0))],
            core_axis_name=("c", "s"),
            dimension_semantics=(pltpu.PARALLEL,))(slot_hbm, w_hbm, out_hbm)

    return run(table, slots_tk, w_rep)


def _ffn_kernel(nt_ref, first_ref, te_ref, tt_ref, xsa_hbm, xsb_hbm, wg_ref, wu_ref, wd_ref,
                ysa_hbm, ysb_hbm, xbuf, ybuf, wg_b, wu_b, wd_b, sem_in, sem_out, *, cap):
    e = pl.program_id(0)
    n_exp = pl.num_programs(0)
    n = nt_ref[e]
    g0 = first_ref[e]
    total = first_ref[n_exp - 1] + nt_ref[n_exp - 1]
    xs_hbm = (xsa_hbm, xsb_hbm)
    ys_hbm = (ysa_hbm, ysb_hbm)
    n_in = FFN_AHEAD + 1

    def rows(ex, t):
        return pl.ds(pl.multiple_of(ex * cap + t * FFN_TILE, FFN_TILE), FFN_TILE)

    def in_copy(ex, t, slot, part):
        return pltpu.make_async_copy(xs_hbm[part].at[rows(ex, t)], xbuf.at[slot, part],
                                     sem_in.at[slot, part])

    def out_copy(ex, t, slot, part):
        return pltpu.make_async_copy(ybuf.at[slot, part], ys_hbm[part].at[rows(ex, t)],
                                     sem_out.at[slot, part])

    def start_in(g):
        for part in range(PACK_PARTS):
            in_copy(te_ref[g], tt_ref[g], g % n_in, part).start()

    def wait_out(slot):
        for part in range(PACK_PARTS):
            out_copy(e, 0, slot, part).wait()

    @pl.when(n > 0)
    def _():
        @pl.when(g0 == 0)
        def _():
            for ahead in range(FFN_AHEAD):
                @pl.when(ahead < total)
                def _():
                    start_in(ahead)

        wg_b[...] = wg_ref[0].astype(BF16)
        wu_b[...] = wu_ref[0].astype(BF16)
        wd_b[...] = wd_ref[0].astype(BF16)

        def tile(t, carry):
            g = g0 + t
            slot = g % 2
            slot_in = g % n_in
            for part in range(PACK_PARTS):
                in_copy(e, t, slot_in, part).wait()

            @pl.when(g + FFN_AHEAD < total)
            def _():
                start_in(g + FFN_AHEAD)

            @pl.when(g >= 2)
            def _():
                wait_out(slot)

            x = _unpack_rows([xbuf[slot_in, part] for part in range(PACK_PARTS)]).astype(BF16)
            a = _silu(_dot(x, wg_b[...])) * _dot(x, wu_b[...])
            y = _pack_rows(_dot(a.astype(BF16), wd_b[...]))
            for part in range(PACK_PARTS):
                ybuf[slot, part] = y[part]
                out_copy(e, t, slot, part).start()
            return carry

        lax.fori_loop(0, n, tile, 0)

        @pl.when(g0 + n == total)
        def _():
            @pl.when(total >= 2)
            def _():
                wait_out(total % 2)

            wait_out((total - 1) % 2)


def _ffn(xs_parts, counts, w_gate, w_up, w_down):
    n_exp, d = w_gate.shape[0], w_gate.shape[1]
    cap = xs_parts[0].shape[0] // n_exp
    ntiles = (counts + FFN_TILE - 1) // FFN_TILE
    cum = jnp.cumsum(ntiles)
    first = cum - ntiles
    max_tiles = (cap * TOP_K) // FFN_TILE + n_exp
    g = jnp.arange(max_tiles, dtype=I32)
    done = g[:, None] >= cum[None, :]
    tile_e = jnp.minimum(jnp.sum(done, axis=1), n_exp - 1).astype(I32)
    tile_t = (g - jnp.sum(jnp.where(done, ntiles[None, :], 0), axis=1)).astype(I32)
    wmap = lambda e, nt, first, te, tt: (e, 0, 0)
    hbm = pl.BlockSpec(memory_space=pl.ANY)
    return pl.pallas_call(
        functools.partial(_ffn_kernel, cap=cap),
        grid_spec=pltpu.PrefetchScalarGridSpec(
            num_scalar_prefetch=4, grid=(n_exp,),
            in_specs=[hbm] * PACK_PARTS
                     + [pl.BlockSpec((1, d, D_EXPERT), wmap), pl.BlockSpec((1, d, D_EXPERT), wmap),
                        pl.BlockSpec((1, D_EXPERT, d), wmap)],
            out_specs=[hbm] * PACK_PARTS,
            scratch_shapes=[pltpu.VMEM((FFN_AHEAD + 1, PACK_PARTS, FFN_TILE, PACK_W), U32),
                            pltpu.VMEM((2, PACK_PARTS, FFN_TILE, PACK_W), U32),
                            pltpu.VMEM((d, D_EXPERT), BF16), pltpu.VMEM((d, D_EXPERT), BF16),
                            pltpu.VMEM((D_EXPERT, d), BF16),
                            pltpu.SemaphoreType.DMA((FFN_AHEAD + 1, PACK_PARTS)),
                            pltpu.SemaphoreType.DMA((2, PACK_PARTS))]),
        out_shape=[jax.ShapeDtypeStruct(xs_parts[0].shape, U32)] * PACK_PARTS,
        compiler_params=pltpu.CompilerParams(
            dimension_semantics=("arbitrary",), vmem_limit_bytes=VMEM_LIMIT),
        name="ffn",
    )(ntiles.astype(I32), first.astype(I32), tile_e, tile_t, *xs_parts, w_gate, w_up, w_down)


def _combine_kernel(x1_ref, h2pa_ref, h2pb_ref, ra_ref, rb_ref, mod_ref,
                    wsg_ref, wsu_ref, wsd_ref, o_ref):
    h2 = _unpack_rows([h2pa_ref[0], h2pb_ref[0]]).astype(BF16)
    a = _silu(_dot(h2, wsg_ref[...])) * _dot(h2, wsu_ref[...])
    routed = jnp.concatenate([ra_ref[0], rb_ref[0]], axis=1)
    o_ref[0] = x1_ref[0] + mod_ref[0][5:6] * (_dot(a.astype(BF16), wsd_ref[...]) + routed)


def _combine(x1, h2p_parts, routed_parts, mod, per_batch_mod, p, tm):
    b, l, d = x1.shape
    mod_map = (lambda bi, i: (bi, 0, 0)) if per_batch_mod else (lambda bi, i: (0, 0, 0))
    tok = lambda w: pl.BlockSpec((1, tm, w), lambda bi, i: (bi, i, 0))
    full = lambda a: pl.BlockSpec(a.shape, lambda bi, i: (0, 0))
    weights = [p['w_sh_gate'], p['w_sh_up'], p['w_sh_down']]
    routed_parts = [r.reshape(b, l, d // PACK_PARTS) for r in routed_parts]
    return pl.pallas_call(
        _combine_kernel,
        grid=(b, l // tm),
        in_specs=[tok(d)] + [tok(PACK_W)] * PACK_PARTS + [tok(d // PACK_PARTS)] * PACK_PARTS
                 + [pl.BlockSpec((1, 6, d), mod_map)] + [full(w) for w in weights],
        out_specs=tok(d),
        out_shape=jax.ShapeDtypeStruct((b, l, d), F32),
        compiler_params=pltpu.CompilerParams(
            dimension_semantics=("parallel", "parallel"), vmem_limit_bytes=VMEM_LIMIT),
        name="combine",
    )(x1, *h2p_parts, *routed_parts, mod, *weights)


def _pad_heads(w, parts):
    k = w.shape[0]
    per = w.shape[1] // MLA_HEADS
    w = w.reshape(k, MLA_HEADS, per)[:, :, parts[0]:parts[1]]
    w = jnp.pad(w, ((0, 0), (0, 0), (0, HEAD_W - (parts[1] - parts[0]))))
    return w.reshape(k, MLA_HEADS * HEAD_W)


def _prep_params(l, g_norm1, g_norm2, w_in, w_gk_fwd, b_gk_fwd, w_gk_bwd, b_gk_bwd, g_gla_out,
                 g_q_lora, w_uq, g_kv_lora, w_ukv, g_qk_q, g_qk_k, w_out, w_router, b_router,
                 w_sh_gate, w_sh_up, w_sh_down):
    w = w_in[l]
    d = w.shape[0]
    o_lrf = COL_CQ
    o_lrb = o_lrf + GLA_GATE_RANK
    o_cq = o_lrb + GLA_GATE_RANK
    o_ckv = o_cq + Q_LORA
    o_kr = o_ckv + KV_LORA
    w_in_p = jnp.concatenate([
        w[:, 0:COL_CQ], w[:, o_cq:o_ckv], w[:, o_ckv:o_kr], jnp.zeros((d, KR_LANE0), w.dtype),
        w[:, o_kr:o_kr + ROPE_DIM], w[:, o_lrf:o_lrb], w[:, o_lrb:o_cq]], axis=1).astype(BF16)
    z = jnp.zeros((GLA_GATE_RANK, GLA_KEY_W), F32)
    w_gk_big = jnp.concatenate([
        jnp.zeros((LANES - 2 * GLA_GATE_RANK, 2 * GLA_KEY_W), F32),
        jnp.concatenate([w_gk_fwd[l], z], axis=1),
        jnp.concatenate([z, w_gk_bwd[l]], axis=1)], axis=0).astype(BF16)
    pad_gain = lambda g: jnp.pad(g, (0, HEAD_W - QK_DIM)).reshape(1, HEAD_W)
    w_ukv_h = w_ukv[l]
    return {
        'g_norm1': g_norm1[l].reshape(1, d), 'g_norm2': g_norm2[l].reshape(1, d),
        'w_in_p': w_in_p, 'w_gk_big': w_gk_big,
        'b_gk': jnp.concatenate([b_gk_fwd[l], b_gk_bwd[l]]).reshape(1, 2 * GLA_KEY_W),
        'g_gla_out': g_gla_out[l].reshape(1, GLA_DV),
        'g_q_lora': g_q_lora[l].reshape(1, Q_LORA),
        'w_uq_p': _pad_heads(w_uq[l], (0, QK_DIM)).astype(BF16),
        'g_kv_lora': g_kv_lora[l].reshape(1, KV_LORA),
        'w_ukv_p': jnp.concatenate([_pad_heads(w_ukv_h, (0, NOPE_DIM)),
                                    _pad_heads(w_ukv_h, (NOPE_DIM, NOPE_DIM + V_DIM))],
                                   axis=1).astype(BF16),
        'gq': pad_gain(g_qk_q[l]), 'gk': pad_gain(g_qk_k[l]),
        'w_out_top': w_out[l][:GLA_VAL_W].astype(BF16),
        'w_out_bot': w_out[l][GLA_VAL_W:].astype(BF16),
        'w_router_t': w_router[l].T, 'b_router_col': b_router[l].reshape(N_EXPERTS, 1),
        'w_sh_gate': w_sh_gate[l].astype(BF16), 'w_sh_up': w_sh_up[l].astype(BF16),
        'w_sh_down': w_sh_down[l].astype(BF16),
    }


def _rope_tables(length):
    pos = np.arange(length)
    r = (pos // GRID_W).astype(np.float32)
    c = (pos % GRID_W).astype(np.float32)
    half = ROPE_DIM // 4
    inv_freq = np.float32(ROPE_THETA) ** (-np.arange(half, dtype=np.float32) / np.float32(half))
    ang_r = r[:, None] * inv_freq[None, :]
    ang_c = c[:, None] * inv_freq[None, :]
    zeros = lambda w: np.zeros((length, w), np.float32)
    ones = lambda w: np.ones((length, w), np.float32)
    tail_w = HEAD_W - ROPE_LANE0 - ROPE_DIM
    cos = np.concatenate([ones(ROPE_LANE0), np.cos(ang_r), np.cos(ang_r), np.cos(ang_c),
                          np.cos(ang_c), ones(tail_w)], axis=1)
    s1 = np.concatenate([zeros(ROPE_LANE0), -np.sin(ang_r), zeros(half), -np.sin(ang_c),
                         zeros(half), zeros(tail_w)], axis=1)
    s2 = np.concatenate([zeros(ROPE_LANE0), zeros(half), np.sin(ang_r), zeros(half),
                         np.sin(ang_c), zeros(tail_w)], axis=1)
    return tuple(jnp.asarray(t, F32) for t in (cos, s1, s2))


def _mix_route_dispatch(x_tok, seq_shape, mod, per_batch_mod, p, s0f, s0b, rope_tabs, ctx_kv,
                        tq):
    bt, lt, d = x_tok.shape
    b, l = seq_shape
    n_tok = bt * lt
    (q, k, v, g, lgf, lgb, qm, km, vm, ckvn, kr) = _inproj(x_tok, mod, per_batch_mod, p,
                                                            rope_tabs, IN_TILE)
    seq = lambda a: a.reshape(b, l, a.shape[-1])
    og, sf, sb = _gla(seq(q), seq(k), seq(v), seq(g), seq(lgf), seq(lgb), s0f, s0b,
                      p['g_gla_out'], GLA_SEQS if l == GLA_BLOCK else 1)
    kv_pairs = [(seq(km), seq(vm))] + ([ctx_kv] if ctx_kv is not None else [])
    om = _attn(seq(qm), kv_pairs, tq)
    tokv = lambda a: a.reshape(bt, lt, a.shape[-1])
    x1, h2a, h2b, slot8, w8, cnt = _outproj(tokv(og), tokv(om), x_tok, mod, per_batch_mod, p,
                                            OUT_TILE)
    x_sorted = [_sc_scatter_rows(h.reshape(n_tok, PACK_W), slot8,
                                 N_EXPERTS * _expert_cap(n_tok)) for h in (h2a, h2b)]
    routed = dict(x1=x1, h2=(h2a, h2b), slot8=slot8, w8=w8, cnt=cnt, x_sorted=x_sorted)
    return routed, sf, sb, ckvn, kr


def _experts_collect(r, w_gate, w_up, w_down):
    n_tok = r['slot8'].shape[1]
    y_sorted = _ffn(r['x_sorted'], r['cnt'][:, 0].astype(I32), w_gate, w_up, w_down)
    slots_tk = r['slot8'].T.reshape(1, TOP_K * n_tok)
    w_rep = r['w8'].reshape(n_tok, LANES)
    return [_sc_collect_sum(y, slots_tk, w_rep) for y in y_sorted]


def kernel(x_prompt, x_sample, c, state_gla_fwd, state_gla_bwd, cache_mla_ckv, cache_mla_krope,
           c_ctx, w_ada, b_ada, g_norm1, g_norm2, w_in, w_gk_fwd, b_gk_fwd, w_gk_bwd, b_gk_bwd,
           g_gla_out, g_q_lora, w_uq, g_kv_lora, w_ukv, g_qk_q, g_qk_k, w_out,
           w_router, b_router, w_exp_gate, w_exp_up, w_exp_down, w_sh_gate, w_sh_up, w_sh_down):
    bp, lp, d = x_prompt.shape
    bs, ls, _ = x_sample.shape
    depth = w_ada.shape[0]
    xp = x_prompt.reshape(1, bp * lp, d)
    xs = x_sample
    new_f, new_b, new_ckv, new_kr = [], [], [], []
    lat_tabs = _rope_tables(ls)
    cvecs = jnp.concatenate([c_ctx[None], c, jnp.zeros((8 - 1 - bs, d), F32)], axis=0)
    for l in range(depth):
        p = _prep_params(l, g_norm1, g_norm2, w_in, w_gk_fwd, b_gk_fwd, w_gk_bwd, b_gk_bwd,
                         g_gla_out, g_q_lora, w_uq, g_kv_lora, w_ukv, g_qk_q, g_qk_k, w_out,
                         w_router, b_router, w_sh_gate, w_sh_up, w_sh_down)
        mod = _ada(cvecs, w_ada[l], b_ada[l]).reshape(8, 6, d)
        mod_p, mod_s = mod[0:1], mod[1:1 + bs]
        zeros = jnp.zeros((bp, GLA_KEY_W, GLA_DV), F32)
        r_ctx, sf, sb, ckvn, kr = _mix_route_dispatch(
            xp, (bp, lp), mod_p, False, p, zeros, zeros, (), None, lp)
        new_f.append(sf.reshape(bp, GLA_HEADS, GLA_DK, GLA_DV))
        new_b.append(sb.reshape(bp, GLA_HEADS, GLA_DK, GLA_DV))
        new_ckv.append(ckvn.reshape(bp, lp, KV_LORA))
        new_kr.append(kr.reshape(bp, lp, LANES)[:, :, KR_LANE0:KR_LANE0 + ROPE_DIM])
        kr_cache = jnp.pad(cache_mla_krope[:, l],
                           ((0, 0), (0, 0), (ROPE_LANE0, LANES - ROPE_LANE0 - ROPE_DIM)))
        ctx_kv = _cache_kv(cache_mla_ckv[:, l], kr_cache, p)
        s0f = state_gla_fwd[:, l].reshape(bs, GLA_KEY_W, GLA_DV)
        s0b = state_gla_bwd[:, l].reshape(bs, GLA_KEY_W, GLA_DV)
        r_lat, _, _, _, _ = _mix_route_dispatch(
            xs, (bs, ls), mod_s, True, p, s0f, s0b, lat_tabs, ctx_kv, 256)
        experts = (w_exp_gate[l], w_exp_up[l], w_exp_down[l])
        routed_ctx = _experts_collect(r_ctx, *experts)
        routed_lat = _experts_collect(r_lat, *experts)
        xp = _combine(r_ctx['x1'], r_ctx['h2'], routed_ctx, mod_p, False, p, OUT_TILE)
        xs = _combine(r_lat['x1'], r_lat['h2'], routed_lat, mod_s, True, p, OUT_TILE)
    return (xp.reshape(bp, lp, d), xs, jnp.stack(new_f, axis=1), jnp.stack(new_b, axis=1),
            jnp.stack(new_ckv, axis=1), jnp.stack(new_kr, axis=1))
```

```python
import functools

import jax
import jax.numpy as jnp
import numpy as np
from jax import lax
from jax.experimental import pallas as pl
from jax.experimental.pallas import tpu as pltpu
from jax.experimental.pallas import tpu_sc as plsc

F32 = jnp.float32
BF16 = jnp.bfloat16
I32 = jnp.int32
U32 = jnp.uint32

D_MODEL = 1024
EPS = 1e-6
GRID_W = 64
GLA_HEADS = 4
GLA_DK = 64
GLA_DV = 128
GLA_GATE_RANK = 16
GLA_GATE_NORM = 16.0
GLA_KEY_W = GLA_HEADS * GLA_DK
GLA_VAL_W = GLA_HEADS * GLA_DV
MLA_HEADS = 4
Q_LORA = 256
KV_LORA = 128
NOPE_DIM = 64
ROPE_DIM = 32
V_DIM = 128
QK_DIM = NOPE_DIM + ROPE_DIM
ROPE_THETA = 10000.0
N_EXPERTS = 64
TOP_K = 8
N_GROUPS = 8
TOPK_GROUPS = 4
D_EXPERT = 256
ROUTED_SCALE = 2.5

LANES = 128
HEAD_W = LANES
MLA_W = MLA_HEADS * HEAD_W
ROPE_LANE0 = NOPE_DIM
COL_V = 2 * GLA_KEY_W
COL_G = COL_V + GLA_VAL_W
COL_CQ = COL_G + GLA_VAL_W
COL_CKV = COL_CQ + Q_LORA
TAIL0 = COL_CKV + KV_LORA
IN_W = TAIL0 + LANES
KR_LANE0 = LANES - ROPE_DIM - 2 * GLA_GATE_RANK
GLA_BLOCK = 256
GLA_SUB = 64
GLA_SEQS = 2
IN_TILE = 512
IN_SUB = 256
OUT_TILE = 1024
OUT_SUB = 1024
FFN_TILE = 512
FFN_AHEAD = 5
PACK_PARTS = 2
PACK_W = D_MODEL // (2 * PACK_PARTS)
SC_WINDOW = 128
SC_LANES = 16
VMEM_LIMIT = 56 * 1024 * 1024

HIGHEST = lax.Precision.HIGHEST


def _dot(a, b, precision=None):
    return jnp.dot(a, b, preferred_element_type=F32, precision=precision)


def _dot_nt(a, b, precision=None):
    return lax.dot_general(a, b, (((1,), (1,)), ((), ())), preferred_element_type=F32,
                           precision=precision)


def _rms(x, width):
    ss = jnp.sum(x * x, axis=-1, keepdims=True) * (1.0 / width)
    return x * lax.rsqrt(ss + EPS)


def _silu(x):
    return x * jax.nn.sigmoid(x)


def _log_sigmoid(x):
    return jnp.minimum(x, 0.0) - jnp.log1p(jnp.exp(-jnp.abs(x)))


def _pack_rows(x):
    parts = []
    for i in range(PACK_PARTS):
        c0 = i * 2 * PACK_W
        lo = lax.bitcast_convert_type(x[:, c0:c0 + PACK_W].astype(BF16).astype(F32), U32)
        hi = lax.bitcast_convert_type(
            x[:, c0 + PACK_W:c0 + 2 * PACK_W].astype(BF16).astype(F32), U32)
        parts.append(hi | (lo >> 16))
    return parts


def _unpack_rows(parts):
    cols = []
    for w in parts:
        cols.append(lax.bitcast_convert_type(w << 16, F32))
        cols.append(lax.bitcast_convert_type(w & jnp.uint32(0xFFFF0000), F32))
    return jnp.concatenate(cols, axis=1)


def _ada_kernel(c_ref, w_ref, b_ref, o_ref):
    o_ref[...] = _dot(_silu(c_ref[...]), w_ref[...], precision=HIGHEST) + b_ref[...]


def _ada(cvecs, w_ada, b_ada):
    n = w_ada.shape[1]
    tn = 768
    return pl.pallas_call(
        _ada_kernel,
        grid=(n // tn,),
        in_specs=[pl.BlockSpec((8, D_MODEL), lambda j: (0, 0)),
                  pl.BlockSpec((D_MODEL, tn), lambda j: (0, j)),
                  pl.BlockSpec((1, tn), lambda j: (0, j))],
        out_specs=pl.BlockSpec((8, tn), lambda j: (0, j)),
        out_shape=jax.ShapeDtypeStruct((8, n), F32),
        name="ada",
    )(cvecs, w_ada, b_ada.reshape(1, n))


def _rope(x, c, s1, s2):
    return x * c + pltpu.roll(x, LANES - 8, 1) * s1 + pltpu.roll(x, 8, 1) * s2


def _mla_kv(ckv_n, kr, w_ukv_ref, gk, rope_tabs, k_ref, v_ref, rs=slice(None)):
    kv = _dot(ckv_n.astype(BF16), w_ukv_ref[...])
    for h in range(MLA_HEADS):
        k_h = kv[:, h * HEAD_W:(h + 1) * HEAD_W] + kr
        k_h = _rms(k_h, QK_DIM) * gk
        if rope_tabs is not None:
            k_h = _rope(k_h, *rope_tabs)
        k_ref[0, rs, h * HEAD_W:(h + 1) * HEAD_W] = k_h.astype(BF16)
    v_ref[0, rs, :] = kv[:, MLA_W:].astype(BF16)


def _lane_mask(lo, hi, rows):
    lane = lax.broadcasted_iota(I32, (rows, LANES), 1)
    return (lane >= lo) & (lane < hi)


def _inproj_kernel(x_ref, mod_ref, g1_ref, win_ref, wgk_ref, bgk_ref, gql_ref, wuq_ref,
                   gkv_ref, wukv_ref, gq_ref, gk_ref, *refs, positions):
    tab_refs, outs = (refs[:3], refs[3:]) if positions else ((), refs)
    (q_ref, k_ref, v_ref, g_ref, lgf_ref, lgb_ref, qm_ref, km_ref, vm_ref, ckvn_ref,
     kr_ref) = outs
    mod = mod_ref[0]
    for r0 in range(0, x_ref.shape[1], IN_SUB):
        rs = slice(r0, r0 + IN_SUB)
        x = x_ref[0, rs, :]
        h = _rms(x, D_MODEL) * g1_ref[...] * (1.0 + mod[1:2]) + mod[0:1]
        y = _dot(h.astype(BF16), win_ref[...])
        q_ref[0, rs, :] = (y[:, 0:GLA_KEY_W] * GLA_DK ** -0.5).astype(BF16)
        k_ref[0, rs, :] = y[:, GLA_KEY_W:COL_V].astype(BF16)
        v_ref[0, rs, :] = y[:, COL_V:COL_G].astype(BF16)
        g_ref[0, rs, :] = y[:, COL_G:COL_CQ].astype(BF16)
        tail = y[:, TAIL0:IN_W]
        pre = _dot(tail.astype(BF16), wgk_ref[...]) + bgk_ref[...]
        logg = _log_sigmoid(pre) * (1.0 / GLA_GATE_NORM)
        lgf_ref[0, rs, :] = logg[:, 0:GLA_KEY_W]
        lgb_ref[0, rs, :] = logg[:, GLA_KEY_W:]
        tabs = tuple(r[rs, :] for r in tab_refs) if positions else None
        cq = _rms(y[:, COL_CQ:COL_CKV], Q_LORA) * gql_ref[...]
        qm = _dot(cq.astype(BF16), wuq_ref[...])
        gq = gq_ref[...]
        for hh in range(MLA_HEADS):
            q_h = _rms(qm[:, hh * HEAD_W:(hh + 1) * HEAD_W], QK_DIM) * gq
            if positions:
                q_h = _rope(q_h, *tabs)
            qm_ref[0, rs, hh * HEAD_W:(hh + 1) * HEAD_W] = (q_h * QK_DIM ** -0.5).astype(BF16)
        ckv_n = _rms(y[:, COL_CKV:TAIL0], KV_LORA) * gkv_ref[...]
        ckvn_ref[0, rs, :] = ckv_n
        kr_ref[0, rs, :] = tail
        kr = jnp.where(_lane_mask(KR_LANE0, KR_LANE0 + ROPE_DIM, IN_SUB), tail, 0.0)
        _mla_kv(ckv_n, kr, wukv_ref, gk_ref[...], tabs, km_ref, vm_ref, rs)


def _inproj(x, mod, per_batch_mod, p, rope_tabs, tm):
    b, l, d = x.shape
    nt = l // tm
    mod_map = (lambda bi, i: (bi, 0, 0)) if per_batch_mod else (lambda bi, i: (0, 0, 0))
    tab_map = lambda bi, i: (i, 0)
    const = lambda bi, i: (0, 0)
    tok = lambda w: pl.BlockSpec((1, tm, w), lambda bi, i: (bi, i, 0))
    full = lambda a: pl.BlockSpec(a.shape, const)
    weights = [p['g_norm1'], p['w_in_p'], p['w_gk_big'], p['b_gk'], p['g_q_lora'], p['w_uq_p'],
               p['g_kv_lora'], p['w_ukv_p'], p['gq'], p['gk']]
    outs = [(GLA_KEY_W, BF16), (GLA_KEY_W, BF16), (GLA_VAL_W, BF16), (GLA_VAL_W, BF16),
            (GLA_KEY_W, F32), (GLA_KEY_W, F32), (MLA_W, BF16), (MLA_W, BF16), (MLA_W, BF16),
            (KV_LORA, F32), (LANES, F32)]
    return pl.pallas_call(
        functools.partial(_inproj_kernel, positions=bool(rope_tabs)),
        grid=(b, nt),
        in_specs=[tok(d), pl.BlockSpec((1, 6, d), mod_map)] + [full(w) for w in weights]
                 + [pl.BlockSpec((tm, LANES), tab_map)] * len(rope_tabs),
        out_specs=[tok(w) for w, _ in outs],
        out_shape=[jax.ShapeDtypeStruct((b, l, w), dt) for w, dt in outs],
        compiler_params=pltpu.CompilerParams(
            dimension_semantics=("parallel", "parallel"), vmem_limit_bytes=VMEM_LIMIT),
        name="inproj",
    )(x, mod, *weights, *rope_tabs)


def _cache_kv_kernel(ckv_ref, kr_ref, wukv_ref, gk_ref, k_ref, v_ref):
    _mla_kv(ckv_ref[0], kr_ref[0], wukv_ref, gk_ref[...], None, k_ref, v_ref)


def _cache_kv(ckv, kr128, p):
    b, l, _ = ckv.shape
    tok = lambda w: pl.BlockSpec((1, l, w), lambda bi: (bi, 0, 0))
    full = lambda a: pl.BlockSpec(a.shape, lambda bi: (0, 0))
    return pl.pallas_call(
        _cache_kv_kernel,
        grid=(b,),
        in_specs=[tok(KV_LORA), tok(LANES), full(p['w_ukv_p']), full(p['gk'])],
        out_specs=[tok(MLA_W), tok(MLA_W)],
        out_shape=[jax.ShapeDtypeStruct((b, l, MLA_W), BF16)] * 2,
        name="cache_kv",
    )(ckv, kr128, p['w_ukv_p'], p['gk'])


def _split3(x):
    hi = x.astype(BF16)
    r1 = x - hi.astype(F32)
    mid = r1.astype(BF16)
    lo = (r1 - mid.astype(F32)).astype(BF16)
    return hi, mid, lo


def _gla_block(q, k, v, lg, s, fwd):
    n = GLA_BLOCK
    nc = n // GLA_SUB
    row = lax.broadcasted_iota(I32, (n, n), 0)
    col = lax.broadcasted_iota(I32, (n, n), 1)
    same = (row // GLA_SUB) == (col // GLA_SUB)
    causal = same & ((col <= row) if fwd else (col >= row))
    tri = causal.astype(BF16)
    hi, mid, lo = _split3(lg)
    cum = _dot(tri, hi) + _dot(tri, mid) + _dot(tri, lo)
    tot_rows, mid_rows = [], []
    for c in range(nc):
        r_tot = c * GLA_SUB + (GLA_SUB - 1 if fwd else 0)
        r_mid = c * GLA_SUB + GLA_SUB // 2
        tot_rows.append(jnp.broadcast_to(cum[r_tot:r_tot + 1], (GLA_SUB, GLA_KEY_W)))
        mid_rows.append(jnp.broadcast_to(cum[r_mid:r_mid + 1], (GLA_SUB, GLA_KEY_W)))
    tot_rows = jnp.concatenate(tot_rows, axis=0)
    mid_rows = jnp.concatenate(mid_rows, axis=0)
    rel = cum - mid_rows
    qi = q * jnp.exp(rel)
    ki = (k * jnp.exp(-rel)).astype(BF16)
    q_in = (q * jnp.exp(cum)).astype(BF16)
    k_up = k * jnp.exp(tot_rows - cum)
    dec = jnp.exp(tot_rows)
    k_up_t = k_up.T.astype(BF16)
    dec_t = dec.T
    lane_head = lax.broadcasted_iota(I32, (n, GLA_KEY_W), 1) // GLA_DK
    srow_head = lax.broadcasted_iota(I32, (GLA_KEY_W, GLA_DV), 0) // GLA_DK
    vrow_chunk = lax.broadcasted_iota(I32, (n, GLA_VAL_W), 0) // GLA_SUB
    o_heads = []
    for h in range(GLA_HEADS):
        a = _dot_nt(jnp.where(lane_head == h, qi, 0.0).astype(BF16), ki)
        a = jnp.where(causal, a, 0.0).astype(BF16)
        o_heads.append(_dot(a, v[:, h * GLA_DV:(h + 1) * GLA_DV]))
    o = jnp.concatenate(o_heads, axis=1)
    o_inter = [None] * nc
    for c in (range(nc) if fwd else range(nc - 1, -1, -1)):
        s_bd = jnp.concatenate(
            [jnp.where(srow_head == h, s, 0.0).astype(BF16) for h in range(GLA_HEADS)], axis=1)
        o_inter[c] = _dot(q_in[c * GLA_SUB:(c + 1) * GLA_SUB], s_bd)
        v_c = jnp.where(vrow_chunk == c, v, jnp.zeros_like(v))
        u = jnp.concatenate(
            [_dot(k_up_t[h * GLA_DK:(h + 1) * GLA_DK], v_c[:, h * GLA_DV:(h + 1) * GLA_DV])
             for h in range(GLA_HEADS)], axis=0)
        s = dec_t[:, c * GLA_SUB:c * GLA_SUB + 1] * s + u
    return o + jnp.concatenate(o_inter, axis=0), s


def _gla_kernel(q_ref, k_ref, v_ref, g_ref, lgf_ref, lgb_ref, s0f_ref, s0b_ref, gout_ref,
                o_ref, sf_ref, sb_ref, acc_ref, st_ref, *, nblk, n_seq):
    for si in range(n_seq):
        _gla_sequence(si, q_ref, k_ref, v_ref, g_ref, lgf_ref, lgb_ref, s0f_ref, s0b_ref,
                      gout_ref, o_ref, sf_ref, sb_ref, acc_ref, st_ref, nblk)


def _gla_sequence(si, q_ref, k_ref, v_ref, g_ref, lgf_ref, lgb_ref, s0f_ref, s0b_ref, gout_ref,
                  o_ref, sf_ref, sb_ref, acc_ref, st_ref, nblk):
    st_ref[si, 0] = s0f_ref[si]
    st_ref[si, 1] = s0b_ref[si]

    def load(blk):
        start = blk * GLA_BLOCK
        r = pl.ds(start if isinstance(blk, int) else pl.multiple_of(start, GLA_BLOCK), GLA_BLOCK)
        return r, q_ref[si, r, :].astype(F32), k_ref[si, r, :].astype(F32), v_ref[si, r, :]

    def pair_step(i, first_touch):
        rf, qf, kf, vf = load(i)
        rb, qb, kb, vb = load(nblk - 1 - i)
        of, s_f = _gla_block(qf, kf, vf, lgf_ref[si, rf, :], st_ref[si, 0], True)
        ob, s_b = _gla_block(qb, kb, vb, lgb_ref[si, rb, :], st_ref[si, 1], False)
        if first_touch:
            acc_ref[si, rf, :] = of
            acc_ref[si, rb, :] = ob
        else:
            acc_ref[si, rf, :] += of
            acc_ref[si, rb, :] += ob
        st_ref[si, 0] = s_f
        st_ref[si, 1] = s_b

    if nblk == 1:
        r, q, k, v = load(0)
        of, s_f = _gla_block(q, k, v, lgf_ref[si, r, :], st_ref[si, 0], True)
        ob, s_b = _gla_block(q, k, v, lgb_ref[si, r, :], st_ref[si, 1], False)
        acc_ref[si, r, :] = of + ob
        st_ref[si, 0] = s_f
        st_ref[si, 1] = s_b
    else:
        half = nblk // 2
        lax.fori_loop(0, half, lambda i, c: (pair_step(i, True), c)[1], 0)
        lax.fori_loop(half, nblk, lambda i, c: (pair_step(i, False), c)[1], 0)
    sf_ref[si] = st_ref[si, 0]
    sb_ref[si] = st_ref[si, 1]

    def fin_step(blk, carry):
        r = pl.ds(pl.multiple_of(blk * GLA_BLOCK, GLA_BLOCK), GLA_BLOCK)
        o = acc_ref[si, r, :]
        gate = _silu(g_ref[si, r, :].astype(F32))
        for h in range(GLA_HEADS):
            sl = slice(h * GLA_DV, (h + 1) * GLA_DV)
            o_ref[si, r, sl] = (_rms(o[:, sl], GLA_DV) * gout_ref[...] * gate[:, sl]).astype(BF16)
        return carry

    lax.fori_loop(0, nblk, fin_step, 0)


def _gla(q, k, v, g, lgf, lgb, s0f, s0b, g_out, n_seq):
    b, l, _ = q.shape
    assert l == GLA_BLOCK or l % (2 * GLA_BLOCK) == 0
    seq = lambda w: pl.BlockSpec((n_seq, l, w), lambda bi: (bi, 0, 0))
    st = pl.BlockSpec((n_seq, GLA_KEY_W, GLA_DV), lambda bi: (bi, 0, 0))
    return pl.pallas_call(
        functools.partial(_gla_kernel, nblk=l // GLA_BLOCK, n_seq=n_seq),
        grid=(b // n_seq,),
        in_specs=[seq(GLA_KEY_W), seq(GLA_KEY_W), seq(GLA_VAL_W), seq(GLA_VAL_W),
                  seq(GLA_KEY_W), seq(GLA_KEY_W), st, st,
                  pl.BlockSpec((1, GLA_DV), lambda bi: (0, 0))],
        out_specs=[seq(GLA_VAL_W), st, st],
        out_shape=[jax.ShapeDtypeStruct((b, l, GLA_VAL_W), BF16),
                   jax.ShapeDtypeStruct((b, GLA_KEY_W, GLA_DV), F32),
                   jax.ShapeDtypeStruct((b, GLA_KEY_W, GLA_DV), F32)],
        scratch_shapes=[pltpu.VMEM((n_seq, l, GLA_VAL_W), F32),
                        pltpu.VMEM((n_seq, 2, GLA_KEY_W, GLA_DV), F32)],
        compiler_params=pltpu.CompilerParams(
            dimension_semantics=("parallel",), vmem_limit_bytes=VMEM_LIMIT),
        name="gla",
    )(q, k, v, g, lgf, lgb, s0f, s0b, g_out)


def _attn_kernel(*refs, n_kv):
    q_ref, o_ref = refs[0], refs[-1]
    kv = [(refs[1 + 2 * i], refs[2 + 2 * i]) for i in range(n_kv)]
    for h in range(MLA_HEADS):
        sl = slice(h * HEAD_W, (h + 1) * HEAD_W)
        q = q_ref[0, :, sl]
        scores = [_dot_nt(q, k_ref[0, :, sl]) for k_ref, _ in kv]
        m = functools.reduce(jnp.maximum, [jnp.max(s, axis=-1, keepdims=True) for s in scores])
        o, den = 0.0, 0.0
        for s, (_, v_ref) in zip(scores, kv):
            p = jnp.exp(s - m)
            den = den + jnp.sum(p, axis=-1, keepdims=True)
            o = o + _dot(p.astype(BF16), v_ref[0, :, sl])
        o_ref[0, :, sl] = (o / den).astype(BF16)


def _attn(q, kv_pairs, tq):
    b, l, _ = q.shape
    kv_specs, kv_args = [], []
    for k, v in kv_pairs:
        spec = pl.BlockSpec((1, k.shape[1], MLA_W), lambda bi, i: (bi, 0, 0))
        kv_specs += [spec, spec]
        kv_args += [k, v]
    return pl.pallas_call(
        functools.partial(_attn_kernel, n_kv=len(kv_pairs)),
        grid=(b, l // tq),
        in_specs=[pl.BlockSpec((1, tq, MLA_W), lambda bi, i: (bi, i, 0))] + kv_specs,
        out_specs=pl.BlockSpec((1, tq, MLA_W), lambda bi, i: (bi, i, 0)),
        out_shape=jax.ShapeDtypeStruct((b, l, MLA_W), BF16),
        compiler_params=pltpu.CompilerParams(
            dimension_semantics=("parallel", "parallel"), vmem_limit_bytes=VMEM_LIMIT),
        name="attn",
    )(q, *kv_args)


def _expert_cap(n_tok):
    return -(-n_tok // FFN_TILE) * FFN_TILE


def _sum01(x):
    return jnp.sum(jnp.sum(x, axis=1, keepdims=True), axis=0, keepdims=True)


def _route(logits_t, bias_col, cnt_col, cap):
    t = logits_t.shape[1]
    gsz = N_EXPERTS // N_GROUPS
    scores = jax.nn.sigmoid(logits_t)
    sel = (scores + bias_col).reshape(N_GROUPS, gsz, t)
    scores = scores.reshape(N_GROUPS, gsz, t)
    neg = jnp.float32(-jnp.inf)
    ie = lax.broadcasted_iota(I32, (N_GROUPS, gsz, t), 1)
    ig = lax.broadcasted_iota(I32, (N_GROUPS, gsz, t), 0)
    m1 = jnp.max(sel, axis=1, keepdims=True)
    first = jnp.min(jnp.where(sel == m1, ie, gsz), axis=1, keepdims=True)
    m2 = jnp.max(jnp.where(ie == first, neg, sel), axis=1, keepdims=True)
    grp = m1 + m2
    igk = lax.broadcasted_iota(I32, (N_GROUPS, 1, t), 0)
    g_sel = jnp.zeros((N_GROUPS, 1, t), jnp.bool_)
    cur = grp
    for _ in range(TOPK_GROUPS):
        m = jnp.max(cur, axis=0, keepdims=True)
        pick = igk == jnp.min(jnp.where(cur == m, igk, N_GROUPS), axis=0, keepdims=True)
        g_sel = g_sel | pick
        cur = jnp.where(pick, neg, cur)
    cur = jnp.where(g_sel, sel, neg)
    idx = ig * gsz + ie
    e_sel = jnp.zeros((N_GROUPS, gsz, t), jnp.bool_)
    picks = []
    for _ in range(TOP_K):
        m = jnp.max(jnp.max(cur, axis=1, keepdims=True), axis=0, keepdims=True)
        cand = jnp.where(cur == m, idx, N_EXPERTS)
        pick = idx == jnp.min(jnp.min(cand, axis=1, keepdims=True), axis=0, keepdims=True)
        picks.append(pick)
        e_sel = e_sel | pick
        cur = jnp.where(pick, neg, cur)
    w = jnp.where(e_sel, scores, 0.0)
    gate = w / _sum01(w) * ROUTED_SCALE
    sel_f = e_sel.astype(F32).reshape(N_EXPERTS, t)
    earlier = (lax.broadcasted_iota(I32, (t, t), 0) < lax.broadcasted_iota(I32, (t, t), 1))
    rank = _dot(sel_f.astype(BF16), earlier.astype(BF16))
    base = lax.broadcasted_iota(I32, (N_EXPERTS, 1), 0).astype(F32) * float(cap) + cnt_col
    slot = (base + rank).reshape(N_GROUPS, gsz, t)
    slot8 = jnp.concatenate([_sum01(jnp.where(pk, slot, 0.0)).reshape(1, t) for pk in picks],
                            axis=0).astype(I32)
    w8 = jnp.concatenate([_sum01(jnp.where(pk, gate, 0.0)).reshape(1, t) for pk in picks], axis=0)
    return slot8, w8, cnt_col + jnp.sum(sel_f, axis=1, keepdims=True)


def _outproj_kernel(og_ref, om_ref, x_ref, mod_ref, wtop_ref, wbot_ref, g2_ref, wr_ref, br_ref,
                    x1_ref, h2pa_ref, h2pb_ref, slot_ref, w8_ref, cnt_ref, cnt_scr, *, cap):
    @pl.when((pl.program_id(0) == 0) & (pl.program_id(1) == 0))
    def _():
        cnt_scr[...] = jnp.zeros_like(cnt_scr)

    mod = mod_ref[0]
    cnt = cnt_scr[:, 0:1]
    for r0 in range(0, x_ref.shape[1], OUT_SUB):
        rs = slice(r0, r0 + OUT_SUB)
        mix = _dot(og_ref[0, rs, :], wtop_ref[...]) + _dot(om_ref[0, rs, :], wbot_ref[...])
        x1 = x_ref[0, rs, :] + mod[2:3] * mix
        x1_ref[0, rs, :] = x1
        h2 = _rms(x1, D_MODEL) * g2_ref[...] * (1.0 + mod[4:5]) + mod[3:4]
        h2pa_ref[0, rs, :], h2pb_ref[0, rs, :] = _pack_rows(h2)
        logits_t = _dot_nt(wr_ref[...], h2, precision=HIGHEST)
        slot8, w8, cnt = _route(logits_t, br_ref[...], cnt, cap)
        slot_ref[:, rs] = slot8
        wb = lax.bitcast_convert_type(w8.astype(BF16).astype(F32), U32)
        wb = wb | (wb >> 16)
        w8_ref[0, rs, :] = jnp.concatenate(
            [jnp.broadcast_to(wb[k:k + 1], (SC_LANES, OUT_SUB)) for k in range(TOP_K)], axis=0).T
    cnt_scr[...] = jnp.broadcast_to(cnt, cnt_scr.shape)
    cnt_ref[...] = cnt_scr[...]


def _outproj(og, om, x, mod, per_batch_mod, p, tm):
    b, l, d = x.shape
    nt = l // tm
    mod_map = (lambda bi, i: (bi, 0, 0)) if per_batch_mod else (lambda bi, i: (0, 0, 0))
    tok = lambda w: pl.BlockSpec((1, tm, w), lambda bi, i: (bi, i, 0))
    full = lambda a: pl.BlockSpec(a.shape, lambda bi, i: (0, 0))
    weights = [p['w_out_top'], p['w_out_bot'], p['g_norm2'], p['w_router_t'], p['b_router_col']]
    cnt_shape = (N_EXPERTS, LANES)
    return pl.pallas_call(
        functools.partial(_outproj_kernel, cap=_expert_cap(b * l)),
        grid=(b, nt),
        in_specs=[tok(GLA_VAL_W), tok(MLA_W), tok(d), pl.BlockSpec((1, 6, d), mod_map)]
                 + [full(w) for w in weights],
        out_specs=[tok(d), tok(PACK_W), tok(PACK_W),
                   pl.BlockSpec((TOP_K, tm), lambda bi, i: (0, bi * nt + i)),
                   tok(LANES), pl.BlockSpec(cnt_shape, lambda bi, i: (0, 0))],
        out_shape=[jax.ShapeDtypeStruct((b, l, d), F32), jax.ShapeDtypeStruct((b, l, PACK_W), U32),
                   jax.ShapeDtypeStruct((b, l, PACK_W), U32),
                   jax.ShapeDtypeStruct((TOP_K, b * l), I32),
                   jax.ShapeDtypeStruct((b, l, LANES), U32),
                   jax.ShapeDtypeStruct(cnt_shape, F32)],
        scratch_shapes=[pltpu.VMEM(cnt_shape, F32)],
        compiler_params=pltpu.CompilerParams(
            dimension_semantics=("arbitrary", "arbitrary"), vmem_limit_bytes=VMEM_LIMIT),
        name="outproj",
    )(og, om, x, mod, *weights)


def _sc_mesh():
    return plsc.VectorSubcoreMesh(core_axis_name="c", subcore_axis_name="s")


def _sc_scatter_rows(src, slot8, n_rows_out):
    t, d = src.shape
    nk = slot8.shape[0]

    @functools.partial(pl.kernel, out_type=jax.ShapeDtypeStruct((n_rows_out, d), src.dtype),
                       mesh=_sc_mesh(), scratch_types=[], name="dispatch")
    def run(src_hbm, slot_hbm, out_hbm):
        def body(x_vmem, i_vmem):
            for k in range(nk):
                pltpu.sync_copy(x_vmem, out_hbm.at[i_vmem.at[k]])

        pltpu.emit_pipeline(
            body, grid=(t // SC_WINDOW,),
            in_specs=[pl.BlockSpec((SC_WINDOW, d), lambda i: (i, 0)),
                      pl.BlockSpec((nk, SC_WINDOW), lambda i: (0, i))],
            out_specs=[], core_axis_name=("c", "s"),
            dimension_semantics=(pltpu.PARALLEL,))(src_hbm, slot_hbm)

    return run(src, slot8)


def _sc_collect_sum(table, slots_tk, w_rep):
    n_tok = w_rep.shape[0]
    d = table.shape[1]
    group = SC_WINDOW // TOP_K
    chunks = d // SC_LANES

    @functools.partial(pl.kernel, out_type=jax.ShapeDtypeStruct((n_tok, 2 * d), F32),
                       mesh=_sc_mesh(), scratch_types=[pltpu.VMEM((SC_WINDOW, d), table.dtype)],
                       compiler_params=pltpu.CompilerParams(needs_layout_passes=False),
                       name="collect")
    def run(tab_hbm, slot_hbm, w_hbm, out_hbm, rows_v):
        def body(i_vmem, w_vmem, o_vmem):
            pltpu.sync_copy(tab_hbm.at[i_vmem.at[0]], rows_v)

            @pl.loop(0, group)
            def _(j):
                wv = [plsc.bitcast(w_vmem[j, pl.ds(k * SC_LANES, SC_LANES)], BF16)
                      for k in range(TOP_K)]

                def chunk(off):
                    prods = [plsc.bitcast(rows_v[j * TOP_K + k, pl.ds(off, SC_LANES)], BF16)
                             * wv[k] for k in range(TOP_K)]
                    lo = jnp.zeros((SC_LANES,), F32)
                    hi = jnp.zeros((SC_LANES,), F32)
                    for k in range(0, TOP_K, 2):
                        u = plsc.bitcast(prods[k] + prods[k + 1], U32)
                        lo = lo + lax.bitcast_convert_type(u << 16, F32)
                        hi = hi + lax.bitcast_convert_type(u & jnp.uint32(0xFFFF0000), F32)
                    o_vmem[j, pl.ds(off, SC_LANES)] = lo
                    o_vmem[j, pl.ds(d + off, SC_LANES)] = hi

                @plsc.parallel_loop(0, chunks, unroll=2)
                def _(c):
                    chunk(c * SC_LANES)

        pltpu.emit_pipeline(
            body, grid=(n_tok // group,),
            in_specs=[pl.BlockSpec((1, SC_WINDOW), lambda i: (0, i)),
                      pl.BlockSpec((group, TOP_K * SC_LANES), lambda i: (i, 0))],
            out_specs=[pl.BlockSpec((group, 2 * d), lambda i: (i, 0))],
            core_axis_name=("c", "s"),
            dimension_semantics=(pltpu.PARALLEL,))(slot_hbm, w_hbm, out_hbm)

    return run(table, slots_tk, w_rep)


def _ffn_kernel(nt_ref, first_ref, te_ref, tt_ref, xsa_hbm, xsb_hbm, wg_ref, wu_ref, wd_ref,
                ysa_hbm, ysb_hbm, *rest, cap, cast):
    if cast:
        wgo_ref, wuo_ref, wdo_ref, xbuf, ybuf, sem_in, sem_out = rest
        wgo_ref[0] = wg_ref[0].astype(BF16)
        wuo_ref[0] = wu_ref[0].astype(BF16)
        wdo_ref[0] = wd_ref[0].astype(BF16)
        wg_b, wu_b, wd_b = wgo_ref.at[0], wuo_ref.at[0], wdo_ref.at[0]
    else:
        xbuf, ybuf, sem_in, sem_out = rest
        wg_b, wu_b, wd_b = wg_ref.at[0], wu_ref.at[0], wd_ref.at[0]
    e = pl.program_id(0)
    n_exp = pl.num_programs(0)
    n = nt_ref[e]
    g0 = first_ref[e]
    total = first_ref[n_exp - 1] + nt_ref[n_exp - 1]
    xs_hbm = (xsa_hbm, xsb_hbm)
    ys_hbm = (ysa_hbm, ysb_hbm)
    n_in = FFN_AHEAD + 1

    def rows(ex, t):
        return pl.ds(pl.multiple_of(ex * cap + t * FFN_TILE, FFN_TILE), FFN_TILE)

    def in_copy(ex, t, slot, part):
        return pltpu.make_async_copy(xs_hbm[part].at[rows(ex, t)], xbuf.at[slot, part],
                                     sem_in.at[slot, part])

    def out_copy(ex, t, slot, part):
        return pltpu.make_async_copy(ybuf.at[slot, part], ys_hbm[part].at[rows(ex, t)],
                                     sem_out.at[slot, part])

    def start_in(g):
        for part in range(PACK_PARTS):
            in_copy(te_ref[g], tt_ref[g], g % n_in, part).start()

    def wait_out(slot):
        for part in range(PACK_PARTS):
            out_copy(e, 0, slot, part).wait()

    @pl.when(n > 0)
    def _():
        @pl.when(g0 == 0)
        def _():
            for ahead in range(FFN_AHEAD):
                @pl.when(ahead < total)
                def _():
                    start_in(ahead)

        def tile(t, carry):
            g = g0 + t
            slot = g % 2
            slot_in = g % n_in
            for part in range(PACK_PARTS):
                in_copy(e, t, slot_in, part).wait()

            @pl.when(g + FFN_AHEAD < total)
            def _():
                start_in(g + FFN_AHEAD)

            @pl.when(g >= 2)
            def _():
                wait_out(slot)

            x = _unpack_rows([xbuf[slot_in, part] for part in range(PACK_PARTS)]).astype(BF16)
            a = _silu(_dot(x, wg_b[...])) * _dot(x, wu_b[...])
            y = _pack_rows(_dot(a.astype(BF16), wd_b[...]))
            for part in range(PACK_PARTS):
                ybuf[slot, part] = y[part]
                out_copy(e, t, slot, part).start()
            return carry

        lax.fori_loop(0, n, tile, 0)

        @pl.when(g0 + n == total)
        def _():
            @pl.when(total >= 2)
            def _():
                wait_out(total % 2)

            wait_out((total - 1) % 2)


def _ffn(xs_parts, counts, w_gate, w_up, w_down):
    cast = w_gate.dtype != BF16
    n_exp, d = w_gate.shape[0], w_gate.shape[1]
    cap = xs_parts[0].shape[0] // n_exp
    ntiles = (counts + FFN_TILE - 1) // FFN_TILE
    cum = jnp.cumsum(ntiles)
    first = cum - ntiles
    max_tiles = (cap * TOP_K) // FFN_TILE + n_exp
    g = jnp.arange(max_tiles, dtype=I32)
    done = g[:, None] >= cum[None, :]
    tile_e = jnp.minimum(jnp.sum(done, axis=1), n_exp - 1).astype(I32)
    tile_t = (g - jnp.sum(jnp.where(done, ntiles[None, :], 0), axis=1)).astype(I32)
    wmap = lambda e, nt, first, te, tt: (e, 0, 0)
    hbm = pl.BlockSpec(memory_space=pl.ANY)
    w_specs = [pl.BlockSpec((1, d, D_EXPERT), wmap), pl.BlockSpec((1, d, D_EXPERT), wmap),
               pl.BlockSpec((1, D_EXPERT, d), wmap)]
    weights = (w_gate, w_up, w_down)
    outs = pl.pallas_call(
        functools.partial(_ffn_kernel, cap=cap, cast=cast),
        grid_spec=pltpu.PrefetchScalarGridSpec(
            num_scalar_prefetch=4, grid=(n_exp,),
            in_specs=[hbm] * PACK_PARTS + w_specs,
            out_specs=[hbm] * PACK_PARTS + (w_specs if cast else []),
            scratch_shapes=[pltpu.VMEM((FFN_AHEAD + 1, PACK_PARTS, FFN_TILE, PACK_W), U32),
                            pltpu.VMEM((2, PACK_PARTS, FFN_TILE, PACK_W), U32),
                            pltpu.SemaphoreType.DMA((FFN_AHEAD + 1, PACK_PARTS)),
                            pltpu.SemaphoreType.DMA((2, PACK_PARTS))]),
        out_shape=[jax.ShapeDtypeStruct(xs_parts[0].shape, U32)] * PACK_PARTS
                  + ([jax.ShapeDtypeStruct(w.shape, BF16) for w in weights] if cast else []),
        compiler_params=pltpu.CompilerParams(
            dimension_semantics=("arbitrary",), vmem_limit_bytes=VMEM_LIMIT),
        name="ffn",
    )(ntiles.astype(I32), first.astype(I32), tile_e, tile_t, *xs_parts, *weights)
    return outs[:PACK_PARTS], (tuple(outs[PACK_PARTS:]) if cast else weights)


def _combine_kernel(x1_ref, h2pa_ref, h2pb_ref, ra_ref, rb_ref, mod_ref,
                    wsg_ref, wsu_ref, wsd_ref, o_ref):
    h2 = _unpack_rows([h2pa_ref[0], h2pb_ref[0]]).astype(BF16)
    a = _silu(_dot(h2, wsg_ref[...])) * _dot(h2, wsu_ref[...])
    routed = jnp.concatenate([ra_ref[0], rb_ref[0]], axis=1)
    o_ref[0] = x1_ref[0] + mod_ref[0][5:6] * (_dot(a.astype(BF16), wsd_ref[...]) + routed)


def _combine(x1, h2p_parts, routed_parts, mod, per_batch_mod, p, tm):
    b, l, d = x1.shape
    mod_map = (lambda bi, i: (bi, 0, 0)) if per_batch_mod else (lambda bi, i: (0, 0, 0))
    tok = lambda w: pl.BlockSpec((1, tm, w), lambda bi, i: (bi, i, 0))
    full = lambda a: pl.BlockSpec(a.shape, lambda bi, i: (0, 0))
    weights = [p['w_sh_gate'], p['w_sh_up'], p['w_sh_down']]
    routed_parts = [r.reshape(b, l, d // PACK_PARTS) for r in routed_parts]
    return pl.pallas_call(
        _combine_kernel,
        grid=(b, l // tm),
        in_specs=[tok(d)] + [tok(PACK_W)] * PACK_PARTS + [tok(d // PACK_PARTS)] * PACK_PARTS
                 + [pl.BlockSpec((1, 6, d), mod_map)] + [full(w) for w in weights],
        out_specs=tok(d),
        out_shape=jax.ShapeDtypeStruct((b, l, d), F32),
        compiler_params=pltpu.CompilerParams(
            dimension_semantics=("parallel", "parallel"), vmem_limit_bytes=VMEM_LIMIT),
        name="combine",
    )(x1, *h2p_parts, *routed_parts, mod, *weights)


def _pad_heads(w, parts):
    k = w.shape[0]
    per = w.shape[1] // MLA_HEADS
    w = w.reshape(k, MLA_HEADS, per)[:, :, parts[0]:parts[1]]
    w = jnp.pad(w, ((0, 0), (0, 0), (0, HEAD_W - (parts[1] - parts[0]))))
    return w.reshape(k, MLA_HEADS * HEAD_W)


def _prep_params(l, g_norm1, g_norm2, w_in, w_gk_fwd, b_gk_fwd, w_gk_bwd, b_gk_bwd, g_gla_out,
                 g_q_lora, w_uq, g_kv_lora, w_ukv, g_qk_q, g_qk_k, w_out, w_router, b_router,
                 w_sh_gate, w_sh_up, w_sh_down):
    w = w_in[l]
    d = w.shape[0]
    o_lrf = COL_CQ
    o_lrb = o_lrf + GLA_GATE_RANK
    o_cq = o_lrb + GLA_GATE_RANK
    o_ckv = o_cq + Q_LORA
    o_kr = o_ckv + KV_LORA
    w_in_p = jnp.concatenate([
        w[:, 0:COL_CQ], w[:, o_cq:o_ckv], w[:, o_ckv:o_kr], jnp.zeros((d, KR_LANE0), w.dtype),
        w[:, o_kr:o_kr + ROPE_DIM], w[:, o_lrf:o_lrb], w[:, o_lrb:o_cq]], axis=1).astype(BF16)
    z = jnp.zeros((GLA_GATE_RANK, GLA_KEY_W), F32)
    w_gk_big = jnp.concatenate([
        jnp.zeros((LANES - 2 * GLA_GATE_RANK, 2 * GLA_KEY_W), F32),
        jnp.concatenate([w_gk_fwd[l], z], axis=1),
        jnp.concatenate([z, w_gk_bwd[l]], axis=1)], axis=0).astype(BF16)
    pad_gain = lambda g: jnp.pad(g, (0, HEAD_W - QK_DIM)).reshape(1, HEAD_W)
    w_ukv_h = w_ukv[l]
    return {
        'g_norm1': g_norm1[l].reshape(1, d), 'g_norm2': g_norm2[l].reshape(1, d),
        'w_in_p': w_in_p, 'w_gk_big': w_gk_big,
        'b_gk': jnp.concatenate([b_gk_fwd[l], b_gk_bwd[l]]).reshape(1, 2 * GLA_KEY_W),
        'g_gla_out': g_gla_out[l].reshape(1, GLA_DV),
        'g_q_lora': g_q_lora[l].reshape(1, Q_LORA),
        'w_uq_p': _pad_heads(w_uq[l], (0, QK_DIM)).astype(BF16),
        'g_kv_lora': g_kv_lora[l].reshape(1, KV_LORA),
        'w_ukv_p': jnp.concatenate([_pad_heads(w_ukv_h, (0, NOPE_DIM)),
                                    _pad_heads(w_ukv_h, (NOPE_DIM, NOPE_DIM + V_DIM))],
                                   axis=1).astype(BF16),
        'gq': pad_gain(g_qk_q[l]), 'gk': pad_gain(g_qk_k[l]),
        'w_out_top': w_out[l][:GLA_VAL_W].astype(BF16),
        'w_out_bot': w_out[l][GLA_VAL_W:].astype(BF16),
        'w_router_t': w_router[l].T, 'b_router_col': b_router[l].reshape(N_EXPERTS, 1),
        'w_sh_gate': w_sh_gate[l].astype(BF16), 'w_sh_up': w_sh_up[l].astype(BF16),
        'w_sh_down': w_sh_down[l].astype(BF16),
    }


def _rope_tables(length):
    pos = np.arange(length)
    r = (pos // GRID_W).astype(np.float32)
    c = (pos % GRID_W).astype(np.float32)
    half = ROPE_DIM // 4
    inv_freq = np.float32(ROPE_THETA) ** (-np.arange(half, dtype=np.float32) / np.float32(half))
    ang_r = r[:, None] * inv_freq[None, :]
    ang_c = c[:, None] * inv_freq[None, :]
    zeros = lambda w: np.zeros((length, w), np.float32)
    ones = lambda w: np.ones((length, w), np.float32)
    tail_w = HEAD_W - ROPE_LANE0 - ROPE_DIM
    cos = np.concatenate([ones(ROPE_LANE0), np.cos(ang_r), np.cos(ang_r), np.cos(ang_c),
                          np.cos(ang_c), ones(tail_w)], axis=1)
    s1 = np.concatenate([zeros(ROPE_LANE0), -np.sin(ang_r), zeros(half), -np.sin(ang_c),
                         zeros(half), zeros(tail_w)], axis=1)
    s2 = np.concatenate([zeros(ROPE_LANE0), zeros(half), np.sin(ang_r), zeros(half),
                         np.sin(ang_c), zeros(tail_w)], axis=1)
    return tuple(jnp.asarray(t, F32) for t in (cos, s1, s2))


def _mix_route_dispatch(x_tok, seq_shape, mod, per_batch_mod, p, s0f, s0b, rope_tabs, ctx_kv,
                        tq):
    bt, lt, d = x_tok.shape
    b, l = seq_shape
    n_tok = bt * lt
    (q, k, v, g, lgf, lgb, qm, km, vm, ckvn, kr) = _inproj(x_tok, mod, per_batch_mod, p,
                                                            rope_tabs, IN_TILE)
    seq = lambda a: a.reshape(b, l, a.shape[-1])
    og, sf, sb = _gla(seq(q), seq(k), seq(v), seq(g), seq(lgf), seq(lgb), s0f, s0b,
                      p['g_gla_out'], GLA_SEQS if l == GLA_BLOCK else 1)
    kv_pairs = [(seq(km), seq(vm))] + ([ctx_kv] if ctx_kv is not None else [])
    om = _attn(seq(qm), kv_pairs, tq)
    tokv = lambda a: a.reshape(bt, lt, a.shape[-1])
    x1, h2a, h2b, slot8, w8, cnt = _outproj(tokv(og), tokv(om), x_tok, mod, per_batch_mod, p,
                                            OUT_TILE)
    x_sorted = [_sc_scatter_rows(h.reshape(n_tok, PACK_W), slot8,
                                 N_EXPERTS * _expert_cap(n_tok)) for h in (h2a, h2b)]
    routed = dict(x1=x1, h2=(h2a, h2b), slot8=slot8, w8=w8, cnt=cnt, x_sorted=x_sorted)
    return routed, sf, sb, ckvn, kr


def _experts_collect(r, w_gate, w_up, w_down):
    n_tok = r['slot8'].shape[1]
    y_sorted, w_bf16 = _ffn(r['x_sorted'], r['cnt'][:, 0].astype(I32), w_gate, w_up, w_down)
    slots_tk = r['slot8'].T.reshape(1, TOP_K * n_tok)
    w_rep = r['w8'].reshape(n_tok, LANES)
    return [_sc_collect_sum(y, slots_tk, w_rep) for y in y_sorted], w_bf16


def kernel(x_prompt, x_sample, c, state_gla_fwd, state_gla_bwd, cache_mla_ckv, cache_mla_krope,
           c_ctx, w_ada, b_ada, g_norm1, g_norm2, w_in, w_gk_fwd, b_gk_fwd, w_gk_bwd, b_gk_bwd,
           g_gla_out, g_q_lora, w_uq, g_kv_lora, w_ukv, g_qk_q, g_qk_k, w_out,
           w_router, b_router, w_exp_gate, w_exp_up, w_exp_down, w_sh_gate, w_sh_up, w_sh_down):
    bp, lp, d = x_prompt.shape
    bs, ls, _ = x_sample.shape
    depth = w_ada.shape[0]
    xp = x_prompt.reshape(1, bp * lp, d)
    xs = x_sample
    new_f, new_b, new_ckv, new_kr = [], [], [], []
    lat_tabs = _rope_tables(ls)
    cvecs = jnp.concatenate([c_ctx[None], c, jnp.zeros((8 - 1 - bs, d), F32)], axis=0)
    for l in range(depth):
        p = _prep_params(l, g_norm1, g_norm2, w_in, w_gk_fwd, b_gk_fwd, w_gk_bwd, b_gk_bwd,
                         g_gla_out, g_q_lora, w_uq, g_kv_lora, w_ukv, g_qk_q, g_qk_k, w_out,
                         w_router, b_router, w_sh_gate, w_sh_up, w_sh_down)
        mod = _ada(cvecs, w_ada[l], b_ada[l]).reshape(8, 6, d)
        mod_p, mod_s = mod[0:1], mod[1:1 + bs]
        zeros = jnp.zeros((bp, GLA_KEY_W, GLA_DV), F32)
        r_ctx, sf, sb, ckvn, kr = _mix_route_dispatch(
            xp, (bp, lp), mod_p, False, p, zeros, zeros, (), None, lp)
        new_f.append(sf.reshape(bp, GLA_HEADS, GLA_DK, GLA_DV))
        new_b.append(sb.reshape(bp, GLA_HEADS, GLA_DK, GLA_DV))
        new_ckv.append(ckvn.reshape(bp, lp, KV_LORA))
        new_kr.append(kr.reshape(bp, lp, LANES)[:, :, KR_LANE0:KR_LANE0 + ROPE_DIM])
        kr_cache = jnp.pad(cache_mla_krope[:, l],
                           ((0, 0), (0, 0), (ROPE_LANE0, LANES - ROPE_LANE0 - ROPE_DIM)))
        ctx_kv = _cache_kv(cache_mla_ckv[:, l], kr_cache, p)
        s0f = state_gla_fwd[:, l].reshape(bs, GLA_KEY_W, GLA_DV)
        s0b = state_gla_bwd[:, l].reshape(bs, GLA_KEY_W, GLA_DV)
        r_lat, _, _, _, _ = _mix_route_dispatch(
            xs, (bs, ls), mod_s, True, p, s0f, s0b, lat_tabs, ctx_kv, 256)
        experts = (w_exp_gate[l], w_exp_up[l], w_exp_down[l])
        routed_lat, experts_bf16 = _experts_collect(r_lat, *experts)
        routed_ctx, _ = _experts_collect(r_ctx, *experts_bf16)
        xp = _combine(r_ctx['x1'], r_ctx['h2'], routed_ctx, mod_p, False, p, OUT_TILE)
        xs = _combine(r_lat['x1'], r_lat['h2'], routed_lat, mod_s, True, p, OUT_TILE)
    return (xp.reshape(bp, lp, d), xs, jnp.stack(new_f, axis=1), jnp.stack(new_b, axis=1),
            jnp.stack(new_ckv, axis=1), jnp.stack(new_kr, axis=1))
```

```python
import functools

import jax
import jax.numpy as jnp
import numpy as np
from jax import lax
from jax.experimental import pallas as pl
from jax.experimental.pallas import tpu as pltpu
from jax.experimental.pallas import tpu_sc as plsc

F32 = jnp.float32
BF16 = jnp.bfloat16
I32 = jnp.int32
U32 = jnp.uint32

D_MODEL = 1024
EPS = 1e-6
GRID_W = 64
GLA_HEADS = 4
GLA_DK = 64
GLA_DV = 128
GLA_GATE_RANK = 16
GLA_GATE_NORM = 16.0
GLA_KEY_W = GLA_HEADS * GLA_DK
GLA_VAL_W = GLA_HEADS * GLA_DV
MLA_HEADS = 4
Q_LORA = 256
KV_LORA = 128
NOPE_DIM = 64
ROPE_DIM = 32
V_DIM = 128
QK_DIM = NOPE_DIM + ROPE_DIM
ROPE_THETA = 10000.0
N_EXPERTS = 64
TOP_K = 8
N_GROUPS = 8
TOPK_GROUPS = 4
D_EXPERT = 256
ROUTED_SCALE = 2.5

LANES = 128
HEAD_W = LANES
MLA_W = MLA_HEADS * HEAD_W
ROPE_LANE0 = NOPE_DIM
COL_V = 2 * GLA_KEY_W
COL_G = COL_V + GLA_VAL_W
COL_CQ = COL_G + GLA_VAL_W
COL_CKV = COL_CQ + Q_LORA
TAIL0 = COL_CKV + KV_LORA
IN_W = TAIL0 + LANES
KR_LANE0 = LANES - ROPE_DIM - 2 * GLA_GATE_RANK
GLA_BLOCK = 256
GLA_SUB = 64
GLA_SEQS = 2
IN_TILE = 1024
IN_SUB = 256
OUT_TILE = 1024
OUT_SUB = 1024
FFN_TILE = 512
FFN_AHEAD = 5
PACK_PARTS = 2
PACK_W = D_MODEL // (2 * PACK_PARTS)
SC_WINDOW = 128
SC_LANES = 16
SC_GATHERS = 2
VMEM_LIMIT = 56 * 1024 * 1024

HIGHEST = lax.Precision.HIGHEST


def _dot(a, b, precision=None):
    return jnp.dot(a, b, preferred_element_type=F32, precision=precision)


def _dot_nt(a, b, precision=None):
    return lax.dot_general(a, b, (((1,), (1,)), ((), ())), preferred_element_type=F32,
                           precision=precision)


def _rms(x, width):
    ss = jnp.sum(x * x, axis=-1, keepdims=True) * (1.0 / width)
    return x * lax.rsqrt(ss + EPS)


def _silu(x):
    return x * jax.nn.sigmoid(x)


def _log_sigmoid(x):
    return jnp.minimum(x, 0.0) - jnp.log1p(jnp.exp(-jnp.abs(x)))


def _pack_rows(x):
    parts = []
    for i in range(PACK_PARTS):
        c0 = i * 2 * PACK_W
        lo = lax.bitcast_convert_type(x[:, c0:c0 + PACK_W].astype(BF16).astype(F32), U32)
        hi = lax.bitcast_convert_type(
            x[:, c0 + PACK_W:c0 + 2 * PACK_W].astype(BF16).astype(F32), U32)
        parts.append(hi | (lo >> 16))
    return parts


def _unpack_rows(parts):
    cols = []
    for w in parts:
        cols.append(lax.bitcast_convert_type(w << 16, F32))
        cols.append(lax.bitcast_convert_type(w & jnp.uint32(0xFFFF0000), F32))
    return jnp.concatenate(cols, axis=1)


def _ada_kernel(c_ref, w_ref, b_ref, o_ref):
    o_ref[...] = _dot(_silu(c_ref[...]), w_ref[...], precision=HIGHEST) + b_ref[...]


def _ada(cvecs, w_ada, b_ada):
    n = w_ada.shape[1]
    tn = 768
    return pl.pallas_call(
        _ada_kernel,
        grid=(n // tn,),
        in_specs=[pl.BlockSpec((8, D_MODEL), lambda j: (0, 0)),
                  pl.BlockSpec((D_MODEL, tn), lambda j: (0, j)),
                  pl.BlockSpec((1, tn), lambda j: (0, j))],
        out_specs=pl.BlockSpec((8, tn), lambda j: (0, j)),
        out_shape=jax.ShapeDtypeStruct((8, n), F32),
        name="ada",
    )(cvecs, w_ada, b_ada.reshape(1, n))


def _rope(x, c, s1, s2):
    return x * c + pltpu.roll(x, LANES - 8, 1) * s1 + pltpu.roll(x, 8, 1) * s2


def _mla_kv(ckv_n, kr, w_ukv_ref, gk, rope_tabs, k_ref, v_ref, rs=slice(None)):
    kv = _dot(ckv_n.astype(BF16), w_ukv_ref[...])
    for h in range(MLA_HEADS):
        k_h = kv[:, h * HEAD_W:(h + 1) * HEAD_W] + kr
        k_h = _rms(k_h, QK_DIM) * gk
        if rope_tabs is not None:
            k_h = _rope(k_h, *rope_tabs)
        k_ref[0, rs, h * HEAD_W:(h + 1) * HEAD_W] = k_h.astype(BF16)
    v_ref[0, rs, :] = kv[:, MLA_W:].astype(BF16)


def _lane_mask(lo, hi, rows):
    lane = lax.broadcasted_iota(I32, (rows, LANES), 1)
    return (lane >= lo) & (lane < hi)


def _inproj_kernel(x_ref, mod_ref, g1_ref, win_ref, wgk_ref, bgk_ref, gql_ref, wuq_ref,
                   gkv_ref, wukv_ref, gq_ref, gk_ref, *refs, positions):
    tab_refs, outs = (refs[:3], refs[3:]) if positions else ((), refs)
    (q_ref, k_ref, v_ref, g_ref, lgf_ref, lgb_ref, qm_ref, km_ref, vm_ref, ckvn_ref,
     kr_ref) = outs
    mod = mod_ref[0]
    for r0 in range(0, x_ref.shape[1], IN_SUB):
        rs = slice(r0, r0 + IN_SUB)
        x = x_ref[0, rs, :]
        h = _rms(x, D_MODEL) * g1_ref[...] * (1.0 + mod[1:2]) + mod[0:1]
        y = _dot(h.astype(BF16), win_ref[...])
        q_ref[0, rs, :] = (y[:, 0:GLA_KEY_W] * GLA_DK ** -0.5).astype(BF16)
        k_ref[0, rs, :] = y[:, GLA_KEY_W:COL_V].astype(BF16)
        v_ref[0, rs, :] = y[:, COL_V:COL_G].astype(BF16)
        g_ref[0, rs, :] = y[:, COL_G:COL_CQ].astype(BF16)
        tail = y[:, TAIL0:IN_W]
        pre = _dot(tail.astype(BF16), wgk_ref[...]) + bgk_ref[...]
        logg = _log_sigmoid(pre) * (1.0 / GLA_GATE_NORM)
        lgf_ref[0, rs, :] = logg[:, 0:GLA_KEY_W]
        lgb_ref[0, rs, :] = logg[:, GLA_KEY_W:]
        tabs = tuple(r[rs, :] for r in tab_refs) if positions else None
        cq = _rms(y[:, COL_CQ:COL_CKV], Q_LORA) * gql_ref[...]
        qm = _dot(cq.astype(BF16), wuq_ref[...])
        gq = gq_ref[...]
        for hh in range(MLA_HEADS):
            q_h = _rms(qm[:, hh * HEAD_W:(hh + 1) * HEAD_W], QK_DIM) * gq
            if positions:
                q_h = _rope(q_h, *tabs)
            qm_ref[0, rs, hh * HEAD_W:(hh + 1) * HEAD_W] = (q_h * QK_DIM ** -0.5).astype(BF16)
        ckv_n = _rms(y[:, COL_CKV:TAIL0], KV_LORA) * gkv_ref[...]
        ckvn_ref[0, rs, :] = ckv_n
        kr_ref[0, rs, :] = tail
        kr = jnp.where(_lane_mask(KR_LANE0, KR_LANE0 + ROPE_DIM, IN_SUB), tail, 0.0)
        _mla_kv(ckv_n, kr, wukv_ref, gk_ref[...], tabs, km_ref, vm_ref, rs)


def _inproj(x, mod, per_batch_mod, p, rope_tabs, tm):
    b, l, d = x.shape
    nt = l // tm
    mod_map = (lambda bi, i: (bi, 0, 0)) if per_batch_mod else (lambda bi, i: (0, 0, 0))
    tab_map = lambda bi, i: (i, 0)
    const = lambda bi, i: (0, 0)
    tok = lambda w: pl.BlockSpec((1, tm, w), lambda bi, i: (bi, i, 0))
    full = lambda a: pl.BlockSpec(a.shape, const)
    weights = [p['g_norm1'], p['w_in_p'], p['w_gk_big'], p['b_gk'], p['g_q_lora'], p['w_uq_p'],
               p['g_kv_lora'], p['w_ukv_p'], p['gq'], p['gk']]
    outs = [(GLA_KEY_W, BF16), (GLA_KEY_W, BF16), (GLA_VAL_W, BF16), (GLA_VAL_W, BF16),
            (GLA_KEY_W, F32), (GLA_KEY_W, F32), (MLA_W, BF16), (MLA_W, BF16), (MLA_W, BF16),
            (KV_LORA, F32), (LANES, F32)]
    return pl.pallas_call(
        functools.partial(_inproj_kernel, positions=bool(rope_tabs)),
        grid=(b, nt),
        in_specs=[tok(d), pl.BlockSpec((1, 6, d), mod_map)] + [full(w) for w in weights]
                 + [pl.BlockSpec((tm, LANES), tab_map)] * len(rope_tabs),
        out_specs=[tok(w) for w, _ in outs],
        out_shape=[jax.ShapeDtypeStruct((b, l, w), dt) for w, dt in outs],
        compiler_params=pltpu.CompilerParams(
            dimension_semantics=("parallel", "parallel"), vmem_limit_bytes=VMEM_LIMIT),
        name="inproj",
    )(x, mod, *weights, *rope_tabs)


def _cache_kv_kernel(ckv_ref, kr_ref, wukv_ref, gk_ref, k_ref, v_ref):
    _mla_kv(ckv_ref[0], kr_ref[0], wukv_ref, gk_ref[...], None, k_ref, v_ref)


def _cache_kv(ckv, kr128, p):
    b, l, _ = ckv.shape
    tok = lambda w: pl.BlockSpec((1, l, w), lambda bi: (bi, 0, 0))
    full = lambda a: pl.BlockSpec(a.shape, lambda bi: (0, 0))
    return pl.pallas_call(
        _cache_kv_kernel,
        grid=(b,),
        in_specs=[tok(KV_LORA), tok(LANES), full(p['w_ukv_p']), full(p['gk'])],
        out_specs=[tok(MLA_W), tok(MLA_W)],
        out_shape=[jax.ShapeDtypeStruct((b, l, MLA_W), BF16)] * 2,
        name="cache_kv",
    )(ckv, kr128, p['w_ukv_p'], p['gk'])


def _split3(x):
    hi = x.astype(BF16)
    r1 = x - hi.astype(F32)
    mid = r1.astype(BF16)
    lo = (r1 - mid.astype(F32)).astype(BF16)
    return hi, mid, lo


def _gla_block(q, k, v, lg, s, fwd):
    n = GLA_BLOCK
    nc = n // GLA_SUB
    row = lax.broadcasted_iota(I32, (n, n), 0)
    col = lax.broadcasted_iota(I32, (n, n), 1)
    same = (row // GLA_SUB) == (col // GLA_SUB)
    causal = same & ((col <= row) if fwd else (col >= row))
    tri = causal.astype(BF16)
    hi, mid, lo = _split3(lg)
    cum = _dot(tri, hi) + _dot(tri, mid) + _dot(tri, lo)
    tot_rows, mid_rows = [], []
    for c in range(nc):
        r_tot = c * GLA_SUB + (GLA_SUB - 1 if fwd else 0)
        r_mid = c * GLA_SUB + GLA_SUB // 2
        tot_rows.append(jnp.broadcast_to(cum[r_tot:r_tot + 1], (GLA_SUB, GLA_KEY_W)))
        mid_rows.append(jnp.broadcast_to(cum[r_mid:r_mid + 1], (GLA_SUB, GLA_KEY_W)))
    tot_rows = jnp.concatenate(tot_rows, axis=0)
    mid_rows = jnp.concatenate(mid_rows, axis=0)
    rel = cum - mid_rows
    qi = q * jnp.exp(rel)
    ki = (k * jnp.exp(-rel)).astype(BF16)
    q_in = (q * jnp.exp(cum)).astype(BF16)
    k_up = k * jnp.exp(tot_rows - cum)
    dec = jnp.exp(tot_rows)
    k_up_t = k_up.T.astype(BF16)
    dec_t = dec.T
    lane_head = lax.broadcasted_iota(I32, (n, GLA_KEY_W), 1) // GLA_DK
    srow_head = lax.broadcasted_iota(I32, (GLA_KEY_W, GLA_DV), 0) // GLA_DK
    vrow_chunk = lax.broadcasted_iota(I32, (n, GLA_VAL_W), 0) // GLA_SUB
    o_heads = []
    for h in range(GLA_HEADS):
        a = _dot_nt(jnp.where(lane_head == h, qi, 0.0).astype(BF16), ki)
        a = jnp.where(causal, a, 0.0).astype(BF16)
        o_heads.append(_dot(a, v[:, h * GLA_DV:(h + 1) * GLA_DV]))
    o = jnp.concatenate(o_heads, axis=1)
    o_inter = [None] * nc
    for c in (range(nc) if fwd else range(nc - 1, -1, -1)):
        s_bd = jnp.concatenate(
            [jnp.where(srow_head == h, s, 0.0).astype(BF16) for h in range(GLA_HEADS)], axis=1)
        o_inter[c] = _dot(q_in[c * GLA_SUB:(c + 1) * GLA_SUB], s_bd)
        v_c = jnp.where(vrow_chunk == c, v, jnp.zeros_like(v))
        u = jnp.concatenate(
            [_dot(k_up_t[h * GLA_DK:(h + 1) * GLA_DK], v_c[:, h * GLA_DV:(h + 1) * GLA_DV])
             for h in range(GLA_HEADS)], axis=0)
        s = dec_t[:, c * GLA_SUB:c * GLA_SUB + 1] * s + u
    return o + jnp.concatenate(o_inter, axis=0), s


def _gla_kernel(q_ref, k_ref, v_ref, g_ref, lgf_ref, lgb_ref, s0f_ref, s0b_ref, gout_ref,
                o_ref, sf_ref, sb_ref, acc_ref, st_ref, *, nblk, n_seq):
    for si in range(n_seq):
        _gla_sequence(si, q_ref, k_ref, v_ref, g_ref, lgf_ref, lgb_ref, s0f_ref, s0b_ref,
                      gout_ref, o_ref, sf_ref, sb_ref, acc_ref, st_ref, nblk)


def _gla_sequence(si, q_ref, k_ref, v_ref, g_ref, lgf_ref, lgb_ref, s0f_ref, s0b_ref, gout_ref,
                  o_ref, sf_ref, sb_ref, acc_ref, st_ref, nblk):
    st_ref[si, 0] = s0f_ref[si]
    st_ref[si, 1] = s0b_ref[si]

    def load(blk):
        start = blk * GLA_BLOCK
        r = pl.ds(start if isinstance(blk, int) else pl.multiple_of(start, GLA_BLOCK), GLA_BLOCK)
        return r, q_ref[si, r, :].astype(F32), k_ref[si, r, :].astype(F32), v_ref[si, r, :]

    def pair_step(i, first_touch):
        rf, qf, kf, vf = load(i)
        rb, qb, kb, vb = load(nblk - 1 - i)
        of, s_f = _gla_block(qf, kf, vf, lgf_ref[si, rf, :], st_ref[si, 0], True)
        ob, s_b = _gla_block(qb, kb, vb, lgb_ref[si, rb, :], st_ref[si, 1], False)
        if first_touch:
            acc_ref[si, rf, :] = of
            acc_ref[si, rb, :] = ob
        else:
            acc_ref[si, rf, :] += of
            acc_ref[si, rb, :] += ob
        st_ref[si, 0] = s_f
        st_ref[si, 1] = s_b

    if nblk == 1:
        r, q, k, v = load(0)
        of, s_f = _gla_block(q, k, v, lgf_ref[si, r, :], st_ref[si, 0], True)
        ob, s_b = _gla_block(q, k, v, lgb_ref[si, r, :], st_ref[si, 1], False)
        acc_ref[si, r, :] = of + ob
        st_ref[si, 0] = s_f
        st_ref[si, 1] = s_b
    else:
        half = nblk // 2
        lax.fori_loop(0, half, lambda i, c: (pair_step(i, True), c)[1], 0)
        lax.fori_loop(half, nblk, lambda i, c: (pair_step(i, False), c)[1], 0)
    sf_ref[si] = st_ref[si, 0]
    sb_ref[si] = st_ref[si, 1]

    def fin_step(blk, carry):
        r = pl.ds(pl.multiple_of(blk * GLA_BLOCK, GLA_BLOCK), GLA_BLOCK)
        o = acc_ref[si, r, :]
        gate = _silu(g_ref[si, r, :].astype(F32))
        for h in range(GLA_HEADS):
            sl = slice(h * GLA_DV, (h + 1) * GLA_DV)
            o_ref[si, r, sl] = (_rms(o[:, sl], GLA_DV) * gout_ref[...] * gate[:, sl]).astype(BF16)
        return carry

    lax.fori_loop(0, nblk, fin_step, 0)


def _gla(q, k, v, g, lgf, lgb, s0f, s0b, g_out, n_seq):
    b, l, _ = q.shape
    assert l == GLA_BLOCK or l % (2 * GLA_BLOCK) == 0
    seq = lambda w: pl.BlockSpec((n_seq, l, w), lambda bi: (bi, 0, 0))
    st = pl.BlockSpec((n_seq, GLA_KEY_W, GLA_DV), lambda bi: (bi, 0, 0))
    return pl.pallas_call(
        functools.partial(_gla_kernel, nblk=l // GLA_BLOCK, n_seq=n_seq),
        grid=(b // n_seq,),
        in_specs=[seq(GLA_KEY_W), seq(GLA_KEY_W), seq(GLA_VAL_W), seq(GLA_VAL_W),
                  seq(GLA_KEY_W), seq(GLA_KEY_W), st, st,
                  pl.BlockSpec((1, GLA_DV), lambda bi: (0, 0))],
        out_specs=[seq(GLA_VAL_W), st, st],
        out_shape=[jax.ShapeDtypeStruct((b, l, GLA_VAL_W), BF16),
                   jax.ShapeDtypeStruct((b, GLA_KEY_W, GLA_DV), F32),
                   jax.ShapeDtypeStruct((b, GLA_KEY_W, GLA_DV), F32)],
        scratch_shapes=[pltpu.VMEM((n_seq, l, GLA_VAL_W), F32),
                        pltpu.VMEM((n_seq, 2, GLA_KEY_W, GLA_DV), F32)],
        compiler_params=pltpu.CompilerParams(
            dimension_semantics=("parallel",), vmem_limit_bytes=VMEM_LIMIT),
        name="gla",
    )(q, k, v, g, lgf, lgb, s0f, s0b, g_out)


def _attn_kernel(*refs, n_kv):
    q_ref, o_ref = refs[0], refs[-1]
    kv = [(refs[1 + 2 * i], refs[2 + 2 * i]) for i in range(n_kv)]
    for h in range(MLA_HEADS):
        sl = slice(h * HEAD_W, (h + 1) * HEAD_W)
        q = q_ref[0, :, sl]
        scores = [_dot_nt(q, k_ref[0, :, sl]) for k_ref, _ in kv]
        m = functools.reduce(jnp.maximum, [jnp.max(s, axis=-1, keepdims=True) for s in scores])
        o, den = 0.0, 0.0
        for s, (_, v_ref) in zip(scores, kv):
            p = jnp.exp(s - m)
            den = den + jnp.sum(p, axis=-1, keepdims=True)
            o = o + _dot(p.astype(BF16), v_ref[0, :, sl])
        o_ref[0, :, sl] = (o / den).astype(BF16)


def _attn(q, kv_pairs, tq):
    b, l, _ = q.shape
    kv_specs, kv_args = [], []
    for k, v in kv_pairs:
        spec = pl.BlockSpec((1, k.shape[1], MLA_W), lambda bi, i: (bi, 0, 0))
        kv_specs += [spec, spec]
        kv_args += [k, v]
    return pl.pallas_call(
        functools.partial(_attn_kernel, n_kv=len(kv_pairs)),
        grid=(b, l // tq),
        in_specs=[pl.BlockSpec((1, tq, MLA_W), lambda bi, i: (bi, i, 0))] + kv_specs,
        out_specs=pl.BlockSpec((1, tq, MLA_W), lambda bi, i: (bi, i, 0)),
        out_shape=jax.ShapeDtypeStruct((b, l, MLA_W), BF16),
        compiler_params=pltpu.CompilerParams(
            dimension_semantics=("parallel", "parallel"), vmem_limit_bytes=VMEM_LIMIT),
        name="attn",
    )(q, *kv_args)


def _expert_cap(n_tok):
    return -(-n_tok // FFN_TILE) * FFN_TILE


def _sum01(x):
    return jnp.sum(jnp.sum(x, axis=1, keepdims=True), axis=0, keepdims=True)


def _route(logits_t, bias_col, cnt_col, cap):
    t = logits_t.shape[1]
    gsz = N_EXPERTS // N_GROUPS
    scores = jax.nn.sigmoid(logits_t)
    sel = (scores + bias_col).reshape(N_GROUPS, gsz, t)
    scores = scores.reshape(N_GROUPS, gsz, t)
    neg = jnp.float32(-jnp.inf)
    ie = lax.broadcasted_iota(I32, (N_GROUPS, gsz, t), 1)
    ig = lax.broadcasted_iota(I32, (N_GROUPS, gsz, t), 0)
    m1 = jnp.max(sel, axis=1, keepdims=True)
    first = jnp.min(jnp.where(sel == m1, ie, gsz), axis=1, keepdims=True)
    m2 = jnp.max(jnp.where(ie == first, neg, sel), axis=1, keepdims=True)
    grp = m1 + m2
    igk = lax.broadcasted_iota(I32, (N_GROUPS, 1, t), 0)
    g_sel = jnp.zeros((N_GROUPS, 1, t), jnp.bool_)
    cur = grp
    for _ in range(TOPK_GROUPS):
        m = jnp.max(cur, axis=0, keepdims=True)
        pick = igk == jnp.min(jnp.where(cur == m, igk, N_GROUPS), axis=0, keepdims=True)
        g_sel = g_sel | pick
        cur = jnp.where(pick, neg, cur)
    cur = jnp.where(g_sel, sel, neg)
    idx = ig * gsz + ie
    e_sel = jnp.zeros((N_GROUPS, gsz, t), jnp.bool_)
    picks = []
    for _ in range(TOP_K):
        m = jnp.max(jnp.max(cur, axis=1, keepdims=True), axis=0, keepdims=True)
        cand = jnp.where(cur == m, idx, N_EXPERTS)
        pick = idx == jnp.min(jnp.min(cand, axis=1, keepdims=True), axis=0, keepdims=True)
        picks.append(pick)
        e_sel = e_sel | pick
        cur = jnp.where(pick, neg, cur)
    w = jnp.where(e_sel, scores, 0.0)
    gate = w / _sum01(w) * ROUTED_SCALE
    sel_f = e_sel.astype(F32).reshape(N_EXPERTS, t)
    earlier = (lax.broadcasted_iota(I32, (t, t), 0) < lax.broadcasted_iota(I32, (t, t), 1))
    rank = _dot(sel_f.astype(BF16), earlier.astype(BF16))
    base = lax.broadcasted_iota(I32, (N_EXPERTS, 1), 0).astype(F32) * float(cap) + cnt_col
    slot = (base + rank).reshape(N_GROUPS, gsz, t)
    slot8 = jnp.concatenate([_sum01(jnp.where(pk, slot, 0.0)).reshape(1, t) for pk in picks],
                            axis=0).astype(I32)
    w8 = jnp.concatenate([_sum01(jnp.where(pk, gate, 0.0)).reshape(1, t) for pk in picks], axis=0)
    return slot8, w8, cnt_col + jnp.sum(sel_f, axis=1, keepdims=True)


def _outproj_kernel(og_ref, om_ref, x_ref, mod_ref, wtop_ref, wbot_ref, g2_ref, wr_ref, br_ref,
                    x1_ref, h2pa_ref, h2pb_ref, slot_ref, w8_ref, cnt_ref, cnt_scr, *, cap):
    @pl.when((pl.program_id(0) == 0) & (pl.program_id(1) == 0))
    def _():
        cnt_scr[...] = jnp.zeros_like(cnt_scr)

    mod = mod_ref[0]
    cnt = cnt_scr[:, 0:1]
    for r0 in range(0, x_ref.shape[1], OUT_SUB):
        rs = slice(r0, r0 + OUT_SUB)
        mix = _dot(og_ref[0, rs, :], wtop_ref[...]) + _dot(om_ref[0, rs, :], wbot_ref[...])
        x1 = x_ref[0, rs, :] + mod[2:3] * mix
        x1_ref[0, rs, :] = x1
        h2 = _rms(x1, D_MODEL) * g2_ref[...] * (1.0 + mod[4:5]) + mod[3:4]
        h2pa_ref[0, rs, :], h2pb_ref[0, rs, :] = _pack_rows(h2)
        logits_t = _dot_nt(wr_ref[...], h2, precision=HIGHEST)
        slot8, w8, cnt = _route(logits_t, br_ref[...], cnt, cap)
        slot_ref[:, rs] = slot8
        wb = lax.bitcast_convert_type(w8.astype(BF16).astype(F32), U32)
        wb = wb | (wb >> 16)
        w8_ref[0, rs, :] = jnp.concatenate(
            [jnp.broadcast_to(wb[k:k + 1], (SC_LANES, OUT_SUB)) for k in range(TOP_K)], axis=0).T
    cnt_scr[...] = jnp.broadcast_to(cnt, cnt_scr.shape)
    cnt_ref[...] = cnt_scr[...]


def _outproj(og, om, x, mod, per_batch_mod, p, tm):
    b, l, d = x.shape
    nt = l // tm
    mod_map = (lambda bi, i: (bi, 0, 0)) if per_batch_mod else (lambda bi, i: (0, 0, 0))
    tok = lambda w: pl.BlockSpec((1, tm, w), lambda bi, i: (bi, i, 0))
    full = lambda a: pl.BlockSpec(a.shape, lambda bi, i: (0, 0))
    weights = [p['w_out_top'], p['w_out_bot'], p['g_norm2'], p['w_router_t'], p['b_router_col']]
    cnt_shape = (N_EXPERTS, LANES)
    return pl.pallas_call(
        functools.partial(_outproj_kernel, cap=_expert_cap(b * l)),
        grid=(b, nt),
        in_specs=[tok(GLA_VAL_W), tok(MLA_W), tok(d), pl.BlockSpec((1, 6, d), mod_map)]
                 + [full(w) for w in weights],
        out_specs=[tok(d), tok(PACK_W), tok(PACK_W),
                   pl.BlockSpec((TOP_K, tm), lambda bi, i: (0, bi * nt + i)),
                   tok(LANES), pl.BlockSpec(cnt_shape, lambda bi, i: (0, 0))],
        out_shape=[jax.ShapeDtypeStruct((b, l, d), F32), jax.ShapeDtypeStruct((b, l, PACK_W), U32),
                   jax.ShapeDtypeStruct((b, l, PACK_W), U32),
                   jax.ShapeDtypeStruct((TOP_K, b * l), I32),
                   jax.ShapeDtypeStruct((b, l, LANES), U32),
                   jax.ShapeDtypeStruct(cnt_shape, F32)],
        scratch_shapes=[pltpu.VMEM(cnt_shape, F32)],
        compiler_params=pltpu.CompilerParams(
            dimension_semantics=("arbitrary", "arbitrary"), vmem_limit_bytes=VMEM_LIMIT),
        name="outproj",
    )(og, om, x, mod, *weights)


def _sc_mesh():
    return plsc.VectorSubcoreMesh(core_axis_name="c", subcore_axis_name="s")


def _sc_scatter_rows(src, slot8, n_rows_out):
    t, d = src.shape
    nk = slot8.shape[0]

    @functools.partial(pl.kernel, out_type=jax.ShapeDtypeStruct((n_rows_out, d), src.dtype),
                       mesh=_sc_mesh(), scratch_types=[], name="dispatch")
    def run(src_hbm, slot_hbm, out_hbm):
        def body(x_vmem, i_vmem):
            for k in range(nk):
                pltpu.sync_copy(x_vmem, out_hbm.at[i_vmem.at[k]])

        pltpu.emit_pipeline(
            body, grid=(t // SC_WINDOW,),
            in_specs=[pl.BlockSpec((SC_WINDOW, d), lambda i: (i, 0)),
                      pl.BlockSpec((nk, SC_WINDOW), lambda i: (0, i))],
            out_specs=[], core_axis_name=("c", "s"),
            dimension_semantics=(pltpu.PARALLEL,))(src_hbm, slot_hbm)

    return run(src, slot8)


def _sc_collect_sum(table, slots_tk, w_rep):
    n_tok = w_rep.shape[0]
    d = table.shape[1]
    group = SC_WINDOW // TOP_K
    chunks = d // SC_LANES

    @functools.partial(pl.kernel, out_type=jax.ShapeDtypeStruct((n_tok, 2 * d), F32),
                       mesh=_sc_mesh(),
                       scratch_types=[pltpu.VMEM((SC_GATHERS, SC_WINDOW, d), table.dtype),
                                      pltpu.SemaphoreType.DMA((SC_GATHERS,))],
                       compiler_params=pltpu.CompilerParams(needs_layout_passes=False),
                       name="collect")
    def run(tab_hbm, slot_hbm, w_hbm, out_hbm, rows_v, sems):
        def gather(i_vmem, b):
            return pltpu.make_async_copy(tab_hbm.at[i_vmem.at[b]], rows_v.at[b], sems.at[b])

        def body(i_vmem, w_vmem, o_vmem):
            for b in range(SC_GATHERS):
                gather(i_vmem, b).start()
            for b in range(SC_GATHERS):
                gather(i_vmem, b).wait()

                @pl.loop(0, group)
                def _(j):
                    tok = b * group + j
                    wv = [plsc.bitcast(w_vmem[tok, pl.ds(k * SC_LANES, SC_LANES)], BF16)
                          for k in range(TOP_K)]

                    def chunk(off):
                        prods = [plsc.bitcast(
                            rows_v[b, j * TOP_K + k, pl.ds(off, SC_LANES)], BF16) * wv[k]
                            for k in range(TOP_K)]
                        lo = jnp.zeros((SC_LANES,), F32)
                        hi = jnp.zeros((SC_LANES,), F32)
                        for k in range(0, TOP_K, 2):
                            u = plsc.bitcast(prods[k] + prods[k + 1], U32)
                            lo = lo + lax.bitcast_convert_type(u << 16, F32)
                            hi = hi + lax.bitcast_convert_type(u & jnp.uint32(0xFFFF0000), F32)
                        o_vmem[tok, pl.ds(off, SC_LANES)] = lo
                        o_vmem[tok, pl.ds(d + off, SC_LANES)] = hi

                    @plsc.parallel_loop(0, chunks, unroll=2)
                    def _(c):
                        chunk(c * SC_LANES)

        step_tok = SC_GATHERS * group
        pltpu.emit_pipeline(
            body, grid=(n_tok // step_tok,),
            in_specs=[pl.BlockSpec((SC_GATHERS, SC_WINDOW), lambda i: (i, 0)),
                      pl.BlockSpec((step_tok, TOP_K * SC_LANES), lambda i: (i, 0))],
            out_specs=[pl.BlockSpec((step_tok, 2 * d), lambda i: (i, 0))],
            core_axis_name=("c", "s"),
            dimension_semantics=(pltpu.PARALLEL,))(slot_hbm, w_hbm, out_hbm)

    return run(table, slots_tk.reshape(-1, SC_WINDOW), w_rep)


def _ffn_kernel(nt_ref, first_ref, te_ref, tt_ref, xsa_hbm, xsb_hbm, wg_ref, wu_ref, wd_ref,
                ysa_hbm, ysb_hbm, xbuf, ybuf, wg_b, wu_b, wd_b, sem_in, sem_out, *, cap):
    e = pl.program_id(0)
    n_exp = pl.num_programs(0)
    n = nt_ref[e]
    g0 = first_ref[e]
    total = first_ref[n_exp - 1] + nt_ref[n_exp - 1]
    xs_hbm = (xsa_hbm, xsb_hbm)
    ys_hbm = (ysa_hbm, ysb_hbm)
    n_in = FFN_AHEAD + 1

    def rows(ex, t):
        return pl.ds(pl.multiple_of(ex * cap + t * FFN_TILE, FFN_TILE), FFN_TILE)

    def in_copy(ex, t, slot, part):
        return pltpu.make_async_copy(xs_hbm[part].at[rows(ex, t)], xbuf.at[slot, part],
                                     sem_in.at[slot, part])

    def out_copy(ex, t, slot, part):
        return pltpu.make_async_copy(ybuf.at[slot, part], ys_hbm[part].at[rows(ex, t)],
                                     sem_out.at[slot, part])

    def start_in(g):
        for part in range(PACK_PARTS):
            in_copy(te_ref[g], tt_ref[g], g % n_in, part).start()

    def wait_out(slot):
        for part in range(PACK_PARTS):
            out_copy(e, 0, slot, part).wait()

    @pl.when(n > 0)
    def _():
        @pl.when(g0 == 0)
        def _():
            for ahead in range(FFN_AHEAD):
                @pl.when(ahead < total)
                def _():
                    start_in(ahead)

        wg_b[...] = wg_ref[0].astype(BF16)
        wu_b[...] = wu_ref[0].astype(BF16)
        wd_b[...] = wd_ref[0].astype(BF16)

        def tile(t, carry):
            g = g0 + t
            slot = g % 2
            slot_in = g % n_in
            for part in range(PACK_PARTS):
                in_copy(e, t, slot_in, part).wait()

            @pl.when(g + FFN_AHEAD < total)
            def _():
                start_in(g + FFN_AHEAD)

            @pl.when(g >= 2)
            def _():
                wait_out(slot)

            x = _unpack_rows([xbuf[slot_in, part] for part in range(PACK_PARTS)]).astype(BF16)
            a = _silu(_dot(x, wg_b[...])) * _dot(x, wu_b[...])
            y = _pack_rows(_dot(a.astype(BF16), wd_b[...]))
            for part in range(PACK_PARTS):
                ybuf[slot, part] = y[part]
                out_copy(e, t, slot, part).start()
            return carry

        lax.fori_loop(0, n, tile, 0)

        @pl.when(g0 + n == total)
        def _():
            @pl.when(total >= 2)
            def _():
                wait_out(total % 2)

            wait_out((total - 1) % 2)


def _ffn(xs_parts, counts, w_gate, w_up, w_down):
    n_exp, d = w_gate.shape[0], w_gate.shape[1]
    cap = xs_parts[0].shape[0] // n_exp
    ntiles = (counts + FFN_TILE - 1) // FFN_TILE
    cum = jnp.cumsum(ntiles)
    first = cum - ntiles
    max_tiles = (cap * TOP_K) // FFN_TILE + n_exp
    g = jnp.arange(max_tiles, dtype=I32)
    done = g[:, None] >= cum[None, :]
    tile_e = jnp.minimum(jnp.sum(done, axis=1), n_exp - 1).astype(I32)
    tile_t = (g - jnp.sum(jnp.where(done, ntiles[None, :], 0), axis=1)).astype(I32)
    wmap = lambda e, nt, first, te, tt: (e, 0, 0)
    hbm = pl.BlockSpec(memory_space=pl.ANY)
    return pl.pallas_call(
        functools.partial(_ffn_kernel, cap=cap),
        grid_spec=pltpu.PrefetchScalarGridSpec(
            num_scalar_prefetch=4, grid=(n_exp,),
            in_specs=[hbm] * PACK_PARTS
                     + [pl.BlockSpec((1, d, D_EXPERT), wmap), pl.BlockSpec((1, d, D_EXPERT), wmap),
                        pl.BlockSpec((1, D_EXPERT, d), wmap)],
            out_specs=[hbm] * PACK_PARTS,
            scratch_shapes=[pltpu.VMEM((FFN_AHEAD + 1, PACK_PARTS, FFN_TILE, PACK_W), U32),
                            pltpu.VMEM((2, PACK_PARTS, FFN_TILE, PACK_W), U32),
                            pltpu.VMEM((d, D_EXPERT), BF16), pltpu.VMEM((d, D_EXPERT), BF16),
                            pltpu.VMEM((D_EXPERT, d), BF16),
                            pltpu.SemaphoreType.DMA((FFN_AHEAD + 1, PACK_PARTS)),
                            pltpu.SemaphoreType.DMA((2, PACK_PARTS))]),
        out_shape=[jax.ShapeDtypeStruct(xs_parts[0].shape, U32)] * PACK_PARTS,
        compiler_params=pltpu.CompilerParams(
            dimension_semantics=("arbitrary",), vmem_limit_bytes=VMEM_LIMIT),
        name="ffn",
    )(ntiles.astype(I32), first.astype(I32), tile_e, tile_t, *xs_parts, w_gate, w_up, w_down)


def _combine_kernel(x1_ref, h2pa_ref, h2pb_ref, ra_ref, rb_ref, mod_ref,
                    wsg_ref, wsu_ref, wsd_ref, o_ref):
    h2 = _unpack_rows([h2pa_ref[0], h2pb_ref[0]]).astype(BF16)
    a = _silu(_dot(h2, wsg_ref[...])) * _dot(h2, wsu_ref[...])
    routed = jnp.concatenate([ra_ref[0], rb_ref[0]], axis=1)
    o_ref[0] = x1_ref[0] + mod_ref[0][5:6] * (_dot(a.astype(BF16), wsd_ref[...]) + routed)


def _combine(x1, h2p_parts, routed_parts, mod, per_batch_mod, p, tm):
    b, l, d = x1.shape
    mod_map = (lambda bi, i: (bi, 0, 0)) if per_batch_mod else (lambda bi, i: (0, 0, 0))
    tok = lambda w: pl.BlockSpec((1, tm, w), lambda bi, i: (bi, i, 0))
    full = lambda a: pl.BlockSpec(a.shape, lambda bi, i: (0, 0))
    weights = [p['w_sh_gate'], p['w_sh_up'], p['w_sh_down']]
    routed_parts = [r.reshape(b, l, d // PACK_PARTS) for r in routed_parts]
    return pl.pallas_call(
        _combine_kernel,
        grid=(b, l // tm),
        in_specs=[tok(d)] + [tok(PACK_W)] * PACK_PARTS + [tok(d // PACK_PARTS)] * PACK_PARTS
                 + [pl.BlockSpec((1, 6, d), mod_map)] + [full(w) for w in weights],
        out_specs=tok(d),
        out_shape=jax.ShapeDtypeStruct((b, l, d), F32),
        compiler_params=pltpu.CompilerParams(
            dimension_semantics=("parallel", "parallel"), vmem_limit_bytes=VMEM_LIMIT),
        name="combine",
    )(x1, *h2p_parts, *routed_parts, mod, *weights)


def _pad_heads(w, parts):
    k = w.shape[0]
    per = w.shape[1] // MLA_HEADS
    w = w.reshape(k, MLA_HEADS, per)[:, :, parts[0]:parts[1]]
    w = jnp.pad(w, ((0, 0), (0, 0), (0, HEAD_W - (parts[1] - parts[0]))))
    return w.reshape(k, MLA_HEADS * HEAD_W)


def _prep_params(l, g_norm1, g_norm2, w_in, w_gk_fwd, b_gk_fwd, w_gk_bwd, b_gk_bwd, g_gla_out,
                 g_q_lora, w_uq, g_kv_lora, w_ukv, g_qk_q, g_qk_k, w_out, w_router, b_router,
                 w_sh_gate, w_sh_up, w_sh_down):
    w = w_in[l]
    d = w.shape[0]
    o_lrf = COL_CQ
    o_lrb = o_lrf + GLA_GATE_RANK
    o_cq = o_lrb + GLA_GATE_RANK
    o_ckv = o_cq + Q_LORA
    o_kr = o_ckv + KV_LORA
    w_in_p = jnp.concatenate([
        w[:, 0:COL_CQ], w[:, o_cq:o_ckv], w[:, o_ckv:o_kr], jnp.zeros((d, KR_LANE0), w.dtype),
        w[:, o_kr:o_kr + ROPE_DIM], w[:, o_lrf:o_lrb], w[:, o_lrb:o_cq]], axis=1).astype(BF16)
    z = jnp.zeros((GLA_GATE_RANK, GLA_KEY_W), F32)
    w_gk_big = jnp.concatenate([
        jnp.zeros((LANES - 2 * GLA_GATE_RANK, 2 * GLA_KEY_W), F32),
        jnp.concatenate([w_gk_fwd[l], z], axis=1),
        jnp.concatenate([z, w_gk_bwd[l]], axis=1)], axis=0).astype(BF16)
    pad_gain = lambda g: jnp.pad(g, (0, HEAD_W - QK_DIM)).reshape(1, HEAD_W)
    w_ukv_h = w_ukv[l]
    return {
        'g_norm1': g_norm1[l].reshape(1, d), 'g_norm2': g_norm2[l].reshape(1, d),
        'w_in_p': w_in_p, 'w_gk_big': w_gk_big,
        'b_gk': jnp.concatenate([b_gk_fwd[l], b_gk_bwd[l]]).reshape(1, 2 * GLA_KEY_W),
        'g_gla_out': g_gla_out[l].reshape(1, GLA_DV),
        'g_q_lora': g_q_lora[l].reshape(1, Q_LORA),
        'w_uq_p': _pad_heads(w_uq[l], (0, QK_DIM)).astype(BF16),
        'g_kv_lora': g_kv_lora[l].reshape(1, KV_LORA),
        'w_ukv_p': jnp.concatenate([_pad_heads(w_ukv_h, (0, NOPE_DIM)),
                                    _pad_heads(w_ukv_h, (NOPE_DIM, NOPE_DIM + V_DIM))],
                                   axis=1).astype(BF16),
        'gq': pad_gain(g_qk_q[l]), 'gk': pad_gain(g_qk_k[l]),
        'w_out_top': w_out[l][:GLA_VAL_W].astype(BF16),
        'w_out_bot': w_out[l][GLA_VAL_W:].astype(BF16),
        'w_router_t': w_router[l].T, 'b_router_col': b_router[l].reshape(N_EXPERTS, 1),
        'w_sh_gate': w_sh_gate[l].astype(BF16), 'w_sh_up': w_sh_up[l].astype(BF16),
        'w_sh_down': w_sh_down[l].astype(BF16),
    }


def _rope_tables(length):
    pos = np.arange(length)
    r = (pos // GRID_W).astype(np.float32)
    c = (pos % GRID_W).astype(np.float32)
    half = ROPE_DIM // 4
    inv_freq = np.float32(ROPE_THETA) ** (-np.arange(half, dtype=np.float32) / np.float32(half))
    ang_r = r[:, None] * inv_freq[None, :]
    ang_c = c[:, None] * inv_freq[None, :]
    zeros = lambda w: np.zeros((length, w), np.float32)
    ones = lambda w: np.ones((length, w), np.float32)
    tail_w = HEAD_W - ROPE_LANE0 - ROPE_DIM
    cos = np.concatenate([ones(ROPE_LANE0), np.cos(ang_r), np.cos(ang_r), np.cos(ang_c),
                          np.cos(ang_c), ones(tail_w)], axis=1)
    s1 = np.concatenate([zeros(ROPE_LANE0), -np.sin(ang_r), zeros(half), -np.sin(ang_c),
                         zeros(half), zeros(tail_w)], axis=1)
    s2 = np.concatenate([zeros(ROPE_LANE0), zeros(half), np.sin(ang_r), zeros(half),
                         np.sin(ang_c), zeros(tail_w)], axis=1)
    return tuple(jnp.asarray(t, F32) for t in (cos, s1, s2))


def _mix_route_dispatch(x_tok, seq_shape, mod, per_batch_mod, p, s0f, s0b, rope_tabs, ctx_kv,
                        tq):
    bt, lt, d = x_tok.shape
    b, l = seq_shape
    n_tok = bt * lt
    (q, k, v, g, lgf, lgb, qm, km, vm, ckvn, kr) = _inproj(x_tok, mod, per_batch_mod, p,
                                                            rope_tabs, IN_TILE)
    seq = lambda a: a.reshape(b, l, a.shape[-1])
    og, sf, sb = _gla(seq(q), seq(k), seq(v), seq(g), seq(lgf), seq(lgb), s0f, s0b,
                      p['g_gla_out'], GLA_SEQS if l == GLA_BLOCK else 1)
    kv_pairs = [(seq(km), seq(vm))] + ([ctx_kv] if ctx_kv is not None else [])
    om = _attn(seq(qm), kv_pairs, tq)
    tokv = lambda a: a.reshape(bt, lt, a.shape[-1])
    x1, h2a, h2b, slot8, w8, cnt = _outproj(tokv(og), tokv(om), x_tok, mod, per_batch_mod, p,
                                            OUT_TILE)
    x_sorted = [_sc_scatter_rows(h.reshape(n_tok, PACK_W), slot8,
                                 N_EXPERTS * _expert_cap(n_tok)) for h in (h2a, h2b)]
    routed = dict(x1=x1, h2=(h2a, h2b), slot8=slot8, w8=w8, cnt=cnt, x_sorted=x_sorted)
    return routed, sf, sb, ckvn, kr


def _experts_collect(r, w_gate, w_up, w_down):
    n_tok = r['slot8'].shape[1]
    y_sorted = _ffn(r['x_sorted'], r['cnt'][:, 0].astype(I32), w_gate, w_up, w_down)
    slots_tk = r['slot8'].T.reshape(1, TOP_K * n_tok)
    w_rep = r['w8'].reshape(n_tok, LANES)
    return [_sc_collect_sum(y, slots_tk, w_rep) for y in y_sorted]


def kernel(x_prompt, x_sample, c, state_gla_fwd, state_gla_bwd, cache_mla_ckv, cache_mla_krope,
           c_ctx, w_ada, b_ada, g_norm1, g_norm2, w_in, w_gk_fwd, b_gk_fwd, w_gk_bwd, b_gk_bwd,
           g_gla_out, g_q_lora, w_uq, g_kv_lora, w_ukv, g_qk_q, g_qk_k, w_out,
           w_router, b_router, w_exp_gate, w_exp_up, w_exp_down, w_sh_gate, w_sh_up, w_sh_down):
    bp, lp, d = x_prompt.shape
    bs, ls, _ = x_sample.shape
    depth = w_ada.shape[0]
    xp = x_prompt.reshape(1, bp * lp, d)
    xs = x_sample
    new_f, new_b, new_ckv, new_kr = [], [], [], []
    lat_tabs = _rope_tables(ls)
    cvecs = jnp.concatenate([c_ctx[None], c, jnp.zeros((8 - 1 - bs, d), F32)], axis=0)
    for l in range(depth):
        p = _prep_params(l, g_norm1, g_norm2, w_in, w_gk_fwd, b_gk_fwd, w_gk_bwd, b_gk_bwd,
                         g_gla_out, g_q_lora, w_uq, g_kv_lora, w_ukv, g_qk_q, g_qk_k, w_out,
                         w_router, b_router, w_sh_gate, w_sh_up, w_sh_down)
        mod = _ada(cvecs, w_ada[l], b_ada[l]).reshape(8, 6, d)
        mod_p, mod_s = mod[0:1], mod[1:1 + bs]
        zeros = jnp.zeros((bp, GLA_KEY_W, GLA_DV), F32)
        r_ctx, sf, sb, ckvn, kr = _mix_route_dispatch(
            xp, (bp, lp), mod_p, False, p, zeros, zeros, (), None, lp)
        new_f.append(sf.reshape(bp, GLA_HEADS, GLA_DK, GLA_DV))
        new_b.append(sb.reshape(bp, GLA_HEADS, GLA_DK, GLA_DV))
        new_ckv.append(ckvn.reshape(bp, lp, KV_LORA))
        new_kr.append(kr.reshape(bp, lp, LANES)[:, :, KR_LANE0:KR_LANE0 + ROPE_DIM])
        kr_cache = jnp.pad(cache_mla_krope[:, l],
                           ((0, 0), (0, 0), (ROPE_LANE0, LANES - ROPE_LANE0 - ROPE_DIM)))
        ctx_kv = _cache_kv(cache_mla_ckv[:, l], kr_cache, p)
        s0f = state_gla_fwd[:, l].reshape(bs, GLA_KEY_W, GLA_DV)
        s0b = state_gla_bwd[:, l].reshape(bs, GLA_KEY_W, GLA_DV)
        r_lat, _, _, _, _ = _mix_route_dispatch(
            xs, (bs, ls), mod_s, True, p, s0f, s0b, lat_tabs, ctx_kv, 256)
        experts = (w_exp_gate[l], w_exp_up[l], w_exp_down[l])
        routed_ctx = _experts_collect(r_ctx, *experts)
        routed_lat = _experts_collect(r_lat, *experts)
        xp = _combine(r_ctx['x1'], r_ctx['h2'], routed_ctx, mod_p, False, p, OUT_TILE)
        xs = _combine(r_lat['x1'], r_lat['h2'], routed_lat, mod_s, True, p, OUT_TILE)
    return (xp.reshape(bp, lp, d), xs, jnp.stack(new_f, axis=1), jnp.stack(new_b, axis=1),
            jnp.stack(new_ckv, axis=1), jnp.stack(new_kr, axis=1))
```

```python
import functools

import jax
import jax.numpy as jnp
import numpy as np
from jax import lax
from jax.experimental import pallas as pl
from jax.experimental.pallas import tpu as pltpu
from jax.experimental.pallas import tpu_sc as plsc

F32 = jnp.float32
BF16 = jnp.bfloat16
I32 = jnp.int32
U32 = jnp.uint32

D_MODEL = 1024
EPS = 1e-6
GRID_W = 64
GLA_HEADS = 4
GLA_DK = 64
GLA_DV = 128
GLA_GATE_RANK = 16
GLA_GATE_NORM = 16.0
GLA_KEY_W = GLA_HEADS * GLA_DK
GLA_VAL_W = GLA_HEADS * GLA_DV
MLA_HEADS = 4
Q_LORA = 256
KV_LORA = 128
NOPE_DIM = 64
ROPE_DIM = 32
V_DIM = 128
QK_DIM = NOPE_DIM + ROPE_DIM
ROPE_THETA = 10000.0
N_EXPERTS = 64
TOP_K = 8
N_GROUPS = 8
TOPK_GROUPS = 4
D_EXPERT = 256
ROUTED_SCALE = 2.5

LANES = 128
HEAD_W = LANES
MLA_W = MLA_HEADS * HEAD_W
ROPE_LANE0 = NOPE_DIM
COL_V = 2 * GLA_KEY_W
COL_G = COL_V + GLA_VAL_W
COL_CQ = COL_G + GLA_VAL_W
COL_CKV = COL_CQ + Q_LORA
TAIL0 = COL_CKV + KV_LORA
IN_W = TAIL0 + LANES
KR_LANE0 = LANES - ROPE_DIM - 2 * GLA_GATE_RANK
GLA_BLOCK = 256
GLA_SUB = 64
GLA_SEQS = 2
IN_TILE = 1024
IN_SUB = 256
OUT_TILE = 1024
OUT_SUB = 1024
FFN_TILE = 512
FFN_AHEAD = 5
PACK_PARTS = 2
PACK_W = D_MODEL // (2 * PACK_PARTS)
SC_WINDOW = 128
SC_LANES = 16
SC_GATHERS = 2
VMEM_LIMIT = 56 * 1024 * 1024

HIGHEST = lax.Precision.HIGHEST


def _dot(a, b, precision=None):
    return jnp.dot(a, b, preferred_element_type=F32, precision=precision)


def _dot_nt(a, b, precision=None):
    return lax.dot_general(a, b, (((1,), (1,)), ((), ())), preferred_element_type=F32,
                           precision=precision)


def _rms(x, width):
    ss = jnp.sum(x * x, axis=-1, keepdims=True) * (1.0 / width)
    return x * lax.rsqrt(ss + EPS)


def _silu(x):
    return x * jax.nn.sigmoid(x)


def _log_sigmoid(x):
    return jnp.minimum(x, 0.0) - jnp.log1p(jnp.exp(-jnp.abs(x)))


def _pack_rows(x):
    parts = []
    for i in range(PACK_PARTS):
        c0 = i * 2 * PACK_W
        lo = lax.bitcast_convert_type(x[:, c0:c0 + PACK_W].astype(BF16).astype(F32), U32)
        hi = lax.bitcast_convert_type(
            x[:, c0 + PACK_W:c0 + 2 * PACK_W].astype(BF16).astype(F32), U32)
        parts.append(hi | (lo >> 16))
    return parts


def _unpack_rows(parts):
    cols = []
    for w in parts:
        cols.append(lax.bitcast_convert_type(w << 16, F32))
        cols.append(lax.bitcast_convert_type(w & jnp.uint32(0xFFFF0000), F32))
    return jnp.concatenate(cols, axis=1)


def _ada_kernel(c_ref, w_ref, b_ref, o_ref):
    o_ref[...] = _dot(_silu(c_ref[...]), w_ref[...], precision=HIGHEST) + b_ref[...]


def _ada(cvecs, w_ada, b_ada):
    n = w_ada.shape[1]
    tn = 768
    return pl.pallas_call(
        _ada_kernel,
        grid=(n // tn,),
        in_specs=[pl.BlockSpec((8, D_MODEL), lambda j: (0, 0)),
                  pl.BlockSpec((D_MODEL, tn), lambda j: (0, j)),
                  pl.BlockSpec((1, tn), lambda j: (0, j))],
        out_specs=pl.BlockSpec((8, tn), lambda j: (0, j)),
        out_shape=jax.ShapeDtypeStruct((8, n), F32),
        name="ada",
    )(cvecs, w_ada, b_ada.reshape(1, n))


def _rope(x, c, s1, s2):
    return x * c + pltpu.roll(x, LANES - 8, 1) * s1 + pltpu.roll(x, 8, 1) * s2


def _mla_kv(ckv_n, kr, w_ukv_ref, gk, rope_tabs, k_ref, v_ref, rs=slice(None)):
    kv = _dot(ckv_n.astype(BF16), w_ukv_ref[...])
    for h in range(MLA_HEADS):
        k_h = kv[:, h * HEAD_W:(h + 1) * HEAD_W] + kr
        k_h = _rms(k_h, QK_DIM) * gk
        if rope_tabs is not None:
            k_h = _rope(k_h, *rope_tabs)
        k_ref[0, rs, h * HEAD_W:(h + 1) * HEAD_W] = k_h.astype(BF16)
    v_ref[0, rs, :] = kv[:, MLA_W:].astype(BF16)


def _lane_mask(lo, hi, rows):
    lane = lax.broadcasted_iota(I32, (rows, LANES), 1)
    return (lane >= lo) & (lane < hi)


def _inproj_kernel(x_ref, mod_ref, g1_ref, win_ref, wgk_ref, bgk_ref, gql_ref, wuq_ref,
                   gkv_ref, wukv_ref, gq_ref, gk_ref, *refs, positions):
    tab_refs, outs = (refs[:3], refs[3:]) if positions else ((), refs)
    (q_ref, k_ref, v_ref, g_ref, lgf_ref, lgb_ref, qm_ref, km_ref, vm_ref, ckvn_ref,
     kr_ref) = outs
    mod = mod_ref[0]
    for r0 in range(0, x_ref.shape[1], IN_SUB):
        rs = slice(r0, r0 + IN_SUB)
        x = x_ref[0, rs, :]
        h = _rms(x, D_MODEL) * g1_ref[...] * (1.0 + mod[1:2]) + mod[0:1]
        y = _dot(h.astype(BF16), win_ref[...])
        q_ref[0, rs, :] = (y[:, 0:GLA_KEY_W] * GLA_DK ** -0.5).astype(BF16)
        k_ref[0, rs, :] = y[:, GLA_KEY_W:COL_V].astype(BF16)
        v_ref[0, rs, :] = y[:, COL_V:COL_G].astype(BF16)
        g_ref[0, rs, :] = y[:, COL_G:COL_CQ].astype(BF16)
        tail = y[:, TAIL0:IN_W]
        pre = _dot(tail.astype(BF16), wgk_ref[...]) + bgk_ref[...]
        logg = _log_sigmoid(pre) * (1.0 / GLA_GATE_NORM)
        lgf_ref[0, rs, :] = logg[:, 0:GLA_KEY_W]
        lgb_ref[0, rs, :] = logg[:, GLA_KEY_W:]
        tabs = tuple(r[rs, :] for r in tab_refs) if positions else None
        cq = _rms(y[:, COL_CQ:COL_CKV], Q_LORA) * gql_ref[...]
        qm = _dot(cq.astype(BF16), wuq_ref[...])
        gq = gq_ref[...]
        for hh in range(MLA_HEADS):
            q_h = _rms(qm[:, hh * HEAD_W:(hh + 1) * HEAD_W], QK_DIM) * gq
            if positions:
                q_h = _rope(q_h, *tabs)
            qm_ref[0, rs, hh * HEAD_W:(hh + 1) * HEAD_W] = (q_h * QK_DIM ** -0.5).astype(BF16)
        ckv_n = _rms(y[:, COL_CKV:TAIL0], KV_LORA) * gkv_ref[...]
        ckvn_ref[0, rs, :] = ckv_n
        kr_ref[0, rs, :] = tail
        kr = jnp.where(_lane_mask(KR_LANE0, KR_LANE0 + ROPE_DIM, IN_SUB), tail, 0.0)
        _mla_kv(ckv_n, kr, wukv_ref, gk_ref[...], tabs, km_ref, vm_ref, rs)


def _inproj(x, mod, per_batch_mod, p, rope_tabs, tm):
    b, l, d = x.shape
    nt = l // tm
    mod_map = (lambda bi, i: (bi, 0, 0)) if per_batch_mod else (lambda bi, i: (0, 0, 0))
    tab_map = lambda bi, i: (i, 0)
    const = lambda bi, i: (0, 0)
    tok = lambda w: pl.BlockSpec((1, tm, w), lambda bi, i: (bi, i, 0))
    full = lambda a: pl.BlockSpec(a.shape, const)
    weights = [p['g_norm1'], p['w_in_p'], p['w_gk_big'], p['b_gk'], p['g_q_lora'], p['w_uq_p'],
               p['g_kv_lora'], p['w_ukv_p'], p['gq'], p['gk']]
    outs = [(GLA_KEY_W, BF16), (GLA_KEY_W, BF16), (GLA_VAL_W, BF16), (GLA_VAL_W, BF16),
            (GLA_KEY_W, F32), (GLA_KEY_W, F32), (MLA_W, BF16), (MLA_W, BF16), (MLA_W, BF16),
            (KV_LORA, F32), (LANES, F32)]
    return pl.pallas_call(
        functools.partial(_inproj_kernel, positions=bool(rope_tabs)),
        grid=(b, nt),
        in_specs=[tok(d), pl.BlockSpec((1, 6, d), mod_map)] + [full(w) for w in weights]
                 + [pl.BlockSpec((tm, LANES), tab_map)] * len(rope_tabs),
        out_specs=[tok(w) for w, _ in outs],
        out_shape=[jax.ShapeDtypeStruct((b, l, w), dt) for w, dt in outs],
        compiler_params=pltpu.CompilerParams(
            dimension_semantics=("parallel", "parallel"), vmem_limit_bytes=VMEM_LIMIT),
        name="inproj",
    )(x, mod, *weights, *rope_tabs)


def _cache_kv_kernel(ckv_ref, kr_ref, wukv_ref, gk_ref, k_ref, v_ref):
    _mla_kv(ckv_ref[0], kr_ref[0], wukv_ref, gk_ref[...], None, k_ref, v_ref)


def _cache_kv(ckv, kr128, p):
    b, l, _ = ckv.shape
    tok = lambda w: pl.BlockSpec((1, l, w), lambda bi: (bi, 0, 0))
    full = lambda a: pl.BlockSpec(a.shape, lambda bi: (0, 0))
    return pl.pallas_call(
        _cache_kv_kernel,
        grid=(b,),
        in_specs=[tok(KV_LORA), tok(LANES), full(p['w_ukv_p']), full(p['gk'])],
        out_specs=[tok(MLA_W), tok(MLA_W)],
        out_shape=[jax.ShapeDtypeStruct((b, l, MLA_W), BF16)] * 2,
        name="cache_kv",
    )(ckv, kr128, p['w_ukv_p'], p['gk'])


def _split3(x):
    hi = x.astype(BF16)
    r1 = x - hi.astype(F32)
    mid = r1.astype(BF16)
    lo = (r1 - mid.astype(F32)).astype(BF16)
    return hi, mid, lo


def _gla_block(q, k, v, lg, s, fwd):
    n = GLA_BLOCK
    nc = n // GLA_SUB
    row = lax.broadcasted_iota(I32, (n, n), 0)
    col = lax.broadcasted_iota(I32, (n, n), 1)
    same = (row // GLA_SUB) == (col // GLA_SUB)
    causal = same & ((col <= row) if fwd else (col >= row))
    tri = causal.astype(BF16)
    hi, mid, lo = _split3(lg)
    cum = _dot(tri, hi) + _dot(tri, mid) + _dot(tri, lo)
    tot_rows, mid_rows = [], []
    for c in range(nc):
        r_tot = c * GLA_SUB + (GLA_SUB - 1 if fwd else 0)
        r_mid = c * GLA_SUB + GLA_SUB // 2
        tot_rows.append(jnp.broadcast_to(cum[r_tot:r_tot + 1], (GLA_SUB, GLA_KEY_W)))
        mid_rows.append(jnp.broadcast_to(cum[r_mid:r_mid + 1], (GLA_SUB, GLA_KEY_W)))
    tot_rows = jnp.concatenate(tot_rows, axis=0)
    mid_rows = jnp.concatenate(mid_rows, axis=0)
    rel = cum - mid_rows
    qi = q * jnp.exp(rel)
    ki = (k * jnp.exp(-rel)).astype(BF16)
    q_in = (q * jnp.exp(cum)).astype(BF16)
    k_up = k * jnp.exp(tot_rows - cum)
    dec = jnp.exp(tot_rows)
    k_up_t = k_up.T.astype(BF16)
    dec_t = dec.T
    lane_head = lax.broadcasted_iota(I32, (n, GLA_KEY_W), 1) // GLA_DK
    srow_head = lax.broadcasted_iota(I32, (GLA_KEY_W, GLA_DV), 0) // GLA_DK
    vrow_chunk = lax.broadcasted_iota(I32, (n, GLA_VAL_W), 0) // GLA_SUB
    o_heads = []
    for h in range(GLA_HEADS):
        a = _dot_nt(jnp.where(lane_head == h, qi, 0.0).astype(BF16), ki)
        a = jnp.where(causal, a, 0.0).astype(BF16)
        o_heads.append(_dot(a, v[:, h * GLA_DV:(h + 1) * GLA_DV]))
    o = jnp.concatenate(o_heads, axis=1)
    o_inter = [None] * nc
    for c in (range(nc) if fwd else range(nc - 1, -1, -1)):
        s_bd = jnp.concatenate(
            [jnp.where(srow_head == h, s, 0.0).astype(BF16) for h in range(GLA_HEADS)], axis=1)
        o_inter[c] = _dot(q_in[c * GLA_SUB:(c + 1) * GLA_SUB], s_bd)
        v_c = jnp.where(vrow_chunk == c, v, jnp.zeros_like(v))
        u = jnp.concatenate(
            [_dot(k_up_t[h * GLA_DK:(h + 1) * GLA_DK], v_c[:, h * GLA_DV:(h + 1) * GLA_DV])
             for h in range(GLA_HEADS)], axis=0)
        s = dec_t[:, c * GLA_SUB:c * GLA_SUB + 1] * s + u
    return o + jnp.concatenate(o_inter, axis=0), s


def _gla_kernel(q_ref, k_ref, v_ref, g_ref, lgf_ref, lgb_ref, s0f_ref, s0b_ref, gout_ref,
                o_ref, sf_ref, sb_ref, acc_ref, st_ref, *, nblk, n_seq):
    for si in range(n_seq):
        _gla_sequence(si, q_ref, k_ref, v_ref, g_ref, lgf_ref, lgb_ref, s0f_ref, s0b_ref,
                      gout_ref, o_ref, sf_ref, sb_ref, acc_ref, st_ref, nblk)


def _gla_sequence(si, q_ref, k_ref, v_ref, g_ref, lgf_ref, lgb_ref, s0f_ref, s0b_ref, gout_ref,
                  o_ref, sf_ref, sb_ref, acc_ref, st_ref, nblk):
    st_ref[si, 0] = s0f_ref[si]
    st_ref[si, 1] = s0b_ref[si]

    def load(blk):
        start = blk * GLA_BLOCK
        r = pl.ds(start if isinstance(blk, int) else pl.multiple_of(start, GLA_BLOCK), GLA_BLOCK)
        return r, q_ref[si, r, :].astype(F32), k_ref[si, r, :].astype(F32), v_ref[si, r, :]

    def pair_step(i, first_touch):
        rf, qf, kf, vf = load(i)
        rb, qb, kb, vb = load(nblk - 1 - i)
        of, s_f = _gla_block(qf, kf, vf, lgf_ref[si, rf, :], st_ref[si, 0], True)
        ob, s_b = _gla_block(qb, kb, vb, lgb_ref[si, rb, :], st_ref[si, 1], False)
        if first_touch:
            acc_ref[si, rf, :] = of
            acc_ref[si, rb, :] = ob
        else:
            acc_ref[si, rf, :] += of
            acc_ref[si, rb, :] += ob
        st_ref[si, 0] = s_f
        st_ref[si, 1] = s_b

    if nblk == 1:
        r, q, k, v = load(0)
        of, s_f = _gla_block(q, k, v, lgf_ref[si, r, :], st_ref[si, 0], True)
        ob, s_b = _gla_block(q, k, v, lgb_ref[si, r, :], st_ref[si, 1], False)
        acc_ref[si, r, :] = of + ob
        st_ref[si, 0] = s_f
        st_ref[si, 1] = s_b
    else:
        half = nblk // 2
        lax.fori_loop(0, half, lambda i, c: (pair_step(i, True), c)[1], 0)
        lax.fori_loop(half, nblk, lambda i, c: (pair_step(i, False), c)[1], 0)
    sf_ref[si] = st_ref[si, 0]
    sb_ref[si] = st_ref[si, 1]

    def fin_step(blk, carry):
        r = pl.ds(pl.multiple_of(blk * GLA_BLOCK, GLA_BLOCK), GLA_BLOCK)
        o = acc_ref[si, r, :]
        gate = _silu(g_ref[si, r, :].astype(F32))
        for h in range(GLA_HEADS):
            sl = slice(h * GLA_DV, (h + 1) * GLA_DV)
            o_ref[si, r, sl] = (_rms(o[:, sl], GLA_DV) * gout_ref[...] * gate[:, sl]).astype(BF16)
        return carry

    lax.fori_loop(0, nblk, fin_step, 0)


def _gla(q, k, v, g, lgf, lgb, s0f, s0b, g_out, n_seq):
    b, l, _ = q.shape
    assert l == GLA_BLOCK or l % (2 * GLA_BLOCK) == 0
    seq = lambda w: pl.BlockSpec((n_seq, l, w), lambda bi: (bi, 0, 0))
    st = pl.BlockSpec((n_seq, GLA_KEY_W, GLA_DV), lambda bi: (bi, 0, 0))
    return pl.pallas_call(
        functools.partial(_gla_kernel, nblk=l // GLA_BLOCK, n_seq=n_seq),
        grid=(b // n_seq,),
        in_specs=[seq(GLA_KEY_W), seq(GLA_KEY_W), seq(GLA_VAL_W), seq(GLA_VAL_W),
                  seq(GLA_KEY_W), seq(GLA_KEY_W), st, st,
                  pl.BlockSpec((1, GLA_DV), lambda bi: (0, 0))],
        out_specs=[seq(GLA_VAL_W), st, st],
        out_shape=[jax.ShapeDtypeStruct((b, l, GLA_VAL_W), BF16),
                   jax.ShapeDtypeStruct((b, GLA_KEY_W, GLA_DV), F32),
                   jax.ShapeDtypeStruct((b, GLA_KEY_W, GLA_DV), F32)],
        scratch_shapes=[pltpu.VMEM((n_seq, l, GLA_VAL_W), F32),
                        pltpu.VMEM((n_seq, 2, GLA_KEY_W, GLA_DV), F32)],
        compiler_params=pltpu.CompilerParams(
            dimension_semantics=("parallel",), vmem_limit_bytes=VMEM_LIMIT),
        name="gla",
    )(q, k, v, g, lgf, lgb, s0f, s0b, g_out)


def _attn_kernel(*refs, n_kv):
    q_ref, o_ref = refs[0], refs[-1]
    kv = [(refs[1 + 2 * i], refs[2 + 2 * i]) for i in range(n_kv)]
    for h in range(MLA_HEADS):
        sl = slice(h * HEAD_W, (h + 1) * HEAD_W)
        q = q_ref[0, :, sl]
        scores = [_dot_nt(q, k_ref[0, :, sl]) for k_ref, _ in kv]
        m = functools.reduce(jnp.maximum, [jnp.max(s, axis=-1, keepdims=True) for s in scores])
        o, den = 0.0, 0.0
        for s, (_, v_ref) in zip(scores, kv):
            p = jnp.exp(s - m)
            den = den + jnp.sum(p, axis=-1, keepdims=True)
            o = o + _dot(p.astype(BF16), v_ref[0, :, sl])
        o_ref[0, :, sl] = (o / den).astype(BF16)


def _attn(q, kv_pairs, tq):
    b, l, _ = q.shape
    kv_specs, kv_args = [], []
    for k, v in kv_pairs:
        spec = pl.BlockSpec((1, k.shape[1], MLA_W), lambda bi, i: (bi, 0, 0))
        kv_specs += [spec, spec]
        kv_args += [k, v]
    return pl.pallas_call(
        functools.partial(_attn_kernel, n_kv=len(kv_pairs)),
        grid=(b, l // tq),
        in_specs=[pl.BlockSpec((1, tq, MLA_W), lambda bi, i: (bi, i, 0))] + kv_specs,
        out_specs=pl.BlockSpec((1, tq, MLA_W), lambda bi, i: (bi, i, 0)),
        out_shape=jax.ShapeDtypeStruct((b, l, MLA_W), BF16),
        compiler_params=pltpu.CompilerParams(
            dimension_semantics=("parallel", "parallel"), vmem_limit_bytes=VMEM_LIMIT),
        name="attn",
    )(q, *kv_args)


def _expert_cap(n_tok):
    return -(-n_tok // FFN_TILE) * FFN_TILE


def _sum01(x):
    return jnp.sum(jnp.sum(x, axis=1, keepdims=True), axis=0, keepdims=True)


def _route(logits_t, bias_col, cnt_col, cap):
    t = logits_t.shape[1]
    gsz = N_EXPERTS // N_GROUPS
    scores = jax.nn.sigmoid(logits_t)
    sel = (scores + bias_col).reshape(N_GROUPS, gsz, t)
    scores = scores.reshape(N_GROUPS, gsz, t)
    neg = jnp.float32(-jnp.inf)
    ie = lax.broadcasted_iota(I32, (N_GROUPS, gsz, t), 1)
    ig = lax.broadcasted_iota(I32, (N_GROUPS, gsz, t), 0)
    m1 = jnp.max(sel, axis=1, keepdims=True)
    first = jnp.min(jnp.where(sel == m1, ie, gsz), axis=1, keepdims=True)
    m2 = jnp.max(jnp.where(ie == first, neg, sel), axis=1, keepdims=True)
    grp = m1 + m2
    igk = lax.broadcasted_iota(I32, (N_GROUPS, 1, t), 0)
    g_sel = jnp.zeros((N_GROUPS, 1, t), jnp.bool_)
    cur = grp
    for _ in range(TOPK_GROUPS):
        m = jnp.max(cur, axis=0, keepdims=True)
        pick = igk == jnp.min(jnp.where(cur == m, igk, N_GROUPS), axis=0, keepdims=True)
        g_sel = g_sel | pick
        cur = jnp.where(pick, neg, cur)
    cur = jnp.where(g_sel, sel, neg)
    idx = ig * gsz + ie
    e_sel = jnp.zeros((N_GROUPS, gsz, t), jnp.bool_)
    picks = []
    for _ in range(TOP_K):
        m = jnp.max(jnp.max(cur, axis=1, keepdims=True), axis=0, keepdims=True)
        cand = jnp.where(cur == m, idx, N_EXPERTS)
        pick = idx == jnp.min(jnp.min(cand, axis=1, keepdims=True), axis=0, keepdims=True)
        picks.append(pick)
        e_sel = e_sel | pick
        cur = jnp.where(pick, neg, cur)
    w = jnp.where(e_sel, scores, 0.0)
    gate = w / _sum01(w) * ROUTED_SCALE
    sel_f = e_sel.astype(F32).reshape(N_EXPERTS, t)
    earlier = (lax.broadcasted_iota(I32, (t, t), 0) < lax.broadcasted_iota(I32, (t, t), 1))
    rank = _dot(sel_f.astype(BF16), earlier.astype(BF16))
    base = lax.broadcasted_iota(I32, (N_EXPERTS, 1), 0).astype(F32) * float(cap) + cnt_col
    slot = (base + rank).reshape(N_GROUPS, gsz, t)
    slot8 = jnp.concatenate([_sum01(jnp.where(pk, slot, 0.0)).reshape(1, t) for pk in picks],
                            axis=0).astype(I32)
    w8 = jnp.concatenate([_sum01(jnp.where(pk, gate, 0.0)).reshape(1, t) for pk in picks], axis=0)
    return slot8, w8, cnt_col + jnp.sum(sel_f, axis=1, keepdims=True)


def _outproj_kernel(og_ref, om_ref, x_ref, mod_ref, wtop_ref, wbot_ref, g2_ref, wr_ref, br_ref,
                    x1_ref, h2pa_ref, h2pb_ref, slot_ref, w8_ref, cnt_ref, cnt_scr, *, cap):
    @pl.when((pl.program_id(0) == 0) & (pl.program_id(1) == 0))
    def _():
        cnt_scr[...] = jnp.zeros_like(cnt_scr)

    mod = mod_ref[0]
    cnt = cnt_scr[:, 0:1]
    for r0 in range(0, x_ref.shape[1], OUT_SUB):
        rs = slice(r0, r0 + OUT_SUB)
        mix = _dot(og_ref[0, rs, :], wtop_ref[...]) + _dot(om_ref[0, rs, :], wbot_ref[...])
        x1 = x_ref[0, rs, :] + mod[2:3] * mix
        x1_ref[0, rs, :] = x1
        h2 = _rms(x1, D_MODEL) * g2_ref[...] * (1.0 + mod[4:5]) + mod[3:4]
        h2pa_ref[0, rs, :], h2pb_ref[0, rs, :] = _pack_rows(h2)
        logits_t = _dot_nt(wr_ref[...], h2, precision=HIGHEST)
        slot8, w8, cnt = _route(logits_t, br_ref[...], cnt, cap)
        slot_ref[:, rs] = slot8
        wb = lax.bitcast_convert_type(w8.astype(BF16).astype(F32), U32)
        wb = wb | (wb >> 16)
        w8_ref[0, rs, :] = jnp.concatenate(
            [jnp.broadcast_to(wb[k:k + 1], (SC_LANES, OUT_SUB)) for k in range(TOP_K)], axis=0).T
    cnt_scr[...] = jnp.broadcast_to(cnt, cnt_scr.shape)
    cnt_ref[...] = cnt_scr[...]


def _outproj(og, om, x, mod, per_batch_mod, p, tm):
    b, l, d = x.shape
    nt = l // tm
    mod_map = (lambda bi, i: (bi, 0, 0)) if per_batch_mod else (lambda bi, i: (0, 0, 0))
    tok = lambda w: pl.BlockSpec((1, tm, w), lambda bi, i: (bi, i, 0))
    full = lambda a: pl.BlockSpec(a.shape, lambda bi, i: (0, 0))
    weights = [p['w_out_top'], p['w_out_bot'], p['g_norm2'], p['w_router_t'], p['b_router_col']]
    cnt_shape = (N_EXPERTS, LANES)
    return pl.pallas_call(
        functools.partial(_outproj_kernel, cap=_expert_cap(b * l)),
        grid=(b, nt),
        in_specs=[tok(GLA_VAL_W), tok(MLA_W), tok(d), pl.BlockSpec((1, 6, d), mod_map)]
                 + [full(w) for w in weights],
        out_specs=[tok(d), tok(PACK_W), tok(PACK_W),
                   pl.BlockSpec((TOP_K, tm), lambda bi, i: (0, bi * nt + i)),
                   tok(LANES), pl.BlockSpec(cnt_shape, lambda bi, i: (0, 0))],
        out_shape=[jax.ShapeDtypeStruct((b, l, d), F32), jax.ShapeDtypeStruct((b, l, PACK_W), U32),
                   jax.ShapeDtypeStruct((b, l, PACK_W), U32),
                   jax.ShapeDtypeStruct((TOP_K, b * l), I32),
                   jax.ShapeDtypeStruct((b, l, LANES), U32),
                   jax.ShapeDtypeStruct(cnt_shape, F32)],
        scratch_shapes=[pltpu.VMEM(cnt_shape, F32)],
        compiler_params=pltpu.CompilerParams(
            dimension_semantics=("arbitrary", "arbitrary"), vmem_limit_bytes=VMEM_LIMIT),
        name="outproj",
    )(og, om, x, mod, *weights)


def _sc_mesh():
    return plsc.VectorSubcoreMesh(core_axis_name="c", subcore_axis_name="s")


def _sc_scatter_rows(src, slot8, n_rows_out):
    t, d = src.shape
    nk = slot8.shape[0]

    @functools.partial(pl.kernel, out_type=jax.ShapeDtypeStruct((n_rows_out, d), src.dtype),
                       mesh=_sc_mesh(), scratch_types=[], name="dispatch")
    def run(src_hbm, slot_hbm, out_hbm):
        def body(x_vmem, i_vmem):
            for k in range(nk):
                pltpu.sync_copy(x_vmem, out_hbm.at[i_vmem.at[k]])

        pltpu.emit_pipeline(
            body, grid=(t // SC_WINDOW,),
            in_specs=[pl.BlockSpec((SC_WINDOW, d), lambda i: (i, 0)),
                      pl.BlockSpec((nk, SC_WINDOW), lambda i: (0, i))],
            out_specs=[], core_axis_name=("c", "s"),
            dimension_semantics=(pltpu.PARALLEL,))(src_hbm, slot_hbm)

    return run(src, slot8)


def _sc_collect_sum(table, slots_tk, w_rep):
    n_tok = w_rep.shape[0]
    d = table.shape[1]
    group = SC_WINDOW // TOP_K
    chunks = d // SC_LANES

    @functools.partial(pl.kernel, out_type=jax.ShapeDtypeStruct((n_tok, 2 * d), F32),
                       mesh=_sc_mesh(),
                       scratch_types=[pltpu.VMEM((SC_GATHERS, SC_WINDOW, d), table.dtype),
                                      pltpu.SemaphoreType.DMA((SC_GATHERS,))],
                       compiler_params=pltpu.CompilerParams(needs_layout_passes=False),
                       name="collect")
    def run(tab_hbm, slot_hbm, w_hbm, out_hbm, rows_v, sems):
        def gather(i_vmem, b):
            return pltpu.make_async_copy(tab_hbm.at[i_vmem.at[b]], rows_v.at[b], sems.at[b])

        def body(i_vmem, w_vmem, o_vmem):
            for b in range(SC_GATHERS):
                gather(i_vmem, b).start()
            for b in range(SC_GATHERS):
                gather(i_vmem, b).wait()

                @pl.loop(0, group)
                def _(j):
                    tok = b * group + j
                    wv = [plsc.bitcast(w_vmem[tok, pl.ds(k * SC_LANES, SC_LANES)], BF16)
                          for k in range(TOP_K)]

                    def chunk(off):
                        prods = [plsc.bitcast(
                            rows_v[b, j * TOP_K + k, pl.ds(off, SC_LANES)], BF16) * wv[k]
                            for k in range(TOP_K)]
                        lo = jnp.zeros((SC_LANES,), F32)
                        hi = jnp.zeros((SC_LANES,), F32)
                        for k in range(0, TOP_K, 2):
                            u = plsc.bitcast(prods[k] + prods[k + 1], U32)
                            lo = lo + lax.bitcast_convert_type(u << 16, F32)
                            hi = hi + lax.bitcast_convert_type(u & jnp.uint32(0xFFFF0000), F32)
                        o_vmem[tok, pl.ds(off, SC_LANES)] = lo
                        o_vmem[tok, pl.ds(d + off, SC_LANES)] = hi

                    @plsc.parallel_loop(0, chunks, unroll=2)
                    def _(c):
                        chunk(c * SC_LANES)

        step_tok = SC_GATHERS * group
        pltpu.emit_pipeline(
            body, grid=(n_tok // step_tok,),
            in_specs=[pl.BlockSpec((SC_GATHERS, SC_WINDOW), lambda i: (i, 0)),
                      pl.BlockSpec((step_tok, TOP_K * SC_LANES), lambda i: (i, 0))],
            out_specs=[pl.BlockSpec((step_tok, 2 * d), lambda i: (i, 0))],
            core_axis_name=("c", "s"),
            dimension_semantics=(pltpu.PARALLEL,))(slot_hbm, w_hbm, out_hbm)

    return run(table, slots_tk.reshape(-1, SC_WINDOW), w_rep)


def _ffn_kernel(cnt_ref, nt_ref, first_ref, te_ref, tt_ref, xsa_hbm, xsb_hbm,
                wg_ref, wu_ref, wd_ref, ysa_hbm, ysb_hbm,
                xbuf, ybuf, wg_b, wu_b, wd_b, sem_in, sem_out, *, cap):
    e = pl.program_id(0)
    n_exp = pl.num_programs(0)
    n = nt_ref[e]
    g0 = first_ref[e]
    total = first_ref[n_exp - 1] + nt_ref[n_exp - 1]
    xs_hbm = (xsa_hbm, xsb_hbm)
    ys_hbm = (ysa_hbm, ysb_hbm)
    n_in = FFN_AHEAD + 1

    def rows(ex, t):
        return pl.ds(pl.multiple_of(ex * cap + t * FFN_TILE, FFN_TILE), FFN_TILE)

    def in_copy(ex, t, slot, part):
        return pltpu.make_async_copy(xs_hbm[part].at[rows(ex, t)], xbuf.at[slot, part],
                                     sem_in.at[slot, part])

    def out_copy(ex, t, slot, part):
        return pltpu.make_async_copy(ybuf.at[slot, part], ys_hbm[part].at[rows(ex, t)],
                                     sem_out.at[slot, part])

    def start_in(g):
        for part in range(PACK_PARTS):
            in_copy(te_ref[g], tt_ref[g], g % n_in, part).start()

    def wait_out(slot):
        for part in range(PACK_PARTS):
            out_copy(e, 0, slot, part).wait()

    @pl.when(n > 0)
    def _():
        @pl.when(g0 == 0)
        def _():
            for ahead in range(FFN_AHEAD):
                @pl.when(ahead < total)
                def _():
                    start_in(ahead)

        wg_b[...] = wg_ref[0].astype(BF16)
        wu_b[...] = wu_ref[0].astype(BF16)
        wd_b[...] = wd_ref[0].astype(BF16)

        def tile(t, carry):
            g = g0 + t
            slot = g % 2
            slot_in = g % n_in
            for part in range(PACK_PARTS):
                in_copy(e, t, slot_in, part).wait()

            @pl.when(g + FFN_AHEAD < total)
            def _():
                start_in(g + FFN_AHEAD)

            @pl.when(g >= 2)
            def _():
                wait_out(slot)

            def compute(n_rows):
                x = _unpack_rows([xbuf[slot_in, part, 0:n_rows, :]
                                  for part in range(PACK_PARTS)]).astype(BF16)
                a = _silu(_dot(x, wg_b[...])) * _dot(x, wu_b[...])
                y = _pack_rows(_dot(a.astype(BF16), wd_b[...]))
                for part in range(PACK_PARTS):
                    ybuf[slot, part, 0:n_rows, :] = y[part]

            few = cnt_ref[e] - t * FFN_TILE <= FFN_TILE // 2
            pl.when(few)(lambda: compute(FFN_TILE // 2))
            pl.when(jnp.logical_not(few))(lambda: compute(FFN_TILE))
            for part in range(PACK_PARTS):
                out_copy(e, t, slot, part).start()
            return carry

        lax.fori_loop(0, n, tile, 0)

        @pl.when(g0 + n == total)
        def _():
            @pl.when(total >= 2)
            def _():
                wait_out(total % 2)

            wait_out((total - 1) % 2)


def _ffn(xs_parts, counts, w_gate, w_up, w_down):
    n_exp, d = w_gate.shape[0], w_gate.shape[1]
    cap = xs_parts[0].shape[0] // n_exp
    ntiles = (counts + FFN_TILE - 1) // FFN_TILE
    cum = jnp.cumsum(ntiles)
    first = cum - ntiles
    max_tiles = (cap * TOP_K) // FFN_TILE + n_exp
    g = jnp.arange(max_tiles, dtype=I32)
    done = g[:, None] >= cum[None, :]
    tile_e = jnp.minimum(jnp.sum(done, axis=1), n_exp - 1).astype(I32)
    tile_t = (g - jnp.sum(jnp.where(done, ntiles[None, :], 0), axis=1)).astype(I32)
    wmap = lambda e, cnt, nt, first, te, tt: (e, 0, 0)
    hbm = pl.BlockSpec(memory_space=pl.ANY)
    return pl.pallas_call(
        functools.partial(_ffn_kernel, cap=cap),
        grid_spec=pltpu.PrefetchScalarGridSpec(
            num_scalar_prefetch=5, grid=(n_exp,),
            in_specs=[hbm] * PACK_PARTS
                     + [pl.BlockSpec((1, d, D_EXPERT), wmap), pl.BlockSpec((1, d, D_EXPERT), wmap),
                        pl.BlockSpec((1, D_EXPERT, d), wmap)],
            out_specs=[hbm] * PACK_PARTS,
            scratch_shapes=[pltpu.VMEM((FFN_AHEAD + 1, PACK_PARTS, FFN_TILE, PACK_W), U32),
                            pltpu.VMEM((2, PACK_PARTS, FFN_TILE, PACK_W), U32),
                            pltpu.VMEM((d, D_EXPERT), BF16), pltpu.VMEM((d, D_EXPERT), BF16),
                            pltpu.VMEM((D_EXPERT, d), BF16),
                            pltpu.SemaphoreType.DMA((FFN_AHEAD + 1, PACK_PARTS)),
                            pltpu.SemaphoreType.DMA((2, PACK_PARTS))]),
        out_shape=[jax.ShapeDtypeStruct(xs_parts[0].shape, U32)] * PACK_PARTS,
        compiler_params=pltpu.CompilerParams(
            dimension_semantics=("arbitrary",), vmem_limit_bytes=VMEM_LIMIT),
        name="ffn",
    )(counts.astype(I32), ntiles.astype(I32), first.astype(I32), tile_e, tile_t, *xs_parts,
      w_gate, w_up, w_down)


def _combine_kernel(x1_ref, h2pa_ref, h2pb_ref, ra_ref, rb_ref, mod_ref,
                    wsg_ref, wsu_ref, wsd_ref, o_ref):
    h2 = _unpack_rows([h2pa_ref[0], h2pb_ref[0]]).astype(BF16)
    a = _silu(_dot(h2, wsg_ref[...])) * _dot(h2, wsu_ref[...])
    routed = jnp.concatenate([ra_ref[0], rb_ref[0]], axis=1)
    o_ref[0] = x1_ref[0] + mod_ref[0][5:6] * (_dot(a.astype(BF16), wsd_ref[...]) + routed)


def _combine(x1, h2p_parts, routed_parts, mod, per_batch_mod, p, tm):
    b, l, d = x1.shape
    mod_map = (lambda bi, i: (bi, 0, 0)) if per_batch_mod else (lambda bi, i: (0, 0, 0))
    tok = lambda w: pl.BlockSpec((1, tm, w), lambda bi, i: (bi, i, 0))
    full = lambda a: pl.BlockSpec(a.shape, lambda bi, i: (0, 0))
    weights = [p['w_sh_gate'], p['w_sh_up'], p['w_sh_down']]
    routed_parts = [r.reshape(b, l, d // PACK_PARTS) for r in routed_parts]
    return pl.pallas_call(
        _combine_kernel,
        grid=(b, l // tm),
        in_specs=[tok(d)] + [tok(PACK_W)] * PACK_PARTS + [tok(d // PACK_PARTS)] * PACK_PARTS
                 + [pl.BlockSpec((1, 6, d), mod_map)] + [full(w) for w in weights],
        out_specs=tok(d),
        out_shape=jax.ShapeDtypeStruct((b, l, d), F32),
        compiler_params=pltpu.CompilerParams(
            dimension_semantics=("parallel", "parallel"), vmem_limit_bytes=VMEM_LIMIT),
        name="combine",
    )(x1, *h2p_parts, *routed_parts, mod, *weights)


def _pad_heads(w, parts):
    k = w.shape[0]
    per = w.shape[1] // MLA_HEADS
    w = w.reshape(k, MLA_HEADS, per)[:, :, parts[0]:parts[1]]
    w = jnp.pad(w, ((0, 0), (0, 0), (0, HEAD_W - (parts[1] - parts[0]))))
    return w.reshape(k, MLA_HEADS * HEAD_W)


def _prep_params(l, g_norm1, g_norm2, w_in, w_gk_fwd, b_gk_fwd, w_gk_bwd, b_gk_bwd, g_gla_out,
                 g_q_lora, w_uq, g_kv_lora, w_ukv, g_qk_q, g_qk_k, w_out, w_router, b_router,
                 w_sh_gate, w_sh_up, w_sh_down):
    w = w_in[l]
    d = w.shape[0]
    o_lrf = COL_CQ
    o_lrb = o_lrf + GLA_GATE_RANK
    o_cq = o_lrb + GLA_GATE_RANK
    o_ckv = o_cq + Q_LORA
    o_kr = o_ckv + KV_LORA
    w_in_p = jnp.concatenate([
        w[:, 0:COL_CQ], w[:, o_cq:o_ckv], w[:, o_ckv:o_kr], jnp.zeros((d, KR_LANE0), w.dtype),
        w[:, o_kr:o_kr + ROPE_DIM], w[:, o_lrf:o_lrb], w[:, o_lrb:o_cq]], axis=1).astype(BF16)
    z = jnp.zeros((GLA_GATE_RANK, GLA_KEY_W), F32)
    w_gk_big = jnp.concatenate([
        jnp.zeros((LANES - 2 * GLA_GATE_RANK, 2 * GLA_KEY_W), F32),
        jnp.concatenate([w_gk_fwd[l], z], axis=1),
        jnp.concatenate([z, w_gk_bwd[l]], axis=1)], axis=0).astype(BF16)
    pad_gain = lambda g: jnp.pad(g, (0, HEAD_W - QK_DIM)).reshape(1, HEAD_W)
    w_ukv_h = w_ukv[l]
    return {
        'g_norm1': g_norm1[l].reshape(1, d), 'g_norm2': g_norm2[l].reshape(1, d),
        'w_in_p': w_in_p, 'w_gk_big': w_gk_big,
        'b_gk': jnp.concatenate([b_gk_fwd[l], b_gk_bwd[l]]).reshape(1, 2 * GLA_KEY_W),
        'g_gla_out': g_gla_out[l].reshape(1, GLA_DV),
        'g_q_lora': g_q_lora[l].reshape(1, Q_LORA),
        'w_uq_p': _pad_heads(w_uq[l], (0, QK_DIM)).astype(BF16),
        'g_kv_lora': g_kv_lora[l].reshape(1, KV_LORA),
        'w_ukv_p': jnp.concatenate([_pad_heads(w_ukv_h, (0, NOPE_DIM)),
                                    _pad_heads(w_ukv_h, (NOPE_DIM, NOPE_DIM + V_DIM))],
                                   axis=1).astype(BF16),
        'gq': pad_gain(g_qk_q[l]), 'gk': pad_gain(g_qk_k[l]),
        'w_out_top': w_out[l][:GLA_VAL_W].astype(BF16),
        'w_out_bot': w_out[l][GLA_VAL_W:].astype(BF16),
        'w_router_t': w_router[l].T, 'b_router_col': b_router[l].reshape(N_EXPERTS, 1),
        'w_sh_gate': w_sh_gate[l].astype(BF16), 'w_sh_up': w_sh_up[l].astype(BF16),
        'w_sh_down': w_sh_down[l].astype(BF16),
    }


def _rope_tables(length):
    pos = np.arange(length)
    r = (pos // GRID_W).astype(np.float32)
    c = (pos % GRID_W).astype(np.float32)
    half = ROPE_DIM // 4
    inv_freq = np.float32(ROPE_THETA) ** (-np.arange(half, dtype=np.float32) / np.float32(half))
    ang_r = r[:, None] * inv_freq[None, :]
    ang_c = c[:, None] * inv_freq[None, :]
    zeros = lambda w: np.zeros((length, w), np.float32)
    ones = lambda w: np.ones((length, w), np.float32)
    tail_w = HEAD_W - ROPE_LANE0 - ROPE_DIM
    cos = np.concatenate([ones(ROPE_LANE0), np.cos(ang_r), np.cos(ang_r), np.cos(ang_c),
                          np.cos(ang_c), ones(tail_w)], axis=1)
    s1 = np.concatenate([zeros(ROPE_LANE0), -np.sin(ang_r), zeros(half), -np.sin(ang_c),
                         zeros(half), zeros(tail_w)], axis=1)
    s2 = np.concatenate([zeros(ROPE_LANE0), zeros(half), np.sin(ang_r), zeros(half),
                         np.sin(ang_c), zeros(tail_w)], axis=1)
    return tuple(jnp.asarray(t, F32) for t in (cos, s1, s2))


def _mix_route_dispatch(x_tok, seq_shape, mod, per_batch_mod, p, s0f, s0b, rope_tabs, ctx_kv,
                        tq):
    bt, lt, d = x_tok.shape
    b, l = seq_shape
    n_tok = bt * lt
    (q, k, v, g, lgf, lgb, qm, km, vm, ckvn, kr) = _inproj(x_tok, mod, per_batch_mod, p,
                                                            rope_tabs, IN_TILE)
    seq = lambda a: a.reshape(b, l, a.shape[-1])
    og, sf, sb = _gla(seq(q), seq(k), seq(v), seq(g), seq(lgf), seq(lgb), s0f, s0b,
                      p['g_gla_out'], GLA_SEQS if l == GLA_BLOCK else 1)
    kv_pairs = [(seq(km), seq(vm))] + ([ctx_kv] if ctx_kv is not None else [])
    om = _attn(seq(qm), kv_pairs, tq)
    tokv = lambda a: a.reshape(bt, lt, a.shape[-1])
    x1, h2a, h2b, slot8, w8, cnt = _outproj(tokv(og), tokv(om), x_tok, mod, per_batch_mod, p,
                                            OUT_TILE)
    x_sorted = [_sc_scatter_rows(h.reshape(n_tok, PACK_W), slot8,
                                 N_EXPERTS * _expert_cap(n_tok)) for h in (h2a, h2b)]
    routed = dict(x1=x1, h2=(h2a, h2b), slot8=slot8, w8=w8, cnt=cnt, x_sorted=x_sorted)
    return routed, sf, sb, ckvn, kr


def _experts_collect(r, w_gate, w_up, w_down):
    n_tok = r['slot8'].shape[1]
    y_sorted = _ffn(r['x_sorted'], r['cnt'][:, 0].astype(I32), w_gate, w_up, w_down)
    slots_tk = r['slot8'].T.reshape(1, TOP_K * n_tok)
    w_rep = r['w8'].reshape(n_tok, LANES)
    return [_sc_collect_sum(y, slots_tk, w_rep) for y in y_sorted]


def kernel(x_prompt, x_sample, c, state_gla_fwd, state_gla_bwd, cache_mla_ckv, cache_mla_krope,
           c_ctx, w_ada, b_ada, g_norm1, g_norm2, w_in, w_gk_fwd, b_gk_fwd, w_gk_bwd, b_gk_bwd,
           g_gla_out, g_q_lora, w_uq, g_kv_lora, w_ukv, g_qk_q, g_qk_k, w_out,
           w_router, b_router, w_exp_gate, w_exp_up, w_exp_down, w_sh_gate, w_sh_up, w_sh_down):
    bp, lp, d = x_prompt.shape
    bs, ls, _ = x_sample.shape
    depth = w_ada.shape[0]
    xp = x_prompt.reshape(1, bp * lp, d)
    xs = x_sample
    new_f, new_b, new_ckv, new_kr = [], [], [], []
    lat_tabs = _rope_tables(ls)
    cvecs = jnp.concatenate([c_ctx[None], c, jnp.zeros((8 - 1 - bs, d), F32)], axis=0)
    for l in range(depth):
        p = _prep_params(l, g_norm1, g_norm2, w_in, w_gk_fwd, b_gk_fwd, w_gk_bwd, b_gk_bwd,
                         g_gla_out, g_q_lora, w_uq, g_kv_lora, w_ukv, g_qk_q, g_qk_k, w_out,
                         w_router, b_router, w_sh_gate, w_sh_up, w_sh_down)
        mod = _ada(cvecs, w_ada[l], b_ada[l]).reshape(8, 6, d)
        mod_p, mod_s = mod[0:1], mod[1:1 + bs]
        zeros = jnp.zeros((bp, GLA_KEY_W, GLA_DV), F32)
        r_ctx, sf, sb, ckvn, kr = _mix_route_dispatch(
            xp, (bp, lp), mod_p, False, p, zeros, zeros, (), None, lp)
        new_f.append(sf.reshape(bp, GLA_HEADS, GLA_DK, GLA_DV))
        new_b.append(sb.reshape(bp, GLA_HEADS, GLA_DK, GLA_DV))
        new_ckv.append(ckvn.reshape(bp, lp, KV_LORA))
        new_kr.append(kr.reshape(bp, lp, LANES)[:, :, KR_LANE0:KR_LANE0 + ROPE_DIM])
        kr_cache = jnp.pad(cache_mla_krope[:, l],
                           ((0, 0), (0, 0), (ROPE_LANE0, LANES - ROPE_LANE0 - ROPE_DIM)))
        ctx_kv = _cache_kv(cache_mla_ckv[:, l], kr_cache, p)
        s0f = state_gla_fwd[:, l].reshape(bs, GLA_KEY_W, GLA_DV)
        s0b = state_gla_bwd[:, l].reshape(bs, GLA_KEY_W, GLA_DV)
        r_lat, _, _, _, _ = _mix_route_dispatch(
            xs, (bs, ls), mod_s, True, p, s0f, s0b, lat_tabs, ctx_kv, 256)
        experts = (w_exp_gate[l], w_exp_up[l], w_exp_down[l])
        routed_ctx = _experts_collect(r_ctx, *experts)
        routed_lat = _experts_collect(r_lat, *experts)
        xp = _combine(r_ctx['x1'], r_ctx['h2'], routed_ctx, mod_p, False, p, OUT_TILE)
        xs = _combine(r_lat['x1'], r_lat['h2'], routed_lat, mod_s, True, p, OUT_TILE)
    return (xp.reshape(bp, lp, d), xs, jnp.stack(new_f, axis=1), jnp.stack(new_b, axis=1),
            jnp.stack(new_ckv, axis=1), jnp.stack(new_kr, axis=1))
```

```python
import functools

import jax
import jax.numpy as jnp
import numpy as np
from jax import lax
from jax.experimental import pallas as pl
from jax.experimental.pallas import tpu as pltpu
from jax.experimental.pallas import tpu_sc as plsc

F32 = jnp.float32
BF16 = jnp.bfloat16
I32 = jnp.int32
U32 = jnp.uint32

D_MODEL = 1024
EPS = 1e-6
GRID_W = 64
GLA_HEADS = 4
GLA_DK = 64
GLA_DV = 128
GLA_GATE_RANK = 16
GLA_GATE_NORM = 16.0
GLA_KEY_W = GLA_HEADS * GLA_DK
GLA_VAL_W = GLA_HEADS * GLA_DV
MLA_HEADS = 4
Q_LORA = 256
KV_LORA = 128
NOPE_DIM = 64
ROPE_DIM = 32
V_DIM = 128
QK_DIM = NOPE_DIM + ROPE_DIM
ROPE_THETA = 10000.0
N_EXPERTS = 64
TOP_K = 8
N_GROUPS = 8
TOPK_GROUPS = 4
D_EXPERT = 256
ROUTED_SCALE = 2.5

LANES = 128
HEAD_W = LANES
MLA_W = MLA_HEADS * HEAD_W
ROPE_LANE0 = NOPE_DIM
COL_V = 2 * GLA_KEY_W
COL_G = COL_V + GLA_VAL_W
COL_CQ = COL_G + GLA_VAL_W
COL_CKV = COL_CQ + Q_LORA
TAIL0 = COL_CKV + KV_LORA
IN_W = TAIL0 + LANES
KR_LANE0 = LANES - ROPE_DIM - 2 * GLA_GATE_RANK
GLA_BLOCK = 256
GLA_SUB = 64
GLA_SEQS = 2
IN_TILE = 1024
IN_SUB = 256
OUT_TILE = 1024
OUT_SUB = 1024
FFN_TILE = 512
FFN_AHEAD = 5
PACK_PARTS = 2
PACK_W = D_MODEL // (2 * PACK_PARTS)
SC_WINDOW = 128
SC_LANES = 16
SC_GATHERS = 2
VMEM_LIMIT = 56 * 1024 * 1024

HIGHEST = lax.Precision.HIGHEST


def _dot(a, b, precision=None):
    return jnp.dot(a, b, preferred_element_type=F32, precision=precision)


def _dot_nt(a, b, precision=None):
    return lax.dot_general(a, b, (((1,), (1,)), ((), ())), preferred_element_type=F32,
                           precision=precision)


def _rms(x, width):
    ss = jnp.sum(x * x, axis=-1, keepdims=True) * (1.0 / width)
    return x * lax.rsqrt(ss + EPS)


def _silu(x):
    return x * jax.nn.sigmoid(x)


def _log_sigmoid(x):
    return jnp.minimum(x, 0.0) - jnp.log1p(jnp.exp(-jnp.abs(x)))


def _pack_rows(x):
    parts = []
    for i in range(PACK_PARTS):
        c0 = i * 2 * PACK_W
        lo = lax.bitcast_convert_type(x[:, c0:c0 + PACK_W].astype(BF16).astype(F32), U32)
        hi = lax.bitcast_convert_type(
            x[:, c0 + PACK_W:c0 + 2 * PACK_W].astype(BF16).astype(F32), U32)
        parts.append(hi | (lo >> 16))
    return parts


def _unpack_rows(parts):
    cols = []
    for w in parts:
        cols.append(lax.bitcast_convert_type(w << 16, F32))
        cols.append(lax.bitcast_convert_type(w & jnp.uint32(0xFFFF0000), F32))
    return jnp.concatenate(cols, axis=1)


def _ada_kernel(c_ref, w_ref, b_ref, o_ref):
    o_ref[...] = _dot(_silu(c_ref[...]), w_ref[...], precision=HIGHEST) + b_ref[...]


def _ada(cvecs, w_ada, b_ada):
    n = w_ada.shape[1]
    tn = 768
    return pl.pallas_call(
        _ada_kernel,
        grid=(n // tn,),
        in_specs=[pl.BlockSpec((8, D_MODEL), lambda j: (0, 0)),
                  pl.BlockSpec((D_MODEL, tn), lambda j: (0, j)),
                  pl.BlockSpec((1, tn), lambda j: (0, j))],
        out_specs=pl.BlockSpec((8, tn), lambda j: (0, j)),
        out_shape=jax.ShapeDtypeStruct((8, n), F32),
        name="ada",
    )(cvecs, w_ada, b_ada.reshape(1, n))


def _rope(x, c, s1, s2):
    return x * c + pltpu.roll(x, LANES - 8, 1) * s1 + pltpu.roll(x, 8, 1) * s2


def _mla_kv(ckv_n, kr, w_ukv_ref, gk, rope_tabs, k_ref, v_ref, rs=slice(None)):
    kv = _dot(ckv_n.astype(BF16), w_ukv_ref[...])
    for h in range(MLA_HEADS):
        k_h = kv[:, h * HEAD_W:(h + 1) * HEAD_W] + kr
        k_h = _rms(k_h, QK_DIM) * gk
        if rope_tabs is not None:
            k_h = _rope(k_h, *rope_tabs)
        k_ref[0, rs, h * HEAD_W:(h + 1) * HEAD_W] = k_h.astype(BF16)
    v_ref[0, rs, :] = kv[:, MLA_W:].astype(BF16)


def _lane_mask(lo, hi, rows):
    lane = lax.broadcasted_iota(I32, (rows, LANES), 1)
    return (lane >= lo) & (lane < hi)


def _inproj_kernel(x_ref, mod_ref, g1_ref, win_ref, wgk_ref, bgk_ref, gql_ref, wuq_ref,
                   gkv_ref, wukv_ref, gq_ref, gk_ref, *refs, positions):
    tab_refs, outs = (refs[:3], refs[3:]) if positions else ((), refs)
    (q_ref, k_ref, v_ref, g_ref, lgf_ref, lgb_ref, qm_ref, km_ref, vm_ref, ckvn_ref,
     kr_ref) = outs
    mod = mod_ref[0]
    for r0 in range(0, x_ref.shape[1], IN_SUB):
        rs = slice(r0, r0 + IN_SUB)
        x = x_ref[0, rs, :]
        h = _rms(x, D_MODEL) * g1_ref[...] * (1.0 + mod[1:2]) + mod[0:1]
        y = _dot(h.astype(BF16), win_ref[...])
        q_ref[0, rs, :] = (y[:, 0:GLA_KEY_W] * GLA_DK ** -0.5).astype(BF16)
        k_ref[0, rs, :] = y[:, GLA_KEY_W:COL_V].astype(BF16)
        v_ref[0, rs, :] = y[:, COL_V:COL_G].astype(BF16)
        g_ref[0, rs, :] = y[:, COL_G:COL_CQ].astype(BF16)
        tail = y[:, TAIL0:IN_W]
        pre = _dot(tail.astype(BF16), wgk_ref[...]) + bgk_ref[...]
        logg = _log_sigmoid(pre) * (1.0 / GLA_GATE_NORM)
        lgf_ref[0, rs, :] = logg[:, 0:GLA_KEY_W]
        lgb_ref[0, rs, :] = logg[:, GLA_KEY_W:]
        tabs = tuple(r[rs, :] for r in tab_refs) if positions else None
        cq = _rms(y[:, COL_CQ:COL_CKV], Q_LORA) * gql_ref[...]
        qm = _dot(cq.astype(BF16), wuq_ref[...])
        gq = gq_ref[...]
        for hh in range(MLA_HEADS):
            q_h = _rms(qm[:, hh * HEAD_W:(hh + 1) * HEAD_W], QK_DIM) * gq
            if positions:
                q_h = _rope(q_h, *tabs)
            qm_ref[0, rs, hh * HEAD_W:(hh + 1) * HEAD_W] = (q_h * QK_DIM ** -0.5).astype(BF16)
        ckv_n = _rms(y[:, COL_CKV:TAIL0], KV_LORA) * gkv_ref[...]
        ckvn_ref[0, rs, :] = ckv_n
        kr_ref[0, rs, :] = tail
        kr = jnp.where(_lane_mask(KR_LANE0, KR_LANE0 + ROPE_DIM, IN_SUB), tail, 0.0)
        _mla_kv(ckv_n, kr, wukv_ref, gk_ref[...], tabs, km_ref, vm_ref, rs)


def _inproj(x, mod, per_batch_mod, p, rope_tabs, tm):
    b, l, d = x.shape
    nt = l // tm
    mod_map = (lambda bi, i: (bi, 0, 0)) if per_batch_mod else (lambda bi, i: (0, 0, 0))
    tab_map = lambda bi, i: (i, 0)
    const = lambda bi, i: (0, 0)
    tok = lambda w: pl.BlockSpec((1, tm, w), lambda bi, i: (bi, i, 0))
    full = lambda a: pl.BlockSpec(a.shape, const)
    weights = [p['g_norm1'], p['w_in_p'], p['w_gk_big'], p['b_gk'], p['g_q_lora'], p['w_uq_p'],
               p['g_kv_lora'], p['w_ukv_p'], p['gq'], p['gk']]
    outs = [(GLA_KEY_W, BF16), (GLA_KEY_W, BF16), (GLA_VAL_W, BF16), (GLA_VAL_W, BF16),
            (GLA_KEY_W, F32), (GLA_KEY_W, F32), (MLA_W, BF16), (MLA_W, BF16), (MLA_W, BF16),
            (KV_LORA, F32), (LANES, F32)]
    return pl.pallas_call(
        functools.partial(_inproj_kernel, positions=bool(rope_tabs)),
        grid=(b, nt),
        in_specs=[tok(d), pl.BlockSpec((1, 6, d), mod_map)] + [full(w) for w in weights]
                 + [pl.BlockSpec((tm, LANES), tab_map)] * len(rope_tabs),
        out_specs=[tok(w) for w, _ in outs],
        out_shape=[jax.ShapeDtypeStruct((b, l, w), dt) for w, dt in outs],
        compiler_params=pltpu.CompilerParams(
            dimension_semantics=("parallel", "parallel"), vmem_limit_bytes=VMEM_LIMIT),
        name="inproj",
    )(x, mod, *weights, *rope_tabs)


def _cache_kv_kernel(ckv_ref, kr_ref, wukv_ref, gk_ref, k_ref, v_ref):
    _mla_kv(ckv_ref[0], kr_ref[0], wukv_ref, gk_ref[...], None, k_ref, v_ref)


def _cache_kv(ckv, kr128, p):
    b, l, _ = ckv.shape
    tok = lambda w: pl.BlockSpec((1, l, w), lambda bi: (bi, 0, 0))
    full = lambda a: pl.BlockSpec(a.shape, lambda bi: (0, 0))
    return pl.pallas_call(
        _cache_kv_kernel,
        grid=(b,),
        in_specs=[tok(KV_LORA), tok(LANES), full(p['w_ukv_p']), full(p['gk'])],
        out_specs=[tok(MLA_W), tok(MLA_W)],
        out_shape=[jax.ShapeDtypeStruct((b, l, MLA_W), BF16)] * 2,
        name="cache_kv",
    )(ckv, kr128, p['w_ukv_p'], p['gk'])


def _split3(x):
    hi = x.astype(BF16)
    r1 = x - hi.astype(F32)
    mid = r1.astype(BF16)
    lo = (r1 - mid.astype(F32)).astype(BF16)
    return hi, mid, lo


def _gla_block(q, k, v, lg, s, fwd):
    n = GLA_BLOCK
    nc = n // GLA_SUB
    row = lax.broadcasted_iota(I32, (n, n), 0)
    col = lax.broadcasted_iota(I32, (n, n), 1)
    same = (row // GLA_SUB) == (col // GLA_SUB)
    causal = same & ((col <= row) if fwd else (col >= row))
    tri = causal.astype(BF16)
    hi, mid, lo = _split3(lg)
    cum = _dot(tri, hi) + _dot(tri, mid) + _dot(tri, lo)
    tot_rows, mid_rows = [], []
    for c in range(nc):
        r_tot = c * GLA_SUB + (GLA_SUB - 1 if fwd else 0)
        r_mid = c * GLA_SUB + GLA_SUB // 2
        tot_rows.append(jnp.broadcast_to(cum[r_tot:r_tot + 1], (GLA_SUB, GLA_KEY_W)))
        mid_rows.append(jnp.broadcast_to(cum[r_mid:r_mid + 1], (GLA_SUB, GLA_KEY_W)))
    tot_rows = jnp.concatenate(tot_rows, axis=0)
    mid_rows = jnp.concatenate(mid_rows, axis=0)
    rel = cum - mid_rows
    qi = q * jnp.exp(rel)
    ki = (k * jnp.exp(-rel)).astype(BF16)
    q_in = (q * jnp.exp(cum)).astype(BF16)
    k_up = k * jnp.exp(tot_rows - cum)
    dec = jnp.exp(tot_rows)
    k_up_t = k_up.T.astype(BF16)
    dec_t = dec.T
    lane_head = lax.broadcasted_iota(I32, (n, GLA_KEY_W), 1) // GLA_DK
    srow_head = lax.broadcasted_iota(I32, (GLA_KEY_W, GLA_DV), 0) // GLA_DK
    vrow_chunk = lax.broadcasted_iota(I32, (n, GLA_VAL_W), 0) // GLA_SUB
    o_heads = []
    for h in range(GLA_HEADS):
        a = _dot_nt(jnp.where(lane_head == h, qi, 0.0).astype(BF16), ki)
        a = jnp.where(causal, a, 0.0).astype(BF16)
        o_heads.append(_dot(a, v[:, h * GLA_DV:(h + 1) * GLA_DV]))
    o = jnp.concatenate(o_heads, axis=1)
    o_inter = [None] * nc
    for c in (range(nc) if fwd else range(nc - 1, -1, -1)):
        s_bd = jnp.concatenate(
            [jnp.where(srow_head == h, s, 0.0).astype(BF16) for h in range(GLA_HEADS)], axis=1)
        o_inter[c] = _dot(q_in[c * GLA_SUB:(c + 1) * GLA_SUB], s_bd)
        v_c = jnp.where(vrow_chunk == c, v, jnp.zeros_like(v))
        u = jnp.concatenate(
            [_dot(k_up_t[h * GLA_DK:(h + 1) * GLA_DK], v_c[:, h * GLA_DV:(h + 1) * GLA_DV])
             for h in range(GLA_HEADS)], axis=0)
        s = dec_t[:, c * GLA_SUB:c * GLA_SUB + 1] * s + u
    return o + jnp.concatenate(o_inter, axis=0), s


def _gla_kernel(q_ref, k_ref, v_ref, g_ref, lgf_ref, lgb_ref, s0f_ref, s0b_ref, gout_ref,
                o_ref, sf_ref, sb_ref, acc_ref, st_ref, *, nblk, n_seq):
    for si in range(n_seq):
        _gla_sequence(si, q_ref, k_ref, v_ref, g_ref, lgf_ref, lgb_ref, s0f_ref, s0b_ref,
                      gout_ref, o_ref, sf_ref, sb_ref, acc_ref, st_ref, nblk)


def _gla_sequence(si, q_ref, k_ref, v_ref, g_ref, lgf_ref, lgb_ref, s0f_ref, s0b_ref, gout_ref,
                  o_ref, sf_ref, sb_ref, acc_ref, st_ref, nblk):
    st_ref[si, 0] = s0f_ref[si]
    st_ref[si, 1] = s0b_ref[si]

    def load(blk):
        start = blk * GLA_BLOCK
        r = pl.ds(start if isinstance(blk, int) else pl.multiple_of(start, GLA_BLOCK), GLA_BLOCK)
        return r, q_ref[si, r, :].astype(F32), k_ref[si, r, :].astype(F32), v_ref[si, r, :]

    def pair_step(i, first_touch):
        rf, qf, kf, vf = load(i)
        rb, qb, kb, vb = load(nblk - 1 - i)
        of, s_f = _gla_block(qf, kf, vf, lgf_ref[si, rf, :], st_ref[si, 0], True)
        ob, s_b = _gla_block(qb, kb, vb, lgb_ref[si, rb, :], st_ref[si, 1], False)
        if first_touch:
            acc_ref[si, rf, :] = of
            acc_ref[si, rb, :] = ob
        else:
            acc_ref[si, rf, :] += of
            acc_ref[si, rb, :] += ob
        st_ref[si, 0] = s_f
        st_ref[si, 1] = s_b

    if nblk == 1:
        r, q, k, v = load(0)
        of, s_f = _gla_block(q, k, v, lgf_ref[si, r, :], st_ref[si, 0], True)
        ob, s_b = _gla_block(q, k, v, lgb_ref[si, r, :], st_ref[si, 1], False)
        acc_ref[si, r, :] = of + ob
        st_ref[si, 0] = s_f
        st_ref[si, 1] = s_b
    else:
        half = nblk // 2
        lax.fori_loop(0, half, lambda i, c: (pair_step(i, True), c)[1], 0)
        lax.fori_loop(half, nblk, lambda i, c: (pair_step(i, False), c)[1], 0)
    sf_ref[si] = st_ref[si, 0]
    sb_ref[si] = st_ref[si, 1]

    def fin_step(blk, carry):
        r = pl.ds(pl.multiple_of(blk * GLA_BLOCK, GLA_BLOCK), GLA_BLOCK)
        o = acc_ref[si, r, :]
        gate = _silu(g_ref[si, r, :].astype(F32))
        for h in range(GLA_HEADS):
            sl = slice(h * GLA_DV, (h + 1) * GLA_DV)
            o_ref[si, r, sl] = (_rms(o[:, sl], GLA_DV) * gout_ref[...] * gate[:, sl]).astype(BF16)
        return carry

    lax.fori_loop(0, nblk, fin_step, 0)


def _gla(q, k, v, g, lgf, lgb, s0f, s0b, g_out, n_seq):
    b, l, _ = q.shape
    assert l == GLA_BLOCK or l % (2 * GLA_BLOCK) == 0
    seq = lambda w: pl.BlockSpec((n_seq, l, w), lambda bi: (bi, 0, 0))
    st = pl.BlockSpec((n_seq, GLA_KEY_W, GLA_DV), lambda bi: (bi, 0, 0))
    return pl.pallas_call(
        functools.partial(_gla_kernel, nblk=l // GLA_BLOCK, n_seq=n_seq),
        grid=(b // n_seq,),
        in_specs=[seq(GLA_KEY_W), seq(GLA_KEY_W), seq(GLA_VAL_W), seq(GLA_VAL_W),
                  seq(GLA_KEY_W), seq(GLA_KEY_W), st, st,
                  pl.BlockSpec((1, GLA_DV), lambda bi: (0, 0))],
        out_specs=[seq(GLA_VAL_W), st, st],
        out_shape=[jax.ShapeDtypeStruct((b, l, GLA_VAL_W), BF16),
                   jax.ShapeDtypeStruct((b, GLA_KEY_W, GLA_DV), F32),
                   jax.ShapeDtypeStruct((b, GLA_KEY_W, GLA_DV), F32)],
        scratch_shapes=[pltpu.VMEM((n_seq, l, GLA_VAL_W), F32),
                        pltpu.VMEM((n_seq, 2, GLA_KEY_W, GLA_DV), F32)],
        compiler_params=pltpu.CompilerParams(
            dimension_semantics=("parallel",), vmem_limit_bytes=VMEM_LIMIT),
        name="gla",
    )(q, k, v, g, lgf, lgb, s0f, s0b, g_out)


def _attn_kernel(*refs, n_kv):
    q_ref, o_ref = refs[0], refs[-1]
    kv = [(refs[1 + 2 * i], refs[2 + 2 * i]) for i in range(n_kv)]
    for h in range(MLA_HEADS):
        sl = slice(h * HEAD_W, (h + 1) * HEAD_W)
        q = q_ref[0, :, sl]
        scores = [_dot_nt(q, k_ref[0, :, sl]) for k_ref, _ in kv]
        m = functools.reduce(jnp.maximum, [jnp.max(s, axis=-1, keepdims=True) for s in scores])
        o, den = 0.0, 0.0
        for s, (_, v_ref) in zip(scores, kv):
            p = jnp.exp(s - m)
            den = den + jnp.sum(p, axis=-1, keepdims=True)
            o = o + _dot(p.astype(BF16), v_ref[0, :, sl])
        o_ref[0, :, sl] = (o / den).astype(BF16)


def _attn(q, kv_pairs, tq):
    b, l, _ = q.shape
    kv_specs, kv_args = [], []
    for k, v in kv_pairs:
        spec = pl.BlockSpec((1, k.shape[1], MLA_W), lambda bi, i: (bi, 0, 0))
        kv_specs += [spec, spec]
        kv_args += [k, v]
    return pl.pallas_call(
        functools.partial(_attn_kernel, n_kv=len(kv_pairs)),
        grid=(b, l // tq),
        in_specs=[pl.BlockSpec((1, tq, MLA_W), lambda bi, i: (bi, i, 0))] + kv_specs,
        out_specs=pl.BlockSpec((1, tq, MLA_W), lambda bi, i: (bi, i, 0)),
        out_shape=jax.ShapeDtypeStruct((b, l, MLA_W), BF16),
        compiler_params=pltpu.CompilerParams(
            dimension_semantics=("parallel", "parallel"), vmem_limit_bytes=VMEM_LIMIT),
        name="attn",
    )(q, *kv_args)


def _expert_cap(n_tok):
    return -(-n_tok // FFN_TILE) * FFN_TILE


def _sum01(x):
    return jnp.sum(jnp.sum(x, axis=1, keepdims=True), axis=0, keepdims=True)


def _route(logits_t, bias_col, cnt_col, cap):
    t = logits_t.shape[1]
    gsz = N_EXPERTS // N_GROUPS
    scores = jax.nn.sigmoid(logits_t)
    sel = (scores + bias_col).reshape(N_GROUPS, gsz, t)
    scores = scores.reshape(N_GROUPS, gsz, t)
    neg = jnp.float32(-jnp.inf)
    ie = lax.broadcasted_iota(I32, (N_GROUPS, gsz, t), 1)
    ig = lax.broadcasted_iota(I32, (N_GROUPS, gsz, t), 0)
    m1 = jnp.max(sel, axis=1, keepdims=True)
    first = jnp.min(jnp.where(sel == m1, ie, gsz), axis=1, keepdims=True)
    m2 = jnp.max(jnp.where(ie == first, neg, sel), axis=1, keepdims=True)
    grp = m1 + m2
    igk = lax.broadcasted_iota(I32, (N_GROUPS, 1, t), 0)
    g_sel = jnp.zeros((N_GROUPS, 1, t), jnp.bool_)
    cur = grp
    for _ in range(TOPK_GROUPS):
        m = jnp.max(cur, axis=0, keepdims=True)
        pick = igk == jnp.min(jnp.where(cur == m, igk, N_GROUPS), axis=0, keepdims=True)
        g_sel = g_sel | pick
        cur = jnp.where(pick, neg, cur)
    cur = jnp.where(g_sel, sel, neg)
    idx = ig * gsz + ie
    e_sel = jnp.zeros((N_GROUPS, gsz, t), jnp.bool_)
    picks = []
    for _ in range(TOP_K):
        m = jnp.max(jnp.max(cur, axis=1, keepdims=True), axis=0, keepdims=True)
        cand = jnp.where(cur == m, idx, N_EXPERTS)
        pick = idx == jnp.min(jnp.min(cand, axis=1, keepdims=True), axis=0, keepdims=True)
        picks.append(pick)
        e_sel = e_sel | pick
        cur = jnp.where(pick, neg, cur)
    w = jnp.where(e_sel, scores, 0.0)
    gate = w / _sum01(w) * ROUTED_SCALE
    sel_f = e_sel.astype(F32).reshape(N_EXPERTS, t)
    earlier = (lax.broadcasted_iota(I32, (t, t), 0) < lax.broadcasted_iota(I32, (t, t), 1))
    rank = _dot(sel_f.astype(BF16), earlier.astype(BF16))
    base = lax.broadcasted_iota(I32, (N_EXPERTS, 1), 0).astype(F32) * float(cap) + cnt_col
    slot = (base + rank).reshape(N_GROUPS, gsz, t)
    slot8 = jnp.concatenate([_sum01(jnp.where(pk, slot, 0.0)).reshape(1, t) for pk in picks],
                            axis=0).astype(I32)
    w8 = jnp.concatenate([_sum01(jnp.where(pk, gate, 0.0)).reshape(1, t) for pk in picks], axis=0)
    return slot8, w8, cnt_col + jnp.sum(sel_f, axis=1, keepdims=True)


def _outproj_kernel(og_ref, om_ref, x_ref, mod_ref, wtop_ref, wbot_ref, g2_ref, wr_ref, br_ref,
                    x1_ref, h2pa_ref, h2pb_ref, slot_ref, w8_ref, cnt_ref, cnt_scr, *, cap):
    @pl.when((pl.program_id(0) == 0) & (pl.program_id(1) == 0))
    def _():
        cnt_scr[...] = jnp.zeros_like(cnt_scr)

    mod = mod_ref[0]
    cnt = cnt_scr[:, 0:1]
    for r0 in range(0, x_ref.shape[1], OUT_SUB):
        rs = slice(r0, r0 + OUT_SUB)
        mix = _dot(og_ref[0, rs, :], wtop_ref[...]) + _dot(om_ref[0, rs, :], wbot_ref[...])
        x1 = x_ref[0, rs, :] + mod[2:3] * mix
        x1_ref[0, rs, :] = x1
        h2 = _rms(x1, D_MODEL) * g2_ref[...] * (1.0 + mod[4:5]) + mod[3:4]
        h2pa_ref[0, rs, :], h2pb_ref[0, rs, :] = _pack_rows(h2)
        logits_t = _dot_nt(wr_ref[...], h2, precision=HIGHEST)
        slot8, w8, cnt = _route(logits_t, br_ref[...], cnt, cap)
        slot_ref[:, rs] = slot8
        wb = lax.bitcast_convert_type(w8.astype(BF16).astype(F32), U32)
        wb = wb | (wb >> 16)
        w8_ref[0, rs, :] = jnp.concatenate(
            [jnp.broadcast_to(wb[k:k + 1], (SC_LANES, OUT_SUB)) for k in range(TOP_K)], axis=0).T
    cnt_scr[...] = jnp.broadcast_to(cnt, cnt_scr.shape)
    cnt_ref[...] = cnt_scr[...]


def _outproj(og, om, x, mod, per_batch_mod, p, tm):
    b, l, d = x.shape
    nt = l // tm
    mod_map = (lambda bi, i: (bi, 0, 0)) if per_batch_mod else (lambda bi, i: (0, 0, 0))
    tok = lambda w: pl.BlockSpec((1, tm, w), lambda bi, i: (bi, i, 0))
    full = lambda a: pl.BlockSpec(a.shape, lambda bi, i: (0, 0))
    weights = [p['w_out_top'], p['w_out_bot'], p['g_norm2'], p['w_router_t'], p['b_router_col']]
    cnt_shape = (N_EXPERTS, LANES)
    return pl.pallas_call(
        functools.partial(_outproj_kernel, cap=_expert_cap(b * l)),
        grid=(b, nt),
        in_specs=[tok(GLA_VAL_W), tok(MLA_W), tok(d), pl.BlockSpec((1, 6, d), mod_map)]
                 + [full(w) for w in weights],
        out_specs=[tok(d), tok(PACK_W), tok(PACK_W),
                   pl.BlockSpec((TOP_K, tm), lambda bi, i: (0, bi * nt + i)),
                   tok(LANES), pl.BlockSpec(cnt_shape, lambda bi, i: (0, 0))],
        out_shape=[jax.ShapeDtypeStruct((b, l, d), F32), jax.ShapeDtypeStruct((b, l, PACK_W), U32),
                   jax.ShapeDtypeStruct((b, l, PACK_W), U32),
                   jax.ShapeDtypeStruct((TOP_K, b * l), I32),
                   jax.ShapeDtypeStruct((b, l, LANES), U32),
                   jax.ShapeDtypeStruct(cnt_shape, F32)],
        scratch_shapes=[pltpu.VMEM(cnt_shape, F32)],
        compiler_params=pltpu.CompilerParams(
            dimension_semantics=("arbitrary", "arbitrary"), vmem_limit_bytes=VMEM_LIMIT),
        name="outproj",
    )(og, om, x, mod, *weights)


def _sc_mesh():
    return plsc.VectorSubcoreMesh(core_axis_name="c", subcore_axis_name="s")


def _sc_scatter_rows(src, slot8, n_rows_out):
    t, d = src.shape
    nk = slot8.shape[0]

    @functools.partial(pl.kernel, out_type=jax.ShapeDtypeStruct((n_rows_out, d), src.dtype),
                       mesh=_sc_mesh(), scratch_types=[], name="dispatch")
    def run(src_hbm, slot_hbm, out_hbm):
        def body(x_vmem, i_vmem):
            for k in range(nk):
                pltpu.sync_copy(x_vmem, out_hbm.at[i_vmem.at[k]])

        pltpu.emit_pipeline(
            body, grid=(t // SC_WINDOW,),
            in_specs=[pl.BlockSpec((SC_WINDOW, d), lambda i: (i, 0)),
                      pl.BlockSpec((nk, SC_WINDOW), lambda i: (0, i))],
            out_specs=[], core_axis_name=("c", "s"),
            dimension_semantics=(pltpu.PARALLEL,))(src_hbm, slot_hbm)

    return run(src, slot8)


def _sc_collect_sum(table, slots_tk, w_rep):
    n_tok = w_rep.shape[0]
    d = table.shape[1]
    group = SC_WINDOW // TOP_K
    chunks = d // SC_LANES

    @functools.partial(pl.kernel, out_type=jax.ShapeDtypeStruct((n_tok, 2 * d), F32),
                       mesh=_sc_mesh(),
                       scratch_types=[pltpu.VMEM((SC_GATHERS, SC_WINDOW, d), table.dtype),
                                      pltpu.SemaphoreType.DMA((SC_GATHERS,))],
                       compiler_params=pltpu.CompilerParams(needs_layout_passes=False),
                       name="collect")
    def run(tab_hbm, slot_hbm, w_hbm, out_hbm, rows_v, sems):
        def gather(i_vmem, b):
            return pltpu.make_async_copy(tab_hbm.at[i_vmem.at[b]], rows_v.at[b], sems.at[b])

        def body(i_vmem, w_vmem, o_vmem):
            for b in range(SC_GATHERS):
                gather(i_vmem, b).start()
            for b in range(SC_GATHERS):
                gather(i_vmem, b).wait()

                @pl.loop(0, group)
                def _(j):
                    tok = b * group + j
                    wv = [plsc.bitcast(w_vmem[tok, pl.ds(k * SC_LANES, SC_LANES)], BF16)
                          for k in range(TOP_K)]

                    def chunk(off):
                        prods = [plsc.bitcast(
                            rows_v[b, j * TOP_K + k, pl.ds(off, SC_LANES)], BF16) * wv[k]
                            for k in range(TOP_K)]
                        lo = jnp.zeros((SC_LANES,), F32)
                        hi = jnp.zeros((SC_LANES,), F32)
                        for k in range(0, TOP_K, 2):
                            u = plsc.bitcast(prods[k] + prods[k + 1], U32)
                            lo = lo + lax.bitcast_convert_type(u << 16, F32)
                            hi = hi + lax.bitcast_convert_type(u & jnp.uint32(0xFFFF0000), F32)
                        o_vmem[tok, pl.ds(off, SC_LANES)] = lo
                        o_vmem[tok, pl.ds(d + off, SC_LANES)] = hi

                    @plsc.parallel_loop(0, chunks, unroll=2)
                    def _(c):
                        chunk(c * SC_LANES)

        step_tok = SC_GATHERS * group
        pltpu.emit_pipeline(
            body, grid=(n_tok // step_tok,),
            in_specs=[pl.BlockSpec((SC_GATHERS, SC_WINDOW), lambda i: (i, 0)),
                      pl.BlockSpec((step_tok, TOP_K * SC_LANES), lambda i: (i, 0))],
            out_specs=[pl.BlockSpec((step_tok, 2 * d), lambda i: (i, 0))],
            core_axis_name=("c", "s"),
            dimension_semantics=(pltpu.PARALLEL,))(slot_hbm, w_hbm, out_hbm)

    return run(table, slots_tk.reshape(-1, SC_WINDOW), w_rep)


def _ffn_kernel(cnt_ref, nt_ref, first_ref, te_ref, tt_ref, xsa_hbm, xsb_hbm,
                wg_ref, wu_ref, wd_ref, ysa_hbm, ysb_hbm,
                xbuf, ybuf, wg_b, wu_b, wd_b, sem_in, sem_out, *, cap):
    e = pl.program_id(0)
    n_exp = pl.num_programs(0)
    n = nt_ref[e]
    g0 = first_ref[e]
    total = first_ref[n_exp - 1] + nt_ref[n_exp - 1]
    xs_hbm = (xsa_hbm, xsb_hbm)
    ys_hbm = (ysa_hbm, ysb_hbm)
    n_in = FFN_AHEAD + 1
    half = FFN_TILE // 2

    def at_height(ex, t, fn):
        few = cnt_ref[ex] - t * FFN_TILE <= half

        @pl.when(few)
        def _():
            fn(half)

        @pl.when(jnp.logical_not(few))
        def _():
            fn(FFN_TILE)

    def rows(ex, t, n_rows):
        return pl.ds(pl.multiple_of(ex * cap + t * FFN_TILE, FFN_TILE), n_rows)

    def in_copy(ex, t, slot, part, n_rows):
        return pltpu.make_async_copy(xs_hbm[part].at[rows(ex, t, n_rows)],
                                     xbuf.at[slot, part, pl.ds(0, n_rows)],
                                     sem_in.at[slot, part])

    def out_copy(ex, t, slot, part, n_rows):
        return pltpu.make_async_copy(ybuf.at[slot, part, pl.ds(0, n_rows)],
                                     ys_hbm[part].at[rows(ex, t, n_rows)],
                                     sem_out.at[slot, part])

    def start_in(g):
        ex, t = te_ref[g], tt_ref[g]
        at_height(ex, t, lambda nr: [in_copy(ex, t, g % n_in, part, nr).start()
                                     for part in range(PACK_PARTS)])

    def wait_out(g):
        ex, t = te_ref[g], tt_ref[g]
        at_height(ex, t, lambda nr: [out_copy(ex, t, g % 2, part, nr).wait()
                                     for part in range(PACK_PARTS)])

    @pl.when(n > 0)
    def _():
        @pl.when(g0 == 0)
        def _():
            for ahead in range(FFN_AHEAD):
                @pl.when(ahead < total)
                def _():
                    start_in(ahead)

        wg_b[...] = wg_ref[0].astype(BF16)
        wu_b[...] = wu_ref[0].astype(BF16)
        wd_b[...] = wd_ref[0].astype(BF16)

        def tile(t, carry):
            g = g0 + t
            slot = g % 2
            slot_in = g % n_in
            at_height(e, t, lambda nr: [in_copy(e, t, slot_in, part, nr).wait()
                                        for part in range(PACK_PARTS)])

            @pl.when(g + FFN_AHEAD < total)
            def _():
                start_in(g + FFN_AHEAD)

            @pl.when(g >= 2)
            def _():
                wait_out(g - 2)

            def compute(n_rows):
                x = _unpack_rows([xbuf[slot_in, part, 0:n_rows, :]
                                  for part in range(PACK_PARTS)]).astype(BF16)
                a = _silu(_dot(x, wg_b[...])) * _dot(x, wu_b[...])
                y = _pack_rows(_dot(a.astype(BF16), wd_b[...]))
                for part in range(PACK_PARTS):
                    ybuf[slot, part, 0:n_rows, :] = y[part]
                    out_copy(e, t, slot, part, n_rows).start()

            at_height(e, t, compute)
            return carry

        lax.fori_loop(0, n, tile, 0)

        @pl.when(g0 + n == total)
        def _():
            @pl.when(total >= 2)
            def _():
                wait_out(total - 2)

            wait_out(total - 1)


def _ffn(xs_parts, counts, w_gate, w_up, w_down):
    n_exp, d = w_gate.shape[0], w_gate.shape[1]
    cap = xs_parts[0].shape[0] // n_exp
    ntiles = (counts + FFN_TILE - 1) // FFN_TILE
    cum = jnp.cumsum(ntiles)
    first = cum - ntiles
    max_tiles = (cap * TOP_K) // FFN_TILE + n_exp
    g = jnp.arange(max_tiles, dtype=I32)
    done = g[:, None] >= cum[None, :]
    tile_e = jnp.minimum(jnp.sum(done, axis=1), n_exp - 1).astype(I32)
    tile_t = (g - jnp.sum(jnp.where(done, ntiles[None, :], 0), axis=1)).astype(I32)
    wmap = lambda e, cnt, nt, first, te, tt: (e, 0, 0)
    hbm = pl.BlockSpec(memory_space=pl.ANY)
    return pl.pallas_call(
        functools.partial(_ffn_kernel, cap=cap),
        grid_spec=pltpu.PrefetchScalarGridSpec(
            num_scalar_prefetch=5, grid=(n_exp,),
            in_specs=[hbm] * PACK_PARTS
                     + [pl.BlockSpec((1, d, D_EXPERT), wmap), pl.BlockSpec((1, d, D_EXPERT), wmap),
                        pl.BlockSpec((1, D_EXPERT, d), wmap)],
            out_specs=[hbm] * PACK_PARTS,
            scratch_shapes=[pltpu.VMEM((FFN_AHEAD + 1, PACK_PARTS, FFN_TILE, PACK_W), U32),
                            pltpu.VMEM((2, PACK_PARTS, FFN_TILE, PACK_W), U32),
                            pltpu.VMEM((d, D_EXPERT), BF16), pltpu.VMEM((d, D_EXPERT), BF16),
                            pltpu.VMEM((D_EXPERT, d), BF16),
                            pltpu.SemaphoreType.DMA((FFN_AHEAD + 1, PACK_PARTS)),
                            pltpu.SemaphoreType.DMA((2, PACK_PARTS))]),
        out_shape=[jax.ShapeDtypeStruct(xs_parts[0].shape, U32)] * PACK_PARTS,
        compiler_params=pltpu.CompilerParams(
            dimension_semantics=("arbitrary",), vmem_limit_bytes=VMEM_LIMIT),
        name="ffn",
    )(counts.astype(I32), ntiles.astype(I32), first.astype(I32), tile_e, tile_t, *xs_parts,
      w_gate, w_up, w_down)


def _combine_kernel(x1_ref, h2pa_ref, h2pb_ref, ra_ref, rb_ref, mod_ref,
                    wsg_ref, wsu_ref, wsd_ref, o_ref):
    h2 = _unpack_rows([h2pa_ref[0], h2pb_ref[0]]).astype(BF16)
    a = _silu(_dot(h2, wsg_ref[...])) * _dot(h2, wsu_ref[...])
    routed = jnp.concatenate([ra_ref[0], rb_ref[0]], axis=1)
    o_ref[0] = x1_ref[0] + mod_ref[0][5:6] * (_dot(a.astype(BF16), wsd_ref[...]) + routed)


def _combine(x1, h2p_parts, routed_parts, mod, per_batch_mod, p, tm):
    b, l, d = x1.shape
    mod_map = (lambda bi, i: (bi, 0, 0)) if per_batch_mod else (lambda bi, i: (0, 0, 0))
    tok = lambda w: pl.BlockSpec((1, tm, w), lambda bi, i: (bi, i, 0))
    full = lambda a: pl.BlockSpec(a.shape, lambda bi, i: (0, 0))
    weights = [p['w_sh_gate'], p['w_sh_up'], p['w_sh_down']]
    routed_parts = [r.reshape(b, l, d // PACK_PARTS) for r in routed_parts]
    return pl.pallas_call(
        _combine_kernel,
        grid=(b, l // tm),
        in_specs=[tok(d)] + [tok(PACK_W)] * PACK_PARTS + [tok(d // PACK_PARTS)] * PACK_PARTS
                 + [pl.BlockSpec((1, 6, d), mod_map)] + [full(w) for w in weights],
        out_specs=tok(d),
        out_shape=jax.ShapeDtypeStruct((b, l, d), F32),
        compiler_params=pltpu.CompilerParams(
            dimension_semantics=("parallel", "parallel"), vmem_limit_bytes=VMEM_LIMIT),
        name="combine",
    )(x1, *h2p_parts, *routed_parts, mod, *weights)


def _pad_heads(w, parts):
    k = w.shape[0]
    per = w.shape[1] // MLA_HEADS
    w = w.reshape(k, MLA_HEADS, per)[:, :, parts[0]:parts[1]]
    w = jnp.pad(w, ((0, 0), (0, 0), (0, HEAD_W - (parts[1] - parts[0]))))
    return w.reshape(k, MLA_HEADS * HEAD_W)


def _prep_params(l, g_norm1, g_norm2, w_in, w_gk_fwd, b_gk_fwd, w_gk_bwd, b_gk_bwd, g_gla_out,
                 g_q_lora, w_uq, g_kv_lora, w_ukv, g_qk_q, g_qk_k, w_out, w_router, b_router,
                 w_sh_gate, w_sh_up, w_sh_down):
    w = w_in[l]
    d = w.shape[0]
    o_lrf = COL_CQ
    o_lrb = o_lrf + GLA_GATE_RANK
    o_cq = o_lrb + GLA_GATE_RANK
    o_ckv = o_cq + Q_LORA
    o_kr = o_ckv + KV_LORA
    w_in_p = jnp.concatenate([
        w[:, 0:COL_CQ], w[:, o_cq:o_ckv], w[:, o_ckv:o_kr], jnp.zeros((d, KR_LANE0), w.dtype),
        w[:, o_kr:o_kr + ROPE_DIM], w[:, o_lrf:o_lrb], w[:, o_lrb:o_cq]], axis=1).astype(BF16)
    z = jnp.zeros((GLA_GATE_RANK, GLA_KEY_W), F32)
    w_gk_big = jnp.concatenate([
        jnp.zeros((LANES - 2 * GLA_GATE_RANK, 2 * GLA_KEY_W), F32),
        jnp.concatenate([w_gk_fwd[l], z], axis=1),
        jnp.concatenate([z, w_gk_bwd[l]], axis=1)], axis=0).astype(BF16)
    pad_gain = lambda g: jnp.pad(g, (0, HEAD_W - QK_DIM)).reshape(1, HEAD_W)
    w_ukv_h = w_ukv[l]
    return {
        'g_norm1': g_norm1[l].reshape(1, d), 'g_norm2': g_norm2[l].reshape(1, d),
        'w_in_p': w_in_p, 'w_gk_big': w_gk_big,
        'b_gk': jnp.concatenate([b_gk_fwd[l], b_gk_bwd[l]]).reshape(1, 2 * GLA_KEY_W),
        'g_gla_out': g_gla_out[l].reshape(1, GLA_DV),
        'g_q_lora': g_q_lora[l].reshape(1, Q_LORA),
        'w_uq_p': _pad_heads(w_uq[l], (0, QK_DIM)).astype(BF16),
        'g_kv_lora': g_kv_lora[l].reshape(1, KV_LORA),
        'w_ukv_p': jnp.concatenate([_pad_heads(w_ukv_h, (0, NOPE_DIM)),
                                    _pad_heads(w_ukv_h, (NOPE_DIM, NOPE_DIM + V_DIM))],
                                   axis=1).astype(BF16),
        'gq': pad_gain(g_qk_q[l]), 'gk': pad_gain(g_qk_k[l]),
        'w_out_top': w_out[l][:GLA_VAL_W].astype(BF16),
        'w_out_bot': w_out[l][GLA_VAL_W:].astype(BF16),
        'w_router_t': w_router[l].T, 'b_router_col': b_router[l].reshape(N_EXPERTS, 1),
        'w_sh_gate': w_sh_gate[l].astype(BF16), 'w_sh_up': w_sh_up[l].astype(BF16),
        'w_sh_down': w_sh_down[l].astype(BF16),
    }


def _rope_tables(length):
    pos = np.arange(length)
    r = (pos // GRID_W).astype(np.float32)
    c = (pos % GRID_W).astype(np.float32)
    half = ROPE_DIM // 4
    inv_freq = np.float32(ROPE_THETA) ** (-np.arange(half, dtype=np.float32) / np.float32(half))
    ang_r = r[:, None] * inv_freq[None, :]
    ang_c = c[:, None] * inv_freq[None, :]
    zeros = lambda w: np.zeros((length, w), np.float32)
    ones = lambda w: np.ones((length, w), np.float32)
    tail_w = HEAD_W - ROPE_LANE0 - ROPE_DIM
    cos = np.concatenate([ones(ROPE_LANE0), np.cos(ang_r), np.cos(ang_r), np.cos(ang_c),
                          np.cos(ang_c), ones(tail_w)], axis=1)
    s1 = np.concatenate([zeros(ROPE_LANE0), -np.sin(ang_r), zeros(half), -np.sin(ang_c),
                         zeros(half), zeros(tail_w)], axis=1)
    s2 = np.concatenate([zeros(ROPE_LANE0), zeros(half), np.sin(ang_r), zeros(half),
                         np.sin(ang_c), zeros(tail_w)], axis=1)
    return tuple(jnp.asarray(t, F32) for t in (cos, s1, s2))


def _mix_route_dispatch(x_tok, seq_shape, mod, per_batch_mod, p, s0f, s0b, rope_tabs, ctx_kv,
                        tq):
    bt, lt, d = x_tok.shape
    b, l = seq_shape
    n_tok = bt * lt
    (q, k, v, g, lgf, lgb, qm, km, vm, ckvn, kr) = _inproj(x_tok, mod, per_batch_mod, p,
                                                            rope_tabs, IN_TILE)
    seq = lambda a: a.reshape(b, l, a.shape[-1])
    og, sf, sb = _gla(seq(q), seq(k), seq(v), seq(g), seq(lgf), seq(lgb), s0f, s0b,
                      p['g_gla_out'], GLA_SEQS if l == GLA_BLOCK else 1)
    kv_pairs = [(seq(km), seq(vm))] + ([ctx_kv] if ctx_kv is not None else [])
    om = _attn(seq(qm), kv_pairs, tq)
    tokv = lambda a: a.reshape(bt, lt, a.shape[-1])
    x1, h2a, h2b, slot8, w8, cnt = _outproj(tokv(og), tokv(om), x_tok, mod, per_batch_mod, p,
                                            OUT_TILE)
    x_sorted = [_sc_scatter_rows(h.reshape(n_tok, PACK_W), slot8,
                                 N_EXPERTS * _expert_cap(n_tok)) for h in (h2a, h2b)]
    routed = dict(x1=x1, h2=(h2a, h2b), slot8=slot8, w8=w8, cnt=cnt, x_sorted=x_sorted)
    return routed, sf, sb, ckvn, kr


def _experts_collect(r, w_gate, w_up, w_down):
    n_tok = r['slot8'].shape[1]
    y_sorted = _ffn(r['x_sorted'], r['cnt'][:, 0].astype(I32), w_gate, w_up, w_down)
    slots_tk = r['slot8'].T.reshape(1, TOP_K * n_tok)
    w_rep = r['w8'].reshape(n_tok, LANES)
    return [_sc_collect_sum(y, slots_tk, w_rep) for y in y_sorted]


def kernel(x_prompt, x_sample, c, state_gla_fwd, state_gla_bwd, cache_mla_ckv, cache_mla_krope,
           c_ctx, w_ada, b_ada, g_norm1, g_norm2, w_in, w_gk_fwd, b_gk_fwd, w_gk_bwd, b_gk_bwd,
           g_gla_out, g_q_lora, w_uq, g_kv_lora, w_ukv, g_qk_q, g_qk_k, w_out,
           w_router, b_router, w_exp_gate, w_exp_up, w_exp_down, w_sh_gate, w_sh_up, w_sh_down):
    bp, lp, d = x_prompt.shape
    bs, ls, _ = x_sample.shape
    depth = w_ada.shape[0]
    xp = x_prompt.reshape(1, bp * lp, d)
    xs = x_sample
    new_f, new_b, new_ckv, new_kr = [], [], [], []
    lat_tabs = _rope_tables(ls)
    cvecs = jnp.concatenate([c_ctx[None], c, jnp.zeros((8 - 1 - bs, d), F32)], axis=0)
    for l in range(depth):
        p = _prep_params(l, g_norm1, g_norm2, w_in, w_gk_fwd, b_gk_fwd, w_gk_bwd, b_gk_bwd,
                         g_gla_out, g_q_lora, w_uq, g_kv_lora, w_ukv, g_qk_q, g_qk_k, w_out,
                         w_router, b_router, w_sh_gate, w_sh_up, w_sh_down)
        mod = _ada(cvecs, w_ada[l], b_ada[l]).reshape(8, 6, d)
        mod_p, mod_s = mod[0:1], mod[1:1 + bs]
        zeros = jnp.zeros((bp, GLA_KEY_W, GLA_DV), F32)
        r_ctx, sf, sb, ckvn, kr = _mix_route_dispatch(
            xp, (bp, lp), mod_p, False, p, zeros, zeros, (), None, lp)
        new_f.append(sf.reshape(bp, GLA_HEADS, GLA_DK, GLA_DV))
        new_b.append(sb.reshape(bp, GLA_HEADS, GLA_DK, GLA_DV))
        new_ckv.append(ckvn.reshape(bp, lp, KV_LORA))
        new_kr.append(kr.reshape(bp, lp, LANES)[:, :, KR_LANE0:KR_LANE0 + ROPE_DIM])
        kr_cache = jnp.pad(cache_mla_krope[:, l],
                           ((0, 0), (0, 0), (ROPE_LANE0, LANES - ROPE_LANE0 - ROPE_DIM)))
        ctx_kv = _cache_kv(cache_mla_ckv[:, l], kr_cache, p)
        s0f = state_gla_fwd[:, l].reshape(bs, GLA_KEY_W, GLA_DV)
        s0b = state_gla_bwd[:, l].reshape(bs, GLA_KEY_W, GLA_DV)
        r_lat, _, _, _, _ = _mix_route_dispatch(
            xs, (bs, ls), mod_s, True, p, s0f, s0b, lat_tabs, ctx_kv, 256)
        experts = (w_exp_gate[l], w_exp_up[l], w_exp_down[l])
        routed_ctx = _experts_collect(r_ctx, *experts)
        routed_lat = _experts_collect(r_lat, *experts)
        xp = _combine(r_ctx['x1'], r_ctx['h2'], routed_ctx, mod_p, False, p, OUT_TILE)
        xs = _combine(r_lat['x1'], r_lat['h2'], routed_lat, mod_s, True, p, OUT_TILE)
    return (xp.reshape(bp, lp, d), xs, jnp.stack(new_f, axis=1), jnp.stack(new_b, axis=1),
            jnp.stack(new_ckv, axis=1), jnp.stack(new_kr, axis=1))
```

```python
import functools

import jax
import jax.numpy as jnp
import numpy as np
from jax import lax
from jax.experimental import pallas as pl
from jax.experimental.pallas import tpu as pltpu
from jax.experimental.pallas import tpu_sc as plsc

F32 = jnp.float32
BF16 = jnp.bfloat16
I32 = jnp.int32
U32 = jnp.uint32

D_MODEL = 1024
EPS = 1e-6
GRID_W = 64
GLA_HEADS = 4
GLA_DK = 64
GLA_DV = 128
GLA_GATE_RANK = 16
GLA_GATE_NORM = 16.0
GLA_KEY_W = GLA_HEADS * GLA_DK
GLA_VAL_W = GLA_HEADS * GLA_DV
MLA_HEADS = 4
Q_LORA = 256
KV_LORA = 128
NOPE_DIM = 64
ROPE_DIM = 32
V_DIM = 128
QK_DIM = NOPE_DIM + ROPE_DIM
ROPE_THETA = 10000.0
N_EXPERTS = 64
TOP_K = 8
N_GROUPS = 8
TOPK_GROUPS = 4
D_EXPERT = 256
ROUTED_SCALE = 2.5

LANES = 128
HEAD_W = LANES
MLA_W = MLA_HEADS * HEAD_W
ROPE_LANE0 = NOPE_DIM
COL_V = 2 * GLA_KEY_W
COL_G = COL_V + GLA_VAL_W
COL_CQ = COL_G + GLA_VAL_W
COL_CKV = COL_CQ + Q_LORA
TAIL0 = COL_CKV + KV_LORA
IN_W = TAIL0 + LANES
KR_LANE0 = LANES - ROPE_DIM - 2 * GLA_GATE_RANK
GLA_BLOCK = 256
GLA_SUB = 64
GLA_SEQS = 2
ATTN_SEQS = 4
IN_TILE = 1024
IN_SUB = 256
OUT_TILE = 1024
FFN_TILE = 512
FFN_AHEAD = 5
PACK_PARTS = 2
PACK_W = D_MODEL // (2 * PACK_PARTS)
SC_WINDOW = 128
SC_LANES = 16
SC_GATHERS = 2
VMEM_LIMIT = 56 * 1024 * 1024

HIGHEST = lax.Precision.HIGHEST


def _dot(a, b, precision=None):
    return jnp.dot(a, b, preferred_element_type=F32, precision=precision)


def _dot_nt(a, b, precision=None):
    return lax.dot_general(a, b, (((1,), (1,)), ((), ())), preferred_element_type=F32,
                           precision=precision)


def _rms(x, width):
    ss = jnp.sum(x * x, axis=-1, keepdims=True) * (1.0 / width)
    return x * lax.rsqrt(ss + EPS)


def _silu(x):
    return x * jax.nn.sigmoid(x)


def _log_sigmoid(x):
    return jnp.minimum(x, 0.0) - jnp.log1p(jnp.exp(-jnp.abs(x)))


def _pack_rows(x):
    parts = []
    for i in range(PACK_PARTS):
        c0 = i * 2 * PACK_W
        lo = lax.bitcast_convert_type(x[:, c0:c0 + PACK_W].astype(BF16).astype(F32), U32)
        hi = lax.bitcast_convert_type(
            x[:, c0 + PACK_W:c0 + 2 * PACK_W].astype(BF16).astype(F32), U32)
        parts.append(hi | (lo >> 16))
    return parts


def _unpack_rows(parts):
    cols = []
    for w in parts:
        cols.append(lax.bitcast_convert_type(w << 16, F32))
        cols.append(lax.bitcast_convert_type(w & jnp.uint32(0xFFFF0000), F32))
    return jnp.concatenate(cols, axis=1)


def _ada_kernel(c_ref, w_ref, b_ref, o_ref):
    o_ref[...] = _dot(_silu(c_ref[...]), w_ref[...], precision=HIGHEST) + b_ref[...]


def _ada(cvecs, w_ada, b_ada):
    n = w_ada.shape[1]
    tn = 768
    return pl.pallas_call(
        _ada_kernel,
        grid=(n // tn,),
        in_specs=[pl.BlockSpec((8, D_MODEL), lambda j: (0, 0)),
                  pl.BlockSpec((D_MODEL, tn), lambda j: (0, j)),
                  pl.BlockSpec((1, tn), lambda j: (0, j))],
        out_specs=pl.BlockSpec((8, tn), lambda j: (0, j)),
        out_shape=jax.ShapeDtypeStruct((8, n), F32),
        name="ada",
    )(cvecs, w_ada, b_ada.reshape(1, n))


def _rope(x, c, s1, s2):
    return x * c + pltpu.roll(x, LANES - 8, 1) * s1 + pltpu.roll(x, 8, 1) * s2


def _mla_kv(ckv_n, kr, w_ukv_ref, gk, rope_tabs, k_ref, v_ref, rs=slice(None)):
    kv = _dot(ckv_n.astype(BF16), w_ukv_ref[...])
    for h in range(MLA_HEADS):
        k_h = kv[:, h * HEAD_W:(h + 1) * HEAD_W] + kr
        k_h = _rms(k_h, QK_DIM) * gk
        if rope_tabs is not None:
            k_h = _rope(k_h, *rope_tabs)
        k_ref[0, rs, h * HEAD_W:(h + 1) * HEAD_W] = k_h.astype(BF16)
    v_ref[0, rs, :] = kv[:, MLA_W:].astype(BF16)


def _lane_mask(lo, hi, rows):
    lane = lax.broadcasted_iota(I32, (rows, LANES), 1)
    return (lane >= lo) & (lane < hi)


def _inproj_kernel(x_ref, mod_ref, g1_ref, win_ref, wgk_ref, bgk_ref, gql_ref, wuq_ref,
                   gkv_ref, wukv_ref, gq_ref, gk_ref, *refs, positions):
    tab_refs, outs = (refs[:3], refs[3:]) if positions else ((), refs)
    (q_ref, k_ref, v_ref, g_ref, lgf_ref, lgb_ref, qm_ref, km_ref, vm_ref, ckvn_ref,
     kr_ref) = outs
    mod = mod_ref[0]
    for r0 in range(0, x_ref.shape[1], IN_SUB):
        rs = slice(r0, r0 + IN_SUB)
        x = x_ref[0, rs, :]
        h = _rms(x, D_MODEL) * g1_ref[...] * (1.0 + mod[1:2]) + mod[0:1]
        y = _dot(h.astype(BF16), win_ref[...])
        q_ref[0, rs, :] = (y[:, 0:GLA_KEY_W] * GLA_DK ** -0.5).astype(BF16)
        k_ref[0, rs, :] = y[:, GLA_KEY_W:COL_V].astype(BF16)
        v_ref[0, rs, :] = y[:, COL_V:COL_G].astype(BF16)
        g_ref[0, rs, :] = y[:, COL_G:COL_CQ].astype(BF16)
        tail = y[:, TAIL0:IN_W]
        pre = _dot(tail.astype(BF16), wgk_ref[...]) + bgk_ref[...]
        logg = _log_sigmoid(pre) * (1.0 / GLA_GATE_NORM)
        lgf_ref[0, rs, :] = logg[:, 0:GLA_KEY_W]
        lgb_ref[0, rs, :] = logg[:, GLA_KEY_W:]
        tabs = tuple(r[rs, :] for r in tab_refs) if positions else None
        cq = _rms(y[:, COL_CQ:COL_CKV], Q_LORA) * gql_ref[...]
        qm = _dot(cq.astype(BF16), wuq_ref[...])
        gq = gq_ref[...]
        for hh in range(MLA_HEADS):
            q_h = _rms(qm[:, hh * HEAD_W:(hh + 1) * HEAD_W], QK_DIM) * gq
            if positions:
                q_h = _rope(q_h, *tabs)
            qm_ref[0, rs, hh * HEAD_W:(hh + 1) * HEAD_W] = (q_h * QK_DIM ** -0.5).astype(BF16)
        ckv_n = _rms(y[:, COL_CKV:TAIL0], KV_LORA) * gkv_ref[...]
        ckvn_ref[0, rs, :] = ckv_n
        kr_ref[0, rs, :] = tail
        kr = jnp.where(_lane_mask(KR_LANE0, KR_LANE0 + ROPE_DIM, IN_SUB), tail, 0.0)
        _mla_kv(ckv_n, kr, wukv_ref, gk_ref[...], tabs, km_ref, vm_ref, rs)


def _inproj(x, mod, per_batch_mod, p, rope_tabs, tm):
    b, l, d = x.shape
    nt = l // tm
    mod_map = (lambda bi, i: (bi, 0, 0)) if per_batch_mod else (lambda bi, i: (0, 0, 0))
    tab_map = lambda bi, i: (i, 0)
    const = lambda bi, i: (0, 0)
    tok = lambda w: pl.BlockSpec((1, tm, w), lambda bi, i: (bi, i, 0))
    full = lambda a: pl.BlockSpec(a.shape, const)
    weights = [p['g_norm1'], p['w_in_p'], p['w_gk_big'], p['b_gk'], p['g_q_lora'], p['w_uq_p'],
               p['g_kv_lora'], p['w_ukv_p'], p['gq'], p['gk']]
    outs = [(GLA_KEY_W, BF16), (GLA_KEY_W, BF16), (GLA_VAL_W, BF16), (GLA_VAL_W, BF16),
            (GLA_KEY_W, F32), (GLA_KEY_W, F32), (MLA_W, BF16), (MLA_W, BF16), (MLA_W, BF16),
            (KV_LORA, F32), (LANES, F32)]
    return pl.pallas_call(
        functools.partial(_inproj_kernel, positions=bool(rope_tabs)),
        grid=(b, nt),
        in_specs=[tok(d), pl.BlockSpec((1, 6, d), mod_map)] + [full(w) for w in weights]
                 + [pl.BlockSpec((tm, LANES), tab_map)] * len(rope_tabs),
        out_specs=[tok(w) for w, _ in outs],
        out_shape=[jax.ShapeDtypeStruct((b, l, w), dt) for w, dt in outs],
        compiler_params=pltpu.CompilerParams(
            dimension_semantics=("parallel", "parallel"), vmem_limit_bytes=VMEM_LIMIT),
        name="inproj",
    )(x, mod, *weights, *rope_tabs)


def _cache_kv_kernel(ckv_ref, kr_ref, wukv_ref, gk_ref, k_ref, v_ref):
    _mla_kv(ckv_ref[0], kr_ref[0], wukv_ref, gk_ref[...], None, k_ref, v_ref)


def _cache_kv(ckv, kr128, p):
    b, l, _ = ckv.shape
    tok = lambda w: pl.BlockSpec((1, l, w), lambda bi: (bi, 0, 0))
    full = lambda a: pl.BlockSpec(a.shape, lambda bi: (0, 0))
    return pl.pallas_call(
        _cache_kv_kernel,
        grid=(b,),
        in_specs=[tok(KV_LORA), tok(LANES), full(p['w_ukv_p']), full(p['gk'])],
        out_specs=[tok(MLA_W), tok(MLA_W)],
        out_shape=[jax.ShapeDtypeStruct((b, l, MLA_W), BF16)] * 2,
        name="cache_kv",
    )(ckv, kr128, p['w_ukv_p'], p['gk'])


def _split3(x):
    hi = x.astype(BF16)
    r1 = x - hi.astype(F32)
    mid = r1.astype(BF16)
    lo = (r1 - mid.astype(F32)).astype(BF16)
    return hi, mid, lo


def _gla_block(q, k, v, lg, s, fwd):
    n = GLA_BLOCK
    nc = n // GLA_SUB
    row = lax.broadcasted_iota(I32, (n, n), 0)
    col = lax.broadcasted_iota(I32, (n, n), 1)
    same = (row // GLA_SUB) == (col // GLA_SUB)
    causal = same & ((col <= row) if fwd else (col >= row))
    tri = causal.astype(BF16)
    hi, mid, lo = _split3(lg)
    cum = _dot(tri, hi) + _dot(tri, mid) + _dot(tri, lo)
    tot_rows, mid_rows = [], []
    for c in range(nc):
        r_tot = c * GLA_SUB + (GLA_SUB - 1 if fwd else 0)
        r_mid = c * GLA_SUB + GLA_SUB // 2
        tot_rows.append(jnp.broadcast_to(cum[r_tot:r_tot + 1], (GLA_SUB, GLA_KEY_W)))
        mid_rows.append(jnp.broadcast_to(cum[r_mid:r_mid + 1], (GLA_SUB, GLA_KEY_W)))
    tot_rows = jnp.concatenate(tot_rows, axis=0)
    mid_rows = jnp.concatenate(mid_rows, axis=0)
    rel = cum - mid_rows
    qi = q * jnp.exp(rel)
    ki = (k * jnp.exp(-rel)).astype(BF16)
    q_in = (q * jnp.exp(cum)).astype(BF16)
    k_up = k * jnp.exp(tot_rows - cum)
    dec = jnp.exp(tot_rows)
    k_up_t = k_up.T.astype(BF16)
    dec_t = dec.T
    lane_head = lax.broadcasted_iota(I32, (n, GLA_KEY_W), 1) // GLA_DK
    srow_head = lax.broadcasted_iota(I32, (GLA_KEY_W, GLA_DV), 0) // GLA_DK
    vrow_chunk = lax.broadcasted_iota(I32, (n, GLA_VAL_W), 0) // GLA_SUB
    o_heads = []
    for h in range(GLA_HEADS):
        a = _dot_nt(jnp.where(lane_head == h, qi, 0.0).astype(BF16), ki)
        a = jnp.where(causal, a, 0.0).astype(BF16)
        o_heads.append(_dot(a, v[:, h * GLA_DV:(h + 1) * GLA_DV]))
    o = jnp.concatenate(o_heads, axis=1)
    o_inter = [None] * nc
    for c in (range(nc) if fwd else range(nc - 1, -1, -1)):
        s_bd = jnp.concatenate(
            [jnp.where(srow_head == h, s, 0.0).astype(BF16) for h in range(GLA_HEADS)], axis=1)
        o_inter[c] = _dot(q_in[c * GLA_SUB:(c + 1) * GLA_SUB], s_bd)
        v_c = jnp.where(vrow_chunk == c, v, jnp.zeros_like(v))
        u = jnp.concatenate(
            [_dot(k_up_t[h * GLA_DK:(h + 1) * GLA_DK], v_c[:, h * GLA_DV:(h + 1) * GLA_DV])
             for h in range(GLA_HEADS)], axis=0)
        s = dec_t[:, c * GLA_SUB:c * GLA_SUB + 1] * s + u
    return o + jnp.concatenate(o_inter, axis=0), s


def _gla_kernel(q_ref, k_ref, v_ref, g_ref, lgf_ref, lgb_ref, s0f_ref, s0b_ref, gout_ref,
                o_ref, sf_ref, sb_ref, acc_ref, st_ref, *, nblk, n_seq):
    for si in range(n_seq):
        _gla_sequence(si, q_ref, k_ref, v_ref, g_ref, lgf_ref, lgb_ref, s0f_ref, s0b_ref,
                      gout_ref, o_ref, sf_ref, sb_ref, acc_ref, st_ref, nblk)


def _gla_sequence(si, q_ref, k_ref, v_ref, g_ref, lgf_ref, lgb_ref, s0f_ref, s0b_ref, gout_ref,
                  o_ref, sf_ref, sb_ref, acc_ref, st_ref, nblk):
    st_ref[si, 0] = s0f_ref[si]
    st_ref[si, 1] = s0b_ref[si]

    def load(blk):
        start = blk * GLA_BLOCK
        r = pl.ds(start if isinstance(blk, int) else pl.multiple_of(start, GLA_BLOCK), GLA_BLOCK)
        return r, q_ref[si, r, :].astype(F32), k_ref[si, r, :].astype(F32), v_ref[si, r, :]

    def pair_step(i, first_touch):
        rf, qf, kf, vf = load(i)
        rb, qb, kb, vb = load(nblk - 1 - i)
        of, s_f = _gla_block(qf, kf, vf, lgf_ref[si, rf, :], st_ref[si, 0], True)
        ob, s_b = _gla_block(qb, kb, vb, lgb_ref[si, rb, :], st_ref[si, 1], False)
        if first_touch:
            acc_ref[si, rf, :] = of
            acc_ref[si, rb, :] = ob
        else:
            acc_ref[si, rf, :] += of
            acc_ref[si, rb, :] += ob
        st_ref[si, 0] = s_f
        st_ref[si, 1] = s_b

    if nblk == 1:
        r, q, k, v = load(0)
        of, s_f = _gla_block(q, k, v, lgf_ref[si, r, :], st_ref[si, 0], True)
        ob, s_b = _gla_block(q, k, v, lgb_ref[si, r, :], st_ref[si, 1], False)
        acc_ref[si, r, :] = of + ob
        st_ref[si, 0] = s_f
        st_ref[si, 1] = s_b
    else:
        half = nblk // 2
        lax.fori_loop(0, half, lambda i, c: (pair_step(i, True), c)[1], 0)
        lax.fori_loop(half, nblk, lambda i, c: (pair_step(i, False), c)[1], 0)
    sf_ref[si] = st_ref[si, 0]
    sb_ref[si] = st_ref[si, 1]

    def fin_step(blk, carry):
        r = pl.ds(pl.multiple_of(blk * GLA_BLOCK, GLA_BLOCK), GLA_BLOCK)
        o = acc_ref[si, r, :]
        gate = _silu(g_ref[si, r, :].astype(F32))
        for h in range(GLA_HEADS):
            sl = slice(h * GLA_DV, (h + 1) * GLA_DV)
            o_ref[si, r, sl] = (_rms(o[:, sl], GLA_DV) * gout_ref[...] * gate[:, sl]).astype(BF16)
        return carry

    lax.fori_loop(0, nblk, fin_step, 0)


def _gla(q, k, v, g, lgf, lgb, s0f, s0b, g_out, n_seq):
    b, l, _ = q.shape
    assert l == GLA_BLOCK or l % (2 * GLA_BLOCK) == 0
    seq = lambda w: pl.BlockSpec((n_seq, l, w), lambda bi: (bi, 0, 0))
    st = pl.BlockSpec((n_seq, GLA_KEY_W, GLA_DV), lambda bi: (bi, 0, 0))
    return pl.pallas_call(
        functools.partial(_gla_kernel, nblk=l // GLA_BLOCK, n_seq=n_seq),
        grid=(b // n_seq,),
        in_specs=[seq(GLA_KEY_W), seq(GLA_KEY_W), seq(GLA_VAL_W), seq(GLA_VAL_W),
                  seq(GLA_KEY_W), seq(GLA_KEY_W), st, st,
                  pl.BlockSpec((1, GLA_DV), lambda bi: (0, 0))],
        out_specs=[seq(GLA_VAL_W), st, st],
        out_shape=[jax.ShapeDtypeStruct((b, l, GLA_VAL_W), BF16),
                   jax.ShapeDtypeStruct((b, GLA_KEY_W, GLA_DV), F32),
                   jax.ShapeDtypeStruct((b, GLA_KEY_W, GLA_DV), F32)],
        scratch_shapes=[pltpu.VMEM((n_seq, l, GLA_VAL_W), F32),
                        pltpu.VMEM((n_seq, 2, GLA_KEY_W, GLA_DV), F32)],
        compiler_params=pltpu.CompilerParams(
            dimension_semantics=("parallel",), vmem_limit_bytes=VMEM_LIMIT),
        name="gla",
    )(q, k, v, g, lgf, lgb, s0f, s0b, g_out)


def _attn_kernel(*refs, n_kv, n_seq):
    q_ref, o_ref = refs[0], refs[-1]
    kv = [(refs[1 + 2 * i], refs[2 + 2 * i]) for i in range(n_kv)]
    for si in range(n_seq):
        for h in range(MLA_HEADS):
            sl = slice(h * HEAD_W, (h + 1) * HEAD_W)
            q = q_ref[si, :, sl]
            scores = [_dot_nt(q, k_ref[si, :, sl]) for k_ref, _ in kv]
            m = functools.reduce(jnp.maximum,
                                 [jnp.max(s, axis=-1, keepdims=True) for s in scores])
            o, den = 0.0, 0.0
            for s, (_, v_ref) in zip(scores, kv):
                p = jnp.exp(s - m)
                den = den + jnp.sum(p, axis=-1, keepdims=True)
                o = o + _dot(p.astype(BF16), v_ref[si, :, sl])
            o_ref[si, :, sl] = (o / den).astype(BF16)


def _attn(q, kv_pairs, tq, n_seq):
    b, l, _ = q.shape
    kv_specs, kv_args = [], []
    for k, v in kv_pairs:
        spec = pl.BlockSpec((n_seq, k.shape[1], MLA_W), lambda bi, i: (bi, 0, 0))
        kv_specs += [spec, spec]
        kv_args += [k, v]
    return pl.pallas_call(
        functools.partial(_attn_kernel, n_kv=len(kv_pairs), n_seq=n_seq),
        grid=(b // n_seq, l // tq),
        in_specs=[pl.BlockSpec((n_seq, tq, MLA_W), lambda bi, i: (bi, i, 0))] + kv_specs,
        out_specs=pl.BlockSpec((n_seq, tq, MLA_W), lambda bi, i: (bi, i, 0)),
        out_shape=jax.ShapeDtypeStruct((b, l, MLA_W), BF16),
        compiler_params=pltpu.CompilerParams(
            dimension_semantics=("parallel", "parallel"), vmem_limit_bytes=VMEM_LIMIT),
        name="attn",
    )(q, *kv_args)


def _expert_cap(n_tok):
    return -(-n_tok // FFN_TILE) * FFN_TILE


def _sum01(x):
    return jnp.sum(jnp.sum(x, axis=1, keepdims=True), axis=0, keepdims=True)


def _route(logits_t, bias_col, cnt_col, cap):
    t = logits_t.shape[1]
    gsz = N_EXPERTS // N_GROUPS
    scores = jax.nn.sigmoid(logits_t)
    sel = (scores + bias_col).reshape(N_GROUPS, gsz, t)
    scores = scores.reshape(N_GROUPS, gsz, t)
    neg = jnp.float32(-jnp.inf)
    ie = lax.broadcasted_iota(I32, (N_GROUPS, gsz, t), 1)
    ig = lax.broadcasted_iota(I32, (N_GROUPS, gsz, t), 0)
    m1 = jnp.max(sel, axis=1, keepdims=True)
    first = jnp.min(jnp.where(sel == m1, ie, gsz), axis=1, keepdims=True)
    m2 = jnp.max(jnp.where(ie == first, neg, sel), axis=1, keepdims=True)
    grp = m1 + m2
    igk = lax.broadcasted_iota(I32, (N_GROUPS, 1, t), 0)
    g_sel = jnp.zeros((N_GROUPS, 1, t), jnp.bool_)
    cur = grp
    for _ in range(TOPK_GROUPS):
        m = jnp.max(cur, axis=0, keepdims=True)
        pick = igk == jnp.min(jnp.where(cur == m, igk, N_GROUPS), axis=0, keepdims=True)
        g_sel = g_sel | pick
        cur = jnp.where(pick, neg, cur)
    cur = jnp.where(g_sel, sel, neg)
    idx = ig * gsz + ie
    e_sel = jnp.zeros((N_GROUPS, gsz, t), jnp.bool_)
    picks = []
    for _ in range(TOP_K):
        m = jnp.max(jnp.max(cur, axis=1, keepdims=True), axis=0, keepdims=True)
        cand = jnp.where(cur == m, idx, N_EXPERTS)
        pick = idx == jnp.min(jnp.min(cand, axis=1, keepdims=True), axis=0, keepdims=True)
        picks.append(pick)
        e_sel = e_sel | pick
        cur = jnp.where(pick, neg, cur)
    w = jnp.where(e_sel, scores, 0.0)
    gate = w / _sum01(w) * ROUTED_SCALE
    sel_f = e_sel.astype(F32).reshape(N_EXPERTS, t)
    earlier = (lax.broadcasted_iota(I32, (t, t), 0) < lax.broadcasted_iota(I32, (t, t), 1))
    rank = _dot(sel_f.astype(BF16), earlier.astype(BF16))
    base = lax.broadcasted_iota(I32, (N_EXPERTS, 1), 0).astype(F32) * float(cap) + cnt_col
    slot = (base + rank).reshape(N_GROUPS, gsz, t)
    slot8 = jnp.concatenate([_sum01(jnp.where(pk, slot, 0.0)).reshape(1, t) for pk in picks],
                            axis=0).astype(I32)
    w8 = jnp.concatenate([_sum01(jnp.where(pk, gate, 0.0)).reshape(1, t) for pk in picks], axis=0)
    return slot8, w8, cnt_col + jnp.sum(sel_f, axis=1, keepdims=True)


def _outproj_kernel(og_ref, om_ref, x_ref, mod_ref, wtop_ref, wbot_ref, g2_ref, wr_ref, br_ref,
                    x1_ref, h2pa_ref, h2pb_ref, slot_ref, w8_ref, cnt_ref, cnt_scr, *, cap):
    @pl.when((pl.program_id(0) == 0) & (pl.program_id(1) == 0))
    def _():
        cnt_scr[...] = jnp.zeros_like(cnt_scr)

    mod = mod_ref[0]
    mix = _dot(og_ref[0], wtop_ref[...]) + _dot(om_ref[0], wbot_ref[...])
    x1 = x_ref[0] + mod[2:3] * mix
    x1_ref[0] = x1
    h2 = _rms(x1, D_MODEL) * g2_ref[...] * (1.0 + mod[4:5]) + mod[3:4]
    h2pa_ref[0], h2pb_ref[0] = _pack_rows(h2)
    logits_t = _dot_nt(wr_ref[...], h2, precision=HIGHEST)
    slot8, w8, cnt = _route(logits_t, br_ref[...], cnt_scr[:, 0:1], cap)
    slot_ref[...] = slot8
    t = w8.shape[1]
    wb = lax.bitcast_convert_type(w8.astype(BF16).astype(F32), U32)
    wb = wb | (wb >> 16)
    w8_ref[0] = jnp.concatenate(
        [jnp.broadcast_to(wb[k:k + 1], (SC_LANES, t)) for k in range(TOP_K)], axis=0).T
    cnt_scr[...] = jnp.broadcast_to(cnt, cnt_scr.shape)
    cnt_ref[...] = cnt_scr[...]


def _outproj(og, om, x, mod, per_batch_mod, p, tm):
    b, l, d = x.shape
    nt = l // tm
    mod_map = (lambda bi, i: (bi, 0, 0)) if per_batch_mod else (lambda bi, i: (0, 0, 0))
    tok = lambda w: pl.BlockSpec((1, tm, w), lambda bi, i: (bi, i, 0))
    full = lambda a: pl.BlockSpec(a.shape, lambda bi, i: (0, 0))
    weights = [p['w_out_top'], p['w_out_bot'], p['g_norm2'], p['w_router_t'], p['b_router_col']]
    cnt_shape = (N_EXPERTS, LANES)
    return pl.pallas_call(
        functools.partial(_outproj_kernel, cap=_expert_cap(b * l)),
        grid=(b, nt),
        in_specs=[tok(GLA_VAL_W), tok(MLA_W), tok(d), pl.BlockSpec((1, 6, d), mod_map)]
                 + [full(w) for w in weights],
        out_specs=[tok(d), tok(PACK_W), tok(PACK_W),
                   pl.BlockSpec((TOP_K, tm), lambda bi, i: (0, bi * nt + i)),
                   tok(LANES), pl.BlockSpec(cnt_shape, lambda bi, i: (0, 0))],
        out_shape=[jax.ShapeDtypeStruct((b, l, d), F32), jax.ShapeDtypeStruct((b, l, PACK_W), U32),
                   jax.ShapeDtypeStruct((b, l, PACK_W), U32),
                   jax.ShapeDtypeStruct((TOP_K, b * l), I32),
                   jax.ShapeDtypeStruct((b, l, LANES), U32),
                   jax.ShapeDtypeStruct(cnt_shape, F32)],
        scratch_shapes=[pltpu.VMEM(cnt_shape, F32)],
        compiler_params=pltpu.CompilerParams(
            dimension_semantics=("arbitrary", "arbitrary"), vmem_limit_bytes=VMEM_LIMIT),
        name="outproj",
    )(og, om, x, mod, *weights)


def _sc_mesh():
    return plsc.VectorSubcoreMesh(core_axis_name="c", subcore_axis_name="s")


def _sc_scatter_rows(src, slot8, n_rows_out):
    t, d = src.shape
    nk = slot8.shape[0]

    @functools.partial(pl.kernel, out_type=jax.ShapeDtypeStruct((n_rows_out, d), src.dtype),
                       mesh=_sc_mesh(), scratch_types=[], name="dispatch")
    def run(src_hbm, slot_hbm, out_hbm):
        def body(x_vmem, i_vmem):
            for k in range(nk):
                pltpu.sync_copy(x_vmem, out_hbm.at[i_vmem.at[k]])

        pltpu.emit_pipeline(
            body, grid=(t // SC_WINDOW,),
            in_specs=[pl.BlockSpec((SC_WINDOW, d), lambda i: (i, 0)),
                      pl.BlockSpec((nk, SC_WINDOW), lambda i: (0, i))],
            out_specs=[], core_axis_name=("c", "s"),
            dimension_semantics=(pltpu.PARALLEL,))(src_hbm, slot_hbm)

    return run(src, slot8)


def _sc_collect_sum(table, slots_tk, w_rep):
    n_tok = w_rep.shape[0]
    d = table.shape[1]
    group = SC_WINDOW // TOP_K
    chunks = d // SC_LANES

    @functools.partial(pl.kernel, out_type=jax.ShapeDtypeStruct((n_tok, 2 * d), F32),
                       mesh=_sc_mesh(),
                       scratch_types=[pltpu.VMEM((SC_GATHERS, SC_WINDOW, d), table.dtype),
                                      pltpu.SemaphoreType.DMA((SC_GATHERS,))],
                       compiler_params=pltpu.CompilerParams(needs_layout_passes=False),
                       name="collect")
    def run(tab_hbm, slot_hbm, w_hbm, out_hbm, rows_v, sems):
        def gather(i_vmem, b):
            return pltpu.make_async_copy(tab_hbm.at[i_vmem.at[b]], rows_v.at[b], sems.at[b])

        def body(i_vmem, w_vmem, o_vmem):
            for b in range(SC_GATHERS):
                gather(i_vmem, b).start()
            for b in range(SC_GATHERS):
                gather(i_vmem, b).wait()

                @pl.loop(0, group)
                def _(j):
                    tok = b * group + j
                    wv = [plsc.bitcast(w_vmem[tok, pl.ds(k * SC_LANES, SC_LANES)], BF16)
                          for k in range(TOP_K)]

                    def chunk(off):
                        prods = [plsc.bitcast(
                            rows_v[b, j * TOP_K + k, pl.ds(off, SC_LANES)], BF16) * wv[k]
                            for k in range(TOP_K)]
                        lo = jnp.zeros((SC_LANES,), F32)
                        hi = jnp.zeros((SC_LANES,), F32)
                        for k in range(0, TOP_K, 2):
                            u = plsc.bitcast(prods[k] + prods[k + 1], U32)
                            lo = lo + lax.bitcast_convert_type(u << 16, F32)
                            hi = hi + lax.bitcast_convert_type(u & jnp.uint32(0xFFFF0000), F32)
                        o_vmem[tok, pl.ds(off, SC_LANES)] = lo
                        o_vmem[tok, pl.ds(d + off, SC_LANES)] = hi

                    @plsc.parallel_loop(0, chunks, unroll=2)
                    def _(c):
                        chunk(c * SC_LANES)

        step_tok = SC_GATHERS * group
        pltpu.emit_pipeline(
            body, grid=(n_tok // step_tok,),
            in_specs=[pl.BlockSpec((SC_GATHERS, SC_WINDOW), lambda i: (i, 0)),
                      pl.BlockSpec((step_tok, TOP_K * SC_LANES), lambda i: (i, 0))],
            out_specs=[pl.BlockSpec((step_tok, 2 * d), lambda i: (i, 0))],
            core_axis_name=("c", "s"),
            dimension_semantics=(pltpu.PARALLEL,))(slot_hbm, w_hbm, out_hbm)

    return run(table, slots_tk.reshape(-1, SC_WINDOW), w_rep)


def _ffn_kernel(cnt_ref, nt_ref, first_ref, te_ref, tt_ref, xsa_hbm, xsb_hbm,
                wg_ref, wu_ref, wd_ref, ysa_hbm, ysb_hbm,
                xbuf, ybuf, wg_b, wu_b, wd_b, sem_in, sem_out, *, cap):
    e = pl.program_id(0)
    n_exp = pl.num_programs(0)
    n = nt_ref[e]
    g0 = first_ref[e]
    total = first_ref[n_exp - 1] + nt_ref[n_exp - 1]
    xs_hbm = (xsa_hbm, xsb_hbm)
    ys_hbm = (ysa_hbm, ysb_hbm)
    n_in = FFN_AHEAD + 1
    half = FFN_TILE // 2

    def at_height(ex, t, fn):
        few = cnt_ref[ex] - t * FFN_TILE <= half

        @pl.when(few)
        def _():
            fn(half)

        @pl.when(jnp.logical_not(few))
        def _():
            fn(FFN_TILE)

    def rows(ex, t, n_rows):
        return pl.ds(pl.multiple_of(ex * cap + t * FFN_TILE, FFN_TILE), n_rows)

    def in_copy(ex, t, slot, part, n_rows):
        return pltpu.make_async_copy(xs_hbm[part].at[rows(ex, t, n_rows)],
                                     xbuf.at[slot, part, pl.ds(0, n_rows)],
                                     sem_in.at[slot, part])

    def out_copy(ex, t, slot, part, n_rows):
        return pltpu.make_async_copy(ybuf.at[slot, part, pl.ds(0, n_rows)],
                                     ys_hbm[part].at[rows(ex, t, n_rows)],
                                     sem_out.at[slot, part])

    def start_in(g):
        ex, t = te_ref[g], tt_ref[g]
        at_height(ex, t, lambda nr: [in_copy(ex, t, g % n_in, part, nr).start()
                                     for part in range(PACK_PARTS)])

    def wait_out(g):
        ex, t = te_ref[g], tt_ref[g]
        at_height(ex, t, lambda nr: [out_copy(ex, t, g % 2, part, nr).wait()
                                     for part in range(PACK_PARTS)])

    @pl.when(n > 0)
    def _():
        @pl.when(g0 == 0)
        def _():
            for ahead in range(FFN_AHEAD):
                @pl.when(ahead < total)
                def _():
                    start_in(ahead)

        wg_b[...] = wg_ref[0].astype(BF16)
        wu_b[...] = wu_ref[0].astype(BF16)
        wd_b[...] = wd_ref[0].astype(BF16)

        def tile(t, carry):
            g = g0 + t
            slot = g % 2
            slot_in = g % n_in
            at_height(e, t, lambda nr: [in_copy(e, t, slot_in, part, nr).wait()
                                        for part in range(PACK_PARTS)])

            @pl.when(g + FFN_AHEAD < total)
            def _():
                start_in(g + FFN_AHEAD)

            @pl.when(g >= 2)
            def _():
                wait_out(g - 2)

            def compute(n_rows):
                x = _unpack_rows([xbuf[slot_in, part, 0:n_rows, :]
                                  for part in range(PACK_PARTS)]).astype(BF16)
                a = _silu(_dot(x, wg_b[...])) * _dot(x, wu_b[...])
                y = _pack_rows(_dot(a.astype(BF16), wd_b[...]))
                for part in range(PACK_PARTS):
                    ybuf[slot, part, 0:n_rows, :] = y[part]
                    out_copy(e, t, slot, part, n_rows).start()

            at_height(e, t, compute)
            return carry

        lax.fori_loop(0, n, tile, 0)

        @pl.when(g0 + n == total)
        def _():
            @pl.when(total >= 2)
            def _():
                wait_out(total - 2)

            wait_out(total - 1)


def _ffn(xs_parts, counts, w_gate, w_up, w_down):
    n_exp, d = w_gate.shape[0], w_gate.shape[1]
    cap = xs_parts[0].shape[0] // n_exp
    ntiles = (counts + FFN_TILE - 1) // FFN_TILE
    cum = jnp.cumsum(ntiles)
    first = cum - ntiles
    max_tiles = (cap * TOP_K) // FFN_TILE + n_exp
    g = jnp.arange(max_tiles, dtype=I32)
    done = g[:, None] >= cum[None, :]
    tile_e = jnp.minimum(jnp.sum(done, axis=1), n_exp - 1).astype(I32)
    tile_t = (g - jnp.sum(jnp.where(done, ntiles[None, :], 0), axis=1)).astype(I32)
    wmap = lambda e, cnt, nt, first, te, tt: (e, 0, 0)
    hbm = pl.BlockSpec(memory_space=pl.ANY)
    return pl.pallas_call(
        functools.partial(_ffn_kernel, cap=cap),
        grid_spec=pltpu.PrefetchScalarGridSpec(
            num_scalar_prefetch=5, grid=(n_exp,),
            in_specs=[hbm] * PACK_PARTS
                     + [pl.BlockSpec((1, d, D_EXPERT), wmap), pl.BlockSpec((1, d, D_EXPERT), wmap),
                        pl.BlockSpec((1, D_EXPERT, d), wmap)],
            out_specs=[hbm] * PACK_PARTS,
            scratch_shapes=[pltpu.VMEM((FFN_AHEAD + 1, PACK_PARTS, FFN_TILE, PACK_W), U32),
                            pltpu.VMEM((2, PACK_PARTS, FFN_TILE, PACK_W), U32),
                            pltpu.VMEM((d, D_EXPERT), BF16), pltpu.VMEM((d, D_EXPERT), BF16),
                            pltpu.VMEM((D_EXPERT, d), BF16),
                            pltpu.SemaphoreType.DMA((FFN_AHEAD + 1, PACK_PARTS)),
                            pltpu.SemaphoreType.DMA((2, PACK_PARTS))]),
        out_shape=[jax.ShapeDtypeStruct(xs_parts[0].shape, U32)] * PACK_PARTS,
        compiler_params=pltpu.CompilerParams(
            dimension_semantics=("arbitrary",), vmem_limit_bytes=VMEM_LIMIT),
        name="ffn",
    )(counts.astype(I32), ntiles.astype(I32), first.astype(I32), tile_e, tile_t, *xs_parts,
      w_gate, w_up, w_down)


def _combine_kernel(x1_ref, h2pa_ref, h2pb_ref, ra_ref, rb_ref, mod_ref,
                    wsg_ref, wsu_ref, wsd_ref, o_ref):
    h2 = _unpack_rows([h2pa_ref[0], h2pb_ref[0]]).astype(BF16)
    a = _silu(_dot(h2, wsg_ref[...])) * _dot(h2, wsu_ref[...])
    routed = jnp.concatenate([ra_ref[0], rb_ref[0]], axis=1)
    o_ref[0] = x1_ref[0] + mod_ref[0][5:6] * (_dot(a.astype(BF16), wsd_ref[...]) + routed)


def _combine(x1, h2p_parts, routed_parts, mod, per_batch_mod, p, tm):
    b, l, d = x1.shape
    mod_map = (lambda bi, i: (bi, 0, 0)) if per_batch_mod else (lambda bi, i: (0, 0, 0))
    tok = lambda w: pl.BlockSpec((1, tm, w), lambda bi, i: (bi, i, 0))
    full = lambda a: pl.BlockSpec(a.shape, lambda bi, i: (0, 0))
    weights = [p['w_sh_gate'], p['w_sh_up'], p['w_sh_down']]
    routed_parts = [r.reshape(b, l, d // PACK_PARTS) for r in routed_parts]
    return pl.pallas_call(
        _combine_kernel,
        grid=(b, l // tm),
        in_specs=[tok(d)] + [tok(PACK_W)] * PACK_PARTS + [tok(d // PACK_PARTS)] * PACK_PARTS
                 + [pl.BlockSpec((1, 6, d), mod_map)] + [full(w) for w in weights],
        out_specs=tok(d),
        out_shape=jax.ShapeDtypeStruct((b, l, d), F32),
        compiler_params=pltpu.CompilerParams(
            dimension_semantics=("parallel", "parallel"), vmem_limit_bytes=VMEM_LIMIT),
        name="combine",
    )(x1, *h2p_parts, *routed_parts, mod, *weights)


def _pad_heads(w, parts):
    k = w.shape[0]
    per = w.shape[1] // MLA_HEADS
    w = w.reshape(k, MLA_HEADS, per)[:, :, parts[0]:parts[1]]
    w = jnp.pad(w, ((0, 0), (0, 0), (0, HEAD_W - (parts[1] - parts[0]))))
    return w.reshape(k, MLA_HEADS * HEAD_W)


def _prep_params(l, g_norm1, g_norm2, w_in, w_gk_fwd, b_gk_fwd, w_gk_bwd, b_gk_bwd, g_gla_out,
                 g_q_lora, w_uq, g_kv_lora, w_ukv, g_qk_q, g_qk_k, w_out, w_router, b_router,
                 w_sh_gate, w_sh_up, w_sh_down):
    w = w_in[l]
    d = w.shape[0]
    o_lrf = COL_CQ
    o_lrb = o_lrf + GLA_GATE_RANK
    o_cq = o_lrb + GLA_GATE_RANK
    o_ckv = o_cq + Q_LORA
    o_kr = o_ckv + KV_LORA
    w_in_p = jnp.concatenate([
        w[:, 0:COL_CQ], w[:, o_cq:o_ckv], w[:, o_ckv:o_kr], jnp.zeros((d, KR_LANE0), w.dtype),
        w[:, o_kr:o_kr + ROPE_DIM], w[:, o_lrf:o_lrb], w[:, o_lrb:o_cq]], axis=1).astype(BF16)
    z = jnp.zeros((GLA_GATE_RANK, GLA_KEY_W), F32)
    w_gk_big = jnp.concatenate([
        jnp.zeros((LANES - 2 * GLA_GATE_RANK, 2 * GLA_KEY_W), F32),
        jnp.concatenate([w_gk_fwd[l], z], axis=1),
        jnp.concatenate([z, w_gk_bwd[l]], axis=1)], axis=0).astype(BF16)
    pad_gain = lambda g: jnp.pad(g, (0, HEAD_W - QK_DIM)).reshape(1, HEAD_W)
    w_ukv_h = w_ukv[l]
    return {
        'g_norm1': g_norm1[l].reshape(1, d), 'g_norm2': g_norm2[l].reshape(1, d),
        'w_in_p': w_in_p, 'w_gk_big': w_gk_big,
        'b_gk': jnp.concatenate([b_gk_fwd[l], b_gk_bwd[l]]).reshape(1, 2 * GLA_KEY_W),
        'g_gla_out': g_gla_out[l].reshape(1, GLA_DV),
        'g_q_lora': g_q_lora[l].reshape(1, Q_LORA),
        'w_uq_p': _pad_heads(w_uq[l], (0, QK_DIM)).astype(BF16),
        'g_kv_lora': g_kv_lora[l].reshape(1, KV_LORA),
        'w_ukv_p': jnp.concatenate([_pad_heads(w_ukv_h, (0, NOPE_DIM)),
                                    _pad_heads(w_ukv_h, (NOPE_DIM, NOPE_DIM + V_DIM))],
                                   axis=1).astype(BF16),
        'gq': pad_gain(g_qk_q[l]), 'gk': pad_gain(g_qk_k[l]),
        'w_out_top': w_out[l][:GLA_VAL_W].astype(BF16),
        'w_out_bot': w_out[l][GLA_VAL_W:].astype(BF16),
        'w_router_t': w_router[l].T, 'b_router_col': b_router[l].reshape(N_EXPERTS, 1),
        'w_sh_gate': w_sh_gate[l].astype(BF16), 'w_sh_up': w_sh_up[l].astype(BF16),
        'w_sh_down': w_sh_down[l].astype(BF16),
    }


def _rope_tables(length):
    pos = np.arange(length)
    r = (pos // GRID_W).astype(np.float32)
    c = (pos % GRID_W).astype(np.float32)
    half = ROPE_DIM // 4
    inv_freq = np.float32(ROPE_THETA) ** (-np.arange(half, dtype=np.float32) / np.float32(half))
    ang_r = r[:, None] * inv_freq[None, :]
    ang_c = c[:, None] * inv_freq[None, :]
    zeros = lambda w: np.zeros((length, w), np.float32)
    ones = lambda w: np.ones((length, w), np.float32)
    tail_w = HEAD_W - ROPE_LANE0 - ROPE_DIM
    cos = np.concatenate([ones(ROPE_LANE0), np.cos(ang_r), np.cos(ang_r), np.cos(ang_c),
                          np.cos(ang_c), ones(tail_w)], axis=1)
    s1 = np.concatenate([zeros(ROPE_LANE0), -np.sin(ang_r), zeros(half), -np.sin(ang_c),
                         zeros(half), zeros(tail_w)], axis=1)
    s2 = np.concatenate([zeros(ROPE_LANE0), zeros(half), np.sin(ang_r), zeros(half),
                         np.sin(ang_c), zeros(tail_w)], axis=1)
    return tuple(jnp.asarray(t, F32) for t in (cos, s1, s2))


def _mix_route_dispatch(x_tok, seq_shape, mod, per_batch_mod, p, s0f, s0b, rope_tabs, ctx_kv,
                        tq):
    bt, lt, d = x_tok.shape
    b, l = seq_shape
    n_tok = bt * lt
    (q, k, v, g, lgf, lgb, qm, km, vm, ckvn, kr) = _inproj(x_tok, mod, per_batch_mod, p,
                                                            rope_tabs, IN_TILE)
    seq = lambda a: a.reshape(b, l, a.shape[-1])
    og, sf, sb = _gla(seq(q), seq(k), seq(v), seq(g), seq(lgf), seq(lgb), s0f, s0b,
                      p['g_gla_out'], GLA_SEQS if l == GLA_BLOCK else 1)
    kv_pairs = [(seq(km), seq(vm))] + ([ctx_kv] if ctx_kv is not None else [])
    om = _attn(seq(qm), kv_pairs, tq, ATTN_SEQS if l == tq else 1)
    tokv = lambda a: a.reshape(bt, lt, a.shape[-1])
    x1, h2a, h2b, slot8, w8, cnt = _outproj(tokv(og), tokv(om), x_tok, mod, per_batch_mod, p,
                                            OUT_TILE)
    x_sorted = [_sc_scatter_rows(h.reshape(n_tok, PACK_W), slot8,
                                 N_EXPERTS * _expert_cap(n_tok)) for h in (h2a, h2b)]
    routed = dict(x1=x1, h2=(h2a, h2b), slot8=slot8, w8=w8, cnt=cnt, x_sorted=x_sorted)
    return routed, sf, sb, ckvn, kr


def _experts_collect(r, w_gate, w_up, w_down):
    n_tok = r['slot8'].shape[1]
    y_sorted = _ffn(r['x_sorted'], r['cnt'][:, 0].astype(I32), w_gate, w_up, w_down)
    slots_tk = r['slot8'].T.reshape(1, TOP_K * n_tok)
    w_rep = r['w8'].reshape(n_tok, LANES)
    return [_sc_collect_sum(y, slots_tk, w_rep) for y in y_sorted]


def kernel(x_prompt, x_sample, c, state_gla_fwd, state_gla_bwd, cache_mla_ckv, cache_mla_krope,
           c_ctx, w_ada, b_ada, g_norm1, g_norm2, w_in, w_gk_fwd, b_gk_fwd, w_gk_bwd, b_gk_bwd,
           g_gla_out, g_q_lora, w_uq, g_kv_lora, w_ukv, g_qk_q, g_qk_k, w_out,
           w_router, b_router, w_exp_gate, w_exp_up, w_exp_down, w_sh_gate, w_sh_up, w_sh_down):
    bp, lp, d = x_prompt.shape
    bs, ls, _ = x_sample.shape
    depth = w_ada.shape[0]
    xp = x_prompt.reshape(1, bp * lp, d)
    xs = x_sample
    new_f, new_b, new_ckv, new_kr = [], [], [], []
    lat_tabs = _rope_tables(ls)
    cvecs = jnp.concatenate([c_ctx[None], c, jnp.zeros((8 - 1 - bs, d), F32)], axis=0)
    for l in range(depth):
        p = _prep_params(l, g_norm1, g_norm2, w_in, w_gk_fwd, b_gk_fwd, w_gk_bwd, b_gk_bwd,
                         g_gla_out, g_q_lora, w_uq, g_kv_lora, w_ukv, g_qk_q, g_qk_k, w_out,
                         w_router, b_router, w_sh_gate, w_sh_up, w_sh_down)
        mod = _ada(cvecs, w_ada[l], b_ada[l]).reshape(8, 6, d)
        mod_p, mod_s = mod[0:1], mod[1:1 + bs]
        zeros = jnp.zeros((bp, GLA_KEY_W, GLA_DV), F32)
        r_ctx, sf, sb, ckvn, kr = _mix_route_dispatch(
            xp, (bp, lp), mod_p, False, p, zeros, zeros, (), None, lp)
        new_f.append(sf.reshape(bp, GLA_HEADS, GLA_DK, GLA_DV))
        new_b.append(sb.reshape(bp, GLA_HEADS, GLA_DK, GLA_DV))
        new_ckv.append(ckvn.reshape(bp, lp, KV_LORA))
        new_kr.append(kr.reshape(bp, lp, LANES)[:, :, KR_LANE0:KR_LANE0 + ROPE_DIM])
        kr_cache = jnp.pad(cache_mla_krope[:, l],
                           ((0, 0), (0, 0), (ROPE_LANE0, LANES - ROPE_LANE0 - ROPE_DIM)))
        ctx_kv = _cache_kv(cache_mla_ckv[:, l], kr_cache, p)
        s0f = state_gla_fwd[:, l].reshape(bs, GLA_KEY_W, GLA_DV)
        s0b = state_gla_bwd[:, l].reshape(bs, GLA_KEY_W, GLA_DV)
        r_lat, _, _, _, _ = _mix_route_dispatch(
            xs, (bs, ls), mod_s, True, p, s0f, s0b, lat_tabs, ctx_kv, 256)
        experts = (w_exp_gate[l], w_exp_up[l], w_exp_down[l])
        routed_ctx = _experts_collect(r_ctx, *experts)
        routed_lat = _experts_collect(r_lat, *experts)
        xp = _combine(r_ctx['x1'], r_ctx['h2'], routed_ctx, mod_p, False, p, OUT_TILE)
        xs = _combine(r_lat['x1'], r_lat['h2'], routed_lat, mod_s, True, p, OUT_TILE)
    return (xp.reshape(bp, lp, d), xs, jnp.stack(new_f, axis=1), jnp.stack(new_b, axis=1),
            jnp.stack(new_ckv, axis=1), jnp.stack(new_kr, axis=1))
```

```python
import functools

import jax
import jax.numpy as jnp
import numpy as np
from jax import lax
from jax.experimental import pallas as pl
from jax.experimental.pallas import tpu as pltpu
from jax.experimental.pallas import tpu_sc as plsc

F32 = jnp.float32
BF16 = jnp.bfloat16
I32 = jnp.int32
U32 = jnp.uint32

D_MODEL = 1024
EPS = 1e-6
GRID_W = 64
GLA_HEADS = 4
GLA_DK = 64
GLA_DV = 128
GLA_GATE_RANK = 16
GLA_GATE_NORM = 16.0
GLA_KEY_W = GLA_HEADS * GLA_DK
GLA_VAL_W = GLA_HEADS * GLA_DV
MLA_HEADS = 4
Q_LORA = 256
KV_LORA = 128
NOPE_DIM = 64
ROPE_DIM = 32
V_DIM = 128
QK_DIM = NOPE_DIM + ROPE_DIM
ROPE_THETA = 10000.0
N_EXPERTS = 64
TOP_K = 8
N_GROUPS = 8
TOPK_GROUPS = 4
D_EXPERT = 256
ROUTED_SCALE = 2.5

LANES = 128
HEAD_W = LANES
MLA_W = MLA_HEADS * HEAD_W
ROPE_LANE0 = NOPE_DIM
COL_V = 2 * GLA_KEY_W
COL_G = COL_V + GLA_VAL_W
COL_CQ = COL_G + GLA_VAL_W
COL_CKV = COL_CQ + Q_LORA
TAIL0 = COL_CKV + KV_LORA
IN_W = TAIL0 + LANES
KR_LANE0 = LANES - ROPE_DIM - 2 * GLA_GATE_RANK
GLA_BLOCK = 256
GLA_SUB = 64
GLA_SEQS = 4
ATTN_SEQS = 4
ATTN_TILE = 512
ADA_TILE = 3072
IN_TILE = 1024
IN_SUB = 256
OUT_TILE = 1024
FFN_TILE = 512
FFN_AHEAD = 5
PACK_PARTS = 2
PACK_W = D_MODEL // (2 * PACK_PARTS)
SC_WINDOW = 128
SC_LANES = 16
SC_GATHERS = 2
VMEM_LIMIT = 56 * 1024 * 1024

HIGHEST = lax.Precision.HIGHEST


def _dot(a, b, precision=None):
    return jnp.dot(a, b, preferred_element_type=F32, precision=precision)


def _dot_nt(a, b, precision=None):
    return lax.dot_general(a, b, (((1,), (1,)), ((), ())), preferred_element_type=F32,
                           precision=precision)


def _rms(x, width):
    ss = jnp.sum(x * x, axis=-1, keepdims=True) * (1.0 / width)
    return x * lax.rsqrt(ss + EPS)


def _silu(x):
    return x * jax.nn.sigmoid(x)


def _log_sigmoid(x):
    return jnp.minimum(x, 0.0) - jnp.log1p(jnp.exp(-jnp.abs(x)))


def _pack_rows(x):
    parts = []
    for i in range(PACK_PARTS):
        c0 = i * 2 * PACK_W
        lo = lax.bitcast_convert_type(x[:, c0:c0 + PACK_W].astype(BF16).astype(F32), U32)
        hi = lax.bitcast_convert_type(
            x[:, c0 + PACK_W:c0 + 2 * PACK_W].astype(BF16).astype(F32), U32)
        parts.append(hi | (lo >> 16))
    return parts


def _unpack_rows(parts):
    cols = []
    for w in parts:
        cols.append(lax.bitcast_convert_type(w << 16, F32))
        cols.append(lax.bitcast_convert_type(w & jnp.uint32(0xFFFF0000), F32))
    return jnp.concatenate(cols, axis=1)


def _ada_kernel(c_ref, w_ref, b_ref, o_ref):
    o_ref[...] = _dot(_silu(c_ref[...]), w_ref[...], precision=HIGHEST) + b_ref[...]


def _ada(cvecs, w_ada, b_ada):
    n = w_ada.shape[1]
    tn = ADA_TILE
    return pl.pallas_call(
        _ada_kernel,
        grid=(n // tn,),
        in_specs=[pl.BlockSpec((8, D_MODEL), lambda j: (0, 0)),
                  pl.BlockSpec((D_MODEL, tn), lambda j: (0, j)),
                  pl.BlockSpec((1, tn), lambda j: (0, j))],
        out_specs=pl.BlockSpec((8, tn), lambda j: (0, j)),
        out_shape=jax.ShapeDtypeStruct((8, n), F32),
        name="ada",
    )(cvecs, w_ada, b_ada.reshape(1, n))


def _rope(x, c, s1, s2):
    return x * c + pltpu.roll(x, LANES - 8, 1) * s1 + pltpu.roll(x, 8, 1) * s2


def _mla_kv(ckv_n, kr, w_ukv_ref, gk, rope_tabs, k_ref, v_ref, rs=slice(None)):
    kv = _dot(ckv_n.astype(BF16), w_ukv_ref[...])
    for h in range(MLA_HEADS):
        k_h = kv[:, h * HEAD_W:(h + 1) * HEAD_W] + kr
        k_h = _rms(k_h, QK_DIM) * gk
        if rope_tabs is not None:
            k_h = _rope(k_h, *rope_tabs)
        k_ref[0, rs, h * HEAD_W:(h + 1) * HEAD_W] = k_h.astype(BF16)
    v_ref[0, rs, :] = kv[:, MLA_W:].astype(BF16)


def _lane_mask(lo, hi, rows):
    lane = lax.broadcasted_iota(I32, (rows, LANES), 1)
    return (lane >= lo) & (lane < hi)


def _inproj_kernel(x_ref, mod_ref, g1_ref, win_ref, wgk_ref, bgk_ref, gql_ref, wuq_ref,
                   gkv_ref, wukv_ref, gq_ref, gk_ref, *refs, positions):
    tab_refs, outs = (refs[:3], refs[3:]) if positions else ((), refs)
    (q_ref, k_ref, v_ref, g_ref, lgf_ref, lgb_ref, qm_ref, km_ref, vm_ref, ckvn_ref,
     kr_ref) = outs
    mod = mod_ref[0]
    for r0 in range(0, x_ref.shape[1], IN_SUB):
        rs = slice(r0, r0 + IN_SUB)
        x = x_ref[0, rs, :]
        h = _rms(x, D_MODEL) * g1_ref[...] * (1.0 + mod[1:2]) + mod[0:1]
        y = _dot(h.astype(BF16), win_ref[...])
        q_ref[0, rs, :] = (y[:, 0:GLA_KEY_W] * GLA_DK ** -0.5).astype(BF16)
        k_ref[0, rs, :] = y[:, GLA_KEY_W:COL_V].astype(BF16)
        v_ref[0, rs, :] = y[:, COL_V:COL_G].astype(BF16)
        g_ref[0, rs, :] = y[:, COL_G:COL_CQ].astype(BF16)
        tail = y[:, TAIL0:IN_W]
        pre = _dot(tail.astype(BF16), wgk_ref[...]) + bgk_ref[...]
        logg = _log_sigmoid(pre) * (1.0 / GLA_GATE_NORM)
        lgf_ref[0, rs, :] = logg[:, 0:GLA_KEY_W]
        lgb_ref[0, rs, :] = logg[:, GLA_KEY_W:]
        tabs = tuple(r[rs, :] for r in tab_refs) if positions else None
        cq = _rms(y[:, COL_CQ:COL_CKV], Q_LORA) * gql_ref[...]
        qm = _dot(cq.astype(BF16), wuq_ref[...])
        gq = gq_ref[...]
        for hh in range(MLA_HEADS):
            q_h = _rms(qm[:, hh * HEAD_W:(hh + 1) * HEAD_W], QK_DIM) * gq
            if positions:
                q_h = _rope(q_h, *tabs)
            qm_ref[0, rs, hh * HEAD_W:(hh + 1) * HEAD_W] = (q_h * QK_DIM ** -0.5).astype(BF16)
        ckv_n = _rms(y[:, COL_CKV:TAIL0], KV_LORA) * gkv_ref[...]
        ckvn_ref[0, rs, :] = ckv_n
        kr_ref[0, rs, :] = tail
        kr = jnp.where(_lane_mask(KR_LANE0, KR_LANE0 + ROPE_DIM, IN_SUB), tail, 0.0)
        _mla_kv(ckv_n, kr, wukv_ref, gk_ref[...], tabs, km_ref, vm_ref, rs)


def _inproj(x, mod, per_batch_mod, p, rope_tabs, tm):
    b, l, d = x.shape
    nt = l // tm
    mod_map = (lambda bi, i: (bi, 0, 0)) if per_batch_mod else (lambda bi, i: (0, 0, 0))
    tab_map = lambda bi, i: (i, 0)
    const = lambda bi, i: (0, 0)
    tok = lambda w: pl.BlockSpec((1, tm, w), lambda bi, i: (bi, i, 0))
    full = lambda a: pl.BlockSpec(a.shape, const)
    weights = [p['g_norm1'], p['w_in_p'], p['w_gk_big'], p['b_gk'], p['g_q_lora'], p['w_uq_p'],
               p['g_kv_lora'], p['w_ukv_p'], p['gq'], p['gk']]
    outs = [(GLA_KEY_W, BF16), (GLA_KEY_W, BF16), (GLA_VAL_W, BF16), (GLA_VAL_W, BF16),
            (GLA_KEY_W, F32), (GLA_KEY_W, F32), (MLA_W, BF16), (MLA_W, BF16), (MLA_W, BF16),
            (KV_LORA, F32), (LANES, F32)]
    return pl.pallas_call(
        functools.partial(_inproj_kernel, positions=bool(rope_tabs)),
        grid=(b, nt),
        in_specs=[tok(d), pl.BlockSpec((1, 6, d), mod_map)] + [full(w) for w in weights]
                 + [pl.BlockSpec((tm, LANES), tab_map)] * len(rope_tabs),
        out_specs=[tok(w) for w, _ in outs],
        out_shape=[jax.ShapeDtypeStruct((b, l, w), dt) for w, dt in outs],
        compiler_params=pltpu.CompilerParams(
            dimension_semantics=("parallel", "parallel"), vmem_limit_bytes=VMEM_LIMIT),
        name="inproj",
    )(x, mod, *weights, *rope_tabs)


def _cache_kv_kernel(ckv_ref, kr_ref, wukv_ref, gk_ref, k_ref, v_ref):
    _mla_kv(ckv_ref[0], kr_ref[0], wukv_ref, gk_ref[...], None, k_ref, v_ref)


def _cache_kv(ckv, kr128, p):
    b, l, _ = ckv.shape
    tok = lambda w: pl.BlockSpec((1, l, w), lambda bi: (bi, 0, 0))
    full = lambda a: pl.BlockSpec(a.shape, lambda bi: (0, 0))
    return pl.pallas_call(
        _cache_kv_kernel,
        grid=(b,),
        in_specs=[tok(KV_LORA), tok(LANES), full(p['w_ukv_p']), full(p['gk'])],
        out_specs=[tok(MLA_W), tok(MLA_W)],
        out_shape=[jax.ShapeDtypeStruct((b, l, MLA_W), BF16)] * 2,
        name="cache_kv",
    )(ckv, kr128, p['w_ukv_p'], p['gk'])


def _split3(x):
    hi = x.astype(BF16)
    r1 = x - hi.astype(F32)
    mid = r1.astype(BF16)
    lo = (r1 - mid.astype(F32)).astype(BF16)
    return hi, mid, lo


def _gla_block(q, k, v, lg, s, fwd):
    n = GLA_BLOCK
    nc = n // GLA_SUB
    row = lax.broadcasted_iota(I32, (n, n), 0)
    col = lax.broadcasted_iota(I32, (n, n), 1)
    same = (row // GLA_SUB) == (col // GLA_SUB)
    causal = same & ((col <= row) if fwd else (col >= row))
    tri = causal.astype(BF16)
    hi, mid, lo = _split3(lg)
    cum = _dot(tri, hi) + _dot(tri, mid) + _dot(tri, lo)
    tot_rows, mid_rows = [], []
    for c in range(nc):
        r_tot = c * GLA_SUB + (GLA_SUB - 1 if fwd else 0)
        r_mid = c * GLA_SUB + GLA_SUB // 2
        tot_rows.append(jnp.broadcast_to(cum[r_tot:r_tot + 1], (GLA_SUB, GLA_KEY_W)))
        mid_rows.append(jnp.broadcast_to(cum[r_mid:r_mid + 1], (GLA_SUB, GLA_KEY_W)))
    tot_rows = jnp.concatenate(tot_rows, axis=0)
    mid_rows = jnp.concatenate(mid_rows, axis=0)
    rel = cum - mid_rows
    qi = q * jnp.exp(rel)
    ki = (k * jnp.exp(-rel)).astype(BF16)
    q_in = (q * jnp.exp(cum)).astype(BF16)
    k_up = k * jnp.exp(tot_rows - cum)
    dec = jnp.exp(tot_rows)
    k_up_t = k_up.T.astype(BF16)
    dec_t = dec.T
    lane_head = lax.broadcasted_iota(I32, (n, GLA_KEY_W), 1) // GLA_DK
    srow_head = lax.broadcasted_iota(I32, (GLA_KEY_W, GLA_DV), 0) // GLA_DK
    vrow_chunk = lax.broadcasted_iota(I32, (n, GLA_VAL_W), 0) // GLA_SUB
    o_heads = []
    for h in range(GLA_HEADS):
        a = _dot_nt(jnp.where(lane_head == h, qi, 0.0).astype(BF16), ki)
        a = jnp.where(causal, a, 0.0).astype(BF16)
        o_heads.append(_dot(a, v[:, h * GLA_DV:(h + 1) * GLA_DV]))
    o = jnp.concatenate(o_heads, axis=1)
    o_inter = [None] * nc
    for c in (range(nc) if fwd else range(nc - 1, -1, -1)):
        s_bd = jnp.concatenate(
            [jnp.where(srow_head == h, s, 0.0).astype(BF16) for h in range(GLA_HEADS)], axis=1)
        o_inter[c] = _dot(q_in[c * GLA_SUB:(c + 1) * GLA_SUB], s_bd)
        v_c = jnp.where(vrow_chunk == c, v, jnp.zeros_like(v))
        u = jnp.concatenate(
            [_dot(k_up_t[h * GLA_DK:(h + 1) * GLA_DK], v_c[:, h * GLA_DV:(h + 1) * GLA_DV])
             for h in range(GLA_HEADS)], axis=0)
        s = dec_t[:, c * GLA_SUB:c * GLA_SUB + 1] * s + u
    return o + jnp.concatenate(o_inter, axis=0), s


def _gla_kernel(q_ref, k_ref, v_ref, g_ref, lgf_ref, lgb_ref, s0f_ref, s0b_ref, gout_ref,
                o_ref, sf_ref, sb_ref, acc_ref, st_ref, *, nblk, n_seq):
    for si in range(n_seq):
        _gla_sequence(si, q_ref, k_ref, v_ref, g_ref, lgf_ref, lgb_ref, s0f_ref, s0b_ref,
                      gout_ref, o_ref, sf_ref, sb_ref, acc_ref, st_ref, nblk)


def _gla_sequence(si, q_ref, k_ref, v_ref, g_ref, lgf_ref, lgb_ref, s0f_ref, s0b_ref, gout_ref,
                  o_ref, sf_ref, sb_ref, acc_ref, st_ref, nblk):
    st_ref[si, 0] = s0f_ref[si]
    st_ref[si, 1] = s0b_ref[si]

    def load(blk):
        start = blk * GLA_BLOCK
        r = pl.ds(start if isinstance(blk, int) else pl.multiple_of(start, GLA_BLOCK), GLA_BLOCK)
        return r, q_ref[si, r, :].astype(F32), k_ref[si, r, :].astype(F32), v_ref[si, r, :]

    def pair_step(i, first_touch):
        rf, qf, kf, vf = load(i)
        rb, qb, kb, vb = load(nblk - 1 - i)
        of, s_f = _gla_block(qf, kf, vf, lgf_ref[si, rf, :], st_ref[si, 0], True)
        ob, s_b = _gla_block(qb, kb, vb, lgb_ref[si, rb, :], st_ref[si, 1], False)
        if first_touch:
            acc_ref[si, rf, :] = of
            acc_ref[si, rb, :] = ob
        else:
            acc_ref[si, rf, :] += of
            acc_ref[si, rb, :] += ob
        st_ref[si, 0] = s_f
        st_ref[si, 1] = s_b

    if nblk == 1:
        r, q, k, v = load(0)
        of, s_f = _gla_block(q, k, v, lgf_ref[si, r, :], st_ref[si, 0], True)
        ob, s_b = _gla_block(q, k, v, lgb_ref[si, r, :], st_ref[si, 1], False)
        acc_ref[si, r, :] = of + ob
        st_ref[si, 0] = s_f
        st_ref[si, 1] = s_b
    else:
        half = nblk // 2
        lax.fori_loop(0, half, lambda i, c: (pair_step(i, True), c)[1], 0)
        lax.fori_loop(half, nblk, lambda i, c: (pair_step(i, False), c)[1], 0)
    sf_ref[si] = st_ref[si, 0]
    sb_ref[si] = st_ref[si, 1]

    def fin_step(blk, carry):
        r = pl.ds(pl.multiple_of(blk * GLA_BLOCK, GLA_BLOCK), GLA_BLOCK)
        o = acc_ref[si, r, :]
        gate = _silu(g_ref[si, r, :].astype(F32))
        for h in range(GLA_HEADS):
            sl = slice(h * GLA_DV, (h + 1) * GLA_DV)
            o_ref[si, r, sl] = (_rms(o[:, sl], GLA_DV) * gout_ref[...] * gate[:, sl]).astype(BF16)
        return carry

    lax.fori_loop(0, nblk, fin_step, 0)


def _gla(q, k, v, g, lgf, lgb, s0f, s0b, g_out, n_seq):
    b, l, _ = q.shape
    assert l == GLA_BLOCK or l % (2 * GLA_BLOCK) == 0
    seq = lambda w: pl.BlockSpec((n_seq, l, w), lambda bi: (bi, 0, 0))
    st = pl.BlockSpec((n_seq, GLA_KEY_W, GLA_DV), lambda bi: (bi, 0, 0))
    return pl.pallas_call(
        functools.partial(_gla_kernel, nblk=l // GLA_BLOCK, n_seq=n_seq),
        grid=(b // n_seq,),
        in_specs=[seq(GLA_KEY_W), seq(GLA_KEY_W), seq(GLA_VAL_W), seq(GLA_VAL_W),
                  seq(GLA_KEY_W), seq(GLA_KEY_W), st, st,
                  pl.BlockSpec((1, GLA_DV), lambda bi: (0, 0))],
        out_specs=[seq(GLA_VAL_W), st, st],
        out_shape=[jax.ShapeDtypeStruct((b, l, GLA_VAL_W), BF16),
                   jax.ShapeDtypeStruct((b, GLA_KEY_W, GLA_DV), F32),
                   jax.ShapeDtypeStruct((b, GLA_KEY_W, GLA_DV), F32)],
        scratch_shapes=[pltpu.VMEM((n_seq, l, GLA_VAL_W), F32),
                        pltpu.VMEM((n_seq, 2, GLA_KEY_W, GLA_DV), F32)],
        compiler_params=pltpu.CompilerParams(
            dimension_semantics=("parallel",), vmem_limit_bytes=VMEM_LIMIT),
        name="gla",
    )(q, k, v, g, lgf, lgb, s0f, s0b, g_out)


def _attn_kernel(*refs, n_kv, n_seq):
    q_ref, o_ref = refs[0], refs[-1]
    kv = [(refs[1 + 2 * i], refs[2 + 2 * i]) for i in range(n_kv)]
    for si in range(n_seq):
        for h in range(MLA_HEADS):
            sl = slice(h * HEAD_W, (h + 1) * HEAD_W)
            q = q_ref[si, :, sl]
            scores = [_dot_nt(q, k_ref[si, :, sl]) for k_ref, _ in kv]
            m = functools.reduce(jnp.maximum,
                                 [jnp.max(s, axis=-1, keepdims=True) for s in scores])
            o, den = 0.0, 0.0
            for s, (_, v_ref) in zip(scores, kv):
                p = jnp.exp(s - m)
                den = den + jnp.sum(p, axis=-1, keepdims=True)
                o = o + _dot(p.astype(BF16), v_ref[si, :, sl])
            o_ref[si, :, sl] = (o / den).astype(BF16)


def _attn(q, kv_pairs, tq, n_seq):
    b, l, _ = q.shape
    kv_specs, kv_args = [], []
    for k, v in kv_pairs:
        spec = pl.BlockSpec((n_seq, k.shape[1], MLA_W), lambda bi, i: (bi, 0, 0))
        kv_specs += [spec, spec]
        kv_args += [k, v]
    return pl.pallas_call(
        functools.partial(_attn_kernel, n_kv=len(kv_pairs), n_seq=n_seq),
        grid=(b // n_seq, l // tq),
        in_specs=[pl.BlockSpec((n_seq, tq, MLA_W), lambda bi, i: (bi, i, 0))] + kv_specs,
        out_specs=pl.BlockSpec((n_seq, tq, MLA_W), lambda bi, i: (bi, i, 0)),
        out_shape=jax.ShapeDtypeStruct((b, l, MLA_W), BF16),
        compiler_params=pltpu.CompilerParams(
            dimension_semantics=("parallel", "parallel"), vmem_limit_bytes=VMEM_LIMIT),
        name="attn",
    )(q, *kv_args)


def _expert_cap(n_tok):
    return -(-n_tok // FFN_TILE) * FFN_TILE


def _sum01(x):
    return jnp.sum(jnp.sum(x, axis=1, keepdims=True), axis=0, keepdims=True)


def _route(logits_t, bias_col, cnt_col, cap):
    t = logits_t.shape[1]
    gsz = N_EXPERTS // N_GROUPS
    scores = jax.nn.sigmoid(logits_t)
    sel = (scores + bias_col).reshape(N_GROUPS, gsz, t)
    scores = scores.reshape(N_GROUPS, gsz, t)
    neg = jnp.float32(-jnp.inf)
    ie = lax.broadcasted_iota(I32, (N_GROUPS, gsz, t), 1)
    ig = lax.broadcasted_iota(I32, (N_GROUPS, gsz, t), 0)
    m1 = jnp.max(sel, axis=1, keepdims=True)
    first = jnp.min(jnp.where(sel == m1, ie, gsz), axis=1, keepdims=True)
    m2 = jnp.max(jnp.where(ie == first, neg, sel), axis=1, keepdims=True)
    grp = m1 + m2
    igk = lax.broadcasted_iota(I32, (N_GROUPS, 1, t), 0)
    g_sel = jnp.zeros((N_GROUPS, 1, t), jnp.bool_)
    cur = grp
    for _ in range(TOPK_GROUPS):
        m = jnp.max(cur, axis=0, keepdims=True)
        pick = igk == jnp.min(jnp.where(cur == m, igk, N_GROUPS), axis=0, keepdims=True)
        g_sel = g_sel | pick
        cur = jnp.where(pick, neg, cur)
    cur = jnp.where(g_sel, sel, neg)
    idx = ig * gsz + ie
    e_sel = jnp.zeros((N_GROUPS, gsz, t), jnp.bool_)
    picks = []
    for _ in range(TOP_K):
        m = jnp.max(jnp.max(cur, axis=1, keepdims=True), axis=0, keepdims=True)
        cand = jnp.where(cur == m, idx, N_EXPERTS)
        pick = idx == jnp.min(jnp.min(cand, axis=1, keepdims=True), axis=0, keepdims=True)
        picks.append(pick)
        e_sel = e_sel | pick
        cur = jnp.where(pick, neg, cur)
    w = jnp.where(e_sel, scores, 0.0)
    gate = w / _sum01(w) * ROUTED_SCALE
    sel_f = e_sel.astype(F32).reshape(N_EXPERTS, t)
    earlier = (lax.broadcasted_iota(I32, (t, t), 0) < lax.broadcasted_iota(I32, (t, t), 1))
    rank = _dot(sel_f.astype(BF16), earlier.astype(BF16))
    base = lax.broadcasted_iota(I32, (N_EXPERTS, 1), 0).astype(F32) * float(cap) + cnt_col
    slot = (base + rank).reshape(N_GROUPS, gsz, t)
    slot8 = jnp.concatenate([_sum01(jnp.where(pk, slot, 0.0)).reshape(1, t) for pk in picks],
                            axis=0).astype(I32)
    w8 = jnp.concatenate([_sum01(jnp.where(pk, gate, 0.0)).reshape(1, t) for pk in picks], axis=0)
    return slot8, w8, cnt_col + jnp.sum(sel_f, axis=1, keepdims=True)


def _outproj_kernel(og_ref, om_ref, x_ref, mod_ref, wtop_ref, wbot_ref, g2_ref, wr_ref, br_ref,
                    x1_ref, h2pa_ref, h2pb_ref, slot_ref, w8_ref, cnt_ref, cnt_scr, *, cap):
    @pl.when((pl.program_id(0) == 0) & (pl.program_id(1) == 0))
    def _():
        cnt_scr[...] = jnp.zeros_like(cnt_scr)

    mod = mod_ref[0]
    mix = _dot(og_ref[0], wtop_ref[...]) + _dot(om_ref[0], wbot_ref[...])
    x1 = x_ref[0] + mod[2:3] * mix
    x1_ref[0] = x1
    h2 = _rms(x1, D_MODEL) * g2_ref[...] * (1.0 + mod[4:5]) + mod[3:4]
    h2pa_ref[0], h2pb_ref[0] = _pack_rows(h2)
    logits_t = _dot_nt(wr_ref[...], h2, precision=HIGHEST)
    slot8, w8, cnt = _route(logits_t, br_ref[...], cnt_scr[:, 0:1], cap)
    slot_ref[...] = slot8
    t = w8.shape[1]
    wb = lax.bitcast_convert_type(w8.astype(BF16).astype(F32), U32)
    wb = wb | (wb >> 16)
    w8_ref[0] = jnp.concatenate(
        [jnp.broadcast_to(wb[k:k + 1], (SC_LANES, t)) for k in range(TOP_K)], axis=0).T
    cnt_scr[...] = jnp.broadcast_to(cnt, cnt_scr.shape)
    cnt_ref[...] = cnt_scr[...]


def _outproj(og, om, x, mod, per_batch_mod, p, tm):
    b, l, d = x.shape
    nt = l // tm
    mod_map = (lambda bi, i: (bi, 0, 0)) if per_batch_mod else (lambda bi, i: (0, 0, 0))
    tok = lambda w: pl.BlockSpec((1, tm, w), lambda bi, i: (bi, i, 0))
    full = lambda a: pl.BlockSpec(a.shape, lambda bi, i: (0, 0))
    weights = [p['w_out_top'], p['w_out_bot'], p['g_norm2'], p['w_router_t'], p['b_router_col']]
    cnt_shape = (N_EXPERTS, LANES)
    return pl.pallas_call(
        functools.partial(_outproj_kernel, cap=_expert_cap(b * l)),
        grid=(b, nt),
        in_specs=[tok(GLA_VAL_W), tok(MLA_W), tok(d), pl.BlockSpec((1, 6, d), mod_map)]
                 + [full(w) for w in weights],
        out_specs=[tok(d), tok(PACK_W), tok(PACK_W),
                   pl.BlockSpec((TOP_K, tm), lambda bi, i: (0, bi * nt + i)),
                   tok(LANES), pl.BlockSpec(cnt_shape, lambda bi, i: (0, 0))],
        out_shape=[jax.ShapeDtypeStruct((b, l, d), F32), jax.ShapeDtypeStruct((b, l, PACK_W), U32),
                   jax.ShapeDtypeStruct((b, l, PACK_W), U32),
                   jax.ShapeDtypeStruct((TOP_K, b * l), I32),
                   jax.ShapeDtypeStruct((b, l, LANES), U32),
                   jax.ShapeDtypeStruct(cnt_shape, F32)],
        scratch_shapes=[pltpu.VMEM(cnt_shape, F32)],
        compiler_params=pltpu.CompilerParams(
            dimension_semantics=("arbitrary", "arbitrary"), vmem_limit_bytes=VMEM_LIMIT),
        name="outproj",
    )(og, om, x, mod, *weights)


def _sc_mesh():
    return plsc.VectorSubcoreMesh(core_axis_name="c", subcore_axis_name="s")


def _sc_scatter_rows(src, slot8, n_rows_out):
    t, d = src.shape
    nk = slot8.shape[0]

    @functools.partial(pl.kernel, out_type=jax.ShapeDtypeStruct((n_rows_out, d), src.dtype),
                       mesh=_sc_mesh(), scratch_types=[], name="dispatch")
    def run(src_hbm, slot_hbm, out_hbm):
        def body(x_vmem, i_vmem):
            for k in range(nk):
                pltpu.sync_copy(x_vmem, out_hbm.at[i_vmem.at[k]])

        pltpu.emit_pipeline(
            body, grid=(t // SC_WINDOW,),
            in_specs=[pl.BlockSpec((SC_WINDOW, d), lambda i: (i, 0)),
                      pl.BlockSpec((nk, SC_WINDOW), lambda i: (0, i))],
            out_specs=[], core_axis_name=("c", "s"),
            dimension_semantics=(pltpu.PARALLEL,))(src_hbm, slot_hbm)

    return run(src, slot8)


def _sc_collect_sum(table, slots_tk, w_rep):
    n_tok = w_rep.shape[0]
    d = table.shape[1]
    group = SC_WINDOW // TOP_K
    chunks = d // SC_LANES

    @functools.partial(pl.kernel, out_type=jax.ShapeDtypeStruct((n_tok, 2 * d), F32),
                       mesh=_sc_mesh(),
                       scratch_types=[pltpu.VMEM((SC_GATHERS, SC_WINDOW, d), table.dtype),
                                      pltpu.SemaphoreType.DMA((SC_GATHERS,))],
                       compiler_params=pltpu.CompilerParams(needs_layout_passes=False),
                       name="collect")
    def run(tab_hbm, slot_hbm, w_hbm, out_hbm, rows_v, sems):
        def gather(i_vmem, b):
            return pltpu.make_async_copy(tab_hbm.at[i_vmem.at[b]], rows_v.at[b], sems.at[b])

        def body(i_vmem, w_vmem, o_vmem):
            for b in range(SC_GATHERS):
                gather(i_vmem, b).start()
            for b in range(SC_GATHERS):
                gather(i_vmem, b).wait()

                @pl.loop(0, group)
                def _(j):
                    tok = b * group + j
                    wv = [plsc.bitcast(w_vmem[tok, pl.ds(k * SC_LANES, SC_LANES)], BF16)
                          for k in range(TOP_K)]

                    def chunk(off):
                        prods = [plsc.bitcast(
                            rows_v[b, j * TOP_K + k, pl.ds(off, SC_LANES)], BF16) * wv[k]
                            for k in range(TOP_K)]
                        lo = jnp.zeros((SC_LANES,), F32)
                        hi = jnp.zeros((SC_LANES,), F32)
                        for k in range(0, TOP_K, 2):
                            u = plsc.bitcast(prods[k] + prods[k + 1], U32)
                            lo = lo + lax.bitcast_convert_type(u << 16, F32)
                            hi = hi + lax.bitcast_convert_type(u & jnp.uint32(0xFFFF0000), F32)
                        o_vmem[tok, pl.ds(off, SC_LANES)] = lo
                        o_vmem[tok, pl.ds(d + off, SC_LANES)] = hi

                    @plsc.parallel_loop(0, chunks, unroll=2)
                    def _(c):
                        chunk(c * SC_LANES)

        step_tok = SC_GATHERS * group
        pltpu.emit_pipeline(
            body, grid=(n_tok // step_tok,),
            in_specs=[pl.BlockSpec((SC_GATHERS, SC_WINDOW), lambda i: (i, 0)),
                      pl.BlockSpec((step_tok, TOP_K * SC_LANES), lambda i: (i, 0))],
            out_specs=[pl.BlockSpec((step_tok, 2 * d), lambda i: (i, 0))],
            core_axis_name=("c", "s"),
            dimension_semantics=(pltpu.PARALLEL,))(slot_hbm, w_hbm, out_hbm)

    return run(table, slots_tk.reshape(-1, SC_WINDOW), w_rep)


def _ffn_kernel(cnt_ref, nt_ref, first_ref, te_ref, tt_ref, xsa_hbm, xsb_hbm,
                wg_ref, wu_ref, wd_ref, ysa_hbm, ysb_hbm,
                xbuf, ybuf, wg_b, wu_b, wd_b, sem_in, sem_out, *, cap):
    e = pl.program_id(0)
    n_exp = pl.num_programs(0)
    n = nt_ref[e]
    g0 = first_ref[e]
    total = first_ref[n_exp - 1] + nt_ref[n_exp - 1]
    xs_hbm = (xsa_hbm, xsb_hbm)
    ys_hbm = (ysa_hbm, ysb_hbm)
    n_in = FFN_AHEAD + 1
    half = FFN_TILE // 2

    def at_height(ex, t, fn):
        few = cnt_ref[ex] - t * FFN_TILE <= half

        @pl.when(few)
        def _():
            fn(half)

        @pl.when(jnp.logical_not(few))
        def _():
            fn(FFN_TILE)

    def rows(ex, t, n_rows):
        return pl.ds(pl.multiple_of(ex * cap + t * FFN_TILE, FFN_TILE), n_rows)

    def in_copy(ex, t, slot, part, n_rows):
        return pltpu.make_async_copy(xs_hbm[part].at[rows(ex, t, n_rows)],
                                     xbuf.at[slot, part, pl.ds(0, n_rows)],
                                     sem_in.at[slot, part])

    def out_copy(ex, t, slot, part, n_rows):
        return pltpu.make_async_copy(ybuf.at[slot, part, pl.ds(0, n_rows)],
                                     ys_hbm[part].at[rows(ex, t, n_rows)],
                                     sem_out.at[slot, part])

    def start_in(g):
        ex, t = te_ref[g], tt_ref[g]
        at_height(ex, t, lambda nr: [in_copy(ex, t, g % n_in, part, nr).start()
                                     for part in range(PACK_PARTS)])

    def wait_out(g):
        ex, t = te_ref[g], tt_ref[g]
        at_height(ex, t, lambda nr: [out_copy(ex, t, g % 2, part, nr).wait()
                                     for part in range(PACK_PARTS)])

    @pl.when(n > 0)
    def _():
        @pl.when(g0 == 0)
        def _():
            for ahead in range(FFN_AHEAD):
                @pl.when(ahead < total)
                def _():
                    start_in(ahead)

        wg_b[...] = wg_ref[0].astype(BF16)
        wu_b[...] = wu_ref[0].astype(BF16)
        wd_b[...] = wd_ref[0].astype(BF16)

        def tile(t, carry):
            g = g0 + t
            slot = g % 2
            slot_in = g % n_in
            at_height(e, t, lambda nr: [in_copy(e, t, slot_in, part, nr).wait()
                                        for part in range(PACK_PARTS)])

            @pl.when(g + FFN_AHEAD < total)
            def _():
                start_in(g + FFN_AHEAD)

            @pl.when(g >= 2)
            def _():
                wait_out(g - 2)

            def compute(n_rows):
                x = _unpack_rows([xbuf[slot_in, part, 0:n_rows, :]
                                  for part in range(PACK_PARTS)]).astype(BF16)
                a = _silu(_dot(x, wg_b[...])) * _dot(x, wu_b[...])
                y = _pack_rows(_dot(a.astype(BF16), wd_b[...]))
                for part in range(PACK_PARTS):
                    ybuf[slot, part, 0:n_rows, :] = y[part]
                    out_copy(e, t, slot, part, n_rows).start()

            at_height(e, t, compute)
            return carry

        lax.fori_loop(0, n, tile, 0)

        @pl.when(g0 + n == total)
        def _():
            @pl.when(total >= 2)
            def _():
                wait_out(total - 2)

            wait_out(total - 1)


def _ffn(xs_parts, counts, w_gate, w_up, w_down):
    n_exp, d = w_gate.shape[0], w_gate.shape[1]
    cap = xs_parts[0].shape[0] // n_exp
    ntiles = (counts + FFN_TILE - 1) // FFN_TILE
    cum = jnp.cumsum(ntiles)
    first = cum - ntiles
    max_tiles = (cap * TOP_K) // FFN_TILE + n_exp
    g = jnp.arange(max_tiles, dtype=I32)
    done = g[:, None] >= cum[None, :]
    tile_e = jnp.minimum(jnp.sum(done, axis=1), n_exp - 1).astype(I32)
    tile_t = (g - jnp.sum(jnp.where(done, ntiles[None, :], 0), axis=1)).astype(I32)
    wmap = lambda e, cnt, nt, first, te, tt: (e, 0, 0)
    hbm = pl.BlockSpec(memory_space=pl.ANY)
    return pl.pallas_call(
        functools.partial(_ffn_kernel, cap=cap),
        grid_spec=pltpu.PrefetchScalarGridSpec(
            num_scalar_prefetch=5, grid=(n_exp,),
            in_specs=[hbm] * PACK_PARTS
                     + [pl.BlockSpec((1, d, D_EXPERT), wmap), pl.BlockSpec((1, d, D_EXPERT), wmap),
                        pl.BlockSpec((1, D_EXPERT, d), wmap)],
            out_specs=[hbm] * PACK_PARTS,
            scratch_shapes=[pltpu.VMEM((FFN_AHEAD + 1, PACK_PARTS, FFN_TILE, PACK_W), U32),
                            pltpu.VMEM((2, PACK_PARTS, FFN_TILE, PACK_W), U32),
                            pltpu.VMEM((d, D_EXPERT), BF16), pltpu.VMEM((d, D_EXPERT), BF16),
                            pltpu.VMEM((D_EXPERT, d), BF16),
                            pltpu.SemaphoreType.DMA((FFN_AHEAD + 1, PACK_PARTS)),
                            pltpu.SemaphoreType.DMA((2, PACK_PARTS))]),
        out_shape=[jax.ShapeDtypeStruct(xs_parts[0].shape, U32)] * PACK_PARTS,
        compiler_params=pltpu.CompilerParams(
            dimension_semantics=("arbitrary",), vmem_limit_bytes=VMEM_LIMIT),
        name="ffn",
    )(counts.astype(I32), ntiles.astype(I32), first.astype(I32), tile_e, tile_t, *xs_parts,
      w_gate, w_up, w_down)


def _combine_kernel(x1_ref, h2pa_ref, h2pb_ref, ra_ref, rb_ref, mod_ref,
                    wsg_ref, wsu_ref, wsd_ref, o_ref):
    h2 = _unpack_rows([h2pa_ref[0], h2pb_ref[0]]).astype(BF16)
    a = _silu(_dot(h2, wsg_ref[...])) * _dot(h2, wsu_ref[...])
    routed = jnp.concatenate([ra_ref[0], rb_ref[0]], axis=1)
    o_ref[0] = x1_ref[0] + mod_ref[0][5:6] * (_dot(a.astype(BF16), wsd_ref[...]) + routed)


def _combine(x1, h2p_parts, routed_parts, mod, per_batch_mod, p, tm):
    b, l, d = x1.shape
    mod_map = (lambda bi, i: (bi, 0, 0)) if per_batch_mod else (lambda bi, i: (0, 0, 0))
    tok = lambda w: pl.BlockSpec((1, tm, w), lambda bi, i: (bi, i, 0))
    full = lambda a: pl.BlockSpec(a.shape, lambda bi, i: (0, 0))
    weights = [p['w_sh_gate'], p['w_sh_up'], p['w_sh_down']]
    routed_parts = [r.reshape(b, l, d // PACK_PARTS) for r in routed_parts]
    return pl.pallas_call(
        _combine_kernel,
        grid=(b, l // tm),
        in_specs=[tok(d)] + [tok(PACK_W)] * PACK_PARTS + [tok(d // PACK_PARTS)] * PACK_PARTS
                 + [pl.BlockSpec((1, 6, d), mod_map)] + [full(w) for w in weights],
        out_specs=tok(d),
        out_shape=jax.ShapeDtypeStruct((b, l, d), F32),
        compiler_params=pltpu.CompilerParams(
            dimension_semantics=("parallel", "parallel"), vmem_limit_bytes=VMEM_LIMIT),
        name="combine",
    )(x1, *h2p_parts, *routed_parts, mod, *weights)


def _pad_heads(w, parts):
    k = w.shape[0]
    per = w.shape[1] // MLA_HEADS
    w = w.reshape(k, MLA_HEADS, per)[:, :, parts[0]:parts[1]]
    w = jnp.pad(w, ((0, 0), (0, 0), (0, HEAD_W - (parts[1] - parts[0]))))
    return w.reshape(k, MLA_HEADS * HEAD_W)


def _prep_params(l, g_norm1, g_norm2, w_in, w_gk_fwd, b_gk_fwd, w_gk_bwd, b_gk_bwd, g_gla_out,
                 g_q_lora, w_uq, g_kv_lora, w_ukv, g_qk_q, g_qk_k, w_out, w_router, b_router,
                 w_sh_gate, w_sh_up, w_sh_down):
    w = w_in[l]
    d = w.shape[0]
    o_lrf = COL_CQ
    o_lrb = o_lrf + GLA_GATE_RANK
    o_cq = o_lrb + GLA_GATE_RANK
    o_ckv = o_cq + Q_LORA
    o_kr = o_ckv + KV_LORA
    w_in_p = jnp.concatenate([
        w[:, 0:COL_CQ], w[:, o_cq:o_ckv], w[:, o_ckv:o_kr], jnp.zeros((d, KR_LANE0), w.dtype),
        w[:, o_kr:o_kr + ROPE_DIM], w[:, o_lrf:o_lrb], w[:, o_lrb:o_cq]], axis=1).astype(BF16)
    z = jnp.zeros((GLA_GATE_RANK, GLA_KEY_W), F32)
    w_gk_big = jnp.concatenate([
        jnp.zeros((LANES - 2 * GLA_GATE_RANK, 2 * GLA_KEY_W), F32),
        jnp.concatenate([w_gk_fwd[l], z], axis=1),
        jnp.concatenate([z, w_gk_bwd[l]], axis=1)], axis=0).astype(BF16)
    pad_gain = lambda g: jnp.pad(g, (0, HEAD_W - QK_DIM)).reshape(1, HEAD_W)
    w_ukv_h = w_ukv[l]
    return {
        'g_norm1': g_norm1[l].reshape(1, d), 'g_norm2': g_norm2[l].reshape(1, d),
        'w_in_p': w_in_p, 'w_gk_big': w_gk_big,
        'b_gk': jnp.concatenate([b_gk_fwd[l], b_gk_bwd[l]]).reshape(1, 2 * GLA_KEY_W),
        'g_gla_out': g_gla_out[l].reshape(1, GLA_DV),
        'g_q_lora': g_q_lora[l].reshape(1, Q_LORA),
        'w_uq_p': _pad_heads(w_uq[l], (0, QK_DIM)).astype(BF16),
        'g_kv_lora': g_kv_lora[l].reshape(1, KV_LORA),
        'w_ukv_p': jnp.concatenate([_pad_heads(w_ukv_h, (0, NOPE_DIM)),
                                    _pad_heads(w_ukv_h, (NOPE_DIM, NOPE_DIM + V_DIM))],
                                   axis=1).astype(BF16),
        'gq': pad_gain(g_qk_q[l]), 'gk': pad_gain(g_qk_k[l]),
        'w_out_top': w_out[l][:GLA_VAL_W].astype(BF16),
        'w_out_bot': w_out[l][GLA_VAL_W:].astype(BF16),
        'w_router_t': w_router[l].T, 'b_router_col': b_router[l].reshape(N_EXPERTS, 1),
        'w_sh_gate': w_sh_gate[l].astype(BF16), 'w_sh_up': w_sh_up[l].astype(BF16),
        'w_sh_down': w_sh_down[l].astype(BF16),
    }


def _rope_tables(length):
    pos = np.arange(length)
    r = (pos // GRID_W).astype(np.float32)
    c = (pos % GRID_W).astype(np.float32)
    half = ROPE_DIM // 4
    inv_freq = np.float32(ROPE_THETA) ** (-np.arange(half, dtype=np.float32) / np.float32(half))
    ang_r = r[:, None] * inv_freq[None, :]
    ang_c = c[:, None] * inv_freq[None, :]
    zeros = lambda w: np.zeros((length, w), np.float32)
    ones = lambda w: np.ones((length, w), np.float32)
    tail_w = HEAD_W - ROPE_LANE0 - ROPE_DIM
    cos = np.concatenate([ones(ROPE_LANE0), np.cos(ang_r), np.cos(ang_r), np.cos(ang_c),
                          np.cos(ang_c), ones(tail_w)], axis=1)
    s1 = np.concatenate([zeros(ROPE_LANE0), -np.sin(ang_r), zeros(half), -np.sin(ang_c),
                         zeros(half), zeros(tail_w)], axis=1)
    s2 = np.concatenate([zeros(ROPE_LANE0), zeros(half), np.sin(ang_r), zeros(half),
                         np.sin(ang_c), zeros(tail_w)], axis=1)
    return tuple(jnp.asarray(t, F32) for t in (cos, s1, s2))


def _mix_route_dispatch(x_tok, seq_shape, mod, per_batch_mod, p, s0f, s0b, rope_tabs, ctx_kv,
                        tq):
    bt, lt, d = x_tok.shape
    b, l = seq_shape
    n_tok = bt * lt
    (q, k, v, g, lgf, lgb, qm, km, vm, ckvn, kr) = _inproj(x_tok, mod, per_batch_mod, p,
                                                            rope_tabs, IN_TILE)
    seq = lambda a: a.reshape(b, l, a.shape[-1])
    og, sf, sb = _gla(seq(q), seq(k), seq(v), seq(g), seq(lgf), seq(lgb), s0f, s0b,
                      p['g_gla_out'], GLA_SEQS if l == GLA_BLOCK else 1)
    kv_pairs = [(seq(km), seq(vm))] + ([ctx_kv] if ctx_kv is not None else [])
    om = _attn(seq(qm), kv_pairs, tq, ATTN_SEQS if l == tq else 1)
    tokv = lambda a: a.reshape(bt, lt, a.shape[-1])
    x1, h2a, h2b, slot8, w8, cnt = _outproj(tokv(og), tokv(om), x_tok, mod, per_batch_mod, p,
                                            OUT_TILE)
    x_sorted = [_sc_scatter_rows(h.reshape(n_tok, PACK_W), slot8,
                                 N_EXPERTS * _expert_cap(n_tok)) for h in (h2a, h2b)]
    routed = dict(x1=x1, h2=(h2a, h2b), slot8=slot8, w8=w8, cnt=cnt, x_sorted=x_sorted)
    return routed, sf, sb, ckvn, kr


def _experts_collect(r, w_gate, w_up, w_down):
    n_tok = r['slot8'].shape[1]
    y_sorted = _ffn(r['x_sorted'], r['cnt'][:, 0].astype(I32), w_gate, w_up, w_down)
    slots_tk = r['slot8'].T.reshape(1, TOP_K * n_tok)
    w_rep = r['w8'].reshape(n_tok, LANES)
    return [_sc_collect_sum(y, slots_tk, w_rep) for y in y_sorted]


def kernel(x_prompt, x_sample, c, state_gla_fwd, state_gla_bwd, cache_mla_ckv, cache_mla_krope,
           c_ctx, w_ada, b_ada, g_norm1, g_norm2, w_in, w_gk_fwd, b_gk_fwd, w_gk_bwd, b_gk_bwd,
           g_gla_out, g_q_lora, w_uq, g_kv_lora, w_ukv, g_qk_q, g_qk_k, w_out,
           w_router, b_router, w_exp_gate, w_exp_up, w_exp_down, w_sh_gate, w_sh_up, w_sh_down):
    bp, lp, d = x_prompt.shape
    bs, ls, _ = x_sample.shape
    depth = w_ada.shape[0]
    xp = x_prompt.reshape(1, bp * lp, d)
    xs = x_sample
    new_f, new_b, new_ckv, new_kr = [], [], [], []
    lat_tabs = _rope_tables(ls)
    cvecs = jnp.concatenate([c_ctx[None], c, jnp.zeros((8 - 1 - bs, d), F32)], axis=0)
    for l in range(depth):
        p = _prep_params(l, g_norm1, g_norm2, w_in, w_gk_fwd, b_gk_fwd, w_gk_bwd, b_gk_bwd,
                         g_gla_out, g_q_lora, w_uq, g_kv_lora, w_ukv, g_qk_q, g_qk_k, w_out,
                         w_router, b_router, w_sh_gate, w_sh_up, w_sh_down)
        mod = _ada(cvecs, w_ada[l], b_ada[l]).reshape(8, 6, d)
        mod_p, mod_s = mod[0:1], mod[1:1 + bs]
        zeros = jnp.zeros((bp, GLA_KEY_W, GLA_DV), F32)
        r_ctx, sf, sb, ckvn, kr = _mix_route_dispatch(
            xp, (bp, lp), mod_p, False, p, zeros, zeros, (), None, lp)
        new_f.append(sf.reshape(bp, GLA_HEADS, GLA_DK, GLA_DV))
        new_b.append(sb.reshape(bp, GLA_HEADS, GLA_DK, GLA_DV))
        new_ckv.append(ckvn.reshape(bp, lp, KV_LORA))
        new_kr.append(kr.reshape(bp, lp, LANES)[:, :, KR_LANE0:KR_LANE0 + ROPE_DIM])
        kr_cache = jnp.pad(cache_mla_krope[:, l],
                           ((0, 0), (0, 0), (ROPE_LANE0, LANES - ROPE_LANE0 - ROPE_DIM)))
        ctx_kv = _cache_kv(cache_mla_ckv[:, l], kr_cache, p)
        s0f = state_gla_fwd[:, l].reshape(bs, GLA_KEY_W, GLA_DV)
        s0b = state_gla_bwd[:, l].reshape(bs, GLA_KEY_W, GLA_DV)
        r_lat, _, _, _, _ = _mix_route_dispatch(
            xs, (bs, ls), mod_s, True, p, s0f, s0b, lat_tabs, ctx_kv, ATTN_TILE)
        experts = (w_exp_gate[l], w_exp_up[l], w_exp_down[l])
        routed_ctx = _experts_collect(r_ctx, *experts)
        routed_lat = _experts_collect(r_lat, *experts)
        xp = _combine(r_ctx['x1'], r_ctx['h2'], routed_ctx, mod_p, False, p, OUT_TILE)
        xs = _combine(r_lat['x1'], r_lat['h2'], routed_lat, mod_s, True, p, OUT_TILE)
    return (xp.reshape(bp, lp, d), xs, jnp.stack(new_f, axis=1), jnp.stack(new_b, axis=1),
            jnp.stack(new_ckv, axis=1), jnp.stack(new_kr, axis=1))
```

```python
import functools

import jax
import jax.numpy as jnp
import numpy as np
from jax import lax
from jax.experimental import pallas as pl
from jax.experimental.pallas import tpu as pltpu
from jax.experimental.pallas import tpu_sc as plsc

F32 = jnp.float32
BF16 = jnp.bfloat16
I32 = jnp.int32
U32 = jnp.uint32

D_MODEL = 1024
EPS = 1e-6
GRID_W = 64
GLA_HEADS = 4
GLA_DK = 64
GLA_DV = 128
GLA_GATE_RANK = 16
GLA_GATE_NORM = 16.0
GLA_KEY_W = GLA_HEADS * GLA_DK
GLA_VAL_W = GLA_HEADS * GLA_DV
MLA_HEADS = 4
Q_LORA = 256
KV_LORA = 128
NOPE_DIM = 64
ROPE_DIM = 32
V_DIM = 128
QK_DIM = NOPE_DIM + ROPE_DIM
ROPE_THETA = 10000.0
N_EXPERTS = 64
TOP_K = 8
N_GROUPS = 8
TOPK_GROUPS = 4
D_EXPERT = 256
ROUTED_SCALE = 2.5

LANES = 128
HEAD_W = LANES
MLA_W = MLA_HEADS * HEAD_W
ROPE_LANE0 = NOPE_DIM
COL_V = 2 * GLA_KEY_W
COL_G = COL_V + GLA_VAL_W
COL_CQ = COL_G + GLA_VAL_W
COL_CKV = COL_CQ + Q_LORA
TAIL0 = COL_CKV + KV_LORA
IN_W = TAIL0 + LANES
KR_LANE0 = LANES - ROPE_DIM - 2 * GLA_GATE_RANK
GLA_BLOCK = 256
GLA_SUB = 64
GLA_SEQS = 2
ATTN_SEQS = 4
ATTN_TILE = 1024
ADA_TILE = 768
IN_TILE = 1024
IN_SUB = 256
OUT_TILE = 1024
FFN_TILE = 512
FFN_AHEAD = 5
PACK_PARTS = 2
PACK_W = D_MODEL // (2 * PACK_PARTS)
SC_WINDOW = 128
SC_LANES = 16
SC_GATHERS = 2
VMEM_LIMIT = 56 * 1024 * 1024

HIGHEST = lax.Precision.HIGHEST


def _dot(a, b, precision=None):
    return jnp.dot(a, b, preferred_element_type=F32, precision=precision)


def _dot_nt(a, b, precision=None):
    return lax.dot_general(a, b, (((1,), (1,)), ((), ())), preferred_element_type=F32,
                           precision=precision)


def _rms(x, width):
    ss = jnp.sum(x * x, axis=-1, keepdims=True) * (1.0 / width)
    return x * lax.rsqrt(ss + EPS)


def _silu(x):
    return x * jax.nn.sigmoid(x)


def _log_sigmoid(x):
    return jnp.minimum(x, 0.0) - jnp.log1p(jnp.exp(-jnp.abs(x)))


def _pack_rows(x):
    parts = []
    for i in range(PACK_PARTS):
        c0 = i * 2 * PACK_W
        lo = lax.bitcast_convert_type(x[:, c0:c0 + PACK_W].astype(BF16).astype(F32), U32)
        hi = lax.bitcast_convert_type(
            x[:, c0 + PACK_W:c0 + 2 * PACK_W].astype(BF16).astype(F32), U32)
        parts.append(hi | (lo >> 16))
    return parts


def _unpack_rows(parts):
    cols = []
    for w in parts:
        cols.append(lax.bitcast_convert_type(w << 16, F32))
        cols.append(lax.bitcast_convert_type(w & jnp.uint32(0xFFFF0000), F32))
    return jnp.concatenate(cols, axis=1)


def _ada_kernel(c_ref, w_ref, b_ref, o_ref):
    o_ref[...] = _dot(_silu(c_ref[...]), w_ref[...], precision=HIGHEST) + b_ref[...]


def _ada(cvecs, w_ada, b_ada):
    n = w_ada.shape[1]
    tn = ADA_TILE
    return pl.pallas_call(
        _ada_kernel,
        grid=(n // tn,),
        in_specs=[pl.BlockSpec((8, D_MODEL), lambda j: (0, 0)),
                  pl.BlockSpec((D_MODEL, tn), lambda j: (0, j)),
                  pl.BlockSpec((1, tn), lambda j: (0, j))],
        out_specs=pl.BlockSpec((8, tn), lambda j: (0, j)),
        out_shape=jax.ShapeDtypeStruct((8, n), F32),
        name="ada",
    )(cvecs, w_ada, b_ada.reshape(1, n))


def _rope(x, c, s1, s2):
    return x * c + pltpu.roll(x, LANES - 8, 1) * s1 + pltpu.roll(x, 8, 1) * s2


def _mla_kv(ckv_n, kr, w_ukv_ref, gk, rope_tabs, k_ref, v_ref, rs=slice(None)):
    kv = _dot(ckv_n.astype(BF16), w_ukv_ref[...])
    for h in range(MLA_HEADS):
        k_h = kv[:, h * HEAD_W:(h + 1) * HEAD_W] + kr
        k_h = _rms(k_h, QK_DIM) * gk
        if rope_tabs is not None:
            k_h = _rope(k_h, *rope_tabs)
        k_ref[0, rs, h * HEAD_W:(h + 1) * HEAD_W] = k_h.astype(BF16)
    v_ref[0, rs, :] = kv[:, MLA_W:].astype(BF16)


def _lane_mask(lo, hi, rows):
    lane = lax.broadcasted_iota(I32, (rows, LANES), 1)
    return (lane >= lo) & (lane < hi)


def _inproj_kernel(x_ref, mod_ref, g1_ref, win_ref, wgk_ref, bgk_ref, gql_ref, wuq_ref,
                   gkv_ref, wukv_ref, gq_ref, gk_ref, *refs, positions):
    tab_refs, outs = (refs[:3], refs[3:]) if positions else ((), refs)
    (q_ref, k_ref, v_ref, g_ref, lgf_ref, lgb_ref, qm_ref, km_ref, vm_ref, ckvn_ref,
     kr_ref) = outs
    mod = mod_ref[0]
    for r0 in range(0, x_ref.shape[1], IN_SUB):
        rs = slice(r0, r0 + IN_SUB)
        x = x_ref[0, rs, :]
        h = _rms(x, D_MODEL) * g1_ref[...] * (1.0 + mod[1:2]) + mod[0:1]
        y = _dot(h.astype(BF16), win_ref[...])
        q_ref[0, rs, :] = (y[:, 0:GLA_KEY_W] * GLA_DK ** -0.5).astype(BF16)
        k_ref[0, rs, :] = y[:, GLA_KEY_W:COL_V].astype(BF16)
        v_ref[0, rs, :] = y[:, COL_V:COL_G].astype(BF16)
        g_ref[0, rs, :] = y[:, COL_G:COL_CQ].astype(BF16)
        tail = y[:, TAIL0:IN_W]
        pre = _dot(tail.astype(BF16), wgk_ref[...]) + bgk_ref[...]
        logg = _log_sigmoid(pre) * (1.0 / GLA_GATE_NORM)
        lgf_ref[0, rs, :] = logg[:, 0:GLA_KEY_W]
        lgb_ref[0, rs, :] = logg[:, GLA_KEY_W:]
        tabs = tuple(r[rs, :] for r in tab_refs) if positions else None
        cq = _rms(y[:, COL_CQ:COL_CKV], Q_LORA) * gql_ref[...]
        qm = _dot(cq.astype(BF16), wuq_ref[...])
        gq = gq_ref[...]
        for hh in range(MLA_HEADS):
            q_h = _rms(qm[:, hh * HEAD_W:(hh + 1) * HEAD_W], QK_DIM) * gq
            if positions:
                q_h = _rope(q_h, *tabs)
            qm_ref[0, rs, hh * HEAD_W:(hh + 1) * HEAD_W] = (q_h * QK_DIM ** -0.5).astype(BF16)
        ckv_n = _rms(y[:, COL_CKV:TAIL0], KV_LORA) * gkv_ref[...]
        ckvn_ref[0, rs, :] = ckv_n
        kr_ref[0, rs, :] = tail
        kr = jnp.where(_lane_mask(KR_LANE0, KR_LANE0 + ROPE_DIM, IN_SUB), tail, 0.0)
        _mla_kv(ckv_n, kr, wukv_ref, gk_ref[...], tabs, km_ref, vm_ref, rs)


def _inproj(x, mod, per_batch_mod, p, rope_tabs, tm):
    b, l, d = x.shape
    nt = l // tm
    mod_map = (lambda bi, i: (bi, 0, 0)) if per_batch_mod else (lambda bi, i: (0, 0, 0))
    tab_map = lambda bi, i: (i, 0)
    const = lambda bi, i: (0, 0)
    tok = lambda w: pl.BlockSpec((1, tm, w), lambda bi, i: (bi, i, 0))
    full = lambda a: pl.BlockSpec(a.shape, const)
    weights = [p['g_norm1'], p['w_in_p'], p['w_gk_big'], p['b_gk'], p['g_q_lora'], p['w_uq_p'],
               p['g_kv_lora'], p['w_ukv_p'], p['gq'], p['gk']]
    outs = [(GLA_KEY_W, BF16), (GLA_KEY_W, BF16), (GLA_VAL_W, BF16), (GLA_VAL_W, BF16),
            (GLA_KEY_W, F32), (GLA_KEY_W, F32), (MLA_W, BF16), (MLA_W, BF16), (MLA_W, BF16),
            (KV_LORA, F32), (LANES, F32)]
    return pl.pallas_call(
        functools.partial(_inproj_kernel, positions=bool(rope_tabs)),
        grid=(b, nt),
        in_specs=[tok(d), pl.BlockSpec((1, 6, d), mod_map)] + [full(w) for w in weights]
                 + [pl.BlockSpec((tm, LANES), tab_map)] * len(rope_tabs),
        out_specs=[tok(w) for w, _ in outs],
        out_shape=[jax.ShapeDtypeStruct((b, l, w), dt) for w, dt in outs],
        compiler_params=pltpu.CompilerParams(
            dimension_semantics=("parallel", "parallel"), vmem_limit_bytes=VMEM_LIMIT),
        name="inproj",
    )(x, mod, *weights, *rope_tabs)


def _cache_kv_kernel(ckv_ref, kr_ref, wukv_ref, gk_ref, k_ref, v_ref):
    _mla_kv(ckv_ref[0], kr_ref[0], wukv_ref, gk_ref[...], None, k_ref, v_ref)


def _cache_kv(ckv, kr128, p):
    b, l, _ = ckv.shape
    tok = lambda w: pl.BlockSpec((1, l, w), lambda bi: (bi, 0, 0))
    full = lambda a: pl.BlockSpec(a.shape, lambda bi: (0, 0))
    return pl.pallas_call(
        _cache_kv_kernel,
        grid=(b,),
        in_specs=[tok(KV_LORA), tok(LANES), full(p['w_ukv_p']), full(p['gk'])],
        out_specs=[tok(MLA_W), tok(MLA_W)],
        out_shape=[jax.ShapeDtypeStruct((b, l, MLA_W), BF16)] * 2,
        name="cache_kv",
    )(ckv, kr128, p['w_ukv_p'], p['gk'])


def _split3(x):
    hi = x.astype(BF16)
    r1 = x - hi.astype(F32)
    mid = r1.astype(BF16)
    lo = (r1 - mid.astype(F32)).astype(BF16)
    return hi, mid, lo


def _gla_block(q, k, v, lg, s, fwd):
    n = GLA_BLOCK
    nc = n // GLA_SUB
    row = lax.broadcasted_iota(I32, (n, n), 0)
    col = lax.broadcasted_iota(I32, (n, n), 1)
    same = (row // GLA_SUB) == (col // GLA_SUB)
    causal = same & ((col <= row) if fwd else (col >= row))
    tri = causal.astype(BF16)
    hi, mid, lo = _split3(lg)
    cum = _dot(tri, hi) + _dot(tri, mid) + _dot(tri, lo)
    tot_rows, mid_rows = [], []
    for c in range(nc):
        r_tot = c * GLA_SUB + (GLA_SUB - 1 if fwd else 0)
        r_mid = c * GLA_SUB + GLA_SUB // 2
        tot_rows.append(jnp.broadcast_to(cum[r_tot:r_tot + 1], (GLA_SUB, GLA_KEY_W)))
        mid_rows.append(jnp.broadcast_to(cum[r_mid:r_mid + 1], (GLA_SUB, GLA_KEY_W)))
    tot_rows = jnp.concatenate(tot_rows, axis=0)
    mid_rows = jnp.concatenate(mid_rows, axis=0)
    rel = cum - mid_rows
    qi = q * jnp.exp(rel)
    ki = (k * jnp.exp(-rel)).astype(BF16)
    q_in = (q * jnp.exp(cum)).astype(BF16)
    k_up = k * jnp.exp(tot_rows - cum)
    dec = jnp.exp(tot_rows)
    k_up_t = k_up.T.astype(BF16)
    dec_t = dec.T
    lane_head = lax.broadcasted_iota(I32, (n, GLA_KEY_W), 1) // GLA_DK
    srow_head = lax.broadcasted_iota(I32, (GLA_KEY_W, GLA_DV), 0) // GLA_DK
    vrow_chunk = lax.broadcasted_iota(I32, (n, GLA_VAL_W), 0) // GLA_SUB
    o_heads = []
    for h in range(GLA_HEADS):
        a = _dot_nt(jnp.where(lane_head == h, qi, 0.0).astype(BF16), ki)
        a = jnp.where(causal, a, 0.0).astype(BF16)
        o_heads.append(_dot(a, v[:, h * GLA_DV:(h + 1) * GLA_DV]))
    o = jnp.concatenate(o_heads, axis=1)
    o_inter = [None] * nc
    for c in (range(nc) if fwd else range(nc - 1, -1, -1)):
        s_bd = jnp.concatenate(
            [jnp.where(srow_head == h, s, 0.0).astype(BF16) for h in range(GLA_HEADS)], axis=1)
        o_inter[c] = _dot(q_in[c * GLA_SUB:(c + 1) * GLA_SUB], s_bd)
        v_c = jnp.where(vrow_chunk == c, v, jnp.zeros_like(v))
        u = jnp.concatenate(
            [_dot(k_up_t[h * GLA_DK:(h + 1) * GLA_DK], v_c[:, h * GLA_DV:(h + 1) * GLA_DV])
             for h in range(GLA_HEADS)], axis=0)
        s = dec_t[:, c * GLA_SUB:c * GLA_SUB + 1] * s + u
    return o + jnp.concatenate(o_inter, axis=0), s


def _gla_kernel(q_ref, k_ref, v_ref, g_ref, lgf_ref, lgb_ref, s0f_ref, s0b_ref, gout_ref,
                o_ref, sf_ref, sb_ref, acc_ref, st_ref, *, nblk, n_seq):
    for si in range(n_seq):
        _gla_sequence(si, q_ref, k_ref, v_ref, g_ref, lgf_ref, lgb_ref, s0f_ref, s0b_ref,
                      gout_ref, o_ref, sf_ref, sb_ref, acc_ref, st_ref, nblk)


def _gla_sequence(si, q_ref, k_ref, v_ref, g_ref, lgf_ref, lgb_ref, s0f_ref, s0b_ref, gout_ref,
                  o_ref, sf_ref, sb_ref, acc_ref, st_ref, nblk):
    st_ref[si, 0] = s0f_ref[si]
    st_ref[si, 1] = s0b_ref[si]

    def load(blk):
        start = blk * GLA_BLOCK
        r = pl.ds(start if isinstance(blk, int) else pl.multiple_of(start, GLA_BLOCK), GLA_BLOCK)
        return r, q_ref[si, r, :].astype(F32), k_ref[si, r, :].astype(F32), v_ref[si, r, :]

    def pair_step(i, first_touch):
        rf, qf, kf, vf = load(i)
        rb, qb, kb, vb = load(nblk - 1 - i)
        of, s_f = _gla_block(qf, kf, vf, lgf_ref[si, rf, :], st_ref[si, 0], True)
        ob, s_b = _gla_block(qb, kb, vb, lgb_ref[si, rb, :], st_ref[si, 1], False)
        if first_touch:
            acc_ref[si, rf, :] = of
            acc_ref[si, rb, :] = ob
        else:
            acc_ref[si, rf, :] += of
            acc_ref[si, rb, :] += ob
        st_ref[si, 0] = s_f
        st_ref[si, 1] = s_b

    if nblk == 1:
        r, q, k, v = load(0)
        of, s_f = _gla_block(q, k, v, lgf_ref[si, r, :], st_ref[si, 0], True)
        ob, s_b = _gla_block(q, k, v, lgb_ref[si, r, :], st_ref[si, 1], False)
        acc_ref[si, r, :] = of + ob
        st_ref[si, 0] = s_f
        st_ref[si, 1] = s_b
    else:
        half = nblk // 2
        lax.fori_loop(0, half, lambda i, c: (pair_step(i, True), c)[1], 0)
        lax.fori_loop(half, nblk, lambda i, c: (pair_step(i, False), c)[1], 0)
    sf_ref[si] = st_ref[si, 0]
    sb_ref[si] = st_ref[si, 1]

    def fin_step(blk, carry):
        r = pl.ds(pl.multiple_of(blk * GLA_BLOCK, GLA_BLOCK), GLA_BLOCK)
        o = acc_ref[si, r, :]
        gate = _silu(g_ref[si, r, :].astype(F32))
        for h in range(GLA_HEADS):
            sl = slice(h * GLA_DV, (h + 1) * GLA_DV)
            o_ref[si, r, sl] = (_rms(o[:, sl], GLA_DV) * gout_ref[...] * gate[:, sl]).astype(BF16)
        return carry

    lax.fori_loop(0, nblk, fin_step, 0)


def _gla(q, k, v, g, lgf, lgb, s0f, s0b, g_out, n_seq):
    b, l, _ = q.shape
    assert l == GLA_BLOCK or l % (2 * GLA_BLOCK) == 0
    seq = lambda w: pl.BlockSpec((n_seq, l, w), lambda bi: (bi, 0, 0))
    st = pl.BlockSpec((n_seq, GLA_KEY_W, GLA_DV), lambda bi: (bi, 0, 0))
    return pl.pallas_call(
        functools.partial(_gla_kernel, nblk=l // GLA_BLOCK, n_seq=n_seq),
        grid=(b // n_seq,),
        in_specs=[seq(GLA_KEY_W), seq(GLA_KEY_W), seq(GLA_VAL_W), seq(GLA_VAL_W),
                  seq(GLA_KEY_W), seq(GLA_KEY_W), st, st,
                  pl.BlockSpec((1, GLA_DV), lambda bi: (0, 0))],
        out_specs=[seq(GLA_VAL_W), st, st],
        out_shape=[jax.ShapeDtypeStruct((b, l, GLA_VAL_W), BF16),
                   jax.ShapeDtypeStruct((b, GLA_KEY_W, GLA_DV), F32),
                   jax.ShapeDtypeStruct((b, GLA_KEY_W, GLA_DV), F32)],
        scratch_shapes=[pltpu.VMEM((n_seq, l, GLA_VAL_W), F32),
                        pltpu.VMEM((n_seq, 2, GLA_KEY_W, GLA_DV), F32)],
        compiler_params=pltpu.CompilerParams(
            dimension_semantics=("parallel",), vmem_limit_bytes=VMEM_LIMIT),
        name="gla",
    )(q, k, v, g, lgf, lgb, s0f, s0b, g_out)


def _attn_kernel(*refs, n_kv, n_seq):
    q_ref, o_ref = refs[0], refs[-1]
    kv = [(refs[1 + 2 * i], refs[2 + 2 * i]) for i in range(n_kv)]
    for si in range(n_seq):
        for h in range(MLA_HEADS):
            sl = slice(h * HEAD_W, (h + 1) * HEAD_W)
            q = q_ref[si, :, sl]
            scores = [_dot_nt(q, k_ref[si, :, sl]) for k_ref, _ in kv]
            m = functools.reduce(jnp.maximum,
                                 [jnp.max(s, axis=-1, keepdims=True) for s in scores])
            o, den = 0.0, 0.0
            for s, (_, v_ref) in zip(scores, kv):
                p = jnp.exp(s - m)
                den = den + jnp.sum(p, axis=-1, keepdims=True)
                o = o + _dot(p.astype(BF16), v_ref[si, :, sl])
            o_ref[si, :, sl] = (o / den).astype(BF16)


def _attn(q, kv_pairs, tq, n_seq):
    b, l, _ = q.shape
    kv_specs, kv_args = [], []
    for k, v in kv_pairs:
        spec = pl.BlockSpec((n_seq, k.shape[1], MLA_W), lambda bi, i: (bi, 0, 0))
        kv_specs += [spec, spec]
        kv_args += [k, v]
    return pl.pallas_call(
        functools.partial(_attn_kernel, n_kv=len(kv_pairs), n_seq=n_seq),
        grid=(b // n_seq, l // tq),
        in_specs=[pl.BlockSpec((n_seq, tq, MLA_W), lambda bi, i: (bi, i, 0))] + kv_specs,
        out_specs=pl.BlockSpec((n_seq, tq, MLA_W), lambda bi, i: (bi, i, 0)),
        out_shape=jax.ShapeDtypeStruct((b, l, MLA_W), BF16),
        compiler_params=pltpu.CompilerParams(
            dimension_semantics=("parallel", "parallel"), vmem_limit_bytes=VMEM_LIMIT),
        name="attn",
    )(q, *kv_args)


def _expert_cap(n_tok):
    return -(-n_tok // FFN_TILE) * FFN_TILE


def _sum01(x):
    return jnp.sum(jnp.sum(x, axis=1, keepdims=True), axis=0, keepdims=True)


def _route(logits_t, bias_col, cnt_col, cap):
    t = logits_t.shape[1]
    gsz = N_EXPERTS // N_GROUPS
    scores = jax.nn.sigmoid(logits_t)
    sel = (scores + bias_col).reshape(N_GROUPS, gsz, t)
    scores = scores.reshape(N_GROUPS, gsz, t)
    neg = jnp.float32(-jnp.inf)
    ie = lax.broadcasted_iota(I32, (N_GROUPS, gsz, t), 1)
    ig = lax.broadcasted_iota(I32, (N_GROUPS, gsz, t), 0)
    m1 = jnp.max(sel, axis=1, keepdims=True)
    first = jnp.min(jnp.where(sel == m1, ie, gsz), axis=1, keepdims=True)
    m2 = jnp.max(jnp.where(ie == first, neg, sel), axis=1, keepdims=True)
    grp = m1 + m2
    igk = lax.broadcasted_iota(I32, (N_GROUPS, 1, t), 0)
    g_sel = jnp.zeros((N_GROUPS, 1, t), jnp.bool_)
    cur = grp
    for _ in range(TOPK_GROUPS):
        m = jnp.max(cur, axis=0, keepdims=True)
        pick = igk == jnp.min(jnp.where(cur == m, igk, N_GROUPS), axis=0, keepdims=True)
        g_sel = g_sel | pick
        cur = jnp.where(pick, neg, cur)
    cur = jnp.where(g_sel, sel, neg)
    idx = ig * gsz + ie
    e_sel = jnp.zeros((N_GROUPS, gsz, t), jnp.bool_)
    picks = []
    for _ in range(TOP_K):
        m = jnp.max(jnp.max(cur, axis=1, keepdims=True), axis=0, keepdims=True)
        cand = jnp.where(cur == m, idx, N_EXPERTS)
        pick = idx == jnp.min(jnp.min(cand, axis=1, keepdims=True), axis=0, keepdims=True)
        picks.append(pick)
        e_sel = e_sel | pick
        cur = jnp.where(pick, neg, cur)
    w = jnp.where(e_sel, scores, 0.0)
    gate = w / _sum01(w) * ROUTED_SCALE
    sel_f = e_sel.astype(F32).reshape(N_EXPERTS, t)
    earlier = (lax.broadcasted_iota(I32, (t, t), 0) < lax.broadcasted_iota(I32, (t, t), 1))
    rank = _dot(sel_f.astype(BF16), earlier.astype(BF16))
    base = lax.broadcasted_iota(I32, (N_EXPERTS, 1), 0).astype(F32) * float(cap) + cnt_col
    slot = (base + rank).reshape(N_GROUPS, gsz, t)
    slot8 = jnp.concatenate([_sum01(jnp.where(pk, slot, 0.0)).reshape(1, t) for pk in picks],
                            axis=0).astype(I32)
    w8 = jnp.concatenate([_sum01(jnp.where(pk, gate, 0.0)).reshape(1, t) for pk in picks], axis=0)
    return slot8, w8, cnt_col + jnp.sum(sel_f, axis=1, keepdims=True)


def _outproj_kernel(og_ref, om_ref, x_ref, mod_ref, wtop_ref, wbot_ref, g2_ref, wr_ref, br_ref,
                    x1_ref, h2pa_ref, h2pb_ref, slot_ref, w8_ref, cnt_ref, cnt_scr, *, cap):
    @pl.when((pl.program_id(0) == 0) & (pl.program_id(1) == 0))
    def _():
        cnt_scr[...] = jnp.zeros_like(cnt_scr)

    mod = mod_ref[0]
    mix = _dot(og_ref[0], wtop_ref[...]) + _dot(om_ref[0], wbot_ref[...])
    x1 = x_ref[0] + mod[2:3] * mix
    x1_ref[0] = x1
    h2 = _rms(x1, D_MODEL) * g2_ref[...] * (1.0 + mod[4:5]) + mod[3:4]
    h2pa_ref[0], h2pb_ref[0] = _pack_rows(h2)
    logits_t = _dot_nt(wr_ref[...], h2, precision=HIGHEST)
    slot8, w8, cnt = _route(logits_t, br_ref[...], cnt_scr[:, 0:1], cap)
    slot_ref[...] = slot8
    t = w8.shape[1]
    wb = lax.bitcast_convert_type(w8.astype(BF16).astype(F32), U32)
    wb = wb | (wb >> 16)
    w8_ref[0] = jnp.concatenate(
        [jnp.broadcast_to(wb[k:k + 1], (SC_LANES, t)) for k in range(TOP_K)], axis=0).T
    cnt_scr[...] = jnp.broadcast_to(cnt, cnt_scr.shape)
    cnt_ref[...] = cnt_scr[...]


def _outproj(og, om, x, mod, per_batch_mod, p, tm):
    b, l, d = x.shape
    nt = l // tm
    mod_map = (lambda bi, i: (bi, 0, 0)) if per_batch_mod else (lambda bi, i: (0, 0, 0))
    tok = lambda w: pl.BlockSpec((1, tm, w), lambda bi, i: (bi, i, 0))
    full = lambda a: pl.BlockSpec(a.shape, lambda bi, i: (0, 0))
    weights = [p['w_out_top'], p['w_out_bot'], p['g_norm2'], p['w_router_t'], p['b_router_col']]
    cnt_shape = (N_EXPERTS, LANES)
    return pl.pallas_call(
        functools.partial(_outproj_kernel, cap=_expert_cap(b * l)),
        grid=(b, nt),
        in_specs=[tok(GLA_VAL_W), tok(MLA_W), tok(d), pl.BlockSpec((1, 6, d), mod_map)]
                 + [full(w) for w in weights],
        out_specs=[tok(d), tok(PACK_W), tok(PACK_W),
                   pl.BlockSpec((TOP_K, tm), lambda bi, i: (0, bi * nt + i)),
                   tok(LANES), pl.BlockSpec(cnt_shape, lambda bi, i: (0, 0))],
        out_shape=[jax.ShapeDtypeStruct((b, l, d), F32), jax.ShapeDtypeStruct((b, l, PACK_W), U32),
                   jax.ShapeDtypeStruct((b, l, PACK_W), U32),
                   jax.ShapeDtypeStruct((TOP_K, b * l), I32),
                   jax.ShapeDtypeStruct((b, l, LANES), U32),
                   jax.ShapeDtypeStruct(cnt_shape, F32)],
        scratch_shapes=[pltpu.VMEM(cnt_shape, F32)],
        compiler_params=pltpu.CompilerParams(
            dimension_semantics=("arbitrary", "arbitrary"), vmem_limit_bytes=VMEM_LIMIT),
        name="outproj",
    )(og, om, x, mod, *weights)


def _sc_mesh():
    return plsc.VectorSubcoreMesh(core_axis_name="c", subcore_axis_name="s")


def _sc_scatter_rows(src, slot8, n_rows_out):
    t, d = src.shape
    nk = slot8.shape[0]

    @functools.partial(pl.kernel, out_type=jax.ShapeDtypeStruct((n_rows_out, d), src.dtype),
                       mesh=_sc_mesh(), scratch_types=[], name="dispatch")
    def run(src_hbm, slot_hbm, out_hbm):
        def body(x_vmem, i_vmem):
            for k in range(nk):
                pltpu.sync_copy(x_vmem, out_hbm.at[i_vmem.at[k]])

        pltpu.emit_pipeline(
            body, grid=(t // SC_WINDOW,),
            in_specs=[pl.BlockSpec((SC_WINDOW, d), lambda i: (i, 0)),
                      pl.BlockSpec((nk, SC_WINDOW), lambda i: (0, i))],
            out_specs=[], core_axis_name=("c", "s"),
            dimension_semantics=(pltpu.PARALLEL,))(src_hbm, slot_hbm)

    return run(src, slot8)


def _sc_collect_sum(table, slots_tk, w_rep):
    n_tok = w_rep.shape[0]
    d = table.shape[1]
    group = SC_WINDOW // TOP_K
    chunks = d // SC_LANES

    @functools.partial(pl.kernel, out_type=jax.ShapeDtypeStruct((n_tok, 2 * d), F32),
                       mesh=_sc_mesh(),
                       scratch_types=[pltpu.VMEM((SC_GATHERS, SC_WINDOW, d), table.dtype),
                                      pltpu.SemaphoreType.DMA((SC_GATHERS,))],
                       compiler_params=pltpu.CompilerParams(needs_layout_passes=False),
                       name="collect")
    def run(tab_hbm, slot_hbm, w_hbm, out_hbm, rows_v, sems):
        def gather(i_vmem, b):
            return pltpu.make_async_copy(tab_hbm.at[i_vmem.at[b]], rows_v.at[b], sems.at[b])

        def body(i_vmem, w_vmem, o_vmem):
            for b in range(SC_GATHERS):
                gather(i_vmem, b).start()
            for b in range(SC_GATHERS):
                gather(i_vmem, b).wait()

                @pl.loop(0, group)
                def _(j):
                    tok = b * group + j
                    wv = [plsc.bitcast(w_vmem[tok, pl.ds(k * SC_LANES, SC_LANES)], BF16)
                          for k in range(TOP_K)]

                    def chunk(off):
                        prods = [plsc.bitcast(
                            rows_v[b, j * TOP_K + k, pl.ds(off, SC_LANES)], BF16) * wv[k]
                            for k in range(TOP_K)]
                        lo = jnp.zeros((SC_LANES,), F32)
                        hi = jnp.zeros((SC_LANES,), F32)
                        for k in range(0, TOP_K, 2):
                            u = plsc.bitcast(prods[k] + prods[k + 1], U32)
                            lo = lo + lax.bitcast_convert_type(u << 16, F32)
                            hi = hi + lax.bitcast_convert_type(u & jnp.uint32(0xFFFF0000), F32)
                        o_vmem[tok, pl.ds(off, SC_LANES)] = lo
                        o_vmem[tok, pl.ds(d + off, SC_LANES)] = hi

                    @plsc.parallel_loop(0, chunks, unroll=2)
                    def _(c):
                        chunk(c * SC_LANES)

        step_tok = SC_GATHERS * group
        pltpu.emit_pipeline(
            body, grid=(n_tok // step_tok,),
            in_specs=[pl.BlockSpec((SC_GATHERS, SC_WINDOW), lambda i: (i, 0)),
                      pl.BlockSpec((step_tok, TOP_K * SC_LANES), lambda i: (i, 0))],
            out_specs=[pl.BlockSpec((step_tok, 2 * d), lambda i: (i, 0))],
            core_axis_name=("c", "s"),
            dimension_semantics=(pltpu.PARALLEL,))(slot_hbm, w_hbm, out_hbm)

    return run(table, slots_tk.reshape(-1, SC_WINDOW), w_rep)


def _ffn_kernel(cnt_ref, nt_ref, first_ref, te_ref, tt_ref, xsa_hbm, xsb_hbm,
                wg_ref, wu_ref, wd_ref, ysa_hbm, ysb_hbm,
                xbuf, ybuf, wg_b, wu_b, wd_b, sem_in, sem_out, *, cap):
    e = pl.program_id(0)
    n_exp = pl.num_programs(0)
    n = nt_ref[e]
    g0 = first_ref[e]
    total = first_ref[n_exp - 1] + nt_ref[n_exp - 1]
    xs_hbm = (xsa_hbm, xsb_hbm)
    ys_hbm = (ysa_hbm, ysb_hbm)
    n_in = FFN_AHEAD + 1
    half = FFN_TILE // 2

    def at_height(ex, t, fn):
        few = cnt_ref[ex] - t * FFN_TILE <= half

        @pl.when(few)
        def _():
            fn(half)

        @pl.when(jnp.logical_not(few))
        def _():
            fn(FFN_TILE)

    def rows(ex, t, n_rows):
        return pl.ds(pl.multiple_of(ex * cap + t * FFN_TILE, FFN_TILE), n_rows)

    def in_copy(ex, t, slot, part, n_rows):
        return pltpu.make_async_copy(xs_hbm[part].at[rows(ex, t, n_rows)],
                                     xbuf.at[slot, part, pl.ds(0, n_rows)],
                                     sem_in.at[slot, part])

    def out_copy(ex, t, slot, part, n_rows):
        return pltpu.make_async_copy(ybuf.at[slot, part, pl.ds(0, n_rows)],
                                     ys_hbm[part].at[rows(ex, t, n_rows)],
                                     sem_out.at[slot, part])

    def start_in(g):
        ex, t = te_ref[g], tt_ref[g]
        at_height(ex, t, lambda nr: [in_copy(ex, t, g % n_in, part, nr).start()
                                     for part in range(PACK_PARTS)])

    def wait_out(g):
        ex, t = te_ref[g], tt_ref[g]
        at_height(ex, t, lambda nr: [out_copy(ex, t, g % 2, part, nr).wait()
                                     for part in range(PACK_PARTS)])

    @pl.when(n > 0)
    def _():
        @pl.when(g0 == 0)
        def _():
            for ahead in range(FFN_AHEAD):
                @pl.when(ahead < total)
                def _():
                    start_in(ahead)

        wg_b[...] = wg_ref[0].astype(BF16)
        wu_b[...] = wu_ref[0].astype(BF16)
        wd_b[...] = wd_ref[0].astype(BF16)

        def tile(t, carry):
            g = g0 + t
            slot = g % 2
            slot_in = g % n_in
            at_height(e, t, lambda nr: [in_copy(e, t, slot_in, part, nr).wait()
                                        for part in range(PACK_PARTS)])

            @pl.when(g + FFN_AHEAD < total)
            def _():
                start_in(g + FFN_AHEAD)

            @pl.when(g >= 2)
            def _():
                wait_out(g - 2)

            def compute(n_rows):
                x = _unpack_rows([xbuf[slot_in, part, 0:n_rows, :]
                                  for part in range(PACK_PARTS)]).astype(BF16)
                a = _silu(_dot(x, wg_b[...])) * _dot(x, wu_b[...])
                y = _pack_rows(_dot(a.astype(BF16), wd_b[...]))
                for part in range(PACK_PARTS):
                    ybuf[slot, part, 0:n_rows, :] = y[part]
                    out_copy(e, t, slot, part, n_rows).start()

            at_height(e, t, compute)
            return carry

        lax.fori_loop(0, n, tile, 0)

        @pl.when(g0 + n == total)
        def _():
            @pl.when(total >= 2)
            def _():
                wait_out(total - 2)

            wait_out(total - 1)


def _ffn(xs_parts, counts, w_gate, w_up, w_down):
    n_exp, d = w_gate.shape[0], w_gate.shape[1]
    cap = xs_parts[0].shape[0] // n_exp
    ntiles = (counts + FFN_TILE - 1) // FFN_TILE
    cum = jnp.cumsum(ntiles)
    first = cum - ntiles
    max_tiles = (cap * TOP_K) // FFN_TILE + n_exp
    g = jnp.arange(max_tiles, dtype=I32)
    done = g[:, None] >= cum[None, :]
    tile_e = jnp.minimum(jnp.sum(done, axis=1), n_exp - 1).astype(I32)
    tile_t = (g - jnp.sum(jnp.where(done, ntiles[None, :], 0), axis=1)).astype(I32)
    wmap = lambda e, cnt, nt, first, te, tt: (e, 0, 0)
    hbm = pl.BlockSpec(memory_space=pl.ANY)
    return pl.pallas_call(
        functools.partial(_ffn_kernel, cap=cap),
        grid_spec=pltpu.PrefetchScalarGridSpec(
            num_scalar_prefetch=5, grid=(n_exp,),
            in_specs=[hbm] * PACK_PARTS
                     + [pl.BlockSpec((1, d, D_EXPERT), wmap), pl.BlockSpec((1, d, D_EXPERT), wmap),
                        pl.BlockSpec((1, D_EXPERT, d), wmap)],
            out_specs=[hbm] * PACK_PARTS,
            scratch_shapes=[pltpu.VMEM((FFN_AHEAD + 1, PACK_PARTS, FFN_TILE, PACK_W), U32),
                            pltpu.VMEM((2, PACK_PARTS, FFN_TILE, PACK_W), U32),
                            pltpu.VMEM((d, D_EXPERT), BF16), pltpu.VMEM((d, D_EXPERT), BF16),
                            pltpu.VMEM((D_EXPERT, d), BF16),
                            pltpu.SemaphoreType.DMA((FFN_AHEAD + 1, PACK_PARTS)),
                            pltpu.SemaphoreType.DMA((2, PACK_PARTS))]),
        out_shape=[jax.ShapeDtypeStruct(xs_parts[0].shape, U32)] * PACK_PARTS,
        compiler_params=pltpu.CompilerParams(
            dimension_semantics=("arbitrary",), vmem_limit_bytes=VMEM_LIMIT),
        name="ffn",
    )(counts.astype(I32), ntiles.astype(I32), first.astype(I32), tile_e, tile_t, *xs_parts,
      w_gate, w_up, w_down)


def _combine_kernel(x1_ref, h2pa_ref, h2pb_ref, ra_ref, rb_ref, mod_ref,
                    wsg_ref, wsu_ref, wsd_ref, o_ref):
    h2 = _unpack_rows([h2pa_ref[0], h2pb_ref[0]]).astype(BF16)
    a = _silu(_dot(h2, wsg_ref[...])) * _dot(h2, wsu_ref[...])
    routed = jnp.concatenate([ra_ref[0], rb_ref[0]], axis=1)
    o_ref[0] = x1_ref[0] + mod_ref[0][5:6] * (_dot(a.astype(BF16), wsd_ref[...]) + routed)


def _combine(x1, h2p_parts, routed_parts, mod, per_batch_mod, p, tm):
    b, l, d = x1.shape
    mod_map = (lambda bi, i: (bi, 0, 0)) if per_batch_mod else (lambda bi, i: (0, 0, 0))
    tok = lambda w: pl.BlockSpec((1, tm, w), lambda bi, i: (bi, i, 0))
    full = lambda a: pl.BlockSpec(a.shape, lambda bi, i: (0, 0))
    weights = [p['w_sh_gate'], p['w_sh_up'], p['w_sh_down']]
    routed_parts = [r.reshape(b, l, d // PACK_PARTS) for r in routed_parts]
    return pl.pallas_call(
        _combine_kernel,
        grid=(b, l // tm),
        in_specs=[tok(d)] + [tok(PACK_W)] * PACK_PARTS + [tok(d // PACK_PARTS)] * PACK_PARTS
                 + [pl.BlockSpec((1, 6, d), mod_map)] + [full(w) for w in weights],
        out_specs=tok(d),
        out_shape=jax.ShapeDtypeStruct((b, l, d), F32),
        compiler_params=pltpu.CompilerParams(
            dimension_semantics=("parallel", "parallel"), vmem_limit_bytes=VMEM_LIMIT),
        name="combine",
    )(x1, *h2p_parts, *routed_parts, mod, *weights)


def _pad_heads(w, parts):
    k = w.shape[0]
    per = w.shape[1] // MLA_HEADS
    w = w.reshape(k, MLA_HEADS, per)[:, :, parts[0]:parts[1]]
    w = jnp.pad(w, ((0, 0), (0, 0), (0, HEAD_W - (parts[1] - parts[0]))))
    return w.reshape(k, MLA_HEADS * HEAD_W)


def _prep_params(l, g_norm1, g_norm2, w_in, w_gk_fwd, b_gk_fwd, w_gk_bwd, b_gk_bwd, g_gla_out,
                 g_q_lora, w_uq, g_kv_lora, w_ukv, g_qk_q, g_qk_k, w_out, w_router, b_router,
                 w_sh_gate, w_sh_up, w_sh_down):
    w = w_in[l]
    d = w.shape[0]
    o_lrf = COL_CQ
    o_lrb = o_lrf + GLA_GATE_RANK
    o_cq = o_lrb + GLA_GATE_RANK
    o_ckv = o_cq + Q_LORA
    o_kr = o_ckv + KV_LORA
    w_in_p = jnp.concatenate([
        w[:, 0:COL_CQ], w[:, o_cq:o_ckv], w[:, o_ckv:o_kr], jnp.zeros((d, KR_LANE0), w.dtype),
        w[:, o_kr:o_kr + ROPE_DIM], w[:, o_lrf:o_lrb], w[:, o_lrb:o_cq]], axis=1).astype(BF16)
    z = jnp.zeros((GLA_GATE_RANK, GLA_KEY_W), F32)
    w_gk_big = jnp.concatenate([
        jnp.zeros((LANES - 2 * GLA_GATE_RANK, 2 * GLA_KEY_W), F32),
        jnp.concatenate([w_gk_fwd[l], z], axis=1),
        jnp.concatenate([z, w_gk_bwd[l]], axis=1)], axis=0).astype(BF16)
    pad_gain = lambda g: jnp.pad(g, (0, HEAD_W - QK_DIM)).reshape(1, HEAD_W)
    w_ukv_h = w_ukv[l]
    return {
        'g_norm1': g_norm1[l].reshape(1, d), 'g_norm2': g_norm2[l].reshape(1, d),
        'w_in_p': w_in_p, 'w_gk_big': w_gk_big,
        'b_gk': jnp.concatenate([b_gk_fwd[l], b_gk_bwd[l]]).reshape(1, 2 * GLA_KEY_W),
        'g_gla_out': g_gla_out[l].reshape(1, GLA_DV),
        'g_q_lora': g_q_lora[l].reshape(1, Q_LORA),
        'w_uq_p': _pad_heads(w_uq[l], (0, QK_DIM)).astype(BF16),
        'g_kv_lora': g_kv_lora[l].reshape(1, KV_LORA),
        'w_ukv_p': jnp.concatenate([_pad_heads(w_ukv_h, (0, NOPE_DIM)),
                                    _pad_heads(w_ukv_h, (NOPE_DIM, NOPE_DIM + V_DIM))],
                                   axis=1).astype(BF16),
        'gq': pad_gain(g_qk_q[l]), 'gk': pad_gain(g_qk_k[l]),
        'w_out_top': w_out[l][:GLA_VAL_W].astype(BF16),
        'w_out_bot': w_out[l][GLA_VAL_W:].astype(BF16),
        'w_router_t': w_router[l].T, 'b_router_col': b_router[l].reshape(N_EXPERTS, 1),
        'w_sh_gate': w_sh_gate[l].astype(BF16), 'w_sh_up': w_sh_up[l].astype(BF16),
        'w_sh_down': w_sh_down[l].astype(BF16),
    }


def _rope_tables(length):
    pos = np.arange(length)
    r = (pos // GRID_W).astype(np.float32)
    c = (pos % GRID_W).astype(np.float32)
    half = ROPE_DIM // 4
    inv_freq = np.float32(ROPE_THETA) ** (-np.arange(half, dtype=np.float32) / np.float32(half))
    ang_r = r[:, None] * inv_freq[None, :]
    ang_c = c[:, None] * inv_freq[None, :]
    zeros = lambda w: np.zeros((length, w), np.float32)
    ones = lambda w: np.ones((length, w), np.float32)
    tail_w = HEAD_W - ROPE_LANE0 - ROPE_DIM
    cos = np.concatenate([ones(ROPE_LANE0), np.cos(ang_r), np.cos(ang_r), np.cos(ang_c),
                          np.cos(ang_c), ones(tail_w)], axis=1)
    s1 = np.concatenate([zeros(ROPE_LANE0), -np.sin(ang_r), zeros(half), -np.sin(ang_c),
                         zeros(half), zeros(tail_w)], axis=1)
    s2 = np.concatenate([zeros(ROPE_LANE0), zeros(half), np.sin(ang_r), zeros(half),
                         np.sin(ang_c), zeros(tail_w)], axis=1)
    return tuple(jnp.asarray(t, F32) for t in (cos, s1, s2))


def _mix_route_dispatch(x_tok, seq_shape, mod, per_batch_mod, p, s0f, s0b, rope_tabs, ctx_kv,
                        tq):
    bt, lt, d = x_tok.shape
    b, l = seq_shape
    n_tok = bt * lt
    (q, k, v, g, lgf, lgb, qm, km, vm, ckvn, kr) = _inproj(x_tok, mod, per_batch_mod, p,
                                                            rope_tabs, IN_TILE)
    seq = lambda a: a.reshape(b, l, a.shape[-1])
    og, sf, sb = _gla(seq(q), seq(k), seq(v), seq(g), seq(lgf), seq(lgb), s0f, s0b,
                      p['g_gla_out'], GLA_SEQS if l == GLA_BLOCK else 1)
    kv_pairs = [(seq(km), seq(vm))] + ([ctx_kv] if ctx_kv is not None else [])
    om = _attn(seq(qm), kv_pairs, tq, ATTN_SEQS if l == tq else 1)
    tokv = lambda a: a.reshape(bt, lt, a.shape[-1])
    x1, h2a, h2b, slot8, w8, cnt = _outproj(tokv(og), tokv(om), x_tok, mod, per_batch_mod, p,
                                            OUT_TILE)
    x_sorted = [_sc_scatter_rows(h.reshape(n_tok, PACK_W), slot8,
                                 N_EXPERTS * _expert_cap(n_tok)) for h in (h2a, h2b)]
    routed = dict(x1=x1, h2=(h2a, h2b), slot8=slot8, w8=w8, cnt=cnt, x_sorted=x_sorted)
    return routed, sf, sb, ckvn, kr


def _experts_collect(r, w_gate, w_up, w_down):
    n_tok = r['slot8'].shape[1]
    y_sorted = _ffn(r['x_sorted'], r['cnt'][:, 0].astype(I32), w_gate, w_up, w_down)
    slots_tk = r['slot8'].T.reshape(1, TOP_K * n_tok)
    w_rep = r['w8'].reshape(n_tok, LANES)
    return [_sc_collect_sum(y, slots_tk, w_rep) for y in y_sorted]


def kernel(x_prompt, x_sample, c, state_gla_fwd, state_gla_bwd, cache_mla_ckv, cache_mla_krope,
           c_ctx, w_ada, b_ada, g_norm1, g_norm2, w_in, w_gk_fwd, b_gk_fwd, w_gk_bwd, b_gk_bwd,
           g_gla_out, g_q_lora, w_uq, g_kv_lora, w_ukv, g_qk_q, g_qk_k, w_out,
           w_router, b_router, w_exp_gate, w_exp_up, w_exp_down, w_sh_gate, w_sh_up, w_sh_down):
    bp, lp, d = x_prompt.shape
    bs, ls, _ = x_sample.shape
    depth = w_ada.shape[0]
    xp = x_prompt.reshape(1, bp * lp, d)
    xs = x_sample
    new_f, new_b, new_ckv, new_kr = [], [], [], []
    lat_tabs = _rope_tables(ls)
    cvecs = jnp.concatenate([c_ctx[None], c, jnp.zeros((8 - 1 - bs, d), F32)], axis=0)
    for l in range(depth):
        p = _prep_params(l, g_norm1, g_norm2, w_in, w_gk_fwd, b_gk_fwd, w_gk_bwd, b_gk_bwd,
                         g_gla_out, g_q_lora, w_uq, g_kv_lora, w_ukv, g_qk_q, g_qk_k, w_out,
                         w_router, b_router, w_sh_gate, w_sh_up, w_sh_down)
        mod = _ada(cvecs, w_ada[l], b_ada[l]).reshape(8, 6, d)
        mod_p, mod_s = mod[0:1], mod[1:1 + bs]
        zeros = jnp.zeros((bp, GLA_KEY_W, GLA_DV), F32)
        r_ctx, sf, sb, ckvn, kr = _mix_route_dispatch(
            xp, (bp, lp), mod_p, False, p, zeros, zeros, (), None, lp)
        new_f.append(sf.reshape(bp, GLA_HEADS, GLA_DK, GLA_DV))
        new_b.append(sb.reshape(bp, GLA_HEADS, GLA_DK, GLA_DV))
        new_ckv.append(ckvn.reshape(bp, lp, KV_LORA))
        new_kr.append(kr.reshape(bp, lp, LANES)[:, :, KR_LANE0:KR_LANE0 + ROPE_DIM])
        kr_cache = jnp.pad(cache_mla_krope[:, l],
                           ((0, 0), (0, 0), (ROPE_LANE0, LANES - ROPE_LANE0 - ROPE_DIM)))
        ctx_kv = _cache_kv(cache_mla_ckv[:, l], kr_cache, p)
        s0f = state_gla_fwd[:, l].reshape(bs, GLA_KEY_W, GLA_DV)
        s0b = state_gla_bwd[:, l].reshape(bs, GLA_KEY_W, GLA_DV)
        r_lat, _, _, _, _ = _mix_route_dispatch(
            xs, (bs, ls), mod_s, True, p, s0f, s0b, lat_tabs, ctx_kv, ATTN_TILE)
        experts = (w_exp_gate[l], w_exp_up[l], w_exp_down[l])
        routed_ctx = _experts_collect(r_ctx, *experts)
        routed_lat = _experts_collect(r_lat, *experts)
        xp = _combine(r_ctx['x1'], r_ctx['h2'], routed_ctx, mod_p, False, p, OUT_TILE)
        xs = _combine(r_lat['x1'], r_lat['h2'], routed_lat, mod_s, True, p, OUT_TILE)
    return (xp.reshape(bp, lp, d), xs, jnp.stack(new_f, axis=1), jnp.stack(new_b, axis=1),
            jnp.stack(new_ckv, axis=1), jnp.stack(new_kr, axis=1))
```

```python
import functools

import jax
import jax.numpy as jnp
import numpy as np
from jax import lax
from jax.experimental import pallas as pl
from jax.experimental.pallas import tpu as pltpu
from jax.experimental.pallas import tpu_sc as plsc

F32 = jnp.float32
BF16 = jnp.bfloat16
I32 = jnp.int32
U32 = jnp.uint32

D_MODEL = 1024
EPS = 1e-6
GRID_W = 64
GLA_HEADS = 4
GLA_DK = 64
GLA_DV = 128
GLA_GATE_RANK = 16
GLA_GATE_NORM = 16.0
GLA_KEY_W = GLA_HEADS * GLA_DK
GLA_VAL_W = GLA_HEADS * GLA_DV
MLA_HEADS = 4
Q_LORA = 256
KV_LORA = 128
NOPE_DIM = 64
ROPE_DIM = 32
V_DIM = 128
QK_DIM = NOPE_DIM + ROPE_DIM
ROPE_THETA = 10000.0
N_EXPERTS = 64
TOP_K = 8
N_GROUPS = 8
TOPK_GROUPS = 4
D_EXPERT = 256
ROUTED_SCALE = 2.5

LANES = 128
HEAD_W = LANES
MLA_W = MLA_HEADS * HEAD_W
ROPE_LANE0 = NOPE_DIM
COL_V = 2 * GLA_KEY_W
COL_G = COL_V + GLA_VAL_W
COL_CQ = COL_G + GLA_VAL_W
COL_CKV = COL_CQ + Q_LORA
TAIL0 = COL_CKV + KV_LORA
IN_W = TAIL0 + LANES
KR_LANE0 = LANES - ROPE_DIM - 2 * GLA_GATE_RANK
GLA_BLOCK = 256
GLA_SUB = 64
GLA_SEQS = 2
ATTN_SEQS = 4
ATTN_TILE = 1024
ADA_TILE = 768
IN_TILE = 1024
IN_SUB = 256
OUT_TILE = 1024
FFN_TILE = 512
FFN_AHEAD = 5
FFN_GROUP = 2
PACK_PARTS = 2
PACK_W = D_MODEL // (2 * PACK_PARTS)
SC_WINDOW = 128
SC_LANES = 16
SC_GATHERS = 2
VMEM_LIMIT = 56 * 1024 * 1024

HIGHEST = lax.Precision.HIGHEST


def _dot(a, b, precision=None):
    return jnp.dot(a, b, preferred_element_type=F32, precision=precision)


def _dot_nt(a, b, precision=None):
    return lax.dot_general(a, b, (((1,), (1,)), ((), ())), preferred_element_type=F32,
                           precision=precision)


def _rms(x, width):
    ss = jnp.sum(x * x, axis=-1, keepdims=True) * (1.0 / width)
    return x * lax.rsqrt(ss + EPS)


def _silu(x):
    return x * jax.nn.sigmoid(x)


def _log_sigmoid(x):
    return jnp.minimum(x, 0.0) - jnp.log1p(jnp.exp(-jnp.abs(x)))


def _pack_rows(x):
    parts = []
    for i in range(PACK_PARTS):
        c0 = i * 2 * PACK_W
        lo = lax.bitcast_convert_type(x[:, c0:c0 + PACK_W].astype(BF16).astype(F32), U32)
        hi = lax.bitcast_convert_type(
            x[:, c0 + PACK_W:c0 + 2 * PACK_W].astype(BF16).astype(F32), U32)
        parts.append(hi | (lo >> 16))
    return parts


def _unpack_rows(parts):
    cols = []
    for w in parts:
        cols.append(lax.bitcast_convert_type(w << 16, F32))
        cols.append(lax.bitcast_convert_type(w & jnp.uint32(0xFFFF0000), F32))
    return jnp.concatenate(cols, axis=1)


def _ada_kernel(c_ref, w_ref, b_ref, o_ref):
    o_ref[...] = _dot(_silu(c_ref[...]), w_ref[...], precision=HIGHEST) + b_ref[...]


def _ada(cvecs, w_ada, b_ada):
    n = w_ada.shape[1]
    tn = ADA_TILE
    return pl.pallas_call(
        _ada_kernel,
        grid=(n // tn,),
        in_specs=[pl.BlockSpec((8, D_MODEL), lambda j: (0, 0)),
                  pl.BlockSpec((D_MODEL, tn), lambda j: (0, j)),
                  pl.BlockSpec((1, tn), lambda j: (0, j))],
        out_specs=pl.BlockSpec((8, tn), lambda j: (0, j)),
        out_shape=jax.ShapeDtypeStruct((8, n), F32),
        name="ada",
    )(cvecs, w_ada, b_ada.reshape(1, n))


def _rope(x, c, s1, s2):
    return x * c + pltpu.roll(x, LANES - 8, 1) * s1 + pltpu.roll(x, 8, 1) * s2


def _mla_kv(ckv_n, kr, w_ukv_ref, gk, rope_tabs, k_ref, v_ref, rs=slice(None)):
    kv = _dot(ckv_n.astype(BF16), w_ukv_ref[...])
    for h in range(MLA_HEADS):
        k_h = kv[:, h * HEAD_W:(h + 1) * HEAD_W] + kr
        k_h = _rms(k_h, QK_DIM) * gk
        if rope_tabs is not None:
            k_h = _rope(k_h, *rope_tabs)
        k_ref[0, rs, h * HEAD_W:(h + 1) * HEAD_W] = k_h.astype(BF16)
    v_ref[0, rs, :] = kv[:, MLA_W:].astype(BF16)


def _lane_mask(lo, hi, rows):
    lane = lax.broadcasted_iota(I32, (rows, LANES), 1)
    return (lane >= lo) & (lane < hi)


def _inproj_kernel(x_ref, mod_ref, g1_ref, win_ref, wgk_ref, bgk_ref, gql_ref, wuq_ref,
                   gkv_ref, wukv_ref, gq_ref, gk_ref, *refs, positions):
    tab_refs, outs = (refs[:3], refs[3:]) if positions else ((), refs)
    (q_ref, k_ref, v_ref, g_ref, lgf_ref, lgb_ref, qm_ref, km_ref, vm_ref, ckvn_ref,
     kr_ref) = outs
    mod = mod_ref[0]
    for r0 in range(0, x_ref.shape[1], IN_SUB):
        rs = slice(r0, r0 + IN_SUB)
        x = x_ref[0, rs, :]
        h = _rms(x, D_MODEL) * g1_ref[...] * (1.0 + mod[1:2]) + mod[0:1]
        y = _dot(h.astype(BF16), win_ref[...])
        q_ref[0, rs, :] = (y[:, 0:GLA_KEY_W] * GLA_DK ** -0.5).astype(BF16)
        k_ref[0, rs, :] = y[:, GLA_KEY_W:COL_V].astype(BF16)
        v_ref[0, rs, :] = y[:, COL_V:COL_G].astype(BF16)
        g_ref[0, rs, :] = y[:, COL_G:COL_CQ].astype(BF16)
        tail = y[:, TAIL0:IN_W]
        pre = _dot(tail.astype(BF16), wgk_ref[...]) + bgk_ref[...]
        logg = _log_sigmoid(pre) * (1.0 / GLA_GATE_NORM)
        lgf_ref[0, rs, :] = logg[:, 0:GLA_KEY_W]
        lgb_ref[0, rs, :] = logg[:, GLA_KEY_W:]
        tabs = tuple(r[rs, :] for r in tab_refs) if positions else None
        cq = _rms(y[:, COL_CQ:COL_CKV], Q_LORA) * gql_ref[...]
        qm = _dot(cq.astype(BF16), wuq_ref[...])
        gq = gq_ref[...]
        for hh in range(MLA_HEADS):
            q_h = _rms(qm[:, hh * HEAD_W:(hh + 1) * HEAD_W], QK_DIM) * gq
            if positions:
                q_h = _rope(q_h, *tabs)
            qm_ref[0, rs, hh * HEAD_W:(hh + 1) * HEAD_W] = (q_h * QK_DIM ** -0.5).astype(BF16)
        ckv_n = _rms(y[:, COL_CKV:TAIL0], KV_LORA) * gkv_ref[...]
        ckvn_ref[0, rs, :] = ckv_n
        kr_ref[0, rs, :] = tail
        kr = jnp.where(_lane_mask(KR_LANE0, KR_LANE0 + ROPE_DIM, IN_SUB), tail, 0.0)
        _mla_kv(ckv_n, kr, wukv_ref, gk_ref[...], tabs, km_ref, vm_ref, rs)


def _inproj(x, mod, per_batch_mod, p, rope_tabs, tm):
    b, l, d = x.shape
    nt = l // tm
    mod_map = (lambda bi, i: (bi, 0, 0)) if per_batch_mod else (lambda bi, i: (0, 0, 0))
    tab_map = lambda bi, i: (i, 0)
    const = lambda bi, i: (0, 0)
    tok = lambda w: pl.BlockSpec((1, tm, w), lambda bi, i: (bi, i, 0))
    full = lambda a: pl.BlockSpec(a.shape, const)
    weights = [p['g_norm1'], p['w_in_p'], p['w_gk_big'], p['b_gk'], p['g_q_lora'], p['w_uq_p'],
               p['g_kv_lora'], p['w_ukv_p'], p['gq'], p['gk']]
    outs = [(GLA_KEY_W, BF16), (GLA_KEY_W, BF16), (GLA_VAL_W, BF16), (GLA_VAL_W, BF16),
            (GLA_KEY_W, F32), (GLA_KEY_W, F32), (MLA_W, BF16), (MLA_W, BF16), (MLA_W, BF16),
            (KV_LORA, F32), (LANES, F32)]
    return pl.pallas_call(
        functools.partial(_inproj_kernel, positions=bool(rope_tabs)),
        grid=(b, nt),
        in_specs=[tok(d), pl.BlockSpec((1, 6, d), mod_map)] + [full(w) for w in weights]
                 + [pl.BlockSpec((tm, LANES), tab_map)] * len(rope_tabs),
        out_specs=[tok(w) for w, _ in outs],
        out_shape=[jax.ShapeDtypeStruct((b, l, w), dt) for w, dt in outs],
        compiler_params=pltpu.CompilerParams(
            dimension_semantics=("parallel", "parallel"), vmem_limit_bytes=VMEM_LIMIT),
        name="inproj",
    )(x, mod, *weights, *rope_tabs)


def _cache_kv_kernel(ckv_ref, kr_ref, wukv_ref, gk_ref, k_ref, v_ref):
    _mla_kv(ckv_ref[0], kr_ref[0], wukv_ref, gk_ref[...], None, k_ref, v_ref)


def _cache_kv(ckv, kr128, p):
    b, l, _ = ckv.shape
    tok = lambda w: pl.BlockSpec((1, l, w), lambda bi: (bi, 0, 0))
    full = lambda a: pl.BlockSpec(a.shape, lambda bi: (0, 0))
    return pl.pallas_call(
        _cache_kv_kernel,
        grid=(b,),
        in_specs=[tok(KV_LORA), tok(LANES), full(p['w_ukv_p']), full(p['gk'])],
        out_specs=[tok(MLA_W), tok(MLA_W)],
        out_shape=[jax.ShapeDtypeStruct((b, l, MLA_W), BF16)] * 2,
        name="cache_kv",
    )(ckv, kr128, p['w_ukv_p'], p['gk'])


def _split3(x):
    hi = x.astype(BF16)
    r1 = x - hi.astype(F32)
    mid = r1.astype(BF16)
    lo = (r1 - mid.astype(F32)).astype(BF16)
    return hi, mid, lo


def _gla_block(q, k, v, lg, s, fwd):
    n = GLA_BLOCK
    nc = n // GLA_SUB
    row = lax.broadcasted_iota(I32, (n, n), 0)
    col = lax.broadcasted_iota(I32, (n, n), 1)
    same = (row // GLA_SUB) == (col // GLA_SUB)
    causal = same & ((col <= row) if fwd else (col >= row))
    tri = causal.astype(BF16)
    hi, mid, lo = _split3(lg)
    cum = _dot(tri, hi) + _dot(tri, mid) + _dot(tri, lo)
    tot_rows, mid_rows = [], []
    for c in range(nc):
        r_tot = c * GLA_SUB + (GLA_SUB - 1 if fwd else 0)
        r_mid = c * GLA_SUB + GLA_SUB // 2
        tot_rows.append(jnp.broadcast_to(cum[r_tot:r_tot + 1], (GLA_SUB, GLA_KEY_W)))
        mid_rows.append(jnp.broadcast_to(cum[r_mid:r_mid + 1], (GLA_SUB, GLA_KEY_W)))
    tot_rows = jnp.concatenate(tot_rows, axis=0)
    mid_rows = jnp.concatenate(mid_rows, axis=0)
    rel = cum - mid_rows
    qi = q * jnp.exp(rel)
    ki = (k * jnp.exp(-rel)).astype(BF16)
    q_in = (q * jnp.exp(cum)).astype(BF16)
    k_up = k * jnp.exp(tot_rows - cum)
    dec = jnp.exp(tot_rows)
    k_up_t = k_up.T.astype(BF16)
    dec_t = dec.T
    lane_head = lax.broadcasted_iota(I32, (n, GLA_KEY_W), 1) // GLA_DK
    srow_head = lax.broadcasted_iota(I32, (GLA_KEY_W, GLA_DV), 0) // GLA_DK
    vrow_chunk = lax.broadcasted_iota(I32, (n, GLA_VAL_W), 0) // GLA_SUB
    o_heads = []
    for h in range(GLA_HEADS):
        a = _dot_nt(jnp.where(lane_head == h, qi, 0.0).astype(BF16), ki)
        a = jnp.where(causal, a, 0.0).astype(BF16)
        o_heads.append(_dot(a, v[:, h * GLA_DV:(h + 1) * GLA_DV]))
    o = jnp.concatenate(o_heads, axis=1)
    o_inter = [None] * nc
    for c in (range(nc) if fwd else range(nc - 1, -1, -1)):
        s_bd = jnp.concatenate(
            [jnp.where(srow_head == h, s, 0.0).astype(BF16) for h in range(GLA_HEADS)], axis=1)
        o_inter[c] = _dot(q_in[c * GLA_SUB:(c + 1) * GLA_SUB], s_bd)
        v_c = jnp.where(vrow_chunk == c, v, jnp.zeros_like(v))
        u = jnp.concatenate(
            [_dot(k_up_t[h * GLA_DK:(h + 1) * GLA_DK], v_c[:, h * GLA_DV:(h + 1) * GLA_DV])
             for h in range(GLA_HEADS)], axis=0)
        s = dec_t[:, c * GLA_SUB:c * GLA_SUB + 1] * s + u
    return o + jnp.concatenate(o_inter, axis=0), s


def _gla_kernel(q_ref, k_ref, v_ref, g_ref, lgf_ref, lgb_ref, s0f_ref, s0b_ref, gout_ref,
                o_ref, sf_ref, sb_ref, acc_ref, st_ref, *, nblk, n_seq):
    for si in range(n_seq):
        _gla_sequence(si, q_ref, k_ref, v_ref, g_ref, lgf_ref, lgb_ref, s0f_ref, s0b_ref,
                      gout_ref, o_ref, sf_ref, sb_ref, acc_ref, st_ref, nblk)


def _gla_sequence(si, q_ref, k_ref, v_ref, g_ref, lgf_ref, lgb_ref, s0f_ref, s0b_ref, gout_ref,
                  o_ref, sf_ref, sb_ref, acc_ref, st_ref, nblk):
    st_ref[si, 0] = s0f_ref[si]
    st_ref[si, 1] = s0b_ref[si]

    def load(blk):
        start = blk * GLA_BLOCK
        r = pl.ds(start if isinstance(blk, int) else pl.multiple_of(start, GLA_BLOCK), GLA_BLOCK)
        return r, q_ref[si, r, :].astype(F32), k_ref[si, r, :].astype(F32), v_ref[si, r, :]

    def pair_step(i, first_touch):
        rf, qf, kf, vf = load(i)
        rb, qb, kb, vb = load(nblk - 1 - i)
        of, s_f = _gla_block(qf, kf, vf, lgf_ref[si, rf, :], st_ref[si, 0], True)
        ob, s_b = _gla_block(qb, kb, vb, lgb_ref[si, rb, :], st_ref[si, 1], False)
        if first_touch:
            acc_ref[si, rf, :] = of
            acc_ref[si, rb, :] = ob
        else:
            acc_ref[si, rf, :] += of
            acc_ref[si, rb, :] += ob
        st_ref[si, 0] = s_f
        st_ref[si, 1] = s_b

    if nblk == 1:
        r, q, k, v = load(0)
        of, s_f = _gla_block(q, k, v, lgf_ref[si, r, :], st_ref[si, 0], True)
        ob, s_b = _gla_block(q, k, v, lgb_ref[si, r, :], st_ref[si, 1], False)
        acc_ref[si, r, :] = of + ob
        st_ref[si, 0] = s_f
        st_ref[si, 1] = s_b
    else:
        half = nblk // 2
        lax.fori_loop(0, half, lambda i, c: (pair_step(i, True), c)[1], 0)
        lax.fori_loop(half, nblk, lambda i, c: (pair_step(i, False), c)[1], 0)
    sf_ref[si] = st_ref[si, 0]
    sb_ref[si] = st_ref[si, 1]

    def fin_step(blk, carry):
        r = pl.ds(pl.multiple_of(blk * GLA_BLOCK, GLA_BLOCK), GLA_BLOCK)
        o = acc_ref[si, r, :]
        gate = _silu(g_ref[si, r, :].astype(F32))
        for h in range(GLA_HEADS):
            sl = slice(h * GLA_DV, (h + 1) * GLA_DV)
            o_ref[si, r, sl] = (_rms(o[:, sl], GLA_DV) * gout_ref[...] * gate[:, sl]).astype(BF16)
        return carry

    lax.fori_loop(0, nblk, fin_step, 0)


def _gla(q, k, v, g, lgf, lgb, s0f, s0b, g_out, n_seq):
    b, l, _ = q.shape
    assert l == GLA_BLOCK or l % (2 * GLA_BLOCK) == 0
    seq = lambda w: pl.BlockSpec((n_seq, l, w), lambda bi: (bi, 0, 0))
    st = pl.BlockSpec((n_seq, GLA_KEY_W, GLA_DV), lambda bi: (bi, 0, 0))
    return pl.pallas_call(
        functools.partial(_gla_kernel, nblk=l // GLA_BLOCK, n_seq=n_seq),
        grid=(b // n_seq,),
        in_specs=[seq(GLA_KEY_W), seq(GLA_KEY_W), seq(GLA_VAL_W), seq(GLA_VAL_W),
                  seq(GLA_KEY_W), seq(GLA_KEY_W), st, st,
                  pl.BlockSpec((1, GLA_DV), lambda bi: (0, 0))],
        out_specs=[seq(GLA_VAL_W), st, st],
        out_shape=[jax.ShapeDtypeStruct((b, l, GLA_VAL_W), BF16),
                   jax.ShapeDtypeStruct((b, GLA_KEY_W, GLA_DV), F32),
                   jax.ShapeDtypeStruct((b, GLA_KEY_W, GLA_DV), F32)],
        scratch_shapes=[pltpu.VMEM((n_seq, l, GLA_VAL_W), F32),
                        pltpu.VMEM((n_seq, 2, GLA_KEY_W, GLA_DV), F32)],
        compiler_params=pltpu.CompilerParams(
            dimension_semantics=("parallel",), vmem_limit_bytes=VMEM_LIMIT),
        name="gla",
    )(q, k, v, g, lgf, lgb, s0f, s0b, g_out)


def _attn_kernel(*refs, n_kv, n_seq):
    q_ref, o_ref = refs[0], refs[-1]
    kv = [(refs[1 + 2 * i], refs[2 + 2 * i]) for i in range(n_kv)]
    for si in range(n_seq):
        for h in range(MLA_HEADS):
            sl = slice(h * HEAD_W, (h + 1) * HEAD_W)
            q = q_ref[si, :, sl]
            scores = [_dot_nt(q, k_ref[si, :, sl]) for k_ref, _ in kv]
            m = functools.reduce(jnp.maximum,
                                 [jnp.max(s, axis=-1, keepdims=True) for s in scores])
            o, den = 0.0, 0.0
            for s, (_, v_ref) in zip(scores, kv):
                p = jnp.exp(s - m)
                den = den + jnp.sum(p, axis=-1, keepdims=True)
                o = o + _dot(p.astype(BF16), v_ref[si, :, sl])
            o_ref[si, :, sl] = (o / den).astype(BF16)


def _attn(q, kv_pairs, tq, n_seq):
    b, l, _ = q.shape
    kv_specs, kv_args = [], []
    for k, v in kv_pairs:
        spec = pl.BlockSpec((n_seq, k.shape[1], MLA_W), lambda bi, i: (bi, 0, 0))
        kv_specs += [spec, spec]
        kv_args += [k, v]
    return pl.pallas_call(
        functools.partial(_attn_kernel, n_kv=len(kv_pairs), n_seq=n_seq),
        grid=(b // n_seq, l // tq),
        in_specs=[pl.BlockSpec((n_seq, tq, MLA_W), lambda bi, i: (bi, i, 0))] + kv_specs,
        out_specs=pl.BlockSpec((n_seq, tq, MLA_W), lambda bi, i: (bi, i, 0)),
        out_shape=jax.ShapeDtypeStruct((b, l, MLA_W), BF16),
        compiler_params=pltpu.CompilerParams(
            dimension_semantics=("parallel", "parallel"), vmem_limit_bytes=VMEM_LIMIT),
        name="attn",
    )(q, *kv_args)


def _expert_cap(n_tok):
    return -(-n_tok // FFN_TILE) * FFN_TILE


def _sum01(x):
    return jnp.sum(jnp.sum(x, axis=1, keepdims=True), axis=0, keepdims=True)


def _route(logits_t, bias_col, cnt_col, cap):
    t = logits_t.shape[1]
    gsz = N_EXPERTS // N_GROUPS
    scores = jax.nn.sigmoid(logits_t)
    sel = (scores + bias_col).reshape(N_GROUPS, gsz, t)
    scores = scores.reshape(N_GROUPS, gsz, t)
    neg = jnp.float32(-jnp.inf)
    ie = lax.broadcasted_iota(I32, (N_GROUPS, gsz, t), 1)
    ig = lax.broadcasted_iota(I32, (N_GROUPS, gsz, t), 0)
    m1 = jnp.max(sel, axis=1, keepdims=True)
    first = jnp.min(jnp.where(sel == m1, ie, gsz), axis=1, keepdims=True)
    m2 = jnp.max(jnp.where(ie == first, neg, sel), axis=1, keepdims=True)
    grp = m1 + m2
    igk = lax.broadcasted_iota(I32, (N_GROUPS, 1, t), 0)
    g_sel = jnp.zeros((N_GROUPS, 1, t), jnp.bool_)
    cur = grp
    for _ in range(TOPK_GROUPS):
        m = jnp.max(cur, axis=0, keepdims=True)
        pick = igk == jnp.min(jnp.where(cur == m, igk, N_GROUPS), axis=0, keepdims=True)
        g_sel = g_sel | pick
        cur = jnp.where(pick, neg, cur)
    cur = jnp.where(g_sel, sel, neg)
    idx = ig * gsz + ie
    e_sel = jnp.zeros((N_GROUPS, gsz, t), jnp.bool_)
    picks = []
    for _ in range(TOP_K):
        m = jnp.max(jnp.max(cur, axis=1, keepdims=True), axis=0, keepdims=True)
        cand = jnp.where(cur == m, idx, N_EXPERTS)
        pick = idx == jnp.min(jnp.min(cand, axis=1, keepdims=True), axis=0, keepdims=True)
        picks.append(pick)
        e_sel = e_sel | pick
        cur = jnp.where(pick, neg, cur)
    w = jnp.where(e_sel, scores, 0.0)
    gate = w / _sum01(w) * ROUTED_SCALE
    sel_f = e_sel.astype(F32).reshape(N_EXPERTS, t)
    earlier = (lax.broadcasted_iota(I32, (t, t), 0) < lax.broadcasted_iota(I32, (t, t), 1))
    rank = _dot(sel_f.astype(BF16), earlier.astype(BF16))
    base = lax.broadcasted_iota(I32, (N_EXPERTS, 1), 0).astype(F32) * float(cap) + cnt_col
    slot = (base + rank).reshape(N_GROUPS, gsz, t)
    slot8 = jnp.concatenate([_sum01(jnp.where(pk, slot, 0.0)).reshape(1, t) for pk in picks],
                            axis=0).astype(I32)
    w8 = jnp.concatenate([_sum01(jnp.where(pk, gate, 0.0)).reshape(1, t) for pk in picks], axis=0)
    return slot8, w8, cnt_col + jnp.sum(sel_f, axis=1, keepdims=True)


def _outproj_kernel(og_ref, om_ref, x_ref, mod_ref, wtop_ref, wbot_ref, g2_ref, wr_ref, br_ref,
                    x1_ref, h2pa_ref, h2pb_ref, slot_ref, w8_ref, cnt_ref, cnt_scr, *, cap):
    @pl.when((pl.program_id(0) == 0) & (pl.program_id(1) == 0))
    def _():
        cnt_scr[...] = jnp.zeros_like(cnt_scr)

    mod = mod_ref[0]
    mix = _dot(og_ref[0], wtop_ref[...]) + _dot(om_ref[0], wbot_ref[...])
    x1 = x_ref[0] + mod[2:3] * mix
    x1_ref[0] = x1
    h2 = _rms(x1, D_MODEL) * g2_ref[...] * (1.0 + mod[4:5]) + mod[3:4]
    h2pa_ref[0], h2pb_ref[0] = _pack_rows(h2)
    logits_t = _dot_nt(wr_ref[...], h2, precision=HIGHEST)
    slot8, w8, cnt = _route(logits_t, br_ref[...], cnt_scr[:, 0:1], cap)
    slot_ref[...] = slot8
    t = w8.shape[1]
    wb = lax.bitcast_convert_type(w8.astype(BF16).astype(F32), U32)
    wb = wb | (wb >> 16)
    w8_ref[0] = jnp.concatenate(
        [jnp.broadcast_to(wb[k:k + 1], (SC_LANES, t)) for k in range(TOP_K)], axis=0).T
    cnt_scr[...] = jnp.broadcast_to(cnt, cnt_scr.shape)
    cnt_ref[...] = cnt_scr[...]


def _outproj(og, om, x, mod, per_batch_mod, p, tm):
    b, l, d = x.shape
    nt = l // tm
    mod_map = (lambda bi, i: (bi, 0, 0)) if per_batch_mod else (lambda bi, i: (0, 0, 0))
    tok = lambda w: pl.BlockSpec((1, tm, w), lambda bi, i: (bi, i, 0))
    full = lambda a: pl.BlockSpec(a.shape, lambda bi, i: (0, 0))
    weights = [p['w_out_top'], p['w_out_bot'], p['g_norm2'], p['w_router_t'], p['b_router_col']]
    cnt_shape = (N_EXPERTS, LANES)
    return pl.pallas_call(
        functools.partial(_outproj_kernel, cap=_expert_cap(b * l)),
        grid=(b, nt),
        in_specs=[tok(GLA_VAL_W), tok(MLA_W), tok(d), pl.BlockSpec((1, 6, d), mod_map)]
                 + [full(w) for w in weights],
        out_specs=[tok(d), tok(PACK_W), tok(PACK_W),
                   pl.BlockSpec((TOP_K, tm), lambda bi, i: (0, bi * nt + i)),
                   tok(LANES), pl.BlockSpec(cnt_shape, lambda bi, i: (0, 0))],
        out_shape=[jax.ShapeDtypeStruct((b, l, d), F32), jax.ShapeDtypeStruct((b, l, PACK_W), U32),
                   jax.ShapeDtypeStruct((b, l, PACK_W), U32),
                   jax.ShapeDtypeStruct((TOP_K, b * l), I32),
                   jax.ShapeDtypeStruct((b, l, LANES), U32),
                   jax.ShapeDtypeStruct(cnt_shape, F32)],
        scratch_shapes=[pltpu.VMEM(cnt_shape, F32)],
        compiler_params=pltpu.CompilerParams(
            dimension_semantics=("arbitrary", "arbitrary"), vmem_limit_bytes=VMEM_LIMIT),
        name="outproj",
    )(og, om, x, mod, *weights)


def _sc_mesh():
    return plsc.VectorSubcoreMesh(core_axis_name="c", subcore_axis_name="s")


def _sc_scatter_rows(src, slot8, n_rows_out):
    t, d = src.shape
    nk = slot8.shape[0]

    @functools.partial(pl.kernel, out_type=jax.ShapeDtypeStruct((n_rows_out, d), src.dtype),
                       mesh=_sc_mesh(), scratch_types=[], name="dispatch")
    def run(src_hbm, slot_hbm, out_hbm):
        def body(x_vmem, i_vmem):
            for k in range(nk):
                pltpu.sync_copy(x_vmem, out_hbm.at[i_vmem.at[k]])

        pltpu.emit_pipeline(
            body, grid=(t // SC_WINDOW,),
            in_specs=[pl.BlockSpec((SC_WINDOW, d), lambda i: (i, 0)),
                      pl.BlockSpec((nk, SC_WINDOW), lambda i: (0, i))],
            out_specs=[], core_axis_name=("c", "s"),
            dimension_semantics=(pltpu.PARALLEL,))(src_hbm, slot_hbm)

    return run(src, slot8)


def _sc_collect_sum(table, slots_tk, w_rep):
    n_tok = w_rep.shape[0]
    d = table.shape[1]
    group = SC_WINDOW // TOP_K
    chunks = d // SC_LANES

    @functools.partial(pl.kernel, out_type=jax.ShapeDtypeStruct((n_tok, 2 * d), F32),
                       mesh=_sc_mesh(),
                       scratch_types=[pltpu.VMEM((SC_GATHERS, SC_WINDOW, d), table.dtype),
                                      pltpu.SemaphoreType.DMA((SC_GATHERS,))],
                       compiler_params=pltpu.CompilerParams(needs_layout_passes=False),
                       name="collect")
    def run(tab_hbm, slot_hbm, w_hbm, out_hbm, rows_v, sems):
        def gather(i_vmem, b):
            return pltpu.make_async_copy(tab_hbm.at[i_vmem.at[b]], rows_v.at[b], sems.at[b])

        def body(i_vmem, w_vmem, o_vmem):
            for b in range(SC_GATHERS):
                gather(i_vmem, b).start()
            for b in range(SC_GATHERS):
                gather(i_vmem, b).wait()

                @pl.loop(0, group)
                def _(j):
                    tok = b * group + j
                    wv = [plsc.bitcast(w_vmem[tok, pl.ds(k * SC_LANES, SC_LANES)], BF16)
                          for k in range(TOP_K)]

                    def chunk(off):
                        prods = [plsc.bitcast(
                            rows_v[b, j * TOP_K + k, pl.ds(off, SC_LANES)], BF16) * wv[k]
                            for k in range(TOP_K)]
                        lo = jnp.zeros((SC_LANES,), F32)
                        hi = jnp.zeros((SC_LANES,), F32)
                        for k in range(0, TOP_K, 2):
                            u = plsc.bitcast(prods[k] + prods[k + 1], U32)
                            lo = lo + lax.bitcast_convert_type(u << 16, F32)
                            hi = hi + lax.bitcast_convert_type(u & jnp.uint32(0xFFFF0000), F32)
                        o_vmem[tok, pl.ds(off, SC_LANES)] = lo
                        o_vmem[tok, pl.ds(d + off, SC_LANES)] = hi

                    @plsc.parallel_loop(0, chunks, unroll=2)
                    def _(c):
                        chunk(c * SC_LANES)

        step_tok = SC_GATHERS * group
        pltpu.emit_pipeline(
            body, grid=(n_tok // step_tok,),
            in_specs=[pl.BlockSpec((SC_GATHERS, SC_WINDOW), lambda i: (i, 0)),
                      pl.BlockSpec((step_tok, TOP_K * SC_LANES), lambda i: (i, 0))],
            out_specs=[pl.BlockSpec((step_tok, 2 * d), lambda i: (i, 0))],
            core_axis_name=("c", "s"),
            dimension_semantics=(pltpu.PARALLEL,))(slot_hbm, w_hbm, out_hbm)

    return run(table, slots_tk.reshape(-1, SC_WINDOW), w_rep)


def _ffn_kernel(*refs, cap, n_exp):
    for j in range(FFN_GROUP):
        _ffn_expert(pl.program_id(0) * FFN_GROUP + j, j, *refs, cap=cap, n_exp=n_exp)


def _ffn_expert(e, j, cnt_ref, nt_ref, first_ref, te_ref, tt_ref, xsa_hbm, xsb_hbm,
                wg_ref, wu_ref, wd_ref, ysa_hbm, ysb_hbm,
                xbuf, ybuf, wg_b, wu_b, wd_b, sem_in, sem_out, *, cap, n_exp):
    n = nt_ref[e]
    g0 = first_ref[e]
    total = first_ref[n_exp - 1] + nt_ref[n_exp - 1]
    xs_hbm = (xsa_hbm, xsb_hbm)
    ys_hbm = (ysa_hbm, ysb_hbm)
    n_in = FFN_AHEAD + 1
    half = FFN_TILE // 2

    def at_height(ex, t, fn):
        few = cnt_ref[ex] - t * FFN_TILE <= half

        @pl.when(few)
        def _():
            fn(half)

        @pl.when(jnp.logical_not(few))
        def _():
            fn(FFN_TILE)

    def rows(ex, t, n_rows):
        return pl.ds(pl.multiple_of(ex * cap + t * FFN_TILE, FFN_TILE), n_rows)

    def in_copy(ex, t, slot, part, n_rows):
        return pltpu.make_async_copy(xs_hbm[part].at[rows(ex, t, n_rows)],
                                     xbuf.at[slot, part, pl.ds(0, n_rows)],
                                     sem_in.at[slot, part])

    def out_copy(ex, t, slot, part, n_rows):
        return pltpu.make_async_copy(ybuf.at[slot, part, pl.ds(0, n_rows)],
                                     ys_hbm[part].at[rows(ex, t, n_rows)],
                                     sem_out.at[slot, part])

    def start_in(g):
        ex, t = te_ref[g], tt_ref[g]
        at_height(ex, t, lambda nr: [in_copy(ex, t, g % n_in, part, nr).start()
                                     for part in range(PACK_PARTS)])

    def wait_out(g):
        ex, t = te_ref[g], tt_ref[g]
        at_height(ex, t, lambda nr: [out_copy(ex, t, g % 2, part, nr).wait()
                                     for part in range(PACK_PARTS)])

    @pl.when(n > 0)
    def _():
        @pl.when(g0 == 0)
        def _():
            for ahead in range(FFN_AHEAD):
                @pl.when(ahead < total)
                def _():
                    start_in(ahead)

        wg_b[...] = wg_ref[j].astype(BF16)
        wu_b[...] = wu_ref[j].astype(BF16)
        wd_b[...] = wd_ref[j].astype(BF16)

        def tile(t, carry):
            g = g0 + t
            slot = g % 2
            slot_in = g % n_in
            at_height(e, t, lambda nr: [in_copy(e, t, slot_in, part, nr).wait()
                                        for part in range(PACK_PARTS)])

            @pl.when(g + FFN_AHEAD < total)
            def _():
                start_in(g + FFN_AHEAD)

            @pl.when(g >= 2)
            def _():
                wait_out(g - 2)

            def compute(n_rows):
                x = _unpack_rows([xbuf[slot_in, part, 0:n_rows, :]
                                  for part in range(PACK_PARTS)]).astype(BF16)
                a = _silu(_dot(x, wg_b[...])) * _dot(x, wu_b[...])
                y = _pack_rows(_dot(a.astype(BF16), wd_b[...]))
                for part in range(PACK_PARTS):
                    ybuf[slot, part, 0:n_rows, :] = y[part]
                    out_copy(e, t, slot, part, n_rows).start()

            at_height(e, t, compute)
            return carry

        lax.fori_loop(0, n, tile, 0)

        @pl.when(g0 + n == total)
        def _():
            @pl.when(total >= 2)
            def _():
                wait_out(total - 2)

            wait_out(total - 1)


def _ffn(xs_parts, counts, w_gate, w_up, w_down):
    n_exp, d = w_gate.shape[0], w_gate.shape[1]
    cap = xs_parts[0].shape[0] // n_exp
    ntiles = (counts + FFN_TILE - 1) // FFN_TILE
    cum = jnp.cumsum(ntiles)
    first = cum - ntiles
    max_tiles = (cap * TOP_K) // FFN_TILE + n_exp
    g = jnp.arange(max_tiles, dtype=I32)
    done = g[:, None] >= cum[None, :]
    tile_e = jnp.minimum(jnp.sum(done, axis=1), n_exp - 1).astype(I32)
    tile_t = (g - jnp.sum(jnp.where(done, ntiles[None, :], 0), axis=1)).astype(I32)
    wmap = lambda e, cnt, nt, first, te, tt: (e, 0, 0)
    hbm = pl.BlockSpec(memory_space=pl.ANY)
    return pl.pallas_call(
        functools.partial(_ffn_kernel, cap=cap, n_exp=n_exp),
        grid_spec=pltpu.PrefetchScalarGridSpec(
            num_scalar_prefetch=5, grid=(n_exp // FFN_GROUP,),
            in_specs=[hbm] * PACK_PARTS
                     + [pl.BlockSpec((FFN_GROUP, d, D_EXPERT), wmap),
                        pl.BlockSpec((FFN_GROUP, d, D_EXPERT), wmap),
                        pl.BlockSpec((FFN_GROUP, D_EXPERT, d), wmap)],
            out_specs=[hbm] * PACK_PARTS,
            scratch_shapes=[pltpu.VMEM((FFN_AHEAD + 1, PACK_PARTS, FFN_TILE, PACK_W), U32),
                            pltpu.VMEM((2, PACK_PARTS, FFN_TILE, PACK_W), U32),
                            pltpu.VMEM((d, D_EXPERT), BF16), pltpu.VMEM((d, D_EXPERT), BF16),
                            pltpu.VMEM((D_EXPERT, d), BF16),
                            pltpu.SemaphoreType.DMA((FFN_AHEAD + 1, PACK_PARTS)),
                            pltpu.SemaphoreType.DMA((2, PACK_PARTS))]),
        out_shape=[jax.ShapeDtypeStruct(xs_parts[0].shape, U32)] * PACK_PARTS,
        compiler_params=pltpu.CompilerParams(
            dimension_semantics=("arbitrary",), vmem_limit_bytes=VMEM_LIMIT),
        name="ffn",
    )(counts.astype(I32), ntiles.astype(I32), first.astype(I32), tile_e, tile_t, *xs_parts,
      w_gate, w_up, w_down)


def _combine_kernel(x1_ref, h2pa_ref, h2pb_ref, ra_ref, rb_ref, mod_ref,
                    wsg_ref, wsu_ref, wsd_ref, o_ref):
    h2 = _unpack_rows([h2pa_ref[0], h2pb_ref[0]]).astype(BF16)
    a = _silu(_dot(h2, wsg_ref[...])) * _dot(h2, wsu_ref[...])
    routed = jnp.concatenate([ra_ref[0], rb_ref[0]], axis=1)
    o_ref[0] = x1_ref[0] + mod_ref[0][5:6] * (_dot(a.astype(BF16), wsd_ref[...]) + routed)


def _combine(x1, h2p_parts, routed_parts, mod, per_batch_mod, p, tm):
    b, l, d = x1.shape
    mod_map = (lambda bi, i: (bi, 0, 0)) if per_batch_mod else (lambda bi, i: (0, 0, 0))
    tok = lambda w: pl.BlockSpec((1, tm, w), lambda bi, i: (bi, i, 0))
    full = lambda a: pl.BlockSpec(a.shape, lambda bi, i: (0, 0))
    weights = [p['w_sh_gate'], p['w_sh_up'], p['w_sh_down']]
    routed_parts = [r.reshape(b, l, d // PACK_PARTS) for r in routed_parts]
    return pl.pallas_call(
        _combine_kernel,
        grid=(b, l // tm),
        in_specs=[tok(d)] + [tok(PACK_W)] * PACK_PARTS + [tok(d // PACK_PARTS)] * PACK_PARTS
                 + [pl.BlockSpec((1, 6, d), mod_map)] + [full(w) for w in weights],
        out_specs=tok(d),
        out_shape=jax.ShapeDtypeStruct((b, l, d), F32),
        compiler_params=pltpu.CompilerParams(
            dimension_semantics=("parallel", "parallel"), vmem_limit_bytes=VMEM_LIMIT),
        name="combine",
    )(x1, *h2p_parts, *routed_parts, mod, *weights)


def _pad_heads(w, parts):
    k = w.shape[0]
    per = w.shape[1] // MLA_HEADS
    w = w.reshape(k, MLA_HEADS, per)[:, :, parts[0]:parts[1]]
    w = jnp.pad(w, ((0, 0), (0, 0), (0, HEAD_W - (parts[1] - parts[0]))))
    return w.reshape(k, MLA_HEADS * HEAD_W)


def _prep_params(l, g_norm1, g_norm2, w_in, w_gk_fwd, b_gk_fwd, w_gk_bwd, b_gk_bwd, g_gla_out,
                 g_q_lora, w_uq, g_kv_lora, w_ukv, g_qk_q, g_qk_k, w_out, w_router, b_router,
                 w_sh_gate, w_sh_up, w_sh_down):
    w = w_in[l]
    d = w.shape[0]
    o_lrf = COL_CQ
    o_lrb = o_lrf + GLA_GATE_RANK
    o_cq = o_lrb + GLA_GATE_RANK
    o_ckv = o_cq + Q_LORA
    o_kr = o_ckv + KV_LORA
    w_in_p = jnp.concatenate([
        w[:, 0:COL_CQ], w[:, o_cq:o_ckv], w[:, o_ckv:o_kr], jnp.zeros((d, KR_LANE0), w.dtype),
        w[:, o_kr:o_kr + ROPE_DIM], w[:, o_lrf:o_lrb], w[:, o_lrb:o_cq]], axis=1).astype(BF16)
    z = jnp.zeros((GLA_GATE_RANK, GLA_KEY_W), F32)
    w_gk_big = jnp.concatenate([
        jnp.zeros((LANES - 2 * GLA_GATE_RANK, 2 * GLA_KEY_W), F32),
        jnp.concatenate([w_gk_fwd[l], z], axis=1),
        jnp.concatenate([z, w_gk_bwd[l]], axis=1)], axis=0).astype(BF16)
    pad_gain = lambda g: jnp.pad(g, (0, HEAD_W - QK_DIM)).reshape(1, HEAD_W)
    w_ukv_h = w_ukv[l]
    return {
        'g_norm1': g_norm1[l].reshape(1, d), 'g_norm2': g_norm2[l].reshape(1, d),
        'w_in_p': w_in_p, 'w_gk_big': w_gk_big,
        'b_gk': jnp.concatenate([b_gk_fwd[l], b_gk_bwd[l]]).reshape(1, 2 * GLA_KEY_W),
        'g_gla_out': g_gla_out[l].reshape(1, GLA_DV),
        'g_q_lora': g_q_lora[l].reshape(1, Q_LORA),
        'w_uq_p': _pad_heads(w_uq[l], (0, QK_DIM)).astype(BF16),
        'g_kv_lora': g_kv_lora[l].reshape(1, KV_LORA),
        'w_ukv_p': jnp.concatenate([_pad_heads(w_ukv_h, (0, NOPE_DIM)),
                                    _pad_heads(w_ukv_h, (NOPE_DIM, NOPE_DIM + V_DIM))],
                                   axis=1).astype(BF16),
        'gq': pad_gain(g_qk_q[l]), 'gk': pad_gain(g_qk_k[l]),
        'w_out_top': w_out[l][:GLA_VAL_W].astype(BF16),
        'w_out_bot': w_out[l][GLA_VAL_W:].astype(BF16),
        'w_router_t': w_router[l].T, 'b_router_col': b_router[l].reshape(N_EXPERTS, 1),
        'w_sh_gate': w_sh_gate[l].astype(BF16), 'w_sh_up': w_sh_up[l].astype(BF16),
        'w_sh_down': w_sh_down[l].astype(BF16),
    }


def _rope_tables(length):
    pos = np.arange(length)
    r = (pos // GRID_W).astype(np.float32)
    c = (pos % GRID_W).astype(np.float32)
    half = ROPE_DIM // 4
    inv_freq = np.float32(ROPE_THETA) ** (-np.arange(half, dtype=np.float32) / np.float32(half))
    ang_r = r[:, None] * inv_freq[None, :]
    ang_c = c[:, None] * inv_freq[None, :]
    zeros = lambda w: np.zeros((length, w), np.float32)
    ones = lambda w: np.ones((length, w), np.float32)
    tail_w = HEAD_W - ROPE_LANE0 - ROPE_DIM
    cos = np.concatenate([ones(ROPE_LANE0), np.cos(ang_r), np.cos(ang_r), np.cos(ang_c),
                          np.cos(ang_c), ones(tail_w)], axis=1)
    s1 = np.concatenate([zeros(ROPE_LANE0), -np.sin(ang_r), zeros(half), -np.sin(ang_c),
                         zeros(half), zeros(tail_w)], axis=1)
    s2 = np.concatenate([zeros(ROPE_LANE0), zeros(half), np.sin(ang_r), zeros(half),
                         np.sin(ang_c), zeros(tail_w)], axis=1)
    return tuple(jnp.asarray(t, F32) for t in (cos, s1, s2))


def _mix_route_dispatch(x_tok, seq_shape, mod, per_batch_mod, p, s0f, s0b, rope_tabs, ctx_kv,
                        tq):
    bt, lt, d = x_tok.shape
    b, l = seq_shape
    n_tok = bt * lt
    (q, k, v, g, lgf, lgb, qm, km, vm, ckvn, kr) = _inproj(x_tok, mod, per_batch_mod, p,
                                                            rope_tabs, IN_TILE)
    seq = lambda a: a.reshape(b, l, a.shape[-1])
    og, sf, sb = _gla(seq(q), seq(k), seq(v), seq(g), seq(lgf), seq(lgb), s0f, s0b,
                      p['g_gla_out'], GLA_SEQS if l == GLA_BLOCK else 1)
    kv_pairs = [(seq(km), seq(vm))] + ([ctx_kv] if ctx_kv is not None else [])
    om = _attn(seq(qm), kv_pairs, tq, ATTN_SEQS if l == tq else 1)
    tokv = lambda a: a.reshape(bt, lt, a.shape[-1])
    x1, h2a, h2b, slot8, w8, cnt = _outproj(tokv(og), tokv(om), x_tok, mod, per_batch_mod, p,
                                            OUT_TILE)
    x_sorted = [_sc_scatter_rows(h.reshape(n_tok, PACK_W), slot8,
                                 N_EXPERTS * _expert_cap(n_tok)) for h in (h2a, h2b)]
    routed = dict(x1=x1, h2=(h2a, h2b), slot8=slot8, w8=w8, cnt=cnt, x_sorted=x_sorted)
    return routed, sf, sb, ckvn, kr


def _experts_collect(r, w_gate, w_up, w_down):
    n_tok = r['slot8'].shape[1]
    y_sorted = _ffn(r['x_sorted'], r['cnt'][:, 0].astype(I32), w_gate, w_up, w_down)
    slots_tk = r['slot8'].T.reshape(1, TOP_K * n_tok)
    w_rep = r['w8'].reshape(n_tok, LANES)
    return [_sc_collect_sum(y, slots_tk, w_rep) for y in y_sorted]


def kernel(x_prompt, x_sample, c, state_gla_fwd, state_gla_bwd, cache_mla_ckv, cache_mla_krope,
           c_ctx, w_ada, b_ada, g_norm1, g_norm2, w_in, w_gk_fwd, b_gk_fwd, w_gk_bwd, b_gk_bwd,
           g_gla_out, g_q_lora, w_uq, g_kv_lora, w_ukv, g_qk_q, g_qk_k, w_out,
           w_router, b_router, w_exp_gate, w_exp_up, w_exp_down, w_sh_gate, w_sh_up, w_sh_down):
    bp, lp, d = x_prompt.shape
    bs, ls, _ = x_sample.shape
    depth = w_ada.shape[0]
    xp = x_prompt.reshape(1, bp * lp, d)
    xs = x_sample
    new_f, new_b, new_ckv, new_kr = [], [], [], []
    lat_tabs = _rope_tables(ls)
    cvecs = jnp.concatenate([c_ctx[None], c, jnp.zeros((8 - 1 - bs, d), F32)], axis=0)
    for l in range(depth):
        p = _prep_params(l, g_norm1, g_norm2, w_in, w_gk_fwd, b_gk_fwd, w_gk_bwd, b_gk_bwd,
                         g_gla_out, g_q_lora, w_uq, g_kv_lora, w_ukv, g_qk_q, g_qk_k, w_out,
                         w_router, b_router, w_sh_gate, w_sh_up, w_sh_down)
        mod = _ada(cvecs, w_ada[l], b_ada[l]).reshape(8, 6, d)
        mod_p, mod_s = mod[0:1], mod[1:1 + bs]
        zeros = jnp.zeros((bp, GLA_KEY_W, GLA_DV), F32)
        r_ctx, sf, sb, ckvn, kr = _mix_route_dispatch(
            xp, (bp, lp), mod_p, False, p, zeros, zeros, (), None, lp)
        new_f.append(sf.reshape(bp, GLA_HEADS, GLA_DK, GLA_DV))
        new_b.append(sb.reshape(bp, GLA_HEADS, GLA_DK, GLA_DV))
        new_ckv.append(ckvn.reshape(bp, lp, KV_LORA))
        new_kr.append(kr.reshape(bp, lp, LANES)[:, :, KR_LANE0:KR_LANE0 + ROPE_DIM])
        kr_cache = jnp.pad(cache_mla_krope[:, l],
                           ((0, 0), (0, 0), (ROPE_LANE0, LANES - ROPE_LANE0 - ROPE_DIM)))
        ctx_kv = _cache_kv(cache_mla_ckv[:, l], kr_cache, p)
        s0f = state_gla_fwd[:, l].reshape(bs, GLA_KEY_W, GLA_DV)
        s0b = state_gla_bwd[:, l].reshape(bs, GLA_KEY_W, GLA_DV)
        r_lat, _, _, _, _ = _mix_route_dispatch(
            xs, (bs, ls), mod_s, True, p, s0f, s0b, lat_tabs, ctx_kv, ATTN_TILE)
        experts = (w_exp_gate[l], w_exp_up[l], w_exp_down[l])
        routed_ctx = _experts_collect(r_ctx, *experts)
        routed_lat = _experts_collect(r_lat, *experts)
        xp = _combine(r_ctx['x1'], r_ctx['h2'], routed_ctx, mod_p, False, p, OUT_TILE)
        xs = _combine(r_lat['x1'], r_lat['h2'], routed_lat, mod_s, True, p, OUT_TILE)
    return (xp.reshape(bp, lp, d), xs, jnp.stack(new_f, axis=1), jnp.stack(new_b, axis=1),
            jnp.stack(new_ckv, axis=1), jnp.stack(new_kr, axis=1))
```

```python
import functools

import jax
import jax.numpy as jnp
import numpy as np
from jax import lax
from jax.experimental import pallas as pl
from jax.experimental.pallas import tpu as pltpu
from jax.experimental.pallas import tpu_sc as plsc

F32 = jnp.float32
BF16 = jnp.bfloat16
I32 = jnp.int32
U32 = jnp.uint32

D_MODEL = 1024
EPS = 1e-6
GRID_W = 64
GLA_HEADS = 4
GLA_DK = 64
GLA_DV = 128
GLA_GATE_RANK = 16
GLA_GATE_NORM = 16.0
GLA_KEY_W = GLA_HEADS * GLA_DK
GLA_VAL_W = GLA_HEADS * GLA_DV
MLA_HEADS = 4
Q_LORA = 256
KV_LORA = 128
NOPE_DIM = 64
ROPE_DIM = 32
V_DIM = 128
QK_DIM = NOPE_DIM + ROPE_DIM
ROPE_THETA = 10000.0
N_EXPERTS = 64
TOP_K = 8
N_GROUPS = 8
TOPK_GROUPS = 4
D_EXPERT = 256
ROUTED_SCALE = 2.5

LANES = 128
HEAD_W = LANES
MLA_W = MLA_HEADS * HEAD_W
ROPE_LANE0 = NOPE_DIM
COL_V = 2 * GLA_KEY_W
COL_G = COL_V + GLA_VAL_W
COL_CQ = COL_G + GLA_VAL_W
COL_CKV = COL_CQ + Q_LORA
TAIL0 = COL_CKV + KV_LORA
IN_W = TAIL0 + LANES
KR_LANE0 = LANES - ROPE_DIM - 2 * GLA_GATE_RANK
GLA_BLOCK = 256
GLA_SUB = 64
GLA_SEQS = 2
ATTN_SEQS = 4
ATTN_TILE = 1024
ADA_TILE = 768
IN_TILE = 1024
IN_SUB = 256
OUT_TILE = 1024
FFN_TILE = 512
FFN_AHEAD = 5
FFN_GROUP = 2
PACK_PARTS = 2
PACK_W = D_MODEL // (2 * PACK_PARTS)
SC_WINDOW = 128
SC_LANES = 16
SC_GATHERS = 2
VMEM_LIMIT = 56 * 1024 * 1024

HIGHEST = lax.Precision.HIGHEST


def _dot(a, b, precision=None):
    return jnp.dot(a, b, preferred_element_type=F32, precision=precision)


def _dot_nt(a, b, precision=None):
    return lax.dot_general(a, b, (((1,), (1,)), ((), ())), preferred_element_type=F32,
                           precision=precision)


def _rms(x, width):
    ss = jnp.sum(x * x, axis=-1, keepdims=True) * (1.0 / width)
    return x * lax.rsqrt(ss + EPS)


def _silu(x):
    return x * jax.nn.sigmoid(x)


def _log_sigmoid(x):
    return jnp.minimum(x, 0.0) - jnp.log1p(jnp.exp(-jnp.abs(x)))


def _pack_rows(x):
    parts = []
    for i in range(PACK_PARTS):
        c0 = i * 2 * PACK_W
        lo = lax.bitcast_convert_type(x[:, c0:c0 + PACK_W].astype(BF16).astype(F32), U32)
        hi = lax.bitcast_convert_type(
            x[:, c0 + PACK_W:c0 + 2 * PACK_W].astype(BF16).astype(F32), U32)
        parts.append(hi | (lo >> 16))
    return parts


def _unpack_rows(parts):
    cols = []
    for w in parts:
        cols.append(lax.bitcast_convert_type(w << 16, F32))
        cols.append(lax.bitcast_convert_type(w & jnp.uint32(0xFFFF0000), F32))
    return jnp.concatenate(cols, axis=1)


def _ada_kernel(c_ref, w_ref, b_ref, o_ref):
    o_ref[...] = _dot(_silu(c_ref[...]), w_ref[...], precision=HIGHEST) + b_ref[...]


def _ada(cvecs, w_ada, b_ada):
    n = w_ada.shape[1]
    tn = ADA_TILE
    return pl.pallas_call(
        _ada_kernel,
        grid=(n // tn,),
        in_specs=[pl.BlockSpec((8, D_MODEL), lambda j: (0, 0)),
                  pl.BlockSpec((D_MODEL, tn), lambda j: (0, j)),
                  pl.BlockSpec((1, tn), lambda j: (0, j))],
        out_specs=pl.BlockSpec((8, tn), lambda j: (0, j)),
        out_shape=jax.ShapeDtypeStruct((8, n), F32),
        name="ada",
    )(cvecs, w_ada, b_ada.reshape(1, n))


def _rope(x, c, s1, s2):
    return x * c + pltpu.roll(x, LANES - 8, 1) * s1 + pltpu.roll(x, 8, 1) * s2


def _mla_kv(ckv_n, kr, w_ukv_ref, gk, rope_tabs, k_ref, v_ref, rs=slice(None)):
    kv = _dot(ckv_n.astype(BF16), w_ukv_ref[...])
    for h in range(MLA_HEADS):
        k_h = kv[:, h * HEAD_W:(h + 1) * HEAD_W] + kr
        k_h = _rms(k_h, QK_DIM) * gk
        if rope_tabs is not None:
            k_h = _rope(k_h, *rope_tabs)
        k_ref[0, rs, h * HEAD_W:(h + 1) * HEAD_W] = k_h.astype(BF16)
    v_ref[0, rs, :] = kv[:, MLA_W:].astype(BF16)


def _lane_mask(lo, hi, rows):
    lane = lax.broadcasted_iota(I32, (rows, LANES), 1)
    return (lane >= lo) & (lane < hi)


def _inproj_kernel(x_ref, mod_ref, g1_ref, win_ref, wgk_ref, bgk_ref, gql_ref, wuq_ref,
                   gkv_ref, wukv_ref, gq_ref, gk_ref, *refs, positions):
    tab_refs, outs = (refs[:3], refs[3:]) if positions else ((), refs)
    (q_ref, k_ref, v_ref, g_ref, lgf_ref, lgb_ref, qm_ref, km_ref, vm_ref, ckvn_ref,
     kr_ref) = outs
    mod = mod_ref[0]
    for r0 in range(0, x_ref.shape[1], IN_SUB):
        rs = slice(r0, r0 + IN_SUB)
        x = x_ref[0, rs, :]
        h = _rms(x, D_MODEL) * g1_ref[...] * (1.0 + mod[1:2]) + mod[0:1]
        y = _dot(h.astype(BF16), win_ref[...])
        q_ref[0, rs, :] = (y[:, 0:GLA_KEY_W] * GLA_DK ** -0.5).astype(BF16)
        k_ref[0, rs, :] = y[:, GLA_KEY_W:COL_V].astype(BF16)
        v_ref[0, rs, :] = y[:, COL_V:COL_G].astype(BF16)
        g_ref[0, rs, :] = y[:, COL_G:COL_CQ].astype(BF16)
        tail = y[:, TAIL0:IN_W]
        pre = _dot(tail.astype(BF16), wgk_ref[...]) + bgk_ref[...]
        logg = _log_sigmoid(pre) * (1.0 / GLA_GATE_NORM)
        lgf_ref[0, rs, :] = logg[:, 0:GLA_KEY_W]
        lgb_ref[0, rs, :] = logg[:, GLA_KEY_W:]
        tabs = tuple(r[rs, :] for r in tab_refs) if positions else None
        cq = _rms(y[:, COL_CQ:COL_CKV], Q_LORA) * gql_ref[...]
        qm = _dot(cq.astype(BF16), wuq_ref[...])
        gq = gq_ref[...]
        for hh in range(MLA_HEADS):
            q_h = _rms(qm[:, hh * HEAD_W:(hh + 1) * HEAD_W], QK_DIM) * gq
            if positions:
                q_h = _rope(q_h, *tabs)
            qm_ref[0, rs, hh * HEAD_W:(hh + 1) * HEAD_W] = (q_h * QK_DIM ** -0.5).astype(BF16)
        ckv_n = _rms(y[:, COL_CKV:TAIL0], KV_LORA) * gkv_ref[...]
        ckvn_ref[0, rs, :] = ckv_n
        kr_ref[0, rs, :] = tail
        kr = jnp.where(_lane_mask(KR_LANE0, KR_LANE0 + ROPE_DIM, IN_SUB), tail, 0.0)
        _mla_kv(ckv_n, kr, wukv_ref, gk_ref[...], tabs, km_ref, vm_ref, rs)


def _inproj(x, mod, per_batch_mod, p, rope_tabs, tm):
    b, l, d = x.shape
    nt = l // tm
    mod_map = (lambda bi, i: (bi, 0, 0)) if per_batch_mod else (lambda bi, i: (0, 0, 0))
    tab_map = lambda bi, i: (i, 0)
    const = lambda bi, i: (0, 0)
    tok = lambda w: pl.BlockSpec((1, tm, w), lambda bi, i: (bi, i, 0))
    full = lambda a: pl.BlockSpec(a.shape, const)
    weights = [p['g_norm1'], p['w_in_p'], p['w_gk_big'], p['b_gk'], p['g_q_lora'], p['w_uq_p'],
               p['g_kv_lora'], p['w_ukv_p'], p['gq'], p['gk']]
    outs = [(GLA_KEY_W, BF16), (GLA_KEY_W, BF16), (GLA_VAL_W, BF16), (GLA_VAL_W, BF16),
            (GLA_KEY_W, F32), (GLA_KEY_W, F32), (MLA_W, BF16), (MLA_W, BF16), (MLA_W, BF16),
            (KV_LORA, F32), (LANES, F32)]
    return pl.pallas_call(
        functools.partial(_inproj_kernel, positions=bool(rope_tabs)),
        grid=(b, nt),
        in_specs=[tok(d), pl.BlockSpec((1, 6, d), mod_map)] + [full(w) for w in weights]
                 + [pl.BlockSpec((tm, LANES), tab_map)] * len(rope_tabs),
        out_specs=[tok(w) for w, _ in outs],
        out_shape=[jax.ShapeDtypeStruct((b, l, w), dt) for w, dt in outs],
        compiler_params=pltpu.CompilerParams(
            dimension_semantics=("parallel", "parallel"), vmem_limit_bytes=VMEM_LIMIT),
        name="inproj",
    )(x, mod, *weights, *rope_tabs)


def _cache_kv_kernel(ckv_ref, kr_ref, wukv_ref, gk_ref, k_ref, v_ref):
    _mla_kv(ckv_ref[0], kr_ref[0], wukv_ref, gk_ref[...], None, k_ref, v_ref)


def _cache_kv(ckv, kr128, p):
    b, l, _ = ckv.shape
    tok = lambda w: pl.BlockSpec((1, l, w), lambda bi: (bi, 0, 0))
    full = lambda a: pl.BlockSpec(a.shape, lambda bi: (0, 0))
    return pl.pallas_call(
        _cache_kv_kernel,
        grid=(b,),
        in_specs=[tok(KV_LORA), tok(LANES), full(p['w_ukv_p']), full(p['gk'])],
        out_specs=[tok(MLA_W), tok(MLA_W)],
        out_shape=[jax.ShapeDtypeStruct((b, l, MLA_W), BF16)] * 2,
        name="cache_kv",
    )(ckv, kr128, p['w_ukv_p'], p['gk'])


def _split3(x):
    hi = x.astype(BF16)
    r1 = x - hi.astype(F32)
    mid = r1.astype(BF16)
    lo = (r1 - mid.astype(F32)).astype(BF16)
    return hi, mid, lo


def _gla_block(q, k, v, lg, s, fwd):
    n = GLA_BLOCK
    nc = n // GLA_SUB
    row = lax.broadcasted_iota(I32, (n, n), 0)
    col = lax.broadcasted_iota(I32, (n, n), 1)
    same = (row // GLA_SUB) == (col // GLA_SUB)
    causal = same & ((col <= row) if fwd else (col >= row))
    tri = causal.astype(BF16)
    hi, mid, lo = _split3(lg)
    cum = _dot(tri, hi) + _dot(tri, mid) + _dot(tri, lo)
    tot_rows, mid_rows = [], []
    for c in range(nc):
        r_tot = c * GLA_SUB + (GLA_SUB - 1 if fwd else 0)
        r_mid = c * GLA_SUB + GLA_SUB // 2
        tot_rows.append(jnp.broadcast_to(cum[r_tot:r_tot + 1], (GLA_SUB, GLA_KEY_W)))
        mid_rows.append(jnp.broadcast_to(cum[r_mid:r_mid + 1], (GLA_SUB, GLA_KEY_W)))
    tot_rows = jnp.concatenate(tot_rows, axis=0)
    mid_rows = jnp.concatenate(mid_rows, axis=0)
    rel = cum - mid_rows
    qi = q * jnp.exp(rel)
    ki = (k * jnp.exp(-rel)).astype(BF16)
    q_in = (q * jnp.exp(cum)).astype(BF16)
    k_up = k * jnp.exp(tot_rows - cum)
    dec = jnp.exp(tot_rows)
    k_up_t = k_up.T.astype(BF16)
    dec_t = dec.T
    o_heads = []
    for h in range(GLA_HEADS):
        hk = slice(h * GLA_DK, (h + 1) * GLA_DK)
        a = _dot_nt(qi[:, hk].astype(BF16), ki[:, hk])
        a = jnp.where(causal, a, 0.0).astype(BF16)
        o_heads.append(_dot(a, v[:, h * GLA_DV:(h + 1) * GLA_DV]))
    o = jnp.concatenate(o_heads, axis=1)
    o_inter = [None] * nc
    for c in (range(nc) if fwd else range(nc - 1, -1, -1)):
        rc = slice(c * GLA_SUB, (c + 1) * GLA_SUB)
        s_b = s.astype(BF16)
        o_inter[c] = jnp.concatenate(
            [_dot(q_in[rc, h * GLA_DK:(h + 1) * GLA_DK], s_b[h * GLA_DK:(h + 1) * GLA_DK])
             for h in range(GLA_HEADS)], axis=1)
        u = jnp.concatenate(
            [_dot(k_up_t[h * GLA_DK:(h + 1) * GLA_DK, rc], v[rc, h * GLA_DV:(h + 1) * GLA_DV])
             for h in range(GLA_HEADS)], axis=0)
        s = dec_t[:, c * GLA_SUB:c * GLA_SUB + 1] * s + u
    return o + jnp.concatenate(o_inter, axis=0), s


def _gla_kernel(q_ref, k_ref, v_ref, g_ref, lgf_ref, lgb_ref, s0f_ref, s0b_ref, gout_ref,
                o_ref, sf_ref, sb_ref, acc_ref, st_ref, *, nblk, n_seq):
    for si in range(n_seq):
        _gla_sequence(si, q_ref, k_ref, v_ref, g_ref, lgf_ref, lgb_ref, s0f_ref, s0b_ref,
                      gout_ref, o_ref, sf_ref, sb_ref, acc_ref, st_ref, nblk)


def _gla_sequence(si, q_ref, k_ref, v_ref, g_ref, lgf_ref, lgb_ref, s0f_ref, s0b_ref, gout_ref,
                  o_ref, sf_ref, sb_ref, acc_ref, st_ref, nblk):
    st_ref[si, 0] = s0f_ref[si]
    st_ref[si, 1] = s0b_ref[si]

    def load(blk):
        start = blk * GLA_BLOCK
        r = pl.ds(start if isinstance(blk, int) else pl.multiple_of(start, GLA_BLOCK), GLA_BLOCK)
        return r, q_ref[si, r, :].astype(F32), k_ref[si, r, :].astype(F32), v_ref[si, r, :]

    def pair_step(i, first_touch):
        rf, qf, kf, vf = load(i)
        rb, qb, kb, vb = load(nblk - 1 - i)
        of, s_f = _gla_block(qf, kf, vf, lgf_ref[si, rf, :], st_ref[si, 0], True)
        ob, s_b = _gla_block(qb, kb, vb, lgb_ref[si, rb, :], st_ref[si, 1], False)
        if first_touch:
            acc_ref[si, rf, :] = of
            acc_ref[si, rb, :] = ob
        else:
            acc_ref[si, rf, :] += of
            acc_ref[si, rb, :] += ob
        st_ref[si, 0] = s_f
        st_ref[si, 1] = s_b

    if nblk == 1:
        r, q, k, v = load(0)
        of, s_f = _gla_block(q, k, v, lgf_ref[si, r, :], st_ref[si, 0], True)
        ob, s_b = _gla_block(q, k, v, lgb_ref[si, r, :], st_ref[si, 1], False)
        acc_ref[si, r, :] = of + ob
        st_ref[si, 0] = s_f
        st_ref[si, 1] = s_b
    else:
        half = nblk // 2
        lax.fori_loop(0, half, lambda i, c: (pair_step(i, True), c)[1], 0)
        lax.fori_loop(half, nblk, lambda i, c: (pair_step(i, False), c)[1], 0)
    sf_ref[si] = st_ref[si, 0]
    sb_ref[si] = st_ref[si, 1]

    def fin_step(blk, carry):
        r = pl.ds(pl.multiple_of(blk * GLA_BLOCK, GLA_BLOCK), GLA_BLOCK)
        o = acc_ref[si, r, :]
        gate = _silu(g_ref[si, r, :].astype(F32))
        for h in range(GLA_HEADS):
            sl = slice(h * GLA_DV, (h + 1) * GLA_DV)
            o_ref[si, r, sl] = (_rms(o[:, sl], GLA_DV) * gout_ref[...] * gate[:, sl]).astype(BF16)
        return carry

    lax.fori_loop(0, nblk, fin_step, 0)


def _gla(q, k, v, g, lgf, lgb, s0f, s0b, g_out, n_seq):
    b, l, _ = q.shape
    assert l == GLA_BLOCK or l % (2 * GLA_BLOCK) == 0
    seq = lambda w: pl.BlockSpec((n_seq, l, w), lambda bi: (bi, 0, 0))
    st = pl.BlockSpec((n_seq, GLA_KEY_W, GLA_DV), lambda bi: (bi, 0, 0))
    return pl.pallas_call(
        functools.partial(_gla_kernel, nblk=l // GLA_BLOCK, n_seq=n_seq),
        grid=(b // n_seq,),
        in_specs=[seq(GLA_KEY_W), seq(GLA_KEY_W), seq(GLA_VAL_W), seq(GLA_VAL_W),
                  seq(GLA_KEY_W), seq(GLA_KEY_W), st, st,
                  pl.BlockSpec((1, GLA_DV), lambda bi: (0, 0))],
        out_specs=[seq(GLA_VAL_W), st, st],
        out_shape=[jax.ShapeDtypeStruct((b, l, GLA_VAL_W), BF16),
                   jax.ShapeDtypeStruct((b, GLA_KEY_W, GLA_DV), F32),
                   jax.ShapeDtypeStruct((b, GLA_KEY_W, GLA_DV), F32)],
        scratch_shapes=[pltpu.VMEM((n_seq, l, GLA_VAL_W), F32),
                        pltpu.VMEM((n_seq, 2, GLA_KEY_W, GLA_DV), F32)],
        compiler_params=pltpu.CompilerParams(
            dimension_semantics=("parallel",), vmem_limit_bytes=VMEM_LIMIT),
        name="gla",
    )(q, k, v, g, lgf, lgb, s0f, s0b, g_out)


def _attn_kernel(*refs, n_kv, n_seq):
    q_ref, o_ref = refs[0], refs[-1]
    kv = [(refs[1 + 2 * i], refs[2 + 2 * i]) for i in range(n_kv)]
    for si in range(n_seq):
        for h in range(MLA_HEADS):
            sl = slice(h * HEAD_W, (h + 1) * HEAD_W)
            q = q_ref[si, :, sl]
            scores = [_dot_nt(q, k_ref[si, :, sl]) for k_ref, _ in kv]
            m = functools.reduce(jnp.maximum,
                                 [jnp.max(s, axis=-1, keepdims=True) for s in scores])
            o, den = 0.0, 0.0
            for s, (_, v_ref) in zip(scores, kv):
                p = jnp.exp(s - m)
                den = den + jnp.sum(p, axis=-1, keepdims=True)
                o = o + _dot(p.astype(BF16), v_ref[si, :, sl])
            o_ref[si, :, sl] = (o / den).astype(BF16)


def _attn(q, kv_pairs, tq, n_seq):
    b, l, _ = q.shape
    kv_specs, kv_args = [], []
    for k, v in kv_pairs:
        spec = pl.BlockSpec((n_seq, k.shape[1], MLA_W), lambda bi, i: (bi, 0, 0))
        kv_specs += [spec, spec]
        kv_args += [k, v]
    return pl.pallas_call(
        functools.partial(_attn_kernel, n_kv=len(kv_pairs), n_seq=n_seq),
        grid=(b // n_seq, l // tq),
        in_specs=[pl.BlockSpec((n_seq, tq, MLA_W), lambda bi, i: (bi, i, 0))] + kv_specs,
        out_specs=pl.BlockSpec((n_seq, tq, MLA_W), lambda bi, i: (bi, i, 0)),
        out_shape=jax.ShapeDtypeStruct((b, l, MLA_W), BF16),
        compiler_params=pltpu.CompilerParams(
            dimension_semantics=("parallel", "parallel"), vmem_limit_bytes=VMEM_LIMIT),
        name="attn",
    )(q, *kv_args)


def _expert_cap(n_tok):
    return -(-n_tok // FFN_TILE) * FFN_TILE


def _sum01(x):
    return jnp.sum(jnp.sum(x, axis=1, keepdims=True), axis=0, keepdims=True)


def _route(logits_t, bias_col, cnt_col, cap):
    t = logits_t.shape[1]
    gsz = N_EXPERTS // N_GROUPS
    scores = jax.nn.sigmoid(logits_t)
    sel = (scores + bias_col).reshape(N_GROUPS, gsz, t)
    scores = scores.reshape(N_GROUPS, gsz, t)
    neg = jnp.float32(-jnp.inf)
    ie = lax.broadcasted_iota(I32, (N_GROUPS, gsz, t), 1)
    ig = lax.broadcasted_iota(I32, (N_GROUPS, gsz, t), 0)
    m1 = jnp.max(sel, axis=1, keepdims=True)
    first = jnp.min(jnp.where(sel == m1, ie, gsz), axis=1, keepdims=True)
    m2 = jnp.max(jnp.where(ie == first, neg, sel), axis=1, keepdims=True)
    grp = m1 + m2
    igk = lax.broadcasted_iota(I32, (N_GROUPS, 1, t), 0)
    g_sel = jnp.zeros((N_GROUPS, 1, t), jnp.bool_)
    cur = grp
    for _ in range(TOPK_GROUPS):
        m = jnp.max(cur, axis=0, keepdims=True)
        pick = igk == jnp.min(jnp.where(cur == m, igk, N_GROUPS), axis=0, keepdims=True)
        g_sel = g_sel | pick
        cur = jnp.where(pick, neg, cur)
    cur = jnp.where(g_sel, sel, neg)
    idx = ig * gsz + ie
    e_sel = jnp.zeros((N_GROUPS, gsz, t), jnp.bool_)
    picks = []
    for _ in range(TOP_K):
        m = jnp.max(jnp.max(cur, axis=1, keepdims=True), axis=0, keepdims=True)
        cand = jnp.where(cur == m, idx, N_EXPERTS)
        pick = idx == jnp.min(jnp.min(cand, axis=1, keepdims=True), axis=0, keepdims=True)
        picks.append(pick)
        e_sel = e_sel | pick
        cur = jnp.where(pick, neg, cur)
    w = jnp.where(e_sel, scores, 0.0)
    gate = w / _sum01(w) * ROUTED_SCALE
    sel_f = e_sel.astype(F32).reshape(N_EXPERTS, t)
    earlier = (lax.broadcasted_iota(I32, (t, t), 0) < lax.broadcasted_iota(I32, (t, t), 1))
    rank = _dot(sel_f.astype(BF16), earlier.astype(BF16))
    base = lax.broadcasted_iota(I32, (N_EXPERTS, 1), 0).astype(F32) * float(cap) + cnt_col
    slot = (base + rank).reshape(N_GROUPS, gsz, t)
    slot8 = jnp.concatenate([_sum01(jnp.where(pk, slot, 0.0)).reshape(1, t) for pk in picks],
                            axis=0).astype(I32)
    w8 = jnp.concatenate([_sum01(jnp.where(pk, gate, 0.0)).reshape(1, t) for pk in picks], axis=0)
    return slot8, w8, cnt_col + jnp.sum(sel_f, axis=1, keepdims=True)


def _outproj_kernel(og_ref, om_ref, x_ref, mod_ref, wtop_ref, wbot_ref, g2_ref, wr_ref, br_ref,
                    x1_ref, h2pa_ref, h2pb_ref, slot_ref, w8_ref, cnt_ref, cnt_scr, *, cap):
    @pl.when((pl.program_id(0) == 0) & (pl.program_id(1) == 0))
    def _():
        cnt_scr[...] = jnp.zeros_like(cnt_scr)

    mod = mod_ref[0]
    mix = _dot(og_ref[0], wtop_ref[...]) + _dot(om_ref[0], wbot_ref[...])
    x1 = x_ref[0] + mod[2:3] * mix
    x1_ref[0] = x1
    h2 = _rms(x1, D_MODEL) * g2_ref[...] * (1.0 + mod[4:5]) + mod[3:4]
    h2pa_ref[0], h2pb_ref[0] = _pack_rows(h2)
    logits_t = _dot_nt(wr_ref[...], h2, precision=HIGHEST)
    slot8, w8, cnt = _route(logits_t, br_ref[...], cnt_scr[:, 0:1], cap)
    slot_ref[...] = slot8
    t = w8.shape[1]
    wb = lax.bitcast_convert_type(w8.astype(BF16).astype(F32), U32)
    wb = wb | (wb >> 16)
    w8_ref[0] = jnp.concatenate(
        [jnp.broadcast_to(wb[k:k + 1], (SC_LANES, t)) for k in range(TOP_K)], axis=0).T
    cnt_scr[...] = jnp.broadcast_to(cnt, cnt_scr.shape)
    cnt_ref[...] = cnt_scr[...]


def _outproj(og, om, x, mod, per_batch_mod, p, tm):
    b, l, d = x.shape
    nt = l // tm
    mod_map = (lambda bi, i: (bi, 0, 0)) if per_batch_mod else (lambda bi, i: (0, 0, 0))
    tok = lambda w: pl.BlockSpec((1, tm, w), lambda bi, i: (bi, i, 0))
    full = lambda a: pl.BlockSpec(a.shape, lambda bi, i: (0, 0))
    weights = [p['w_out_top'], p['w_out_bot'], p['g_norm2'], p['w_router_t'], p['b_router_col']]
    cnt_shape = (N_EXPERTS, LANES)
    return pl.pallas_call(
        functools.partial(_outproj_kernel, cap=_expert_cap(b * l)),
        grid=(b, nt),
        in_specs=[tok(GLA_VAL_W), tok(MLA_W), tok(d), pl.BlockSpec((1, 6, d), mod_map)]
                 + [full(w) for w in weights],
        out_specs=[tok(d), tok(PACK_W), tok(PACK_W),
                   pl.BlockSpec((TOP_K, tm), lambda bi, i: (0, bi * nt + i)),
                   tok(LANES), pl.BlockSpec(cnt_shape, lambda bi, i: (0, 0))],
        out_shape=[jax.ShapeDtypeStruct((b, l, d), F32), jax.ShapeDtypeStruct((b, l, PACK_W), U32),
                   jax.ShapeDtypeStruct((b, l, PACK_W), U32),
                   jax.ShapeDtypeStruct((TOP_K, b * l), I32),
                   jax.ShapeDtypeStruct((b, l, LANES), U32),
                   jax.ShapeDtypeStruct(cnt_shape, F32)],
        scratch_shapes=[pltpu.VMEM(cnt_shape, F32)],
        compiler_params=pltpu.CompilerParams(
            dimension_semantics=("arbitrary", "arbitrary"), vmem_limit_bytes=VMEM_LIMIT),
        name="outproj",
    )(og, om, x, mod, *weights)


def _sc_mesh():
    return plsc.VectorSubcoreMesh(core_axis_name="c", subcore_axis_name="s")


def _sc_scatter_rows(src, slot8, n_rows_out):
    t, d = src.shape
    nk = slot8.shape[0]

    @functools.partial(pl.kernel, out_type=jax.ShapeDtypeStruct((n_rows_out, d), src.dtype),
                       mesh=_sc_mesh(), scratch_types=[], name="dispatch")
    def run(src_hbm, slot_hbm, out_hbm):
        def body(x_vmem, i_vmem):
            for k in range(nk):
                pltpu.sync_copy(x_vmem, out_hbm.at[i_vmem.at[k]])

        pltpu.emit_pipeline(
            body, grid=(t // SC_WINDOW,),
            in_specs=[pl.BlockSpec((SC_WINDOW, d), lambda i: (i, 0)),
                      pl.BlockSpec((nk, SC_WINDOW), lambda i: (0, i))],
            out_specs=[], core_axis_name=("c", "s"),
            dimension_semantics=(pltpu.PARALLEL,))(src_hbm, slot_hbm)

    return run(src, slot8)


def _sc_collect_sum(table, slots_tk, w_rep):
    n_tok = w_rep.shape[0]
    d = table.shape[1]
    group = SC_WINDOW // TOP_K
    chunks = d // SC_LANES

    @functools.partial(pl.kernel, out_type=jax.ShapeDtypeStruct((n_tok, 2 * d), F32),
                       mesh=_sc_mesh(),
                       scratch_types=[pltpu.VMEM((SC_GATHERS, SC_WINDOW, d), table.dtype),
                                      pltpu.SemaphoreType.DMA((SC_GATHERS,))],
                       compiler_params=pltpu.CompilerParams(needs_layout_passes=False),
                       name="collect")
    def run(tab_hbm, slot_hbm, w_hbm, out_hbm, rows_v, sems):
        def gather(i_vmem, b):
            return pltpu.make_async_copy(tab_hbm.at[i_vmem.at[b]], rows_v.at[b], sems.at[b])

        def body(i_vmem, w_vmem, o_vmem):
            for b in range(SC_GATHERS):
                gather(i_vmem, b).start()
            for b in range(SC_GATHERS):
                gather(i_vmem, b).wait()

                @pl.loop(0, group)
                def _(j):
                    tok = b * group + j
                    wv = [plsc.bitcast(w_vmem[tok, pl.ds(k * SC_LANES, SC_LANES)], BF16)
                          for k in range(TOP_K)]

                    def chunk(off):
                        prods = [plsc.bitcast(
                            rows_v[b, j * TOP_K + k, pl.ds(off, SC_LANES)], BF16) * wv[k]
                            for k in range(TOP_K)]
                        lo = jnp.zeros((SC_LANES,), F32)
                        hi = jnp.zeros((SC_LANES,), F32)
                        for k in range(0, TOP_K, 2):
                            u = plsc.bitcast(prods[k] + prods[k + 1], U32)
                            lo = lo + lax.bitcast_convert_type(u << 16, F32)
                            hi = hi + lax.bitcast_convert_type(u & jnp.uint32(0xFFFF0000), F32)
                        o_vmem[tok, pl.ds(off, SC_LANES)] = lo
                        o_vmem[tok, pl.ds(d + off, SC_LANES)] = hi

                    @plsc.parallel_loop(0, chunks, unroll=2)
                    def _(c):
                        chunk(c * SC_LANES)

        step_tok = SC_GATHERS * group
        pltpu.emit_pipeline(
            body, grid=(n_tok // step_tok,),
            in_specs=[pl.BlockSpec((SC_GATHERS, SC_WINDOW), lambda i: (i, 0)),
                      pl.BlockSpec((step_tok, TOP_K * SC_LANES), lambda i: (i, 0))],
            out_specs=[pl.BlockSpec((step_tok, 2 * d), lambda i: (i, 0))],
            core_axis_name=("c", "s"),
            dimension_semantics=(pltpu.PARALLEL,))(slot_hbm, w_hbm, out_hbm)

    return run(table, slots_tk.reshape(-1, SC_WINDOW), w_rep)


def _ffn_kernel(*refs, cap, n_exp):
    for j in range(FFN_GROUP):
        _ffn_expert(pl.program_id(0) * FFN_GROUP + j, j, *refs, cap=cap, n_exp=n_exp)


def _ffn_expert(e, j, cnt_ref, nt_ref, first_ref, te_ref, tt_ref, xsa_hbm, xsb_hbm,
                wg_ref, wu_ref, wd_ref, ysa_hbm, ysb_hbm,
                xbuf, ybuf, wg_b, wu_b, wd_b, sem_in, sem_out, *, cap, n_exp):
    n = nt_ref[e]
    g0 = first_ref[e]
    total = first_ref[n_exp - 1] + nt_ref[n_exp - 1]
    xs_hbm = (xsa_hbm, xsb_hbm)
    ys_hbm = (ysa_hbm, ysb_hbm)
    n_in = FFN_AHEAD + 1
    half = FFN_TILE // 2

    def at_height(ex, t, fn):
        few = cnt_ref[ex] - t * FFN_TILE <= half

        @pl.when(few)
        def _():
            fn(half)

        @pl.when(jnp.logical_not(few))
        def _():
            fn(FFN_TILE)

    def rows(ex, t, n_rows):
        return pl.ds(pl.multiple_of(ex * cap + t * FFN_TILE, FFN_TILE), n_rows)

    def in_copy(ex, t, slot, part, n_rows):
        return pltpu.make_async_copy(xs_hbm[part].at[rows(ex, t, n_rows)],
                                     xbuf.at[slot, part, pl.ds(0, n_rows)],
                                     sem_in.at[slot, part])

    def out_copy(ex, t, slot, part, n_rows):
        return pltpu.make_async_copy(ybuf.at[slot, part, pl.ds(0, n_rows)],
                                     ys_hbm[part].at[rows(ex, t, n_rows)],
                                     sem_out.at[slot, part])

    def start_in(g):
        ex, t = te_ref[g], tt_ref[g]
        at_height(ex, t, lambda nr: [in_copy(ex, t, g % n_in, part, nr).start()
                                     for part in range(PACK_PARTS)])

    def wait_out(g):
        ex, t = te_ref[g], tt_ref[g]
        at_height(ex, t, lambda nr: [out_copy(ex, t, g % 2, part, nr).wait()
                                     for part in range(PACK_PARTS)])

    @pl.when(n > 0)
    def _():
        @pl.when(g0 == 0)
        def _():
            for ahead in range(FFN_AHEAD):
                @pl.when(ahead < total)
                def _():
                    start_in(ahead)

        wg_b[...] = wg_ref[j].astype(BF16)
        wu_b[...] = wu_ref[j].astype(BF16)
        wd_b[...] = wd_ref[j].astype(BF16)

        def tile(t, carry):
            g = g0 + t
            slot = g % 2
            slot_in = g % n_in
            at_height(e, t, lambda nr: [in_copy(e, t, slot_in, part, nr).wait()
                                        for part in range(PACK_PARTS)])

            @pl.when(g + FFN_AHEAD < total)
            def _():
                start_in(g + FFN_AHEAD)

            @pl.when(g >= 2)
            def _():
                wait_out(g - 2)

            def compute(n_rows):
                x = _unpack_rows([xbuf[slot_in, part, 0:n_rows, :]
                                  for part in range(PACK_PARTS)]).astype(BF16)
                a = _silu(_dot(x, wg_b[...])) * _dot(x, wu_b[...])
                y = _pack_rows(_dot(a.astype(BF16), wd_b[...]))
                for part in range(PACK_PARTS):
                    ybuf[slot, part, 0:n_rows, :] = y[part]
                    out_copy(e, t, slot, part, n_rows).start()

            at_height(e, t, compute)
            return carry

        lax.fori_loop(0, n, tile, 0)

        @pl.when(g0 + n == total)
        def _():
            @pl.when(total >= 2)
            def _():
                wait_out(total - 2)

            wait_out(total - 1)


def _ffn(xs_parts, counts, w_gate, w_up, w_down):
    n_exp, d = w_gate.shape[0], w_gate.shape[1]
    cap = xs_parts[0].shape[0] // n_exp
    ntiles = (counts + FFN_TILE - 1) // FFN_TILE
    cum = jnp.cumsum(ntiles)
    first = cum - ntiles
    max_tiles = (cap * TOP_K) // FFN_TILE + n_exp
    g = jnp.arange(max_tiles, dtype=I32)
    done = g[:, None] >= cum[None, :]
    tile_e = jnp.minimum(jnp.sum(done, axis=1), n_exp - 1).astype(I32)
    tile_t = (g - jnp.sum(jnp.where(done, ntiles[None, :], 0), axis=1)).astype(I32)
    wmap = lambda e, cnt, nt, first, te, tt: (e, 0, 0)
    hbm = pl.BlockSpec(memory_space=pl.ANY)
    return pl.pallas_call(
        functools.partial(_ffn_kernel, cap=cap, n_exp=n_exp),
        grid_spec=pltpu.PrefetchScalarGridSpec(
            num_scalar_prefetch=5, grid=(n_exp // FFN_GROUP,),
            in_specs=[hbm] * PACK_PARTS
                     + [pl.BlockSpec((FFN_GROUP, d, D_EXPERT), wmap),
                        pl.BlockSpec((FFN_GROUP, d, D_EXPERT), wmap),
                        pl.BlockSpec((FFN_GROUP, D_EXPERT, d), wmap)],
            out_specs=[hbm] * PACK_PARTS,
            scratch_shapes=[pltpu.VMEM((FFN_AHEAD + 1, PACK_PARTS, FFN_TILE, PACK_W), U32),
                            pltpu.VMEM((2, PACK_PARTS, FFN_TILE, PACK_W), U32),
                            pltpu.VMEM((d, D_EXPERT), BF16), pltpu.VMEM((d, D_EXPERT), BF16),
                            pltpu.VMEM((D_EXPERT, d), BF16),
                            pltpu.SemaphoreType.DMA((FFN_AHEAD + 1, PACK_PARTS)),
                            pltpu.SemaphoreType.DMA((2, PACK_PARTS))]),
        out_shape=[jax.ShapeDtypeStruct(xs_parts[0].shape, U32)] * PACK_PARTS,
        compiler_params=pltpu.CompilerParams(
            dimension_semantics=("arbitrary",), vmem_limit_bytes=VMEM_LIMIT),
        name="ffn",
    )(counts.astype(I32), ntiles.astype(I32), first.astype(I32), tile_e, tile_t, *xs_parts,
      w_gate, w_up, w_down)


def _combine_kernel(x1_ref, h2pa_ref, h2pb_ref, ra_ref, rb_ref, mod_ref,
                    wsg_ref, wsu_ref, wsd_ref, o_ref):
    h2 = _unpack_rows([h2pa_ref[0], h2pb_ref[0]]).astype(BF16)
    a = _silu(_dot(h2, wsg_ref[...])) * _dot(h2, wsu_ref[...])
    routed = jnp.concatenate([ra_ref[0], rb_ref[0]], axis=1)
    o_ref[0] = x1_ref[0] + mod_ref[0][5:6] * (_dot(a.astype(BF16), wsd_ref[...]) + routed)


def _combine(x1, h2p_parts, routed_parts, mod, per_batch_mod, p, tm):
    b, l, d = x1.shape
    mod_map = (lambda bi, i: (bi, 0, 0)) if per_batch_mod else (lambda bi, i: (0, 0, 0))
    tok = lambda w: pl.BlockSpec((1, tm, w), lambda bi, i: (bi, i, 0))
    full = lambda a: pl.BlockSpec(a.shape, lambda bi, i: (0, 0))
    weights = [p['w_sh_gate'], p['w_sh_up'], p['w_sh_down']]
    routed_parts = [r.reshape(b, l, d // PACK_PARTS) for r in routed_parts]
    return pl.pallas_call(
        _combine_kernel,
        grid=(b, l // tm),
        in_specs=[tok(d)] + [tok(PACK_W)] * PACK_PARTS + [tok(d // PACK_PARTS)] * PACK_PARTS
                 + [pl.BlockSpec((1, 6, d), mod_map)] + [full(w) for w in weights],
        out_specs=tok(d),
        out_shape=jax.ShapeDtypeStruct((b, l, d), F32),
        compiler_params=pltpu.CompilerParams(
            dimension_semantics=("parallel", "parallel"), vmem_limit_bytes=VMEM_LIMIT),
        name="combine",
    )(x1, *h2p_parts, *routed_parts, mod, *weights)


def _pad_heads(w, parts):
    k = w.shape[0]
    per = w.shape[1] // MLA_HEADS
    w = w.reshape(k, MLA_HEADS, per)[:, :, parts[0]:parts[1]]
    w = jnp.pad(w, ((0, 0), (0, 0), (0, HEAD_W - (parts[1] - parts[0]))))
    return w.reshape(k, MLA_HEADS * HEAD_W)


def _prep_params(l, g_norm1, g_norm2, w_in, w_gk_fwd, b_gk_fwd, w_gk_bwd, b_gk_bwd, g_gla_out,
                 g_q_lora, w_uq, g_kv_lora, w_ukv, g_qk_q, g_qk_k, w_out, w_router, b_router,
                 w_sh_gate, w_sh_up, w_sh_down):
    w = w_in[l]
    d = w.shape[0]
    o_lrf = COL_CQ
    o_lrb = o_lrf + GLA_GATE_RANK
    o_cq = o_lrb + GLA_GATE_RANK
    o_ckv = o_cq + Q_LORA
    o_kr = o_ckv + KV_LORA
    w_in_p = jnp.concatenate([
        w[:, 0:COL_CQ], w[:, o_cq:o_ckv], w[:, o_ckv:o_kr], jnp.zeros((d, KR_LANE0), w.dtype),
        w[:, o_kr:o_kr + ROPE_DIM], w[:, o_lrf:o_lrb], w[:, o_lrb:o_cq]], axis=1).astype(BF16)
    z = jnp.zeros((GLA_GATE_RANK, GLA_KEY_W), F32)
    w_gk_big = jnp.concatenate([
        jnp.zeros((LANES - 2 * GLA_GATE_RANK, 2 * GLA_KEY_W), F32),
        jnp.concatenate([w_gk_fwd[l], z], axis=1),
        jnp.concatenate([z, w_gk_bwd[l]], axis=1)], axis=0).astype(BF16)
    pad_gain = lambda g: jnp.pad(g, (0, HEAD_W - QK_DIM)).reshape(1, HEAD_W)
    w_ukv_h = w_ukv[l]
    return {
        'g_norm1': g_norm1[l].reshape(1, d), 'g_norm2': g_norm2[l].reshape(1, d),
        'w_in_p': w_in_p, 'w_gk_big': w_gk_big,
        'b_gk': jnp.concatenate([b_gk_fwd[l], b_gk_bwd[l]]).reshape(1, 2 * GLA_KEY_W),
        'g_gla_out': g_gla_out[l].reshape(1, GLA_DV),
        'g_q_lora': g_q_lora[l].reshape(1, Q_LORA),
        'w_uq_p': _pad_heads(w_uq[l], (0, QK_DIM)).astype(BF16),
        'g_kv_lora': g_kv_lora[l].reshape(1, KV_LORA),
        'w_ukv_p': jnp.concatenate([_pad_heads(w_ukv_h, (0, NOPE_DIM)),
                                    _pad_heads(w_ukv_h, (NOPE_DIM, NOPE_DIM + V_DIM))],
                                   axis=1).astype(BF16),
        'gq': pad_gain(g_qk_q[l]), 'gk': pad_gain(g_qk_k[l]),
        'w_out_top': w_out[l][:GLA_VAL_W].astype(BF16),
        'w_out_bot': w_out[l][GLA_VAL_W:].astype(BF16),
        'w_router_t': w_router[l].T, 'b_router_col': b_router[l].reshape(N_EXPERTS, 1),
        'w_sh_gate': w_sh_gate[l].astype(BF16), 'w_sh_up': w_sh_up[l].astype(BF16),
        'w_sh_down': w_sh_down[l].astype(BF16),
    }


def _rope_tables(length):
    pos = np.arange(length)
    r = (pos // GRID_W).astype(np.float32)
    c = (pos % GRID_W).astype(np.float32)
    half = ROPE_DIM // 4
    inv_freq = np.float32(ROPE_THETA) ** (-np.arange(half, dtype=np.float32) / np.float32(half))
    ang_r = r[:, None] * inv_freq[None, :]
    ang_c = c[:, None] * inv_freq[None, :]
    zeros = lambda w: np.zeros((length, w), np.float32)
    ones = lambda w: np.ones((length, w), np.float32)
    tail_w = HEAD_W - ROPE_LANE0 - ROPE_DIM
    cos = np.concatenate([ones(ROPE_LANE0), np.cos(ang_r), np.cos(ang_r), np.cos(ang_c),
                          np.cos(ang_c), ones(tail_w)], axis=1)
    s1 = np.concatenate([zeros(ROPE_LANE0), -np.sin(ang_r), zeros(half), -np.sin(ang_c),
                         zeros(half), zeros(tail_w)], axis=1)
    s2 = np.concatenate([zeros(ROPE_LANE0), zeros(half), np.sin(ang_r), zeros(half),
                         np.sin(ang_c), zeros(tail_w)], axis=1)
    return tuple(jnp.asarray(t, F32) for t in (cos, s1, s2))


def _mix_route_dispatch(x_tok, seq_shape, mod, per_batch_mod, p, s0f, s0b, rope_tabs, ctx_kv,
                        tq):
    bt, lt, d = x_tok.shape
    b, l = seq_shape
    n_tok = bt * lt
    (q, k, v, g, lgf, lgb, qm, km, vm, ckvn, kr) = _inproj(x_tok, mod, per_batch_mod, p,
                                                            rope_tabs, IN_TILE)
    seq = lambda a: a.reshape(b, l, a.shape[-1])
    og, sf, sb = _gla(seq(q), seq(k), seq(v), seq(g), seq(lgf), seq(lgb), s0f, s0b,
                      p['g_gla_out'], GLA_SEQS if l == GLA_BLOCK else 1)
    kv_pairs = [(seq(km), seq(vm))] + ([ctx_kv] if ctx_kv is not None else [])
    om = _attn(seq(qm), kv_pairs, tq, ATTN_SEQS if l == tq else 1)
    tokv = lambda a: a.reshape(bt, lt, a.shape[-1])
    x1, h2a, h2b, slot8, w8, cnt = _outproj(tokv(og), tokv(om), x_tok, mod, per_batch_mod, p,
                                            OUT_TILE)
    x_sorted = [_sc_scatter_rows(h.reshape(n_tok, PACK_W), slot8,
                                 N_EXPERTS * _expert_cap(n_tok)) for h in (h2a, h2b)]
    routed = dict(x1=x1, h2=(h2a, h2b), slot8=slot8, w8=w8, cnt=cnt, x_sorted=x_sorted)
    return routed, sf, sb, ckvn, kr


def _experts_collect(r, w_gate, w_up, w_down):
    n_tok = r['slot8'].shape[1]
    y_sorted = _ffn(r['x_sorted'], r['cnt'][:, 0].astype(I32), w_gate, w_up, w_down)
    slots_tk = r['slot8'].T.reshape(1, TOP_K * n_tok)
    w_rep = r['w8'].reshape(n_tok, LANES)
    return [_sc_collect_sum(y, slots_tk, w_rep) for y in y_sorted]


def kernel(x_prompt, x_sample, c, state_gla_fwd, state_gla_bwd, cache_mla_ckv, cache_mla_krope,
           c_ctx, w_ada, b_ada, g_norm1, g_norm2, w_in, w_gk_fwd, b_gk_fwd, w_gk_bwd, b_gk_bwd,
           g_gla_out, g_q_lora, w_uq, g_kv_lora, w_ukv, g_qk_q, g_qk_k, w_out,
           w_router, b_router, w_exp_gate, w_exp_up, w_exp_down, w_sh_gate, w_sh_up, w_sh_down):
    bp, lp, d = x_prompt.shape
    bs, ls, _ = x_sample.shape
    depth = w_ada.shape[0]
    xp = x_prompt.reshape(1, bp * lp, d)
    xs = x_sample
    new_f, new_b, new_ckv, new_kr = [], [], [], []
    lat_tabs = _rope_tables(ls)
    cvecs = jnp.concatenate([c_ctx[None], c, jnp.zeros((8 - 1 - bs, d), F32)], axis=0)
    for l in range(depth):
        p = _prep_params(l, g_norm1, g_norm2, w_in, w_gk_fwd, b_gk_fwd, w_gk_bwd, b_gk_bwd,
                         g_gla_out, g_q_lora, w_uq, g_kv_lora, w_ukv, g_qk_q, g_qk_k, w_out,
                         w_router, b_router, w_sh_gate, w_sh_up, w_sh_down)
        mod = _ada(cvecs, w_ada[l], b_ada[l]).reshape(8, 6, d)
        mod_p, mod_s = mod[0:1], mod[1:1 + bs]
        zeros = jnp.zeros((bp, GLA_KEY_W, GLA_DV), F32)
        r_ctx, sf, sb, ckvn, kr = _mix_route_dispatch(
            xp, (bp, lp), mod_p, False, p, zeros, zeros, (), None, lp)
        new_f.append(sf.reshape(bp, GLA_HEADS, GLA_DK, GLA_DV))
        new_b.append(sb.reshape(bp, GLA_HEADS, GLA_DK, GLA_DV))
        new_ckv.append(ckvn.reshape(bp, lp, KV_LORA))
        new_kr.append(kr.reshape(bp, lp, LANES)[:, :, KR_LANE0:KR_LANE0 + ROPE_DIM])
        kr_cache = jnp.pad(cache_mla_krope[:, l],
                           ((0, 0), (0, 0), (ROPE_LANE0, LANES - ROPE_LANE0 - ROPE_DIM)))
        ctx_kv = _cache_kv(cache_mla_ckv[:, l], kr_cache, p)
        s0f = state_gla_fwd[:, l].reshape(bs, GLA_KEY_W, GLA_DV)
        s0b = state_gla_bwd[:, l].reshape(bs, GLA_KEY_W, GLA_DV)
        r_lat, _, _, _, _ = _mix_route_dispatch(
            xs, (bs, ls), mod_s, True, p, s0f, s0b, lat_tabs, ctx_kv, ATTN_TILE)
        experts = (w_exp_gate[l], w_exp_up[l], w_exp_down[l])
        routed_ctx = _experts_collect(r_ctx, *experts)
        routed_lat = _experts_collect(r_lat, *experts)
        xp = _combine(r_ctx['x1'], r_ctx['h2'], routed_ctx, mod_p, False, p, OUT_TILE)
        xs = _combine(r_lat['x1'], r_lat['h2'], routed_lat, mod_s, True, p, OUT_TILE)
    return (xp.reshape(bp, lp, d), xs, jnp.stack(new_f, axis=1), jnp.stack(new_b, axis=1),
            jnp.stack(new_ckv, axis=1), jnp.stack(new_kr, axis=1))
```

```python
import functools

import jax
import jax.numpy as jnp
import numpy as np
from jax import lax
from jax.experimental import pallas as pl
from jax.experimental.pallas import tpu as pltpu
from jax.experimental.pallas import tpu_sc as plsc

F32 = jnp.float32
BF16 = jnp.bfloat16
I32 = jnp.int32
U32 = jnp.uint32

D_MODEL = 1024
EPS = 1e-6
GRID_W = 64
GLA_HEADS = 4
GLA_DK = 64
GLA_DV = 128
GLA_GATE_RANK = 16
GLA_GATE_NORM = 16.0
GLA_KEY_W = GLA_HEADS * GLA_DK
GLA_VAL_W = GLA_HEADS * GLA_DV
MLA_HEADS = 4
Q_LORA = 256
KV_LORA = 128
NOPE_DIM = 64
ROPE_DIM = 32
V_DIM = 128
QK_DIM = NOPE_DIM + ROPE_DIM
ROPE_THETA = 10000.0
N_EXPERTS = 64
TOP_K = 8
N_GROUPS = 8
TOPK_GROUPS = 4
D_EXPERT = 256
ROUTED_SCALE = 2.5

LANES = 128
HEAD_W = LANES
MLA_W = MLA_HEADS * HEAD_W
ROPE_LANE0 = NOPE_DIM
COL_V = 2 * GLA_KEY_W
COL_G = COL_V + GLA_VAL_W
COL_CQ = COL_G + GLA_VAL_W
COL_CKV = COL_CQ + Q_LORA
TAIL0 = COL_CKV + KV_LORA
IN_W = TAIL0 + LANES
KR_LANE0 = LANES - ROPE_DIM - 2 * GLA_GATE_RANK
GLA_BLOCK = 256
GLA_SUB = 64
GLA_SEQS = 2
ATTN_SEQS = 4
ATTN_TILE = 1024
ADA_TILE = 768
IN_TILE = 1024
IN_SUB = 256
OUT_TILE = 1024
FFN_TILE = 512
FFN_AHEAD = 5
FFN_GROUP = 4
PACK_PARTS = 2
PACK_W = D_MODEL // (2 * PACK_PARTS)
SC_WINDOW = 128
SC_LANES = 16
SC_GATHERS = 2
VMEM_LIMIT = 56 * 1024 * 1024

HIGHEST = lax.Precision.HIGHEST


def _dot(a, b, precision=None):
    return jnp.dot(a, b, preferred_element_type=F32, precision=precision)


def _dot_nt(a, b, precision=None):
    return lax.dot_general(a, b, (((1,), (1,)), ((), ())), preferred_element_type=F32,
                           precision=precision)


def _dot_nt_3pass(a, b):
    a_hi = a.astype(BF16)
    b_hi = b.astype(BF16)
    a_lo = (a - a_hi.astype(F32)).astype(BF16)
    b_lo = (b - b_hi.astype(F32)).astype(BF16)
    return _dot_nt(a_hi, b_hi) + _dot_nt(a_hi, b_lo) + _dot_nt(a_lo, b_hi)


def _rms(x, width):
    ss = jnp.sum(x * x, axis=-1, keepdims=True) * (1.0 / width)
    return x * lax.rsqrt(ss + EPS)


def _silu(x):
    return x * jax.nn.sigmoid(x)


def _log_sigmoid(x):
    return jnp.minimum(x, 0.0) - jnp.log1p(jnp.exp(-jnp.abs(x)))


def _pack_rows(x):
    parts = []
    for i in range(PACK_PARTS):
        c0 = i * 2 * PACK_W
        lo = lax.bitcast_convert_type(x[:, c0:c0 + PACK_W].astype(BF16).astype(F32), U32)
        hi = lax.bitcast_convert_type(
            x[:, c0 + PACK_W:c0 + 2 * PACK_W].astype(BF16).astype(F32), U32)
        parts.append(hi | (lo >> 16))
    return parts


def _unpack_rows(parts):
    cols = []
    for w in parts:
        cols.append(lax.bitcast_convert_type(w << 16, F32))
        cols.append(lax.bitcast_convert_type(w & jnp.uint32(0xFFFF0000), F32))
    return jnp.concatenate(cols, axis=1)


def _ada_kernel(c_ref, w_ref, b_ref, o_ref):
    o_ref[...] = _dot(_silu(c_ref[...]), w_ref[...], precision=HIGHEST) + b_ref[...]


def _ada(cvecs, w_ada, b_ada):
    n = w_ada.shape[1]
    tn = ADA_TILE
    return pl.pallas_call(
        _ada_kernel,
        grid=(n // tn,),
        in_specs=[pl.BlockSpec((8, D_MODEL), lambda j: (0, 0)),
                  pl.BlockSpec((D_MODEL, tn), lambda j: (0, j)),
                  pl.BlockSpec((1, tn), lambda j: (0, j))],
        out_specs=pl.BlockSpec((8, tn), lambda j: (0, j)),
        out_shape=jax.ShapeDtypeStruct((8, n), F32),
        name="ada",
    )(cvecs, w_ada, b_ada.reshape(1, n))


def _rope(x, c, s1, s2):
    return x * c + pltpu.roll(x, LANES - 8, 1) * s1 + pltpu.roll(x, 8, 1) * s2


def _mla_kv(ckv_n, kr, w_ukv_ref, gk, rope_tabs, k_ref, v_ref, rs=slice(None)):
    kv = _dot(ckv_n.astype(BF16), w_ukv_ref[...])
    for h in range(MLA_HEADS):
        k_h = kv[:, h * HEAD_W:(h + 1) * HEAD_W] + kr
        k_h = _rms(k_h, QK_DIM) * gk
        if rope_tabs is not None:
            k_h = _rope(k_h, *rope_tabs)
        k_ref[0, rs, h * HEAD_W:(h + 1) * HEAD_W] = k_h.astype(BF16)
    v_ref[0, rs, :] = kv[:, MLA_W:].astype(BF16)


def _lane_mask(lo, hi, rows):
    lane = lax.broadcasted_iota(I32, (rows, LANES), 1)
    return (lane >= lo) & (lane < hi)


def _inproj_kernel(x_ref, mod_ref, g1_ref, win_ref, wgk_ref, bgk_ref, gql_ref, wuq_ref,
                   gkv_ref, wukv_ref, gq_ref, gk_ref, *refs, positions):
    tab_refs, outs = (refs[:3], refs[3:]) if positions else ((), refs)
    (q_ref, k_ref, v_ref, g_ref, lgf_ref, lgb_ref, qm_ref, km_ref, vm_ref, ckvn_ref,
     kr_ref) = outs
    mod = mod_ref[0]
    for r0 in range(0, x_ref.shape[1], IN_SUB):
        rs = slice(r0, r0 + IN_SUB)
        x = x_ref[0, rs, :]
        h = _rms(x, D_MODEL) * g1_ref[...] * (1.0 + mod[1:2]) + mod[0:1]
        y = _dot(h.astype(BF16), win_ref[...])
        q_ref[0, rs, :] = (y[:, 0:GLA_KEY_W] * GLA_DK ** -0.5).astype(BF16)
        k_ref[0, rs, :] = y[:, GLA_KEY_W:COL_V].astype(BF16)
        v_ref[0, rs, :] = y[:, COL_V:COL_G].astype(BF16)
        g_ref[0, rs, :] = y[:, COL_G:COL_CQ].astype(BF16)
        tail = y[:, TAIL0:IN_W]
        pre = _dot(tail.astype(BF16), wgk_ref[...]) + bgk_ref[...]
        logg = _log_sigmoid(pre) * (1.0 / GLA_GATE_NORM)
        lgf_ref[0, rs, :] = logg[:, 0:GLA_KEY_W]
        lgb_ref[0, rs, :] = logg[:, GLA_KEY_W:]
        tabs = tuple(r[rs, :] for r in tab_refs) if positions else None
        cq = _rms(y[:, COL_CQ:COL_CKV], Q_LORA) * gql_ref[...]
        qm = _dot(cq.astype(BF16), wuq_ref[...])
        gq = gq_ref[...]
        for hh in range(MLA_HEADS):
            q_h = _rms(qm[:, hh * HEAD_W:(hh + 1) * HEAD_W], QK_DIM) * gq
            if positions:
                q_h = _rope(q_h, *tabs)
            qm_ref[0, rs, hh * HEAD_W:(hh + 1) * HEAD_W] = (q_h * QK_DIM ** -0.5).astype(BF16)
        ckv_n = _rms(y[:, COL_CKV:TAIL0], KV_LORA) * gkv_ref[...]
        ckvn_ref[0, rs, :] = ckv_n
        kr_ref[0, rs, :] = tail
        kr = jnp.where(_lane_mask(KR_LANE0, KR_LANE0 + ROPE_DIM, IN_SUB), tail, 0.0)
        _mla_kv(ckv_n, kr, wukv_ref, gk_ref[...], tabs, km_ref, vm_ref, rs)


def _inproj(x, mod, per_batch_mod, p, rope_tabs, tm):
    b, l, d = x.shape
    nt = l // tm
    mod_map = (lambda bi, i: (bi, 0, 0)) if per_batch_mod else (lambda bi, i: (0, 0, 0))
    tab_map = lambda bi, i: (i, 0)
    const = lambda bi, i: (0, 0)
    tok = lambda w: pl.BlockSpec((1, tm, w), lambda bi, i: (bi, i, 0))
    full = lambda a: pl.BlockSpec(a.shape, const)
    weights = [p['g_norm1'], p['w_in_p'], p['w_gk_big'], p['b_gk'], p['g_q_lora'], p['w_uq_p'],
               p['g_kv_lora'], p['w_ukv_p'], p['gq'], p['gk']]
    outs = [(GLA_KEY_W, BF16), (GLA_KEY_W, BF16), (GLA_VAL_W, BF16), (GLA_VAL_W, BF16),
            (GLA_KEY_W, F32), (GLA_KEY_W, F32), (MLA_W, BF16), (MLA_W, BF16), (MLA_W, BF16),
            (KV_LORA, F32), (LANES, F32)]
    return pl.pallas_call(
        functools.partial(_inproj_kernel, positions=bool(rope_tabs)),
        grid=(b, nt),
        in_specs=[tok(d), pl.BlockSpec((1, 6, d), mod_map)] + [full(w) for w in weights]
                 + [pl.BlockSpec((tm, LANES), tab_map)] * len(rope_tabs),
        out_specs=[tok(w) for w, _ in outs],
        out_shape=[jax.ShapeDtypeStruct((b, l, w), dt) for w, dt in outs],
        compiler_params=pltpu.CompilerParams(
            dimension_semantics=("parallel", "parallel"), vmem_limit_bytes=VMEM_LIMIT),
        name="inproj",
    )(x, mod, *weights, *rope_tabs)


def _cache_kv_kernel(ckv_ref, kr_ref, wukv_ref, gk_ref, k_ref, v_ref):
    _mla_kv(ckv_ref[0], kr_ref[0], wukv_ref, gk_ref[...], None, k_ref, v_ref)


def _cache_kv(ckv, kr128, p):
    b, l, _ = ckv.shape
    tok = lambda w: pl.BlockSpec((1, l, w), lambda bi: (bi, 0, 0))
    full = lambda a: pl.BlockSpec(a.shape, lambda bi: (0, 0))
    return pl.pallas_call(
        _cache_kv_kernel,
        grid=(b,),
        in_specs=[tok(KV_LORA), tok(LANES), full(p['w_ukv_p']), full(p['gk'])],
        out_specs=[tok(MLA_W), tok(MLA_W)],
        out_shape=[jax.ShapeDtypeStruct((b, l, MLA_W), BF16)] * 2,
        name="cache_kv",
    )(ckv, kr128, p['w_ukv_p'], p['gk'])


def _split3(x):
    hi = x.astype(BF16)
    r1 = x - hi.astype(F32)
    mid = r1.astype(BF16)
    lo = (r1 - mid.astype(F32)).astype(BF16)
    return hi, mid, lo


def _gla_block(q, k, v, lg, s, fwd):
    n = GLA_BLOCK
    nc = n // GLA_SUB
    row = lax.broadcasted_iota(I32, (n, n), 0)
    col = lax.broadcasted_iota(I32, (n, n), 1)
    same = (row // GLA_SUB) == (col // GLA_SUB)
    causal = same & ((col <= row) if fwd else (col >= row))
    tri = causal.astype(BF16)
    hi, mid, lo = _split3(lg)
    cum = _dot(tri, hi) + _dot(tri, mid) + _dot(tri, lo)
    tot_rows, mid_rows = [], []
    for c in range(nc):
        r_tot = c * GLA_SUB + (GLA_SUB - 1 if fwd else 0)
        r_mid = c * GLA_SUB + GLA_SUB // 2
        tot_rows.append(jnp.broadcast_to(cum[r_tot:r_tot + 1], (GLA_SUB, GLA_KEY_W)))
        mid_rows.append(jnp.broadcast_to(cum[r_mid:r_mid + 1], (GLA_SUB, GLA_KEY_W)))
    tot_rows = jnp.concatenate(tot_rows, axis=0)
    mid_rows = jnp.concatenate(mid_rows, axis=0)
    rel = cum - mid_rows
    qi = q * jnp.exp(rel)
    ki = (k * jnp.exp(-rel)).astype(BF16)
    q_in = (q * jnp.exp(cum)).astype(BF16)
    k_up = k * jnp.exp(tot_rows - cum)
    dec = jnp.exp(tot_rows)
    k_up_t = k_up.T.astype(BF16)
    dec_t = dec.T
    o_heads = []
    for h in range(GLA_HEADS):
        hk = slice(h * GLA_DK, (h + 1) * GLA_DK)
        a = _dot_nt(qi[:, hk].astype(BF16), ki[:, hk])
        a = jnp.where(causal, a, 0.0).astype(BF16)
        o_heads.append(_dot(a, v[:, h * GLA_DV:(h + 1) * GLA_DV]))
    o = jnp.concatenate(o_heads, axis=1)
    o_inter = [None] * nc
    for c in (range(nc) if fwd else range(nc - 1, -1, -1)):
        rc = slice(c * GLA_SUB, (c + 1) * GLA_SUB)
        s_b = s.astype(BF16)
        o_inter[c] = jnp.concatenate(
            [_dot(q_in[rc, h * GLA_DK:(h + 1) * GLA_DK], s_b[h * GLA_DK:(h + 1) * GLA_DK])
             for h in range(GLA_HEADS)], axis=1)
        u = jnp.concatenate(
            [_dot(k_up_t[h * GLA_DK:(h + 1) * GLA_DK, rc], v[rc, h * GLA_DV:(h + 1) * GLA_DV])
             for h in range(GLA_HEADS)], axis=0)
        s = dec_t[:, c * GLA_SUB:c * GLA_SUB + 1] * s + u
    return o + jnp.concatenate(o_inter, axis=0), s


def _gla_kernel(q_ref, k_ref, v_ref, g_ref, lgf_ref, lgb_ref, s0f_ref, s0b_ref, gout_ref,
                o_ref, sf_ref, sb_ref, acc_ref, st_ref, *, nblk, n_seq):
    for si in range(n_seq):
        _gla_sequence(si, q_ref, k_ref, v_ref, g_ref, lgf_ref, lgb_ref, s0f_ref, s0b_ref,
                      gout_ref, o_ref, sf_ref, sb_ref, acc_ref, st_ref, nblk)


def _gla_sequence(si, q_ref, k_ref, v_ref, g_ref, lgf_ref, lgb_ref, s0f_ref, s0b_ref, gout_ref,
                  o_ref, sf_ref, sb_ref, acc_ref, st_ref, nblk):
    st_ref[si, 0] = s0f_ref[si]
    st_ref[si, 1] = s0b_ref[si]

    def load(blk):
        start = blk * GLA_BLOCK
        r = pl.ds(start if isinstance(blk, int) else pl.multiple_of(start, GLA_BLOCK), GLA_BLOCK)
        return r, q_ref[si, r, :].astype(F32), k_ref[si, r, :].astype(F32), v_ref[si, r, :]

    def pair_step(i, first_touch):
        rf, qf, kf, vf = load(i)
        rb, qb, kb, vb = load(nblk - 1 - i)
        of, s_f = _gla_block(qf, kf, vf, lgf_ref[si, rf, :], st_ref[si, 0], True)
        ob, s_b = _gla_block(qb, kb, vb, lgb_ref[si, rb, :], st_ref[si, 1], False)
        if first_touch:
            acc_ref[si, rf, :] = of
            acc_ref[si, rb, :] = ob
        else:
            acc_ref[si, rf, :] += of
            acc_ref[si, rb, :] += ob
        st_ref[si, 0] = s_f
        st_ref[si, 1] = s_b

    if nblk == 1:
        r, q, k, v = load(0)
        of, s_f = _gla_block(q, k, v, lgf_ref[si, r, :], st_ref[si, 0], True)
        ob, s_b = _gla_block(q, k, v, lgb_ref[si, r, :], st_ref[si, 1], False)
        acc_ref[si, r, :] = of + ob
        st_ref[si, 0] = s_f
        st_ref[si, 1] = s_b
    else:
        half = nblk // 2
        lax.fori_loop(0, half, lambda i, c: (pair_step(i, True), c)[1], 0)
        lax.fori_loop(half, nblk, lambda i, c: (pair_step(i, False), c)[1], 0)
    sf_ref[si] = st_ref[si, 0]
    sb_ref[si] = st_ref[si, 1]

    def fin_step(blk, carry):
        r = pl.ds(pl.multiple_of(blk * GLA_BLOCK, GLA_BLOCK), GLA_BLOCK)
        o = acc_ref[si, r, :]
        gate = _silu(g_ref[si, r, :].astype(F32))
        for h in range(GLA_HEADS):
            sl = slice(h * GLA_DV, (h + 1) * GLA_DV)
            o_ref[si, r, sl] = (_rms(o[:, sl], GLA_DV) * gout_ref[...] * gate[:, sl]).astype(BF16)
        return carry

    lax.fori_loop(0, nblk, fin_step, 0)


def _gla(q, k, v, g, lgf, lgb, s0f, s0b, g_out, n_seq):
    b, l, _ = q.shape
    assert l == GLA_BLOCK or l % (2 * GLA_BLOCK) == 0
    seq = lambda w: pl.BlockSpec((n_seq, l, w), lambda bi: (bi, 0, 0))
    st = pl.BlockSpec((n_seq, GLA_KEY_W, GLA_DV), lambda bi: (bi, 0, 0))
    return pl.pallas_call(
        functools.partial(_gla_kernel, nblk=l // GLA_BLOCK, n_seq=n_seq),
        grid=(b // n_seq,),
        in_specs=[seq(GLA_KEY_W), seq(GLA_KEY_W), seq(GLA_VAL_W), seq(GLA_VAL_W),
                  seq(GLA_KEY_W), seq(GLA_KEY_W), st, st,
                  pl.BlockSpec((1, GLA_DV), lambda bi: (0, 0))],
        out_specs=[seq(GLA_VAL_W), st, st],
        out_shape=[jax.ShapeDtypeStruct((b, l, GLA_VAL_W), BF16),
                   jax.ShapeDtypeStruct((b, GLA_KEY_W, GLA_DV), F32),
                   jax.ShapeDtypeStruct((b, GLA_KEY_W, GLA_DV), F32)],
        scratch_shapes=[pltpu.VMEM((n_seq, l, GLA_VAL_W), F32),
                        pltpu.VMEM((n_seq, 2, GLA_KEY_W, GLA_DV), F32)],
        compiler_params=pltpu.CompilerParams(
            dimension_semantics=("parallel",), vmem_limit_bytes=VMEM_LIMIT),
        name="gla",
    )(q, k, v, g, lgf, lgb, s0f, s0b, g_out)


def _attn_kernel(*refs, n_kv, n_seq):
    q_ref, o_ref = refs[0], refs[-1]
    kv = [(refs[1 + 2 * i], refs[2 + 2 * i]) for i in range(n_kv)]
    for si in range(n_seq):
        for h in range(MLA_HEADS):
            sl = slice(h * HEAD_W, (h + 1) * HEAD_W)
            q = q_ref[si, :, sl]
            scores = [_dot_nt(q, k_ref[si, :, sl]) for k_ref, _ in kv]
            m = functools.reduce(jnp.maximum,
                                 [jnp.max(s, axis=-1, keepdims=True) for s in scores])
            o, den = 0.0, 0.0
            for s, (_, v_ref) in zip(scores, kv):
                p = jnp.exp(s - m)
                den = den + jnp.sum(p, axis=-1, keepdims=True)
                o = o + _dot(p.astype(BF16), v_ref[si, :, sl])
            o_ref[si, :, sl] = (o / den).astype(BF16)


def _attn(q, kv_pairs, tq, n_seq):
    b, l, _ = q.shape
    kv_specs, kv_args = [], []
    for k, v in kv_pairs:
        spec = pl.BlockSpec((n_seq, k.shape[1], MLA_W), lambda bi, i: (bi, 0, 0))
        kv_specs += [spec, spec]
        kv_args += [k, v]
    return pl.pallas_call(
        functools.partial(_attn_kernel, n_kv=len(kv_pairs), n_seq=n_seq),
        grid=(b // n_seq, l // tq),
        in_specs=[pl.BlockSpec((n_seq, tq, MLA_W), lambda bi, i: (bi, i, 0))] + kv_specs,
        out_specs=pl.BlockSpec((n_seq, tq, MLA_W), lambda bi, i: (bi, i, 0)),
        out_shape=jax.ShapeDtypeStruct((b, l, MLA_W), BF16),
        compiler_params=pltpu.CompilerParams(
            dimension_semantics=("parallel", "parallel"), vmem_limit_bytes=VMEM_LIMIT),
        name="attn",
    )(q, *kv_args)


def _expert_cap(n_tok):
    return -(-n_tok // FFN_TILE) * FFN_TILE


def _sum01(x):
    return jnp.sum(jnp.sum(x, axis=1, keepdims=True), axis=0, keepdims=True)


def _route(logits_t, bias_col, cnt_col, cap):
    t = logits_t.shape[1]
    gsz = N_EXPERTS // N_GROUPS
    scores = jax.nn.sigmoid(logits_t)
    sel = (scores + bias_col).reshape(N_GROUPS, gsz, t)
    scores = scores.reshape(N_GROUPS, gsz, t)
    neg = jnp.float32(-jnp.inf)
    ie = lax.broadcasted_iota(I32, (N_GROUPS, gsz, t), 1)
    ig = lax.broadcasted_iota(I32, (N_GROUPS, gsz, t), 0)
    m1 = jnp.max(sel, axis=1, keepdims=True)
    first = jnp.min(jnp.where(sel == m1, ie, gsz), axis=1, keepdims=True)
    m2 = jnp.max(jnp.where(ie == first, neg, sel), axis=1, keepdims=True)
    grp = m1 + m2
    igk = lax.broadcasted_iota(I32, (N_GROUPS, 1, t), 0)
    g_sel = jnp.zeros((N_GROUPS, 1, t), jnp.bool_)
    cur = grp
    for _ in range(TOPK_GROUPS):
        m = jnp.max(cur, axis=0, keepdims=True)
        pick = igk == jnp.min(jnp.where(cur == m, igk, N_GROUPS), axis=0, keepdims=True)
        g_sel = g_sel | pick
        cur = jnp.where(pick, neg, cur)
    cur = jnp.where(g_sel, sel, neg)
    idx = ig * gsz + ie
    e_sel = jnp.zeros((N_GROUPS, gsz, t), jnp.bool_)
    picks = []
    for _ in range(TOP_K):
        m = jnp.max(jnp.max(cur, axis=1, keepdims=True), axis=0, keepdims=True)
        cand = jnp.where(cur == m, idx, N_EXPERTS)
        pick = idx == jnp.min(jnp.min(cand, axis=1, keepdims=True), axis=0, keepdims=True)
        picks.append(pick)
        e_sel = e_sel | pick
        cur = jnp.where(pick, neg, cur)
    w = jnp.where(e_sel, scores, 0.0)
    gate = w / _sum01(w) * ROUTED_SCALE
    sel_f = e_sel.astype(F32).reshape(N_EXPERTS, t)
    earlier = (lax.broadcasted_iota(I32, (t, t), 0) < lax.broadcasted_iota(I32, (t, t), 1))
    rank = _dot(sel_f.astype(BF16), earlier.astype(BF16))
    base = lax.broadcasted_iota(I32, (N_EXPERTS, 1), 0).astype(F32) * float(cap) + cnt_col
    slot = (base + rank).reshape(N_GROUPS, gsz, t)
    slot8 = jnp.concatenate([_sum01(jnp.where(pk, slot, 0.0)).reshape(1, t) for pk in picks],
                            axis=0).astype(I32)
    w8 = jnp.concatenate([_sum01(jnp.where(pk, gate, 0.0)).reshape(1, t) for pk in picks], axis=0)
    return slot8, w8, cnt_col + jnp.sum(sel_f, axis=1, keepdims=True)


def _outproj_kernel(og_ref, om_ref, x_ref, mod_ref, wtop_ref, wbot_ref, g2_ref, wr_ref, br_ref,
                    x1_ref, h2pa_ref, h2pb_ref, slot_ref, w8_ref, cnt_ref, cnt_scr, *, cap):
    @pl.when((pl.program_id(0) == 0) & (pl.program_id(1) == 0))
    def _():
        cnt_scr[...] = jnp.zeros_like(cnt_scr)

    mod = mod_ref[0]
    mix = _dot(og_ref[0], wtop_ref[...]) + _dot(om_ref[0], wbot_ref[...])
    x1 = x_ref[0] + mod[2:3] * mix
    x1_ref[0] = x1
    h2 = _rms(x1, D_MODEL) * g2_ref[...] * (1.0 + mod[4:5]) + mod[3:4]
    h2pa_ref[0], h2pb_ref[0] = _pack_rows(h2)
    logits_t = _dot_nt_3pass(wr_ref[...], h2)
    slot8, w8, cnt = _route(logits_t, br_ref[...], cnt_scr[:, 0:1], cap)
    slot_ref[...] = slot8
    t = w8.shape[1]
    wb = lax.bitcast_convert_type(w8.astype(BF16).astype(F32), U32)
    wb = wb | (wb >> 16)
    w8_ref[0] = jnp.concatenate(
        [jnp.broadcast_to(wb[k:k + 1], (SC_LANES, t)) for k in range(TOP_K)], axis=0).T
    cnt_scr[...] = jnp.broadcast_to(cnt, cnt_scr.shape)
    cnt_ref[...] = cnt_scr[...]


def _outproj(og, om, x, mod, per_batch_mod, p, tm):
    b, l, d = x.shape
    nt = l // tm
    mod_map = (lambda bi, i: (bi, 0, 0)) if per_batch_mod else (lambda bi, i: (0, 0, 0))
    tok = lambda w: pl.BlockSpec((1, tm, w), lambda bi, i: (bi, i, 0))
    full = lambda a: pl.BlockSpec(a.shape, lambda bi, i: (0, 0))
    weights = [p['w_out_top'], p['w_out_bot'], p['g_norm2'], p['w_router_t'], p['b_router_col']]
    cnt_shape = (N_EXPERTS, LANES)
    return pl.pallas_call(
        functools.partial(_outproj_kernel, cap=_expert_cap(b * l)),
        grid=(b, nt),
        in_specs=[tok(GLA_VAL_W), tok(MLA_W), tok(d), pl.BlockSpec((1, 6, d), mod_map)]
                 + [full(w) for w in weights],
        out_specs=[tok(d), tok(PACK_W), tok(PACK_W),
                   pl.BlockSpec((TOP_K, tm), lambda bi, i: (0, bi * nt + i)),
                   tok(LANES), pl.BlockSpec(cnt_shape, lambda bi, i: (0, 0))],
        out_shape=[jax.ShapeDtypeStruct((b, l, d), F32), jax.ShapeDtypeStruct((b, l, PACK_W), U32),
                   jax.ShapeDtypeStruct((b, l, PACK_W), U32),
                   jax.ShapeDtypeStruct((TOP_K, b * l), I32),
                   jax.ShapeDtypeStruct((b, l, LANES), U32),
                   jax.ShapeDtypeStruct(cnt_shape, F32)],
        scratch_shapes=[pltpu.VMEM(cnt_shape, F32)],
        compiler_params=pltpu.CompilerParams(
            dimension_semantics=("arbitrary", "arbitrary"), vmem_limit_bytes=VMEM_LIMIT),
        name="outproj",
    )(og, om, x, mod, *weights)


def _sc_mesh():
    return plsc.VectorSubcoreMesh(core_axis_name="c", subcore_axis_name="s")


def _sc_scatter_rows(src, slot8, n_rows_out):
    t, d = src.shape
    nk = slot8.shape[0]

    @functools.partial(pl.kernel, out_type=jax.ShapeDtypeStruct((n_rows_out, d), src.dtype),
                       mesh=_sc_mesh(), scratch_types=[], name="dispatch")
    def run(src_hbm, slot_hbm, out_hbm):
        def body(x_vmem, i_vmem):
            for k in range(nk):
                pltpu.sync_copy(x_vmem, out_hbm.at[i_vmem.at[k]])

        pltpu.emit_pipeline(
            body, grid=(t // SC_WINDOW,),
            in_specs=[pl.BlockSpec((SC_WINDOW, d), lambda i: (i, 0)),
                      pl.BlockSpec((nk, SC_WINDOW), lambda i: (0, i))],
            out_specs=[], core_axis_name=("c", "s"),
            dimension_semantics=(pltpu.PARALLEL,))(src_hbm, slot_hbm)

    return run(src, slot8)


def _sc_collect_sum(table, slots_tk, w_rep):
    n_tok = w_rep.shape[0]
    d = table.shape[1]
    group = SC_WINDOW // TOP_K
    chunks = d // SC_LANES

    @functools.partial(pl.kernel, out_type=jax.ShapeDtypeStruct((n_tok, 2 * d), F32),
                       mesh=_sc_mesh(),
                       scratch_types=[pltpu.VMEM((SC_GATHERS, SC_WINDOW, d), table.dtype),
                                      pltpu.SemaphoreType.DMA((SC_GATHERS,))],
                       compiler_params=pltpu.CompilerParams(needs_layout_passes=False),
                       name="collect")
    def run(tab_hbm, slot_hbm, w_hbm, out_hbm, rows_v, sems):
        def gather(i_vmem, b):
            return pltpu.make_async_copy(tab_hbm.at[i_vmem.at[b]], rows_v.at[b], sems.at[b])

        def body(i_vmem, w_vmem, o_vmem):
            for b in range(SC_GATHERS):
                gather(i_vmem, b).start()
            for b in range(SC_GATHERS):
                gather(i_vmem, b).wait()

                @pl.loop(0, group)
                def _(j):
                    tok = b * group + j
                    wv = [plsc.bitcast(w_vmem[tok, pl.ds(k * SC_LANES, SC_LANES)], BF16)
                          for k in range(TOP_K)]

                    def chunk(off):
                        prods = [plsc.bitcast(
                            rows_v[b, j * TOP_K + k, pl.ds(off, SC_LANES)], BF16) * wv[k]
                            for k in range(TOP_K)]
                        lo = jnp.zeros((SC_LANES,), F32)
                        hi = jnp.zeros((SC_LANES,), F32)
                        for k in range(0, TOP_K, 2):
                            u = plsc.bitcast(prods[k] + prods[k + 1], U32)
                            lo = lo + lax.bitcast_convert_type(u << 16, F32)
                            hi = hi + lax.bitcast_convert_type(u & jnp.uint32(0xFFFF0000), F32)
                        o_vmem[tok, pl.ds(off, SC_LANES)] = lo
                        o_vmem[tok, pl.ds(d + off, SC_LANES)] = hi

                    @plsc.parallel_loop(0, chunks, unroll=2)
                    def _(c):
                        chunk(c * SC_LANES)

        step_tok = SC_GATHERS * group
        pltpu.emit_pipeline(
            body, grid=(n_tok // step_tok,),
            in_specs=[pl.BlockSpec((SC_GATHERS, SC_WINDOW), lambda i: (i, 0)),
                      pl.BlockSpec((step_tok, TOP_K * SC_LANES), lambda i: (i, 0))],
            out_specs=[pl.BlockSpec((step_tok, 2 * d), lambda i: (i, 0))],
            core_axis_name=("c", "s"),
            dimension_semantics=(pltpu.PARALLEL,))(slot_hbm, w_hbm, out_hbm)

    return run(table, slots_tk.reshape(-1, SC_WINDOW), w_rep)


def _ffn_kernel(*refs, cap, n_exp):
    for j in range(FFN_GROUP):
        _ffn_expert(pl.program_id(0) * FFN_GROUP + j, j, *refs, cap=cap, n_exp=n_exp)


def _ffn_expert(e, j, cnt_ref, nt_ref, first_ref, te_ref, tt_ref, xsa_hbm, xsb_hbm,
                wg_ref, wu_ref, wd_ref, ysa_hbm, ysb_hbm,
                xbuf, ybuf, wg_b, wu_b, wd_b, sem_in, sem_out, *, cap, n_exp):
    n = nt_ref[e]
    g0 = first_ref[e]
    total = first_ref[n_exp - 1] + nt_ref[n_exp - 1]
    xs_hbm = (xsa_hbm, xsb_hbm)
    ys_hbm = (ysa_hbm, ysb_hbm)
    n_in = FFN_AHEAD + 1
    half = FFN_TILE // 2

    def at_height(ex, t, fn):
        few = cnt_ref[ex] - t * FFN_TILE <= half

        @pl.when(few)
        def _():
            fn(half)

        @pl.when(jnp.logical_not(few))
        def _():
            fn(FFN_TILE)

    def rows(ex, t, n_rows):
        return pl.ds(pl.multiple_of(ex * cap + t * FFN_TILE, FFN_TILE), n_rows)

    def in_copy(ex, t, slot, part, n_rows):
        return pltpu.make_async_copy(xs_hbm[part].at[rows(ex, t, n_rows)],
                                     xbuf.at[slot, part, pl.ds(0, n_rows)],
                                     sem_in.at[slot, part])

    def out_copy(ex, t, slot, part, n_rows):
        return pltpu.make_async_copy(ybuf.at[slot, part, pl.ds(0, n_rows)],
                                     ys_hbm[part].at[rows(ex, t, n_rows)],
                                     sem_out.at[slot, part])

    def start_in(g):
        ex, t = te_ref[g], tt_ref[g]
        at_height(ex, t, lambda nr: [in_copy(ex, t, g % n_in, part, nr).start()
                                     for part in range(PACK_PARTS)])

    def wait_out(g):
        ex, t = te_ref[g], tt_ref[g]
        at_height(ex, t, lambda nr: [out_copy(ex, t, g % 2, part, nr).wait()
                                     for part in range(PACK_PARTS)])

    @pl.when(n > 0)
    def _():
        @pl.when(g0 == 0)
        def _():
            for ahead in range(FFN_AHEAD):
                @pl.when(ahead < total)
                def _():
                    start_in(ahead)

        wg_b[...] = wg_ref[j].astype(BF16)
        wu_b[...] = wu_ref[j].astype(BF16)
        wd_b[...] = wd_ref[j].astype(BF16)

        def tile(t, carry):
            g = g0 + t
            slot = g % 2
            slot_in = g % n_in
            at_height(e, t, lambda nr: [in_copy(e, t, slot_in, part, nr).wait()
                                        for part in range(PACK_PARTS)])

            @pl.when(g + FFN_AHEAD < total)
            def _():
                start_in(g + FFN_AHEAD)

            @pl.when(g >= 2)
            def _():
                wait_out(g - 2)

            def compute(n_rows):
                x = _unpack_rows([xbuf[slot_in, part, 0:n_rows, :]
                                  for part in range(PACK_PARTS)]).astype(BF16)
                a = _silu(_dot(x, wg_b[...])) * _dot(x, wu_b[...])
                y = _pack_rows(_dot(a.astype(BF16), wd_b[...]))
                for part in range(PACK_PARTS):
                    ybuf[slot, part, 0:n_rows, :] = y[part]
                    out_copy(e, t, slot, part, n_rows).start()

            at_height(e, t, compute)
            return carry

        lax.fori_loop(0, n, tile, 0)

        @pl.when(g0 + n == total)
        def _():
            @pl.when(total >= 2)
            def _():
                wait_out(total - 2)

            wait_out(total - 1)


def _ffn(xs_parts, counts, w_gate, w_up, w_down):
    n_exp, d = w_gate.shape[0], w_gate.shape[1]
    cap = xs_parts[0].shape[0] // n_exp
    ntiles = (counts + FFN_TILE - 1) // FFN_TILE
    cum = jnp.cumsum(ntiles)
    first = cum - ntiles
    max_tiles = (cap * TOP_K) // FFN_TILE + n_exp
    g = jnp.arange(max_tiles, dtype=I32)
    done = g[:, None] >= cum[None, :]
    tile_e = jnp.minimum(jnp.sum(done, axis=1), n_exp - 1).astype(I32)
    tile_t = (g - jnp.sum(jnp.where(done, ntiles[None, :], 0), axis=1)).astype(I32)
    wmap = lambda e, cnt, nt, first, te, tt: (e, 0, 0)
    hbm = pl.BlockSpec(memory_space=pl.ANY)
    return pl.pallas_call(
        functools.partial(_ffn_kernel, cap=cap, n_exp=n_exp),
        grid_spec=pltpu.PrefetchScalarGridSpec(
            num_scalar_prefetch=5, grid=(n_exp // FFN_GROUP,),
            in_specs=[hbm] * PACK_PARTS
                     + [pl.BlockSpec((FFN_GROUP, d, D_EXPERT), wmap),
                        pl.BlockSpec((FFN_GROUP, d, D_EXPERT), wmap),
                        pl.BlockSpec((FFN_GROUP, D_EXPERT, d), wmap)],
            out_specs=[hbm] * PACK_PARTS,
            scratch_shapes=[pltpu.VMEM((FFN_AHEAD + 1, PACK_PARTS, FFN_TILE, PACK_W), U32),
                            pltpu.VMEM((2, PACK_PARTS, FFN_TILE, PACK_W), U32),
                            pltpu.VMEM((d, D_EXPERT), BF16), pltpu.VMEM((d, D_EXPERT), BF16),
                            pltpu.VMEM((D_EXPERT, d), BF16),
                            pltpu.SemaphoreType.DMA((FFN_AHEAD + 1, PACK_PARTS)),
                            pltpu.SemaphoreType.DMA((2, PACK_PARTS))]),
        out_shape=[jax.ShapeDtypeStruct(xs_parts[0].shape, U32)] * PACK_PARTS,
        compiler_params=pltpu.CompilerParams(
            dimension_semantics=("arbitrary",), vmem_limit_bytes=VMEM_LIMIT),
        name="ffn",
    )(counts.astype(I32), ntiles.astype(I32), first.astype(I32), tile_e, tile_t, *xs_parts,
      w_gate, w_up, w_down)


def _combine_kernel(x1_ref, h2pa_ref, h2pb_ref, ra_ref, rb_ref, mod_ref,
                    wsg_ref, wsu_ref, wsd_ref, o_ref):
    h2 = _unpack_rows([h2pa_ref[0], h2pb_ref[0]]).astype(BF16)
    a = _silu(_dot(h2, wsg_ref[...])) * _dot(h2, wsu_ref[...])
    routed = jnp.concatenate([ra_ref[0], rb_ref[0]], axis=1)
    o_ref[0] = x1_ref[0] + mod_ref[0][5:6] * (_dot(a.astype(BF16), wsd_ref[...]) + routed)


def _combine(x1, h2p_parts, routed_parts, mod, per_batch_mod, p, tm):
    b, l, d = x1.shape
    mod_map = (lambda bi, i: (bi, 0, 0)) if per_batch_mod else (lambda bi, i: (0, 0, 0))
    tok = lambda w: pl.BlockSpec((1, tm, w), lambda bi, i: (bi, i, 0))
    full = lambda a: pl.BlockSpec(a.shape, lambda bi, i: (0, 0))
    weights = [p['w_sh_gate'], p['w_sh_up'], p['w_sh_down']]
    routed_parts = [r.reshape(b, l, d // PACK_PARTS) for r in routed_parts]
    return pl.pallas_call(
        _combine_kernel,
        grid=(b, l // tm),
        in_specs=[tok(d)] + [tok(PACK_W)] * PACK_PARTS + [tok(d // PACK_PARTS)] * PACK_PARTS
                 + [pl.BlockSpec((1, 6, d), mod_map)] + [full(w) for w in weights],
        out_specs=tok(d),
        out_shape=jax.ShapeDtypeStruct((b, l, d), F32),
        compiler_params=pltpu.CompilerParams(
            dimension_semantics=("parallel", "parallel"), vmem_limit_bytes=VMEM_LIMIT),
        name="combine",
    )(x1, *h2p_parts, *routed_parts, mod, *weights)


def _pad_heads(w, parts):
    k = w.shape[0]
    per = w.shape[1] // MLA_HEADS
    w = w.reshape(k, MLA_HEADS, per)[:, :, parts[0]:parts[1]]
    w = jnp.pad(w, ((0, 0), (0, 0), (0, HEAD_W - (parts[1] - parts[0]))))
    return w.reshape(k, MLA_HEADS * HEAD_W)


def _prep_params(l, g_norm1, g_norm2, w_in, w_gk_fwd, b_gk_fwd, w_gk_bwd, b_gk_bwd, g_gla_out,
                 g_q_lora, w_uq, g_kv_lora, w_ukv, g_qk_q, g_qk_k, w_out, w_router, b_router,
                 w_sh_gate, w_sh_up, w_sh_down):
    w = w_in[l]
    d = w.shape[0]
    o_lrf = COL_CQ
    o_lrb = o_lrf + GLA_GATE_RANK
    o_cq = o_lrb + GLA_GATE_RANK
    o_ckv = o_cq + Q_LORA
    o_kr = o_ckv + KV_LORA
    w_in_p = jnp.concatenate([
        w[:, 0:COL_CQ], w[:, o_cq:o_ckv], w[:, o_ckv:o_kr], jnp.zeros((d, KR_LANE0), w.dtype),
        w[:, o_kr:o_kr + ROPE_DIM], w[:, o_lrf:o_lrb], w[:, o_lrb:o_cq]], axis=1).astype(BF16)
    z = jnp.zeros((GLA_GATE_RANK, GLA_KEY_W), F32)
    w_gk_big = jnp.concatenate([
        jnp.zeros((LANES - 2 * GLA_GATE_RANK, 2 * GLA_KEY_W), F32),
        jnp.concatenate([w_gk_fwd[l], z], axis=1),
        jnp.concatenate([z, w_gk_bwd[l]], axis=1)], axis=0).astype(BF16)
    pad_gain = lambda g: jnp.pad(g, (0, HEAD_W - QK_DIM)).reshape(1, HEAD_W)
    w_ukv_h = w_ukv[l]
    return {
        'g_norm1': g_norm1[l].reshape(1, d), 'g_norm2': g_norm2[l].reshape(1, d),
        'w_in_p': w_in_p, 'w_gk_big': w_gk_big,
        'b_gk': jnp.concatenate([b_gk_fwd[l], b_gk_bwd[l]]).reshape(1, 2 * GLA_KEY_W),
        'g_gla_out': g_gla_out[l].reshape(1, GLA_DV),
        'g_q_lora': g_q_lora[l].reshape(1, Q_LORA),
        'w_uq_p': _pad_heads(w_uq[l], (0, QK_DIM)).astype(BF16),
        'g_kv_lora': g_kv_lora[l].reshape(1, KV_LORA),
        'w_ukv_p': jnp.concatenate([_pad_heads(w_ukv_h, (0, NOPE_DIM)),
                                    _pad_heads(w_ukv_h, (NOPE_DIM, NOPE_DIM + V_DIM))],
                                   axis=1).astype(BF16),
        'gq': pad_gain(g_qk_q[l]), 'gk': pad_gain(g_qk_k[l]),
        'w_out_top': w_out[l][:GLA_VAL_W].astype(BF16),
        'w_out_bot': w_out[l][GLA_VAL_W:].astype(BF16),
        'w_router_t': w_router[l].T, 'b_router_col': b_router[l].reshape(N_EXPERTS, 1),
        'w_sh_gate': w_sh_gate[l].astype(BF16), 'w_sh_up': w_sh_up[l].astype(BF16),
        'w_sh_down': w_sh_down[l].astype(BF16),
    }


def _rope_tables(length):
    pos = np.arange(length)
    r = (pos // GRID_W).astype(np.float32)
    c = (pos % GRID_W).astype(np.float32)
    half = ROPE_DIM // 4
    inv_freq = np.float32(ROPE_THETA) ** (-np.arange(half, dtype=np.float32) / np.float32(half))
    ang_r = r[:, None] * inv_freq[None, :]
    ang_c = c[:, None] * inv_freq[None, :]
    zeros = lambda w: np.zeros((length, w), np.float32)
    ones = lambda w: np.ones((length, w), np.float32)
    tail_w = HEAD_W - ROPE_LANE0 - ROPE_DIM
    cos = np.concatenate([ones(ROPE_LANE0), np.cos(ang_r), np.cos(ang_r), np.cos(ang_c),
                          np.cos(ang_c), ones(tail_w)], axis=1)
    s1 = np.concatenate([zeros(ROPE_LANE0), -np.sin(ang_r), zeros(half), -np.sin(ang_c),
                         zeros(half), zeros(tail_w)], axis=1)
    s2 = np.concatenate([zeros(ROPE_LANE0), zeros(half), np.sin(ang_r), zeros(half),
                         np.sin(ang_c), zeros(tail_w)], axis=1)
    return tuple(jnp.asarray(t, F32) for t in (cos, s1, s2))


def _mix_route_dispatch(x_tok, seq_shape, mod, per_batch_mod, p, s0f, s0b, rope_tabs, ctx_kv,
                        tq):
    bt, lt, d = x_tok.shape
    b, l = seq_shape
    n_tok = bt * lt
    (q, k, v, g, lgf, lgb, qm, km, vm, ckvn, kr) = _inproj(x_tok, mod, per_batch_mod, p,
                                                            rope_tabs, IN_TILE)
    seq = lambda a: a.reshape(b, l, a.shape[-1])
    og, sf, sb = _gla(seq(q), seq(k), seq(v), seq(g), seq(lgf), seq(lgb), s0f, s0b,
                      p['g_gla_out'], GLA_SEQS if l == GLA_BLOCK else 1)
    kv_pairs = [(seq(km), seq(vm))] + ([ctx_kv] if ctx_kv is not None else [])
    om = _attn(seq(qm), kv_pairs, tq, ATTN_SEQS if l == tq else 1)
    tokv = lambda a: a.reshape(bt, lt, a.shape[-1])
    x1, h2a, h2b, slot8, w8, cnt = _outproj(tokv(og), tokv(om), x_tok, mod, per_batch_mod, p,
                                            OUT_TILE)
    x_sorted = [_sc_scatter_rows(h.reshape(n_tok, PACK_W), slot8,
                                 N_EXPERTS * _expert_cap(n_tok)) for h in (h2a, h2b)]
    routed = dict(x1=x1, h2=(h2a, h2b), slot8=slot8, w8=w8, cnt=cnt, x_sorted=x_sorted)
    return routed, sf, sb, ckvn, kr


def _experts_collect(r, w_gate, w_up, w_down):
    n_tok = r['slot8'].shape[1]
    y_sorted = _ffn(r['x_sorted'], r['cnt'][:, 0].astype(I32), w_gate, w_up, w_down)
    slots_tk = r['slot8'].T.reshape(1, TOP_K * n_tok)
    w_rep = r['w8'].reshape(n_tok, LANES)
    return [_sc_collect_sum(y, slots_tk, w_rep) for y in y_sorted]


def kernel(x_prompt, x_sample, c, state_gla_fwd, state_gla_bwd, cache_mla_ckv, cache_mla_krope,
           c_ctx, w_ada, b_ada, g_norm1, g_norm2, w_in, w_gk_fwd, b_gk_fwd, w_gk_bwd, b_gk_bwd,
           g_gla_out, g_q_lora, w_uq, g_kv_lora, w_ukv, g_qk_q, g_qk_k, w_out,
           w_router, b_router, w_exp_gate, w_exp_up, w_exp_down, w_sh_gate, w_sh_up, w_sh_down):
    bp, lp, d = x_prompt.shape
    bs, ls, _ = x_sample.shape
    depth = w_ada.shape[0]
    xp = x_prompt.reshape(1, bp * lp, d)
    xs = x_sample
    new_f, new_b, new_ckv, new_kr = [], [], [], []
    lat_tabs = _rope_tables(ls)
    cvecs = jnp.concatenate([c_ctx[None], c, jnp.zeros((8 - 1 - bs, d), F32)], axis=0)
    for l in range(depth):
        p = _prep_params(l, g_norm1, g_norm2, w_in, w_gk_fwd, b_gk_fwd, w_gk_bwd, b_gk_bwd,
                         g_gla_out, g_q_lora, w_uq, g_kv_lora, w_ukv, g_qk_q, g_qk_k, w_out,
                         w_router, b_router, w_sh_gate, w_sh_up, w_sh_down)
        mod = _ada(cvecs, w_ada[l], b_ada[l]).reshape(8, 6, d)
        mod_p, mod_s = mod[0:1], mod[1:1 + bs]
        zeros = jnp.zeros((bp, GLA_KEY_W, GLA_DV), F32)
        r_ctx, sf, sb, ckvn, kr = _mix_route_dispatch(
            xp, (bp, lp), mod_p, False, p, zeros, zeros, (), None, lp)
        new_f.append(sf.reshape(bp, GLA_HEADS, GLA_DK, GLA_DV))
        new_b.append(sb.reshape(bp, GLA_HEADS, GLA_DK, GLA_DV))
        new_ckv.append(ckvn.reshape(bp, lp, KV_LORA))
        new_kr.append(kr.reshape(bp, lp, LANES)[:, :, KR_LANE0:KR_LANE0 + ROPE_DIM])
        kr_cache = jnp.pad(cache_mla_krope[:, l],
                           ((0, 0), (0, 0), (ROPE_LANE0, LANES - ROPE_LANE0 - ROPE_DIM)))
        ctx_kv = _cache_kv(cache_mla_ckv[:, l], kr_cache, p)
        s0f = state_gla_fwd[:, l].reshape(bs, GLA_KEY_W, GLA_DV)
        s0b = state_gla_bwd[:, l].reshape(bs, GLA_KEY_W, GLA_DV)
        r_lat, _, _, _, _ = _mix_route_dispatch(
            xs, (bs, ls), mod_s, True, p, s0f, s0b, lat_tabs, ctx_kv, ATTN_TILE)
        experts = (w_exp_gate[l], w_exp_up[l], w_exp_down[l])
        routed_ctx = _experts_collect(r_ctx, *experts)
        routed_lat = _experts_collect(r_lat, *experts)
        xp = _combine(r_ctx['x1'], r_ctx['h2'], routed_ctx, mod_p, False, p, OUT_TILE)
        xs = _combine(r_lat['x1'], r_lat['h2'], routed_lat, mod_s, True, p, OUT_TILE)
    return (xp.reshape(bp, lp, d), xs, jnp.stack(new_f, axis=1), jnp.stack(new_b, axis=1),
            jnp.stack(new_ckv, axis=1), jnp.stack(new_kr, axis=1))
```

```python
import functools

import jax
import jax.numpy as jnp
import numpy as np
from jax import lax
from jax.experimental import pallas as pl
from jax.experimental.pallas import tpu as pltpu
from jax.experimental.pallas import tpu_sc as plsc

F32 = jnp.float32
BF16 = jnp.bfloat16
I32 = jnp.int32
U32 = jnp.uint32

D_MODEL = 1024
EPS = 1e-6
GRID_W = 64
GLA_HEADS = 4
GLA_DK = 64
GLA_DV = 128
GLA_GATE_RANK = 16
GLA_GATE_NORM = 16.0
GLA_KEY_W = GLA_HEADS * GLA_DK
GLA_VAL_W = GLA_HEADS * GLA_DV
MLA_HEADS = 4
Q_LORA = 256
KV_LORA = 128
NOPE_DIM = 64
ROPE_DIM = 32
V_DIM = 128
QK_DIM = NOPE_DIM + ROPE_DIM
ROPE_THETA = 10000.0
N_EXPERTS = 64
TOP_K = 8
N_GROUPS = 8
TOPK_GROUPS = 4
D_EXPERT = 256
ROUTED_SCALE = 2.5

LANES = 128
HEAD_W = LANES
MLA_W = MLA_HEADS * HEAD_W
ROPE_LANE0 = NOPE_DIM
COL_V = 2 * GLA_KEY_W
COL_G = COL_V + GLA_VAL_W
COL_CQ = COL_G + GLA_VAL_W
COL_CKV = COL_CQ + Q_LORA
TAIL0 = COL_CKV + KV_LORA
IN_W = TAIL0 + LANES
KR_LANE0 = LANES - ROPE_DIM - 2 * GLA_GATE_RANK
GLA_BLOCK = 256
GLA_SUB = 64
GLA_SEQS = 2
ATTN_SEQS = 4
ATTN_TILE = 1024
ADA_TILE = 768
IN_TILE = 1024
IN_SUB = 256
OUT_TILE = 1024
FFN_TILE = 512
FFN_AHEAD = 5
FFN_GROUP = 2
PACK_PARTS = 2
PACK_W = D_MODEL // (2 * PACK_PARTS)
SC_WINDOW = 128
SC_LANES = 16
SC_GATHERS = 2
VMEM_LIMIT = 56 * 1024 * 1024

HIGHEST = lax.Precision.HIGHEST


def _dot(a, b, precision=None):
    return jnp.dot(a, b, preferred_element_type=F32, precision=precision)


def _dot_nt(a, b, precision=None):
    return lax.dot_general(a, b, (((1,), (1,)), ((), ())), preferred_element_type=F32,
                           precision=precision)


def _dot_nt_3pass(a, b):
    a_hi = a.astype(BF16)
    b_hi = b.astype(BF16)
    a_lo = (a - a_hi.astype(F32)).astype(BF16)
    b_lo = (b - b_hi.astype(F32)).astype(BF16)
    return _dot_nt(a_hi, b_hi) + _dot_nt(a_hi, b_lo) + _dot_nt(a_lo, b_hi)


def _rms(x, width):
    ss = jnp.sum(x * x, axis=-1, keepdims=True) * (1.0 / width)
    return x * lax.rsqrt(ss + EPS)


def _silu(x):
    return x * jax.nn.sigmoid(x)


def _log_sigmoid(x):
    return jnp.minimum(x, 0.0) - jnp.log1p(jnp.exp(-jnp.abs(x)))


def _pack_rows(x):
    parts = []
    for i in range(PACK_PARTS):
        c0 = i * 2 * PACK_W
        lo = lax.bitcast_convert_type(x[:, c0:c0 + PACK_W].astype(BF16).astype(F32), U32)
        hi = lax.bitcast_convert_type(
            x[:, c0 + PACK_W:c0 + 2 * PACK_W].astype(BF16).astype(F32), U32)
        parts.append(hi | (lo >> 16))
    return parts


def _unpack_rows(parts):
    cols = []
    for w in parts:
        cols.append(lax.bitcast_convert_type(w << 16, F32))
        cols.append(lax.bitcast_convert_type(w & jnp.uint32(0xFFFF0000), F32))
    return jnp.concatenate(cols, axis=1)


def _ada_kernel(c_ref, w_ref, b_ref, o_ref):
    o_ref[...] = _dot(_silu(c_ref[...]), w_ref[...], precision=HIGHEST) + b_ref[...]


def _ada(cvecs, w_ada, b_ada):
    n = w_ada.shape[1]
    tn = ADA_TILE
    return pl.pallas_call(
        _ada_kernel,
        grid=(n // tn,),
        in_specs=[pl.BlockSpec((8, D_MODEL), lambda j: (0, 0)),
                  pl.BlockSpec((D_MODEL, tn), lambda j: (0, j)),
                  pl.BlockSpec((1, tn), lambda j: (0, j))],
        out_specs=pl.BlockSpec((8, tn), lambda j: (0, j)),
        out_shape=jax.ShapeDtypeStruct((8, n), F32),
        name="ada",
    )(cvecs, w_ada, b_ada.reshape(1, n))


def _rope(x, c, s1, s2):
    return x * c + pltpu.roll(x, LANES - 8, 1) * s1 + pltpu.roll(x, 8, 1) * s2


def _mla_kv(ckv_n, kr, w_ukv_ref, gk, rope_tabs, k_ref, v_ref, rs=slice(None)):
    kv = _dot(ckv_n.astype(BF16), w_ukv_ref[...])
    for h in range(MLA_HEADS):
        k_h = kv[:, h * HEAD_W:(h + 1) * HEAD_W] + kr
        k_h = _rms(k_h, QK_DIM) * gk
        if rope_tabs is not None:
            k_h = _rope(k_h, *rope_tabs)
        k_ref[0, rs, h * HEAD_W:(h + 1) * HEAD_W] = k_h.astype(BF16)
    v_ref[0, rs, :] = kv[:, MLA_W:].astype(BF16)


def _lane_mask(lo, hi, rows):
    lane = lax.broadcasted_iota(I32, (rows, LANES), 1)
    return (lane >= lo) & (lane < hi)


def _inproj_kernel(x_ref, mod_ref, g1_ref, win_ref, wgk_ref, bgk_ref, gql_ref, wuq_ref,
                   gkv_ref, wukv_ref, gq_ref, gk_ref, *refs, positions):
    tab_refs, outs = (refs[:3], refs[3:]) if positions else ((), refs)
    (q_ref, k_ref, v_ref, g_ref, lgf_ref, lgb_ref, qm_ref, km_ref, vm_ref, ckvn_ref,
     kr_ref) = outs
    mod = mod_ref[0]
    for r0 in range(0, x_ref.shape[1], IN_SUB):
        rs = slice(r0, r0 + IN_SUB)
        x = x_ref[0, rs, :]
        h = _rms(x, D_MODEL) * g1_ref[...] * (1.0 + mod[1:2]) + mod[0:1]
        y = _dot(h.astype(BF16), win_ref[...])
        q_ref[0, rs, :] = (y[:, 0:GLA_KEY_W] * GLA_DK ** -0.5).astype(BF16)
        k_ref[0, rs, :] = y[:, GLA_KEY_W:COL_V].astype(BF16)
        v_ref[0, rs, :] = y[:, COL_V:COL_G].astype(BF16)
        g_ref[0, rs, :] = y[:, COL_G:COL_CQ].astype(BF16)
        tail = y[:, TAIL0:IN_W]
        pre = _dot(tail.astype(BF16), wgk_ref[...]) + bgk_ref[...]
        logg = _log_sigmoid(pre) * (1.0 / GLA_GATE_NORM)
        lgf_ref[0, rs, :] = logg[:, 0:GLA_KEY_W]
        lgb_ref[0, rs, :] = logg[:, GLA_KEY_W:]
        tabs = tuple(r[rs, :] for r in tab_refs) if positions else None
        cq = _rms(y[:, COL_CQ:COL_CKV], Q_LORA) * gql_ref[...]
        qm = _dot(cq.astype(BF16), wuq_ref[...])
        gq = gq_ref[...]
        for hh in range(MLA_HEADS):
            q_h = _rms(qm[:, hh * HEAD_W:(hh + 1) * HEAD_W], QK_DIM) * gq
            if positions:
                q_h = _rope(q_h, *tabs)
            qm_ref[0, rs, hh * HEAD_W:(hh + 1) * HEAD_W] = (q_h * QK_DIM ** -0.5).astype(BF16)
        ckv_n = _rms(y[:, COL_CKV:TAIL0], KV_LORA) * gkv_ref[...]
        ckvn_ref[0, rs, :] = ckv_n
        kr_ref[0, rs, :] = tail
        kr = jnp.where(_lane_mask(KR_LANE0, KR_LANE0 + ROPE_DIM, IN_SUB), tail, 0.0)
        _mla_kv(ckv_n, kr, wukv_ref, gk_ref[...], tabs, km_ref, vm_ref, rs)


def _inproj(x, mod, per_batch_mod, p, rope_tabs, tm):
    b, l, d = x.shape
    nt = l // tm
    mod_map = (lambda bi, i: (bi, 0, 0)) if per_batch_mod else (lambda bi, i: (0, 0, 0))
    tab_map = lambda bi, i: (i, 0)
    const = lambda bi, i: (0, 0)
    tok = lambda w: pl.BlockSpec((1, tm, w), lambda bi, i: (bi, i, 0))
    full = lambda a: pl.BlockSpec(a.shape, const)
    weights = [p['g_norm1'], p['w_in_p'], p['w_gk_big'], p['b_gk'], p['g_q_lora'], p['w_uq_p'],
               p['g_kv_lora'], p['w_ukv_p'], p['gq'], p['gk']]
    outs = [(GLA_KEY_W, BF16), (GLA_KEY_W, BF16), (GLA_VAL_W, BF16), (GLA_VAL_W, BF16),
            (GLA_KEY_W, F32), (GLA_KEY_W, F32), (MLA_W, BF16), (MLA_W, BF16), (MLA_W, BF16),
            (KV_LORA, F32), (LANES, F32)]
    return pl.pallas_call(
        functools.partial(_inproj_kernel, positions=bool(rope_tabs)),
        grid=(b, nt),
        in_specs=[tok(d), pl.BlockSpec((1, 6, d), mod_map)] + [full(w) for w in weights]
                 + [pl.BlockSpec((tm, LANES), tab_map)] * len(rope_tabs),
        out_specs=[tok(w) for w, _ in outs],
        out_shape=[jax.ShapeDtypeStruct((b, l, w), dt) for w, dt in outs],
        compiler_params=pltpu.CompilerParams(
            dimension_semantics=("parallel", "parallel"), vmem_limit_bytes=VMEM_LIMIT),
        name="inproj",
    )(x, mod, *weights, *rope_tabs)


def _cache_kv_kernel(ckv_ref, kr_ref, wukv_ref, gk_ref, k_ref, v_ref):
    _mla_kv(ckv_ref[0], kr_ref[0], wukv_ref, gk_ref[...], None, k_ref, v_ref)


def _cache_kv(ckv, kr128, p):
    b, l, _ = ckv.shape
    tok = lambda w: pl.BlockSpec((1, l, w), lambda bi: (bi, 0, 0))
    full = lambda a: pl.BlockSpec(a.shape, lambda bi: (0, 0))
    return pl.pallas_call(
        _cache_kv_kernel,
        grid=(b,),
        in_specs=[tok(KV_LORA), tok(LANES), full(p['w_ukv_p']), full(p['gk'])],
        out_specs=[tok(MLA_W), tok(MLA_W)],
        out_shape=[jax.ShapeDtypeStruct((b, l, MLA_W), BF16)] * 2,
        name="cache_kv",
    )(ckv, kr128, p['w_ukv_p'], p['gk'])


def _split3(x):
    hi = x.astype(BF16)
    r1 = x - hi.astype(F32)
    mid = r1.astype(BF16)
    lo = (r1 - mid.astype(F32)).astype(BF16)
    return hi, mid, lo


def _gla_block(q, k, v, lg, s, fwd):
    n = GLA_BLOCK
    nc = n // GLA_SUB
    row = lax.broadcasted_iota(I32, (n, n), 0)
    col = lax.broadcasted_iota(I32, (n, n), 1)
    same = (row // GLA_SUB) == (col // GLA_SUB)
    causal = same & ((col <= row) if fwd else (col >= row))
    tri = causal.astype(BF16)
    hi, mid, lo = _split3(lg)
    cum = _dot(tri, hi) + _dot(tri, mid) + _dot(tri, lo)
    tot_rows, mid_rows = [], []
    for c in range(nc):
        r_tot = c * GLA_SUB + (GLA_SUB - 1 if fwd else 0)
        r_mid = c * GLA_SUB + GLA_SUB // 2
        tot_rows.append(jnp.broadcast_to(cum[r_tot:r_tot + 1], (GLA_SUB, GLA_KEY_W)))
        mid_rows.append(jnp.broadcast_to(cum[r_mid:r_mid + 1], (GLA_SUB, GLA_KEY_W)))
    tot_rows = jnp.concatenate(tot_rows, axis=0)
    mid_rows = jnp.concatenate(mid_rows, axis=0)
    rel = cum - mid_rows
    qi = q * jnp.exp(rel)
    ki = (k * jnp.exp(-rel)).astype(BF16)
    q_in = (q * jnp.exp(cum)).astype(BF16)
    k_up = k * jnp.exp(tot_rows - cum)
    dec = jnp.exp(tot_rows)
    k_up_t = k_up.T.astype(BF16)
    dec_t = dec.T
    o_heads = []
    for h in range(GLA_HEADS):
        hk = slice(h * GLA_DK, (h + 1) * GLA_DK)
        a = _dot_nt(qi[:, hk].astype(BF16), ki[:, hk])
        a = jnp.where(causal, a, 0.0).astype(BF16)
        o_heads.append(_dot(a, v[:, h * GLA_DV:(h + 1) * GLA_DV]))
    o = jnp.concatenate(o_heads, axis=1)
    o_inter = [None] * nc
    for c in (range(nc) if fwd else range(nc - 1, -1, -1)):
        rc = slice(c * GLA_SUB, (c + 1) * GLA_SUB)
        s_b = s.astype(BF16)
        o_inter[c] = jnp.concatenate(
            [_dot(q_in[rc, h * GLA_DK:(h + 1) * GLA_DK], s_b[h * GLA_DK:(h + 1) * GLA_DK])
             for h in range(GLA_HEADS)], axis=1)
        u = jnp.concatenate(
            [_dot(k_up_t[h * GLA_DK:(h + 1) * GLA_DK, rc], v[rc, h * GLA_DV:(h + 1) * GLA_DV])
             for h in range(GLA_HEADS)], axis=0)
        s = dec_t[:, c * GLA_SUB:c * GLA_SUB + 1] * s + u
    return o + jnp.concatenate(o_inter, axis=0), s


def _gla_kernel(q_ref, k_ref, v_ref, g_ref, lgf_ref, lgb_ref, s0f_ref, s0b_ref, gout_ref,
                o_ref, sf_ref, sb_ref, acc_ref, st_ref, *, nblk, n_seq):
    for si in range(n_seq):
        _gla_sequence(si, q_ref, k_ref, v_ref, g_ref, lgf_ref, lgb_ref, s0f_ref, s0b_ref,
                      gout_ref, o_ref, sf_ref, sb_ref, acc_ref, st_ref, nblk)


def _gla_sequence(si, q_ref, k_ref, v_ref, g_ref, lgf_ref, lgb_ref, s0f_ref, s0b_ref, gout_ref,
                  o_ref, sf_ref, sb_ref, acc_ref, st_ref, nblk):
    st_ref[si, 0] = s0f_ref[si]
    st_ref[si, 1] = s0b_ref[si]

    def load(blk):
        start = blk * GLA_BLOCK
        r = pl.ds(start if isinstance(blk, int) else pl.multiple_of(start, GLA_BLOCK), GLA_BLOCK)
        return r, q_ref[si, r, :].astype(F32), k_ref[si, r, :].astype(F32), v_ref[si, r, :]

    def pair_step(i, first_touch):
        rf, qf, kf, vf = load(i)
        rb, qb, kb, vb = load(nblk - 1 - i)
        of, s_f = _gla_block(qf, kf, vf, lgf_ref[si, rf, :], st_ref[si, 0], True)
        ob, s_b = _gla_block(qb, kb, vb, lgb_ref[si, rb, :], st_ref[si, 1], False)
        if first_touch:
            acc_ref[si, rf, :] = of
            acc_ref[si, rb, :] = ob
        else:
            acc_ref[si, rf, :] += of
            acc_ref[si, rb, :] += ob
        st_ref[si, 0] = s_f
        st_ref[si, 1] = s_b

    if nblk == 1:
        r, q, k, v = load(0)
        of, s_f = _gla_block(q, k, v, lgf_ref[si, r, :], st_ref[si, 0], True)
        ob, s_b = _gla_block(q, k, v, lgb_ref[si, r, :], st_ref[si, 1], False)
        acc_ref[si, r, :] = of + ob
        st_ref[si, 0] = s_f
        st_ref[si, 1] = s_b
    else:
        half = nblk // 2
        lax.fori_loop(0, half, lambda i, c: (pair_step(i, True), c)[1], 0)
        lax.fori_loop(half, nblk, lambda i, c: (pair_step(i, False), c)[1], 0)
    sf_ref[si] = st_ref[si, 0]
    sb_ref[si] = st_ref[si, 1]

    def fin_step(blk, carry):
        r = pl.ds(pl.multiple_of(blk * GLA_BLOCK, GLA_BLOCK), GLA_BLOCK)
        o = acc_ref[si, r, :]
        gate = _silu(g_ref[si, r, :].astype(F32))
        for h in range(GLA_HEADS):
            sl = slice(h * GLA_DV, (h + 1) * GLA_DV)
            o_ref[si, r, sl] = (_rms(o[:, sl], GLA_DV) * gout_ref[...] * gate[:, sl]).astype(BF16)
        return carry

    lax.fori_loop(0, nblk, fin_step, 0)


def _gla(q, k, v, g, lgf, lgb, s0f, s0b, g_out, n_seq):
    b, l, _ = q.shape
    assert l == GLA_BLOCK or l % (2 * GLA_BLOCK) == 0
    seq = lambda w: pl.BlockSpec((n_seq, l, w), lambda bi: (bi, 0, 0))
    st = pl.BlockSpec((n_seq, GLA_KEY_W, GLA_DV), lambda bi: (bi, 0, 0))
    return pl.pallas_call(
        functools.partial(_gla_kernel, nblk=l // GLA_BLOCK, n_seq=n_seq),
        grid=(b // n_seq,),
        in_specs=[seq(GLA_KEY_W), seq(GLA_KEY_W), seq(GLA_VAL_W), seq(GLA_VAL_W),
                  seq(GLA_KEY_W), seq(GLA_KEY_W), st, st,
                  pl.BlockSpec((1, GLA_DV), lambda bi: (0, 0))],
        out_specs=[seq(GLA_VAL_W), st, st],
        out_shape=[jax.ShapeDtypeStruct((b, l, GLA_VAL_W), BF16),
                   jax.ShapeDtypeStruct((b, GLA_KEY_W, GLA_DV), F32),
                   jax.ShapeDtypeStruct((b, GLA_KEY_W, GLA_DV), F32)],
        scratch_shapes=[pltpu.VMEM((n_seq, l, GLA_VAL_W), F32),
                        pltpu.VMEM((n_seq, 2, GLA_KEY_W, GLA_DV), F32)],
        compiler_params=pltpu.CompilerParams(
            dimension_semantics=("parallel",), vmem_limit_bytes=VMEM_LIMIT),
        name="gla",
    )(q, k, v, g, lgf, lgb, s0f, s0b, g_out)


def _attn_kernel(*refs, n_kv, n_seq):
    q_ref, o_ref = refs[0], refs[-1]
    kv = [(refs[1 + 2 * i], refs[2 + 2 * i]) for i in range(n_kv)]
    for si in range(n_seq):
        for h in range(MLA_HEADS):
            sl = slice(h * HEAD_W, (h + 1) * HEAD_W)
            q = q_ref[si, :, sl]
            scores = [_dot_nt(q, k_ref[si, :, sl]) for k_ref, _ in kv]
            m = functools.reduce(jnp.maximum,
                                 [jnp.max(s, axis=-1, keepdims=True) for s in scores])
            o, den = 0.0, 0.0
            for s, (_, v_ref) in zip(scores, kv):
                p = jnp.exp(s - m)
                den = den + jnp.sum(p, axis=-1, keepdims=True)
                o = o + _dot(p.astype(BF16), v_ref[si, :, sl])
            o_ref[si, :, sl] = (o / den).astype(BF16)


def _attn(q, kv_pairs, tq, n_seq):
    b, l, _ = q.shape
    kv_specs, kv_args = [], []
    for k, v in kv_pairs:
        spec = pl.BlockSpec((n_seq, k.shape[1], MLA_W), lambda bi, i: (bi, 0, 0))
        kv_specs += [spec, spec]
        kv_args += [k, v]
    return pl.pallas_call(
        functools.partial(_attn_kernel, n_kv=len(kv_pairs), n_seq=n_seq),
        grid=(b // n_seq, l // tq),
        in_specs=[pl.BlockSpec((n_seq, tq, MLA_W), lambda bi, i: (bi, i, 0))] + kv_specs,
        out_specs=pl.BlockSpec((n_seq, tq, MLA_W), lambda bi, i: (bi, i, 0)),
        out_shape=jax.ShapeDtypeStruct((b, l, MLA_W), BF16),
        compiler_params=pltpu.CompilerParams(
            dimension_semantics=("parallel", "parallel"), vmem_limit_bytes=VMEM_LIMIT),
        name="attn",
    )(q, *kv_args)


def _expert_cap(n_tok):
    return -(-n_tok // FFN_TILE) * FFN_TILE


def _sum01(x):
    return jnp.sum(jnp.sum(x, axis=1, keepdims=True), axis=0, keepdims=True)


def _route(logits_t, bias_col, cnt_col, cap):
    t = logits_t.shape[1]
    gsz = N_EXPERTS // N_GROUPS
    scores = jax.nn.sigmoid(logits_t)
    sel = (scores + bias_col).reshape(N_GROUPS, gsz, t)
    scores = scores.reshape(N_GROUPS, gsz, t)
    neg = jnp.float32(-jnp.inf)
    ie = lax.broadcasted_iota(I32, (N_GROUPS, gsz, t), 1)
    ig = lax.broadcasted_iota(I32, (N_GROUPS, gsz, t), 0)
    m1 = jnp.max(sel, axis=1, keepdims=True)
    first = jnp.min(jnp.where(sel == m1, ie, gsz), axis=1, keepdims=True)
    m2 = jnp.max(jnp.where(ie == first, neg, sel), axis=1, keepdims=True)
    grp = m1 + m2
    igk = lax.broadcasted_iota(I32, (N_GROUPS, 1, t), 0)
    g_sel = jnp.zeros((N_GROUPS, 1, t), jnp.bool_)
    cur = grp
    for _ in range(TOPK_GROUPS):
        m = jnp.max(cur, axis=0, keepdims=True)
        pick = igk == jnp.min(jnp.where(cur == m, igk, N_GROUPS), axis=0, keepdims=True)
        g_sel = g_sel | pick
        cur = jnp.where(pick, neg, cur)
    cur = jnp.where(g_sel, sel, neg)
    idx = ig * gsz + ie
    e_sel = jnp.zeros((N_GROUPS, gsz, t), jnp.bool_)
    picks = []
    for _ in range(TOP_K):
        m = jnp.max(jnp.max(cur, axis=1, keepdims=True), axis=0, keepdims=True)
        cand = jnp.where(cur == m, idx, N_EXPERTS)
        pick = idx == jnp.min(jnp.min(cand, axis=1, keepdims=True), axis=0, keepdims=True)
        picks.append(pick)
        e_sel = e_sel | pick
        cur = jnp.where(pick, neg, cur)
    w = jnp.where(e_sel, scores, 0.0)
    gate = w / _sum01(w) * ROUTED_SCALE
    sel_f = e_sel.astype(F32).reshape(N_EXPERTS, t)
    earlier = (lax.broadcasted_iota(I32, (t, t), 0) < lax.broadcasted_iota(I32, (t, t), 1))
    rank = _dot(sel_f.astype(BF16), earlier.astype(BF16))
    base = lax.broadcasted_iota(I32, (N_EXPERTS, 1), 0).astype(F32) * float(cap) + cnt_col
    slot = (base + rank).reshape(N_GROUPS, gsz, t)
    slot8 = jnp.concatenate([_sum01(jnp.where(pk, slot, 0.0)).reshape(1, t) for pk in picks],
                            axis=0).astype(I32)
    w8 = jnp.concatenate([_sum01(jnp.where(pk, gate, 0.0)).reshape(1, t) for pk in picks], axis=0)
    return slot8, w8, cnt_col + jnp.sum(sel_f, axis=1, keepdims=True)


def _outproj_kernel(og_ref, om_ref, x_ref, mod_ref, wtop_ref, wbot_ref, g2_ref, wr_ref, br_ref,
                    x1_ref, h2pa_ref, h2pb_ref, slot_ref, w8_ref, cnt_ref, cnt_scr, *, cap):
    @pl.when((pl.program_id(0) == 0) & (pl.program_id(1) == 0))
    def _():
        cnt_scr[...] = jnp.zeros_like(cnt_scr)

    mod = mod_ref[0]
    mix = _dot(og_ref[0], wtop_ref[...]) + _dot(om_ref[0], wbot_ref[...])
    x1 = x_ref[0] + mod[2:3] * mix
    x1_ref[0] = x1
    h2 = _rms(x1, D_MODEL) * g2_ref[...] * (1.0 + mod[4:5]) + mod[3:4]
    h2pa_ref[0], h2pb_ref[0] = _pack_rows(h2)
    logits_t = _dot_nt_3pass(wr_ref[...], h2)
    slot8, w8, cnt = _route(logits_t, br_ref[...], cnt_scr[:, 0:1], cap)
    slot_ref[...] = slot8
    t = w8.shape[1]
    wb = lax.bitcast_convert_type(w8.astype(BF16).astype(F32), U32)
    wb = wb | (wb >> 16)
    w8_ref[0] = jnp.concatenate(
        [jnp.broadcast_to(wb[k:k + 1], (SC_LANES, t)) for k in range(TOP_K)], axis=0).T
    cnt_scr[...] = jnp.broadcast_to(cnt, cnt_scr.shape)
    cnt_ref[...] = cnt_scr[...]


def _outproj(og, om, x, mod, per_batch_mod, p, tm):
    b, l, d = x.shape
    nt = l // tm
    mod_map = (lambda bi, i: (bi, 0, 0)) if per_batch_mod else (lambda bi, i: (0, 0, 0))
    tok = lambda w: pl.BlockSpec((1, tm, w), lambda bi, i: (bi, i, 0))
    full = lambda a: pl.BlockSpec(a.shape, lambda bi, i: (0, 0))
    weights = [p['w_out_top'], p['w_out_bot'], p['g_norm2'], p['w_router_t'], p['b_router_col']]
    cnt_shape = (N_EXPERTS, LANES)
    return pl.pallas_call(
        functools.partial(_outproj_kernel, cap=_expert_cap(b * l)),
        grid=(b, nt),
        in_specs=[tok(GLA_VAL_W), tok(MLA_W), tok(d), pl.BlockSpec((1, 6, d), mod_map)]
                 + [full(w) for w in weights],
        out_specs=[tok(d), tok(PACK_W), tok(PACK_W),
                   pl.BlockSpec((TOP_K, tm), lambda bi, i: (0, bi * nt + i)),
                   tok(LANES), pl.BlockSpec(cnt_shape, lambda bi, i: (0, 0))],
        out_shape=[jax.ShapeDtypeStruct((b, l, d), F32), jax.ShapeDtypeStruct((b, l, PACK_W), U32),
                   jax.ShapeDtypeStruct((b, l, PACK_W), U32),
                   jax.ShapeDtypeStruct((TOP_K, b * l), I32),
                   jax.ShapeDtypeStruct((b, l, LANES), U32),
                   jax.ShapeDtypeStruct(cnt_shape, F32)],
        scratch_shapes=[pltpu.VMEM(cnt_shape, F32)],
        compiler_params=pltpu.CompilerParams(
            dimension_semantics=("arbitrary", "arbitrary"), vmem_limit_bytes=VMEM_LIMIT),
        name="outproj",
    )(og, om, x, mod, *weights)


def _sc_mesh():
    return plsc.VectorSubcoreMesh(core_axis_name="c", subcore_axis_name="s")


def _sc_scatter_rows(src, slot8, n_rows_out):
    t, d = src.shape
    nk = slot8.shape[0]

    @functools.partial(pl.kernel, out_type=jax.ShapeDtypeStruct((n_rows_out, d), src.dtype),
                       mesh=_sc_mesh(), scratch_types=[], name="dispatch")
    def run(src_hbm, slot_hbm, out_hbm):
        def body(x_vmem, i_vmem):
            for k in range(nk):
                pltpu.sync_copy(x_vmem, out_hbm.at[i_vmem.at[k]])

        pltpu.emit_pipeline(
            body, grid=(t // SC_WINDOW,),
            in_specs=[pl.BlockSpec((SC_WINDOW, d), lambda i: (i, 0)),
                      pl.BlockSpec((nk, SC_WINDOW), lambda i: (0, i))],
            out_specs=[], core_axis_name=("c", "s"),
            dimension_semantics=(pltpu.PARALLEL,))(src_hbm, slot_hbm)

    return run(src, slot8)


def _sc_collect_sum(table, slots_tk, w_rep):
    n_tok = w_rep.shape[0]
    d = table.shape[1]
    group = SC_WINDOW // TOP_K
    chunks = d // SC_LANES

    @functools.partial(pl.kernel, out_type=jax.ShapeDtypeStruct((n_tok, 2 * d), F32),
                       mesh=_sc_mesh(),
                       scratch_types=[pltpu.VMEM((SC_GATHERS, SC_WINDOW, d), table.dtype),
                                      pltpu.SemaphoreType.DMA((SC_GATHERS,))],
                       compiler_params=pltpu.CompilerParams(needs_layout_passes=False),
                       name="collect")
    def run(tab_hbm, slot_hbm, w_hbm, out_hbm, rows_v, sems):
        def gather(i_vmem, b):
            return pltpu.make_async_copy(tab_hbm.at[i_vmem.at[b]], rows_v.at[b], sems.at[b])

        def body(i_vmem, w_vmem, o_vmem):
            for b in range(SC_GATHERS):
                gather(i_vmem, b).start()
            for b in range(SC_GATHERS):
                gather(i_vmem, b).wait()

                @pl.loop(0, group)
                def _(j):
                    tok = b * group + j
                    wv = [plsc.bitcast(w_vmem[tok, pl.ds(k * SC_LANES, SC_LANES)], BF16)
                          for k in range(TOP_K)]

                    def chunk(off):
                        prods = [plsc.bitcast(
                            rows_v[b, j * TOP_K + k, pl.ds(off, SC_LANES)], BF16) * wv[k]
                            for k in range(TOP_K)]
                        lo = jnp.zeros((SC_LANES,), F32)
                        hi = jnp.zeros((SC_LANES,), F32)
                        for k in range(0, TOP_K, 2):
                            u = plsc.bitcast(prods[k] + prods[k + 1], U32)
                            lo = lo + lax.bitcast_convert_type(u << 16, F32)
                            hi = hi + lax.bitcast_convert_type(u & jnp.uint32(0xFFFF0000), F32)
                        o_vmem[tok, pl.ds(off, SC_LANES)] = lo
                        o_vmem[tok, pl.ds(d + off, SC_LANES)] = hi

                    @plsc.parallel_loop(0, chunks, unroll=2)
                    def _(c):
                        chunk(c * SC_LANES)

        step_tok = SC_GATHERS * group
        pltpu.emit_pipeline(
            body, grid=(n_tok // step_tok,),
            in_specs=[pl.BlockSpec((SC_GATHERS, SC_WINDOW), lambda i: (i, 0)),
                      pl.BlockSpec((step_tok, TOP_K * SC_LANES), lambda i: (i, 0))],
            out_specs=[pl.BlockSpec((step_tok, 2 * d), lambda i: (i, 0))],
            core_axis_name=("c", "s"),
            dimension_semantics=(pltpu.PARALLEL,))(slot_hbm, w_hbm, out_hbm)

    return run(table, slots_tk.reshape(-1, SC_WINDOW), w_rep)


def _ffn_kernel(*refs, cap, n_exp):
    for j in range(FFN_GROUP):
        _ffn_expert(pl.program_id(0) * FFN_GROUP + j, j, *refs, cap=cap, n_exp=n_exp)


def _ffn_expert(e, j, cnt_ref, nt_ref, first_ref, te_ref, tt_ref, xsa_hbm, xsb_hbm,
                wg_ref, wu_ref, wd_ref, ysa_hbm, ysb_hbm,
                xbuf, ybuf, wg_b, wu_b, wd_b, sem_in, sem_out, *, cap, n_exp):
    n = nt_ref[e]
    g0 = first_ref[e]
    total = first_ref[n_exp - 1] + nt_ref[n_exp - 1]
    xs_hbm = (xsa_hbm, xsb_hbm)
    ys_hbm = (ysa_hbm, ysb_hbm)
    n_in = FFN_AHEAD + 1
    half = FFN_TILE // 2

    def at_height(ex, t, fn):
        few = cnt_ref[ex] - t * FFN_TILE <= half

        @pl.when(few)
        def _():
            fn(half)

        @pl.when(jnp.logical_not(few))
        def _():
            fn(FFN_TILE)

    def rows(ex, t, n_rows):
        return pl.ds(pl.multiple_of(ex * cap + t * FFN_TILE, FFN_TILE), n_rows)

    def in_copy(ex, t, slot, part, n_rows):
        return pltpu.make_async_copy(xs_hbm[part].at[rows(ex, t, n_rows)],
                                     xbuf.at[slot, part, pl.ds(0, n_rows)],
                                     sem_in.at[slot, part])

    def out_copy(ex, t, slot, part, n_rows):
        return pltpu.make_async_copy(ybuf.at[slot, part, pl.ds(0, n_rows)],
                                     ys_hbm[part].at[rows(ex, t, n_rows)],
                                     sem_out.at[slot, part])

    def start_in(g):
        ex, t = te_ref[g], tt_ref[g]
        at_height(ex, t, lambda nr: [in_copy(ex, t, g % n_in, part, nr).start()
                                     for part in range(PACK_PARTS)])

    def wait_out(g):
        ex, t = te_ref[g], tt_ref[g]
        at_height(ex, t, lambda nr: [out_copy(ex, t, g % 2, part, nr).wait()
                                     for part in range(PACK_PARTS)])

    @pl.when(n > 0)
    def _():
        @pl.when(g0 == 0)
        def _():
            for ahead in range(FFN_AHEAD):
                @pl.when(ahead < total)
                def _():
                    start_in(ahead)

        wg_b[...] = wg_ref[j].astype(BF16)
        wu_b[...] = wu_ref[j].astype(BF16)
        wd_b[...] = wd_ref[j].astype(BF16)

        def tile(t, carry):
            g = g0 + t
            slot = g % 2
            slot_in = g % n_in
            at_height(e, t, lambda nr: [in_copy(e, t, slot_in, part, nr).wait()
                                        for part in range(PACK_PARTS)])

            @pl.when(g + FFN_AHEAD < total)
            def _():
                start_in(g + FFN_AHEAD)

            @pl.when(g >= 2)
            def _():
                wait_out(g - 2)

            def compute(n_rows):
                x = _unpack_rows([xbuf[slot_in, part, 0:n_rows, :]
                                  for part in range(PACK_PARTS)]).astype(BF16)
                a = _silu(_dot(x, wg_b[...])) * _dot(x, wu_b[...])
                y = _pack_rows(_dot(a.astype(BF16), wd_b[...]))
                for part in range(PACK_PARTS):
                    ybuf[slot, part, 0:n_rows, :] = y[part]
                    out_copy(e, t, slot, part, n_rows).start()

            at_height(e, t, compute)
            return carry

        lax.fori_loop(0, n, tile, 0)

        @pl.when(g0 + n == total)
        def _():
            @pl.when(total >= 2)
            def _():
                wait_out(total - 2)

            wait_out(total - 1)


def _ffn(xs_parts, counts, w_gate, w_up, w_down):
    n_exp, d = w_gate.shape[0], w_gate.shape[1]
    cap = xs_parts[0].shape[0] // n_exp
    ntiles = (counts + FFN_TILE - 1) // FFN_TILE
    cum = jnp.cumsum(ntiles)
    first = cum - ntiles
    max_tiles = (cap * TOP_K) // FFN_TILE + n_exp
    g = jnp.arange(max_tiles, dtype=I32)
    done = g[:, None] >= cum[None, :]
    tile_e = jnp.minimum(jnp.sum(done, axis=1), n_exp - 1).astype(I32)
    tile_t = (g - jnp.sum(jnp.where(done, ntiles[None, :], 0), axis=1)).astype(I32)
    wmap = lambda e, cnt, nt, first, te, tt: (e, 0, 0)
    hbm = pl.BlockSpec(memory_space=pl.ANY)
    return pl.pallas_call(
        functools.partial(_ffn_kernel, cap=cap, n_exp=n_exp),
        grid_spec=pltpu.PrefetchScalarGridSpec(
            num_scalar_prefetch=5, grid=(n_exp // FFN_GROUP,),
            in_specs=[hbm] * PACK_PARTS
                     + [pl.BlockSpec((FFN_GROUP, d, D_EXPERT), wmap),
                        pl.BlockSpec((FFN_GROUP, d, D_EXPERT), wmap),
                        pl.BlockSpec((FFN_GROUP, D_EXPERT, d), wmap)],
            out_specs=[hbm] * PACK_PARTS,
            scratch_shapes=[pltpu.VMEM((FFN_AHEAD + 1, PACK_PARTS, FFN_TILE, PACK_W), U32),
                            pltpu.VMEM((2, PACK_PARTS, FFN_TILE, PACK_W), U32),
                            pltpu.VMEM((d, D_EXPERT), BF16), pltpu.VMEM((d, D_EXPERT), BF16),
                            pltpu.VMEM((D_EXPERT, d), BF16),
                            pltpu.SemaphoreType.DMA((FFN_AHEAD + 1, PACK_PARTS)),
                            pltpu.SemaphoreType.DMA((2, PACK_PARTS))]),
        out_shape=[jax.ShapeDtypeStruct(xs_parts[0].shape, U32)] * PACK_PARTS,
        compiler_params=pltpu.CompilerParams(
            dimension_semantics=("arbitrary",), vmem_limit_bytes=VMEM_LIMIT),
        name="ffn",
    )(counts.astype(I32), ntiles.astype(I32), first.astype(I32), tile_e, tile_t, *xs_parts,
      w_gate, w_up, w_down)


def _combine_kernel(x1_ref, h2pa_ref, h2pb_ref, ra_ref, rb_ref, mod_ref,
                    wsg_ref, wsu_ref, wsd_ref, o_ref):
    h2 = _unpack_rows([h2pa_ref[0], h2pb_ref[0]]).astype(BF16)
    a = _silu(_dot(h2, wsg_ref[...])) * _dot(h2, wsu_ref[...])
    routed = jnp.concatenate([ra_ref[0], rb_ref[0]], axis=1)
    o_ref[0] = x1_ref[0] + mod_ref[0][5:6] * (_dot(a.astype(BF16), wsd_ref[...]) + routed)


def _combine(x1, h2p_parts, routed_parts, mod, per_batch_mod, p, tm):
    b, l, d = x1.shape
    mod_map = (lambda bi, i: (bi, 0, 0)) if per_batch_mod else (lambda bi, i: (0, 0, 0))
    tok = lambda w: pl.BlockSpec((1, tm, w), lambda bi, i: (bi, i, 0))
    full = lambda a: pl.BlockSpec(a.shape, lambda bi, i: (0, 0))
    weights = [p['w_sh_gate'], p['w_sh_up'], p['w_sh_down']]
    routed_parts = [r.reshape(b, l, d // PACK_PARTS) for r in routed_parts]
    return pl.pallas_call(
        _combine_kernel,
        grid=(b, l // tm),
        in_specs=[tok(d)] + [tok(PACK_W)] * PACK_PARTS + [tok(d // PACK_PARTS)] * PACK_PARTS
                 + [pl.BlockSpec((1, 6, d), mod_map)] + [full(w) for w in weights],
        out_specs=tok(d),
        out_shape=jax.ShapeDtypeStruct((b, l, d), F32),
        compiler_params=pltpu.CompilerParams(
            dimension_semantics=("parallel", "parallel"), vmem_limit_bytes=VMEM_LIMIT),
        name="combine",
    )(x1, *h2p_parts, *routed_parts, mod, *weights)


def _pad_heads(w, parts):
    k = w.shape[0]
    per = w.shape[1] // MLA_HEADS
    w = w.reshape(k, MLA_HEADS, per)[:, :, parts[0]:parts[1]]
    w = jnp.pad(w, ((0, 0), (0, 0), (0, HEAD_W - (parts[1] - parts[0]))))
    return w.reshape(k, MLA_HEADS * HEAD_W)


def _prep_params(l, g_norm1, g_norm2, w_in, w_gk_fwd, b_gk_fwd, w_gk_bwd, b_gk_bwd, g_gla_out,
                 g_q_lora, w_uq, g_kv_lora, w_ukv, g_qk_q, g_qk_k, w_out, w_router, b_router,
                 w_sh_gate, w_sh_up, w_sh_down):
    w = w_in[l]
    d = w.shape[0]
    o_lrf = COL_CQ
    o_lrb = o_lrf + GLA_GATE_RANK
    o_cq = o_lrb + GLA_GATE_RANK
    o_ckv = o_cq + Q_LORA
    o_kr = o_ckv + KV_LORA
    w_in_p = jnp.concatenate([
        w[:, 0:COL_CQ], w[:, o_cq:o_ckv], w[:, o_ckv:o_kr], jnp.zeros((d, KR_LANE0), w.dtype),
        w[:, o_kr:o_kr + ROPE_DIM], w[:, o_lrf:o_lrb], w[:, o_lrb:o_cq]], axis=1).astype(BF16)
    z = jnp.zeros((GLA_GATE_RANK, GLA_KEY_W), F32)
    w_gk_big = jnp.concatenate([
        jnp.zeros((LANES - 2 * GLA_GATE_RANK, 2 * GLA_KEY_W), F32),
        jnp.concatenate([w_gk_fwd[l], z], axis=1),
        jnp.concatenate([z, w_gk_bwd[l]], axis=1)], axis=0).astype(BF16)
    pad_gain = lambda g: jnp.pad(g, (0, HEAD_W - QK_DIM)).reshape(1, HEAD_W)
    w_ukv_h = w_ukv[l]
    return {
        'g_norm1': g_norm1[l].reshape(1, d), 'g_norm2': g_norm2[l].reshape(1, d),
        'w_in_p': w_in_p, 'w_gk_big': w_gk_big,
        'b_gk': jnp.concatenate([b_gk_fwd[l], b_gk_bwd[l]]).reshape(1, 2 * GLA_KEY_W),
        'g_gla_out': g_gla_out[l].reshape(1, GLA_DV),
        'g_q_lora': g_q_lora[l].reshape(1, Q_LORA),
        'w_uq_p': _pad_heads(w_uq[l], (0, QK_DIM)).astype(BF16),
        'g_kv_lora': g_kv_lora[l].reshape(1, KV_LORA),
        'w_ukv_p': jnp.concatenate([_pad_heads(w_ukv_h, (0, NOPE_DIM)),
                                    _pad_heads(w_ukv_h, (NOPE_DIM, NOPE_DIM + V_DIM))],
                                   axis=1).astype(BF16),
        'gq': pad_gain(g_qk_q[l]), 'gk': pad_gain(g_qk_k[l]),
        'w_out_top': w_out[l][:GLA_VAL_W].astype(BF16),
        'w_out_bot': w_out[l][GLA_VAL_W:].astype(BF16),
        'w_router_t': w_router[l].T, 'b_router_col': b_router[l].reshape(N_EXPERTS, 1),
        'w_sh_gate': w_sh_gate[l].astype(BF16), 'w_sh_up': w_sh_up[l].astype(BF16),
        'w_sh_down': w_sh_down[l].astype(BF16),
    }


def _rope_tables(length):
    pos = np.arange(length)
    r = (pos // GRID_W).astype(np.float32)
    c = (pos % GRID_W).astype(np.float32)
    half = ROPE_DIM // 4
    inv_freq = np.float32(ROPE_THETA) ** (-np.arange(half, dtype=np.float32) / np.float32(half))
    ang_r = r[:, None] * inv_freq[None, :]
    ang_c = c[:, None] * inv_freq[None, :]
    zeros = lambda w: np.zeros((length, w), np.float32)
    ones = lambda w: np.ones((length, w), np.float32)
    tail_w = HEAD_W - ROPE_LANE0 - ROPE_DIM
    cos = np.concatenate([ones(ROPE_LANE0), np.cos(ang_r), np.cos(ang_r), np.cos(ang_c),
                          np.cos(ang_c), ones(tail_w)], axis=1)
    s1 = np.concatenate([zeros(ROPE_LANE0), -np.sin(ang_r), zeros(half), -np.sin(ang_c),
                         zeros(half), zeros(tail_w)], axis=1)
    s2 = np.concatenate([zeros(ROPE_LANE0), zeros(half), np.sin(ang_r), zeros(half),
                         np.sin(ang_c), zeros(tail_w)], axis=1)
    return tuple(jnp.asarray(t, F32) for t in (cos, s1, s2))


def _mix_route_dispatch(x_tok, seq_shape, mod, per_batch_mod, p, s0f, s0b, rope_tabs, ctx_kv,
                        tq):
    bt, lt, d = x_tok.shape
    b, l = seq_shape
    n_tok = bt * lt
    (q, k, v, g, lgf, lgb, qm, km, vm, ckvn, kr) = _inproj(x_tok, mod, per_batch_mod, p,
                                                            rope_tabs, IN_TILE)
    seq = lambda a: a.reshape(b, l, a.shape[-1])
    og, sf, sb = _gla(seq(q), seq(k), seq(v), seq(g), seq(lgf), seq(lgb), s0f, s0b,
                      p['g_gla_out'], GLA_SEQS if l == GLA_BLOCK else 1)
    kv_pairs = [(seq(km), seq(vm))] + ([ctx_kv] if ctx_kv is not None else [])
    om = _attn(seq(qm), kv_pairs, tq, ATTN_SEQS if l == tq else 1)
    tokv = lambda a: a.reshape(bt, lt, a.shape[-1])
    x1, h2a, h2b, slot8, w8, cnt = _outproj(tokv(og), tokv(om), x_tok, mod, per_batch_mod, p,
                                            OUT_TILE)
    x_sorted = [_sc_scatter_rows(h.reshape(n_tok, PACK_W), slot8,
                                 N_EXPERTS * _expert_cap(n_tok)) for h in (h2a, h2b)]
    routed = dict(x1=x1, h2=(h2a, h2b), slot8=slot8, w8=w8, cnt=cnt, x_sorted=x_sorted)
    return routed, sf, sb, ckvn, kr


def _experts_collect(r, w_gate, w_up, w_down):
    n_tok = r['slot8'].shape[1]
    y_sorted = _ffn(r['x_sorted'], r['cnt'][:, 0].astype(I32), w_gate, w_up, w_down)
    slots_tk = r['slot8'].T.reshape(1, TOP_K * n_tok)
    w_rep = r['w8'].reshape(n_tok, LANES)
    return [_sc_collect_sum(y, slots_tk, w_rep) for y in y_sorted]


def kernel(x_prompt, x_sample, c, state_gla_fwd, state_gla_bwd, cache_mla_ckv, cache_mla_krope,
           c_ctx, w_ada, b_ada, g_norm1, g_norm2, w_in, w_gk_fwd, b_gk_fwd, w_gk_bwd, b_gk_bwd,
           g_gla_out, g_q_lora, w_uq, g_kv_lora, w_ukv, g_qk_q, g_qk_k, w_out,
           w_router, b_router, w_exp_gate, w_exp_up, w_exp_down, w_sh_gate, w_sh_up, w_sh_down):
    bp, lp, d = x_prompt.shape
    bs, ls, _ = x_sample.shape
    depth = w_ada.shape[0]
    xp = x_prompt.reshape(1, bp * lp, d)
    xs = x_sample
    new_f, new_b, new_ckv, new_kr = [], [], [], []
    lat_tabs = _rope_tables(ls)
    cvecs = jnp.concatenate([c_ctx[None], c, jnp.zeros((8 - 1 - bs, d), F32)], axis=0)
    for l in range(depth):
        p = _prep_params(l, g_norm1, g_norm2, w_in, w_gk_fwd, b_gk_fwd, w_gk_bwd, b_gk_bwd,
                         g_gla_out, g_q_lora, w_uq, g_kv_lora, w_ukv, g_qk_q, g_qk_k, w_out,
                         w_router, b_router, w_sh_gate, w_sh_up, w_sh_down)
        mod = _ada(cvecs, w_ada[l], b_ada[l]).reshape(8, 6, d)
        mod_p, mod_s = mod[0:1], mod[1:1 + bs]
        zeros = jnp.zeros((bp, GLA_KEY_W, GLA_DV), F32)
        r_ctx, sf, sb, ckvn, kr = _mix_route_dispatch(
            xp, (bp, lp), mod_p, False, p, zeros, zeros, (), None, lp)
        new_f.append(sf.reshape(bp, GLA_HEADS, GLA_DK, GLA_DV))
        new_b.append(sb.reshape(bp, GLA_HEADS, GLA_DK, GLA_DV))
        new_ckv.append(ckvn.reshape(bp, lp, KV_LORA))
        new_kr.append(kr.reshape(bp, lp, LANES)[:, :, KR_LANE0:KR_LANE0 + ROPE_DIM])
        kr_cache = jnp.pad(cache_mla_krope[:, l],
                           ((0, 0), (0, 0), (ROPE_LANE0, LANES - ROPE_LANE0 - ROPE_DIM)))
        ctx_kv = _cache_kv(cache_mla_ckv[:, l], kr_cache, p)
        s0f = state_gla_fwd[:, l].reshape(bs, GLA_KEY_W, GLA_DV)
        s0b = state_gla_bwd[:, l].reshape(bs, GLA_KEY_W, GLA_DV)
        r_lat, _, _, _, _ = _mix_route_dispatch(
            xs, (bs, ls), mod_s, True, p, s0f, s0b, lat_tabs, ctx_kv, ATTN_TILE)
        experts = (w_exp_gate[l], w_exp_up[l], w_exp_down[l])
        routed_ctx = _experts_collect(r_ctx, *experts)
        routed_lat = _experts_collect(r_lat, *experts)
        xp = _combine(r_ctx['x1'], r_ctx['h2'], routed_ctx, mod_p, False, p, OUT_TILE)
        xs = _combine(r_lat['x1'], r_lat['h2'], routed_lat, mod_s, True, p, OUT_TILE)
    return (xp.reshape(bp, lp, d), xs, jnp.stack(new_f, axis=1), jnp.stack(new_b, axis=1),
            jnp.stack(new_ckv, axis=1), jnp.stack(new_kr, axis=1))
```

```python
import functools

import jax
import jax.numpy as jnp
import numpy as np
from jax import lax
from jax.experimental import pallas as pl
from jax.experimental.pallas import tpu as pltpu
from jax.experimental.pallas import tpu_sc as plsc

F32 = jnp.float32
BF16 = jnp.bfloat16
I32 = jnp.int32
U32 = jnp.uint32

D_MODEL = 1024
EPS = 1e-6
GRID_W = 64
GLA_HEADS = 4
GLA_DK = 64
GLA_DV = 128
GLA_GATE_RANK = 16
GLA_GATE_NORM = 16.0
GLA_KEY_W = GLA_HEADS * GLA_DK
GLA_VAL_W = GLA_HEADS * GLA_DV
MLA_HEADS = 4
Q_LORA = 256
KV_LORA = 128
NOPE_DIM = 64
ROPE_DIM = 32
V_DIM = 128
QK_DIM = NOPE_DIM + ROPE_DIM
ROPE_THETA = 10000.0
N_EXPERTS = 64
TOP_K = 8
N_GROUPS = 8
TOPK_GROUPS = 4
D_EXPERT = 256
ROUTED_SCALE = 2.5

LANES = 128
HEAD_W = LANES
MLA_W = MLA_HEADS * HEAD_W
ROPE_LANE0 = NOPE_DIM
COL_V = 2 * GLA_KEY_W
COL_G = COL_V + GLA_VAL_W
COL_CQ = COL_G + GLA_VAL_W
COL_CKV = COL_CQ + Q_LORA
TAIL0 = COL_CKV + KV_LORA
IN_W = TAIL0 + LANES
KR_LANE0 = LANES - ROPE_DIM - 2 * GLA_GATE_RANK
GLA_BLOCK = 256
GLA_SUB = 64
GLA_SEQS = 2
ATTN_SEQS = 8
ATTN_TILE = 1024
ADA_TILE = 768
IN_TILE = 1024
IN_SUB = 256
OUT_TILE = 1024
FFN_TILE = 512
FFN_AHEAD = 5
FFN_GROUP = 2
PACK_PARTS = 2
PACK_W = D_MODEL // (2 * PACK_PARTS)
SC_WINDOW = 128
SC_LANES = 16
SC_GATHERS = 2
VMEM_LIMIT = 56 * 1024 * 1024

HIGHEST = lax.Precision.HIGHEST


def _dot(a, b, precision=None):
    return jnp.dot(a, b, preferred_element_type=F32, precision=precision)


def _dot_nt(a, b, precision=None):
    return lax.dot_general(a, b, (((1,), (1,)), ((), ())), preferred_element_type=F32,
                           precision=precision)


def _dot_nt_3pass(a, b):
    a_hi = a.astype(BF16)
    b_hi = b.astype(BF16)
    a_lo = (a - a_hi.astype(F32)).astype(BF16)
    b_lo = (b - b_hi.astype(F32)).astype(BF16)
    return _dot_nt(a_hi, b_hi) + _dot_nt(a_hi, b_lo) + _dot_nt(a_lo, b_hi)


def _rms(x, width):
    ss = jnp.sum(x * x, axis=-1, keepdims=True) * (1.0 / width)
    return x * lax.rsqrt(ss + EPS)


def _silu(x):
    return x * jax.nn.sigmoid(x)


def _log_sigmoid(x):
    return jnp.minimum(x, 0.0) - jnp.log1p(jnp.exp(-jnp.abs(x)))


def _pack_rows(x):
    parts = []
    for i in range(PACK_PARTS):
        c0 = i * 2 * PACK_W
        lo = lax.bitcast_convert_type(x[:, c0:c0 + PACK_W].astype(BF16).astype(F32), U32)
        hi = lax.bitcast_convert_type(
            x[:, c0 + PACK_W:c0 + 2 * PACK_W].astype(BF16).astype(F32), U32)
        parts.append(hi | (lo >> 16))
    return parts


def _unpack_rows(parts):
    cols = []
    for w in parts:
        cols.append(lax.bitcast_convert_type(w << 16, F32))
        cols.append(lax.bitcast_convert_type(w & jnp.uint32(0xFFFF0000), F32))
    return jnp.concatenate(cols, axis=1)


def _ada_kernel(c_ref, w_ref, b_ref, o_ref):
    o_ref[...] = _dot(_silu(c_ref[...]), w_ref[...], precision=HIGHEST) + b_ref[...]


def _ada(cvecs, w_ada, b_ada):
    n = w_ada.shape[1]
    tn = ADA_TILE
    return pl.pallas_call(
        _ada_kernel,
        grid=(n // tn,),
        in_specs=[pl.BlockSpec((8, D_MODEL), lambda j: (0, 0)),
                  pl.BlockSpec((D_MODEL, tn), lambda j: (0, j)),
                  pl.BlockSpec((1, tn), lambda j: (0, j))],
        out_specs=pl.BlockSpec((8, tn), lambda j: (0, j)),
        out_shape=jax.ShapeDtypeStruct((8, n), F32),
        name="ada",
    )(cvecs, w_ada, b_ada.reshape(1, n))


def _rope(x, c, s1, s2):
    return x * c + pltpu.roll(x, LANES - 8, 1) * s1 + pltpu.roll(x, 8, 1) * s2


def _mla_kv(ckv_n, kr, w_ukv_ref, gk, rope_tabs, k_ref, v_ref, rs=slice(None)):
    kv = _dot(ckv_n.astype(BF16), w_ukv_ref[...])
    for h in range(MLA_HEADS):
        k_h = kv[:, h * HEAD_W:(h + 1) * HEAD_W] + kr
        k_h = _rms(k_h, QK_DIM) * gk
        if rope_tabs is not None:
            k_h = _rope(k_h, *rope_tabs)
        k_ref[0, rs, h * HEAD_W:(h + 1) * HEAD_W] = k_h.astype(BF16)
    v_ref[0, rs, :] = kv[:, MLA_W:].astype(BF16)


def _lane_mask(lo, hi, rows):
    lane = lax.broadcasted_iota(I32, (rows, LANES), 1)
    return (lane >= lo) & (lane < hi)


def _inproj_kernel(x_ref, mod_ref, g1_ref, win_ref, wgk_ref, bgk_ref, gql_ref, wuq_ref,
                   gkv_ref, wukv_ref, gq_ref, gk_ref, *refs, positions):
    tab_refs, outs = (refs[:3], refs[3:]) if positions else ((), refs)
    (q_ref, k_ref, v_ref, g_ref, lgf_ref, lgb_ref, qm_ref, km_ref, vm_ref, ckvn_ref,
     kr_ref) = outs
    mod = mod_ref[0]
    for r0 in range(0, x_ref.shape[1], IN_SUB):
        rs = slice(r0, r0 + IN_SUB)
        x = x_ref[0, rs, :]
        h = _rms(x, D_MODEL) * g1_ref[...] * (1.0 + mod[1:2]) + mod[0:1]
        y = _dot(h.astype(BF16), win_ref[...])
        q_ref[0, rs, :] = (y[:, 0:GLA_KEY_W] * GLA_DK ** -0.5).astype(BF16)
        k_ref[0, rs, :] = y[:, GLA_KEY_W:COL_V].astype(BF16)
        v_ref[0, rs, :] = y[:, COL_V:COL_G].astype(BF16)
        g_ref[0, rs, :] = y[:, COL_G:COL_CQ].astype(BF16)
        tail = y[:, TAIL0:IN_W]
        pre = _dot(tail.astype(BF16), wgk_ref[...]) + bgk_ref[...]
        logg = _log_sigmoid(pre) * (1.0 / GLA_GATE_NORM)
        lgf_ref[0, rs, :] = logg[:, 0:GLA_KEY_W]
        lgb_ref[0, rs, :] = logg[:, GLA_KEY_W:]
        tabs = tuple(r[rs, :] for r in tab_refs) if positions else None
        cq = _rms(y[:, COL_CQ:COL_CKV], Q_LORA) * gql_ref[...]
        qm = _dot(cq.astype(BF16), wuq_ref[...])
        gq = gq_ref[...]
        for hh in range(MLA_HEADS):
            q_h = _rms(qm[:, hh * HEAD_W:(hh + 1) * HEAD_W], QK_DIM) * gq
            if positions:
                q_h = _rope(q_h, *tabs)
            qm_ref[0, rs, hh * HEAD_W:(hh + 1) * HEAD_W] = (q_h * QK_DIM ** -0.5).astype(BF16)
        ckv_n = _rms(y[:, COL_CKV:TAIL0], KV_LORA) * gkv_ref[...]
        ckvn_ref[0, rs, :] = ckv_n
        kr_ref[0, rs, :] = tail
        kr = jnp.where(_lane_mask(KR_LANE0, KR_LANE0 + ROPE_DIM, IN_SUB), tail, 0.0)
        _mla_kv(ckv_n, kr, wukv_ref, gk_ref[...], tabs, km_ref, vm_ref, rs)


def _inproj(x, mod, per_batch_mod, p, rope_tabs, tm):
    b, l, d = x.shape
    nt = l // tm
    mod_map = (lambda bi, i: (bi, 0, 0)) if per_batch_mod else (lambda bi, i: (0, 0, 0))
    tab_map = lambda bi, i: (i, 0)
    const = lambda bi, i: (0, 0)
    tok = lambda w: pl.BlockSpec((1, tm, w), lambda bi, i: (bi, i, 0))
    full = lambda a: pl.BlockSpec(a.shape, const)
    weights = [p['g_norm1'], p['w_in_p'], p['w_gk_big'], p['b_gk'], p['g_q_lora'], p['w_uq_p'],
               p['g_kv_lora'], p['w_ukv_p'], p['gq'], p['gk']]
    outs = [(GLA_KEY_W, BF16), (GLA_KEY_W, BF16), (GLA_VAL_W, BF16), (GLA_VAL_W, BF16),
            (GLA_KEY_W, F32), (GLA_KEY_W, F32), (MLA_W, BF16), (MLA_W, BF16), (MLA_W, BF16),
            (KV_LORA, F32), (LANES, F32)]
    return pl.pallas_call(
        functools.partial(_inproj_kernel, positions=bool(rope_tabs)),
        grid=(b, nt),
        in_specs=[tok(d), pl.BlockSpec((1, 6, d), mod_map)] + [full(w) for w in weights]
                 + [pl.BlockSpec((tm, LANES), tab_map)] * len(rope_tabs),
        out_specs=[tok(w) for w, _ in outs],
        out_shape=[jax.ShapeDtypeStruct((b, l, w), dt) for w, dt in outs],
        compiler_params=pltpu.CompilerParams(
            dimension_semantics=("parallel", "parallel"), vmem_limit_bytes=VMEM_LIMIT),
        name="inproj",
    )(x, mod, *weights, *rope_tabs)


def _cache_kv_kernel(ckv_ref, kr_ref, wukv_ref, gk_ref, k_ref, v_ref):
    _mla_kv(ckv_ref[0], kr_ref[0], wukv_ref, gk_ref[...], None, k_ref, v_ref)


def _cache_kv(ckv, kr128, p):
    b, l, _ = ckv.shape
    tok = lambda w: pl.BlockSpec((1, l, w), lambda bi: (bi, 0, 0))
    full = lambda a: pl.BlockSpec(a.shape, lambda bi: (0, 0))
    return pl.pallas_call(
        _cache_kv_kernel,
        grid=(b,),
        in_specs=[tok(KV_LORA), tok(LANES), full(p['w_ukv_p']), full(p['gk'])],
        out_specs=[tok(MLA_W), tok(MLA_W)],
        out_shape=[jax.ShapeDtypeStruct((b, l, MLA_W), BF16)] * 2,
        name="cache_kv",
    )(ckv, kr128, p['w_ukv_p'], p['gk'])


def _split3(x):
    hi = x.astype(BF16)
    r1 = x - hi.astype(F32)
    mid = r1.astype(BF16)
    lo = (r1 - mid.astype(F32)).astype(BF16)
    return hi, mid, lo


def _gla_block(q, k, v, lg, s, fwd):
    n = GLA_BLOCK
    nc = n // GLA_SUB
    row = lax.broadcasted_iota(I32, (n, n), 0)
    col = lax.broadcasted_iota(I32, (n, n), 1)
    same = (row // GLA_SUB) == (col // GLA_SUB)
    causal = same & ((col <= row) if fwd else (col >= row))
    tri = causal.astype(BF16)
    hi, mid, lo = _split3(lg)
    cum = _dot(tri, hi) + _dot(tri, mid) + _dot(tri, lo)
    tot_rows, mid_rows = [], []
    for c in range(nc):
        r_tot = c * GLA_SUB + (GLA_SUB - 1 if fwd else 0)
        r_mid = c * GLA_SUB + GLA_SUB // 2
        tot_rows.append(jnp.broadcast_to(cum[r_tot:r_tot + 1], (GLA_SUB, GLA_KEY_W)))
        mid_rows.append(jnp.broadcast_to(cum[r_mid:r_mid + 1], (GLA_SUB, GLA_KEY_W)))
    tot_rows = jnp.concatenate(tot_rows, axis=0)
    mid_rows = jnp.concatenate(mid_rows, axis=0)
    rel = cum - mid_rows
    qi = q * jnp.exp(rel)
    ki = (k * jnp.exp(-rel)).astype(BF16)
    q_in = (q * jnp.exp(cum)).astype(BF16)
    k_up = k * jnp.exp(tot_rows - cum)
    dec = jnp.exp(tot_rows)
    k_up_t = k_up.T.astype(BF16)
    dec_t = dec.T
    o_heads = []
    for h in range(GLA_HEADS):
        hk = slice(h * GLA_DK, (h + 1) * GLA_DK)
        a = _dot_nt(qi[:, hk].astype(BF16), ki[:, hk])
        a = jnp.where(causal, a, 0.0).astype(BF16)
        o_heads.append(_dot(a, v[:, h * GLA_DV:(h + 1) * GLA_DV]))
    o = jnp.concatenate(o_heads, axis=1)
    o_inter = [None] * nc
    for c in (range(nc) if fwd else range(nc - 1, -1, -1)):
        rc = slice(c * GLA_SUB, (c + 1) * GLA_SUB)
        s_b = s.astype(BF16)
        o_inter[c] = jnp.concatenate(
            [_dot(q_in[rc, h * GLA_DK:(h + 1) * GLA_DK], s_b[h * GLA_DK:(h + 1) * GLA_DK])
             for h in range(GLA_HEADS)], axis=1)
        u = jnp.concatenate(
            [_dot(k_up_t[h * GLA_DK:(h + 1) * GLA_DK, rc], v[rc, h * GLA_DV:(h + 1) * GLA_DV])
             for h in range(GLA_HEADS)], axis=0)
        s = dec_t[:, c * GLA_SUB:c * GLA_SUB + 1] * s + u
    return o + jnp.concatenate(o_inter, axis=0), s


def _gla_kernel(q_ref, k_ref, v_ref, g_ref, lgf_ref, lgb_ref, s0f_ref, s0b_ref, gout_ref,
                o_ref, sf_ref, sb_ref, acc_ref, st_ref, *, nblk, n_seq):
    for si in range(n_seq):
        _gla_sequence(si, q_ref, k_ref, v_ref, g_ref, lgf_ref, lgb_ref, s0f_ref, s0b_ref,
                      gout_ref, o_ref, sf_ref, sb_ref, acc_ref, st_ref, nblk)


def _gla_sequence(si, q_ref, k_ref, v_ref, g_ref, lgf_ref, lgb_ref, s0f_ref, s0b_ref, gout_ref,
                  o_ref, sf_ref, sb_ref, acc_ref, st_ref, nblk):
    st_ref[si, 0] = s0f_ref[si]
    st_ref[si, 1] = s0b_ref[si]

    def load(blk):
        start = blk * GLA_BLOCK
        r = pl.ds(start if isinstance(blk, int) else pl.multiple_of(start, GLA_BLOCK), GLA_BLOCK)
        return r, q_ref[si, r, :].astype(F32), k_ref[si, r, :].astype(F32), v_ref[si, r, :]

    def pair_step(i, first_touch):
        rf, qf, kf, vf = load(i)
        rb, qb, kb, vb = load(nblk - 1 - i)
        of, s_f = _gla_block(qf, kf, vf, lgf_ref[si, rf, :], st_ref[si, 0], True)
        ob, s_b = _gla_block(qb, kb, vb, lgb_ref[si, rb, :], st_ref[si, 1], False)
        if first_touch:
            acc_ref[si, rf, :] = of
            acc_ref[si, rb, :] = ob
        else:
            acc_ref[si, rf, :] += of
            acc_ref[si, rb, :] += ob
        st_ref[si, 0] = s_f
        st_ref[si, 1] = s_b

    if nblk == 1:
        r, q, k, v = load(0)
        of, s_f = _gla_block(q, k, v, lgf_ref[si, r, :], st_ref[si, 0], True)
        ob, s_b = _gla_block(q, k, v, lgb_ref[si, r, :], st_ref[si, 1], False)
        acc_ref[si, r, :] = of + ob
        st_ref[si, 0] = s_f
        st_ref[si, 1] = s_b
    else:
        half = nblk // 2
        lax.fori_loop(0, half, lambda i, c: (pair_step(i, True), c)[1], 0)
        lax.fori_loop(half, nblk, lambda i, c: (pair_step(i, False), c)[1], 0)
    sf_ref[si] = st_ref[si, 0]
    sb_ref[si] = st_ref[si, 1]

    def fin_step(blk, carry):
        r = pl.ds(pl.multiple_of(blk * GLA_BLOCK, GLA_BLOCK), GLA_BLOCK)
        o = acc_ref[si, r, :]
        gate = _silu(g_ref[si, r, :].astype(F32))
        for h in range(GLA_HEADS):
            sl = slice(h * GLA_DV, (h + 1) * GLA_DV)
            o_ref[si, r, sl] = (_rms(o[:, sl], GLA_DV) * gout_ref[...] * gate[:, sl]).astype(BF16)
        return carry

    lax.fori_loop(0, nblk, fin_step, 0)


def _gla(q, k, v, g, lgf, lgb, s0f, s0b, g_out, n_seq):
    b, l, _ = q.shape
    assert l == GLA_BLOCK or l % (2 * GLA_BLOCK) == 0
    seq = lambda w: pl.BlockSpec((n_seq, l, w), lambda bi: (bi, 0, 0))
    st = pl.BlockSpec((n_seq, GLA_KEY_W, GLA_DV), lambda bi: (bi, 0, 0))
    return pl.pallas_call(
        functools.partial(_gla_kernel, nblk=l // GLA_BLOCK, n_seq=n_seq),
        grid=(b // n_seq,),
        in_specs=[seq(GLA_KEY_W), seq(GLA_KEY_W), seq(GLA_VAL_W), seq(GLA_VAL_W),
                  seq(GLA_KEY_W), seq(GLA_KEY_W), st, st,
                  pl.BlockSpec((1, GLA_DV), lambda bi: (0, 0))],
        out_specs=[seq(GLA_VAL_W), st, st],
        out_shape=[jax.ShapeDtypeStruct((b, l, GLA_VAL_W), BF16),
                   jax.ShapeDtypeStruct((b, GLA_KEY_W, GLA_DV), F32),
                   jax.ShapeDtypeStruct((b, GLA_KEY_W, GLA_DV), F32)],
        scratch_shapes=[pltpu.VMEM((n_seq, l, GLA_VAL_W), F32),
                        pltpu.VMEM((n_seq, 2, GLA_KEY_W, GLA_DV), F32)],
        compiler_params=pltpu.CompilerParams(
            dimension_semantics=("parallel",), vmem_limit_bytes=VMEM_LIMIT),
        name="gla",
    )(q, k, v, g, lgf, lgb, s0f, s0b, g_out)


def _attn_kernel(*refs, n_kv, n_seq):
    q_ref, o_ref = refs[0], refs[-1]
    kv = [(refs[1 + 2 * i], refs[2 + 2 * i]) for i in range(n_kv)]
    for si in range(n_seq):
        for h in range(MLA_HEADS):
            sl = slice(h * HEAD_W, (h + 1) * HEAD_W)
            q = q_ref[si, :, sl]
            scores = [_dot_nt(q, k_ref[si, :, sl]) for k_ref, _ in kv]
            m = functools.reduce(jnp.maximum,
                                 [jnp.max(s, axis=-1, keepdims=True) for s in scores])
            o, den = 0.0, 0.0
            for s, (_, v_ref) in zip(scores, kv):
                p = jnp.exp(s - m)
                den = den + jnp.sum(p, axis=-1, keepdims=True)
                o = o + _dot(p.astype(BF16), v_ref[si, :, sl])
            o_ref[si, :, sl] = (o / den).astype(BF16)


def _attn(q, kv_pairs, tq, n_seq):
    b, l, _ = q.shape
    kv_specs, kv_args = [], []
    for k, v in kv_pairs:
        spec = pl.BlockSpec((n_seq, k.shape[1], MLA_W), lambda bi, i: (bi, 0, 0))
        kv_specs += [spec, spec]
        kv_args += [k, v]
    return pl.pallas_call(
        functools.partial(_attn_kernel, n_kv=len(kv_pairs), n_seq=n_seq),
        grid=(b // n_seq, l // tq),
        in_specs=[pl.BlockSpec((n_seq, tq, MLA_W), lambda bi, i: (bi, i, 0))] + kv_specs,
        out_specs=pl.BlockSpec((n_seq, tq, MLA_W), lambda bi, i: (bi, i, 0)),
        out_shape=jax.ShapeDtypeStruct((b, l, MLA_W), BF16),
        compiler_params=pltpu.CompilerParams(
            dimension_semantics=("parallel", "parallel"), vmem_limit_bytes=VMEM_LIMIT),
        name="attn",
    )(q, *kv_args)


def _expert_cap(n_tok):
    return -(-n_tok // FFN_TILE) * FFN_TILE


def _sum01(x):
    return jnp.sum(jnp.sum(x, axis=1, keepdims=True), axis=0, keepdims=True)


def _route(logits_t, bias_col, cnt_col, cap):
    t = logits_t.shape[1]
    gsz = N_EXPERTS // N_GROUPS
    scores = jax.nn.sigmoid(logits_t)
    sel = (scores + bias_col).reshape(N_GROUPS, gsz, t)
    scores = scores.reshape(N_GROUPS, gsz, t)
    neg = jnp.float32(-jnp.inf)
    ie = lax.broadcasted_iota(I32, (N_GROUPS, gsz, t), 1)
    ig = lax.broadcasted_iota(I32, (N_GROUPS, gsz, t), 0)
    m1 = jnp.max(sel, axis=1, keepdims=True)
    first = jnp.min(jnp.where(sel == m1, ie, gsz), axis=1, keepdims=True)
    m2 = jnp.max(jnp.where(ie == first, neg, sel), axis=1, keepdims=True)
    grp = m1 + m2
    igk = lax.broadcasted_iota(I32, (N_GROUPS, 1, t), 0)
    g_sel = jnp.zeros((N_GROUPS, 1, t), jnp.bool_)
    cur = grp
    for _ in range(TOPK_GROUPS):
        m = jnp.max(cur, axis=0, keepdims=True)
        pick = igk == jnp.min(jnp.where(cur == m, igk, N_GROUPS), axis=0, keepdims=True)
        g_sel = g_sel | pick
        cur = jnp.where(pick, neg, cur)
    cur = jnp.where(g_sel, sel, neg)
    idx = ig * gsz + ie
    e_sel = jnp.zeros((N_GROUPS, gsz, t), jnp.bool_)
    picks = []
    for _ in range(TOP_K):
        m = jnp.max(jnp.max(cur, axis=1, keepdims=True), axis=0, keepdims=True)
        cand = jnp.where(cur == m, idx, N_EXPERTS)
        pick = idx == jnp.min(jnp.min(cand, axis=1, keepdims=True), axis=0, keepdims=True)
        picks.append(pick)
        e_sel = e_sel | pick
        cur = jnp.where(pick, neg, cur)
    w = jnp.where(e_sel, scores, 0.0)
    gate = w / _sum01(w) * ROUTED_SCALE
    sel_f = e_sel.astype(F32).reshape(N_EXPERTS, t)
    earlier = (lax.broadcasted_iota(I32, (t, t), 0) < lax.broadcasted_iota(I32, (t, t), 1))
    rank = _dot(sel_f.astype(BF16), earlier.astype(BF16))
    base = lax.broadcasted_iota(I32, (N_EXPERTS, 1), 0).astype(F32) * float(cap) + cnt_col
    slot = (base + rank).reshape(N_GROUPS, gsz, t)
    slot8 = jnp.concatenate([_sum01(jnp.where(pk, slot, 0.0)).reshape(1, t) for pk in picks],
                            axis=0).astype(I32)
    w8 = jnp.concatenate([_sum01(jnp.where(pk, gate, 0.0)).reshape(1, t) for pk in picks], axis=0)
    return slot8, w8, cnt_col + jnp.sum(sel_f, axis=1, keepdims=True)


def _outproj_kernel(og_ref, om_ref, x_ref, mod_ref, wtop_ref, wbot_ref, g2_ref, wr_ref, br_ref,
                    x1_ref, h2pa_ref, h2pb_ref, slot_ref, w8_ref, cnt_ref, cnt_scr, *, cap):
    @pl.when((pl.program_id(0) == 0) & (pl.program_id(1) == 0))
    def _():
        cnt_scr[...] = jnp.zeros_like(cnt_scr)

    mod = mod_ref[0]
    mix = _dot(og_ref[0], wtop_ref[...]) + _dot(om_ref[0], wbot_ref[...])
    x1 = x_ref[0] + mod[2:3] * mix
    x1_ref[0] = x1
    h2 = _rms(x1, D_MODEL) * g2_ref[...] * (1.0 + mod[4:5]) + mod[3:4]
    h2pa_ref[0], h2pb_ref[0] = _pack_rows(h2)
    logits_t = _dot_nt_3pass(wr_ref[...], h2)
    slot8, w8, cnt = _route(logits_t, br_ref[...], cnt_scr[:, 0:1], cap)
    slot_ref[...] = slot8
    t = w8.shape[1]
    wb = lax.bitcast_convert_type(w8.astype(BF16).astype(F32), U32)
    wb = wb | (wb >> 16)
    w8_ref[0] = jnp.concatenate(
        [jnp.broadcast_to(wb[k:k + 1], (SC_LANES, t)) for k in range(TOP_K)], axis=0).T
    cnt_scr[...] = jnp.broadcast_to(cnt, cnt_scr.shape)
    cnt_ref[...] = cnt_scr[...]


def _outproj(og, om, x, mod, per_batch_mod, p, tm):
    b, l, d = x.shape
    nt = l // tm
    mod_map = (lambda bi, i: (bi, 0, 0)) if per_batch_mod else (lambda bi, i: (0, 0, 0))
    tok = lambda w: pl.BlockSpec((1, tm, w), lambda bi, i: (bi, i, 0))
    full = lambda a: pl.BlockSpec(a.shape, lambda bi, i: (0, 0))
    weights = [p['w_out_top'], p['w_out_bot'], p['g_norm2'], p['w_router_t'], p['b_router_col']]
    cnt_shape = (N_EXPERTS, LANES)
    return pl.pallas_call(
        functools.partial(_outproj_kernel, cap=_expert_cap(b * l)),
        grid=(b, nt),
        in_specs=[tok(GLA_VAL_W), tok(MLA_W), tok(d), pl.BlockSpec((1, 6, d), mod_map)]
                 + [full(w) for w in weights],
        out_specs=[tok(d), tok(PACK_W), tok(PACK_W),
                   pl.BlockSpec((TOP_K, tm), lambda bi, i: (0, bi * nt + i)),
                   tok(LANES), pl.BlockSpec(cnt_shape, lambda bi, i: (0, 0))],
        out_shape=[jax.ShapeDtypeStruct((b, l, d), F32), jax.ShapeDtypeStruct((b, l, PACK_W), U32),
                   jax.ShapeDtypeStruct((b, l, PACK_W), U32),
                   jax.ShapeDtypeStruct((TOP_K, b * l), I32),
                   jax.ShapeDtypeStruct((b, l, LANES), U32),
                   jax.ShapeDtypeStruct(cnt_shape, F32)],
        scratch_shapes=[pltpu.VMEM(cnt_shape, F32)],
        compiler_params=pltpu.CompilerParams(
            dimension_semantics=("arbitrary", "arbitrary"), vmem_limit_bytes=VMEM_LIMIT),
        name="outproj",
    )(og, om, x, mod, *weights)


def _sc_mesh():
    return plsc.VectorSubcoreMesh(core_axis_name="c", subcore_axis_name="s")


def _sc_scatter_rows(src, slot8, n_rows_out):
    t, d = src.shape
    nk = slot8.shape[0]

    @functools.partial(pl.kernel, out_type=jax.ShapeDtypeStruct((n_rows_out, d), src.dtype),
                       mesh=_sc_mesh(), scratch_types=[], name="dispatch")
    def run(src_hbm, slot_hbm, out_hbm):
        def body(x_vmem, i_vmem):
            for k in range(nk):
                pltpu.sync_copy(x_vmem, out_hbm.at[i_vmem.at[k]])

        pltpu.emit_pipeline(
            body, grid=(t // SC_WINDOW,),
            in_specs=[pl.BlockSpec((SC_WINDOW, d), lambda i: (i, 0)),
                      pl.BlockSpec((nk, SC_WINDOW), lambda i: (0, i))],
            out_specs=[], core_axis_name=("c", "s"),
            dimension_semantics=(pltpu.PARALLEL,))(src_hbm, slot_hbm)

    return run(src, slot8)


def _sc_collect_sum(table, slots_tk, w_rep):
    n_tok = w_rep.shape[0]
    d = table.shape[1]
    group = SC_WINDOW // TOP_K
    chunks = d // SC_LANES

    @functools.partial(pl.kernel, out_type=jax.ShapeDtypeStruct((n_tok, 2 * d), F32),
                       mesh=_sc_mesh(),
                       scratch_types=[pltpu.VMEM((SC_GATHERS, SC_WINDOW, d), table.dtype),
                                      pltpu.SemaphoreType.DMA((SC_GATHERS,))],
                       compiler_params=pltpu.CompilerParams(needs_layout_passes=False),
                       name="collect")
    def run(tab_hbm, slot_hbm, w_hbm, out_hbm, rows_v, sems):
        def gather(i_vmem, b):
            return pltpu.make_async_copy(tab_hbm.at[i_vmem.at[b]], rows_v.at[b], sems.at[b])

        def body(i_vmem, w_vmem, o_vmem):
            for b in range(SC_GATHERS):
                gather(i_vmem, b).start()
            for b in range(SC_GATHERS):
                gather(i_vmem, b).wait()

                @pl.loop(0, group)
                def _(j):
                    tok = b * group + j
                    wv = [plsc.bitcast(w_vmem[tok, pl.ds(k * SC_LANES, SC_LANES)], BF16)
                          for k in range(TOP_K)]

                    def chunk(off):
                        prods = [plsc.bitcast(
                            rows_v[b, j * TOP_K + k, pl.ds(off, SC_LANES)], BF16) * wv[k]
                            for k in range(TOP_K)]
                        lo = jnp.zeros((SC_LANES,), F32)
                        hi = jnp.zeros((SC_LANES,), F32)
                        for k in range(0, TOP_K, 2):
                            u = plsc.bitcast(prods[k] + prods[k + 1], U32)
                            lo = lo + lax.bitcast_convert_type(u << 16, F32)
                            hi = hi + lax.bitcast_convert_type(u & jnp.uint32(0xFFFF0000), F32)
                        o_vmem[tok, pl.ds(off, SC_LANES)] = lo
                        o_vmem[tok, pl.ds(d + off, SC_LANES)] = hi

                    @plsc.parallel_loop(0, chunks, unroll=2)
                    def _(c):
                        chunk(c * SC_LANES)

        step_tok = SC_GATHERS * group
        pltpu.emit_pipeline(
            body, grid=(n_tok // step_tok,),
            in_specs=[pl.BlockSpec((SC_GATHERS, SC_WINDOW), lambda i: (i, 0)),
                      pl.BlockSpec((step_tok, TOP_K * SC_LANES), lambda i: (i, 0))],
            out_specs=[pl.BlockSpec((step_tok, 2 * d), lambda i: (i, 0))],
            core_axis_name=("c", "s"),
            dimension_semantics=(pltpu.PARALLEL,))(slot_hbm, w_hbm, out_hbm)

    return run(table, slots_tk.reshape(-1, SC_WINDOW), w_rep)


def _ffn_kernel(*refs, cap, n_exp):
    for j in range(FFN_GROUP):
        _ffn_expert(pl.program_id(0) * FFN_GROUP + j, j, *refs, cap=cap, n_exp=n_exp)


def _ffn_expert(e, j, cnt_ref, nt_ref, first_ref, te_ref, tt_ref, xsa_hbm, xsb_hbm,
                wg_ref, wu_ref, wd_ref, ysa_hbm, ysb_hbm,
                xbuf, ybuf, wg_b, wu_b, wd_b, sem_in, sem_out, *, cap, n_exp):
    n = nt_ref[e]
    g0 = first_ref[e]
    total = first_ref[n_exp - 1] + nt_ref[n_exp - 1]
    xs_hbm = (xsa_hbm, xsb_hbm)
    ys_hbm = (ysa_hbm, ysb_hbm)
    n_in = FFN_AHEAD + 1
    half = FFN_TILE // 2

    def at_height(ex, t, fn):
        few = cnt_ref[ex] - t * FFN_TILE <= half

        @pl.when(few)
        def _():
            fn(half)

        @pl.when(jnp.logical_not(few))
        def _():
            fn(FFN_TILE)

    def rows(ex, t, n_rows):
        return pl.ds(pl.multiple_of(ex * cap + t * FFN_TILE, FFN_TILE), n_rows)

    def in_copy(ex, t, slot, part, n_rows):
        return pltpu.make_async_copy(xs_hbm[part].at[rows(ex, t, n_rows)],
                                     xbuf.at[slot, part, pl.ds(0, n_rows)],
                                     sem_in.at[slot, part])

    def out_copy(ex, t, slot, part, n_rows):
        return pltpu.make_async_copy(ybuf.at[slot, part, pl.ds(0, n_rows)],
                                     ys_hbm[part].at[rows(ex, t, n_rows)],
                                     sem_out.at[slot, part])

    def start_in(g):
        ex, t = te_ref[g], tt_ref[g]
        at_height(ex, t, lambda nr: [in_copy(ex, t, g % n_in, part, nr).start()
                                     for part in range(PACK_PARTS)])

    def wait_out(g):
        ex, t = te_ref[g], tt_ref[g]
        at_height(ex, t, lambda nr: [out_copy(ex, t, g % 2, part, nr).wait()
                                     for part in range(PACK_PARTS)])

    @pl.when(n > 0)
    def _():
        @pl.when(g0 == 0)
        def _():
            for ahead in range(FFN_AHEAD):
                @pl.when(ahead < total)
                def _():
                    start_in(ahead)

        wg_b[...] = wg_ref[j].astype(BF16)
        wu_b[...] = wu_ref[j].astype(BF16)
        wd_b[...] = wd_ref[j].astype(BF16)

        def tile(t, carry):
            g = g0 + t
            slot = g % 2
            slot_in = g % n_in
            at_height(e, t, lambda nr: [in_copy(e, t, slot_in, part, nr).wait()
                                        for part in range(PACK_PARTS)])

            @pl.when(g + FFN_AHEAD < total)
            def _():
                start_in(g + FFN_AHEAD)

            @pl.when(g >= 2)
            def _():
                wait_out(g - 2)

            def compute(n_rows):
                x = _unpack_rows([xbuf[slot_in, part, 0:n_rows, :]
                                  for part in range(PACK_PARTS)]).astype(BF16)
                a = _silu(_dot(x, wg_b[...])) * _dot(x, wu_b[...])
                y = _pack_rows(_dot(a.astype(BF16), wd_b[...]))
                for part in range(PACK_PARTS):
                    ybuf[slot, part, 0:n_rows, :] = y[part]
                    out_copy(e, t, slot, part, n_rows).start()

            at_height(e, t, compute)
            return carry

        lax.fori_loop(0, n, tile, 0)

        @pl.when(g0 + n == total)
        def _():
            @pl.when(total >= 2)
            def _():
                wait_out(total - 2)

            wait_out(total - 1)


def _ffn(xs_parts, counts, w_gate, w_up, w_down):
    n_exp, d = w_gate.shape[0], w_gate.shape[1]
    cap = xs_parts[0].shape[0] // n_exp
    ntiles = (counts + FFN_TILE - 1) // FFN_TILE
    cum = jnp.cumsum(ntiles)
    first = cum - ntiles
    max_tiles = (cap * TOP_K) // FFN_TILE + n_exp
    g = jnp.arange(max_tiles, dtype=I32)
    done = g[:, None] >= cum[None, :]
    tile_e = jnp.minimum(jnp.sum(done, axis=1), n_exp - 1).astype(I32)
    tile_t = (g - jnp.sum(jnp.where(done, ntiles[None, :], 0), axis=1)).astype(I32)
    wmap = lambda e, cnt, nt, first, te, tt: (e, 0, 0)
    hbm = pl.BlockSpec(memory_space=pl.ANY)
    return pl.pallas_call(
        functools.partial(_ffn_kernel, cap=cap, n_exp=n_exp),
        grid_spec=pltpu.PrefetchScalarGridSpec(
            num_scalar_prefetch=5, grid=(n_exp // FFN_GROUP,),
            in_specs=[hbm] * PACK_PARTS
                     + [pl.BlockSpec((FFN_GROUP, d, D_EXPERT), wmap),
                        pl.BlockSpec((FFN_GROUP, d, D_EXPERT), wmap),
                        pl.BlockSpec((FFN_GROUP, D_EXPERT, d), wmap)],
            out_specs=[hbm] * PACK_PARTS,
            scratch_shapes=[pltpu.VMEM((FFN_AHEAD + 1, PACK_PARTS, FFN_TILE, PACK_W), U32),
                            pltpu.VMEM((2, PACK_PARTS, FFN_TILE, PACK_W), U32),
                            pltpu.VMEM((d, D_EXPERT), BF16), pltpu.VMEM((d, D_EXPERT), BF16),
                            pltpu.VMEM((D_EXPERT, d), BF16),
                            pltpu.SemaphoreType.DMA((FFN_AHEAD + 1, PACK_PARTS)),
                            pltpu.SemaphoreType.DMA((2, PACK_PARTS))]),
        out_shape=[jax.ShapeDtypeStruct(xs_parts[0].shape, U32)] * PACK_PARTS,
        compiler_params=pltpu.CompilerParams(
            dimension_semantics=("arbitrary",), vmem_limit_bytes=VMEM_LIMIT),
        name="ffn",
    )(counts.astype(I32), ntiles.astype(I32), first.astype(I32), tile_e, tile_t, *xs_parts,
      w_gate, w_up, w_down)


def _combine_kernel(x1_ref, h2pa_ref, h2pb_ref, ra_ref, rb_ref, mod_ref,
                    wsg_ref, wsu_ref, wsd_ref, o_ref):
    h2 = _unpack_rows([h2pa_ref[0], h2pb_ref[0]]).astype(BF16)
    a = _silu(_dot(h2, wsg_ref[...])) * _dot(h2, wsu_ref[...])
    routed = jnp.concatenate([ra_ref[0], rb_ref[0]], axis=1)
    o_ref[0] = x1_ref[0] + mod_ref[0][5:6] * (_dot(a.astype(BF16), wsd_ref[...]) + routed)


def _combine(x1, h2p_parts, routed_parts, mod, per_batch_mod, p, tm):
    b, l, d = x1.shape
    mod_map = (lambda bi, i: (bi, 0, 0)) if per_batch_mod else (lambda bi, i: (0, 0, 0))
    tok = lambda w: pl.BlockSpec((1, tm, w), lambda bi, i: (bi, i, 0))
    full = lambda a: pl.BlockSpec(a.shape, lambda bi, i: (0, 0))
    weights = [p['w_sh_gate'], p['w_sh_up'], p['w_sh_down']]
    routed_parts = [r.reshape(b, l, d // PACK_PARTS) for r in routed_parts]
    return pl.pallas_call(
        _combine_kernel,
        grid=(b, l // tm),
        in_specs=[tok(d)] + [tok(PACK_W)] * PACK_PARTS + [tok(d // PACK_PARTS)] * PACK_PARTS
                 + [pl.BlockSpec((1, 6, d), mod_map)] + [full(w) for w in weights],
        out_specs=tok(d),
        out_shape=jax.ShapeDtypeStruct((b, l, d), F32),
        compiler_params=pltpu.CompilerParams(
            dimension_semantics=("parallel", "parallel"), vmem_limit_bytes=VMEM_LIMIT),
        name="combine",
    )(x1, *h2p_parts, *routed_parts, mod, *weights)


def _pad_heads(w, parts):
    k = w.shape[0]
    per = w.shape[1] // MLA_HEADS
    w = w.reshape(k, MLA_HEADS, per)[:, :, parts[0]:parts[1]]
    w = jnp.pad(w, ((0, 0), (0, 0), (0, HEAD_W - (parts[1] - parts[0]))))
    return w.reshape(k, MLA_HEADS * HEAD_W)


def _prep_params(l, g_norm1, g_norm2, w_in, w_gk_fwd, b_gk_fwd, w_gk_bwd, b_gk_bwd, g_gla_out,
                 g_q_lora, w_uq, g_kv_lora, w_ukv, g_qk_q, g_qk_k, w_out, w_router, b_router,
                 w_sh_gate, w_sh_up, w_sh_down):
    w = w_in[l]
    d = w.shape[0]
    o_lrf = COL_CQ
    o_lrb = o_lrf + GLA_GATE_RANK
    o_cq = o_lrb + GLA_GATE_RANK
    o_ckv = o_cq + Q_LORA
    o_kr = o_ckv + KV_LORA
    w_in_p = jnp.concatenate([
        w[:, 0:COL_CQ], w[:, o_cq:o_ckv], w[:, o_ckv:o_kr], jnp.zeros((d, KR_LANE0), w.dtype),
        w[:, o_kr:o_kr + ROPE_DIM], w[:, o_lrf:o_lrb], w[:, o_lrb:o_cq]], axis=1).astype(BF16)
    z = jnp.zeros((GLA_GATE_RANK, GLA_KEY_W), F32)
    w_gk_big = jnp.concatenate([
        jnp.zeros((LANES - 2 * GLA_GATE_RANK, 2 * GLA_KEY_W), F32),
        jnp.concatenate([w_gk_fwd[l], z], axis=1),
        jnp.concatenate([z, w_gk_bwd[l]], axis=1)], axis=0).astype(BF16)
    pad_gain = lambda g: jnp.pad(g, (0, HEAD_W - QK_DIM)).reshape(1, HEAD_W)
    w_ukv_h = w_ukv[l]
    return {
        'g_norm1': g_norm1[l].reshape(1, d), 'g_norm2': g_norm2[l].reshape(1, d),
        'w_in_p': w_in_p, 'w_gk_big': w_gk_big,
        'b_gk': jnp.concatenate([b_gk_fwd[l], b_gk_bwd[l]]).reshape(1, 2 * GLA_KEY_W),
        'g_gla_out': g_gla_out[l].reshape(1, GLA_DV),
        'g_q_lora': g_q_lora[l].reshape(1, Q_LORA),
        'w_uq_p': _pad_heads(w_uq[l], (0, QK_DIM)).astype(BF16),
        'g_kv_lora': g_kv_lora[l].reshape(1, KV_LORA),
        'w_ukv_p': jnp.concatenate([_pad_heads(w_ukv_h, (0, NOPE_DIM)),
                                    _pad_heads(w_ukv_h, (NOPE_DIM, NOPE_DIM + V_DIM))],
                                   axis=1).astype(BF16),
        'gq': pad_gain(g_qk_q[l]), 'gk': pad_gain(g_qk_k[l]),
        'w_out_top': w_out[l][:GLA_VAL_W].astype(BF16),
        'w_out_bot': w_out[l][GLA_VAL_W:].astype(BF16),
        'w_router_t': w_router[l].T, 'b_router_col': b_router[l].reshape(N_EXPERTS, 1),
        'w_sh_gate': w_sh_gate[l].astype(BF16), 'w_sh_up': w_sh_up[l].astype(BF16),
        'w_sh_down': w_sh_down[l].astype(BF16),
    }


def _rope_tables(length):
    pos = np.arange(length)
    r = (pos // GRID_W).astype(np.float32)
    c = (pos % GRID_W).astype(np.float32)
    half = ROPE_DIM // 4
    inv_freq = np.float32(ROPE_THETA) ** (-np.arange(half, dtype=np.float32) / np.float32(half))
    ang_r = r[:, None] * inv_freq[None, :]
    ang_c = c[:, None] * inv_freq[None, :]
    zeros = lambda w: np.zeros((length, w), np.float32)
    ones = lambda w: np.ones((length, w), np.float32)
    tail_w = HEAD_W - ROPE_LANE0 - ROPE_DIM
    cos = np.concatenate([ones(ROPE_LANE0), np.cos(ang_r), np.cos(ang_r), np.cos(ang_c),
                          np.cos(ang_c), ones(tail_w)], axis=1)
    s1 = np.concatenate([zeros(ROPE_LANE0), -np.sin(ang_r), zeros(half), -np.sin(ang_c),
                         zeros(half), zeros(tail_w)], axis=1)
    s2 = np.concatenate([zeros(ROPE_LANE0), zeros(half), np.sin(ang_r), zeros(half),
                         np.sin(ang_c), zeros(tail_w)], axis=1)
    return tuple(jnp.asarray(t, F32) for t in (cos, s1, s2))


def _mix_route_dispatch(x_tok, seq_shape, mod, per_batch_mod, p, s0f, s0b, rope_tabs, ctx_kv,
                        tq):
    bt, lt, d = x_tok.shape
    b, l = seq_shape
    n_tok = bt * lt
    (q, k, v, g, lgf, lgb, qm, km, vm, ckvn, kr) = _inproj(x_tok, mod, per_batch_mod, p,
                                                            rope_tabs, IN_TILE)
    seq = lambda a: a.reshape(b, l, a.shape[-1])
    og, sf, sb = _gla(seq(q), seq(k), seq(v), seq(g), seq(lgf), seq(lgb), s0f, s0b,
                      p['g_gla_out'], GLA_SEQS if l == GLA_BLOCK else 1)
    kv_pairs = [(seq(km), seq(vm))] + ([ctx_kv] if ctx_kv is not None else [])
    om = _attn(seq(qm), kv_pairs, tq, ATTN_SEQS if l == tq else 1)
    tokv = lambda a: a.reshape(bt, lt, a.shape[-1])
    x1, h2a, h2b, slot8, w8, cnt = _outproj(tokv(og), tokv(om), x_tok, mod, per_batch_mod, p,
                                            OUT_TILE)
    x_sorted = [_sc_scatter_rows(h.reshape(n_tok, PACK_W), slot8,
                                 N_EXPERTS * _expert_cap(n_tok)) for h in (h2a, h2b)]
    routed = dict(x1=x1, h2=(h2a, h2b), slot8=slot8, w8=w8, cnt=cnt, x_sorted=x_sorted)
    return routed, sf, sb, ckvn, kr


def _experts_collect(r, w_gate, w_up, w_down):
    n_tok = r['slot8'].shape[1]
    y_sorted = _ffn(r['x_sorted'], r['cnt'][:, 0].astype(I32), w_gate, w_up, w_down)
    slots_tk = r['slot8'].T.reshape(1, TOP_K * n_tok)
    w_rep = r['w8'].reshape(n_tok, LANES)
    return [_sc_collect_sum(y, slots_tk, w_rep) for y in y_sorted]


def kernel(x_prompt, x_sample, c, state_gla_fwd, state_gla_bwd, cache_mla_ckv, cache_mla_krope,
           c_ctx, w_ada, b_ada, g_norm1, g_norm2, w_in, w_gk_fwd, b_gk_fwd, w_gk_bwd, b_gk_bwd,
           g_gla_out, g_q_lora, w_uq, g_kv_lora, w_ukv, g_qk_q, g_qk_k, w_out,
           w_router, b_router, w_exp_gate, w_exp_up, w_exp_down, w_sh_gate, w_sh_up, w_sh_down):
    bp, lp, d = x_prompt.shape
    bs, ls, _ = x_sample.shape
    depth = w_ada.shape[0]
    xp = x_prompt.reshape(1, bp * lp, d)
    xs = x_sample
    new_f, new_b, new_ckv, new_kr = [], [], [], []
    lat_tabs = _rope_tables(ls)
    cvecs = jnp.concatenate([c_ctx[None], c, jnp.zeros((8 - 1 - bs, d), F32)], axis=0)
    for l in range(depth):
        p = _prep_params(l, g_norm1, g_norm2, w_in, w_gk_fwd, b_gk_fwd, w_gk_bwd, b_gk_bwd,
                         g_gla_out, g_q_lora, w_uq, g_kv_lora, w_ukv, g_qk_q, g_qk_k, w_out,
                         w_router, b_router, w_sh_gate, w_sh_up, w_sh_down)
        mod = _ada(cvecs, w_ada[l], b_ada[l]).reshape(8, 6, d)
        mod_p, mod_s = mod[0:1], mod[1:1 + bs]
        zeros = jnp.zeros((bp, GLA_KEY_W, GLA_DV), F32)
        r_ctx, sf, sb, ckvn, kr = _mix_route_dispatch(
            xp, (bp, lp), mod_p, False, p, zeros, zeros, (), None, lp)
        new_f.append(sf.reshape(bp, GLA_HEADS, GLA_DK, GLA_DV))
        new_b.append(sb.reshape(bp, GLA_HEADS, GLA_DK, GLA_DV))
        new_ckv.append(ckvn.reshape(bp, lp, KV_LORA))
        new_kr.append(kr.reshape(bp, lp, LANES)[:, :, KR_LANE0:KR_LANE0 + ROPE_DIM])
        kr_cache = jnp.pad(cache_mla_krope[:, l],
                           ((0, 0), (0, 0), (ROPE_LANE0, LANES - ROPE_LANE0 - ROPE_DIM)))
        ctx_kv = _cache_kv(cache_mla_ckv[:, l], kr_cache, p)
        s0f = state_gla_fwd[:, l].reshape(bs, GLA_KEY_W, GLA_DV)
        s0b = state_gla_bwd[:, l].reshape(bs, GLA_KEY_W, GLA_DV)
        r_lat, _, _, _, _ = _mix_route_dispatch(
            xs, (bs, ls), mod_s, True, p, s0f, s0b, lat_tabs, ctx_kv, ATTN_TILE)
        experts = (w_exp_gate[l], w_exp_up[l], w_exp_down[l])
        routed_ctx = _experts_collect(r_ctx, *experts)
        routed_lat = _experts_collect(r_lat, *experts)
        xp = _combine(r_ctx['x1'], r_ctx['h2'], routed_ctx, mod_p, False, p, OUT_TILE)
        xs = _combine(r_lat['x1'], r_lat['h2'], routed_lat, mod_s, True, p, OUT_TILE)
    return (xp.reshape(bp, lp, d), xs, jnp.stack(new_f, axis=1), jnp.stack(new_b, axis=1),
            jnp.stack(new_ckv, axis=1), jnp.stack(new_kr, axis=1))
```
